```python
import jax, jax.numpy as jnp
from jax import lax
import numpy as np

D_MODEL = 1024
BATCH = 16
SEQ = 256
DEPTH = 1
DEC_BATCH = 2
DEC_SEQ = 1024
PAST_LEN = 256

GRID_W = 64
N_HEADS = 8
QK_NOPE = 64
QK_ROPE = 32
V_HEAD = 64
Q_LORA = 256
KV_LORA = 128
ROPE_AXIS = QK_ROPE // 2
ROPE_BASE = 10000.0
ATTN_SCALE = (QK_NOPE + QK_ROPE) ** -0.5
Q_BLOCK = 128
D_CONV = D_MODEL
CONV_W = 3
N_GROUPS = 4
EXP_PER_GROUP = 8
N_EXPERTS = N_GROUPS * EXP_PER_GROUP
TOP_K = 2
D_EXPERT = 256
EPS = 1e-6
IN_COLS = 3 * D_CONV + Q_LORA + KV_LORA + QK_ROPE + 2 * D_MODEL

kernel_name = "hybrid_dit_conv_mla_hmoe_step"


def rmsnorm(x, g):
    xf = x.astype(jnp.float32)
    r = lax.rsqrt(jnp.mean(xf * xf, axis=-1, keepdims=True) + EPS)
    return (xf * r).astype(x.dtype) * g


def axial_rope_tables(n_tokens):
    rows = n_tokens // GRID_W
    row = jnp.repeat(jnp.arange(rows), GRID_W).astype(jnp.float32)
    col = jnp.tile(jnp.arange(GRID_W), rows).astype(jnp.float32)
    inv = 1.0 / (ROPE_BASE ** (jnp.arange(0, ROPE_AXIS, 2, dtype=jnp.float32) / ROPE_AXIS))
    ang = jnp.stack([row[:, None] * inv, col[:, None] * inv], axis=1)
    return jnp.cos(ang), jnp.sin(ang)


def apply_axial_rope(x, cos, sin):
    shp = x.shape
    half = ROPE_AXIS // 2
    xr = x.reshape(shp[:-1] + (2, 2, half))
    x1, x2 = xr[..., 0, :], xr[..., 1, :]
    bshape = (shp[1],) + (1,) * (x.ndim - 3) + (2, half)
    cs = cos.reshape(bshape).astype(x.dtype)
    sn = sin.reshape(bshape).astype(x.dtype)
    out = jnp.stack([x1 * cs - x2 * sn, x2 * cs + x1 * sn], axis=-2)
    return out.reshape(shp)


def attend(q_nope, q_rope, k_nope, k_rope, v):
    b, n = q_nope.shape[0], q_nope.shape[1]
    nb = n // Q_BLOCK
    qn = jnp.moveaxis(q_nope.reshape(b, nb, Q_BLOCK, N_HEADS, QK_NOPE), 1, 0)
    qr = jnp.moveaxis(q_rope.reshape(b, nb, Q_BLOCK, N_HEADS, QK_ROPE), 1, 0)

    def block(args):
        qn_b, qr_b = args
        s = (jnp.einsum('bqhd,bkhd->bhqk', qn_b, k_nope)
             + jnp.einsum('bqhr,bkr->bhqk', qr_b, k_rope)) * ATTN_SCALE
        p = jax.nn.softmax(s.astype(jnp.float32), axis=-1).astype(v.dtype)
        return jnp.einsum('bhqk,bkhd->bqhd', p, v)

    o = lax.map(block, (qn, qr))
    return jnp.moveaxis(o, 0, 1).reshape(b, n, N_HEADS * V_HEAD)


def mixer(h, lw, rope, ctx_ckv, ctx_krope):
    b, n, _ = h.shape
    proj = h @ lw['w_in']
    sizes = [D_CONV, D_CONV, D_CONV, Q_LORA, KV_LORA, QK_ROPE, D_MODEL, D_MODEL]
    idx = [int(s) for s in np.cumsum(sizes)[:-1]]
    bg, cg, u_in, cq_raw, ckv_raw, kr, g_conv, g_mla = jnp.split(proj, idx, axis=-1)
    u = jnp.pad(cg * u_in, ((0, 0), (1, 1), (0, 0)))
    w = lw['conv_w']
    conv = u[:, :-2] * w[0] + u[:, 1:-1] * w[1] + u[:, 2:] * w[2]
    y_conv = (bg * conv) @ lw['w_conv_out']
    q = (rmsnorm(cq_raw, lw['q_norm']) @ lw['w_uq']).reshape(b, n, N_HEADS, QK_NOPE + QK_ROPE)
    q_nope, q_rope = q[..., :QK_NOPE], q[..., QK_NOPE:]
    ckv = rmsnorm(ckv_raw, lw['kv_norm'])
    if rope is not None:
        q_rope = apply_axial_rope(q_rope, rope[0], rope[1])
        kr = apply_axial_rope(kr, rope[0], rope[1])
    if ctx_ckv is not None:
        ckv_all = jnp.concatenate([ckv, ctx_ckv.astype(ckv.dtype)], axis=1)
        kr_all = jnp.concatenate([kr, ctx_krope.astype(kr.dtype)], axis=1)
    else:
        ckv_all, kr_all = ckv, kr
    m = ckv_all.shape[1]
    kv = (ckv_all @ lw['w_ukv']).reshape(b, m, N_HEADS, QK_NOPE + V_HEAD)
    k_nope, v_h = kv[..., :QK_NOPE], kv[..., QK_NOPE:]
    y_mla = attend(q_nope, q_rope, k_nope, kr_all, v_h) @ lw['w_o']
    merged = jax.nn.sigmoid(g_conv) * y_conv + jax.nn.sigmoid(g_mla) * y_mla
    return merged @ lw['w_mix_out'], ckv, kr


def hier_moe(h, lw):
    shp = h.shape
    t = h.reshape(-1, D_MODEL)
    p_grp = jax.nn.softmax((t @ lw['w_grp']).astype(jnp.float32), axis=-1)
    p_g, g_idx = lax.top_k(p_grp, 1)
    e_logits = (t @ lw['w_exp']).astype(jnp.float32).reshape(-1, N_GROUPS, EXP_PER_GROUP)
    sel = jnp.take_along_axis(e_logits, g_idx[:, :, None], axis=1)[:, 0]
    p_e = jax.nn.softmax(sel, axis=-1)
    top_w, top_i = lax.top_k(p_e, TOP_K)
    top_w = p_g * top_w / jnp.sum(top_w, axis=-1, keepdims=True)
    glob = g_idx * EXP_PER_GROUP + top_i
    combine = jnp.sum(jax.nn.one_hot(glob, N_EXPERTS, dtype=jnp.float32) * top_w[..., None], axis=1)
    combine = combine.astype(h.dtype)
    a = jnp.einsum('td,edf->tef', t, lw['w_up'])
    g = jnp.einsum('td,edf->tef', t, lw['w_gate'])
    act = jax.nn.silu(g) * a * combine[:, :, None]
    out = jnp.einsum('tef,efd->td', act, lw['w_down'])
    return out.reshape(shp)


def block(x, cond, lw, rope, ctx_ckv, ctx_krope):
    mod = jax.nn.silu(cond) @ lw['w_ada'] + lw['b_ada']
    shift1, scale1, gate1, shift2, scale2, gate2 = jnp.split(mod, 6, axis=-1)
    h = rmsnorm(x, lw['norm1']) * (1 + scale1) + shift1
    y, ckv, kr = mixer(h, lw, rope, ctx_ckv, ctx_krope)
    x = x + gate1 * y
    h = rmsnorm(x, lw['norm2']) * (1 + scale2) + shift2
    x = x + gate2 * hier_moe(h, lw)
    return x, ckv, kr


def setup_inputs(seed: int = 0) -> dict:
    key = jax.random.key(seed)
    ks = jax.random.split(key, 32)
    f32 = jnp.float32

    def nrm(k, shape, scale):
        return jax.random.normal(k, shape, f32) * scale

    def gain(k, shape):
        return 1.0 + 0.05 * jax.random.normal(k, shape, f32)

    L = DEPTH
    return {
        'x_prompt': nrm(ks[0], (BATCH, SEQ, D_MODEL), 1.0),
        'x_sample': nrm(ks[1], (DEC_BATCH, DEC_SEQ, D_MODEL), 1.0),
        'cache_ckv': nrm(ks[2], (DEC_BATCH, DEPTH, PAST_LEN, KV_LORA), 1.0),
        'cache_krope': nrm(ks[3], (DEC_BATCH, DEPTH, PAST_LEN, QK_ROPE), 1.0),
        'c': nrm(ks[4], (DEC_BATCH, D_MODEL), 1.0),
        'c_ctx': nrm(ks[5], (D_MODEL,), 1.0),
        'norm1': gain(ks[6], (L, D_MODEL)),
        'w_ada': nrm(ks[7], (L, D_MODEL, 6 * D_MODEL), D_MODEL ** -0.5),
        'b_ada': nrm(ks[8], (L, 6 * D_MODEL), 0.02),
        'w_in': nrm(ks[9], (L, D_MODEL, IN_COLS), D_MODEL ** -0.5),
        'conv_w': nrm(ks[10], (L, CONV_W, D_CONV), CONV_W ** -0.5),
        'w_conv_out': nrm(ks[11], (L, D_CONV, D_MODEL), D_CONV ** -0.5),
        'q_norm': gain(ks[12], (L, Q_LORA)),
        'w_uq': nrm(ks[13], (L, Q_LORA, N_HEADS * (QK_NOPE + QK_ROPE)), Q_LORA ** -0.5),
        'kv_norm': gain(ks[14], (L, KV_LORA)),
        'w_ukv': nrm(ks[15], (L, KV_LORA, N_HEADS * (QK_NOPE + V_HEAD)), KV_LORA ** -0.5),
        'w_o': nrm(ks[16], (L, N_HEADS * V_HEAD, D_MODEL), (N_HEADS * V_HEAD) ** -0.5),
        'w_mix_out': nrm(ks[17], (L, D_MODEL, D_MODEL), D_MODEL ** -0.5),
        'norm2': gain(ks[18], (L, D_MODEL)),
        'w_grp': nrm(ks[19], (L, D_MODEL, N_GROUPS), D_MODEL ** -0.5),
        'w_exp': nrm(ks[20], (L, D_MODEL, N_EXPERTS), D_MODEL ** -0.5),
        'w_up': nrm(ks[21], (L, N_EXPERTS, D_MODEL, D_EXPERT), D_MODEL ** -0.5),
        'w_gate': nrm(ks[22], (L, N_EXPERTS, D_MODEL, D_EXPERT), D_MODEL ** -0.5),
        'w_down': nrm(ks[23], (L, N_EXPERTS, D_EXPERT, D_MODEL), D_EXPERT ** -0.5),
        'final_norm': gain(ks[24], (D_MODEL,)),
    }


def reference(x_prompt, x_sample, cache_ckv, cache_krope, c, c_ctx, norm1, w_ada, b_ada,
              w_in, conv_w, w_conv_out, q_norm, w_uq, kv_norm, w_ukv, w_o, w_mix_out,
              norm2, w_grp, w_exp, w_up, w_gate, w_down, final_norm):
    rope = axial_rope_tables(x_sample.shape[1])
    xp, xs = x_prompt, x_sample
    ckv_states, krope_states = [], []
    for l in range(DEPTH):
        lw = {
            'norm1': norm1[l], 'w_ada': w_ada[l], 'b_ada': b_ada[l], 'w_in': w_in[l],
            'conv_w': conv_w[l], 'w_conv_out': w_conv_out[l], 'q_norm': q_norm[l],
            'w_uq': w_uq[l], 'kv_norm': kv_norm[l], 'w_ukv': w_ukv[l], 'w_o': w_o[l],
            'w_mix_out': w_mix_out[l], 'norm2': norm2[l], 'w_grp': w_grp[l],
            'w_exp': w_exp[l], 'w_up': w_up[l], 'w_gate': w_gate[l], 'w_down': w_down[l],
        }
        xp, ckv_p, kr_p = block(xp, c_ctx, lw, None, None, None)
        ckv_states.append(ckv_p)
        krope_states.append(kr_p)
        xs, _, _ = block(xs, c[:, None, :], lw, rope, cache_ckv[:, l], cache_krope[:, l])
    y_prompt = rmsnorm(xp, final_norm)
    y_sample = rmsnorm(xs, final_norm)
    new_ckv = jnp.stack(ckv_states, axis=1)
    new_krope = jnp.stack(krope_states, axis=1)
    return (y_prompt, y_sample, new_ckv, new_krope)
```

```python
import functools

import numpy as np
import jax
import jax.numpy as jnp
from jax import lax
from jax.experimental import pallas as pl
from jax.experimental.pallas import tpu as pltpu

F32 = jnp.float32
BF16 = jnp.bfloat16

D_MODEL = 1024
BATCH = 16
SEQ = 256
DEC_BATCH = 2
DEC_SEQ = 1024
PAST_LEN = 256
GRID_W = 64
N_HEADS = 8
QK_NOPE = 64
QK_ROPE = 32
V_HEAD = 64
Q_LORA = 256
KV_LORA = 128
ROPE_AXIS = QK_ROPE // 2
ROPE_BASE = 10000.0
ATTN_SCALE = (QK_NOPE + QK_ROPE) ** -0.5
D_CONV = D_MODEL
N_GROUPS = 4
EXP_PER_GROUP = 8
N_EXPERTS = N_GROUPS * EXP_PER_GROUP
D_EXPERT = 256
EPS = 1e-6

T_P = BATCH * SEQ
T_S = DEC_BATCH * DEC_SEQ
T = T_P + T_S
N_COND = 8
LANE = 128
ROPE_LANE0 = QK_NOPE
SMALL_COLS = Q_LORA + KV_LORA + LANE
VMEM_LIMIT = 56 * 1024 * 1024

TM_IN = 1024
TM_POST = 512
TM_MOE = 1024
Q_BLK = 256
CONV_CHUNK = 512


def _dot(a, b):
    return jnp.dot(a, b, preferred_element_type=F32)


def _rms(x):
    return lax.rsqrt(jnp.mean(x * x, axis=-1, keepdims=True) + EPS)


def _mod_row(i, tm):
    n_prompt = T_P // tm
    return jnp.where(i >= n_prompt, 1 + ((i - n_prompt) * tm) // DEC_SEQ, 0)


def _ada_kernel(cond_ref, w_ref, b_ref, o_ref):
    c = cond_ref[...]
    a = (c * jax.nn.sigmoid(c)).astype(BF16)
    o_ref[...] = _dot(a, w_ref[...].astype(BF16)) + b_ref[...]


def _ada(cond, w_ada, b_ada):
    n = 6 * D_MODEL
    bn = 1536
    return pl.pallas_call(
        _ada_kernel,
        grid=(n // bn,),
        in_specs=[
            pl.BlockSpec((N_COND, D_MODEL), lambda j: (0, 0)),
            pl.BlockSpec((D_MODEL, bn), lambda j: (0, j)),
            pl.BlockSpec((1, bn), lambda j: (0, j)),
        ],
        out_specs=pl.BlockSpec((N_COND, bn), lambda j: (0, j)),
        out_shape=jax.ShapeDtypeStruct((N_COND, n), F32),
        compiler_params=pltpu.CompilerParams(
            dimension_semantics=("parallel",), vmem_limit_bytes=VMEM_LIMIT),
        name="ada_mod",
    )(cond, w_ada, b_ada)


def _inproj_kernel(x_ref, mod_ref, n1_ref, wc_ref, ws_ref, cw_ref, qn_ref, kvn_ref, wuq_ref,
                   rope_ref, zc_ref, q_ref, ckv_ref, krs_ref):
    i = pl.program_id(0)
    is_sample = i >= T_P // TM_IN
    seq = jnp.where(is_sample, DEC_SEQ, SEQ)
    mod = mod_ref[pl.ds(_mod_row(i, TM_IN), 1), :]
    shift1 = mod[:, 0:D_MODEL]
    scale1 = mod[:, D_MODEL:2 * D_MODEL]
    x = x_ref[...]
    h = ((x * _rms(x)) * n1_ref[...]) * (1.0 + scale1) + shift1
    hb = h.astype(BF16)

    pos = lax.broadcasted_iota(jnp.int32, (TM_IN, 1), 0) & (seq - 1)
    first = pos == 0
    last = pos == seq - 1
    for j in range(D_CONV // CONV_CHUNK):
        c0 = j * CONV_CHUNK
        bg = _dot(hb, wc_ref[:, c0:c0 + CONV_CHUNK])
        cg = _dot(hb, wc_ref[:, D_CONV + c0:D_CONV + c0 + CONV_CHUNK])
        ui = _dot(hb, wc_ref[:, 2 * D_CONV + c0:2 * D_CONV + c0 + CONV_CHUNK])
        u = cg * ui
        u_prev = jnp.where(first, 0.0, pltpu.roll(u, 1, 0))
        u_next = jnp.where(last, 0.0, pltpu.roll(u, TM_IN - 1, 0))
        cw = cw_ref[:, c0:c0 + CONV_CHUNK]
        conv = u_prev * cw[0:1] + u * cw[1:2] + u_next * cw[2:3]
        zc_ref[:, c0:c0 + CONV_CHUNK] = (bg * conv).astype(BF16)

    sm = _dot(hb, ws_ref[...])
    cq = sm[:, 0:Q_LORA]
    ckv_raw = sm[:, Q_LORA:Q_LORA + KV_LORA]
    krs = sm[:, Q_LORA + KV_LORA:SMALL_COLS]
    cqn = (cq * _rms(cq)) * qn_ref[...]
    q = _dot(cqn.astype(BF16), wuq_ref[...])
    ckv_ref[...] = (ckv_raw * _rms(ckv_raw)) * kvn_ref[...]

    cos = rope_ref[0]
    sin_lo = rope_ref[1]
    sin_hi = rope_ref[2]

    def rot(v):
        return v * cos + pltpu.roll(v, 8, 1) * sin_lo + pltpu.roll(v, LANE - 8, 1) * sin_hi

    krs_ref[...] = rot(krs)
    for hh in range(N_HEADS):
        q_ref[:, LANE * hh:LANE * (hh + 1)] = rot(q[:, LANE * hh:LANE * (hh + 1)]).astype(BF16)


def _inproj(x, mod, norm1, w_conv3, w_small, conv_w, q_norm, kv_norm, w_uq_slot, rope_tabs):
    n_prompt = T_P // TM_IN
    const = lambda i: (0, 0)
    return pl.pallas_call(
        _inproj_kernel,
        grid=(T // TM_IN,),
        in_specs=[
            pl.BlockSpec((TM_IN, D_MODEL), lambda i: (i, 0)),
            pl.BlockSpec((N_COND, 6 * D_MODEL), const),
            pl.BlockSpec((1, D_MODEL), const),
            pl.BlockSpec((D_MODEL, 3 * D_CONV), const),
            pl.BlockSpec((D_MODEL, SMALL_COLS), const),
            pl.BlockSpec((3, D_CONV), const),
            pl.BlockSpec((1, Q_LORA), const),
            pl.BlockSpec((1, KV_LORA), const),
            pl.BlockSpec((Q_LORA, N_HEADS * LANE), const),
            pl.BlockSpec((None, 3, TM_IN, LANE),
                         lambda i: (jnp.where(i >= n_prompt, 1, 0), 0, 0, 0)),
        ],
        out_specs=[
            pl.BlockSpec((TM_IN, D_CONV), lambda i: (i, 0)),
            pl.BlockSpec((TM_IN, N_HEADS * LANE), lambda i: (i, 0)),
            pl.BlockSpec((TM_IN, KV_LORA), lambda i: (i, 0)),
            pl.BlockSpec((TM_IN, LANE), lambda i: (i, 0)),
        ],
        out_shape=[
            jax.ShapeDtypeStruct((T, D_CONV), BF16),
            jax.ShapeDtypeStruct((T, N_HEADS * LANE), BF16),
            jax.ShapeDtypeStruct((T, KV_LORA), F32),
            jax.ShapeDtypeStruct((T, LANE), F32),
        ],
        compiler_params=pltpu.CompilerParams(
            dimension_semantics=("parallel",), vmem_limit_bytes=VMEM_LIMIT),
        name="in_proj",
    )(x, mod, norm1, w_conv3, w_small, conv_w, q_norm, kv_norm, w_uq_slot, rope_tabs)


def _attn_kernel(*refs, n_src):
    q_ref = refs[0]
    ckv_refs = refs[1:1 + n_src]
    krs_refs = refs[1 + n_src:1 + 2 * n_src]
    wukv_ref = refs[1 + 2 * n_src]
    o_ref = refs[2 + 2 * n_src]
    kf_scr, v_scr = refs[3 + 2 * n_src:]

    @pl.when(pl.program_id(1) == 0)
    def _():
        off = 0
        for c_ref, k_ref in zip(ckv_refs, krs_refs):
            m = c_ref.shape[0]
            kv = _dot(c_ref[...].astype(BF16), wukv_ref[...])
            krs = k_ref[...]
            for hh in range(N_HEADS):
                kf_scr[hh, off:off + m, :] = (kv[:, LANE * hh:LANE * (hh + 1)] + krs).astype(BF16)
            v_scr[off:off + m, :] = kv[:, N_HEADS * LANE:].astype(BF16)
            off += m

    for pair in range(N_HEADS // 2):
        acc = None
        for hh in (2 * pair, 2 * pair + 1):
            qh = q_ref[:, LANE * hh:LANE * (hh + 1)]
            s = lax.dot_general(qh, kf_scr[hh], (((1,), (1,)), ((), ())),
                                preferred_element_type=F32) * ATTN_SCALE
            e = jnp.exp(s - jnp.max(s, axis=-1, keepdims=True))
            p = (e / jnp.sum(e, axis=-1, keepdims=True)).astype(BF16)
            part = _dot(p, v_scr[:, LANE * hh:LANE * (hh + 1)])
            acc = part if acc is None else acc + part
        o_ref[:, LANE * pair:LANE * (pair + 1)] = acc.astype(BF16)


def _attention(q, ckv, krs, cache_ckv, cache_krs, w_ukv_slot):
    kv_cols = 2 * N_HEADS * LANE
    cp = pltpu.CompilerParams(dimension_semantics=("parallel", "arbitrary"),
                              vmem_limit_bytes=VMEM_LIMIT)
    o_prompt = pl.pallas_call(
        functools.partial(_attn_kernel, n_src=1),
        grid=(BATCH, SEQ // Q_BLK),
        in_specs=[
            pl.BlockSpec((Q_BLK, N_HEADS * LANE), lambda b, j: (b * (SEQ // Q_BLK) + j, 0)),
            pl.BlockSpec((SEQ, KV_LORA), lambda b, j: (b, 0)),
            pl.BlockSpec((SEQ, LANE), lambda b, j: (b, 0)),
            pl.BlockSpec((KV_LORA, kv_cols), lambda b, j: (0, 0)),
        ],
        out_specs=pl.BlockSpec((Q_BLK, N_HEADS * V_HEAD), lambda b, j: (b * (SEQ // Q_BLK) + j, 0)),
        out_shape=jax.ShapeDtypeStruct((T_P, N_HEADS * V_HEAD), BF16),
        scratch_shapes=[pltpu.VMEM((N_HEADS, SEQ, LANE), BF16),
                        pltpu.VMEM((SEQ, N_HEADS * LANE), BF16)],
        compiler_params=cp,
        name="attn_prompt",
    )(q, ckv, krs, w_ukv_slot)

    m_all = DEC_SEQ + PAST_LEN
    nq = DEC_SEQ // Q_BLK
    q0 = T_P // Q_BLK
    s0 = T_P // DEC_SEQ
    o_sample = pl.pallas_call(
        functools.partial(_attn_kernel, n_src=2),
        grid=(DEC_BATCH, nq),
        in_specs=[
            pl.BlockSpec((Q_BLK, N_HEADS * LANE), lambda b, j: (q0 + b * nq + j, 0)),
            pl.BlockSpec((DEC_SEQ, KV_LORA), lambda b, j: (s0 + b, 0)),
            pl.BlockSpec((None, PAST_LEN, KV_LORA), lambda b, j: (b, 0, 0)),
            pl.BlockSpec((DEC_SEQ, LANE), lambda b, j: (s0 + b, 0)),
            pl.BlockSpec((None, PAST_LEN, LANE), lambda b, j: (b, 0, 0)),
            pl.BlockSpec((KV_LORA, kv_cols), lambda b, j: (0, 0)),
        ],
        out_specs=pl.BlockSpec((Q_BLK, N_HEADS * V_HEAD), lambda b, j: (b * nq + j, 0)),
        out_shape=jax.ShapeDtypeStruct((T_S, N_HEADS * V_HEAD), BF16),
        scratch_shapes=[pltpu.VMEM((N_HEADS, m_all, LANE), BF16),
                        pltpu.VMEM((m_all, N_HEADS * LANE), BF16)],
        compiler_params=cp,
        name="attn_sample",
    )(q, ckv, cache_ckv, krs, cache_krs, w_ukv_slot)
    return jnp.concatenate([o_prompt, o_sample], axis=0)


def _route(logits):
    lane = lax.broadcasted_iota(jnp.int32, logits.shape, 1)
    neg = -jnp.inf
    big = jnp.int32(1 << 20)
    gmask = (lane >= N_EXPERTS) & (lane < N_EXPERTS + N_GROUPS)
    gl = jnp.where(gmask, logits, neg)
    gmax = jnp.max(gl, axis=-1, keepdims=True)
    gsum = jnp.sum(jnp.where(gmask, jnp.exp(gl - gmax), 0.0), axis=-1, keepdims=True)
    p_g = 1.0 / gsum
    g_idx = jnp.min(jnp.where(gl == gmax, lane, big), axis=-1, keepdims=True) - N_EXPERTS

    emask = (lane < N_EXPERTS) & ((lane >> 3) == g_idx)
    el = jnp.where(emask, logits, neg)
    m1 = jnp.max(el, axis=-1, keepdims=True)
    i1 = jnp.min(jnp.where(el == m1, lane, big), axis=-1, keepdims=True)
    el2 = jnp.where(lane == i1, neg, el)
    m2 = jnp.max(el2, axis=-1, keepdims=True)
    i2 = jnp.min(jnp.where(el2 == m2, lane, big), axis=-1, keepdims=True)
    z = jnp.sum(jnp.where(emask, jnp.exp(el - m1), 0.0), axis=-1, keepdims=True)
    p1 = 1.0 / z
    p2 = jnp.exp(m2 - m1) / z
    tot = p1 + p2
    w1 = p_g * p1 / tot
    w2 = p_g * p2 / tot
    return jnp.where(lane == i1, w1, 0.0) + jnp.where(lane == i2, w2, 0.0)


def _post_kernel(x_ref, mod_ref, n1_ref, wg_ref, zc_ref, o_ref, wco_ref, wo_ref, wmix_ref,
                 n2_ref, wr_ref, x1_ref, h2_ref, comb_ref):
    i = pl.program_id(0)
    mod = mod_ref[pl.ds(_mod_row(i, TM_POST), 1), :]
    shift1 = mod[:, 0:D_MODEL]
    scale1 = mod[:, D_MODEL:2 * D_MODEL]
    gate1 = mod[:, 2 * D_MODEL:3 * D_MODEL]
    shift2 = mod[:, 3 * D_MODEL:4 * D_MODEL]
    scale2 = mod[:, 4 * D_MODEL:5 * D_MODEL]
    x = x_ref[...]
    h = ((x * _rms(x)) * n1_ref[...]) * (1.0 + scale1) + shift1
    g = _dot(h.astype(BF16), wg_ref[...])
    y_conv = _dot(zc_ref[...], wco_ref[...])
    y_mla = _dot(o_ref[...], wo_ref[...])
    merged = (jax.nn.sigmoid(g[:, 0:D_MODEL]) * y_conv
              + jax.nn.sigmoid(g[:, D_MODEL:2 * D_MODEL]) * y_mla)
    y = _dot(merged.astype(BF16), wmix_ref[...])
    x1 = x + gate1 * y
    x1_ref[...] = x1
    h2 = ((x1 * _rms(x1)) * n2_ref[...]) * (1.0 + scale2) + shift2
    h2_ref[...] = h2.astype(BF16)
    logits = jnp.dot(h2, wr_ref[...], preferred_element_type=F32, precision=lax.Precision.HIGHEST)
    comb_ref[...] = _route(logits)


def _post(x, mod, norm1, w_gates, zc, o, w_conv_out, w_o, w_mix_out, norm2, w_route):
    const = lambda i: (0, 0)
    row = lambda i: (i, 0)
    return pl.pallas_call(
        _post_kernel,
        grid=(T // TM_POST,),
        in_specs=[
            pl.BlockSpec((TM_POST, D_MODEL), row),
            pl.BlockSpec((N_COND, 6 * D_MODEL), const),
            pl.BlockSpec((1, D_MODEL), const),
            pl.BlockSpec((D_MODEL, 2 * D_MODEL), const),
            pl.BlockSpec((TM_POST, D_CONV), row),
            pl.BlockSpec((TM_POST, N_HEADS * V_HEAD), row),
            pl.BlockSpec((D_CONV, D_MODEL), const),
            pl.BlockSpec((N_HEADS * V_HEAD, D_MODEL), const),
            pl.BlockSpec((D_MODEL, D_MODEL), const),
            pl.BlockSpec((1, D_MODEL), const),
            pl.BlockSpec((D_MODEL, LANE), const),
        ],
        out_specs=[
            pl.BlockSpec((TM_POST, D_MODEL), row),
            pl.BlockSpec((TM_POST, D_MODEL), row),
            pl.BlockSpec((TM_POST, LANE), row),
        ],
        out_shape=[
            jax.ShapeDtypeStruct((T, D_MODEL), F32),
            jax.ShapeDtypeStruct((T, D_MODEL), BF16),
            jax.ShapeDtypeStruct((T, LANE), F32),
        ],
        compiler_params=pltpu.CompilerParams(
            dimension_semantics=("parallel",), vmem_limit_bytes=VMEM_LIMIT),
        name="post_mixer",
    )(x, mod, norm1, w_gates, zc, o, w_conv_out, w_o, w_mix_out, norm2, w_route)


def _moe_kernel(h2_ref, comb_ref, x1_ref, mod_ref, fn_ref, wup_ref, wgate_ref, wdown_ref,
                y_ref, acc_ref):
    i = pl.program_id(0)
    e = pl.program_id(1)

    @pl.when(e == 0)
    def _():
        acc_ref[...] = jnp.zeros_like(acc_ref)

    hb = h2_ref[...]
    a = _dot(hb, wup_ref[...].astype(BF16))
    g = _dot(hb, wgate_ref[...].astype(BF16))
    comb = comb_ref[...]
    lane = lax.broadcasted_iota(jnp.int32, comb.shape, 1)
    cw = jnp.sum(jnp.where(lane == e, comb, 0.0), axis=-1, keepdims=True)
    act = (g * jax.nn.sigmoid(g)) * a * cw
    acc_ref[...] += _dot(act.astype(BF16), wdown_ref[...].astype(BF16))

    @pl.when(e == N_EXPERTS - 1)
    def _():
        mod = mod_ref[pl.ds(_mod_row(i, TM_MOE), 1), :]
        gate2 = mod[:, 5 * D_MODEL:6 * D_MODEL]
        x2 = x1_ref[...] + gate2 * acc_ref[...]
        y_ref[...] = (x2 * _rms(x2)) * fn_ref[...]


def _moe(h2, comb, x1, mod, final_norm, w_up, w_gate, w_down):
    row = lambda i, e: (i, 0)
    const = lambda i, e: (0, 0)
    return pl.pallas_call(
        _moe_kernel,
        grid=(T // TM_MOE, N_EXPERTS),
        in_specs=[
            pl.BlockSpec((TM_MOE, D_MODEL), row),
            pl.BlockSpec((TM_MOE, LANE), row),
            pl.BlockSpec((TM_MOE, D_MODEL), row),
            pl.BlockSpec((N_COND, 6 * D_MODEL), const),
            pl.BlockSpec((1, D_MODEL), const),
            pl.BlockSpec((None, D_MODEL, D_EXPERT), lambda i, e: (e, 0, 0)),
            pl.BlockSpec((None, D_MODEL, D_EXPERT), lambda i, e: (e, 0, 0)),
            pl.BlockSpec((None, D_EXPERT, D_MODEL), lambda i, e: (e, 0, 0)),
        ],
        out_specs=pl.BlockSpec((TM_MOE, D_MODEL), row),
        out_shape=jax.ShapeDtypeStruct((T, D_MODEL), F32),
        scratch_shapes=[pltpu.VMEM((TM_MOE, D_MODEL), F32)],
        compiler_params=pltpu.CompilerParams(
            dimension_semantics=("parallel", "arbitrary"), vmem_limit_bytes=VMEM_LIMIT),
        name="moe_dense",
    )(h2, comb, x1, mod, final_norm, w_up, w_gate, w_down)


def _rope_tables():
    n = np.arange(DEC_SEQ)
    pos = np.stack([n // GRID_W, n % GRID_W], axis=1).astype(np.float32)
    half = ROPE_AXIS // 2
    inv = (1.0 / (ROPE_BASE ** (np.arange(0, ROPE_AXIS, 2, dtype=np.float32) / ROPE_AXIS))).astype(np.float32)
    ang = (pos[:, :, None] * inv[None, None, :]).astype(np.float32)
    cos = np.cos(ang).astype(np.float32)
    sin = np.sin(ang).astype(np.float32)
    tabs = np.zeros((2, 3, DEC_SEQ, LANE), np.float32)
    tabs[:, 0] = 1.0
    for a in range(2):
        lo = ROPE_LANE0 + a * ROPE_AXIS
        tabs[1, 0, :, lo:lo + half] = cos[:, a]
        tabs[1, 0, :, lo + half:lo + 2 * half] = cos[:, a]
        tabs[1, 1, :, lo + half:lo + 2 * half] = sin[:, a]
        tabs[1, 2, :, lo:lo + half] = -sin[:, a]
    return jnp.asarray(tabs)


def kernel(x_prompt, x_sample, cache_ckv, cache_krope, c, c_ctx, norm1, w_ada, b_ada, w_in, conv_w,
           w_conv_out, q_norm, w_uq, kv_norm, w_ukv, w_o, w_mix_out, norm2, w_grp, w_exp, w_up,
           w_gate, w_down, final_norm):
    l = 0
    x = jnp.concatenate([x_prompt.reshape(T_P, D_MODEL), x_sample.reshape(T_S, D_MODEL)], axis=0)
    cond = jnp.concatenate(
        [c_ctx[None, :], c, jnp.zeros((N_COND - 1 - DEC_BATCH, D_MODEL), F32)], axis=0)
    mod = _ada(cond, w_ada[l], b_ada[l][None, :])

    wi = w_in[l]
    o_cq = 3 * D_CONV
    o_kr = o_cq + Q_LORA + KV_LORA
    o_g = o_kr + QK_ROPE
    w_conv3 = wi[:, :o_cq].astype(BF16)
    w_kr_slot = jnp.pad(wi[:, o_kr:o_g], ((0, 0), (ROPE_LANE0, LANE - ROPE_LANE0 - QK_ROPE)))
    w_small = jnp.concatenate([wi[:, o_cq:o_kr], w_kr_slot], axis=1).astype(BF16)
    w_gates = wi[:, o_g:].astype(BF16)
    w_uq_slot = jnp.pad(w_uq[l].reshape(Q_LORA, N_HEADS, QK_NOPE + QK_ROPE),
                        ((0, 0), (0, 0), (0, LANE - QK_NOPE - QK_ROPE))
                        ).reshape(Q_LORA, N_HEADS * LANE).astype(BF16)
    wkv = w_ukv[l].reshape(KV_LORA, N_HEADS, QK_NOPE + V_HEAD)
    wk_slot = jnp.pad(wkv[:, :, :QK_NOPE], ((0, 0), (0, 0), (0, LANE - QK_NOPE)))
    wv = wkv[:, :, QK_NOPE:].reshape(KV_LORA, N_HEADS // 2, 2, V_HEAD)
    zero = jnp.zeros_like(wv[:, :, 0])
    wv_slot = jnp.stack([jnp.concatenate([wv[:, :, 0], zero], axis=-1),
                         jnp.concatenate([zero, wv[:, :, 1]], axis=-1)], axis=2)
    w_ukv_slot = jnp.concatenate([wk_slot.reshape(KV_LORA, N_HEADS * LANE),
                                  wv_slot.reshape(KV_LORA, N_HEADS * LANE)], axis=1).astype(BF16)
    w_route = jnp.pad(jnp.concatenate([w_exp[l], w_grp[l]], axis=1),
                      ((0, 0), (0, LANE - N_EXPERTS - N_GROUPS)))
    cache_krs = jnp.pad(cache_krope[:, l], ((0, 0), (0, 0), (ROPE_LANE0, LANE - ROPE_LANE0 - QK_ROPE)))

    zc, q, ckv, krs = _inproj(x, mod, norm1[l][None, :], w_conv3, w_small, conv_w[l],
                              q_norm[l][None, :], kv_norm[l][None, :], w_uq_slot, _rope_tables())
    o = _attention(q, ckv, krs, cache_ckv[:, l], cache_krs, w_ukv_slot)
    x1, h2, comb = _post(x, mod, norm1[l][None, :], w_gates, zc, o, w_conv_out[l].astype(BF16),
                         w_o[l].astype(BF16), w_mix_out[l].astype(BF16), norm2[l][None, :], w_route)
    y = _moe(h2, comb, x1, mod, final_norm[None, :], w_up[l], w_gate[l], w_down[l])

    y_prompt = y[:T_P].reshape(BATCH, SEQ, D_MODEL)
    y_sample = y[T_P:].reshape(DEC_BATCH, DEC_SEQ, D_MODEL)
    new_ckv = ckv[:T_P].reshape(BATCH, 1, SEQ, KV_LORA)
    new_krope = krs[:T_P, ROPE_LANE0:ROPE_LANE0 + QK_ROPE].reshape(BATCH, 1, SEQ, QK_ROPE)
    return (y_prompt, y_sample, new_ckv, new_krope)
```

```python
import functools

import numpy as np
import jax
import jax.numpy as jnp
from jax import lax
from jax.experimental import pallas as pl
from jax.experimental.pallas import tpu as pltpu

F32 = jnp.float32
BF16 = jnp.bfloat16

D_MODEL = 1024
BATCH = 16
SEQ = 256
DEC_BATCH = 2
DEC_SEQ = 1024
PAST_LEN = 256
GRID_W = 64
N_HEADS = 8
QK_NOPE = 64
QK_ROPE = 32
V_HEAD = 64
Q_LORA = 256
KV_LORA = 128
ROPE_AXIS = QK_ROPE // 2
ROPE_BASE = 10000.0
ATTN_SCALE = (QK_NOPE + QK_ROPE) ** -0.5
D_CONV = D_MODEL
N_GROUPS = 4
EXP_PER_GROUP = 8
N_EXPERTS = N_GROUPS * EXP_PER_GROUP
D_EXPERT = 256
EPS = 1e-6

T_P = BATCH * SEQ
T_S = DEC_BATCH * DEC_SEQ
T = T_P + T_S
N_COND = 8
LANE = 128
ROPE_LANE0 = QK_NOPE
SMALL_COLS = Q_LORA + KV_LORA + LANE
VMEM_LIMIT = 56 * 1024 * 1024

TM_IN = 1024
TM_POST = 512
TM_MOE = 1024
MOE_SUB = 256
N_VISITS = T // TM_MOE + N_GROUPS - 1
GID_LANE = 40
RANK_LANE = 41
Q_BLK = 256
CONV_CHUNK = 512


def _dot(a, b):
    return jnp.dot(a, b, preferred_element_type=F32)


def _rms(x):
    return lax.rsqrt(jnp.mean(x * x, axis=-1, keepdims=True) + EPS)


def _mod_row(i, tm):
    n_prompt = T_P // tm
    return jnp.where(i >= n_prompt, 1 + ((i - n_prompt) * tm) // DEC_SEQ, 0)


def _ada_kernel(cond_ref, w_ref, b_ref, o_ref):
    c = cond_ref[...]
    a = (c * jax.nn.sigmoid(c)).astype(BF16)
    o_ref[...] = _dot(a, w_ref[...].astype(BF16)) + b_ref[...]


def _ada(cond, w_ada, b_ada):
    n = 6 * D_MODEL
    bn = 1536
    return pl.pallas_call(
        _ada_kernel,
        grid=(n // bn,),
        in_specs=[
            pl.BlockSpec((N_COND, D_MODEL), lambda j: (0, 0)),
            pl.BlockSpec((D_MODEL, bn), lambda j: (0, j)),
            pl.BlockSpec((1, bn), lambda j: (0, j)),
        ],
        out_specs=pl.BlockSpec((N_COND, bn), lambda j: (0, j)),
        out_shape=jax.ShapeDtypeStruct((N_COND, n), F32),
        compiler_params=pltpu.CompilerParams(
            dimension_semantics=("parallel",), vmem_limit_bytes=VMEM_LIMIT),
        name="ada_mod",
    )(cond, w_ada, b_ada)


def _inproj_kernel(x_ref, mod_ref, n1_ref, wc_ref, ws_ref, cw_ref, qn_ref, kvn_ref, wuq_ref,
                   rope_ref, zc_ref, q_ref, ckv_ref, krs_ref):
    i = pl.program_id(0)
    is_sample = i >= T_P // TM_IN
    seq = jnp.where(is_sample, DEC_SEQ, SEQ)
    mod = mod_ref[pl.ds(_mod_row(i, TM_IN), 1), :]
    shift1 = mod[:, 0:D_MODEL]
    scale1 = mod[:, D_MODEL:2 * D_MODEL]
    x = x_ref[...]
    h = ((x * _rms(x)) * n1_ref[...]) * (1.0 + scale1) + shift1
    hb = h.astype(BF16)

    pos = lax.broadcasted_iota(jnp.int32, (TM_IN, 1), 0) & (seq - 1)
    first = pos == 0
    last = pos == seq - 1
    for j in range(D_CONV // CONV_CHUNK):
        c0 = j * CONV_CHUNK
        bg = _dot(hb, wc_ref[:, c0:c0 + CONV_CHUNK])
        cg = _dot(hb, wc_ref[:, D_CONV + c0:D_CONV + c0 + CONV_CHUNK])
        ui = _dot(hb, wc_ref[:, 2 * D_CONV + c0:2 * D_CONV + c0 + CONV_CHUNK])
        u = cg * ui
        u_prev = jnp.where(first, 0.0, pltpu.roll(u, 1, 0))
        u_next = jnp.where(last, 0.0, pltpu.roll(u, TM_IN - 1, 0))
        cw = cw_ref[:, c0:c0 + CONV_CHUNK]
        conv = u_prev * cw[0:1] + u * cw[1:2] + u_next * cw[2:3]
        zc_ref[:, c0:c0 + CONV_CHUNK] = (bg * conv).astype(BF16)

    sm = _dot(hb, ws_ref[...])
    cq = sm[:, 0:Q_LORA]
    ckv_raw = sm[:, Q_LORA:Q_LORA + KV_LORA]
    krs = sm[:, Q_LORA + KV_LORA:SMALL_COLS]
    cqn = (cq * _rms(cq)) * qn_ref[...]
    q = _dot(cqn.astype(BF16), wuq_ref[...])
    ckv_ref[...] = (ckv_raw * _rms(ckv_raw)) * kvn_ref[...]

    cos = rope_ref[0]
    sin_lo = rope_ref[1]
    sin_hi = rope_ref[2]

    def rot(v):
        return v * cos + pltpu.roll(v, 8, 1) * sin_lo + pltpu.roll(v, LANE - 8, 1) * sin_hi

    krs_ref[...] = rot(krs)
    for hh in range(N_HEADS):
        q_ref[:, LANE * hh:LANE * (hh + 1)] = rot(q[:, LANE * hh:LANE * (hh + 1)]).astype(BF16)


def _inproj(x, mod, norm1, w_conv3, w_small, conv_w, q_norm, kv_norm, w_uq_slot, rope_tabs):
    n_prompt = T_P // TM_IN
    const = lambda i: (0, 0)
    return pl.pallas_call(
        _inproj_kernel,
        grid=(T // TM_IN,),
        in_specs=[
            pl.BlockSpec((TM_IN, D_MODEL), lambda i: (i, 0)),
            pl.BlockSpec((N_COND, 6 * D_MODEL), const),
            pl.BlockSpec((1, D_MODEL), const),
            pl.BlockSpec((D_MODEL, 3 * D_CONV), const),
            pl.BlockSpec((D_MODEL, SMALL_COLS), const),
            pl.BlockSpec((3, D_CONV), const),
            pl.BlockSpec((1, Q_LORA), const),
            pl.BlockSpec((1, KV_LORA), const),
            pl.BlockSpec((Q_LORA, N_HEADS * LANE), const),
            pl.BlockSpec((None, 3, TM_IN, LANE),
                         lambda i: (jnp.where(i >= n_prompt, 1, 0), 0, 0, 0)),
        ],
        out_specs=[
            pl.BlockSpec((TM_IN, D_CONV), lambda i: (i, 0)),
            pl.BlockSpec((TM_IN, N_HEADS * LANE), lambda i: (i, 0)),
            pl.BlockSpec((TM_IN, KV_LORA), lambda i: (i, 0)),
            pl.BlockSpec((TM_IN, LANE), lambda i: (i, 0)),
        ],
        out_shape=[
            jax.ShapeDtypeStruct((T, D_CONV), BF16),
            jax.ShapeDtypeStruct((T, N_HEADS * LANE), BF16),
            jax.ShapeDtypeStruct((T, KV_LORA), F32),
            jax.ShapeDtypeStruct((T, LANE), F32),
        ],
        compiler_params=pltpu.CompilerParams(
            dimension_semantics=("parallel",), vmem_limit_bytes=VMEM_LIMIT),
        name="in_proj",
    )(x, mod, norm1, w_conv3, w_small, conv_w, q_norm, kv_norm, w_uq_slot, rope_tabs)


def _attn_kernel(*refs, n_src):
    q_ref = refs[0]
    ckv_refs = refs[1:1 + n_src]
    krs_refs = refs[1 + n_src:1 + 2 * n_src]
    wukv_ref = refs[1 + 2 * n_src]
    o_ref = refs[2 + 2 * n_src]
    kf_scr, v_scr = refs[3 + 2 * n_src:]

    @pl.when(pl.program_id(1) == 0)
    def _():
        off = 0
        for c_ref, k_ref in zip(ckv_refs, krs_refs):
            m = c_ref.shape[0]
            kv = _dot(c_ref[...].astype(BF16), wukv_ref[...])
            krs = k_ref[...]
            for hh in range(N_HEADS):
                kf_scr[hh, off:off + m, :] = (kv[:, LANE * hh:LANE * (hh + 1)] + krs).astype(BF16)
            v_scr[off:off + m, :] = kv[:, N_HEADS * LANE:].astype(BF16)
            off += m

    for pair in range(N_HEADS // 2):
        acc = None
        for hh in (2 * pair, 2 * pair + 1):
            qh = q_ref[:, LANE * hh:LANE * (hh + 1)]
            s = lax.dot_general(qh, kf_scr[hh], (((1,), (1,)), ((), ())),
                                preferred_element_type=F32) * ATTN_SCALE
            e = jnp.exp(s - jnp.max(s, axis=-1, keepdims=True))
            p = (e / jnp.sum(e, axis=-1, keepdims=True)).astype(BF16)
            part = _dot(p, v_scr[:, LANE * hh:LANE * (hh + 1)])
            acc = part if acc is None else acc + part
        o_ref[:, LANE * pair:LANE * (pair + 1)] = acc.astype(BF16)


def _attention(q, ckv, krs, cache_ckv, cache_krs, w_ukv_slot):
    kv_cols = 2 * N_HEADS * LANE
    cp = pltpu.CompilerParams(dimension_semantics=("parallel", "arbitrary"),
                              vmem_limit_bytes=VMEM_LIMIT)
    o_prompt = pl.pallas_call(
        functools.partial(_attn_kernel, n_src=1),
        grid=(BATCH, SEQ // Q_BLK),
        in_specs=[
            pl.BlockSpec((Q_BLK, N_HEADS * LANE), lambda b, j: (b * (SEQ // Q_BLK) + j, 0)),
            pl.BlockSpec((SEQ, KV_LORA), lambda b, j: (b, 0)),
            pl.BlockSpec((SEQ, LANE), lambda b, j: (b, 0)),
            pl.BlockSpec((KV_LORA, kv_cols), lambda b, j: (0, 0)),
        ],
        out_specs=pl.BlockSpec((Q_BLK, N_HEADS * V_HEAD), lambda b, j: (b * (SEQ // Q_BLK) + j, 0)),
        out_shape=jax.ShapeDtypeStruct((T_P, N_HEADS * V_HEAD), BF16),
        scratch_shapes=[pltpu.VMEM((N_HEADS, SEQ, LANE), BF16),
                        pltpu.VMEM((SEQ, N_HEADS * LANE), BF16)],
        compiler_params=cp,
        name="attn_prompt",
    )(q, ckv, krs, w_ukv_slot)

    m_all = DEC_SEQ + PAST_LEN
    nq = DEC_SEQ // Q_BLK
    q0 = T_P // Q_BLK
    s0 = T_P // DEC_SEQ
    o_sample = pl.pallas_call(
        functools.partial(_attn_kernel, n_src=2),
        grid=(DEC_BATCH, nq),
        in_specs=[
            pl.BlockSpec((Q_BLK, N_HEADS * LANE), lambda b, j: (q0 + b * nq + j, 0)),
            pl.BlockSpec((DEC_SEQ, KV_LORA), lambda b, j: (s0 + b, 0)),
            pl.BlockSpec((None, PAST_LEN, KV_LORA), lambda b, j: (b, 0, 0)),
            pl.BlockSpec((DEC_SEQ, LANE), lambda b, j: (s0 + b, 0)),
            pl.BlockSpec((None, PAST_LEN, LANE), lambda b, j: (b, 0, 0)),
            pl.BlockSpec((KV_LORA, kv_cols), lambda b, j: (0, 0)),
        ],
        out_specs=pl.BlockSpec((Q_BLK, N_HEADS * V_HEAD), lambda b, j: (b * nq + j, 0)),
        out_shape=jax.ShapeDtypeStruct((T_S, N_HEADS * V_HEAD), BF16),
        scratch_shapes=[pltpu.VMEM((N_HEADS, m_all, LANE), BF16),
                        pltpu.VMEM((m_all, N_HEADS * LANE), BF16)],
        compiler_params=cp,
        name="attn_sample",
    )(q, ckv, cache_ckv, krs, cache_krs, w_ukv_slot)
    return jnp.concatenate([o_prompt, o_sample], axis=0)


def _route(logits):
    lane = lax.broadcasted_iota(jnp.int32, logits.shape, 1)
    neg = -jnp.inf
    big = jnp.int32(1 << 20)
    gmask = (lane >= N_EXPERTS) & (lane < N_EXPERTS + N_GROUPS)
    gl = jnp.where(gmask, logits, neg)
    gmax = jnp.max(gl, axis=-1, keepdims=True)
    gsum = jnp.sum(jnp.where(gmask, jnp.exp(gl - gmax), 0.0), axis=-1, keepdims=True)
    p_g = 1.0 / gsum
    g_idx = jnp.min(jnp.where(gl == gmax, lane, big), axis=-1, keepdims=True) - N_EXPERTS

    emask = (lane < N_EXPERTS) & ((lane >> 3) == g_idx)
    el = jnp.where(emask, logits, neg)
    m1 = jnp.max(el, axis=-1, keepdims=True)
    i1 = jnp.min(jnp.where(el == m1, lane, big), axis=-1, keepdims=True)
    el2 = jnp.where(lane == i1, neg, el)
    m2 = jnp.max(el2, axis=-1, keepdims=True)
    i2 = jnp.min(jnp.where(el2 == m2, lane, big), axis=-1, keepdims=True)
    z = jnp.sum(jnp.where(emask, jnp.exp(el - m1), 0.0), axis=-1, keepdims=True)
    p1 = 1.0 / z
    p2 = jnp.exp(m2 - m1) / z
    tot = p1 + p2
    w1 = p_g * p1 / tot
    w2 = p_g * p2 / tot
    return jnp.where(lane == i1, w1, 0.0) + jnp.where(lane == i2, w2, 0.0), g_idx


def _post_kernel(x_ref, mod_ref, n1_ref, wg_ref, zc_ref, o_ref, wco_ref, wo_ref, wmix_ref,
                 n2_ref, wr_ref, x1_ref, h2_ref, meta_ref, cnt_ref):
    i = pl.program_id(0)
    mod = mod_ref[pl.ds(_mod_row(i, TM_POST), 1), :]
    shift1 = mod[:, 0:D_MODEL]
    scale1 = mod[:, D_MODEL:2 * D_MODEL]
    gate1 = mod[:, 2 * D_MODEL:3 * D_MODEL]
    shift2 = mod[:, 3 * D_MODEL:4 * D_MODEL]
    scale2 = mod[:, 4 * D_MODEL:5 * D_MODEL]
    x = x_ref[...]
    h = ((x * _rms(x)) * n1_ref[...]) * (1.0 + scale1) + shift1
    g = _dot(h.astype(BF16), wg_ref[...])
    y_conv = _dot(zc_ref[...], wco_ref[...])
    y_mla = _dot(o_ref[...], wo_ref[...])
    merged = (jax.nn.sigmoid(g[:, 0:D_MODEL]) * y_conv
              + jax.nn.sigmoid(g[:, D_MODEL:2 * D_MODEL]) * y_mla)
    y = _dot(merged.astype(BF16), wmix_ref[...])
    x1 = x + gate1 * y
    x1_ref[...] = x1
    h2 = ((x1 * _rms(x1)) * n2_ref[...]) * (1.0 + scale2) + shift2
    logits = jnp.dot(h2, wr_ref[...], preferred_element_type=F32, precision=lax.Precision.HIGHEST)
    comb, g_idx = _route(logits)

    lane = lax.broadcasted_iota(jnp.int32, comb.shape, 1)
    onehot = lane == g_idx + N_EXPERTS
    r_i = lax.broadcasted_iota(jnp.int32, (TM_POST, TM_POST), 0)
    c_i = lax.broadcasted_iota(jnp.int32, (TM_POST, TM_POST), 1)
    lower = jnp.where(c_i < r_i, 1.0, 0.0).astype(BF16)
    before = _dot(lower, jnp.where(onehot, 1.0, 0.0).astype(BF16))
    rank = jnp.sum(jnp.where(onehot, before, 0.0), axis=-1, keepdims=True)
    counts = jnp.sum(jnp.where(onehot, 1.0, 0.0), axis=0, keepdims=True)
    cnt_ref[...] = jnp.broadcast_to(counts, cnt_ref.shape)

    meta = (comb + jnp.where(lane == GID_LANE, g_idx.astype(F32), 0.0)
            + jnp.where(lane == RANK_LANE, rank, 0.0))
    meta_ref[...] = meta
    h2_ref[...] = h2


def _post(x, mod, norm1, w_gates, zc, o, w_conv_out, w_o, w_mix_out, norm2, w_route):
    const = lambda i: (0, 0)
    row = lambda i: (i, 0)
    return pl.pallas_call(
        _post_kernel,
        grid=(T // TM_POST,),
        in_specs=[
            pl.BlockSpec((TM_POST, D_MODEL), row),
            pl.BlockSpec((N_COND, 6 * D_MODEL), const),
            pl.BlockSpec((1, D_MODEL), const),
            pl.BlockSpec((D_MODEL, 2 * D_MODEL), const),
            pl.BlockSpec((TM_POST, D_CONV), row),
            pl.BlockSpec((TM_POST, N_HEADS * V_HEAD), row),
            pl.BlockSpec((D_CONV, D_MODEL), const),
            pl.BlockSpec((N_HEADS * V_HEAD, D_MODEL), const),
            pl.BlockSpec((D_MODEL, D_MODEL), const),
            pl.BlockSpec((1, D_MODEL), const),
            pl.BlockSpec((D_MODEL, LANE), const),
        ],
        out_specs=[
            pl.BlockSpec((TM_POST, D_MODEL), row),
            pl.BlockSpec((TM_POST, D_MODEL), row),
            pl.BlockSpec((TM_POST, LANE), row),
            pl.BlockSpec((None, 8, LANE), lambda i: (i, 0, 0)),
        ],
        out_shape=[
            jax.ShapeDtypeStruct((T, D_MODEL), F32),
            jax.ShapeDtypeStruct((T, D_MODEL), F32),
            jax.ShapeDtypeStruct((T, LANE), F32),
            jax.ShapeDtypeStruct((T // TM_POST, 8, LANE), F32),
        ],
        compiler_params=pltpu.CompilerParams(
            dimension_semantics=("parallel",), vmem_limit_bytes=VMEM_LIMIT),
        name="post_mixer",
    )(x, mod, norm1, w_gates, zc, o, w_conv_out, w_o, w_mix_out, norm2, w_route)


def _dispatch_kernel(pos_ref, h_ref, m_ref, hs_ref, ms_ref):
    base = pl.program_id(1) * TM_MOE
    with_meta = pl.program_id(0) == 0

    def move(src_ref, dst_ref):
        def body(r, carry):
            dst_ref[pl.ds(pos_ref[base + r], 1), :] = src_ref[pl.ds(r, 1), :]
            return carry

        lax.fori_loop(0, TM_MOE, body, 0, unroll=8)

    move(h_ref, hs_ref)

    @pl.when(with_meta)
    def _():
        move(m_ref, ms_ref)


def _dispatch(pos, h2, meta):
    half = D_MODEL // 2
    return pl.pallas_call(
        _dispatch_kernel,
        grid_spec=pltpu.PrefetchScalarGridSpec(
            num_scalar_prefetch=1,
            grid=(2, T // TM_MOE),
            in_specs=[pl.BlockSpec((TM_MOE, half), lambda c, i, pos: (i, c)),
                      pl.BlockSpec((TM_MOE, LANE), lambda c, i, pos: (i, 0))],
            out_specs=[pl.BlockSpec((T, half), lambda c, i, pos: (0, c)),
                       pl.BlockSpec((T, LANE), lambda c, i, pos: (0, 0))],
        ),
        out_shape=[jax.ShapeDtypeStruct((T, D_MODEL), F32),
                   jax.ShapeDtypeStruct((T, LANE), F32)],
        compiler_params=pltpu.CompilerParams(
            dimension_semantics=("arbitrary", "arbitrary"), vmem_limit_bytes=VMEM_LIMIT),
        name="moe_dispatch",
    )(pos, h2, meta)


def _moe_kernel(vt_ref, vg_ref, vlo_ref, vhi_ref, vfirst_ref, vvalid_ref,
                hs_ref, ms_ref, wup_ref, wgate_ref, wdown_ref, y_ref, xb_ref):
    v = pl.program_id(0)
    j = pl.program_id(1)
    valid = vvalid_ref[v] == 1
    lo = vlo_ref[v]
    hi = vhi_ref[v]
    e = vg_ref[v] * EXP_PER_GROUP + j

    @pl.when(valid & (j == 0))
    def _():
        xb_ref[...] = hs_ref[...].astype(BF16)

    @pl.when(valid & (j == 0) & (vfirst_ref[v] == 1))
    def _():
        y_ref[...] = jnp.zeros_like(y_ref)

    @pl.when(valid)
    def _():
        wup = wup_ref[...].astype(BF16)
        wgate = wgate_ref[...].astype(BF16)
        wdown = wdown_ref[...].astype(BF16)
        for s in range(TM_MOE // MOE_SUB):
            r0 = s * MOE_SUB

            @pl.when((lo < r0 + MOE_SUB) & (hi > r0))
            def _():
                xb = xb_ref[r0:r0 + MOE_SUB, :]
                a = _dot(xb, wup)
                g = _dot(xb, wgate)
                comb = ms_ref[r0:r0 + MOE_SUB, :]
                lane = lax.broadcasted_iota(jnp.int32, comb.shape, 1)
                cw = jnp.sum(jnp.where(lane == e, comb, 0.0), axis=-1, keepdims=True)
                act = (g * jax.nn.sigmoid(g)) * a * cw
                y_ref[r0:r0 + MOE_SUB, :] += _dot(act.astype(BF16), wdown)


def _moe(sched, hs, ms, w_up, w_gate, w_down):
    wmap = lambda v, j, vt, vg, vlo, vhi, vfirst, vvalid: (
        vg[v] * EXP_PER_GROUP + jnp.where(vvalid[v] == 1, j, EXP_PER_GROUP - 1), 0, 0)
    tmap = lambda v, j, vt, *_: (vt[v], 0)
    return pl.pallas_call(
        _moe_kernel,
        grid_spec=pltpu.PrefetchScalarGridSpec(
            num_scalar_prefetch=6,
            grid=(N_VISITS, EXP_PER_GROUP),
            in_specs=[
                pl.BlockSpec((TM_MOE, D_MODEL), tmap),
                pl.BlockSpec((TM_MOE, LANE), tmap),
                pl.BlockSpec((None, D_MODEL, D_EXPERT), wmap),
                pl.BlockSpec((None, D_MODEL, D_EXPERT), wmap),
                pl.BlockSpec((None, D_EXPERT, D_MODEL), wmap),
            ],
            out_specs=pl.BlockSpec((TM_MOE, D_MODEL), tmap),
            scratch_shapes=[pltpu.VMEM((TM_MOE, D_MODEL), BF16)],
        ),
        out_shape=jax.ShapeDtypeStruct((T, D_MODEL), F32),
        compiler_params=pltpu.CompilerParams(
            dimension_semantics=("arbitrary", "arbitrary"), vmem_limit_bytes=VMEM_LIMIT),
        name="moe_grouped",
    )(*sched, hs, ms, w_up, w_gate, w_down)


def _final_kernel(pos_ref, ys_ref, x1_ref, mod_ref, fn_ref, y_ref, g_ref):
    i = pl.program_id(0)
    base = i * TM_MOE

    def body(r, carry):
        g_ref[pl.ds(r, 1), :] = ys_ref[pl.ds(pos_ref[base + r], 1), :]
        return carry

    lax.fori_loop(0, TM_MOE, body, 0, unroll=8)
    mod = mod_ref[pl.ds(_mod_row(i, TM_MOE), 1), :]
    gate2 = mod[:, 5 * D_MODEL:6 * D_MODEL]
    x2 = x1_ref[...] + gate2 * g_ref[...]
    y_ref[...] = (x2 * _rms(x2)) * fn_ref[...]


def _final(pos, ys, x1, mod, final_norm):
    return pl.pallas_call(
        _final_kernel,
        grid_spec=pltpu.PrefetchScalarGridSpec(
            num_scalar_prefetch=1,
            grid=(T // TM_MOE,),
            in_specs=[
                pl.BlockSpec((T, D_MODEL), lambda i, pos: (0, 0), pipeline_mode=pl.Buffered(1)),
                pl.BlockSpec((TM_MOE, D_MODEL), lambda i, pos: (i, 0)),
                pl.BlockSpec((N_COND, 6 * D_MODEL), lambda i, pos: (0, 0)),
                pl.BlockSpec((1, D_MODEL), lambda i, pos: (0, 0)),
            ],
            out_specs=pl.BlockSpec((TM_MOE, D_MODEL), lambda i, pos: (i, 0)),
            scratch_shapes=[pltpu.VMEM((TM_MOE, D_MODEL), F32)],
        ),
        out_shape=jax.ShapeDtypeStruct((T, D_MODEL), F32),
        compiler_params=pltpu.CompilerParams(
            dimension_semantics=("arbitrary",), vmem_limit_bytes=VMEM_LIMIT),
        name="moe_unsort_final",
    )(pos, ys, x1, mod, final_norm)


def _schedule(meta, cnt):
    n_tiles_post = T // TM_POST
    counts = cnt[:, 0, N_EXPERTS:N_EXPERTS + N_GROUPS].astype(jnp.int32)
    gtot = jnp.sum(counts, axis=0)
    goff = jnp.cumsum(gtot) - gtot
    tile_base = goff[None, :] + jnp.cumsum(counts, axis=0) - counts
    gid = meta[:, GID_LANE].astype(jnp.int32)
    rank = meta[:, RANK_LANE].astype(jnp.int32)
    tile = jnp.arange(T, dtype=jnp.int32) // TM_POST
    pos = tile_base.reshape(-1)[tile * N_GROUPS + gid] + rank

    n_tiles = T // TM_MOE
    t_lo = (jnp.arange(n_tiles, dtype=jnp.int32) * TM_MOE)[:, None]
    lo = jnp.clip(goff[None, :] - t_lo, 0, TM_MOE)
    hi = jnp.clip(goff[None, :] + gtot[None, :] - t_lo, 0, TM_MOE)
    ok = (hi > lo).reshape(-1)
    slot = jnp.cumsum(ok.astype(jnp.int32)) - 1
    n_ok = slot[-1] + 1
    sel = (slot[None, :] == jnp.arange(N_VISITS, dtype=jnp.int32)[:, None]) & ok[None, :]

    def pick(vals):
        return jnp.sum(jnp.where(sel, vals.reshape(-1)[None, :], 0), axis=1).astype(jnp.int32)

    pair_tile = jnp.broadcast_to(jnp.arange(n_tiles, dtype=jnp.int32)[:, None], (n_tiles, N_GROUPS))
    pair_group = jnp.broadcast_to(jnp.arange(N_GROUPS, dtype=jnp.int32)[None, :], (n_tiles, N_GROUPS))
    vt, vg, vlo, vhi = pick(pair_tile), pick(pair_group), pick(lo), pick(hi)
    valid = jnp.arange(N_VISITS, dtype=jnp.int32) < n_ok
    last = jnp.maximum(n_ok - 1, 0)
    vt = jnp.where(valid, vt, vt[last])
    vg = jnp.where(valid, vg, vg[last])
    first = jnp.concatenate([jnp.ones((1,), jnp.int32), (vt[1:] != vt[:-1]).astype(jnp.int32)])
    return pos, (vt, vg, vlo, vhi, first, valid.astype(jnp.int32))


def _rope_tables():
    n = np.arange(DEC_SEQ)
    pos = np.stack([n // GRID_W, n % GRID_W], axis=1).astype(np.float32)
    half = ROPE_AXIS // 2
    inv = (1.0 / (ROPE_BASE ** (np.arange(0, ROPE_AXIS, 2, dtype=np.float32) / ROPE_AXIS))).astype(np.float32)
    ang = (pos[:, :, None] * inv[None, None, :]).astype(np.float32)
    cos = np.cos(ang).astype(np.float32)
    sin = np.sin(ang).astype(np.float32)
    tabs = np.zeros((2, 3, DEC_SEQ, LANE), np.float32)
    tabs[:, 0] = 1.0
    for a in range(2):
        lo = ROPE_LANE0 + a * ROPE_AXIS
        tabs[1, 0, :, lo:lo + half] = cos[:, a]
        tabs[1, 0, :, lo + half:lo + 2 * half] = cos[:, a]
        tabs[1, 1, :, lo + half:lo + 2 * half] = sin[:, a]
        tabs[1, 2, :, lo:lo + half] = -sin[:, a]
    return jnp.asarray(tabs)


def kernel(x_prompt, x_sample, cache_ckv, cache_krope, c, c_ctx, norm1, w_ada, b_ada, w_in, conv_w,
           w_conv_out, q_norm, w_uq, kv_norm, w_ukv, w_o, w_mix_out, norm2, w_grp, w_exp, w_up,
           w_gate, w_down, final_norm):
    l = 0
    x = jnp.concatenate([x_prompt.reshape(T_P, D_MODEL), x_sample.reshape(T_S, D_MODEL)], axis=0)
    cond = jnp.concatenate(
        [c_ctx[None, :], c, jnp.zeros((N_COND - 1 - DEC_BATCH, D_MODEL), F32)], axis=0)
    mod = _ada(cond, w_ada[l], b_ada[l][None, :])

    wi = w_in[l]
    o_cq = 3 * D_CONV
    o_kr = o_cq + Q_LORA + KV_LORA
    o_g = o_kr + QK_ROPE
    w_conv3 = wi[:, :o_cq].astype(BF16)
    w_kr_slot = jnp.pad(wi[:, o_kr:o_g], ((0, 0), (ROPE_LANE0, LANE - ROPE_LANE0 - QK_ROPE)))
    w_small = jnp.concatenate([wi[:, o_cq:o_kr], w_kr_slot], axis=1).astype(BF16)
    w_gates = wi[:, o_g:].astype(BF16)
    w_uq_slot = jnp.pad(w_uq[l].reshape(Q_LORA, N_HEADS, QK_NOPE + QK_ROPE),
                        ((0, 0), (0, 0), (0, LANE - QK_NOPE - QK_ROPE))
                        ).reshape(Q_LORA, N_HEADS * LANE).astype(BF16)
    wkv = w_ukv[l].reshape(KV_LORA, N_HEADS, QK_NOPE + V_HEAD)
    wk_slot = jnp.pad(wkv[:, :, :QK_NOPE], ((0, 0), (0, 0), (0, LANE - QK_NOPE)))
    wv = wkv[:, :, QK_NOPE:].reshape(KV_LORA, N_HEADS // 2, 2, V_HEAD)
    zero = jnp.zeros_like(wv[:, :, 0])
    wv_slot = jnp.stack([jnp.concatenate([wv[:, :, 0], zero], axis=-1),
                         jnp.concatenate([zero, wv[:, :, 1]], axis=-1)], axis=2)
    w_ukv_slot = jnp.concatenate([wk_slot.reshape(KV_LORA, N_HEADS * LANE),
                                  wv_slot.reshape(KV_LORA, N_HEADS * LANE)], axis=1).astype(BF16)
    w_route = jnp.pad(jnp.concatenate([w_exp[l], w_grp[l]], axis=1),
                      ((0, 0), (0, LANE - N_EXPERTS - N_GROUPS)))
    cache_krs = jnp.pad(cache_krope[:, l], ((0, 0), (0, 0), (ROPE_LANE0, LANE - ROPE_LANE0 - QK_ROPE)))

    zc, q, ckv, krs = _inproj(x, mod, norm1[l][None, :], w_conv3, w_small, conv_w[l],
                              q_norm[l][None, :], kv_norm[l][None, :], w_uq_slot, _rope_tables())
    o = _attention(q, ckv, krs, cache_ckv[:, l], cache_krs, w_ukv_slot)
    x1, h2, meta, cnt = _post(x, mod, norm1[l][None, :], w_gates, zc, o, w_conv_out[l].astype(BF16),
                              w_o[l].astype(BF16), w_mix_out[l].astype(BF16), norm2[l][None, :],
                              w_route)
    pos, sched = _schedule(meta, cnt)
    hs, ms = _dispatch(pos, h2, meta)
    ys = _moe(sched, hs, ms, w_up[l], w_gate[l], w_down[l])
    y = _final(pos, ys, x1, mod, final_norm[None, :])

    y_prompt = y[:T_P].reshape(BATCH, SEQ, D_MODEL)
    y_sample = y[T_P:].reshape(DEC_BATCH, DEC_SEQ, D_MODEL)
    new_ckv = ckv[:T_P].reshape(BATCH, 1, SEQ, KV_LORA)
    new_krope = krs[:T_P, ROPE_LANE0:ROPE_LANE0 + QK_ROPE].reshape(BATCH, 1, SEQ, QK_ROPE)
    return (y_prompt, y_sample, new_ckv, new_krope)
```

```python
import functools

import numpy as np
import jax
import jax.numpy as jnp
from jax import lax
from jax.experimental import pallas as pl
from jax.experimental.pallas import tpu as pltpu

F32 = jnp.float32
BF16 = jnp.bfloat16

D_MODEL = 1024
BATCH = 16
SEQ = 256
DEC_BATCH = 2
DEC_SEQ = 1024
PAST_LEN = 256
GRID_W = 64
N_HEADS = 8
QK_NOPE = 64
QK_ROPE = 32
V_HEAD = 64
Q_LORA = 256
KV_LORA = 128
ROPE_AXIS = QK_ROPE // 2
ROPE_BASE = 10000.0
ATTN_SCALE = (QK_NOPE + QK_ROPE) ** -0.5
D_CONV = D_MODEL
N_GROUPS = 4
EXP_PER_GROUP = 8
N_EXPERTS = N_GROUPS * EXP_PER_GROUP
D_EXPERT = 256
EPS = 1e-6

T_P = BATCH * SEQ
T_S = DEC_BATCH * DEC_SEQ
T = T_P + T_S
N_COND = 8
LANE = 128
ROPE_LANE0 = QK_NOPE
SMALL_COLS = Q_LORA + KV_LORA + LANE
VMEM_LIMIT = 56 * 1024 * 1024

TM_IN = 1024
TM_POST = 512
TM_MOE = 1024
N_SLAB = D_MODEL // LANE
MOE_SUB = 256
N_VISITS = T // TM_MOE + N_GROUPS - 1
GID_LANE = 40
RANK_LANE = 41
Q_BLK = 256
CONV_CHUNK = 512


def _dot(a, b):
    return jnp.dot(a, b, preferred_element_type=F32)


def _rms(x):
    return lax.rsqrt(jnp.mean(x * x, axis=-1, keepdims=True) + EPS)


def _mod_row(i, tm):
    n_prompt = T_P // tm
    return jnp.where(i >= n_prompt, 1 + ((i - n_prompt) * tm) // DEC_SEQ, 0)


def _ada_kernel(cond_ref, w_ref, b_ref, o_ref):
    c = cond_ref[...]
    a = (c * jax.nn.sigmoid(c)).astype(BF16)
    o_ref[...] = _dot(a, w_ref[...].astype(BF16)) + b_ref[...]


def _ada(cond, w_ada, b_ada):
    n = 6 * D_MODEL
    bn = 1536
    return pl.pallas_call(
        _ada_kernel,
        grid=(n // bn,),
        in_specs=[
            pl.BlockSpec((N_COND, D_MODEL), lambda j: (0, 0)),
            pl.BlockSpec((D_MODEL, bn), lambda j: (0, j)),
            pl.BlockSpec((1, bn), lambda j: (0, j)),
        ],
        out_specs=pl.BlockSpec((N_COND, bn), lambda j: (0, j)),
        out_shape=jax.ShapeDtypeStruct((N_COND, n), F32),
        compiler_params=pltpu.CompilerParams(
            dimension_semantics=("parallel",), vmem_limit_bytes=VMEM_LIMIT),
        name="ada_mod",
    )(cond, w_ada, b_ada)


def _inproj_kernel(x_ref, mod_ref, n1_ref, wc_ref, ws_ref, cw_ref, qn_ref, kvn_ref, wuq_ref,
                   rope_ref, zc_ref, q_ref, ckv_ref, krs_ref):
    i = pl.program_id(0)
    is_sample = i >= T_P // TM_IN
    seq = jnp.where(is_sample, DEC_SEQ, SEQ)
    mod = mod_ref[pl.ds(_mod_row(i, TM_IN), 1), :]
    shift1 = mod[:, 0:D_MODEL]
    scale1 = mod[:, D_MODEL:2 * D_MODEL]
    x = x_ref[...]
    h = ((x * _rms(x)) * n1_ref[...]) * (1.0 + scale1) + shift1
    hb = h.astype(BF16)

    pos = lax.broadcasted_iota(jnp.int32, (TM_IN, 1), 0) & (seq - 1)
    first = pos == 0
    last = pos == seq - 1
    for j in range(D_CONV // CONV_CHUNK):
        c0 = j * CONV_CHUNK
        bg = _dot(hb, wc_ref[:, c0:c0 + CONV_CHUNK])
        cg = _dot(hb, wc_ref[:, D_CONV + c0:D_CONV + c0 + CONV_CHUNK])
        ui = _dot(hb, wc_ref[:, 2 * D_CONV + c0:2 * D_CONV + c0 + CONV_CHUNK])
        u = cg * ui
        u_prev = jnp.where(first, 0.0, pltpu.roll(u, 1, 0))
        u_next = jnp.where(last, 0.0, pltpu.roll(u, TM_IN - 1, 0))
        cw = cw_ref[:, c0:c0 + CONV_CHUNK]
        conv = u_prev * cw[0:1] + u * cw[1:2] + u_next * cw[2:3]
        zc_ref[:, c0:c0 + CONV_CHUNK] = (bg * conv).astype(BF16)

    sm = _dot(hb, ws_ref[...])
    cq = sm[:, 0:Q_LORA]
    ckv_raw = sm[:, Q_LORA:Q_LORA + KV_LORA]
    krs = sm[:, Q_LORA + KV_LORA:SMALL_COLS]
    cqn = (cq * _rms(cq)) * qn_ref[...]
    q = _dot(cqn.astype(BF16), wuq_ref[...])
    ckv_ref[...] = (ckv_raw * _rms(ckv_raw)) * kvn_ref[...]

    cos = rope_ref[0]
    sin_lo = rope_ref[1]
    sin_hi = rope_ref[2]

    def rot(v):
        return v * cos + pltpu.roll(v, 8, 1) * sin_lo + pltpu.roll(v, LANE - 8, 1) * sin_hi

    krs_ref[...] = rot(krs)
    for hh in range(N_HEADS):
        q_ref[:, LANE * hh:LANE * (hh + 1)] = rot(q[:, LANE * hh:LANE * (hh + 1)]).astype(BF16)


def _inproj(x, mod, norm1, w_conv3, w_small, conv_w, q_norm, kv_norm, w_uq_slot, rope_tabs):
    n_prompt = T_P // TM_IN
    const = lambda i: (0, 0)
    return pl.pallas_call(
        _inproj_kernel,
        grid=(T // TM_IN,),
        in_specs=[
            pl.BlockSpec((TM_IN, D_MODEL), lambda i: (i, 0)),
            pl.BlockSpec((N_COND, 6 * D_MODEL), const),
            pl.BlockSpec((1, D_MODEL), const),
            pl.BlockSpec((D_MODEL, 3 * D_CONV), const),
            pl.BlockSpec((D_MODEL, SMALL_COLS), const),
            pl.BlockSpec((3, D_CONV), const),
            pl.BlockSpec((1, Q_LORA), const),
            pl.BlockSpec((1, KV_LORA), const),
            pl.BlockSpec((Q_LORA, N_HEADS * LANE), const),
            pl.BlockSpec((None, 3, TM_IN, LANE),
                         lambda i: (jnp.where(i >= n_prompt, 1, 0), 0, 0, 0)),
        ],
        out_specs=[
            pl.BlockSpec((TM_IN, D_CONV), lambda i: (i, 0)),
            pl.BlockSpec((TM_IN, N_HEADS * LANE), lambda i: (i, 0)),
            pl.BlockSpec((TM_IN, KV_LORA), lambda i: (i, 0)),
            pl.BlockSpec((TM_IN, LANE), lambda i: (i, 0)),
        ],
        out_shape=[
            jax.ShapeDtypeStruct((T, D_CONV), BF16),
            jax.ShapeDtypeStruct((T, N_HEADS * LANE), BF16),
            jax.ShapeDtypeStruct((T, KV_LORA), F32),
            jax.ShapeDtypeStruct((T, LANE), F32),
        ],
        compiler_params=pltpu.CompilerParams(
            dimension_semantics=("parallel",), vmem_limit_bytes=VMEM_LIMIT),
        name="in_proj",
    )(x, mod, norm1, w_conv3, w_small, conv_w, q_norm, kv_norm, w_uq_slot, rope_tabs)


def _attn_kernel(*refs, n_src):
    q_ref = refs[0]
    ckv_refs = refs[1:1 + n_src]
    krs_refs = refs[1 + n_src:1 + 2 * n_src]
    wukv_ref = refs[1 + 2 * n_src]
    o_ref = refs[2 + 2 * n_src]
    kf_scr, v_scr = refs[3 + 2 * n_src:]

    @pl.when(pl.program_id(1) == 0)
    def _():
        off = 0
        for c_ref, k_ref in zip(ckv_refs, krs_refs):
            m = c_ref.shape[0]
            kv = _dot(c_ref[...].astype(BF16), wukv_ref[...])
            krs = k_ref[...]
            for hh in range(N_HEADS):
                kf_scr[hh, off:off + m, :] = (kv[:, LANE * hh:LANE * (hh + 1)] + krs).astype(BF16)
            v_scr[off:off + m, :] = kv[:, N_HEADS * LANE:].astype(BF16)
            off += m

    for pair in range(N_HEADS // 2):
        acc = None
        for hh in (2 * pair, 2 * pair + 1):
            qh = q_ref[:, LANE * hh:LANE * (hh + 1)]
            s = lax.dot_general(qh, kf_scr[hh], (((1,), (1,)), ((), ())),
                                preferred_element_type=F32) * ATTN_SCALE
            e = jnp.exp(s - jnp.max(s, axis=-1, keepdims=True))
            p = (e / jnp.sum(e, axis=-1, keepdims=True)).astype(BF16)
            part = _dot(p, v_scr[:, LANE * hh:LANE * (hh + 1)])
            acc = part if acc is None else acc + part
        o_ref[:, LANE * pair:LANE * (pair + 1)] = acc.astype(BF16)


def _attention(q, ckv, krs, cache_ckv, cache_krs, w_ukv_slot):
    kv_cols = 2 * N_HEADS * LANE
    cp = pltpu.CompilerParams(dimension_semantics=("parallel", "arbitrary"),
                              vmem_limit_bytes=VMEM_LIMIT)
    o_prompt = pl.pallas_call(
        functools.partial(_attn_kernel, n_src=1),
        grid=(BATCH, SEQ // Q_BLK),
        in_specs=[
            pl.BlockSpec((Q_BLK, N_HEADS * LANE), lambda b, j: (b * (SEQ // Q_BLK) + j, 0)),
            pl.BlockSpec((SEQ, KV_LORA), lambda b, j: (b, 0)),
            pl.BlockSpec((SEQ, LANE), lambda b, j: (b, 0)),
            pl.BlockSpec((KV_LORA, kv_cols), lambda b, j: (0, 0)),
        ],
        out_specs=pl.BlockSpec((Q_BLK, N_HEADS * V_HEAD), lambda b, j: (b * (SEQ // Q_BLK) + j, 0)),
        out_shape=jax.ShapeDtypeStruct((T_P, N_HEADS * V_HEAD), BF16),
        scratch_shapes=[pltpu.VMEM((N_HEADS, SEQ, LANE), BF16),
                        pltpu.VMEM((SEQ, N_HEADS * LANE), BF16)],
        compiler_params=cp,
        name="attn_prompt",
    )(q, ckv, krs, w_ukv_slot)

    m_all = DEC_SEQ + PAST_LEN
    nq = DEC_SEQ // Q_BLK
    q0 = T_P // Q_BLK
    s0 = T_P // DEC_SEQ
    o_sample = pl.pallas_call(
        functools.partial(_attn_kernel, n_src=2),
        grid=(DEC_BATCH, nq),
        in_specs=[
            pl.BlockSpec((Q_BLK, N_HEADS * LANE), lambda b, j: (q0 + b * nq + j, 0)),
            pl.BlockSpec((DEC_SEQ, KV_LORA), lambda b, j: (s0 + b, 0)),
            pl.BlockSpec((None, PAST_LEN, KV_LORA), lambda b, j: (b, 0, 0)),
            pl.BlockSpec((DEC_SEQ, LANE), lambda b, j: (s0 + b, 0)),
            pl.BlockSpec((None, PAST_LEN, LANE), lambda b, j: (b, 0, 0)),
            pl.BlockSpec((KV_LORA, kv_cols), lambda b, j: (0, 0)),
        ],
        out_specs=pl.BlockSpec((Q_BLK, N_HEADS * V_HEAD), lambda b, j: (b * nq + j, 0)),
        out_shape=jax.ShapeDtypeStruct((T_S, N_HEADS * V_HEAD), BF16),
        scratch_shapes=[pltpu.VMEM((N_HEADS, m_all, LANE), BF16),
                        pltpu.VMEM((m_all, N_HEADS * LANE), BF16)],
        compiler_params=cp,
        name="attn_sample",
    )(q, ckv, cache_ckv, krs, cache_krs, w_ukv_slot)
    return jnp.concatenate([o_prompt, o_sample], axis=0)


def _route(logits):
    lane = lax.broadcasted_iota(jnp.int32, logits.shape, 1)
    neg = -jnp.inf
    big = jnp.int32(1 << 20)
    gmask = (lane >= N_EXPERTS) & (lane < N_EXPERTS + N_GROUPS)
    gl = jnp.where(gmask, logits, neg)
    gmax = jnp.max(gl, axis=-1, keepdims=True)
    gsum = jnp.sum(jnp.where(gmask, jnp.exp(gl - gmax), 0.0), axis=-1, keepdims=True)
    p_g = 1.0 / gsum
    g_idx = jnp.min(jnp.where(gl == gmax, lane, big), axis=-1, keepdims=True) - N_EXPERTS

    emask = (lane < N_EXPERTS) & ((lane >> 3) == g_idx)
    el = jnp.where(emask, logits, neg)
    m1 = jnp.max(el, axis=-1, keepdims=True)
    i1 = jnp.min(jnp.where(el == m1, lane, big), axis=-1, keepdims=True)
    el2 = jnp.where(lane == i1, neg, el)
    m2 = jnp.max(el2, axis=-1, keepdims=True)
    i2 = jnp.min(jnp.where(el2 == m2, lane, big), axis=-1, keepdims=True)
    z = jnp.sum(jnp.where(emask, jnp.exp(el - m1), 0.0), axis=-1, keepdims=True)
    p1 = 1.0 / z
    p2 = jnp.exp(m2 - m1) / z
    tot = p1 + p2
    w1 = p_g * p1 / tot
    w2 = p_g * p2 / tot
    return jnp.where(lane == i1, w1, 0.0) + jnp.where(lane == i2, w2, 0.0), g_idx


def _post_kernel(x_ref, mod_ref, n1_ref, wg_ref, zc_ref, o_ref, wco_ref, wo_ref, wmix_ref,
                 n2_ref, wr_ref, x1_ref, h3_ref, meta_ref, cnt_ref):
    i = pl.program_id(0)
    mod = mod_ref[pl.ds(_mod_row(i, TM_POST), 1), :]
    shift1 = mod[:, 0:D_MODEL]
    scale1 = mod[:, D_MODEL:2 * D_MODEL]
    gate1 = mod[:, 2 * D_MODEL:3 * D_MODEL]
    shift2 = mod[:, 3 * D_MODEL:4 * D_MODEL]
    scale2 = mod[:, 4 * D_MODEL:5 * D_MODEL]
    x = x_ref[...]
    h = ((x * _rms(x)) * n1_ref[...]) * (1.0 + scale1) + shift1
    g = _dot(h.astype(BF16), wg_ref[...])
    y_conv = _dot(zc_ref[...], wco_ref[...])
    y_mla = _dot(o_ref[...], wo_ref[...])
    merged = (jax.nn.sigmoid(g[:, 0:D_MODEL]) * y_conv
              + jax.nn.sigmoid(g[:, D_MODEL:2 * D_MODEL]) * y_mla)
    y = _dot(merged.astype(BF16), wmix_ref[...])
    x1 = x + gate1 * y
    x1_ref[...] = x1
    h2 = ((x1 * _rms(x1)) * n2_ref[...]) * (1.0 + scale2) + shift2
    logits = jnp.dot(h2, wr_ref[...], preferred_element_type=F32, precision=lax.Precision.HIGHEST)
    comb, g_idx = _route(logits)

    lane = lax.broadcasted_iota(jnp.int32, comb.shape, 1)
    onehot = lane == g_idx + N_EXPERTS
    r_i = lax.broadcasted_iota(jnp.int32, (TM_POST, TM_POST), 0)
    c_i = lax.broadcasted_iota(jnp.int32, (TM_POST, TM_POST), 1)
    lower = jnp.where(c_i < r_i, 1.0, 0.0).astype(BF16)
    before = _dot(lower, jnp.where(onehot, 1.0, 0.0).astype(BF16))
    rank = jnp.sum(jnp.where(onehot, before, 0.0), axis=-1, keepdims=True)
    counts = jnp.sum(jnp.where(onehot, 1.0, 0.0), axis=0, keepdims=True)
    cnt_ref[...] = jnp.broadcast_to(counts, cnt_ref.shape)

    meta = (comb + jnp.where(lane == GID_LANE, g_idx.astype(F32), 0.0)
            + jnp.where(lane == RANK_LANE, rank, 0.0))
    meta_ref[...] = meta
    for c in range(N_SLAB):
        h3_ref[pl.ds(c, TM_POST, stride=N_SLAB), :] = h2[:, LANE * c:LANE * (c + 1)]


def _post(x, mod, norm1, w_gates, zc, o, w_conv_out, w_o, w_mix_out, norm2, w_route):
    const = lambda i: (0, 0)
    row = lambda i: (i, 0)
    return pl.pallas_call(
        _post_kernel,
        grid=(T // TM_POST,),
        in_specs=[
            pl.BlockSpec((TM_POST, D_MODEL), row),
            pl.BlockSpec((N_COND, 6 * D_MODEL), const),
            pl.BlockSpec((1, D_MODEL), const),
            pl.BlockSpec((D_MODEL, 2 * D_MODEL), const),
            pl.BlockSpec((TM_POST, D_CONV), row),
            pl.BlockSpec((TM_POST, N_HEADS * V_HEAD), row),
            pl.BlockSpec((D_CONV, D_MODEL), const),
            pl.BlockSpec((N_HEADS * V_HEAD, D_MODEL), const),
            pl.BlockSpec((D_MODEL, D_MODEL), const),
            pl.BlockSpec((1, D_MODEL), const),
            pl.BlockSpec((D_MODEL, LANE), const),
        ],
        out_specs=[
            pl.BlockSpec((TM_POST, D_MODEL), row),
            pl.BlockSpec((TM_POST * N_SLAB, LANE), row),
            pl.BlockSpec((TM_POST, LANE), row),
            pl.BlockSpec((None, 8, LANE), lambda i: (i, 0, 0)),
        ],
        out_shape=[
            jax.ShapeDtypeStruct((T, D_MODEL), F32),
            jax.ShapeDtypeStruct((T * N_SLAB, LANE), F32),
            jax.ShapeDtypeStruct((T, LANE), F32),
            jax.ShapeDtypeStruct((T // TM_POST, 8, LANE), F32),
        ],
        compiler_params=pltpu.CompilerParams(
            dimension_semantics=("parallel",), vmem_limit_bytes=VMEM_LIMIT),
        name="post_mixer",
    )(x, mod, norm1, w_gates, zc, o, w_conv_out, w_o, w_mix_out, norm2, w_route)


def _dispatch_kernel(pos_ref, h3_ref, m_ref, hs_ref, ms_ref, src_ref, xg_ref):
    i = pl.program_id(0)

    @pl.when(i == 0)
    def _():
        def invert(t, carry):
            src_ref[pos_ref[t]] = t
            return carry

        lax.fori_loop(0, T, invert, 0, unroll=8)

    base = i * TM_MOE

    def body(r, carry):
        tok = src_ref[base + r]
        xg_ref[pl.ds(pl.multiple_of(r * N_SLAB, N_SLAB), N_SLAB), :] = h3_ref[tok]
        ms_ref[pl.ds(r, 1), :] = m_ref[pl.ds(tok, 1), :]
        return carry

    lax.fori_loop(0, TM_MOE, body, 0, unroll=8)
    for c in range(N_SLAB):
        hs_ref[:, LANE * c:LANE * (c + 1)] = xg_ref[pl.ds(c, TM_MOE, stride=N_SLAB), :].astype(BF16)


def _dispatch(pos, h3, meta):
    n_slab = N_SLAB
    return pl.pallas_call(
        _dispatch_kernel,
        grid_spec=pltpu.PrefetchScalarGridSpec(
            num_scalar_prefetch=1,
            grid=(T // TM_MOE,),
            in_specs=[pl.BlockSpec((T, n_slab, LANE), lambda i, pos: (0, 0, 0),
                                   pipeline_mode=pl.Buffered(1)),
                      pl.BlockSpec((T, LANE), lambda i, pos: (0, 0), pipeline_mode=pl.Buffered(1))],
            out_specs=[pl.BlockSpec((TM_MOE, D_MODEL), lambda i, pos: (i, 0)),
                       pl.BlockSpec((TM_MOE, LANE), lambda i, pos: (i, 0))],
            scratch_shapes=[pltpu.SMEM((T,), jnp.int32),
                            pltpu.VMEM((TM_MOE * n_slab, LANE), F32)],
        ),
        out_shape=[jax.ShapeDtypeStruct((T, D_MODEL), BF16),
                   jax.ShapeDtypeStruct((T, LANE), F32)],
        compiler_params=pltpu.CompilerParams(
            dimension_semantics=("arbitrary",), vmem_limit_bytes=VMEM_LIMIT),
        name="moe_dispatch",
    )(pos, h3, meta)


def _moe_kernel(vt_ref, vg_ref, vlo_ref, vhi_ref, vfirst_ref, vlast_ref, vvalid_ref,
                hs_ref, ms_ref, wup_ref, wgate_ref, wdown_ref, y3_ref, acc_ref):
    v = pl.program_id(0)
    j = pl.program_id(1)
    valid = vvalid_ref[v] == 1
    lo = vlo_ref[v]
    hi = vhi_ref[v]
    e = vg_ref[v] * EXP_PER_GROUP + j
    full = (lo == 0) & (hi == TM_MOE)

    @pl.when(valid & (j == 0) & (vfirst_ref[v] == 1))
    def _():
        acc_ref[...] = jnp.zeros_like(acc_ref)

    def expert_rows(r0, rows):
        w_in2 = jnp.concatenate([wup_ref[...].astype(BF16), wgate_ref[...].astype(BF16)], axis=1)
        ag = _dot(hs_ref[r0:r0 + rows, :], w_in2)
        a = ag[:, 0:D_EXPERT]
        g = ag[:, D_EXPERT:]
        comb = ms_ref[r0:r0 + rows, :]
        lane = lax.broadcasted_iota(jnp.int32, comb.shape, 1)
        cw = jnp.sum(jnp.where(lane == e, comb, 0.0), axis=-1, keepdims=True)
        act = (g * jax.nn.sigmoid(g)) * a * cw
        acc_ref[r0:r0 + rows, :] += _dot(act.astype(BF16), wdown_ref[...].astype(BF16))

    @pl.when(valid & full)
    def _():
        expert_rows(0, TM_MOE)

    @pl.when(valid & jnp.logical_not(full))
    def _():
        for s in range(TM_MOE // MOE_SUB):
            r0 = s * MOE_SUB

            @pl.when((lo < r0 + MOE_SUB) & (hi > r0))
            def _():
                expert_rows(r0, MOE_SUB)

    @pl.when(valid & (j == EXP_PER_GROUP - 1) & (vlast_ref[v] == 1))
    def _():
        for c in range(N_SLAB):
            y3_ref[pl.ds(c, TM_MOE, stride=N_SLAB), :] = acc_ref[:, LANE * c:LANE * (c + 1)]


def _moe(sched, hs, ms, w_up, w_gate, w_down):
    wmap = lambda v, j, vt, vg, vlo, vhi, vfirst, vlast, vvalid: (
        vg[v] * EXP_PER_GROUP + jnp.where(vvalid[v] == 1, j, EXP_PER_GROUP - 1), 0, 0)
    tmap = lambda v, j, vt, *_: (vt[v], 0)
    n_slab = D_MODEL // LANE
    return pl.pallas_call(
        _moe_kernel,
        grid_spec=pltpu.PrefetchScalarGridSpec(
            num_scalar_prefetch=7,
            grid=(N_VISITS, EXP_PER_GROUP),
            in_specs=[
                pl.BlockSpec((TM_MOE, D_MODEL), tmap),
                pl.BlockSpec((TM_MOE, LANE), tmap),
                pl.BlockSpec((None, D_MODEL, D_EXPERT), wmap),
                pl.BlockSpec((None, D_MODEL, D_EXPERT), wmap),
                pl.BlockSpec((None, D_EXPERT, D_MODEL), wmap),
            ],
            out_specs=pl.BlockSpec((TM_MOE * n_slab, LANE), tmap),
            scratch_shapes=[pltpu.VMEM((TM_MOE, D_MODEL), F32)],
        ),
        out_shape=jax.ShapeDtypeStruct((T * n_slab, LANE), F32),
        compiler_params=pltpu.CompilerParams(
            dimension_semantics=("arbitrary", "arbitrary"), vmem_limit_bytes=VMEM_LIMIT),
        name="moe_grouped",
    )(*sched, hs, ms, w_up, w_gate, w_down)


def _final_kernel(pos_ref, ys_ref, x1_ref, mod_ref, fn_ref, y_ref, g_ref):
    i = pl.program_id(0)
    base = i * TM_MOE

    def body(r, carry):
        g_ref[pl.ds(pl.multiple_of(r * N_SLAB, N_SLAB), N_SLAB), :] = ys_ref[pos_ref[base + r]]
        return carry

    lax.fori_loop(0, TM_MOE, body, 0, unroll=8)
    mod = mod_ref[pl.ds(_mod_row(i, TM_MOE), 1), :]
    gate2 = mod[:, 5 * D_MODEL:6 * D_MODEL]
    moe = jnp.concatenate([g_ref[pl.ds(c, TM_MOE, stride=N_SLAB), :] for c in range(N_SLAB)], axis=1)
    x2 = x1_ref[...] + gate2 * moe
    y_ref[...] = (x2 * _rms(x2)) * fn_ref[...]


def _final(pos, ys, x1, mod, final_norm):
    n_slab = D_MODEL // LANE
    return pl.pallas_call(
        _final_kernel,
        grid_spec=pltpu.PrefetchScalarGridSpec(
            num_scalar_prefetch=1,
            grid=(T // TM_MOE,),
            in_specs=[
                pl.BlockSpec((T, n_slab, LANE), lambda i, pos: (0, 0, 0), pipeline_mode=pl.Buffered(1)),
                pl.BlockSpec((TM_MOE, D_MODEL), lambda i, pos: (i, 0)),
                pl.BlockSpec((N_COND, 6 * D_MODEL), lambda i, pos: (0, 0)),
                pl.BlockSpec((1, D_MODEL), lambda i, pos: (0, 0)),
            ],
            out_specs=pl.BlockSpec((TM_MOE, D_MODEL), lambda i, pos: (i, 0)),
            scratch_shapes=[pltpu.VMEM((TM_MOE * n_slab, LANE), F32)],
        ),
        out_shape=jax.ShapeDtypeStruct((T, D_MODEL), F32),
        compiler_params=pltpu.CompilerParams(
            dimension_semantics=("arbitrary",), vmem_limit_bytes=VMEM_LIMIT),
        name="moe_unsort_final",
    )(pos, ys, x1, mod, final_norm)


def _schedule(meta, cnt):
    n_tiles_post = T // TM_POST
    counts = cnt[:, 0, N_EXPERTS:N_EXPERTS + N_GROUPS].astype(jnp.int32)
    gtot = jnp.sum(counts, axis=0)
    goff = jnp.cumsum(gtot) - gtot
    tile_base = goff[None, :] + jnp.cumsum(counts, axis=0) - counts
    gid = meta[:, GID_LANE].astype(jnp.int32).reshape(n_tiles_post, TM_POST)
    rank = meta[:, RANK_LANE].astype(jnp.int32).reshape(n_tiles_post, TM_POST)
    pos = rank
    for grp in range(N_GROUPS):
        pos = pos + jnp.where(gid == grp, tile_base[:, grp:grp + 1], 0)
    pos = pos.reshape(T)

    n_tiles = T // TM_MOE
    t_lo = (jnp.arange(n_tiles, dtype=jnp.int32) * TM_MOE)[:, None]
    lo = jnp.clip(goff[None, :] - t_lo, 0, TM_MOE)
    hi = jnp.clip(goff[None, :] + gtot[None, :] - t_lo, 0, TM_MOE)
    ok = (hi > lo).reshape(-1)
    slot = jnp.cumsum(ok.astype(jnp.int32)) - 1
    n_ok = slot[-1] + 1
    sel = (slot[None, :] == jnp.arange(N_VISITS, dtype=jnp.int32)[:, None]) & ok[None, :]

    def pick(vals):
        return jnp.sum(jnp.where(sel, vals.reshape(-1)[None, :], 0), axis=1).astype(jnp.int32)

    pair_tile = jnp.broadcast_to(jnp.arange(n_tiles, dtype=jnp.int32)[:, None], (n_tiles, N_GROUPS))
    pair_group = jnp.broadcast_to(jnp.arange(N_GROUPS, dtype=jnp.int32)[None, :], (n_tiles, N_GROUPS))
    vt, vg, vlo, vhi = pick(pair_tile), pick(pair_group), pick(lo), pick(hi)
    valid = jnp.arange(N_VISITS, dtype=jnp.int32) < n_ok
    last = jnp.maximum(n_ok - 1, 0)
    vt = jnp.where(valid, vt, vt[last])
    vg = jnp.where(valid, vg, vg[last])
    change = (vt[1:] != vt[:-1]).astype(jnp.int32)
    one = jnp.ones((1,), jnp.int32)
    first = jnp.concatenate([one, change])
    idx = jnp.arange(N_VISITS, dtype=jnp.int32)
    last = jnp.where(idx == n_ok - 1, 1, jnp.concatenate([change, one]))
    return pos, (vt, vg, vlo, vhi, first, last, valid.astype(jnp.int32))


def _rope_tables():
    n = np.arange(DEC_SEQ)
    pos = np.stack([n // GRID_W, n % GRID_W], axis=1).astype(np.float32)
    half = ROPE_AXIS // 2
    inv = (1.0 / (ROPE_BASE ** (np.arange(0, ROPE_AXIS, 2, dtype=np.float32) / ROPE_AXIS))).astype(np.float32)
    ang = (pos[:, :, None] * inv[None, None, :]).astype(np.float32)
    cos = np.cos(ang).astype(np.float32)
    sin = np.sin(ang).astype(np.float32)
    tabs = np.zeros((2, 3, DEC_SEQ, LANE), np.float32)
    tabs[:, 0] = 1.0
    for a in range(2):
        lo = ROPE_LANE0 + a * ROPE_AXIS
        tabs[1, 0, :, lo:lo + half] = cos[:, a]
        tabs[1, 0, :, lo + half:lo + 2 * half] = cos[:, a]
        tabs[1, 1, :, lo + half:lo + 2 * half] = sin[:, a]
        tabs[1, 2, :, lo:lo + half] = -sin[:, a]
    return jnp.asarray(tabs)


def kernel(x_prompt, x_sample, cache_ckv, cache_krope, c, c_ctx, norm1, w_ada, b_ada, w_in, conv_w,
           w_conv_out, q_norm, w_uq, kv_norm, w_ukv, w_o, w_mix_out, norm2, w_grp, w_exp, w_up,
           w_gate, w_down, final_norm):
    l = 0
    x = jnp.concatenate([x_prompt.reshape(T_P, D_MODEL), x_sample.reshape(T_S, D_MODEL)], axis=0)
    cond = jnp.concatenate(
        [c_ctx[None, :], c, jnp.zeros((N_COND - 1 - DEC_BATCH, D_MODEL), F32)], axis=0)
    mod = _ada(cond, w_ada[l], b_ada[l][None, :])

    wi = w_in[l]
    o_cq = 3 * D_CONV
    o_kr = o_cq + Q_LORA + KV_LORA
    o_g = o_kr + QK_ROPE
    w_conv3 = wi[:, :o_cq].astype(BF16)
    w_kr_slot = jnp.pad(wi[:, o_kr:o_g], ((0, 0), (ROPE_LANE0, LANE - ROPE_LANE0 - QK_ROPE)))
    w_small = jnp.concatenate([wi[:, o_cq:o_kr], w_kr_slot], axis=1).astype(BF16)
    w_gates = wi[:, o_g:].astype(BF16)
    w_uq_slot = jnp.pad(w_uq[l].reshape(Q_LORA, N_HEADS, QK_NOPE + QK_ROPE),
                        ((0, 0), (0, 0), (0, LANE - QK_NOPE - QK_ROPE))
                        ).reshape(Q_LORA, N_HEADS * LANE).astype(BF16)
    wkv = w_ukv[l].reshape(KV_LORA, N_HEADS, QK_NOPE + V_HEAD)
    wk_slot = jnp.pad(wkv[:, :, :QK_NOPE], ((0, 0), (0, 0), (0, LANE - QK_NOPE)))
    wv = wkv[:, :, QK_NOPE:].reshape(KV_LORA, N_HEADS // 2, 2, V_HEAD)
    zero = jnp.zeros_like(wv[:, :, 0])
    wv_slot = jnp.stack([jnp.concatenate([wv[:, :, 0], zero], axis=-1),
                         jnp.concatenate([zero, wv[:, :, 1]], axis=-1)], axis=2)
    w_ukv_slot = jnp.concatenate([wk_slot.reshape(KV_LORA, N_HEADS * LANE),
                                  wv_slot.reshape(KV_LORA, N_HEADS * LANE)], axis=1).astype(BF16)
    w_route = jnp.pad(jnp.concatenate([w_exp[l], w_grp[l]], axis=1),
                      ((0, 0), (0, LANE - N_EXPERTS - N_GROUPS)))
    cache_krs = jnp.pad(cache_krope[:, l], ((0, 0), (0, 0), (ROPE_LANE0, LANE - ROPE_LANE0 - QK_ROPE)))

    zc, q, ckv, krs = _inproj(x, mod, norm1[l][None, :], w_conv3, w_small, conv_w[l],
                              q_norm[l][None, :], kv_norm[l][None, :], w_uq_slot, _rope_tables())
    o = _attention(q, ckv, krs, cache_ckv[:, l], cache_krs, w_ukv_slot)
    x1, h3, meta, cnt = _post(x, mod, norm1[l][None, :], w_gates, zc, o, w_conv_out[l].astype(BF16),
                              w_o[l].astype(BF16), w_mix_out[l].astype(BF16), norm2[l][None, :],
                              w_route)
    pos, sched = _schedule(meta, cnt)
    hs, ms = _dispatch(pos, h3.reshape(T, N_SLAB, LANE), meta)
    ys = _moe(sched, hs, ms, w_up[l], w_gate[l], w_down[l])
    y = _final(pos, ys.reshape(T, N_SLAB, LANE), x1, mod, final_norm[None, :])

    y_prompt = y[:T_P].reshape(BATCH, SEQ, D_MODEL)
    y_sample = y[T_P:].reshape(DEC_BATCH, DEC_SEQ, D_MODEL)
    new_ckv = ckv[:T_P].reshape(BATCH, 1, SEQ, KV_LORA)
    new_krope = krs[:T_P, ROPE_LANE0:ROPE_LANE0 + QK_ROPE].reshape(BATCH, 1, SEQ, QK_ROPE)
    return (y_prompt, y_sample, new_ckv, new_krope)
```

```python
import functools

import numpy as np
import jax
import jax.numpy as jnp
from jax import lax
from jax.experimental import pallas as pl
from jax.experimental.pallas import tpu as pltpu

F32 = jnp.float32
BF16 = jnp.bfloat16

D_MODEL = 1024
BATCH = 16
SEQ = 256
DEC_BATCH = 2
DEC_SEQ = 1024
PAST_LEN = 256
GRID_W = 64
N_HEADS = 8
QK_NOPE = 64
QK_ROPE = 32
V_HEAD = 64
Q_LORA = 256
KV_LORA = 128
ROPE_AXIS = QK_ROPE // 2
ROPE_BASE = 10000.0
ATTN_SCALE = (QK_NOPE + QK_ROPE) ** -0.5
D_CONV = D_MODEL
N_GROUPS = 4
EXP_PER_GROUP = 8
N_EXPERTS = N_GROUPS * EXP_PER_GROUP
D_EXPERT = 256
EPS = 1e-6

T_P = BATCH * SEQ
T_S = DEC_BATCH * DEC_SEQ
T = T_P + T_S
N_COND = 8
LANE = 128
ROPE_LANE0 = QK_NOPE
SMALL_COLS = Q_LORA + KV_LORA + LANE
VMEM_LIMIT = 56 * 1024 * 1024

TM_IN = 1024
TM_POST = 512
TM_MOE = 1024
TM_FINAL = 512
N_SLAB = D_MODEL // LANE
MOE_SUB = 256
N_VISITS = T // TM_MOE + N_GROUPS - 1
GID_LANE = 40
RANK_LANE = 41
Q_BLK = 256
CONV_CHUNK = 512


def _dot(a, b):
    return jnp.dot(a, b, preferred_element_type=F32)


def _rms(x):
    return lax.rsqrt(jnp.mean(x * x, axis=-1, keepdims=True) + EPS)


def _mod_row(i, tm):
    n_prompt = T_P // tm
    return jnp.where(i >= n_prompt, 1 + ((i - n_prompt) * tm) // DEC_SEQ, 0)


def _ada_kernel(cond_ref, w_ref, b_ref, o_ref):
    c = cond_ref[...]
    a = (c * jax.nn.sigmoid(c)).astype(BF16)
    o_ref[...] = _dot(a, w_ref[...].astype(BF16)) + b_ref[...]


def _ada(cond, w_ada, b_ada):
    n = 6 * D_MODEL
    bn = 1536
    return pl.pallas_call(
        _ada_kernel,
        grid=(n // bn,),
        in_specs=[
            pl.BlockSpec((N_COND, D_MODEL), lambda j: (0, 0)),
            pl.BlockSpec((D_MODEL, bn), lambda j: (0, j)),
            pl.BlockSpec((1, bn), lambda j: (0, j)),
        ],
        out_specs=pl.BlockSpec((N_COND, bn), lambda j: (0, j)),
        out_shape=jax.ShapeDtypeStruct((N_COND, n), F32),
        compiler_params=pltpu.CompilerParams(
            dimension_semantics=("parallel",), vmem_limit_bytes=VMEM_LIMIT),
        name="ada_mod",
    )(cond, w_ada, b_ada)


O_CQ = 3 * D_CONV
O_KR = O_CQ + Q_LORA + KV_LORA
O_GATE = O_KR + QK_ROPE
PREP_STEPS = 4


def _prep_kernel(win_ref, wco_ref, wo_ref, wmix_ref, c3_ref, sm_ref, g_ref, co_ref, o_ref, mix_ref):
    w = win_ref[...]
    c3_ref[...] = w[:, 0:O_CQ].astype(BF16)
    sm_ref[:, 0:Q_LORA + KV_LORA] = w[:, O_CQ:O_KR].astype(BF16)
    slab = w[:, O_KR:O_KR + LANE]
    lane = lax.broadcasted_iota(jnp.int32, slab.shape, 1)
    in_slot = (lane >= ROPE_LANE0) & (lane < ROPE_LANE0 + QK_ROPE)
    sm_ref[:, Q_LORA + KV_LORA:] = jnp.where(in_slot, pltpu.roll(slab, ROPE_LANE0, 1), 0.0).astype(BF16)
    g_ref[...] = w[:, O_GATE:].astype(BF16)
    co_ref[...] = wco_ref[...].astype(BF16)
    o_ref[...] = wo_ref[...].astype(BF16)
    mix_ref[...] = wmix_ref[...].astype(BF16)


def _prep_weights(w_in, w_conv_out, w_o, w_mix_out):
    rows = lambda n: n // PREP_STEPS
    spec = lambda n, cols: pl.BlockSpec((rows(n), cols), lambda i: (i, 0))
    in_cols = w_in.shape[1]
    n_o = N_HEADS * V_HEAD
    return pl.pallas_call(
        _prep_kernel,
        grid=(PREP_STEPS,),
        in_specs=[spec(D_MODEL, in_cols), spec(D_CONV, D_MODEL), spec(n_o, D_MODEL),
                  spec(D_MODEL, D_MODEL)],
        out_specs=[spec(D_MODEL, O_CQ), spec(D_MODEL, SMALL_COLS), spec(D_MODEL, 2 * D_MODEL),
                   spec(D_CONV, D_MODEL), spec(n_o, D_MODEL), spec(D_MODEL, D_MODEL)],
        out_shape=[jax.ShapeDtypeStruct((D_MODEL, O_CQ), BF16),
                   jax.ShapeDtypeStruct((D_MODEL, SMALL_COLS), BF16),
                   jax.ShapeDtypeStruct((D_MODEL, 2 * D_MODEL), BF16),
                   jax.ShapeDtypeStruct((D_CONV, D_MODEL), BF16),
                   jax.ShapeDtypeStruct((n_o, D_MODEL), BF16),
                   jax.ShapeDtypeStruct((D_MODEL, D_MODEL), BF16)],
        compiler_params=pltpu.CompilerParams(
            dimension_semantics=("parallel",), vmem_limit_bytes=VMEM_LIMIT),
        name="weight_casts",
    )(w_in, w_conv_out, w_o, w_mix_out)


def _stream_maps(tm):
    n_prompt = T_P // tm
    return (lambda i, *_: (jnp.minimum(i, n_prompt - 1), 0),
            lambda i, *_: (jnp.maximum(i - n_prompt, 0), 0))


def _inproj_kernel(xp_ref, xs_ref, mod_ref, n1_ref, wc_ref, ws_ref, cw_ref, qn_ref, kvn_ref, wuq_ref,
                   rope_ref, zc_ref, q_ref, ckv_ref, krs_ref):
    i = pl.program_id(0)
    is_sample = i >= T_P // TM_IN
    seq = jnp.where(is_sample, DEC_SEQ, SEQ)
    mod = mod_ref[pl.ds(_mod_row(i, TM_IN), 1), :]
    shift1 = mod[:, 0:D_MODEL]
    scale1 = mod[:, D_MODEL:2 * D_MODEL]
    x = jnp.where(is_sample, xs_ref[...], xp_ref[...])
    h = ((x * _rms(x)) * n1_ref[...]) * (1.0 + scale1) + shift1
    hb = h.astype(BF16)

    pos = lax.broadcasted_iota(jnp.int32, (TM_IN, 1), 0) & (seq - 1)
    first = pos == 0
    last = pos == seq - 1
    for j in range(D_CONV // CONV_CHUNK):
        c0 = j * CONV_CHUNK
        bg = _dot(hb, wc_ref[:, c0:c0 + CONV_CHUNK])
        cg = _dot(hb, wc_ref[:, D_CONV + c0:D_CONV + c0 + CONV_CHUNK])
        ui = _dot(hb, wc_ref[:, 2 * D_CONV + c0:2 * D_CONV + c0 + CONV_CHUNK])
        u = cg * ui
        u_prev = jnp.where(first, 0.0, pltpu.roll(u, 1, 0))
        u_next = jnp.where(last, 0.0, pltpu.roll(u, TM_IN - 1, 0))
        cw = cw_ref[:, c0:c0 + CONV_CHUNK]
        conv = u_prev * cw[0:1] + u * cw[1:2] + u_next * cw[2:3]
        zc_ref[:, c0:c0 + CONV_CHUNK] = (bg * conv).astype(BF16)

    sm = _dot(hb, ws_ref[...])
    cq = sm[:, 0:Q_LORA]
    ckv_raw = sm[:, Q_LORA:Q_LORA + KV_LORA]
    krs = sm[:, Q_LORA + KV_LORA:SMALL_COLS]
    cqn = (cq * _rms(cq)) * qn_ref[...]
    q = _dot(cqn.astype(BF16), wuq_ref[...])
    ckv_ref[...] = (ckv_raw * _rms(ckv_raw)) * kvn_ref[...]

    cos = rope_ref[0]
    sin_lo = rope_ref[1]
    sin_hi = rope_ref[2]

    def rot(v):
        return v * cos + pltpu.roll(v, 8, 1) * sin_lo + pltpu.roll(v, LANE - 8, 1) * sin_hi

    krs_ref[...] = rot(krs)
    for hh in range(N_HEADS):
        q_ref[:, LANE * hh:LANE * (hh + 1)] = rot(q[:, LANE * hh:LANE * (hh + 1)]).astype(BF16)


def _inproj(xp, xs, mod, norm1, w_conv3, w_small, conv_w, q_norm, kv_norm, w_uq_slot, rope_tabs):
    n_prompt = T_P // TM_IN
    const = lambda i: (0, 0)
    pmap, smap = _stream_maps(TM_IN)
    return pl.pallas_call(
        _inproj_kernel,
        grid=(T // TM_IN,),
        in_specs=[
            pl.BlockSpec((TM_IN, D_MODEL), pmap),
            pl.BlockSpec((TM_IN, D_MODEL), smap),
            pl.BlockSpec((N_COND, 6 * D_MODEL), const),
            pl.BlockSpec((1, D_MODEL), const),
            pl.BlockSpec((D_MODEL, 3 * D_CONV), const),
            pl.BlockSpec((D_MODEL, SMALL_COLS), const),
            pl.BlockSpec((3, D_CONV), const),
            pl.BlockSpec((1, Q_LORA), const),
            pl.BlockSpec((1, KV_LORA), const),
            pl.BlockSpec((Q_LORA, N_HEADS * LANE), const),
            pl.BlockSpec((None, 3, TM_IN, LANE),
                         lambda i: (jnp.where(i >= n_prompt, 1, 0), 0, 0, 0)),
        ],
        out_specs=[
            pl.BlockSpec((TM_IN, D_CONV), lambda i: (i, 0)),
            pl.BlockSpec((TM_IN, N_HEADS * LANE), lambda i: (i, 0)),
            pl.BlockSpec((TM_IN, KV_LORA), lambda i: (i, 0)),
            pl.BlockSpec((TM_IN, LANE), lambda i: (i, 0)),
        ],
        out_shape=[
            jax.ShapeDtypeStruct((T, D_CONV), BF16),
            jax.ShapeDtypeStruct((T, N_HEADS * LANE), BF16),
            jax.ShapeDtypeStruct((T, KV_LORA), F32),
            jax.ShapeDtypeStruct((T, LANE), F32),
        ],
        compiler_params=pltpu.CompilerParams(
            dimension_semantics=("parallel",), vmem_limit_bytes=VMEM_LIMIT),
        name="in_proj",
    )(xp, xs, mod, norm1, w_conv3, w_small, conv_w, q_norm, kv_norm, w_uq_slot, rope_tabs)


def _attn_kernel(*refs, n_src):
    q_ref = refs[0]
    ckv_refs = refs[1:1 + n_src]
    krs_refs = refs[1 + n_src:1 + 2 * n_src]
    wukv_ref = refs[1 + 2 * n_src]
    o_ref = refs[2 + 2 * n_src]
    kf_scr, v_scr = refs[3 + 2 * n_src:]

    @pl.when(pl.program_id(1) == 0)
    def _():
        off = 0
        for c_ref, k_ref in zip(ckv_refs, krs_refs):
            m = c_ref.shape[0]
            kv = _dot(c_ref[...].astype(BF16), wukv_ref[...])
            krs = k_ref[...]
            for hh in range(N_HEADS):
                kf_scr[hh, off:off + m, :] = (kv[:, LANE * hh:LANE * (hh + 1)] + krs).astype(BF16)
            v_scr[off:off + m, :] = kv[:, N_HEADS * LANE:].astype(BF16)
            off += m

    for pair in range(N_HEADS // 2):
        acc = None
        for hh in (2 * pair, 2 * pair + 1):
            qh = q_ref[:, LANE * hh:LANE * (hh + 1)]
            s = lax.dot_general(qh, kf_scr[hh], (((1,), (1,)), ((), ())),
                                preferred_element_type=F32) * ATTN_SCALE
            e = jnp.exp(s - jnp.max(s, axis=-1, keepdims=True))
            p = (e / jnp.sum(e, axis=-1, keepdims=True)).astype(BF16)
            part = _dot(p, v_scr[:, LANE * hh:LANE * (hh + 1)])
            acc = part if acc is None else acc + part
        o_ref[:, LANE * pair:LANE * (pair + 1)] = acc.astype(BF16)


def _attention(q, ckv, krs, cache_ckv, cache_krs, w_ukv_slot):
    kv_cols = 2 * N_HEADS * LANE
    cp = pltpu.CompilerParams(dimension_semantics=("parallel", "arbitrary"),
                              vmem_limit_bytes=VMEM_LIMIT)
    o_prompt = pl.pallas_call(
        functools.partial(_attn_kernel, n_src=1),
        grid=(BATCH, SEQ // Q_BLK),
        in_specs=[
            pl.BlockSpec((Q_BLK, N_HEADS * LANE), lambda b, j: (b * (SEQ // Q_BLK) + j, 0)),
            pl.BlockSpec((SEQ, KV_LORA), lambda b, j: (b, 0)),
            pl.BlockSpec((SEQ, LANE), lambda b, j: (b, 0)),
            pl.BlockSpec((KV_LORA, kv_cols), lambda b, j: (0, 0)),
        ],
        out_specs=pl.BlockSpec((Q_BLK, N_HEADS * V_HEAD), lambda b, j: (b * (SEQ // Q_BLK) + j, 0)),
        out_shape=jax.ShapeDtypeStruct((T_P, N_HEADS * V_HEAD), BF16),
        scratch_shapes=[pltpu.VMEM((N_HEADS, SEQ, LANE), BF16),
                        pltpu.VMEM((SEQ, N_HEADS * LANE), BF16)],
        compiler_params=cp,
        name="attn_prompt",
    )(q, ckv, krs, w_ukv_slot)

    m_all = DEC_SEQ + PAST_LEN
    nq = DEC_SEQ // Q_BLK
    q0 = T_P // Q_BLK
    s0 = T_P // DEC_SEQ
    o_sample = pl.pallas_call(
        functools.partial(_attn_kernel, n_src=2),
        grid=(DEC_BATCH, nq),
        in_specs=[
            pl.BlockSpec((Q_BLK, N_HEADS * LANE), lambda b, j: (q0 + b * nq + j, 0)),
            pl.BlockSpec((DEC_SEQ, KV_LORA), lambda b, j: (s0 + b, 0)),
            pl.BlockSpec((None, PAST_LEN, KV_LORA), lambda b, j: (b, 0, 0)),
            pl.BlockSpec((DEC_SEQ, LANE), lambda b, j: (s0 + b, 0)),
            pl.BlockSpec((None, PAST_LEN, LANE), lambda b, j: (b, 0, 0)),
            pl.BlockSpec((KV_LORA, kv_cols), lambda b, j: (0, 0)),
        ],
        out_specs=pl.BlockSpec((Q_BLK, N_HEADS * V_HEAD), lambda b, j: (b * nq + j, 0)),
        out_shape=jax.ShapeDtypeStruct((T_S, N_HEADS * V_HEAD), BF16),
        scratch_shapes=[pltpu.VMEM((N_HEADS, m_all, LANE), BF16),
                        pltpu.VMEM((m_all, N_HEADS * LANE), BF16)],
        compiler_params=cp,
        name="attn_sample",
    )(q, ckv, cache_ckv, krs, cache_krs, w_ukv_slot)
    return jnp.concatenate([o_prompt, o_sample], axis=0)


def _route(logits):
    lane = lax.broadcasted_iota(jnp.int32, logits.shape, 1)
    neg = -jnp.inf
    big = jnp.int32(1 << 20)
    gmask = (lane >= N_EXPERTS) & (lane < N_EXPERTS + N_GROUPS)
    gl = jnp.where(gmask, logits, neg)
    gmax = jnp.max(gl, axis=-1, keepdims=True)
    gsum = jnp.sum(jnp.where(gmask, jnp.exp(gl - gmax), 0.0), axis=-1, keepdims=True)
    p_g = 1.0 / gsum
    g_idx = jnp.min(jnp.where(gl == gmax, lane, big), axis=-1, keepdims=True) - N_EXPERTS

    emask = (lane < N_EXPERTS) & ((lane >> 3) == g_idx)
    el = jnp.where(emask, logits, neg)
    m1 = jnp.max(el, axis=-1, keepdims=True)
    i1 = jnp.min(jnp.where(el == m1, lane, big), axis=-1, keepdims=True)
    el2 = jnp.where(lane == i1, neg, el)
    m2 = jnp.max(el2, axis=-1, keepdims=True)
    i2 = jnp.min(jnp.where(el2 == m2, lane, big), axis=-1, keepdims=True)
    z = jnp.sum(jnp.where(emask, jnp.exp(el - m1), 0.0), axis=-1, keepdims=True)
    p1 = 1.0 / z
    p2 = jnp.exp(m2 - m1) / z
    tot = p1 + p2
    w1 = p_g * p1 / tot
    w2 = p_g * p2 / tot
    return jnp.where(lane == i1, w1, 0.0) + jnp.where(lane == i2, w2, 0.0), g_idx


def _post_kernel(xp_ref, xs_ref, mod_ref, n1_ref, wg_ref, zc_ref, o_ref, wco_ref, wo_ref, wmix_ref,
                 n2_ref, wr_ref, x1_ref, h3_ref, meta_ref, cnt_ref):
    i = pl.program_id(0)
    x = jnp.where(i >= T_P // TM_POST, xs_ref[...], xp_ref[...])
    mod = mod_ref[pl.ds(_mod_row(i, TM_POST), 1), :]
    shift1 = mod[:, 0:D_MODEL]
    scale1 = mod[:, D_MODEL:2 * D_MODEL]
    gate1 = mod[:, 2 * D_MODEL:3 * D_MODEL]
    shift2 = mod[:, 3 * D_MODEL:4 * D_MODEL]
    scale2 = mod[:, 4 * D_MODEL:5 * D_MODEL]
    h = ((x * _rms(x)) * n1_ref[...]) * (1.0 + scale1) + shift1
    g = _dot(h.astype(BF16), wg_ref[...])
    y_conv = _dot(zc_ref[...], wco_ref[...])
    y_mla = _dot(o_ref[...], wo_ref[...])
    merged = (jax.nn.sigmoid(g[:, 0:D_MODEL]) * y_conv
              + jax.nn.sigmoid(g[:, D_MODEL:2 * D_MODEL]) * y_mla)
    y = _dot(merged.astype(BF16), wmix_ref[...])
    x1 = x + gate1 * y
    x1_ref[...] = x1
    h2 = ((x1 * _rms(x1)) * n2_ref[...]) * (1.0 + scale2) + shift2
    h2_hi = h2.astype(BF16)
    h2_lo = (h2 - h2_hi.astype(F32)).astype(BF16)
    hh = _dot(h2_hi, wr_ref[...])
    logits = hh[:, 0:LANE] + hh[:, LANE:2 * LANE] + _dot(h2_lo, wr_ref[:, 0:LANE])
    comb, g_idx = _route(logits)

    lane = lax.broadcasted_iota(jnp.int32, comb.shape, 1)
    onehot = lane == g_idx + N_EXPERTS
    r_i = lax.broadcasted_iota(jnp.int32, (TM_POST, TM_POST), 0)
    c_i = lax.broadcasted_iota(jnp.int32, (TM_POST, TM_POST), 1)
    lower = jnp.where(c_i < r_i, 1.0, 0.0).astype(BF16)
    before = _dot(lower, jnp.where(onehot, 1.0, 0.0).astype(BF16))
    rank = jnp.sum(jnp.where(onehot, before, 0.0), axis=-1, keepdims=True)
    counts = jnp.sum(jnp.where(onehot, 1.0, 0.0), axis=0, keepdims=True)
    cnt_ref[...] = jnp.broadcast_to(counts, cnt_ref.shape)

    meta = (comb + jnp.where(lane == GID_LANE, g_idx.astype(F32), 0.0)
            + jnp.where(lane == RANK_LANE, rank, 0.0))
    meta_ref[...] = meta
    for c in range(N_SLAB):
        h3_ref[pl.ds(c, TM_POST, stride=N_SLAB), :] = h2[:, LANE * c:LANE * (c + 1)]


def _post(xp, xs, mod, norm1, w_gates, zc, o, w_conv_out, w_o, w_mix_out, norm2, w_route):
    const = lambda i: (0, 0)
    row = lambda i: (i, 0)
    pmap, smap = _stream_maps(TM_POST)
    return pl.pallas_call(
        _post_kernel,
        grid=(T // TM_POST,),
        in_specs=[
            pl.BlockSpec((TM_POST, D_MODEL), pmap),
            pl.BlockSpec((TM_POST, D_MODEL), smap),
            pl.BlockSpec((N_COND, 6 * D_MODEL), const),
            pl.BlockSpec((1, D_MODEL), const),
            pl.BlockSpec((D_MODEL, 2 * D_MODEL), const),
            pl.BlockSpec((TM_POST, D_CONV), row),
            pl.BlockSpec((TM_POST, N_HEADS * V_HEAD), row),
            pl.BlockSpec((D_CONV, D_MODEL), const),
            pl.BlockSpec((N_HEADS * V_HEAD, D_MODEL), const),
            pl.BlockSpec((D_MODEL, D_MODEL), const),
            pl.BlockSpec((1, D_MODEL), const),
            pl.BlockSpec((D_MODEL, 2 * LANE), const),
        ],
        out_specs=[
            pl.BlockSpec((TM_POST, D_MODEL), row),
            pl.BlockSpec((TM_POST * N_SLAB, LANE), row),
            pl.BlockSpec((TM_POST, LANE), row),
            pl.BlockSpec((None, 8, LANE), lambda i: (i, 0, 0)),
        ],
        out_shape=[
            jax.ShapeDtypeStruct((T, D_MODEL), F32),
            jax.ShapeDtypeStruct((T * N_SLAB, LANE), F32),
            jax.ShapeDtypeStruct((T, LANE), F32),
            jax.ShapeDtypeStruct((T // TM_POST, 8, LANE), F32),
        ],
        compiler_params=pltpu.CompilerParams(
            dimension_semantics=("parallel",), vmem_limit_bytes=VMEM_LIMIT),
        name="post_mixer",
    )(xp, xs, mod, norm1, w_gates, zc, o, w_conv_out, w_o, w_mix_out, norm2, w_route)


def _dispatch_kernel(pos_ref, h3_ref, m_ref, hs_ref, ms_ref, src_ref, xg_ref):
    i = pl.program_id(0)

    @pl.when(i == 0)
    def _():
        def invert(t, carry):
            src_ref[pos_ref[t]] = t
            return carry

        lax.fori_loop(0, T, invert, 0, unroll=8)

    base = i * TM_MOE

    def body(r, carry):
        tok = src_ref[base + r]
        xg_ref[pl.ds(pl.multiple_of(r * N_SLAB, N_SLAB), N_SLAB), :] = h3_ref[tok]
        ms_ref[pl.ds(r, 1), :] = m_ref[pl.ds(tok, 1), :]
        return carry

    lax.fori_loop(0, TM_MOE, body, 0, unroll=8)
    for c in range(N_SLAB):
        hs_ref[:, LANE * c:LANE * (c + 1)] = xg_ref[pl.ds(c, TM_MOE, stride=N_SLAB), :].astype(BF16)


def _dispatch(pos, h3, meta):
    n_slab = N_SLAB
    return pl.pallas_call(
        _dispatch_kernel,
        grid_spec=pltpu.PrefetchScalarGridSpec(
            num_scalar_prefetch=1,
            grid=(T // TM_MOE,),
            in_specs=[pl.BlockSpec((T, n_slab, LANE), lambda i, pos: (0, 0, 0),
                                   pipeline_mode=pl.Buffered(1)),
                      pl.BlockSpec((T, LANE), lambda i, pos: (0, 0), pipeline_mode=pl.Buffered(1))],
            out_specs=[pl.BlockSpec((TM_MOE, D_MODEL), lambda i, pos: (i, 0)),
                       pl.BlockSpec((TM_MOE, LANE), lambda i, pos: (i, 0))],
            scratch_shapes=[pltpu.SMEM((T,), jnp.int32),
                            pltpu.VMEM((TM_MOE * n_slab, LANE), F32)],
        ),
        out_shape=[jax.ShapeDtypeStruct((T, D_MODEL), BF16),
                   jax.ShapeDtypeStruct((T, LANE), F32)],
        compiler_params=pltpu.CompilerParams(
            dimension_semantics=("arbitrary",), vmem_limit_bytes=VMEM_LIMIT),
        name="moe_dispatch",
    )(pos, h3, meta)


def _moe_kernel(vt_ref, vg_ref, vlo_ref, vhi_ref, vfirst_ref, vlast_ref, vvalid_ref,
                hs_ref, ms_ref, wup_ref, wgate_ref, wdown_ref, y3_ref, acc_ref):
    v = pl.program_id(0)
    j = pl.program_id(1)
    valid = vvalid_ref[v] == 1
    lo = vlo_ref[v]
    hi = vhi_ref[v]
    e = vg_ref[v] * EXP_PER_GROUP + j
    full = (lo == 0) & (hi == TM_MOE)

    @pl.when(valid & (j == 0) & (vfirst_ref[v] == 1))
    def _():
        acc_ref[...] = jnp.zeros_like(acc_ref)

    def expert_rows(r0, rows):
        w_in2 = jnp.concatenate([wup_ref[...].astype(BF16), wgate_ref[...].astype(BF16)], axis=1)
        ag = _dot(hs_ref[r0:r0 + rows, :], w_in2)
        a = ag[:, 0:D_EXPERT]
        g = ag[:, D_EXPERT:]
        comb = ms_ref[r0:r0 + rows, :]
        lane = lax.broadcasted_iota(jnp.int32, comb.shape, 1)
        cw = jnp.sum(jnp.where(lane == e, comb, 0.0), axis=-1, keepdims=True)
        act = (g * jax.nn.sigmoid(g)) * a * cw
        acc_ref[r0:r0 + rows, :] += _dot(act.astype(BF16), wdown_ref[...].astype(BF16))

    @pl.when(valid & full)
    def _():
        expert_rows(0, TM_MOE)

    @pl.when(valid & jnp.logical_not(full))
    def _():
        for s in range(TM_MOE // MOE_SUB):
            r0 = s * MOE_SUB

            @pl.when((lo < r0 + MOE_SUB) & (hi > r0))
            def _():
                expert_rows(r0, MOE_SUB)

    @pl.when(valid & (j == EXP_PER_GROUP - 1) & (vlast_ref[v] == 1))
    def _():
        for c in range(N_SLAB):
            y3_ref[pl.ds(c, TM_MOE, stride=N_SLAB), :] = acc_ref[:, LANE * c:LANE * (c + 1)]


def _moe(sched, hs, ms, w_up, w_gate, w_down):
    wmap = lambda v, j, vt, vg, vlo, vhi, vfirst, vlast, vvalid: (
        vg[v] * EXP_PER_GROUP + jnp.where(vvalid[v] == 1, j, EXP_PER_GROUP - 1), 0, 0)
    tmap = lambda v, j, vt, *_: (vt[v], 0)
    n_slab = D_MODEL // LANE
    return pl.pallas_call(
        _moe_kernel,
        grid_spec=pltpu.PrefetchScalarGridSpec(
            num_scalar_prefetch=7,
            grid=(N_VISITS, EXP_PER_GROUP),
            in_specs=[
                pl.BlockSpec((TM_MOE, D_MODEL), tmap),
                pl.BlockSpec((TM_MOE, LANE), tmap),
                pl.BlockSpec((None, D_MODEL, D_EXPERT), wmap),
                pl.BlockSpec((None, D_MODEL, D_EXPERT), wmap),
                pl.BlockSpec((None, D_EXPERT, D_MODEL), wmap),
            ],
            out_specs=pl.BlockSpec((TM_MOE * n_slab, LANE), tmap),
            scratch_shapes=[pltpu.VMEM((TM_MOE, D_MODEL), F32)],
        ),
        out_shape=jax.ShapeDtypeStruct((T * n_slab, LANE), F32),
        compiler_params=pltpu.CompilerParams(
            dimension_semantics=("arbitrary", "arbitrary"), vmem_limit_bytes=VMEM_LIMIT),
        name="moe_grouped",
    )(*sched, hs, ms, w_up, w_gate, w_down)


def _final_kernel(pos_ref, ys_ref, x1_ref, mod_ref, fn_ref, yp_ref, ysm_ref, g_ref):
    i = pl.program_id(0)
    base = i * TM_FINAL

    def body(r, carry):
        g_ref[pl.ds(pl.multiple_of(r * N_SLAB, N_SLAB), N_SLAB), :] = ys_ref[pos_ref[base + r]]
        return carry

    lax.fori_loop(0, TM_FINAL, body, 0, unroll=8)
    mod = mod_ref[pl.ds(_mod_row(i, TM_FINAL), 1), :]
    gate2 = mod[:, 5 * D_MODEL:6 * D_MODEL]
    moe = jnp.concatenate([g_ref[pl.ds(c, TM_FINAL, stride=N_SLAB), :] for c in range(N_SLAB)], axis=1)
    x2 = x1_ref[...] + gate2 * moe
    y = (x2 * _rms(x2)) * fn_ref[...]
    is_sample = i >= T_P // TM_FINAL

    @pl.when(jnp.logical_not(is_sample))
    def _():
        yp_ref[...] = y

    @pl.when(is_sample)
    def _():
        ysm_ref[...] = y


def _final(pos, ys, x1, mod, final_norm):
    n_slab = D_MODEL // LANE
    pmap, smap = _stream_maps(TM_FINAL)
    return pl.pallas_call(
        _final_kernel,
        grid_spec=pltpu.PrefetchScalarGridSpec(
            num_scalar_prefetch=1,
            grid=(T // TM_FINAL,),
            in_specs=[
                pl.BlockSpec((T, n_slab, LANE), lambda i, pos: (0, 0, 0), pipeline_mode=pl.Buffered(1)),
                pl.BlockSpec((TM_FINAL, D_MODEL), lambda i, pos: (i, 0)),
                pl.BlockSpec((N_COND, 6 * D_MODEL), lambda i, pos: (0, 0)),
                pl.BlockSpec((1, D_MODEL), lambda i, pos: (0, 0)),
            ],
            out_specs=[pl.BlockSpec((TM_FINAL, D_MODEL), pmap),
                       pl.BlockSpec((TM_FINAL, D_MODEL), smap)],
            scratch_shapes=[pltpu.VMEM((TM_FINAL * n_slab, LANE), F32)],
        ),
        out_shape=[jax.ShapeDtypeStruct((T_P, D_MODEL), F32),
                   jax.ShapeDtypeStruct((T_S, D_MODEL), F32)],
        compiler_params=pltpu.CompilerParams(
            dimension_semantics=("arbitrary",), vmem_limit_bytes=VMEM_LIMIT),
        name="moe_unsort_final",
    )(pos, ys, x1, mod, final_norm)


def _schedule(meta, cnt):
    n_tiles_post = T // TM_POST
    counts = cnt[:, 0, N_EXPERTS:N_EXPERTS + N_GROUPS].astype(jnp.int32)
    gtot = jnp.sum(counts, axis=0)
    goff = jnp.cumsum(gtot) - gtot
    tile_base = goff[None, :] + jnp.cumsum(counts, axis=0) - counts
    gid = meta[:, GID_LANE].astype(jnp.int32).reshape(n_tiles_post, TM_POST)
    rank = meta[:, RANK_LANE].astype(jnp.int32).reshape(n_tiles_post, TM_POST)
    pos = rank
    for grp in range(N_GROUPS):
        pos = pos + jnp.where(gid == grp, tile_base[:, grp:grp + 1], 0)
    pos = pos.reshape(T)

    n_tiles = T // TM_MOE
    t_lo = (jnp.arange(n_tiles, dtype=jnp.int32) * TM_MOE)[:, None]
    lo = jnp.clip(goff[None, :] - t_lo, 0, TM_MOE)
    hi = jnp.clip(goff[None, :] + gtot[None, :] - t_lo, 0, TM_MOE)
    ok = (hi > lo).reshape(-1)
    slot = jnp.cumsum(ok.astype(jnp.int32)) - 1
    n_ok = slot[-1] + 1
    sel = (slot[None, :] == jnp.arange(N_VISITS, dtype=jnp.int32)[:, None]) & ok[None, :]

    def pick(vals):
        return jnp.sum(jnp.where(sel, vals.reshape(-1)[None, :], 0), axis=1).astype(jnp.int32)

    pair_tile = jnp.broadcast_to(jnp.arange(n_tiles, dtype=jnp.int32)[:, None], (n_tiles, N_GROUPS))
    pair_group = jnp.broadcast_to(jnp.arange(N_GROUPS, dtype=jnp.int32)[None, :], (n_tiles, N_GROUPS))
    vt, vg, vlo, vhi = pick(pair_tile), pick(pair_group), pick(lo), pick(hi)
    valid = jnp.arange(N_VISITS, dtype=jnp.int32) < n_ok
    last = jnp.maximum(n_ok - 1, 0)
    vt = jnp.where(valid, vt, vt[last])
    vg = jnp.where(valid, vg, vg[last])
    change = (vt[1:] != vt[:-1]).astype(jnp.int32)
    one = jnp.ones((1,), jnp.int32)
    first = jnp.concatenate([one, change])
    idx = jnp.arange(N_VISITS, dtype=jnp.int32)
    last = jnp.where(idx == n_ok - 1, 1, jnp.concatenate([change, one]))
    return pos, (vt, vg, vlo, vhi, first, last, valid.astype(jnp.int32))


def _rope_tables():
    n = np.arange(DEC_SEQ)
    pos = np.stack([n // GRID_W, n % GRID_W], axis=1).astype(np.float32)
    half = ROPE_AXIS // 2
    inv = (1.0 / (ROPE_BASE ** (np.arange(0, ROPE_AXIS, 2, dtype=np.float32) / ROPE_AXIS))).astype(np.float32)
    ang = (pos[:, :, None] * inv[None, None, :]).astype(np.float32)
    cos = np.cos(ang).astype(np.float32)
    sin = np.sin(ang).astype(np.float32)
    tabs = np.zeros((2, 3, DEC_SEQ, LANE), np.float32)
    tabs[:, 0] = 1.0
    for a in range(2):
        lo = ROPE_LANE0 + a * ROPE_AXIS
        tabs[1, 0, :, lo:lo + half] = cos[:, a]
        tabs[1, 0, :, lo + half:lo + 2 * half] = cos[:, a]
        tabs[1, 1, :, lo + half:lo + 2 * half] = sin[:, a]
        tabs[1, 2, :, lo:lo + half] = -sin[:, a]
    return jnp.asarray(tabs)


def kernel(x_prompt, x_sample, cache_ckv, cache_krope, c, c_ctx, norm1, w_ada, b_ada, w_in, conv_w,
           w_conv_out, q_norm, w_uq, kv_norm, w_ukv, w_o, w_mix_out, norm2, w_grp, w_exp, w_up,
           w_gate, w_down, final_norm):
    l = 0
    xp = x_prompt.reshape(T_P, D_MODEL)
    xs = x_sample.reshape(T_S, D_MODEL)
    cond = jnp.concatenate(
        [c_ctx[None, :], c, jnp.zeros((N_COND - 1 - DEC_BATCH, D_MODEL), F32)], axis=0)
    mod = _ada(cond, w_ada[l], b_ada[l][None, :])

    w_conv3, w_small, w_gates, w_co_b, w_o_b, w_mix_b = _prep_weights(
        w_in[l], w_conv_out[l], w_o[l], w_mix_out[l])
    w_uq_slot = jnp.pad(w_uq[l].reshape(Q_LORA, N_HEADS, QK_NOPE + QK_ROPE),
                        ((0, 0), (0, 0), (0, LANE - QK_NOPE - QK_ROPE))
                        ).reshape(Q_LORA, N_HEADS * LANE).astype(BF16)
    wkv = w_ukv[l].reshape(KV_LORA, N_HEADS, QK_NOPE + V_HEAD)
    wk_slot = jnp.pad(wkv[:, :, :QK_NOPE], ((0, 0), (0, 0), (0, LANE - QK_NOPE)))
    wv = wkv[:, :, QK_NOPE:].reshape(KV_LORA, N_HEADS // 2, 2, V_HEAD)
    zero = jnp.zeros_like(wv[:, :, 0])
    wv_slot = jnp.stack([jnp.concatenate([wv[:, :, 0], zero], axis=-1),
                         jnp.concatenate([zero, wv[:, :, 1]], axis=-1)], axis=2)
    w_ukv_slot = jnp.concatenate([wk_slot.reshape(KV_LORA, N_HEADS * LANE),
                                  wv_slot.reshape(KV_LORA, N_HEADS * LANE)], axis=1).astype(BF16)
    w_route = jnp.pad(jnp.concatenate([w_exp[l], w_grp[l]], axis=1),
                      ((0, 0), (0, LANE - N_EXPERTS - N_GROUPS)))
    w_route_hi = w_route.astype(BF16)
    w_route_lo = (w_route - w_route_hi.astype(F32)).astype(BF16)
    w_route2 = jnp.concatenate([w_route_hi, w_route_lo], axis=1)
    cache_krs = jnp.pad(cache_krope[:, l], ((0, 0), (0, 0), (ROPE_LANE0, LANE - ROPE_LANE0 - QK_ROPE)))

    zc, q, ckv, krs = _inproj(xp, xs, mod, norm1[l][None, :], w_conv3, w_small, conv_w[l],
                              q_norm[l][None, :], kv_norm[l][None, :], w_uq_slot, _rope_tables())
    o = _attention(q, ckv, krs, cache_ckv[:, l], cache_krs, w_ukv_slot)
    x1, h3, meta, cnt = _post(xp, xs, mod, norm1[l][None, :], w_gates, zc, o, w_co_b, w_o_b, w_mix_b,
                              norm2[l][None, :], w_route2)
    pos, sched = _schedule(meta, cnt)
    hs, ms = _dispatch(pos, h3.reshape(T, N_SLAB, LANE), meta)
    ys = _moe(sched, hs, ms, w_up[l], w_gate[l], w_down[l])
    yp, ysm = _final(pos, ys.reshape(T, N_SLAB, LANE), x1, mod, final_norm[None, :])

    y_prompt = yp.reshape(BATCH, SEQ, D_MODEL)
    y_sample = ysm.reshape(DEC_BATCH, DEC_SEQ, D_MODEL)
    new_ckv = ckv[:T_P].reshape(BATCH, 1, SEQ, KV_LORA)
    new_krope = krs[:T_P, ROPE_LANE0:ROPE_LANE0 + QK_ROPE].reshape(BATCH, 1, SEQ, QK_ROPE)
    return (y_prompt, y_sample, new_ckv, new_krope)
```

```python
import functools

import numpy as np
import jax
import jax.numpy as jnp
from jax import lax
from jax.experimental import pallas as pl
from jax.experimental.pallas import tpu as pltpu

F32 = jnp.float32
BF16 = jnp.bfloat16

D_MODEL = 1024
BATCH = 16
SEQ = 256
DEC_BATCH = 2
DEC_SEQ = 1024
PAST_LEN = 256
GRID_W = 64
N_HEADS = 8
QK_NOPE = 64
QK_ROPE = 32
V_HEAD = 64
Q_LORA = 256
KV_LORA = 128
ROPE_AXIS = QK_ROPE // 2
ROPE_BASE = 10000.0
ATTN_SCALE = (QK_NOPE + QK_ROPE) ** -0.5
D_CONV = D_MODEL
N_GROUPS = 4
EXP_PER_GROUP = 8
N_EXPERTS = N_GROUPS * EXP_PER_GROUP
D_EXPERT = 256
EPS = 1e-6

T_P = BATCH * SEQ
T_S = DEC_BATCH * DEC_SEQ
T = T_P + T_S
N_COND = 8
LANE = 128
ROPE_LANE0 = QK_NOPE
SMALL_COLS = Q_LORA + KV_LORA + LANE
VMEM_LIMIT = 56 * 1024 * 1024

TM_IN = 1024
TM_POST = 512
TM_MOE = 1024
TM_FINAL = 512
N_SLAB = D_MODEL // LANE
MOE_SUB = 256
N_VISITS = T // TM_MOE + N_GROUPS - 1
GID_LANE = 40
RANK_LANE = 41
Q_BLK = 256
CONV_CHUNK = 512


def _dot(a, b):
    return jnp.dot(a, b, preferred_element_type=F32)


def _rms(x):
    return lax.rsqrt(jnp.mean(x * x, axis=-1, keepdims=True) + EPS)


def _slab(t):
    return pl.ds(pl.multiple_of(t * N_SLAB, N_SLAB), N_SLAB)


def _mod_row(i, tm):
    n_prompt = T_P // tm
    return jnp.where(i >= n_prompt, 1 + ((i - n_prompt) * tm) // DEC_SEQ, 0)


def _ada_kernel(cond_ref, w_ref, b_ref, o_ref):
    c = cond_ref[...]
    a = (c * jax.nn.sigmoid(c)).astype(BF16)
    o_ref[...] = _dot(a, w_ref[...].astype(BF16)) + b_ref[...]


def _ada(cond, w_ada, b_ada):
    n = 6 * D_MODEL
    bn = 1536
    return pl.pallas_call(
        _ada_kernel,
        grid=(n // bn,),
        in_specs=[
            pl.BlockSpec((N_COND, D_MODEL), lambda j: (0, 0)),
            pl.BlockSpec((D_MODEL, bn), lambda j: (0, j)),
            pl.BlockSpec((1, bn), lambda j: (0, j)),
        ],
        out_specs=pl.BlockSpec((N_COND, bn), lambda j: (0, j)),
        out_shape=jax.ShapeDtypeStruct((N_COND, n), F32),
        compiler_params=pltpu.CompilerParams(
            dimension_semantics=("parallel",), vmem_limit_bytes=VMEM_LIMIT),
        name="ada_mod",
    )(cond, w_ada, b_ada)


O_CQ = 3 * D_CONV
O_KR = O_CQ + Q_LORA + KV_LORA
O_GATE = O_KR + QK_ROPE
PREP_STEPS = 4


def _prep_kernel(win_ref, wco_ref, wo_ref, wmix_ref, c3_ref, sm_ref, g_ref, co_ref, o_ref, mix_ref):
    w = win_ref[...]
    c3_ref[...] = w[:, 0:O_CQ].astype(BF16)
    sm_ref[:, 0:Q_LORA + KV_LORA] = w[:, O_CQ:O_KR].astype(BF16)
    slab = w[:, O_KR:O_KR + LANE]
    lane = lax.broadcasted_iota(jnp.int32, slab.shape, 1)
    in_slot = (lane >= ROPE_LANE0) & (lane < ROPE_LANE0 + QK_ROPE)
    sm_ref[:, Q_LORA + KV_LORA:] = jnp.where(in_slot, pltpu.roll(slab, ROPE_LANE0, 1), 0.0).astype(BF16)
    g_ref[...] = w[:, O_GATE:].astype(BF16)
    co_ref[...] = wco_ref[...].astype(BF16)
    o_ref[...] = wo_ref[...].astype(BF16)
    mix_ref[...] = wmix_ref[...].astype(BF16)


def _prep_weights(w_in, w_conv_out, w_o, w_mix_out):
    rows = lambda n: n // PREP_STEPS
    spec = lambda n, cols: pl.BlockSpec((rows(n), cols), lambda i: (i, 0))
    in_cols = w_in.shape[1]
    n_o = N_HEADS * V_HEAD
    return pl.pallas_call(
        _prep_kernel,
        grid=(PREP_STEPS,),
        in_specs=[spec(D_MODEL, in_cols), spec(D_CONV, D_MODEL), spec(n_o, D_MODEL),
                  spec(D_MODEL, D_MODEL)],
        out_specs=[spec(D_MODEL, O_CQ), spec(D_MODEL, SMALL_COLS), spec(D_MODEL, 2 * D_MODEL),
                   spec(D_CONV, D_MODEL), spec(n_o, D_MODEL), spec(D_MODEL, D_MODEL)],
        out_shape=[jax.ShapeDtypeStruct((D_MODEL, O_CQ), BF16),
                   jax.ShapeDtypeStruct((D_MODEL, SMALL_COLS), BF16),
                   jax.ShapeDtypeStruct((D_MODEL, 2 * D_MODEL), BF16),
                   jax.ShapeDtypeStruct((D_CONV, D_MODEL), BF16),
                   jax.ShapeDtypeStruct((n_o, D_MODEL), BF16),
                   jax.ShapeDtypeStruct((D_MODEL, D_MODEL), BF16)],
        compiler_params=pltpu.CompilerParams(
            dimension_semantics=("parallel",), vmem_limit_bytes=VMEM_LIMIT),
        name="weight_casts",
    )(w_in, w_conv_out, w_o, w_mix_out)


def _stream_maps(tm):
    n_prompt = T_P // tm
    return (lambda i, *_: (jnp.minimum(i, n_prompt - 1), 0),
            lambda i, *_: (jnp.maximum(i - n_prompt, 0), 0))


def _inproj_kernel(xp_ref, xs_ref, mod_ref, n1_ref, wc_ref, ws_ref, cw_ref, qn_ref, kvn_ref, wuq_ref,
                   rope_ref, zc_ref, q_ref, ckv_ref, krs_ref):
    i = pl.program_id(0)
    is_sample = i >= T_P // TM_IN
    seq = jnp.where(is_sample, DEC_SEQ, SEQ)
    mod = mod_ref[pl.ds(_mod_row(i, TM_IN), 1), :]
    shift1 = mod[:, 0:D_MODEL]
    scale1 = mod[:, D_MODEL:2 * D_MODEL]
    x = jnp.where(is_sample, xs_ref[...], xp_ref[...])
    h = ((x * _rms(x)) * n1_ref[...]) * (1.0 + scale1) + shift1
    hb = h.astype(BF16)

    sm = _dot(hb, ws_ref[...])
    cq = sm[:, 0:Q_LORA]
    ckv_raw = sm[:, Q_LORA:Q_LORA + KV_LORA]
    krs = sm[:, Q_LORA + KV_LORA:SMALL_COLS]
    cqn = (cq * _rms(cq)) * qn_ref[...]
    q = _dot(cqn.astype(BF16), wuq_ref[...])
    ckv_ref[...] = (ckv_raw * _rms(ckv_raw)) * kvn_ref[...]

    cos = rope_ref[0]
    sin_lo = rope_ref[1]
    sin_hi = rope_ref[2]

    def rot(v):
        return v * cos + pltpu.roll(v, 8, 1) * sin_lo + pltpu.roll(v, LANE - 8, 1) * sin_hi

    krs_ref[...] = rot(krs)
    for hh in range(N_HEADS):
        q_ref[:, LANE * hh:LANE * (hh + 1)] = rot(q[:, LANE * hh:LANE * (hh + 1)]).astype(BF16)

    pos = lax.broadcasted_iota(jnp.int32, (TM_IN, 1), 0) & (seq - 1)
    first = pos == 0
    last = pos == seq - 1
    for j in range(D_CONV // CONV_CHUNK):
        c0 = j * CONV_CHUNK
        bg = _dot(hb, wc_ref[:, c0:c0 + CONV_CHUNK])
        cg = _dot(hb, wc_ref[:, D_CONV + c0:D_CONV + c0 + CONV_CHUNK])
        ui = _dot(hb, wc_ref[:, 2 * D_CONV + c0:2 * D_CONV + c0 + CONV_CHUNK])
        u = cg * ui
        u_prev = jnp.where(first, 0.0, pltpu.roll(u, 1, 0))
        u_next = jnp.where(last, 0.0, pltpu.roll(u, TM_IN - 1, 0))
        cw = cw_ref[:, c0:c0 + CONV_CHUNK]
        conv = u_prev * cw[0:1] + u * cw[1:2] + u_next * cw[2:3]
        zc_ref[:, c0:c0 + CONV_CHUNK] = (bg * conv).astype(BF16)


def _inproj(xp, xs, mod, norm1, w_conv3, w_small, conv_w, q_norm, kv_norm, w_uq_slot, rope_tabs):
    n_prompt = T_P // TM_IN
    const = lambda i: (0, 0)
    pmap, smap = _stream_maps(TM_IN)
    return pl.pallas_call(
        _inproj_kernel,
        grid=(T // TM_IN,),
        in_specs=[
            pl.BlockSpec((TM_IN, D_MODEL), pmap),
            pl.BlockSpec((TM_IN, D_MODEL), smap),
            pl.BlockSpec((N_COND, 6 * D_MODEL), const),
            pl.BlockSpec((1, D_MODEL), const),
            pl.BlockSpec((D_MODEL, 3 * D_CONV), const),
            pl.BlockSpec((D_MODEL, SMALL_COLS), const),
            pl.BlockSpec((3, D_CONV), const),
            pl.BlockSpec((1, Q_LORA), const),
            pl.BlockSpec((1, KV_LORA), const),
            pl.BlockSpec((Q_LORA, N_HEADS * LANE), const),
            pl.BlockSpec((None, 3, TM_IN, LANE),
                         lambda i: (jnp.where(i >= n_prompt, 1, 0), 0, 0, 0)),
        ],
        out_specs=[
            pl.BlockSpec((TM_IN, D_CONV), lambda i: (i, 0)),
            pl.BlockSpec((TM_IN, N_HEADS * LANE), lambda i: (i, 0)),
            pl.BlockSpec((TM_IN, KV_LORA), lambda i: (i, 0)),
            pl.BlockSpec((TM_IN, LANE), lambda i: (i, 0)),
        ],
        out_shape=[
            jax.ShapeDtypeStruct((T, D_CONV), BF16),
            jax.ShapeDtypeStruct((T, N_HEADS * LANE), BF16),
            jax.ShapeDtypeStruct((T, KV_LORA), F32),
            jax.ShapeDtypeStruct((T, LANE), F32),
        ],
        compiler_params=pltpu.CompilerParams(
            dimension_semantics=("parallel",), vmem_limit_bytes=VMEM_LIMIT),
        name="in_proj",
    )(xp, xs, mod, norm1, w_conv3, w_small, conv_w, q_norm, kv_norm, w_uq_slot, rope_tabs)


def _attn_kernel(*refs, n_src):
    q_ref = refs[0]
    ckv_refs = refs[1:1 + n_src]
    krs_refs = refs[1 + n_src:1 + 2 * n_src]
    wukv_ref = refs[1 + 2 * n_src]
    o_ref = refs[2 + 2 * n_src]
    kf_scr, v_scr = refs[3 + 2 * n_src:]

    @pl.when(pl.program_id(1) == 0)
    def _():
        off = 0
        for c_ref, k_ref in zip(ckv_refs, krs_refs):
            m = c_ref.shape[0]
            kv = _dot(c_ref[...].astype(BF16), wukv_ref[...])
            krs = k_ref[...]
            for hh in range(N_HEADS):
                kf_scr[hh, off:off + m, :] = (kv[:, LANE * hh:LANE * (hh + 1)] + krs).astype(BF16)
            v_scr[off:off + m, :] = kv[:, N_HEADS * LANE:].astype(BF16)
            off += m

    for pair in range(N_HEADS // 2):
        acc = None
        for hh in (2 * pair, 2 * pair + 1):
            qh = q_ref[:, LANE * hh:LANE * (hh + 1)]
            s = lax.dot_general(qh, kf_scr[hh], (((1,), (1,)), ((), ())),
                                preferred_element_type=F32) * ATTN_SCALE
            e = jnp.exp(s - jnp.max(s, axis=-1, keepdims=True))
            p = (e / jnp.sum(e, axis=-1, keepdims=True)).astype(BF16)
            part = _dot(p, v_scr[:, LANE * hh:LANE * (hh + 1)])
            acc = part if acc is None else acc + part
        o_ref[:, LANE * pair:LANE * (pair + 1)] = acc.astype(BF16)


def _attention(q, ckv, krs, cache_ckv, cache_krs, w_ukv_slot):
    kv_cols = 2 * N_HEADS * LANE
    cp = pltpu.CompilerParams(dimension_semantics=("parallel", "arbitrary"),
                              vmem_limit_bytes=VMEM_LIMIT)
    o_prompt = pl.pallas_call(
        functools.partial(_attn_kernel, n_src=1),
        grid=(BATCH, SEQ // Q_BLK),
        in_specs=[
            pl.BlockSpec((Q_BLK, N_HEADS * LANE), lambda b, j: (b * (SEQ // Q_BLK) + j, 0)),
            pl.BlockSpec((SEQ, KV_LORA), lambda b, j: (b, 0)),
            pl.BlockSpec((SEQ, LANE), lambda b, j: (b, 0)),
            pl.BlockSpec((KV_LORA, kv_cols), lambda b, j: (0, 0)),
        ],
        out_specs=pl.BlockSpec((Q_BLK, N_HEADS * V_HEAD), lambda b, j: (b * (SEQ // Q_BLK) + j, 0)),
        out_shape=jax.ShapeDtypeStruct((T_P, N_HEADS * V_HEAD), BF16),
        scratch_shapes=[pltpu.VMEM((N_HEADS, SEQ, LANE), BF16),
                        pltpu.VMEM((SEQ, N_HEADS * LANE), BF16)],
        compiler_params=cp,
        name="attn_prompt",
    )(q, ckv, krs, w_ukv_slot)

    m_all = DEC_SEQ + PAST_LEN
    nq = DEC_SEQ // Q_BLK
    q0 = T_P // Q_BLK
    s0 = T_P // DEC_SEQ
    o_sample = pl.pallas_call(
        functools.partial(_attn_kernel, n_src=2),
        grid=(DEC_BATCH, nq),
        in_specs=[
            pl.BlockSpec((Q_BLK, N_HEADS * LANE), lambda b, j: (q0 + b * nq + j, 0)),
            pl.BlockSpec((DEC_SEQ, KV_LORA), lambda b, j: (s0 + b, 0)),
            pl.BlockSpec((None, PAST_LEN, KV_LORA), lambda b, j: (b, 0, 0)),
            pl.BlockSpec((DEC_SEQ, LANE), lambda b, j: (s0 + b, 0)),
            pl.BlockSpec((None, PAST_LEN, LANE), lambda b, j: (b, 0, 0)),
            pl.BlockSpec((KV_LORA, kv_cols), lambda b, j: (0, 0)),
        ],
        out_specs=pl.BlockSpec((Q_BLK, N_HEADS * V_HEAD), lambda b, j: (b * nq + j, 0)),
        out_shape=jax.ShapeDtypeStruct((T_S, N_HEADS * V_HEAD), BF16),
        scratch_shapes=[pltpu.VMEM((N_HEADS, m_all, LANE), BF16),
                        pltpu.VMEM((m_all, N_HEADS * LANE), BF16)],
        compiler_params=cp,
        name="attn_sample",
    )(q, ckv, cache_ckv, krs, cache_krs, w_ukv_slot)
    return jnp.concatenate([o_prompt, o_sample], axis=0)


def _route(logits):
    lane = lax.broadcasted_iota(jnp.int32, logits.shape, 1)
    neg = -jnp.inf
    big = jnp.int32(1 << 20)
    gmask = (lane >= N_EXPERTS) & (lane < N_EXPERTS + N_GROUPS)
    gl = jnp.where(gmask, logits, neg)
    gmax = jnp.max(gl, axis=-1, keepdims=True)
    gsum = jnp.sum(jnp.where(gmask, jnp.exp(gl - gmax), 0.0), axis=-1, keepdims=True)
    p_g = 1.0 / gsum
    g_idx = jnp.min(jnp.where(gl == gmax, lane, big), axis=-1, keepdims=True) - N_EXPERTS

    emask = (lane < N_EXPERTS) & ((lane >> 3) == g_idx)
    el = jnp.where(emask, logits, neg)
    m1 = jnp.max(el, axis=-1, keepdims=True)
    i1 = jnp.min(jnp.where(el == m1, lane, big), axis=-1, keepdims=True)
    el2 = jnp.where(lane == i1, neg, el)
    m2 = jnp.max(el2, axis=-1, keepdims=True)
    i2 = jnp.min(jnp.where(el2 == m2, lane, big), axis=-1, keepdims=True)
    z = jnp.sum(jnp.where(emask, jnp.exp(el - m1), 0.0), axis=-1, keepdims=True)
    p1 = 1.0 / z
    p2 = jnp.exp(m2 - m1) / z
    tot = p1 + p2
    w1 = p_g * p1 / tot
    w2 = p_g * p2 / tot
    return jnp.where(lane == i1, w1, 0.0) + jnp.where(lane == i2, w2, 0.0), g_idx


def _post_kernel(xp_ref, xs_ref, mod_ref, n1_ref, wg_ref, zc_ref, o_ref, wco_ref, wo_ref, wmix_ref,
                 n2_ref, wr_ref, x1_ref, h3_ref, meta_ref, cnt_ref):
    i = pl.program_id(0)
    x = jnp.where(i >= T_P // TM_POST, xs_ref[...], xp_ref[...])
    mod = mod_ref[pl.ds(_mod_row(i, TM_POST), 1), :]
    shift1 = mod[:, 0:D_MODEL]
    scale1 = mod[:, D_MODEL:2 * D_MODEL]
    gate1 = mod[:, 2 * D_MODEL:3 * D_MODEL]
    shift2 = mod[:, 3 * D_MODEL:4 * D_MODEL]
    scale2 = mod[:, 4 * D_MODEL:5 * D_MODEL]
    h = ((x * _rms(x)) * n1_ref[...]) * (1.0 + scale1) + shift1
    g = _dot(h.astype(BF16), wg_ref[...])
    y_conv = _dot(zc_ref[...], wco_ref[...])
    y_mla = _dot(o_ref[...], wo_ref[...])
    merged = (jax.nn.sigmoid(g[:, 0:D_MODEL]) * y_conv
              + jax.nn.sigmoid(g[:, D_MODEL:2 * D_MODEL]) * y_mla)
    y = _dot(merged.astype(BF16), wmix_ref[...])
    x1 = x + gate1 * y
    x1_ref[...] = x1
    h2 = ((x1 * _rms(x1)) * n2_ref[...]) * (1.0 + scale2) + shift2
    h2_hi = h2.astype(BF16)
    h2_lo = (h2 - h2_hi.astype(F32)).astype(BF16)
    hh = _dot(h2_hi, wr_ref[...])
    logits = hh[:, 0:LANE] + hh[:, LANE:2 * LANE] + _dot(h2_lo, wr_ref[:, 0:LANE])
    comb, g_idx = _route(logits)

    lane = lax.broadcasted_iota(jnp.int32, comb.shape, 1)
    onehot = lane == g_idx + N_EXPERTS
    r_i = lax.broadcasted_iota(jnp.int32, (TM_POST, TM_POST), 0)
    c_i = lax.broadcasted_iota(jnp.int32, (TM_POST, TM_POST), 1)
    lower = jnp.where(c_i < r_i, 1.0, 0.0).astype(BF16)
    before = _dot(lower, jnp.where(onehot, 1.0, 0.0).astype(BF16))
    rank = jnp.sum(jnp.where(onehot, before, 0.0), axis=-1, keepdims=True)
    counts = jnp.sum(jnp.where(onehot, 1.0, 0.0), axis=0, keepdims=True)
    cnt_ref[...] = jnp.broadcast_to(counts, cnt_ref.shape)

    meta = (comb + jnp.where(lane == GID_LANE, g_idx.astype(F32), 0.0)
            + jnp.where(lane == RANK_LANE, rank, 0.0))
    meta_ref[...] = meta
    for c in range(N_SLAB):
        h3_ref[pl.ds(c, TM_POST, stride=N_SLAB), :] = h2[:, LANE * c:LANE * (c + 1)]


def _post(xp, xs, mod, norm1, w_gates, zc, o, w_conv_out, w_o, w_mix_out, norm2, w_route):
    const = lambda i: (0, 0)
    row = lambda i: (i, 0)
    pmap, smap = _stream_maps(TM_POST)
    return pl.pallas_call(
        _post_kernel,
        grid=(T // TM_POST,),
        in_specs=[
            pl.BlockSpec((TM_POST, D_MODEL), pmap),
            pl.BlockSpec((TM_POST, D_MODEL), smap),
            pl.BlockSpec((N_COND, 6 * D_MODEL), const),
            pl.BlockSpec((1, D_MODEL), const),
            pl.BlockSpec((D_MODEL, 2 * D_MODEL), const),
            pl.BlockSpec((TM_POST, D_CONV), row),
            pl.BlockSpec((TM_POST, N_HEADS * V_HEAD), row),
            pl.BlockSpec((D_CONV, D_MODEL), const),
            pl.BlockSpec((N_HEADS * V_HEAD, D_MODEL), const),
            pl.BlockSpec((D_MODEL, D_MODEL), const),
            pl.BlockSpec((1, D_MODEL), const),
            pl.BlockSpec((D_MODEL, 2 * LANE), const),
        ],
        out_specs=[
            pl.BlockSpec((TM_POST, D_MODEL), row),
            pl.BlockSpec((TM_POST * N_SLAB, LANE), row),
            pl.BlockSpec((TM_POST, LANE), row),
            pl.BlockSpec((None, 8, LANE), lambda i: (i, 0, 0)),
        ],
        out_shape=[
            jax.ShapeDtypeStruct((T, D_MODEL), F32),
            jax.ShapeDtypeStruct((T * N_SLAB, LANE), F32),
            jax.ShapeDtypeStruct((T, LANE), F32),
            jax.ShapeDtypeStruct((T // TM_POST, 8, LANE), F32),
        ],
        compiler_params=pltpu.CompilerParams(
            dimension_semantics=("parallel",), vmem_limit_bytes=VMEM_LIMIT),
        name="post_mixer",
    )(xp, xs, mod, norm1, w_gates, zc, o, w_conv_out, w_o, w_mix_out, norm2, w_route)


def _dispatch_kernel(pos_ref, h3_ref, m_ref, hs_ref, ms_ref, src_ref, xg_ref):
    i = pl.program_id(0)

    @pl.when(i == 0)
    def _():
        def invert(t, carry):
            src_ref[pos_ref[t]] = t
            return carry

        lax.fori_loop(0, T, invert, 0, unroll=8)

    base = i * TM_MOE

    def body(r, carry):
        tok = src_ref[base + r]
        xg_ref[_slab(r), :] = h3_ref[_slab(tok), :]
        ms_ref[pl.ds(r, 1), :] = m_ref[pl.ds(tok, 1), :]
        return carry

    lax.fori_loop(0, TM_MOE, body, 0, unroll=8)
    for c in range(N_SLAB):
        hs_ref[:, LANE * c:LANE * (c + 1)] = xg_ref[pl.ds(c, TM_MOE, stride=N_SLAB), :].astype(BF16)


def _dispatch(pos, h3, meta):
    n_slab = N_SLAB
    return pl.pallas_call(
        _dispatch_kernel,
        grid_spec=pltpu.PrefetchScalarGridSpec(
            num_scalar_prefetch=1,
            grid=(T // TM_MOE,),
            in_specs=[pl.BlockSpec((T * n_slab, LANE), lambda i, pos: (0, 0),
                                   pipeline_mode=pl.Buffered(1)),
                      pl.BlockSpec((T, LANE), lambda i, pos: (0, 0), pipeline_mode=pl.Buffered(1))],
            out_specs=[pl.BlockSpec((TM_MOE, D_MODEL), lambda i, pos: (i, 0)),
                       pl.BlockSpec((TM_MOE, LANE), lambda i, pos: (i, 0))],
            scratch_shapes=[pltpu.SMEM((T,), jnp.int32),
                            pltpu.VMEM((TM_MOE * n_slab, LANE), F32)],
        ),
        out_shape=[jax.ShapeDtypeStruct((T, D_MODEL), BF16),
                   jax.ShapeDtypeStruct((T, LANE), F32)],
        compiler_params=pltpu.CompilerParams(
            dimension_semantics=("arbitrary",), vmem_limit_bytes=VMEM_LIMIT),
        name="moe_dispatch",
    )(pos, h3, meta)


def _moe_kernel(vt_ref, vg_ref, vlo_ref, vhi_ref, vfirst_ref, vlast_ref, vvalid_ref,
                hs_ref, ms_ref, wup_ref, wgate_ref, wdown_ref, y3_ref, acc_ref):
    v = pl.program_id(0)
    j = pl.program_id(1)
    valid = vvalid_ref[v] == 1
    lo = vlo_ref[v]
    hi = vhi_ref[v]
    e = vg_ref[v] * EXP_PER_GROUP + j
    full = (lo == 0) & (hi == TM_MOE)

    @pl.when(valid & (j == 0) & (vfirst_ref[v] == 1))
    def _():
        acc_ref[...] = jnp.zeros_like(acc_ref)

    def expert_rows(r0, rows):
        w_in2 = jnp.concatenate([wup_ref[...].astype(BF16), wgate_ref[...].astype(BF16)], axis=1)
        ag = _dot(hs_ref[r0:r0 + rows, :], w_in2)
        a = ag[:, 0:D_EXPERT]
        g = ag[:, D_EXPERT:]
        comb = ms_ref[r0:r0 + rows, :]
        lane = lax.broadcasted_iota(jnp.int32, comb.shape, 1)
        cw = jnp.sum(jnp.where(lane == e, comb, 0.0), axis=-1, keepdims=True)
        act = (g * jax.nn.sigmoid(g)) * a * cw
        acc_ref[r0:r0 + rows, :] += _dot(act.astype(BF16), wdown_ref[...].astype(BF16))

    @pl.when(valid & full)
    def _():
        expert_rows(0, TM_MOE)

    @pl.when(valid & jnp.logical_not(full))
    def _():
        for s in range(TM_MOE // MOE_SUB):
            r0 = s * MOE_SUB

            @pl.when((lo < r0 + MOE_SUB) & (hi > r0))
            def _():
                expert_rows(r0, MOE_SUB)

    @pl.when(valid & (j == EXP_PER_GROUP - 1) & (vlast_ref[v] == 1))
    def _():
        for c in range(N_SLAB):
            y3_ref[pl.ds(c, TM_MOE, stride=N_SLAB), :] = acc_ref[:, LANE * c:LANE * (c + 1)]


def _moe(sched, hs, ms, w_up, w_gate, w_down):
    wmap = lambda v, j, vt, vg, vlo, vhi, vfirst, vlast, vvalid: (
        vg[v] * EXP_PER_GROUP + jnp.where(vvalid[v] == 1, j, EXP_PER_GROUP - 1), 0, 0)
    tmap = lambda v, j, vt, *_: (vt[v], 0)
    n_slab = D_MODEL // LANE
    return pl.pallas_call(
        _moe_kernel,
        grid_spec=pltpu.PrefetchScalarGridSpec(
            num_scalar_prefetch=7,
            grid=(N_VISITS, EXP_PER_GROUP),
            in_specs=[
                pl.BlockSpec((TM_MOE, D_MODEL), tmap),
                pl.BlockSpec((TM_MOE, LANE), tmap),
                pl.BlockSpec((None, D_MODEL, D_EXPERT), wmap),
                pl.BlockSpec((None, D_MODEL, D_EXPERT), wmap),
                pl.BlockSpec((None, D_EXPERT, D_MODEL), wmap),
            ],
            out_specs=pl.BlockSpec((TM_MOE * n_slab, LANE), tmap),
            scratch_shapes=[pltpu.VMEM((TM_MOE, D_MODEL), F32)],
        ),
        out_shape=jax.ShapeDtypeStruct((T * n_slab, LANE), F32),
        compiler_params=pltpu.CompilerParams(
            dimension_semantics=("arbitrary", "arbitrary"), vmem_limit_bytes=VMEM_LIMIT),
        name="moe_grouped",
    )(*sched, hs, ms, w_up, w_gate, w_down)


def _final_kernel(pos_ref, ys_ref, x1_ref, mod_ref, fn_ref, yp_ref, ysm_ref, g_ref):
    i = pl.program_id(0)
    base = i * TM_FINAL

    def body(r, carry):
        g_ref[_slab(r), :] = ys_ref[_slab(pos_ref[base + r]), :]
        return carry

    lax.fori_loop(0, TM_FINAL, body, 0, unroll=8)
    mod = mod_ref[pl.ds(_mod_row(i, TM_FINAL), 1), :]
    gate2 = mod[:, 5 * D_MODEL:6 * D_MODEL]
    moe = jnp.concatenate([g_ref[pl.ds(c, TM_FINAL, stride=N_SLAB), :] for c in range(N_SLAB)], axis=1)
    x2 = x1_ref[...] + gate2 * moe
    y = (x2 * _rms(x2)) * fn_ref[...]
    is_sample = i >= T_P // TM_FINAL

    @pl.when(jnp.logical_not(is_sample))
    def _():
        yp_ref[...] = y

    @pl.when(is_sample)
    def _():
        ysm_ref[...] = y


def _final(pos, ys, x1, mod, final_norm):
    n_slab = D_MODEL // LANE
    pmap, smap = _stream_maps(TM_FINAL)
    return pl.pallas_call(
        _final_kernel,
        grid_spec=pltpu.PrefetchScalarGridSpec(
            num_scalar_prefetch=1,
            grid=(T // TM_FINAL,),
            in_specs=[
                pl.BlockSpec((T * n_slab, LANE), lambda i, pos: (0, 0), pipeline_mode=pl.Buffered(1)),
                pl.BlockSpec((TM_FINAL, D_MODEL), lambda i, pos: (i, 0)),
                pl.BlockSpec((N_COND, 6 * D_MODEL), lambda i, pos: (0, 0)),
                pl.BlockSpec((1, D_MODEL), lambda i, pos: (0, 0)),
            ],
            out_specs=[pl.BlockSpec((TM_FINAL, D_MODEL), pmap),
                       pl.BlockSpec((TM_FINAL, D_MODEL), smap)],
            scratch_shapes=[pltpu.VMEM((TM_FINAL * n_slab, LANE), F32)],
        ),
        out_shape=[jax.ShapeDtypeStruct((T_P, D_MODEL), F32),
                   jax.ShapeDtypeStruct((T_S, D_MODEL), F32)],
        compiler_params=pltpu.CompilerParams(
            dimension_semantics=("arbitrary",), vmem_limit_bytes=VMEM_LIMIT),
        name="moe_unsort_final",
    )(pos, ys, x1, mod, final_norm)


def _schedule(meta, cnt):
    n_tiles_post = T // TM_POST
    counts = cnt[:, 0, N_EXPERTS:N_EXPERTS + N_GROUPS].astype(jnp.int32)
    gtot = jnp.sum(counts, axis=0)
    goff = jnp.cumsum(gtot) - gtot
    tile_base = goff[None, :] + jnp.cumsum(counts, axis=0) - counts
    gid = meta[:, GID_LANE].astype(jnp.int32).reshape(n_tiles_post, TM_POST)
    rank = meta[:, RANK_LANE].astype(jnp.int32).reshape(n_tiles_post, TM_POST)
    pos = rank
    for grp in range(N_GROUPS):
        pos = pos + jnp.where(gid == grp, tile_base[:, grp:grp + 1], 0)
    pos = pos.reshape(T)

    n_tiles = T // TM_MOE
    t_lo = (jnp.arange(n_tiles, dtype=jnp.int32) * TM_MOE)[:, None]
    lo = jnp.clip(goff[None, :] - t_lo, 0, TM_MOE)
    hi = jnp.clip(goff[None, :] + gtot[None, :] - t_lo, 0, TM_MOE)
    ok = (hi > lo).reshape(-1)
    slot = jnp.cumsum(ok.astype(jnp.int32)) - 1
    n_ok = slot[-1] + 1
    sel = (slot[None, :] == jnp.arange(N_VISITS, dtype=jnp.int32)[:, None]) & ok[None, :]

    def pick(vals):
        return jnp.sum(jnp.where(sel, vals.reshape(-1)[None, :], 0), axis=1).astype(jnp.int32)

    pair_tile = jnp.broadcast_to(jnp.arange(n_tiles, dtype=jnp.int32)[:, None], (n_tiles, N_GROUPS))
    pair_group = jnp.broadcast_to(jnp.arange(N_GROUPS, dtype=jnp.int32)[None, :], (n_tiles, N_GROUPS))
    vt, vg, vlo, vhi = pick(pair_tile), pick(pair_group), pick(lo), pick(hi)
    valid = jnp.arange(N_VISITS, dtype=jnp.int32) < n_ok
    last = jnp.maximum(n_ok - 1, 0)
    vt = jnp.where(valid, vt, vt[last])
    vg = jnp.where(valid, vg, vg[last])
    change = (vt[1:] != vt[:-1]).astype(jnp.int32)
    one = jnp.ones((1,), jnp.int32)
    first = jnp.concatenate([one, change])
    idx = jnp.arange(N_VISITS, dtype=jnp.int32)
    last = jnp.where(idx == n_ok - 1, 1, jnp.concatenate([change, one]))
    return pos, (vt, vg, vlo, vhi, first, last, valid.astype(jnp.int32))


def _rope_tables():
    n = np.arange(DEC_SEQ)
    pos = np.stack([n // GRID_W, n % GRID_W], axis=1).astype(np.float32)
    half = ROPE_AXIS // 2
    inv = (1.0 / (ROPE_BASE ** (np.arange(0, ROPE_AXIS, 2, dtype=np.float32) / ROPE_AXIS))).astype(np.float32)
    ang = (pos[:, :, None] * inv[None, None, :]).astype(np.float32)
    cos = np.cos(ang).astype(np.float32)
    sin = np.sin(ang).astype(np.float32)
    tabs = np.zeros((2, 3, DEC_SEQ, LANE), np.float32)
    tabs[:, 0] = 1.0
    for a in range(2):
        lo = ROPE_LANE0 + a * ROPE_AXIS
        tabs[1, 0, :, lo:lo + half] = cos[:, a]
        tabs[1, 0, :, lo + half:lo + 2 * half] = cos[:, a]
        tabs[1, 1, :, lo + half:lo + 2 * half] = sin[:, a]
        tabs[1, 2, :, lo:lo + half] = -sin[:, a]
    return jnp.asarray(tabs)


def kernel(x_prompt, x_sample, cache_ckv, cache_krope, c, c_ctx, norm1, w_ada, b_ada, w_in, conv_w,
           w_conv_out, q_norm, w_uq, kv_norm, w_ukv, w_o, w_mix_out, norm2, w_grp, w_exp, w_up,
           w_gate, w_down, final_norm):
    l = 0
    xp = x_prompt.reshape(T_P, D_MODEL)
    xs = x_sample.reshape(T_S, D_MODEL)
    cond = jnp.concatenate(
        [c_ctx[None, :], c, jnp.zeros((N_COND - 1 - DEC_BATCH, D_MODEL), F32)], axis=0)
    mod = _ada(cond, w_ada[l], b_ada[l][None, :])

    w_conv3, w_small, w_gates, w_co_b, w_o_b, w_mix_b = _prep_weights(
        w_in[l], w_conv_out[l], w_o[l], w_mix_out[l])
    w_uq_slot = jnp.pad(w_uq[l].reshape(Q_LORA, N_HEADS, QK_NOPE + QK_ROPE),
                        ((0, 0), (0, 0), (0, LANE - QK_NOPE - QK_ROPE))
                        ).reshape(Q_LORA, N_HEADS * LANE).astype(BF16)
    wkv = w_ukv[l].reshape(KV_LORA, N_HEADS, QK_NOPE + V_HEAD)
    wk_slot = jnp.pad(wkv[:, :, :QK_NOPE], ((0, 0), (0, 0), (0, LANE - QK_NOPE)))
    wv = wkv[:, :, QK_NOPE:].reshape(KV_LORA, N_HEADS // 2, 2, V_HEAD)
    zero = jnp.zeros_like(wv[:, :, 0])
    wv_slot = jnp.stack([jnp.concatenate([wv[:, :, 0], zero], axis=-1),
                         jnp.concatenate([zero, wv[:, :, 1]], axis=-1)], axis=2)
    w_ukv_slot = jnp.concatenate([wk_slot.reshape(KV_LORA, N_HEADS * LANE),
                                  wv_slot.reshape(KV_LORA, N_HEADS * LANE)], axis=1).astype(BF16)
    w_route = jnp.pad(jnp.concatenate([w_exp[l], w_grp[l]], axis=1),
                      ((0, 0), (0, LANE - N_EXPERTS - N_GROUPS)))
    w_route_hi = w_route.astype(BF16)
    w_route_lo = (w_route - w_route_hi.astype(F32)).astype(BF16)
    w_route2 = jnp.concatenate([w_route_hi, w_route_lo], axis=1)
    cache_krs = jnp.pad(cache_krope[:, l], ((0, 0), (0, 0), (ROPE_LANE0, LANE - ROPE_LANE0 - QK_ROPE)))

    zc, q, ckv, krs = _inproj(xp, xs, mod, norm1[l][None, :], w_conv3, w_small, conv_w[l],
                              q_norm[l][None, :], kv_norm[l][None, :], w_uq_slot, _rope_tables())
    o = _attention(q, ckv, krs, cache_ckv[:, l], cache_krs, w_ukv_slot)
    x1, h3, meta, cnt = _post(xp, xs, mod, norm1[l][None, :], w_gates, zc, o, w_co_b, w_o_b, w_mix_b,
                              norm2[l][None, :], w_route2)
    pos, sched = _schedule(meta, cnt)
    hs, ms = _dispatch(pos, h3, meta)
    ys = _moe(sched, hs, ms, w_up[l], w_gate[l], w_down[l])
    yp, ysm = _final(pos, ys, x1, mod, final_norm[None, :])

    y_prompt = yp.reshape(BATCH, SEQ, D_MODEL)
    y_sample = ysm.reshape(DEC_BATCH, DEC_SEQ, D_MODEL)
    new_ckv = ckv[:T_P].reshape(BATCH, 1, SEQ, KV_LORA)
    new_krope = krs[:T_P, ROPE_LANE0:ROPE_LANE0 + QK_ROPE].reshape(BATCH, 1, SEQ, QK_ROPE)
    return (y_prompt, y_sample, new_ckv, new_krope)
```

```python
import functools

import numpy as np
import jax
import jax.numpy as jnp
from jax import lax
from jax.experimental import pallas as pl
from jax.experimental.pallas import tpu as pltpu

F32 = jnp.float32
BF16 = jnp.bfloat16

D_MODEL = 1024
BATCH = 16
SEQ = 256
DEC_BATCH = 2
DEC_SEQ = 1024
PAST_LEN = 256
GRID_W = 64
N_HEADS = 8
QK_NOPE = 64
QK_ROPE = 32
V_HEAD = 64
Q_LORA = 256
KV_LORA = 128
ROPE_AXIS = QK_ROPE // 2
ROPE_BASE = 10000.0
ATTN_SCALE = (QK_NOPE + QK_ROPE) ** -0.5
D_CONV = D_MODEL
N_GROUPS = 4
EXP_PER_GROUP = 8
N_EXPERTS = N_GROUPS * EXP_PER_GROUP
D_EXPERT = 256
EPS = 1e-6

T_P = BATCH * SEQ
T_S = DEC_BATCH * DEC_SEQ
T = T_P + T_S
N_COND = 8
LANE = 128
ROPE_LANE0 = QK_NOPE
SMALL_COLS = Q_LORA + KV_LORA + LANE
VMEM_LIMIT = 56 * 1024 * 1024

TM_IN = 1024
TM_POST = 512
TM_MOE = 1024
TM_FINAL = 512
N_SLAB = D_MODEL // LANE
MOE_SUB = 256
N_VISITS = T // TM_MOE + N_GROUPS - 1
GID_LANE = 40
RANK_LANE = 41
Q_BLK = 256
CONV_CHUNK = 512


def _dot(a, b):
    return jnp.dot(a, b, preferred_element_type=F32)


def _rms(x):
    return lax.rsqrt(jnp.mean(x * x, axis=-1, keepdims=True) + EPS)


def _slab(t):
    return pl.ds(pl.multiple_of(t * N_SLAB, N_SLAB), N_SLAB)


def _mod_row(i, tm):
    n_prompt = T_P // tm
    return jnp.where(i >= n_prompt, 1 + ((i - n_prompt) * tm) // DEC_SEQ, 0)


def _ada_kernel(cond_ref, w_ref, b_ref, o_ref):
    c = cond_ref[...]
    a = (c * jax.nn.sigmoid(c)).astype(BF16)
    o_ref[...] = _dot(a, w_ref[...].astype(BF16)) + b_ref[...]


def _ada(cond, w_ada, b_ada):
    n = 6 * D_MODEL
    bn = 1536
    return pl.pallas_call(
        _ada_kernel,
        grid=(n // bn,),
        in_specs=[
            pl.BlockSpec((N_COND, D_MODEL), lambda j: (0, 0)),
            pl.BlockSpec((D_MODEL, bn), lambda j: (0, j)),
            pl.BlockSpec((1, bn), lambda j: (0, j)),
        ],
        out_specs=pl.BlockSpec((N_COND, bn), lambda j: (0, j)),
        out_shape=jax.ShapeDtypeStruct((N_COND, n), F32),
        compiler_params=pltpu.CompilerParams(
            dimension_semantics=("parallel",), vmem_limit_bytes=VMEM_LIMIT),
        name="ada_mod",
    )(cond, w_ada, b_ada)


O_CQ = 3 * D_CONV
O_KR = O_CQ + Q_LORA + KV_LORA
O_GATE = O_KR + QK_ROPE
PREP_STEPS = 4


IN_COLS = O_GATE + 2 * D_MODEL
TAIL_ROWS = IN_COLS // 2
SMALL_BLK = 512
NT = (((1,), (1,)), ((), ()))


def _dot_nt(a, bt):
    return lax.dot_general(a, bt, NT, preferred_element_type=F32)


def _prep_kernel(wct_ref, wst_ref, wtail_ref, wco_ref, wo_ref, wmix_ref,
                 c3_ref, sm_ref, g_ref, co_ref, o_ref, mix_ref):
    c3_ref[...] = wct_ref[...].astype(BF16)
    n_lat = Q_LORA + KV_LORA
    sm_ref[0:n_lat, :] = wst_ref[0:n_lat, :].astype(BF16)
    sm_ref[n_lat:, :] = jnp.zeros((LANE, sm_ref.shape[1]), BF16)
    sm_ref[n_lat + ROPE_LANE0:n_lat + ROPE_LANE0 + QK_ROPE, :] = (
        wst_ref[n_lat:n_lat + QK_ROPE, :].astype(BF16))
    g_ref[...] = wtail_ref[O_GATE - TAIL_ROWS:, :].astype(BF16)
    co_ref[...] = wco_ref[...].astype(BF16)
    o_ref[...] = wo_ref[...].astype(BF16)
    mix_ref[...] = wmix_ref[...].astype(BF16)


def _prep_weights(w_in_t, w_conv_out, w_o, w_mix_out):
    cb = D_MODEL // PREP_STEPS
    col = lambda rows, blk=0: pl.BlockSpec((rows, cb), lambda i: (blk, i))
    n_o = N_HEADS * V_HEAD
    return pl.pallas_call(
        _prep_kernel,
        grid=(PREP_STEPS,),
        in_specs=[col(O_CQ), col(SMALL_BLK, O_CQ // SMALL_BLK), col(TAIL_ROWS, 1),
                  col(D_CONV), col(n_o), col(D_MODEL)],
        out_specs=[col(O_CQ), col(SMALL_COLS), col(2 * D_MODEL), col(D_CONV), col(n_o), col(D_MODEL)],
        out_shape=[jax.ShapeDtypeStruct((O_CQ, D_MODEL), BF16),
                   jax.ShapeDtypeStruct((SMALL_COLS, D_MODEL), BF16),
                   jax.ShapeDtypeStruct((2 * D_MODEL, D_MODEL), BF16),
                   jax.ShapeDtypeStruct((D_CONV, D_MODEL), BF16),
                   jax.ShapeDtypeStruct((n_o, D_MODEL), BF16),
                   jax.ShapeDtypeStruct((D_MODEL, D_MODEL), BF16)],
        compiler_params=pltpu.CompilerParams(
            dimension_semantics=("parallel",), vmem_limit_bytes=VMEM_LIMIT),
        name="weight_casts",
    )(w_in_t, w_in_t, w_in_t, w_conv_out, w_o, w_mix_out)


def _stream_maps(tm):
    n_prompt = T_P // tm
    return (lambda i, *_: (jnp.minimum(i, n_prompt - 1), 0),
            lambda i, *_: (jnp.maximum(i - n_prompt, 0), 0))


def _inproj_kernel(xp_ref, xs_ref, mod_ref, n1_ref, wc_ref, ws_ref, cw_ref, qn_ref, kvn_ref, wuq_ref,
                   rope_ref, zc_ref, q_ref, ckv_ref, krs_ref):
    i = pl.program_id(0)
    is_sample = i >= T_P // TM_IN
    seq = jnp.where(is_sample, DEC_SEQ, SEQ)
    mod = mod_ref[pl.ds(_mod_row(i, TM_IN), 1), :]
    shift1 = mod[:, 0:D_MODEL]
    scale1 = mod[:, D_MODEL:2 * D_MODEL]
    x = jnp.where(is_sample, xs_ref[...], xp_ref[...])
    h = ((x * _rms(x)) * n1_ref[...]) * (1.0 + scale1) + shift1
    hb = h.astype(BF16)

    sm = _dot_nt(hb, ws_ref[...])
    cq = sm[:, 0:Q_LORA]
    ckv_raw = sm[:, Q_LORA:Q_LORA + KV_LORA]
    krs = sm[:, Q_LORA + KV_LORA:SMALL_COLS]
    cqn = (cq * _rms(cq)) * qn_ref[...]
    q = _dot(cqn.astype(BF16), wuq_ref[...])
    ckv_ref[...] = (ckv_raw * _rms(ckv_raw)) * kvn_ref[...]

    cos = rope_ref[0]
    sin_lo = rope_ref[1]
    sin_hi = rope_ref[2]

    def rot(v):
        return v * cos + pltpu.roll(v, 8, 1) * sin_lo + pltpu.roll(v, LANE - 8, 1) * sin_hi

    krs_ref[...] = rot(krs)
    for hh in range(N_HEADS):
        q_ref[:, LANE * hh:LANE * (hh + 1)] = rot(q[:, LANE * hh:LANE * (hh + 1)]).astype(BF16)

    pos = lax.broadcasted_iota(jnp.int32, (TM_IN, 1), 0) & (seq - 1)
    first = pos == 0
    last = pos == seq - 1
    for j in range(D_CONV // CONV_CHUNK):
        c0 = j * CONV_CHUNK
        bg = _dot_nt(hb, wc_ref[c0:c0 + CONV_CHUNK, :])
        cg = _dot_nt(hb, wc_ref[D_CONV + c0:D_CONV + c0 + CONV_CHUNK, :])
        ui = _dot_nt(hb, wc_ref[2 * D_CONV + c0:2 * D_CONV + c0 + CONV_CHUNK, :])
        u = cg * ui
        u_prev = jnp.where(first, 0.0, pltpu.roll(u, 1, 0))
        u_next = jnp.where(last, 0.0, pltpu.roll(u, TM_IN - 1, 0))
        cw = cw_ref[:, c0:c0 + CONV_CHUNK]
        conv = u_prev * cw[0:1] + u * cw[1:2] + u_next * cw[2:3]
        zc_ref[:, c0:c0 + CONV_CHUNK] = (bg * conv).astype(BF16)


def _inproj(xp, xs, mod, norm1, w_conv3_t, w_small_t, conv_w, q_norm, kv_norm, w_uq_slot, rope_tabs):
    n_prompt = T_P // TM_IN
    const = lambda i: (0, 0)
    pmap, smap = _stream_maps(TM_IN)
    return pl.pallas_call(
        _inproj_kernel,
        grid=(T // TM_IN,),
        in_specs=[
            pl.BlockSpec((TM_IN, D_MODEL), pmap),
            pl.BlockSpec((TM_IN, D_MODEL), smap),
            pl.BlockSpec((N_COND, 6 * D_MODEL), const),
            pl.BlockSpec((1, D_MODEL), const),
            pl.BlockSpec((O_CQ, D_MODEL), const),
            pl.BlockSpec((SMALL_COLS, D_MODEL), const),
            pl.BlockSpec((3, D_CONV), const),
            pl.BlockSpec((1, Q_LORA), const),
            pl.BlockSpec((1, KV_LORA), const),
            pl.BlockSpec((Q_LORA, N_HEADS * LANE), const),
            pl.BlockSpec((None, 3, TM_IN, LANE),
                         lambda i: (jnp.where(i >= n_prompt, 1, 0), 0, 0, 0)),
        ],
        out_specs=[
            pl.BlockSpec((TM_IN, D_CONV), lambda i: (i, 0)),
            pl.BlockSpec((TM_IN, N_HEADS * LANE), lambda i: (i, 0)),
            pl.BlockSpec((TM_IN, KV_LORA), lambda i: (i, 0)),
            pl.BlockSpec((TM_IN, LANE), lambda i: (i, 0)),
        ],
        out_shape=[
            jax.ShapeDtypeStruct((T, D_CONV), BF16),
            jax.ShapeDtypeStruct((T, N_HEADS * LANE), BF16),
            jax.ShapeDtypeStruct((T, KV_LORA), F32),
            jax.ShapeDtypeStruct((T, LANE), F32),
        ],
        compiler_params=pltpu.CompilerParams(
            dimension_semantics=("parallel",), vmem_limit_bytes=VMEM_LIMIT),
        name="in_proj",
    )(xp, xs, mod, norm1, w_conv3_t, w_small_t, conv_w, q_norm, kv_norm, w_uq_slot, rope_tabs)


def _attn_kernel(*refs, n_src):
    q_ref = refs[0]
    ckv_refs = refs[1:1 + n_src]
    krs_refs = refs[1 + n_src:1 + 2 * n_src]
    wukv_ref = refs[1 + 2 * n_src]
    o_ref = refs[2 + 2 * n_src]
    kf_scr, v_scr = refs[3 + 2 * n_src:]

    @pl.when(pl.program_id(1) == 0)
    def _():
        off = 0
        for c_ref, k_ref in zip(ckv_refs, krs_refs):
            m = c_ref.shape[0]
            kv = _dot(c_ref[...].astype(BF16), wukv_ref[...])
            krs = k_ref[...]
            for hh in range(N_HEADS):
                kf_scr[hh, off:off + m, :] = (kv[:, LANE * hh:LANE * (hh + 1)] + krs).astype(BF16)
            v_scr[off:off + m, :] = kv[:, N_HEADS * LANE:].astype(BF16)
            off += m

    for pair in range(N_HEADS // 2):
        acc = None
        for hh in (2 * pair, 2 * pair + 1):
            qh = q_ref[:, LANE * hh:LANE * (hh + 1)]
            s = lax.dot_general(qh, kf_scr[hh], (((1,), (1,)), ((), ())),
                                preferred_element_type=F32) * ATTN_SCALE
            e = jnp.exp(s - jnp.max(s, axis=-1, keepdims=True))
            p = (e / jnp.sum(e, axis=-1, keepdims=True)).astype(BF16)
            part = _dot(p, v_scr[:, LANE * hh:LANE * (hh + 1)])
            acc = part if acc is None else acc + part
        o_ref[:, LANE * pair:LANE * (pair + 1)] = acc.astype(BF16)


def _attention(q, ckv, krs, cache_ckv, cache_krs, w_ukv_slot):
    kv_cols = 2 * N_HEADS * LANE
    cp = pltpu.CompilerParams(dimension_semantics=("parallel", "arbitrary"),
                              vmem_limit_bytes=VMEM_LIMIT)
    o_prompt = pl.pallas_call(
        functools.partial(_attn_kernel, n_src=1),
        grid=(BATCH, SEQ // Q_BLK),
        in_specs=[
            pl.BlockSpec((Q_BLK, N_HEADS * LANE), lambda b, j: (b * (SEQ // Q_BLK) + j, 0)),
            pl.BlockSpec((SEQ, KV_LORA), lambda b, j: (b, 0)),
            pl.BlockSpec((SEQ, LANE), lambda b, j: (b, 0)),
            pl.BlockSpec((KV_LORA, kv_cols), lambda b, j: (0, 0)),
        ],
        out_specs=pl.BlockSpec((Q_BLK, N_HEADS * V_HEAD), lambda b, j: (b * (SEQ // Q_BLK) + j, 0)),
        out_shape=jax.ShapeDtypeStruct((T_P, N_HEADS * V_HEAD), BF16),
        scratch_shapes=[pltpu.VMEM((N_HEADS, SEQ, LANE), BF16),
                        pltpu.VMEM((SEQ, N_HEADS * LANE), BF16)],
        compiler_params=cp,
        name="attn_prompt",
    )(q, ckv, krs, w_ukv_slot)

    m_all = DEC_SEQ + PAST_LEN
    nq = DEC_SEQ // Q_BLK
    q0 = T_P // Q_BLK
    s0 = T_P // DEC_SEQ
    o_sample = pl.pallas_call(
        functools.partial(_attn_kernel, n_src=2),
        grid=(DEC_BATCH, nq),
        in_specs=[
            pl.BlockSpec((Q_BLK, N_HEADS * LANE), lambda b, j: (q0 + b * nq + j, 0)),
            pl.BlockSpec((DEC_SEQ, KV_LORA), lambda b, j: (s0 + b, 0)),
            pl.BlockSpec((None, PAST_LEN, KV_LORA), lambda b, j: (b, 0, 0)),
            pl.BlockSpec((DEC_SEQ, LANE), lambda b, j: (s0 + b, 0)),
            pl.BlockSpec((None, PAST_LEN, LANE), lambda b, j: (b, 0, 0)),
            pl.BlockSpec((KV_LORA, kv_cols), lambda b, j: (0, 0)),
        ],
        out_specs=pl.BlockSpec((Q_BLK, N_HEADS * V_HEAD), lambda b, j: (b * nq + j, 0)),
        out_shape=jax.ShapeDtypeStruct((T_S, N_HEADS * V_HEAD), BF16),
        scratch_shapes=[pltpu.VMEM((N_HEADS, m_all, LANE), BF16),
                        pltpu.VMEM((m_all, N_HEADS * LANE), BF16)],
        compiler_params=cp,
        name="attn_sample",
    )(q, ckv, cache_ckv, krs, cache_krs, w_ukv_slot)
    return jnp.concatenate([o_prompt, o_sample], axis=0)


def _route(logits):
    lane = lax.broadcasted_iota(jnp.int32, logits.shape, 1)
    neg = -jnp.inf
    big = jnp.int32(1 << 20)
    gmask = (lane >= N_EXPERTS) & (lane < N_EXPERTS + N_GROUPS)
    gl = jnp.where(gmask, logits, neg)
    gmax = jnp.max(gl, axis=-1, keepdims=True)
    gsum = jnp.sum(jnp.where(gmask, jnp.exp(gl - gmax), 0.0), axis=-1, keepdims=True)
    p_g = 1.0 / gsum
    g_idx = jnp.min(jnp.where(gl == gmax, lane, big), axis=-1, keepdims=True) - N_EXPERTS

    emask = (lane < N_EXPERTS) & ((lane >> 3) == g_idx)
    el = jnp.where(emask, logits, neg)
    m1 = jnp.max(el, axis=-1, keepdims=True)
    i1 = jnp.min(jnp.where(el == m1, lane, big), axis=-1, keepdims=True)
    el2 = jnp.where(lane == i1, neg, el)
    m2 = jnp.max(el2, axis=-1, keepdims=True)
    i2 = jnp.min(jnp.where(el2 == m2, lane, big), axis=-1, keepdims=True)
    z = jnp.sum(jnp.where(emask, jnp.exp(el - m1), 0.0), axis=-1, keepdims=True)
    p1 = 1.0 / z
    p2 = jnp.exp(m2 - m1) / z
    tot = p1 + p2
    w1 = p_g * p1 / tot
    w2 = p_g * p2 / tot
    return jnp.where(lane == i1, w1, 0.0) + jnp.where(lane == i2, w2, 0.0), g_idx


def _post_kernel(xp_ref, xs_ref, mod_ref, n1_ref, wg_ref, zc_ref, o_ref, wco_ref, wo_ref, wmix_ref,
                 n2_ref, wr_ref, x1_ref, h3_ref, meta_ref, cnt_ref):
    i = pl.program_id(0)
    x = jnp.where(i >= T_P // TM_POST, xs_ref[...], xp_ref[...])
    mod = mod_ref[pl.ds(_mod_row(i, TM_POST), 1), :]
    shift1 = mod[:, 0:D_MODEL]
    scale1 = mod[:, D_MODEL:2 * D_MODEL]
    gate1 = mod[:, 2 * D_MODEL:3 * D_MODEL]
    shift2 = mod[:, 3 * D_MODEL:4 * D_MODEL]
    scale2 = mod[:, 4 * D_MODEL:5 * D_MODEL]
    h = ((x * _rms(x)) * n1_ref[...]) * (1.0 + scale1) + shift1
    g = _dot_nt(h.astype(BF16), wg_ref[...])
    y_conv = _dot(zc_ref[...], wco_ref[...])
    y_mla = _dot(o_ref[...], wo_ref[...])
    merged = (jax.nn.sigmoid(g[:, 0:D_MODEL]) * y_conv
              + jax.nn.sigmoid(g[:, D_MODEL:2 * D_MODEL]) * y_mla)
    y = _dot(merged.astype(BF16), wmix_ref[...])
    x1 = x + gate1 * y
    x1_ref[...] = x1
    h2 = ((x1 * _rms(x1)) * n2_ref[...]) * (1.0 + scale2) + shift2
    h2_hi = h2.astype(BF16)
    h2_lo = (h2 - h2_hi.astype(F32)).astype(BF16)
    hh = _dot(h2_hi, wr_ref[...])
    logits = hh[:, 0:LANE] + hh[:, LANE:2 * LANE] + _dot(h2_lo, wr_ref[:, 0:LANE])
    comb, g_idx = _route(logits)

    lane = lax.broadcasted_iota(jnp.int32, comb.shape, 1)
    onehot = lane == g_idx + N_EXPERTS
    r_i = lax.broadcasted_iota(jnp.int32, (TM_POST, TM_POST), 0)
    c_i = lax.broadcasted_iota(jnp.int32, (TM_POST, TM_POST), 1)
    lower = jnp.where(c_i < r_i, 1.0, 0.0).astype(BF16)
    before = _dot(lower, jnp.where(onehot, 1.0, 0.0).astype(BF16))
    rank = jnp.sum(jnp.where(onehot, before, 0.0), axis=-1, keepdims=True)
    counts = jnp.sum(jnp.where(onehot, 1.0, 0.0), axis=0, keepdims=True)
    cnt_ref[...] = jnp.broadcast_to(counts, cnt_ref.shape)

    meta = (comb + jnp.where(lane == GID_LANE, g_idx.astype(F32), 0.0)
            + jnp.where(lane == RANK_LANE, rank, 0.0))
    meta_ref[...] = meta
    for c in range(N_SLAB):
        h3_ref[pl.ds(c, TM_POST, stride=N_SLAB), :] = h2[:, LANE * c:LANE * (c + 1)]


def _post(xp, xs, mod, norm1, w_gates_t, zc, o, w_conv_out, w_o, w_mix_out, norm2, w_route):
    const = lambda i: (0, 0)
    row = lambda i: (i, 0)
    pmap, smap = _stream_maps(TM_POST)
    return pl.pallas_call(
        _post_kernel,
        grid=(T // TM_POST,),
        in_specs=[
            pl.BlockSpec((TM_POST, D_MODEL), pmap),
            pl.BlockSpec((TM_POST, D_MODEL), smap),
            pl.BlockSpec((N_COND, 6 * D_MODEL), const),
            pl.BlockSpec((1, D_MODEL), const),
            pl.BlockSpec((2 * D_MODEL, D_MODEL), const),
            pl.BlockSpec((TM_POST, D_CONV), row),
            pl.BlockSpec((TM_POST, N_HEADS * V_HEAD), row),
            pl.BlockSpec((D_CONV, D_MODEL), const),
            pl.BlockSpec((N_HEADS * V_HEAD, D_MODEL), const),
            pl.BlockSpec((D_MODEL, D_MODEL), const),
            pl.BlockSpec((1, D_MODEL), const),
            pl.BlockSpec((D_MODEL, 2 * LANE), const),
        ],
        out_specs=[
            pl.BlockSpec((TM_POST, D_MODEL), row),
            pl.BlockSpec((TM_POST * N_SLAB, LANE), row),
            pl.BlockSpec((TM_POST, LANE), row),
            pl.BlockSpec((None, 8, LANE), lambda i: (i, 0, 0)),
        ],
        out_shape=[
            jax.ShapeDtypeStruct((T, D_MODEL), F32),
            jax.ShapeDtypeStruct((T * N_SLAB, LANE), F32),
            jax.ShapeDtypeStruct((T, LANE), F32),
            jax.ShapeDtypeStruct((T // TM_POST, 8, LANE), F32),
        ],
        compiler_params=pltpu.CompilerParams(
            dimension_semantics=("parallel",), vmem_limit_bytes=VMEM_LIMIT),
        name="post_mixer",
    )(xp, xs, mod, norm1, w_gates_t, zc, o, w_conv_out, w_o, w_mix_out, norm2, w_route)


def _dispatch_kernel(pos_ref, h3_ref, m_ref, hs_ref, ms_ref, src_ref, xg_ref):
    i = pl.program_id(0)

    @pl.when(i == 0)
    def _():
        def invert(t, carry):
            src_ref[pos_ref[t]] = t
            return carry

        lax.fori_loop(0, T, invert, 0, unroll=8)

    base = i * TM_MOE

    def body(r, carry):
        tok = src_ref[base + r]
        xg_ref[_slab(r), :] = h3_ref[_slab(tok), :]
        ms_ref[pl.ds(r, 1), :] = m_ref[pl.ds(tok, 1), :]
        return carry

    lax.fori_loop(0, TM_MOE, body, 0, unroll=8)
    for c in range(N_SLAB):
        hs_ref[:, LANE * c:LANE * (c + 1)] = xg_ref[pl.ds(c, TM_MOE, stride=N_SLAB), :].astype(BF16)


def _dispatch(pos, h3, meta):
    n_slab = N_SLAB
    return pl.pallas_call(
        _dispatch_kernel,
        grid_spec=pltpu.PrefetchScalarGridSpec(
            num_scalar_prefetch=1,
            grid=(T // TM_MOE,),
            in_specs=[pl.BlockSpec((T * n_slab, LANE), lambda i, pos: (0, 0),
                                   pipeline_mode=pl.Buffered(1)),
                      pl.BlockSpec((T, LANE), lambda i, pos: (0, 0), pipeline_mode=pl.Buffered(1))],
            out_specs=[pl.BlockSpec((TM_MOE, D_MODEL), lambda i, pos: (i, 0)),
                       pl.BlockSpec((TM_MOE, LANE), lambda i, pos: (i, 0))],
            scratch_shapes=[pltpu.SMEM((T,), jnp.int32),
                            pltpu.VMEM((TM_MOE * n_slab, LANE), F32)],
        ),
        out_shape=[jax.ShapeDtypeStruct((T, D_MODEL), BF16),
                   jax.ShapeDtypeStruct((T, LANE), F32)],
        compiler_params=pltpu.CompilerParams(
            dimension_semantics=("arbitrary",), vmem_limit_bytes=VMEM_LIMIT),
        name="moe_dispatch",
    )(pos, h3, meta)


def _moe_kernel(vt_ref, vg_ref, vlo_ref, vhi_ref, vfirst_ref, vlast_ref, vvalid_ref,
                hs_ref, ms_ref, wup_ref, wgate_ref, wdown_ref, y3_ref, acc_ref):
    v = pl.program_id(0)
    j = pl.program_id(1)
    valid = vvalid_ref[v] == 1
    lo = vlo_ref[v]
    hi = vhi_ref[v]
    e = vg_ref[v] * EXP_PER_GROUP + j
    full = (lo == 0) & (hi == TM_MOE)

    @pl.when(valid & (j == 0) & (vfirst_ref[v] == 1))
    def _():
        acc_ref[...] = jnp.zeros_like(acc_ref)

    def expert_rows(r0, rows):
        w_in2 = jnp.concatenate([wup_ref[...].astype(BF16), wgate_ref[...].astype(BF16)], axis=1)
        ag = _dot(hs_ref[r0:r0 + rows, :], w_in2)
        a = ag[:, 0:D_EXPERT]
        g = ag[:, D_EXPERT:]
        comb = ms_ref[r0:r0 + rows, :]
        lane = lax.broadcasted_iota(jnp.int32, comb.shape, 1)
        cw = jnp.sum(jnp.where(lane == e, comb, 0.0), axis=-1, keepdims=True)
        act = (g * jax.nn.sigmoid(g)) * a * cw
        acc_ref[r0:r0 + rows, :] += _dot(act.astype(BF16), wdown_ref[...].astype(BF16))

    @pl.when(valid & full)
    def _():
        expert_rows(0, TM_MOE)

    @pl.when(valid & jnp.logical_not(full))
    def _():
        for s in range(TM_MOE // MOE_SUB):
            r0 = s * MOE_SUB

            @pl.when((lo < r0 + MOE_SUB) & (hi > r0))
            def _():
                expert_rows(r0, MOE_SUB)

    @pl.when(valid & (j == EXP_PER_GROUP - 1) & (vlast_ref[v] == 1))
    def _():
        for c in range(N_SLAB):
            y3_ref[pl.ds(c, TM_MOE, stride=N_SLAB), :] = acc_ref[:, LANE * c:LANE * (c + 1)]


def _moe(sched, hs, ms, w_up, w_gate, w_down):
    wmap = lambda v, j, vt, vg, vlo, vhi, vfirst, vlast, vvalid: (
        vg[v] * EXP_PER_GROUP + jnp.where(vvalid[v] == 1, j, EXP_PER_GROUP - 1), 0, 0)
    tmap = lambda v, j, vt, *_: (vt[v], 0)
    n_slab = D_MODEL // LANE
    return pl.pallas_call(
        _moe_kernel,
        grid_spec=pltpu.PrefetchScalarGridSpec(
            num_scalar_prefetch=7,
            grid=(N_VISITS, EXP_PER_GROUP),
            in_specs=[
                pl.BlockSpec((TM_MOE, D_MODEL), tmap),
                pl.BlockSpec((TM_MOE, LANE), tmap),
                pl.BlockSpec((None, D_MODEL, D_EXPERT), wmap),
                pl.BlockSpec((None, D_MODEL, D_EXPERT), wmap),
                pl.BlockSpec((None, D_EXPERT, D_MODEL), wmap),
            ],
            out_specs=pl.BlockSpec((TM_MOE * n_slab, LANE), tmap),
            scratch_shapes=[pltpu.VMEM((TM_MOE, D_MODEL), F32)],
        ),
        out_shape=jax.ShapeDtypeStruct((T * n_slab, LANE), F32),
        compiler_params=pltpu.CompilerParams(
            dimension_semantics=("arbitrary", "arbitrary"), vmem_limit_bytes=VMEM_LIMIT),
        name="moe_grouped",
    )(*sched, hs, ms, w_up, w_gate, w_down)


def _final_kernel(pos_ref, ys_ref, x1_ref, mod_ref, fn_ref, yp_ref, ysm_ref, g_ref):
    i = pl.program_id(0)
    base = i * TM_FINAL

    def body(r, carry):
        g_ref[_slab(r), :] = ys_ref[_slab(pos_ref[base + r]), :]
        return carry

    lax.fori_loop(0, TM_FINAL, body, 0, unroll=8)
    mod = mod_ref[pl.ds(_mod_row(i, TM_FINAL), 1), :]
    gate2 = mod[:, 5 * D_MODEL:6 * D_MODEL]
    moe = jnp.concatenate([g_ref[pl.ds(c, TM_FINAL, stride=N_SLAB), :] for c in range(N_SLAB)], axis=1)
    x2 = x1_ref[...] + gate2 * moe
    y = (x2 * _rms(x2)) * fn_ref[...]
    is_sample = i >= T_P // TM_FINAL

    @pl.when(jnp.logical_not(is_sample))
    def _():
        yp_ref[...] = y

    @pl.when(is_sample)
    def _():
        ysm_ref[...] = y


def _final(pos, ys, x1, mod, final_norm):
    n_slab = D_MODEL // LANE
    pmap, smap = _stream_maps(TM_FINAL)
    return pl.pallas_call(
        _final_kernel,
        grid_spec=pltpu.PrefetchScalarGridSpec(
            num_scalar_prefetch=1,
            grid=(T // TM_FINAL,),
            in_specs=[
                pl.BlockSpec((T * n_slab, LANE), lambda i, pos: (0, 0), pipeline_mode=pl.Buffered(1)),
                pl.BlockSpec((TM_FINAL, D_MODEL), lambda i, pos: (i, 0)),
                pl.BlockSpec((N_COND, 6 * D_MODEL), lambda i, pos: (0, 0)),
                pl.BlockSpec((1, D_MODEL), lambda i, pos: (0, 0)),
            ],
            out_specs=[pl.BlockSpec((TM_FINAL, D_MODEL), pmap),
                       pl.BlockSpec((TM_FINAL, D_MODEL), smap)],
            scratch_shapes=[pltpu.VMEM((TM_FINAL * n_slab, LANE), F32)],
        ),
        out_shape=[jax.ShapeDtypeStruct((T_P, D_MODEL), F32),
                   jax.ShapeDtypeStruct((T_S, D_MODEL), F32)],
        compiler_params=pltpu.CompilerParams(
            dimension_semantics=("arbitrary",), vmem_limit_bytes=VMEM_LIMIT),
        name="moe_unsort_final",
    )(pos, ys, x1, mod, final_norm)


def _schedule(meta, cnt):
    n_tiles_post = T // TM_POST
    counts = cnt[:, 0, N_EXPERTS:N_EXPERTS + N_GROUPS].astype(jnp.int32)
    gtot = jnp.sum(counts, axis=0)
    goff = jnp.cumsum(gtot) - gtot
    tile_base = goff[None, :] + jnp.cumsum(counts, axis=0) - counts
    gid = meta[:, GID_LANE].astype(jnp.int32).reshape(n_tiles_post, TM_POST)
    rank = meta[:, RANK_LANE].astype(jnp.int32).reshape(n_tiles_post, TM_POST)
    pos = rank
    for grp in range(N_GROUPS):
        pos = pos + jnp.where(gid == grp, tile_base[:, grp:grp + 1], 0)
    pos = pos.reshape(T)

    n_tiles = T // TM_MOE
    t_lo = (jnp.arange(n_tiles, dtype=jnp.int32) * TM_MOE)[:, None]
    lo = jnp.clip(goff[None, :] - t_lo, 0, TM_MOE)
    hi = jnp.clip(goff[None, :] + gtot[None, :] - t_lo, 0, TM_MOE)
    ok = (hi > lo).reshape(-1)
    slot = jnp.cumsum(ok.astype(jnp.int32)) - 1
    n_ok = slot[-1] + 1
    sel = (slot[None, :] == jnp.arange(N_VISITS, dtype=jnp.int32)[:, None]) & ok[None, :]

    def pick(vals):
        return jnp.sum(jnp.where(sel, vals.reshape(-1)[None, :], 0), axis=1).astype(jnp.int32)

    pair_tile = jnp.broadcast_to(jnp.arange(n_tiles, dtype=jnp.int32)[:, None], (n_tiles, N_GROUPS))
    pair_group = jnp.broadcast_to(jnp.arange(N_GROUPS, dtype=jnp.int32)[None, :], (n_tiles, N_GROUPS))
    vt, vg, vlo, vhi = pick(pair_tile), pick(pair_group), pick(lo), pick(hi)
    valid = jnp.arange(N_VISITS, dtype=jnp.int32) < n_ok
    last = jnp.maximum(n_ok - 1, 0)
    vt = jnp.where(valid, vt, vt[last])
    vg = jnp.where(valid, vg, vg[last])
    change = (vt[1:] != vt[:-1]).astype(jnp.int32)
    one = jnp.ones((1,), jnp.int32)
    first = jnp.concatenate([one, change])
    idx = jnp.arange(N_VISITS, dtype=jnp.int32)
    last = jnp.where(idx == n_ok - 1, 1, jnp.concatenate([change, one]))
    return pos, (vt, vg, vlo, vhi, first, last, valid.astype(jnp.int32))


def _rope_tables():
    n = np.arange(DEC_SEQ)
    pos = np.stack([n // GRID_W, n % GRID_W], axis=1).astype(np.float32)
    half = ROPE_AXIS // 2
    inv = (1.0 / (ROPE_BASE ** (np.arange(0, ROPE_AXIS, 2, dtype=np.float32) / ROPE_AXIS))).astype(np.float32)
    ang = (pos[:, :, None] * inv[None, None, :]).astype(np.float32)
    cos = np.cos(ang).astype(np.float32)
    sin = np.sin(ang).astype(np.float32)
    tabs = np.zeros((2, 3, DEC_SEQ, LANE), np.float32)
    tabs[:, 0] = 1.0
    for a in range(2):
        lo = ROPE_LANE0 + a * ROPE_AXIS
        tabs[1, 0, :, lo:lo + half] = cos[:, a]
        tabs[1, 0, :, lo + half:lo + 2 * half] = cos[:, a]
        tabs[1, 1, :, lo + half:lo + 2 * half] = sin[:, a]
        tabs[1, 2, :, lo:lo + half] = -sin[:, a]
    return jnp.asarray(tabs)


def kernel(x_prompt, x_sample, cache_ckv, cache_krope, c, c_ctx, norm1, w_ada, b_ada, w_in, conv_w,
           w_conv_out, q_norm, w_uq, kv_norm, w_ukv, w_o, w_mix_out, norm2, w_grp, w_exp, w_up,
           w_gate, w_down, final_norm):
    l = 0
    xp = x_prompt.reshape(T_P, D_MODEL)
    xs = x_sample.reshape(T_S, D_MODEL)
    cond = jnp.concatenate(
        [c_ctx[None, :], c, jnp.zeros((N_COND - 1 - DEC_BATCH, D_MODEL), F32)], axis=0)
    mod = _ada(cond, w_ada[l], b_ada[l][None, :])

    w_in_t = w_in[l].T
    w_conv3_t, w_small_t, w_gates_t, w_co_b, w_o_b, w_mix_b = _prep_weights(
        w_in_t, w_conv_out[l], w_o[l], w_mix_out[l])
    w_uq_slot = jnp.pad(w_uq[l].reshape(Q_LORA, N_HEADS, QK_NOPE + QK_ROPE),
                        ((0, 0), (0, 0), (0, LANE - QK_NOPE - QK_ROPE))
                        ).reshape(Q_LORA, N_HEADS * LANE).astype(BF16)
    wkv = w_ukv[l].reshape(KV_LORA, N_HEADS, QK_NOPE + V_HEAD)
    wk_slot = jnp.pad(wkv[:, :, :QK_NOPE], ((0, 0), (0, 0), (0, LANE - QK_NOPE)))
    wv = wkv[:, :, QK_NOPE:].reshape(KV_LORA, N_HEADS // 2, 2, V_HEAD)
    zero = jnp.zeros_like(wv[:, :, 0])
    wv_slot = jnp.stack([jnp.concatenate([wv[:, :, 0], zero], axis=-1),
                         jnp.concatenate([zero, wv[:, :, 1]], axis=-1)], axis=2)
    w_ukv_slot = jnp.concatenate([wk_slot.reshape(KV_LORA, N_HEADS * LANE),
                                  wv_slot.reshape(KV_LORA, N_HEADS * LANE)], axis=1).astype(BF16)
    w_route = jnp.pad(jnp.concatenate([w_exp[l], w_grp[l]], axis=1),
                      ((0, 0), (0, LANE - N_EXPERTS - N_GROUPS)))
    w_route_hi = w_route.astype(BF16)
    w_route_lo = (w_route - w_route_hi.astype(F32)).astype(BF16)
    w_route2 = jnp.concatenate([w_route_hi, w_route_lo], axis=1)
    cache_krs = jnp.pad(cache_krope[:, l], ((0, 0), (0, 0), (ROPE_LANE0, LANE - ROPE_LANE0 - QK_ROPE)))

    zc, q, ckv, krs = _inproj(xp, xs, mod, norm1[l][None, :], w_conv3_t, w_small_t, conv_w[l],
                              q_norm[l][None, :], kv_norm[l][None, :], w_uq_slot, _rope_tables())
    o = _attention(q, ckv, krs, cache_ckv[:, l], cache_krs, w_ukv_slot)
    x1, h3, meta, cnt = _post(xp, xs, mod, norm1[l][None, :], w_gates_t, zc, o, w_co_b, w_o_b, w_mix_b,
                              norm2[l][None, :], w_route2)
    pos, sched = _schedule(meta, cnt)
    hs, ms = _dispatch(pos, h3, meta)
    ys = _moe(sched, hs, ms, w_up[l], w_gate[l], w_down[l])
    yp, ysm = _final(pos, ys, x1, mod, final_norm[None, :])

    y_prompt = yp.reshape(BATCH, SEQ, D_MODEL)
    y_sample = ysm.reshape(DEC_BATCH, DEC_SEQ, D_MODEL)
    new_ckv = ckv[:T_P].reshape(BATCH, 1, SEQ, KV_LORA)
    new_krope = krs[:T_P, ROPE_LANE0:ROPE_LANE0 + QK_ROPE].reshape(BATCH, 1, SEQ, QK_ROPE)
    return (y_prompt, y_sample, new_ckv, new_krope)
```

```python
import functools

import numpy as np
import jax
import jax.numpy as jnp
from jax import lax
from jax.experimental import pallas as pl
from jax.experimental.pallas import tpu as pltpu

F32 = jnp.float32
BF16 = jnp.bfloat16

D_MODEL = 1024
BATCH = 16
SEQ = 256
DEC_BATCH = 2
DEC_SEQ = 1024
PAST_LEN = 256
GRID_W = 64
N_HEADS = 8
QK_NOPE = 64
QK_ROPE = 32
V_HEAD = 64
Q_LORA = 256
KV_LORA = 128
ROPE_AXIS = QK_ROPE // 2
ROPE_BASE = 10000.0
ATTN_SCALE = (QK_NOPE + QK_ROPE) ** -0.5
D_CONV = D_MODEL
N_GROUPS = 4
EXP_PER_GROUP = 8
N_EXPERTS = N_GROUPS * EXP_PER_GROUP
D_EXPERT = 256
EPS = 1e-6

T_P = BATCH * SEQ
T_S = DEC_BATCH * DEC_SEQ
T = T_P + T_S
N_COND = 8
LANE = 128
ROPE_LANE0 = QK_NOPE
SMALL_COLS = Q_LORA + KV_LORA + LANE
VMEM_LIMIT = 56 * 1024 * 1024

TM_IN = 1024
TM_POST = 512
TM_MOE = T // N_GROUPS
TM_FINAL = 512
N_SLAB = D_MODEL // LANE
MOE_SUB = 256
N_VISITS = T // TM_MOE + N_GROUPS - 1
GID_LANE = 40
RANK_LANE = 41
Q_BLK = 256
CONV_CHUNK = 512


def _dot(a, b):
    return jnp.dot(a, b, preferred_element_type=F32)


def _rms(x):
    return lax.rsqrt(jnp.mean(x * x, axis=-1, keepdims=True) + EPS)


def _slab(t):
    return pl.ds(pl.multiple_of(t * N_SLAB, N_SLAB), N_SLAB)


def _mod_row(i, tm):
    n_prompt = T_P // tm
    return jnp.where(i >= n_prompt, 1 + ((i - n_prompt) * tm) // DEC_SEQ, 0)


def _ada_kernel(cond_ref, w_ref, b_ref, o_ref):
    c = cond_ref[...]
    a = (c * jax.nn.sigmoid(c)).astype(BF16)
    o_ref[...] = _dot(a, w_ref[...].astype(BF16)) + b_ref[...]


def _ada(cond, w_ada, b_ada):
    n = 6 * D_MODEL
    bn = 1536
    return pl.pallas_call(
        _ada_kernel,
        grid=(n // bn,),
        in_specs=[
            pl.BlockSpec((N_COND, D_MODEL), lambda j: (0, 0)),
            pl.BlockSpec((D_MODEL, bn), lambda j: (0, j)),
            pl.BlockSpec((1, bn), lambda j: (0, j)),
        ],
        out_specs=pl.BlockSpec((N_COND, bn), lambda j: (0, j)),
        out_shape=jax.ShapeDtypeStruct((N_COND, n), F32),
        compiler_params=pltpu.CompilerParams(
            dimension_semantics=("parallel",), vmem_limit_bytes=VMEM_LIMIT),
        name="ada_mod",
    )(cond, w_ada, b_ada)


O_CQ = 3 * D_CONV
O_KR = O_CQ + Q_LORA + KV_LORA
O_GATE = O_KR + QK_ROPE
PREP_STEPS = 4


IN_COLS = O_GATE + 2 * D_MODEL
TAIL_ROWS = IN_COLS // 2
SMALL_BLK = 512
NT = (((1,), (1,)), ((), ()))


def _dot_nt(a, bt):
    return lax.dot_general(a, bt, NT, preferred_element_type=F32)


def _prep_kernel(wct_ref, wst_ref, wtail_ref, wco_ref, wo_ref, wmix_ref,
                 c3_ref, sm_ref, g_ref, co_ref, o_ref, mix_ref):
    c3_ref[...] = wct_ref[...].astype(BF16)
    n_lat = Q_LORA + KV_LORA
    sm_ref[0:n_lat, :] = wst_ref[0:n_lat, :].astype(BF16)
    sm_ref[n_lat:, :] = jnp.zeros((LANE, sm_ref.shape[1]), BF16)
    sm_ref[n_lat + ROPE_LANE0:n_lat + ROPE_LANE0 + QK_ROPE, :] = (
        wst_ref[n_lat:n_lat + QK_ROPE, :].astype(BF16))
    g_ref[...] = wtail_ref[O_GATE - TAIL_ROWS:, :].astype(BF16)
    co_ref[...] = wco_ref[...].astype(BF16)
    o_ref[...] = wo_ref[...].astype(BF16)
    mix_ref[...] = wmix_ref[...].astype(BF16)


def _prep_weights(w_in_t, w_conv_out, w_o, w_mix_out):
    cb = D_MODEL // PREP_STEPS
    col = lambda rows, blk=0: pl.BlockSpec((rows, cb), lambda i: (blk, i))
    n_o = N_HEADS * V_HEAD
    return pl.pallas_call(
        _prep_kernel,
        grid=(PREP_STEPS,),
        in_specs=[col(O_CQ), col(SMALL_BLK, O_CQ // SMALL_BLK), col(TAIL_ROWS, 1),
                  col(D_CONV), col(n_o), col(D_MODEL)],
        out_specs=[col(O_CQ), col(SMALL_COLS), col(2 * D_MODEL), col(D_CONV), col(n_o), col(D_MODEL)],
        out_shape=[jax.ShapeDtypeStruct((O_CQ, D_MODEL), BF16),
                   jax.ShapeDtypeStruct((SMALL_COLS, D_MODEL), BF16),
                   jax.ShapeDtypeStruct((2 * D_MODEL, D_MODEL), BF16),
                   jax.ShapeDtypeStruct((D_CONV, D_MODEL), BF16),
                   jax.ShapeDtypeStruct((n_o, D_MODEL), BF16),
                   jax.ShapeDtypeStruct((D_MODEL, D_MODEL), BF16)],
        compiler_params=pltpu.CompilerParams(
            dimension_semantics=("parallel",), vmem_limit_bytes=VMEM_LIMIT),
        name="weight_casts",
    )(w_in_t, w_in_t, w_in_t, w_conv_out, w_o, w_mix_out)


def _stream_maps(tm):
    n_prompt = T_P // tm
    return (lambda i, *_: (jnp.minimum(i, n_prompt - 1), 0),
            lambda i, *_: (jnp.maximum(i - n_prompt, 0), 0))


def _inproj_kernel(xp_ref, xs_ref, mod_ref, n1_ref, wc_ref, ws_ref, cw_ref, qn_ref, kvn_ref, wuq_ref,
                   rope_ref, zc_ref, q_ref, ckv_ref, krs_ref):
    i = pl.program_id(0)
    is_sample = i >= T_P // TM_IN
    seq = jnp.where(is_sample, DEC_SEQ, SEQ)
    mod = mod_ref[pl.ds(_mod_row(i, TM_IN), 1), :]
    shift1 = mod[:, 0:D_MODEL]
    scale1 = mod[:, D_MODEL:2 * D_MODEL]
    x = jnp.where(is_sample, xs_ref[...], xp_ref[...])
    h = ((x * _rms(x)) * n1_ref[...]) * (1.0 + scale1) + shift1
    hb = h.astype(BF16)

    sm = _dot_nt(hb, ws_ref[...])
    cq = sm[:, 0:Q_LORA]
    ckv_raw = sm[:, Q_LORA:Q_LORA + KV_LORA]
    krs = sm[:, Q_LORA + KV_LORA:SMALL_COLS]
    cqn = (cq * _rms(cq)) * qn_ref[...]
    q = _dot(cqn.astype(BF16), wuq_ref[...])
    ckv_ref[...] = (ckv_raw * _rms(ckv_raw)) * kvn_ref[...]

    cos = rope_ref[0]
    sin_lo = rope_ref[1]
    sin_hi = rope_ref[2]

    def rot(v):
        return v * cos + pltpu.roll(v, 8, 1) * sin_lo + pltpu.roll(v, LANE - 8, 1) * sin_hi

    krs_ref[...] = rot(krs)
    for hh in range(N_HEADS):
        q_ref[:, LANE * hh:LANE * (hh + 1)] = rot(q[:, LANE * hh:LANE * (hh + 1)]).astype(BF16)

    pos = lax.broadcasted_iota(jnp.int32, (TM_IN, 1), 0) & (seq - 1)
    first = pos == 0
    last = pos == seq - 1
    for j in range(D_CONV // CONV_CHUNK):
        c0 = j * CONV_CHUNK
        bg = _dot_nt(hb, wc_ref[c0:c0 + CONV_CHUNK, :])
        cg = _dot_nt(hb, wc_ref[D_CONV + c0:D_CONV + c0 + CONV_CHUNK, :])
        ui = _dot_nt(hb, wc_ref[2 * D_CONV + c0:2 * D_CONV + c0 + CONV_CHUNK, :])
        u = cg * ui
        u_prev = jnp.where(first, 0.0, pltpu.roll(u, 1, 0))
        u_next = jnp.where(last, 0.0, pltpu.roll(u, TM_IN - 1, 0))
        cw = cw_ref[:, c0:c0 + CONV_CHUNK]
        conv = u_prev * cw[0:1] + u * cw[1:2] + u_next * cw[2:3]
        zc_ref[:, c0:c0 + CONV_CHUNK] = (bg * conv).astype(BF16)


def _inproj(xp, xs, mod, norm1, w_conv3_t, w_small_t, conv_w, q_norm, kv_norm, w_uq_slot, rope_tabs):
    n_prompt = T_P // TM_IN
    const = lambda i: (0, 0)
    pmap, smap = _stream_maps(TM_IN)
    return pl.pallas_call(
        _inproj_kernel,
        grid=(T // TM_IN,),
        in_specs=[
            pl.BlockSpec((TM_IN, D_MODEL), pmap),
            pl.BlockSpec((TM_IN, D_MODEL), smap),
            pl.BlockSpec((N_COND, 6 * D_MODEL), const),
            pl.BlockSpec((1, D_MODEL), const),
            pl.BlockSpec((O_CQ, D_MODEL), const),
            pl.BlockSpec((SMALL_COLS, D_MODEL), const),
            pl.BlockSpec((3, D_CONV), const),
            pl.BlockSpec((1, Q_LORA), const),
            pl.BlockSpec((1, KV_LORA), const),
            pl.BlockSpec((Q_LORA, N_HEADS * LANE), const),
            pl.BlockSpec((None, 3, TM_IN, LANE),
                         lambda i: (jnp.where(i >= n_prompt, 1, 0), 0, 0, 0)),
        ],
        out_specs=[
            pl.BlockSpec((TM_IN, D_CONV), lambda i: (i, 0)),
            pl.BlockSpec((TM_IN, N_HEADS * LANE), lambda i: (i, 0)),
            pl.BlockSpec((TM_IN, KV_LORA), lambda i: (i, 0)),
            pl.BlockSpec((TM_IN, LANE), lambda i: (i, 0)),
        ],
        out_shape=[
            jax.ShapeDtypeStruct((T, D_CONV), BF16),
            jax.ShapeDtypeStruct((T, N_HEADS * LANE), BF16),
            jax.ShapeDtypeStruct((T, KV_LORA), F32),
            jax.ShapeDtypeStruct((T, LANE), F32),
        ],
        compiler_params=pltpu.CompilerParams(
            dimension_semantics=("parallel",), vmem_limit_bytes=VMEM_LIMIT),
        name="in_proj",
    )(xp, xs, mod, norm1, w_conv3_t, w_small_t, conv_w, q_norm, kv_norm, w_uq_slot, rope_tabs)


def _attn_kernel(*refs, n_src):
    q_ref = refs[0]
    ckv_refs = refs[1:1 + n_src]
    krs_refs = refs[1 + n_src:1 + 2 * n_src]
    wukv_ref = refs[1 + 2 * n_src]
    o_ref = refs[2 + 2 * n_src]
    kf_scr, v_scr = refs[3 + 2 * n_src:]

    @pl.when(pl.program_id(1) == 0)
    def _():
        off = 0
        for c_ref, k_ref in zip(ckv_refs, krs_refs):
            m = c_ref.shape[0]
            kv = _dot(c_ref[...].astype(BF16), wukv_ref[...])
            krs = k_ref[...]
            for hh in range(N_HEADS):
                kf_scr[hh, off:off + m, :] = (kv[:, LANE * hh:LANE * (hh + 1)] + krs).astype(BF16)
            v_scr[off:off + m, :] = kv[:, N_HEADS * LANE:].astype(BF16)
            off += m

    for pair in range(N_HEADS // 2):
        acc = None
        for hh in (2 * pair, 2 * pair + 1):
            qh = q_ref[:, LANE * hh:LANE * (hh + 1)]
            s = lax.dot_general(qh, kf_scr[hh], (((1,), (1,)), ((), ())),
                                preferred_element_type=F32) * ATTN_SCALE
            e = jnp.exp(s - jnp.max(s, axis=-1, keepdims=True))
            p = (e / jnp.sum(e, axis=-1, keepdims=True)).astype(BF16)
            part = _dot(p, v_scr[:, LANE * hh:LANE * (hh + 1)])
            acc = part if acc is None else acc + part
        o_ref[:, LANE * pair:LANE * (pair + 1)] = acc.astype(BF16)


def _attention(q, ckv, krs, cache_ckv, cache_krs, w_ukv_slot):
    kv_cols = 2 * N_HEADS * LANE
    cp = pltpu.CompilerParams(dimension_semantics=("parallel", "arbitrary"),
                              vmem_limit_bytes=VMEM_LIMIT)
    o_prompt = pl.pallas_call(
        functools.partial(_attn_kernel, n_src=1),
        grid=(BATCH, SEQ // Q_BLK),
        in_specs=[
            pl.BlockSpec((Q_BLK, N_HEADS * LANE), lambda b, j: (b * (SEQ // Q_BLK) + j, 0)),
            pl.BlockSpec((SEQ, KV_LORA), lambda b, j: (b, 0)),
            pl.BlockSpec((SEQ, LANE), lambda b, j: (b, 0)),
            pl.BlockSpec((KV_LORA, kv_cols), lambda b, j: (0, 0)),
        ],
        out_specs=pl.BlockSpec((Q_BLK, N_HEADS * V_HEAD), lambda b, j: (b * (SEQ // Q_BLK) + j, 0)),
        out_shape=jax.ShapeDtypeStruct((T_P, N_HEADS * V_HEAD), BF16),
        scratch_shapes=[pltpu.VMEM((N_HEADS, SEQ, LANE), BF16),
                        pltpu.VMEM((SEQ, N_HEADS * LANE), BF16)],
        compiler_params=cp,
        name="attn_prompt",
    )(q, ckv, krs, w_ukv_slot)

    m_all = DEC_SEQ + PAST_LEN
    nq = DEC_SEQ // Q_BLK
    q0 = T_P // Q_BLK
    s0 = T_P // DEC_SEQ
    o_sample = pl.pallas_call(
        functools.partial(_attn_kernel, n_src=2),
        grid=(DEC_BATCH, nq),
        in_specs=[
            pl.BlockSpec((Q_BLK, N_HEADS * LANE), lambda b, j: (q0 + b * nq + j, 0)),
            pl.BlockSpec((DEC_SEQ, KV_LORA), lambda b, j: (s0 + b, 0)),
            pl.BlockSpec((None, PAST_LEN, KV_LORA), lambda b, j: (b, 0, 0)),
            pl.BlockSpec((DEC_SEQ, LANE), lambda b, j: (s0 + b, 0)),
            pl.BlockSpec((None, PAST_LEN, LANE), lambda b, j: (b, 0, 0)),
            pl.BlockSpec((KV_LORA, kv_cols), lambda b, j: (0, 0)),
        ],
        out_specs=pl.BlockSpec((Q_BLK, N_HEADS * V_HEAD), lambda b, j: (b * nq + j, 0)),
        out_shape=jax.ShapeDtypeStruct((T_S, N_HEADS * V_HEAD), BF16),
        scratch_shapes=[pltpu.VMEM((N_HEADS, m_all, LANE), BF16),
                        pltpu.VMEM((m_all, N_HEADS * LANE), BF16)],
        compiler_params=cp,
        name="attn_sample",
    )(q, ckv, cache_ckv, krs, cache_krs, w_ukv_slot)
    return jnp.concatenate([o_prompt, o_sample], axis=0)


def _route(logits):
    lane = lax.broadcasted_iota(jnp.int32, logits.shape, 1)
    neg = -jnp.inf
    big = jnp.int32(1 << 20)
    gmask = (lane >= N_EXPERTS) & (lane < N_EXPERTS + N_GROUPS)
    gl = jnp.where(gmask, logits, neg)
    gmax = jnp.max(gl, axis=-1, keepdims=True)
    gsum = jnp.sum(jnp.where(gmask, jnp.exp(gl - gmax), 0.0), axis=-1, keepdims=True)
    p_g = 1.0 / gsum
    g_idx = jnp.min(jnp.where(gl == gmax, lane, big), axis=-1, keepdims=True) - N_EXPERTS

    emask = (lane < N_EXPERTS) & ((lane >> 3) == g_idx)
    el = jnp.where(emask, logits, neg)
    m1 = jnp.max(el, axis=-1, keepdims=True)
    i1 = jnp.min(jnp.where(el == m1, lane, big), axis=-1, keepdims=True)
    el2 = jnp.where(lane == i1, neg, el)
    m2 = jnp.max(el2, axis=-1, keepdims=True)
    i2 = jnp.min(jnp.where(el2 == m2, lane, big), axis=-1, keepdims=True)
    z = jnp.sum(jnp.where(emask, jnp.exp(el - m1), 0.0), axis=-1, keepdims=True)
    p1 = 1.0 / z
    p2 = jnp.exp(m2 - m1) / z
    tot = p1 + p2
    w1 = p_g * p1 / tot
    w2 = p_g * p2 / tot
    return jnp.where(lane == i1, w1, 0.0) + jnp.where(lane == i2, w2, 0.0), g_idx


def _post_kernel(xp_ref, xs_ref, mod_ref, n1_ref, wg_ref, zc_ref, o_ref, wco_ref, wo_ref, wmix_ref,
                 n2_ref, wr_ref, x1_ref, h3_ref, meta_ref, cnt_ref):
    i = pl.program_id(0)
    x = jnp.where(i >= T_P // TM_POST, xs_ref[...], xp_ref[...])
    mod = mod_ref[pl.ds(_mod_row(i, TM_POST), 1), :]
    shift1 = mod[:, 0:D_MODEL]
    scale1 = mod[:, D_MODEL:2 * D_MODEL]
    gate1 = mod[:, 2 * D_MODEL:3 * D_MODEL]
    shift2 = mod[:, 3 * D_MODEL:4 * D_MODEL]
    scale2 = mod[:, 4 * D_MODEL:5 * D_MODEL]
    h = ((x * _rms(x)) * n1_ref[...]) * (1.0 + scale1) + shift1
    g = _dot_nt(h.astype(BF16), wg_ref[...])
    y_conv = _dot(zc_ref[...], wco_ref[...])
    y_mla = _dot(o_ref[...], wo_ref[...])
    merged = (jax.nn.sigmoid(g[:, 0:D_MODEL]) * y_conv
              + jax.nn.sigmoid(g[:, D_MODEL:2 * D_MODEL]) * y_mla)
    y = _dot(merged.astype(BF16), wmix_ref[...])
    x1 = x + gate1 * y
    x1_ref[...] = x1
    h2 = ((x1 * _rms(x1)) * n2_ref[...]) * (1.0 + scale2) + shift2
    h2_hi = h2.astype(BF16)
    h2_lo = (h2 - h2_hi.astype(F32)).astype(BF16)
    hh = _dot(h2_hi, wr_ref[...])
    logits = hh[:, 0:LANE] + hh[:, LANE:2 * LANE] + _dot(h2_lo, wr_ref[:, 0:LANE])
    comb, g_idx = _route(logits)

    lane = lax.broadcasted_iota(jnp.int32, comb.shape, 1)
    onehot = lane == g_idx + N_EXPERTS
    r_i = lax.broadcasted_iota(jnp.int32, (TM_POST, TM_POST), 0)
    c_i = lax.broadcasted_iota(jnp.int32, (TM_POST, TM_POST), 1)
    lower = jnp.where(c_i < r_i, 1.0, 0.0).astype(BF16)
    before = _dot(lower, jnp.where(onehot, 1.0, 0.0).astype(BF16))
    rank = jnp.sum(jnp.where(onehot, before, 0.0), axis=-1, keepdims=True)
    counts = jnp.sum(jnp.where(onehot, 1.0, 0.0), axis=0, keepdims=True)
    cnt_ref[...] = jnp.broadcast_to(counts, cnt_ref.shape)

    meta = (comb + jnp.where(lane == GID_LANE, g_idx.astype(F32), 0.0)
            + jnp.where(lane == RANK_LANE, rank, 0.0))
    meta_ref[...] = meta
    for c in range(N_SLAB):
        h3_ref[pl.ds(c, TM_POST, stride=N_SLAB), :] = h2[:, LANE * c:LANE * (c + 1)]


def _post(xp, xs, mod, norm1, w_gates_t, zc, o, w_conv_out, w_o, w_mix_out, norm2, w_route):
    const = lambda i: (0, 0)
    row = lambda i: (i, 0)
    pmap, smap = _stream_maps(TM_POST)
    return pl.pallas_call(
        _post_kernel,
        grid=(T // TM_POST,),
        in_specs=[
            pl.BlockSpec((TM_POST, D_MODEL), pmap),
            pl.BlockSpec((TM_POST, D_MODEL), smap),
            pl.BlockSpec((N_COND, 6 * D_MODEL), const),
            pl.BlockSpec((1, D_MODEL), const),
            pl.BlockSpec((2 * D_MODEL, D_MODEL), const),
            pl.BlockSpec((TM_POST, D_CONV), row),
            pl.BlockSpec((TM_POST, N_HEADS * V_HEAD), row),
            pl.BlockSpec((D_CONV, D_MODEL), const),
            pl.BlockSpec((N_HEADS * V_HEAD, D_MODEL), const),
            pl.BlockSpec((D_MODEL, D_MODEL), const),
            pl.BlockSpec((1, D_MODEL), const),
            pl.BlockSpec((D_MODEL, 2 * LANE), const),
        ],
        out_specs=[
            pl.BlockSpec((TM_POST, D_MODEL), row),
            pl.BlockSpec((TM_POST * N_SLAB, LANE), row),
            pl.BlockSpec((TM_POST, LANE), row),
            pl.BlockSpec((None, 8, LANE), lambda i: (i, 0, 0)),
        ],
        out_shape=[
            jax.ShapeDtypeStruct((T, D_MODEL), F32),
            jax.ShapeDtypeStruct((T * N_SLAB, LANE), F32),
            jax.ShapeDtypeStruct((T, LANE), F32),
            jax.ShapeDtypeStruct((T // TM_POST, 8, LANE), F32),
        ],
        compiler_params=pltpu.CompilerParams(
            dimension_semantics=("parallel",), vmem_limit_bytes=VMEM_LIMIT),
        name="post_mixer",
    )(xp, xs, mod, norm1, w_gates_t, zc, o, w_conv_out, w_o, w_mix_out, norm2, w_route)


def _dispatch_kernel(pos_ref, h3_ref, m_ref, hs_ref, ms_ref, src_ref, xg_ref):
    i = pl.program_id(0)

    @pl.when(i == 0)
    def _():
        def invert(t, carry):
            src_ref[pos_ref[t]] = t
            return carry

        lax.fori_loop(0, T, invert, 0, unroll=8)

    base = i * TM_MOE

    def body(r, carry):
        tok = src_ref[base + r]
        xg_ref[_slab(r), :] = h3_ref[_slab(tok), :]
        ms_ref[pl.ds(r, 1), :] = m_ref[pl.ds(tok, 1), :]
        return carry

    lax.fori_loop(0, TM_MOE, body, 0, unroll=8)
    for c in range(N_SLAB):
        hs_ref[:, LANE * c:LANE * (c + 1)] = xg_ref[pl.ds(c, TM_MOE, stride=N_SLAB), :].astype(BF16)


def _dispatch(pos, h3, meta):
    n_slab = N_SLAB
    return pl.pallas_call(
        _dispatch_kernel,
        grid_spec=pltpu.PrefetchScalarGridSpec(
            num_scalar_prefetch=1,
            grid=(T // TM_MOE,),
            in_specs=[pl.BlockSpec((T * n_slab, LANE), lambda i, pos: (0, 0),
                                   pipeline_mode=pl.Buffered(1)),
                      pl.BlockSpec((T, LANE), lambda i, pos: (0, 0), pipeline_mode=pl.Buffered(1))],
            out_specs=[pl.BlockSpec((TM_MOE, D_MODEL), lambda i, pos: (i, 0)),
                       pl.BlockSpec((TM_MOE, LANE), lambda i, pos: (i, 0))],
            scratch_shapes=[pltpu.SMEM((T,), jnp.int32),
                            pltpu.VMEM((TM_MOE * n_slab, LANE), F32)],
        ),
        out_shape=[jax.ShapeDtypeStruct((T, D_MODEL), BF16),
                   jax.ShapeDtypeStruct((T, LANE), F32)],
        compiler_params=pltpu.CompilerParams(
            dimension_semantics=("arbitrary",), vmem_limit_bytes=VMEM_LIMIT),
        name="moe_dispatch",
    )(pos, h3, meta)


def _moe_kernel(vt_ref, vg_ref, vlo_ref, vhi_ref, vfirst_ref, vlast_ref, vvalid_ref,
                hs_ref, ms_ref, wup_ref, wgate_ref, wdown_ref, y3_ref, acc_ref):
    v = pl.program_id(0)
    j = pl.program_id(1)
    valid = vvalid_ref[v] == 1
    lo = vlo_ref[v]
    hi = vhi_ref[v]
    e = vg_ref[v] * EXP_PER_GROUP + j
    full = (lo == 0) & (hi == TM_MOE)

    @pl.when(valid & (j == 0) & (vfirst_ref[v] == 1))
    def _():
        acc_ref[...] = jnp.zeros_like(acc_ref)

    def expert_rows(r0, rows):
        w_in2 = jnp.concatenate([wup_ref[...].astype(BF16), wgate_ref[...].astype(BF16)], axis=1)
        ag = _dot(hs_ref[r0:r0 + rows, :], w_in2)
        a = ag[:, 0:D_EXPERT]
        g = ag[:, D_EXPERT:]
        comb = ms_ref[r0:r0 + rows, :]
        lane = lax.broadcasted_iota(jnp.int32, comb.shape, 1)
        cw = jnp.sum(jnp.where(lane == e, comb, 0.0), axis=-1, keepdims=True)
        act = (g * jax.nn.sigmoid(g)) * a * cw
        acc_ref[r0:r0 + rows, :] += _dot(act.astype(BF16), wdown_ref[...].astype(BF16))

    @pl.when(valid & full)
    def _():
        expert_rows(0, TM_MOE)

    @pl.when(valid & jnp.logical_not(full))
    def _():
        for s in range(TM_MOE // MOE_SUB):
            r0 = s * MOE_SUB

            @pl.when((lo < r0 + MOE_SUB) & (hi > r0))
            def _():
                expert_rows(r0, MOE_SUB)

    @pl.when(valid & (j == EXP_PER_GROUP - 1) & (vlast_ref[v] == 1))
    def _():
        for c in range(N_SLAB):
            y3_ref[pl.ds(c, TM_MOE, stride=N_SLAB), :] = acc_ref[:, LANE * c:LANE * (c + 1)]


def _moe(sched, hs, ms, w_up, w_gate, w_down):
    wmap = lambda v, j, vt, vg, vlo, vhi, vfirst, vlast, vvalid: (
        vg[v] * EXP_PER_GROUP + jnp.where(vvalid[v] == 1, j, EXP_PER_GROUP - 1), 0, 0)
    tmap = lambda v, j, vt, *_: (vt[v], 0)
    n_slab = D_MODEL // LANE
    return pl.pallas_call(
        _moe_kernel,
        grid_spec=pltpu.PrefetchScalarGridSpec(
            num_scalar_prefetch=7,
            grid=(N_VISITS, EXP_PER_GROUP),
            in_specs=[
                pl.BlockSpec((TM_MOE, D_MODEL), tmap),
                pl.BlockSpec((TM_MOE, LANE), tmap),
                pl.BlockSpec((None, D_MODEL, D_EXPERT), wmap),
                pl.BlockSpec((None, D_MODEL, D_EXPERT), wmap),
                pl.BlockSpec((None, D_EXPERT, D_MODEL), wmap),
            ],
            out_specs=pl.BlockSpec((TM_MOE * n_slab, LANE), tmap),
            scratch_shapes=[pltpu.VMEM((TM_MOE, D_MODEL), F32)],
        ),
        out_shape=jax.ShapeDtypeStruct((T * n_slab, LANE), F32),
        compiler_params=pltpu.CompilerParams(
            dimension_semantics=("arbitrary", "arbitrary"), vmem_limit_bytes=VMEM_LIMIT),
        name="moe_grouped",
    )(*sched, hs, ms, w_up, w_gate, w_down)


def _final_kernel(pos_ref, ys_ref, x1_ref, mod_ref, fn_ref, yp_ref, ysm_ref, g_ref):
    i = pl.program_id(0)
    base = i * TM_FINAL

    def body(r, carry):
        g_ref[_slab(r), :] = ys_ref[_slab(pos_ref[base + r]), :]
        return carry

    lax.fori_loop(0, TM_FINAL, body, 0, unroll=8)
    mod = mod_ref[pl.ds(_mod_row(i, TM_FINAL), 1), :]
    gate2 = mod[:, 5 * D_MODEL:6 * D_MODEL]
    moe = jnp.concatenate([g_ref[pl.ds(c, TM_FINAL, stride=N_SLAB), :] for c in range(N_SLAB)], axis=1)
    x2 = x1_ref[...] + gate2 * moe
    y = (x2 * _rms(x2)) * fn_ref[...]
    is_sample = i >= T_P // TM_FINAL

    @pl.when(jnp.logical_not(is_sample))
    def _():
        yp_ref[...] = y

    @pl.when(is_sample)
    def _():
        ysm_ref[...] = y


def _final(pos, ys, x1, mod, final_norm):
    n_slab = D_MODEL // LANE
    pmap, smap = _stream_maps(TM_FINAL)
    return pl.pallas_call(
        _final_kernel,
        grid_spec=pltpu.PrefetchScalarGridSpec(
            num_scalar_prefetch=1,
            grid=(T // TM_FINAL,),
            in_specs=[
                pl.BlockSpec((T * n_slab, LANE), lambda i, pos: (0, 0), pipeline_mode=pl.Buffered(1)),
                pl.BlockSpec((TM_FINAL, D_MODEL), lambda i, pos: (i, 0)),
                pl.BlockSpec((N_COND, 6 * D_MODEL), lambda i, pos: (0, 0)),
                pl.BlockSpec((1, D_MODEL), lambda i, pos: (0, 0)),
            ],
            out_specs=[pl.BlockSpec((TM_FINAL, D_MODEL), pmap),
                       pl.BlockSpec((TM_FINAL, D_MODEL), smap)],
            scratch_shapes=[pltpu.VMEM((TM_FINAL * n_slab, LANE), F32)],
        ),
        out_shape=[jax.ShapeDtypeStruct((T_P, D_MODEL), F32),
                   jax.ShapeDtypeStruct((T_S, D_MODEL), F32)],
        compiler_params=pltpu.CompilerParams(
            dimension_semantics=("arbitrary",), vmem_limit_bytes=VMEM_LIMIT),
        name="moe_unsort_final",
    )(pos, ys, x1, mod, final_norm)


def _schedule(meta, cnt):
    n_tiles_post = T // TM_POST
    counts = cnt[:, 0, N_EXPERTS:N_EXPERTS + N_GROUPS].astype(jnp.int32)
    gtot = jnp.sum(counts, axis=0)
    goff = jnp.cumsum(gtot) - gtot
    tile_base = goff[None, :] + jnp.cumsum(counts, axis=0) - counts
    gid = meta[:, GID_LANE].astype(jnp.int32).reshape(n_tiles_post, TM_POST)
    rank = meta[:, RANK_LANE].astype(jnp.int32).reshape(n_tiles_post, TM_POST)
    pos = rank
    for grp in range(N_GROUPS):
        pos = pos + jnp.where(gid == grp, tile_base[:, grp:grp + 1], 0)
    pos = pos.reshape(T)

    n_tiles = T // TM_MOE
    t_lo = (jnp.arange(n_tiles, dtype=jnp.int32) * TM_MOE)[:, None]
    lo = jnp.clip(goff[None, :] - t_lo, 0, TM_MOE)
    hi = jnp.clip(goff[None, :] + gtot[None, :] - t_lo, 0, TM_MOE)
    ok = (hi > lo).reshape(-1)
    slot = jnp.cumsum(ok.astype(jnp.int32)) - 1
    n_ok = slot[-1] + 1
    sel = (slot[None, :] == jnp.arange(N_VISITS, dtype=jnp.int32)[:, None]) & ok[None, :]

    def pick(vals):
        return jnp.sum(jnp.where(sel, vals.reshape(-1)[None, :], 0), axis=1).astype(jnp.int32)

    pair_tile = jnp.broadcast_to(jnp.arange(n_tiles, dtype=jnp.int32)[:, None], (n_tiles, N_GROUPS))
    pair_group = jnp.broadcast_to(jnp.arange(N_GROUPS, dtype=jnp.int32)[None, :], (n_tiles, N_GROUPS))
    vt, vg, vlo, vhi = pick(pair_tile), pick(pair_group), pick(lo), pick(hi)
    valid = jnp.arange(N_VISITS, dtype=jnp.int32) < n_ok
    last = jnp.maximum(n_ok - 1, 0)
    vt = jnp.where(valid, vt, vt[last])
    vg = jnp.where(valid, vg, vg[last])
    change = (vt[1:] != vt[:-1]).astype(jnp.int32)
    one = jnp.ones((1,), jnp.int32)
    first = jnp.concatenate([one, change])
    idx = jnp.arange(N_VISITS, dtype=jnp.int32)
    last = jnp.where(idx == n_ok - 1, 1, jnp.concatenate([change, one]))
    return pos, (vt, vg, vlo, vhi, first, last, valid.astype(jnp.int32))


def _rope_tables():
    n = np.arange(DEC_SEQ)
    pos = np.stack([n // GRID_W, n % GRID_W], axis=1).astype(np.float32)
    half = ROPE_AXIS // 2
    inv = (1.0 / (ROPE_BASE ** (np.arange(0, ROPE_AXIS, 2, dtype=np.float32) / ROPE_AXIS))).astype(np.float32)
    ang = (pos[:, :, None] * inv[None, None, :]).astype(np.float32)
    cos = np.cos(ang).astype(np.float32)
    sin = np.sin(ang).astype(np.float32)
    tabs = np.zeros((2, 3, DEC_SEQ, LANE), np.float32)
    tabs[:, 0] = 1.0
    for a in range(2):
        lo = ROPE_LANE0 + a * ROPE_AXIS
        tabs[1, 0, :, lo:lo + half] = cos[:, a]
        tabs[1, 0, :, lo + half:lo + 2 * half] = cos[:, a]
        tabs[1, 1, :, lo + half:lo + 2 * half] = sin[:, a]
        tabs[1, 2, :, lo:lo + half] = -sin[:, a]
    return jnp.asarray(tabs)


def kernel(x_prompt, x_sample, cache_ckv, cache_krope, c, c_ctx, norm1, w_ada, b_ada, w_in, conv_w,
           w_conv_out, q_norm, w_uq, kv_norm, w_ukv, w_o, w_mix_out, norm2, w_grp, w_exp, w_up,
           w_gate, w_down, final_norm):
    l = 0
    xp = x_prompt.reshape(T_P, D_MODEL)
    xs = x_sample.reshape(T_S, D_MODEL)
    cond = jnp.concatenate(
        [c_ctx[None, :], c, jnp.zeros((N_COND - 1 - DEC_BATCH, D_MODEL), F32)], axis=0)
    mod = _ada(cond, w_ada[l], b_ada[l][None, :])

    w_in_t = w_in[l].T
    w_conv3_t, w_small_t, w_gates_t, w_co_b, w_o_b, w_mix_b = _prep_weights(
        w_in_t, w_conv_out[l], w_o[l], w_mix_out[l])
    w_uq_slot = jnp.pad(w_uq[l].reshape(Q_LORA, N_HEADS, QK_NOPE + QK_ROPE),
                        ((0, 0), (0, 0), (0, LANE - QK_NOPE - QK_ROPE))
                        ).reshape(Q_LORA, N_HEADS * LANE).astype(BF16)
    wkv = w_ukv[l].reshape(KV_LORA, N_HEADS, QK_NOPE + V_HEAD)
    wk_slot = jnp.pad(wkv[:, :, :QK_NOPE], ((0, 0), (0, 0), (0, LANE - QK_NOPE)))
    wv = wkv[:, :, QK_NOPE:].reshape(KV_LORA, N_HEADS // 2, 2, V_HEAD)
    zero = jnp.zeros_like(wv[:, :, 0])
    wv_slot = jnp.stack([jnp.concatenate([wv[:, :, 0], zero], axis=-1),
                         jnp.concatenate([zero, wv[:, :, 1]], axis=-1)], axis=2)
    w_ukv_slot = jnp.concatenate([wk_slot.reshape(KV_LORA, N_HEADS * LANE),
                                  wv_slot.reshape(KV_LORA, N_HEADS * LANE)], axis=1).astype(BF16)
    w_route = jnp.pad(jnp.concatenate([w_exp[l], w_grp[l]], axis=1),
                      ((0, 0), (0, LANE - N_EXPERTS - N_GROUPS)))
    w_route_hi = w_route.astype(BF16)
    w_route_lo = (w_route - w_route_hi.astype(F32)).astype(BF16)
    w_route2 = jnp.concatenate([w_route_hi, w_route_lo], axis=1)
    cache_krs = jnp.pad(cache_krope[:, l], ((0, 0), (0, 0), (ROPE_LANE0, LANE - ROPE_LANE0 - QK_ROPE)))

    zc, q, ckv, krs = _inproj(xp, xs, mod, norm1[l][None, :], w_conv3_t, w_small_t, conv_w[l],
                              q_norm[l][None, :], kv_norm[l][None, :], w_uq_slot, _rope_tables())
    o = _attention(q, ckv, krs, cache_ckv[:, l], cache_krs, w_ukv_slot)
    x1, h3, meta, cnt = _post(xp, xs, mod, norm1[l][None, :], w_gates_t, zc, o, w_co_b, w_o_b, w_mix_b,
                              norm2[l][None, :], w_route2)
    pos, sched = _schedule(meta, cnt)
    hs, ms = _dispatch(pos, h3, meta)
    ys = _moe(sched, hs, ms, w_up[l], w_gate[l], w_down[l])
    yp, ysm = _final(pos, ys, x1, mod, final_norm[None, :])

    y_prompt = yp.reshape(BATCH, SEQ, D_MODEL)
    y_sample = ysm.reshape(DEC_BATCH, DEC_SEQ, D_MODEL)
    new_ckv = ckv[:T_P].reshape(BATCH, 1, SEQ, KV_LORA)
    new_krope = krs[:T_P, ROPE_LANE0:ROPE_LANE0 + QK_ROPE].reshape(BATCH, 1, SEQ, QK_ROPE)
    return (y_prompt, y_sample, new_ckv, new_krope)
```

```python
import functools

import numpy as np
import jax
import jax.numpy as jnp
from jax import lax
from jax.experimental import pallas as pl
from jax.experimental.pallas import tpu as pltpu

F32 = jnp.float32
BF16 = jnp.bfloat16

D_MODEL = 1024
BATCH = 16
SEQ = 256
DEC_BATCH = 2
DEC_SEQ = 1024
PAST_LEN = 256
GRID_W = 64
N_HEADS = 8
QK_NOPE = 64
QK_ROPE = 32
V_HEAD = 64
Q_LORA = 256
KV_LORA = 128
ROPE_AXIS = QK_ROPE // 2
ROPE_BASE = 10000.0
ATTN_SCALE = (QK_NOPE + QK_ROPE) ** -0.5
D_CONV = D_MODEL
N_GROUPS = 4
EXP_PER_GROUP = 8
N_EXPERTS = N_GROUPS * EXP_PER_GROUP
D_EXPERT = 256
EPS = 1e-6

T_P = BATCH * SEQ
T_S = DEC_BATCH * DEC_SEQ
T = T_P + T_S
N_COND = 8
LANE = 128
ROPE_LANE0 = QK_NOPE
SMALL_COLS = Q_LORA + KV_LORA + LANE
VMEM_LIMIT = 56 * 1024 * 1024

TM_IN = 1024
TM_POST = 512
TM_MOE = T // N_GROUPS
TM_FINAL = 512
N_SLAB = D_MODEL // LANE
MOE_SUB = 256
MOE_EPS = 2
N_VISITS = T // TM_MOE + N_GROUPS - 1
GID_LANE = 40
RANK_LANE = 41
Q_BLK = 256
CONV_CHUNK = 512


def _dot(a, b):
    return jnp.dot(a, b, preferred_element_type=F32)


def _rms(x):
    return lax.rsqrt(jnp.mean(x * x, axis=-1, keepdims=True) + EPS)


def _slab(t):
    return pl.ds(pl.multiple_of(t * N_SLAB, N_SLAB), N_SLAB)


def _mod_row(i, tm):
    n_prompt = T_P // tm
    return jnp.where(i >= n_prompt, 1 + ((i - n_prompt) * tm) // DEC_SEQ, 0)


def _ada_kernel(cond_ref, w_ref, b_ref, o_ref):
    c = cond_ref[...]
    a = (c * jax.nn.sigmoid(c)).astype(BF16)
    o_ref[...] = _dot(a, w_ref[...].astype(BF16)) + b_ref[...]


def _ada(cond, w_ada, b_ada):
    n = 6 * D_MODEL
    bn = 1536
    return pl.pallas_call(
        _ada_kernel,
        grid=(n // bn,),
        in_specs=[
            pl.BlockSpec((N_COND, D_MODEL), lambda j: (0, 0)),
            pl.BlockSpec((D_MODEL, bn), lambda j: (0, j)),
            pl.BlockSpec((1, bn), lambda j: (0, j)),
        ],
        out_specs=pl.BlockSpec((N_COND, bn), lambda j: (0, j)),
        out_shape=jax.ShapeDtypeStruct((N_COND, n), F32),
        compiler_params=pltpu.CompilerParams(
            dimension_semantics=("parallel",), vmem_limit_bytes=VMEM_LIMIT),
        name="ada_mod",
    )(cond, w_ada, b_ada)


O_CQ = 3 * D_CONV
O_KR = O_CQ + Q_LORA + KV_LORA
O_GATE = O_KR + QK_ROPE
PREP_STEPS = 4


IN_COLS = O_GATE + 2 * D_MODEL
TAIL_ROWS = IN_COLS // 2
SMALL_BLK = 512
NT = (((1,), (1,)), ((), ()))


def _dot_nt(a, bt):
    return lax.dot_general(a, bt, NT, preferred_element_type=F32)


def _prep_kernel(wct_ref, wst_ref, wtail_ref, wco_ref, wo_ref, wmix_ref,
                 c3_ref, sm_ref, g_ref, co_ref, o_ref, mix_ref):
    c3_ref[...] = wct_ref[...].astype(BF16)
    n_lat = Q_LORA + KV_LORA
    sm_ref[0:n_lat, :] = wst_ref[0:n_lat, :].astype(BF16)
    sm_ref[n_lat:, :] = jnp.zeros((LANE, sm_ref.shape[1]), BF16)
    sm_ref[n_lat + ROPE_LANE0:n_lat + ROPE_LANE0 + QK_ROPE, :] = (
        wst_ref[n_lat:n_lat + QK_ROPE, :].astype(BF16))
    g_ref[...] = wtail_ref[O_GATE - TAIL_ROWS:, :].astype(BF16)
    co_ref[...] = wco_ref[...].astype(BF16)
    o_ref[...] = wo_ref[...].astype(BF16)
    mix_ref[...] = wmix_ref[...].astype(BF16)


def _prep_weights(w_in_t, w_conv_out, w_o, w_mix_out):
    cb = D_MODEL // PREP_STEPS
    col = lambda rows, blk=0: pl.BlockSpec((rows, cb), lambda i: (blk, i))
    n_o = N_HEADS * V_HEAD
    return pl.pallas_call(
        _prep_kernel,
        grid=(PREP_STEPS,),
        in_specs=[col(O_CQ), col(SMALL_BLK, O_CQ // SMALL_BLK), col(TAIL_ROWS, 1),
                  col(D_CONV), col(n_o), col(D_MODEL)],
        out_specs=[col(O_CQ), col(SMALL_COLS), col(2 * D_MODEL), col(D_CONV), col(n_o), col(D_MODEL)],
        out_shape=[jax.ShapeDtypeStruct((O_CQ, D_MODEL), BF16),
                   jax.ShapeDtypeStruct((SMALL_COLS, D_MODEL), BF16),
                   jax.ShapeDtypeStruct((2 * D_MODEL, D_MODEL), BF16),
                   jax.ShapeDtypeStruct((D_CONV, D_MODEL), BF16),
                   jax.ShapeDtypeStruct((n_o, D_MODEL), BF16),
                   jax.ShapeDtypeStruct((D_MODEL, D_MODEL), BF16)],
        compiler_params=pltpu.CompilerParams(
            dimension_semantics=("parallel",), vmem_limit_bytes=VMEM_LIMIT),
        name="weight_casts",
    )(w_in_t, w_in_t, w_in_t, w_conv_out, w_o, w_mix_out)


def _stream_maps(tm):
    n_prompt = T_P // tm
    return (lambda i, *_: (jnp.minimum(i, n_prompt - 1), 0),
            lambda i, *_: (jnp.maximum(i - n_prompt, 0), 0))


def _inproj_kernel(xp_ref, xs_ref, mod_ref, n1_ref, wc_ref, ws_ref, cw_ref, qn_ref, kvn_ref, wuq_ref,
                   rope_ref, zc_ref, q_ref, ckv_ref, krs_ref):
    i = pl.program_id(0)
    is_sample = i >= T_P // TM_IN
    seq = jnp.where(is_sample, DEC_SEQ, SEQ)
    mod = mod_ref[pl.ds(_mod_row(i, TM_IN), 1), :]
    shift1 = mod[:, 0:D_MODEL]
    scale1 = mod[:, D_MODEL:2 * D_MODEL]
    x = jnp.where(is_sample, xs_ref[...], xp_ref[...])
    h = ((x * _rms(x)) * n1_ref[...]) * (1.0 + scale1) + shift1
    hb = h.astype(BF16)

    sm = _dot_nt(hb, ws_ref[...])
    cq = sm[:, 0:Q_LORA]
    ckv_raw = sm[:, Q_LORA:Q_LORA + KV_LORA]
    krs = sm[:, Q_LORA + KV_LORA:SMALL_COLS]
    cqn = (cq * _rms(cq)) * qn_ref[...]
    q = _dot(cqn.astype(BF16), wuq_ref[...])
    ckv_ref[...] = (ckv_raw * _rms(ckv_raw)) * kvn_ref[...]

    cos = rope_ref[0]
    sin_lo = rope_ref[1]
    sin_hi = rope_ref[2]

    def rot(v):
        return v * cos + pltpu.roll(v, 8, 1) * sin_lo + pltpu.roll(v, LANE - 8, 1) * sin_hi

    krs_ref[...] = rot(krs)
    for hh in range(N_HEADS):
        q_ref[:, LANE * hh:LANE * (hh + 1)] = rot(q[:, LANE * hh:LANE * (hh + 1)]).astype(BF16)

    pos = lax.broadcasted_iota(jnp.int32, (TM_IN, 1), 0) & (seq - 1)
    first = pos == 0
    last = pos == seq - 1
    for j in range(D_CONV // CONV_CHUNK):
        c0 = j * CONV_CHUNK
        bg = _dot_nt(hb, wc_ref[c0:c0 + CONV_CHUNK, :])
        cg = _dot_nt(hb, wc_ref[D_CONV + c0:D_CONV + c0 + CONV_CHUNK, :])
        ui = _dot_nt(hb, wc_ref[2 * D_CONV + c0:2 * D_CONV + c0 + CONV_CHUNK, :])
        u = cg * ui
        u_prev = jnp.where(first, 0.0, pltpu.roll(u, 1, 0))
        u_next = jnp.where(last, 0.0, pltpu.roll(u, TM_IN - 1, 0))
        cw = cw_ref[:, c0:c0 + CONV_CHUNK]
        conv = u_prev * cw[0:1] + u * cw[1:2] + u_next * cw[2:3]
        zc_ref[:, c0:c0 + CONV_CHUNK] = (bg * conv).astype(BF16)


def _inproj(xp, xs, mod, norm1, w_conv3_t, w_small_t, conv_w, q_norm, kv_norm, w_uq_slot, rope_tabs):
    n_prompt = T_P // TM_IN
    const = lambda i: (0, 0)
    pmap, smap = _stream_maps(TM_IN)
    return pl.pallas_call(
        _inproj_kernel,
        grid=(T // TM_IN,),
        in_specs=[
            pl.BlockSpec((TM_IN, D_MODEL), pmap),
            pl.BlockSpec((TM_IN, D_MODEL), smap),
            pl.BlockSpec((N_COND, 6 * D_MODEL), const),
            pl.BlockSpec((1, D_MODEL), const),
            pl.BlockSpec((O_CQ, D_MODEL), const),
            pl.BlockSpec((SMALL_COLS, D_MODEL), const),
            pl.BlockSpec((3, D_CONV), const),
            pl.BlockSpec((1, Q_LORA), const),
            pl.BlockSpec((1, KV_LORA), const),
            pl.BlockSpec((Q_LORA, N_HEADS * LANE), const),
            pl.BlockSpec((None, 3, TM_IN, LANE),
                         lambda i: (jnp.where(i >= n_prompt, 1, 0), 0, 0, 0)),
        ],
        out_specs=[
            pl.BlockSpec((TM_IN, D_CONV), lambda i: (i, 0)),
            pl.BlockSpec((TM_IN, N_HEADS * LANE), lambda i: (i, 0)),
            pl.BlockSpec((TM_IN, KV_LORA), lambda i: (i, 0)),
            pl.BlockSpec((TM_IN, LANE), lambda i: (i, 0)),
        ],
        out_shape=[
            jax.ShapeDtypeStruct((T, D_CONV), BF16),
            jax.ShapeDtypeStruct((T, N_HEADS * LANE), BF16),
            jax.ShapeDtypeStruct((T, KV_LORA), F32),
            jax.ShapeDtypeStruct((T, LANE), F32),
        ],
        compiler_params=pltpu.CompilerParams(
            dimension_semantics=("parallel",), vmem_limit_bytes=VMEM_LIMIT),
        name="in_proj",
    )(xp, xs, mod, norm1, w_conv3_t, w_small_t, conv_w, q_norm, kv_norm, w_uq_slot, rope_tabs)


def _attn_kernel(*refs, n_src):
    q_ref = refs[0]
    ckv_refs = refs[1:1 + n_src]
    krs_refs = refs[1 + n_src:1 + 2 * n_src]
    wukv_ref = refs[1 + 2 * n_src]
    o_ref = refs[2 + 2 * n_src]
    kf_scr, v_scr = refs[3 + 2 * n_src:]

    @pl.when(pl.program_id(1) == 0)
    def _():
        off = 0
        for c_ref, k_ref in zip(ckv_refs, krs_refs):
            m = c_ref.shape[0]
            kv = _dot(c_ref[...].astype(BF16), wukv_ref[...])
            krs = k_ref[...]
            for hh in range(N_HEADS):
                kf_scr[hh, off:off + m, :] = (kv[:, LANE * hh:LANE * (hh + 1)] + krs).astype(BF16)
            v_scr[off:off + m, :] = kv[:, N_HEADS * LANE:].astype(BF16)
            off += m

    for pair in range(N_HEADS // 2):
        acc = None
        for hh in (2 * pair, 2 * pair + 1):
            qh = q_ref[:, LANE * hh:LANE * (hh + 1)]
            s = lax.dot_general(qh, kf_scr[hh], (((1,), (1,)), ((), ())),
                                preferred_element_type=F32) * ATTN_SCALE
            e = jnp.exp(s - jnp.max(s, axis=-1, keepdims=True))
            p = (e / jnp.sum(e, axis=-1, keepdims=True)).astype(BF16)
            part = _dot(p, v_scr[:, LANE * hh:LANE * (hh + 1)])
            acc = part if acc is None else acc + part
        o_ref[:, LANE * pair:LANE * (pair + 1)] = acc.astype(BF16)


def _attention(q, ckv, krs, cache_ckv, cache_krs, w_ukv_slot):
    kv_cols = 2 * N_HEADS * LANE
    cp = pltpu.CompilerParams(dimension_semantics=("parallel", "arbitrary"),
                              vmem_limit_bytes=VMEM_LIMIT)
    o_prompt = pl.pallas_call(
        functools.partial(_attn_kernel, n_src=1),
        grid=(BATCH, SEQ // Q_BLK),
        in_specs=[
            pl.BlockSpec((Q_BLK, N_HEADS * LANE), lambda b, j: (b * (SEQ // Q_BLK) + j, 0)),
            pl.BlockSpec((SEQ, KV_LORA), lambda b, j: (b, 0)),
            pl.BlockSpec((SEQ, LANE), lambda b, j: (b, 0)),
            pl.BlockSpec((KV_LORA, kv_cols), lambda b, j: (0, 0)),
        ],
        out_specs=pl.BlockSpec((Q_BLK, N_HEADS * V_HEAD), lambda b, j: (b * (SEQ // Q_BLK) + j, 0)),
        out_shape=jax.ShapeDtypeStruct((T_P, N_HEADS * V_HEAD), BF16),
        scratch_shapes=[pltpu.VMEM((N_HEADS, SEQ, LANE), BF16),
                        pltpu.VMEM((SEQ, N_HEADS * LANE), BF16)],
        compiler_params=cp,
        name="attn_prompt",
    )(q, ckv, krs, w_ukv_slot)

    m_all = DEC_SEQ + PAST_LEN
    nq = DEC_SEQ // Q_BLK
    q0 = T_P // Q_BLK
    s0 = T_P // DEC_SEQ
    o_sample = pl.pallas_call(
        functools.partial(_attn_kernel, n_src=2),
        grid=(DEC_BATCH, nq),
        in_specs=[
            pl.BlockSpec((Q_BLK, N_HEADS * LANE), lambda b, j: (q0 + b * nq + j, 0)),
            pl.BlockSpec((DEC_SEQ, KV_LORA), lambda b, j: (s0 + b, 0)),
            pl.BlockSpec((None, PAST_LEN, KV_LORA), lambda b, j: (b, 0, 0)),
            pl.BlockSpec((DEC_SEQ, LANE), lambda b, j: (s0 + b, 0)),
            pl.BlockSpec((None, PAST_LEN, LANE), lambda b, j: (b, 0, 0)),
            pl.BlockSpec((KV_LORA, kv_cols), lambda b, j: (0, 0)),
        ],
        out_specs=pl.BlockSpec((Q_BLK, N_HEADS * V_HEAD), lambda b, j: (b * nq + j, 0)),
        out_shape=jax.ShapeDtypeStruct((T_S, N_HEADS * V_HEAD), BF16),
        scratch_shapes=[pltpu.VMEM((N_HEADS, m_all, LANE), BF16),
                        pltpu.VMEM((m_all, N_HEADS * LANE), BF16)],
        compiler_params=cp,
        name="attn_sample",
    )(q, ckv, cache_ckv, krs, cache_krs, w_ukv_slot)
    return jnp.concatenate([o_prompt, o_sample], axis=0)


def _route(logits):
    lane = lax.broadcasted_iota(jnp.int32, logits.shape, 1)
    neg = -jnp.inf
    big = jnp.int32(1 << 20)
    gmask = (lane >= N_EXPERTS) & (lane < N_EXPERTS + N_GROUPS)
    gl = jnp.where(gmask, logits, neg)
    gmax = jnp.max(gl, axis=-1, keepdims=True)
    gsum = jnp.sum(jnp.where(gmask, jnp.exp(gl - gmax), 0.0), axis=-1, keepdims=True)
    p_g = 1.0 / gsum
    g_idx = jnp.min(jnp.where(gl == gmax, lane, big), axis=-1, keepdims=True) - N_EXPERTS

    emask = (lane < N_EXPERTS) & ((lane >> 3) == g_idx)
    el = jnp.where(emask, logits, neg)
    m1 = jnp.max(el, axis=-1, keepdims=True)
    i1 = jnp.min(jnp.where(el == m1, lane, big), axis=-1, keepdims=True)
    el2 = jnp.where(lane == i1, neg, el)
    m2 = jnp.max(el2, axis=-1, keepdims=True)
    i2 = jnp.min(jnp.where(el2 == m2, lane, big), axis=-1, keepdims=True)
    z = jnp.sum(jnp.where(emask, jnp.exp(el - m1), 0.0), axis=-1, keepdims=True)
    p1 = 1.0 / z
    p2 = jnp.exp(m2 - m1) / z
    tot = p1 + p2
    w1 = p_g * p1 / tot
    w2 = p_g * p2 / tot
    return jnp.where(lane == i1, w1, 0.0) + jnp.where(lane == i2, w2, 0.0), g_idx


def _post_kernel(xp_ref, xs_ref, mod_ref, n1_ref, wg_ref, zc_ref, o_ref, wco_ref, wo_ref, wmix_ref,
                 n2_ref, wr_ref, x1_ref, h3_ref, meta_ref, cnt_ref):
    i = pl.program_id(0)
    x = jnp.where(i >= T_P // TM_POST, xs_ref[...], xp_ref[...])
    mod = mod_ref[pl.ds(_mod_row(i, TM_POST), 1), :]
    shift1 = mod[:, 0:D_MODEL]
    scale1 = mod[:, D_MODEL:2 * D_MODEL]
    gate1 = mod[:, 2 * D_MODEL:3 * D_MODEL]
    shift2 = mod[:, 3 * D_MODEL:4 * D_MODEL]
    scale2 = mod[:, 4 * D_MODEL:5 * D_MODEL]
    h = ((x * _rms(x)) * n1_ref[...]) * (1.0 + scale1) + shift1
    g = _dot_nt(h.astype(BF16), wg_ref[...])
    y_conv = _dot(zc_ref[...], wco_ref[...])
    y_mla = _dot(o_ref[...], wo_ref[...])
    merged = (jax.nn.sigmoid(g[:, 0:D_MODEL]) * y_conv
              + jax.nn.sigmoid(g[:, D_MODEL:2 * D_MODEL]) * y_mla)
    y = _dot(merged.astype(BF16), wmix_ref[...])
    x1 = x + gate1 * y
    x1_ref[...] = x1
    h2 = ((x1 * _rms(x1)) * n2_ref[...]) * (1.0 + scale2) + shift2
    h2_hi = h2.astype(BF16)
    h2_lo = (h2 - h2_hi.astype(F32)).astype(BF16)
    hh = _dot(h2_hi, wr_ref[...])
    logits = hh[:, 0:LANE] + hh[:, LANE:2 * LANE] + _dot(h2_lo, wr_ref[:, 0:LANE])
    comb, g_idx = _route(logits)

    lane = lax.broadcasted_iota(jnp.int32, comb.shape, 1)
    onehot = lane == g_idx + N_EXPERTS
    r_i = lax.broadcasted_iota(jnp.int32, (TM_POST, TM_POST), 0)
    c_i = lax.broadcasted_iota(jnp.int32, (TM_POST, TM_POST), 1)
    lower = jnp.where(c_i < r_i, 1.0, 0.0).astype(BF16)
    before = _dot(lower, jnp.where(onehot, 1.0, 0.0).astype(BF16))
    rank = jnp.sum(jnp.where(onehot, before, 0.0), axis=-1, keepdims=True)
    counts = jnp.sum(jnp.where(onehot, 1.0, 0.0), axis=0, keepdims=True)
    cnt_ref[...] = jnp.broadcast_to(counts, cnt_ref.shape)

    meta = (comb + jnp.where(lane == GID_LANE, g_idx.astype(F32), 0.0)
            + jnp.where(lane == RANK_LANE, rank, 0.0))
    meta_ref[...] = meta
    for c in range(N_SLAB):
        h3_ref[pl.ds(c, TM_POST, stride=N_SLAB), :] = h2[:, LANE * c:LANE * (c + 1)]


def _post(xp, xs, mod, norm1, w_gates_t, zc, o, w_conv_out, w_o, w_mix_out, norm2, w_route):
    const = lambda i: (0, 0)
    row = lambda i: (i, 0)
    pmap, smap = _stream_maps(TM_POST)
    return pl.pallas_call(
        _post_kernel,
        grid=(T // TM_POST,),
        in_specs=[
            pl.BlockSpec((TM_POST, D_MODEL), pmap),
            pl.BlockSpec((TM_POST, D_MODEL), smap),
            pl.BlockSpec((N_COND, 6 * D_MODEL), const),
            pl.BlockSpec((1, D_MODEL), const),
            pl.BlockSpec((2 * D_MODEL, D_MODEL), const),
            pl.BlockSpec((TM_POST, D_CONV), row),
            pl.BlockSpec((TM_POST, N_HEADS * V_HEAD), row),
            pl.BlockSpec((D_CONV, D_MODEL), const),
            pl.BlockSpec((N_HEADS * V_HEAD, D_MODEL), const),
            pl.BlockSpec((D_MODEL, D_MODEL), const),
            pl.BlockSpec((1, D_MODEL), const),
            pl.BlockSpec((D_MODEL, 2 * LANE), const),
        ],
        out_specs=[
            pl.BlockSpec((TM_POST, D_MODEL), row),
            pl.BlockSpec((TM_POST * N_SLAB, LANE), row),
            pl.BlockSpec((TM_POST, LANE), row),
            pl.BlockSpec((None, 8, LANE), lambda i: (i, 0, 0)),
        ],
        out_shape=[
            jax.ShapeDtypeStruct((T, D_MODEL), F32),
            jax.ShapeDtypeStruct((T * N_SLAB, LANE), F32),
            jax.ShapeDtypeStruct((T, LANE), F32),
            jax.ShapeDtypeStruct((T // TM_POST, 8, LANE), F32),
        ],
        compiler_params=pltpu.CompilerParams(
            dimension_semantics=("parallel",), vmem_limit_bytes=VMEM_LIMIT),
        name="post_mixer",
    )(xp, xs, mod, norm1, w_gates_t, zc, o, w_conv_out, w_o, w_mix_out, norm2, w_route)


def _dispatch_kernel(pos_ref, h3_ref, m_ref, hs_ref, ms_ref, src_ref, xg_ref):
    i = pl.program_id(0)

    @pl.when(i == 0)
    def _():
        def invert(t, carry):
            src_ref[pos_ref[t]] = t
            return carry

        lax.fori_loop(0, T, invert, 0, unroll=8)

    base = i * TM_MOE

    def body(r, carry):
        tok = src_ref[base + r]
        xg_ref[_slab(r), :] = h3_ref[_slab(tok), :]
        ms_ref[pl.ds(r, 1), :] = m_ref[pl.ds(tok, 1), :]
        return carry

    lax.fori_loop(0, TM_MOE, body, 0, unroll=8)
    for c in range(N_SLAB):
        hs_ref[:, LANE * c:LANE * (c + 1)] = xg_ref[pl.ds(c, TM_MOE, stride=N_SLAB), :].astype(BF16)


def _dispatch(pos, h3, meta):
    n_slab = N_SLAB
    return pl.pallas_call(
        _dispatch_kernel,
        grid_spec=pltpu.PrefetchScalarGridSpec(
            num_scalar_prefetch=1,
            grid=(T // TM_MOE,),
            in_specs=[pl.BlockSpec((T * n_slab, LANE), lambda i, pos: (0, 0),
                                   pipeline_mode=pl.Buffered(1)),
                      pl.BlockSpec((T, LANE), lambda i, pos: (0, 0), pipeline_mode=pl.Buffered(1))],
            out_specs=[pl.BlockSpec((TM_MOE, D_MODEL), lambda i, pos: (i, 0)),
                       pl.BlockSpec((TM_MOE, LANE), lambda i, pos: (i, 0))],
            scratch_shapes=[pltpu.SMEM((T,), jnp.int32),
                            pltpu.VMEM((TM_MOE * n_slab, LANE), F32)],
        ),
        out_shape=[jax.ShapeDtypeStruct((T, D_MODEL), BF16),
                   jax.ShapeDtypeStruct((T, LANE), F32)],
        compiler_params=pltpu.CompilerParams(
            dimension_semantics=("arbitrary",), vmem_limit_bytes=VMEM_LIMIT),
        name="moe_dispatch",
    )(pos, h3, meta)


def _moe_kernel(vt_ref, vg_ref, vlo_ref, vhi_ref, vfirst_ref, vlast_ref, vvalid_ref,
                hs_ref, ms_ref, wup_ref, wgate_ref, wdown_ref, y3_ref, acc_ref):
    v = pl.program_id(0)
    j = pl.program_id(1)
    valid = vvalid_ref[v] == 1
    lo = vlo_ref[v]
    hi = vhi_ref[v]
    e0 = vg_ref[v] * EXP_PER_GROUP + j * MOE_EPS
    full = (hi - lo) * 4 >= TM_MOE * 3

    @pl.when(valid & (j == 0) & (vfirst_ref[v] == 1))
    def _():
        acc_ref[...] = jnp.zeros_like(acc_ref)

    def expert_rows(r0, rows):
        w_in2 = jnp.concatenate(
            [w[k].astype(BF16) for k in range(MOE_EPS) for w in (wup_ref, wgate_ref)], axis=1)
        ag = _dot(hs_ref[r0:r0 + rows, :], w_in2)
        comb = ms_ref[r0:r0 + rows, :]
        lane = lax.broadcasted_iota(jnp.int32, comb.shape, 1)
        acts = []
        for k in range(MOE_EPS):
            a = ag[:, 2 * k * D_EXPERT:(2 * k + 1) * D_EXPERT]
            g = ag[:, (2 * k + 1) * D_EXPERT:(2 * k + 2) * D_EXPERT]
            cw = jnp.sum(jnp.where(lane == e0 + k, comb, 0.0), axis=-1, keepdims=True)
            acts.append(((g * jax.nn.sigmoid(g)) * a * cw).astype(BF16))
        w_out = jnp.concatenate([wdown_ref[k].astype(BF16) for k in range(MOE_EPS)], axis=0)
        acc_ref[r0:r0 + rows, :] += _dot(jnp.concatenate(acts, axis=1), w_out)

    @pl.when(valid & full)
    def _():
        expert_rows(0, TM_MOE)

    @pl.when(valid & jnp.logical_not(full))
    def _():
        for s in range(TM_MOE // MOE_SUB):
            r0 = s * MOE_SUB

            @pl.when((lo < r0 + MOE_SUB) & (hi > r0))
            def _():
                expert_rows(r0, MOE_SUB)

    @pl.when(valid & (j == EXP_PER_GROUP // MOE_EPS - 1) & (vlast_ref[v] == 1))
    def _():
        for c in range(N_SLAB):
            y3_ref[pl.ds(c, TM_MOE, stride=N_SLAB), :] = acc_ref[:, LANE * c:LANE * (c + 1)]


def _moe(sched, hs, ms, w_up, w_gate, w_down):
    steps = EXP_PER_GROUP // MOE_EPS
    wmap = lambda v, j, vt, vg, vlo, vhi, vfirst, vlast, vvalid: (
        vg[v] * steps + jnp.where(vvalid[v] == 1, j, steps - 1), 0, 0)
    tmap = lambda v, j, vt, *_: (vt[v], 0)
    n_slab = D_MODEL // LANE
    return pl.pallas_call(
        _moe_kernel,
        grid_spec=pltpu.PrefetchScalarGridSpec(
            num_scalar_prefetch=7,
            grid=(N_VISITS, steps),
            in_specs=[
                pl.BlockSpec((TM_MOE, D_MODEL), tmap),
                pl.BlockSpec((TM_MOE, LANE), tmap),
                pl.BlockSpec((MOE_EPS, D_MODEL, D_EXPERT), wmap),
                pl.BlockSpec((MOE_EPS, D_MODEL, D_EXPERT), wmap),
                pl.BlockSpec((MOE_EPS, D_EXPERT, D_MODEL), wmap),
            ],
            out_specs=pl.BlockSpec((TM_MOE * n_slab, LANE), tmap),
            scratch_shapes=[pltpu.VMEM((TM_MOE, D_MODEL), F32)],
        ),
        out_shape=jax.ShapeDtypeStruct((T * n_slab, LANE), F32),
        compiler_params=pltpu.CompilerParams(
            dimension_semantics=("arbitrary", "arbitrary"), vmem_limit_bytes=VMEM_LIMIT),
        name="moe_grouped",
    )(*sched, hs, ms, w_up, w_gate, w_down)


def _final_kernel(pos_ref, ys_ref, x1_ref, mod_ref, fn_ref, yp_ref, ysm_ref, g_ref):
    i = pl.program_id(0)
    base = i * TM_FINAL

    def body(r, carry):
        g_ref[_slab(r), :] = ys_ref[_slab(pos_ref[base + r]), :]
        return carry

    lax.fori_loop(0, TM_FINAL, body, 0, unroll=8)
    mod = mod_ref[pl.ds(_mod_row(i, TM_FINAL), 1), :]
    gate2 = mod[:, 5 * D_MODEL:6 * D_MODEL]
    moe = jnp.concatenate([g_ref[pl.ds(c, TM_FINAL, stride=N_SLAB), :] for c in range(N_SLAB)], axis=1)
    x2 = x1_ref[...] + gate2 * moe
    y = (x2 * _rms(x2)) * fn_ref[...]
    is_sample = i >= T_P // TM_FINAL

    @pl.when(jnp.logical_not(is_sample))
    def _():
        yp_ref[...] = y

    @pl.when(is_sample)
    def _():
        ysm_ref[...] = y


def _final(pos, ys, x1, mod, final_norm):
    n_slab = D_MODEL // LANE
    pmap, smap = _stream_maps(TM_FINAL)
    return pl.pallas_call(
        _final_kernel,
        grid_spec=pltpu.PrefetchScalarGridSpec(
            num_scalar_prefetch=1,
            grid=(T // TM_FINAL,),
            in_specs=[
                pl.BlockSpec((T * n_slab, LANE), lambda i, pos: (0, 0), pipeline_mode=pl.Buffered(1)),
                pl.BlockSpec((TM_FINAL, D_MODEL), lambda i, pos: (i, 0)),
                pl.BlockSpec((N_COND, 6 * D_MODEL), lambda i, pos: (0, 0)),
                pl.BlockSpec((1, D_MODEL), lambda i, pos: (0, 0)),
            ],
            out_specs=[pl.BlockSpec((TM_FINAL, D_MODEL), pmap),
                       pl.BlockSpec((TM_FINAL, D_MODEL), smap)],
            scratch_shapes=[pltpu.VMEM((TM_FINAL * n_slab, LANE), F32)],
        ),
        out_shape=[jax.ShapeDtypeStruct((T_P, D_MODEL), F32),
                   jax.ShapeDtypeStruct((T_S, D_MODEL), F32)],
        compiler_params=pltpu.CompilerParams(
            dimension_semantics=("arbitrary",), vmem_limit_bytes=VMEM_LIMIT),
        name="moe_unsort_final",
    )(pos, ys, x1, mod, final_norm)


def _schedule(meta, cnt):
    n_tiles_post = T // TM_POST
    counts = cnt[:, 0, N_EXPERTS:N_EXPERTS + N_GROUPS].astype(jnp.int32)
    gtot = jnp.sum(counts, axis=0)
    goff = jnp.cumsum(gtot) - gtot
    tile_base = goff[None, :] + jnp.cumsum(counts, axis=0) - counts
    gid = meta[:, GID_LANE].astype(jnp.int32).reshape(n_tiles_post, TM_POST)
    rank = meta[:, RANK_LANE].astype(jnp.int32).reshape(n_tiles_post, TM_POST)
    pos = rank
    for grp in range(N_GROUPS):
        pos = pos + jnp.where(gid == grp, tile_base[:, grp:grp + 1], 0)
    pos = pos.reshape(T)

    n_tiles = T // TM_MOE
    t_lo = (jnp.arange(n_tiles, dtype=jnp.int32) * TM_MOE)[:, None]
    lo = jnp.clip(goff[None, :] - t_lo, 0, TM_MOE)
    hi = jnp.clip(goff[None, :] + gtot[None, :] - t_lo, 0, TM_MOE)
    ok = (hi > lo).reshape(-1)
    slot = jnp.cumsum(ok.astype(jnp.int32)) - 1
    n_ok = slot[-1] + 1
    sel = (slot[None, :] == jnp.arange(N_VISITS, dtype=jnp.int32)[:, None]) & ok[None, :]

    def pick(vals):
        return jnp.sum(jnp.where(sel, vals.reshape(-1)[None, :], 0), axis=1).astype(jnp.int32)

    pair_tile = jnp.broadcast_to(jnp.arange(n_tiles, dtype=jnp.int32)[:, None], (n_tiles, N_GROUPS))
    pair_group = jnp.broadcast_to(jnp.arange(N_GROUPS, dtype=jnp.int32)[None, :], (n_tiles, N_GROUPS))
    vt, vg, vlo, vhi = pick(pair_tile), pick(pair_group), pick(lo), pick(hi)
    valid = jnp.arange(N_VISITS, dtype=jnp.int32) < n_ok
    last = jnp.maximum(n_ok - 1, 0)
    vt = jnp.where(valid, vt, vt[last])
    vg = jnp.where(valid, vg, vg[last])
    change = (vt[1:] != vt[:-1]).astype(jnp.int32)
    one = jnp.ones((1,), jnp.int32)
    first = jnp.concatenate([one, change])
    idx = jnp.arange(N_VISITS, dtype=jnp.int32)
    last = jnp.where(idx == n_ok - 1, 1, jnp.concatenate([change, one]))
    return pos, (vt, vg, vlo, vhi, first, last, valid.astype(jnp.int32))


def _rope_tables():
    n = np.arange(DEC_SEQ)
    pos = np.stack([n // GRID_W, n % GRID_W], axis=1).astype(np.float32)
    half = ROPE_AXIS // 2
    inv = (1.0 / (ROPE_BASE ** (np.arange(0, ROPE_AXIS, 2, dtype=np.float32) / ROPE_AXIS))).astype(np.float32)
    ang = (pos[:, :, None] * inv[None, None, :]).astype(np.float32)
    cos = np.cos(ang).astype(np.float32)
    sin = np.sin(ang).astype(np.float32)
    tabs = np.zeros((2, 3, DEC_SEQ, LANE), np.float32)
    tabs[:, 0] = 1.0
    for a in range(2):
        lo = ROPE_LANE0 + a * ROPE_AXIS
        tabs[1, 0, :, lo:lo + half] = cos[:, a]
        tabs[1, 0, :, lo + half:lo + 2 * half] = cos[:, a]
        tabs[1, 1, :, lo + half:lo + 2 * half] = sin[:, a]
        tabs[1, 2, :, lo:lo + half] = -sin[:, a]
    return jnp.asarray(tabs)


def kernel(x_prompt, x_sample, cache_ckv, cache_krope, c, c_ctx, norm1, w_ada, b_ada, w_in, conv_w,
           w_conv_out, q_norm, w_uq, kv_norm, w_ukv, w_o, w_mix_out, norm2, w_grp, w_exp, w_up,
           w_gate, w_down, final_norm):
    l = 0
    xp = x_prompt.reshape(T_P, D_MODEL)
    xs = x_sample.reshape(T_S, D_MODEL)
    cond = jnp.concatenate(
        [c_ctx[None, :], c, jnp.zeros((N_COND - 1 - DEC_BATCH, D_MODEL), F32)], axis=0)
    mod = _ada(cond, w_ada[l], b_ada[l][None, :])

    w_in_t = w_in[l].T
    w_conv3_t, w_small_t, w_gates_t, w_co_b, w_o_b, w_mix_b = _prep_weights(
        w_in_t, w_conv_out[l], w_o[l], w_mix_out[l])
    w_uq_slot = jnp.pad(w_uq[l].reshape(Q_LORA, N_HEADS, QK_NOPE + QK_ROPE),
                        ((0, 0), (0, 0), (0, LANE - QK_NOPE - QK_ROPE))
                        ).reshape(Q_LORA, N_HEADS * LANE).astype(BF16)
    wkv = w_ukv[l].reshape(KV_LORA, N_HEADS, QK_NOPE + V_HEAD)
    wk_slot = jnp.pad(wkv[:, :, :QK_NOPE], ((0, 0), (0, 0), (0, LANE - QK_NOPE)))
    wv = wkv[:, :, QK_NOPE:].reshape(KV_LORA, N_HEADS // 2, 2, V_HEAD)
    zero = jnp.zeros_like(wv[:, :, 0])
    wv_slot = jnp.stack([jnp.concatenate([wv[:, :, 0], zero], axis=-1),
                         jnp.concatenate([zero, wv[:, :, 1]], axis=-1)], axis=2)
    w_ukv_slot = jnp.concatenate([wk_slot.reshape(KV_LORA, N_HEADS * LANE),
                                  wv_slot.reshape(KV_LORA, N_HEADS * LANE)], axis=1).astype(BF16)
    w_route = jnp.pad(jnp.concatenate([w_exp[l], w_grp[l]], axis=1),
                      ((0, 0), (0, LANE - N_EXPERTS - N_GROUPS)))
    w_route_hi = w_route.astype(BF16)
    w_route_lo = (w_route - w_route_hi.astype(F32)).astype(BF16)
    w_route2 = jnp.concatenate([w_route_hi, w_route_lo], axis=1)
    cache_krs = jnp.pad(cache_krope[:, l], ((0, 0), (0, 0), (ROPE_LANE0, LANE - ROPE_LANE0 - QK_ROPE)))

    zc, q, ckv, krs = _inproj(xp, xs, mod, norm1[l][None, :], w_conv3_t, w_small_t, conv_w[l],
                              q_norm[l][None, :], kv_norm[l][None, :], w_uq_slot, _rope_tables())
    o = _attention(q, ckv, krs, cache_ckv[:, l], cache_krs, w_ukv_slot)
    x1, h3, meta, cnt = _post(xp, xs, mod, norm1[l][None, :], w_gates_t, zc, o, w_co_b, w_o_b, w_mix_b,
                              norm2[l][None, :], w_route2)
    pos, sched = _schedule(meta, cnt)
    hs, ms = _dispatch(pos, h3, meta)
    ys = _moe(sched, hs, ms, w_up[l], w_gate[l], w_down[l])
    yp, ysm = _final(pos, ys, x1, mod, final_norm[None, :])

    y_prompt = yp.reshape(BATCH, SEQ, D_MODEL)
    y_sample = ysm.reshape(DEC_BATCH, DEC_SEQ, D_MODEL)
    new_ckv = ckv[:T_P].reshape(BATCH, 1, SEQ, KV_LORA)
    new_krope = krs[:T_P, ROPE_LANE0:ROPE_LANE0 + QK_ROPE].reshape(BATCH, 1, SEQ, QK_ROPE)
    return (y_prompt, y_sample, new_ckv, new_krope)
```

```python
import functools

import numpy as np
import jax
import jax.numpy as jnp
from jax import lax
from jax.experimental import pallas as pl
from jax.experimental.pallas import tpu as pltpu

F32 = jnp.float32
BF16 = jnp.bfloat16

D_MODEL = 1024
BATCH = 16
SEQ = 256
DEC_BATCH = 2
DEC_SEQ = 1024
PAST_LEN = 256
GRID_W = 64
N_HEADS = 8
QK_NOPE = 64
QK_ROPE = 32
V_HEAD = 64
Q_LORA = 256
KV_LORA = 128
ROPE_AXIS = QK_ROPE // 2
ROPE_BASE = 10000.0
ATTN_SCALE = (QK_NOPE + QK_ROPE) ** -0.5
D_CONV = D_MODEL
N_GROUPS = 4
EXP_PER_GROUP = 8
N_EXPERTS = N_GROUPS * EXP_PER_GROUP
D_EXPERT = 256
EPS = 1e-6

T_P = BATCH * SEQ
T_S = DEC_BATCH * DEC_SEQ
T = T_P + T_S
N_COND = 8
LANE = 128
ROPE_LANE0 = QK_NOPE
SMALL_COLS = Q_LORA + KV_LORA + LANE
VMEM_LIMIT = 56 * 1024 * 1024

TM_IN = 1024
TM_POST = 512
TM_MOE = T // N_GROUPS
TM_FINAL = 512
N_SLAB = D_MODEL // LANE
MOE_SUB = 256
MOE_EPS = 2
N_VISITS = T // TM_MOE + N_GROUPS - 1
GID_LANE = 40
RANK_LANE = 41
Q_BLK = 256
CONV_CHUNK = 512


def _dot(a, b):
    return jnp.dot(a, b, preferred_element_type=F32)


def _rms(x):
    return lax.rsqrt(jnp.mean(x * x, axis=-1, keepdims=True) + EPS)


def _slab(t):
    return pl.ds(pl.multiple_of(t * N_SLAB, N_SLAB), N_SLAB)


def _mod_row(i, tm):
    n_prompt = T_P // tm
    return jnp.where(i >= n_prompt, 1 + ((i - n_prompt) * tm) // DEC_SEQ, 0)


def _ada_kernel(cond_ref, w_ref, b_ref, o_ref):
    c = cond_ref[...]
    a = (c * jax.nn.sigmoid(c)).astype(BF16)
    o_ref[...] = _dot(a, w_ref[...].astype(BF16)) + b_ref[...]


def _ada(cond, w_ada, b_ada):
    n = 6 * D_MODEL
    bn = 1536
    return pl.pallas_call(
        _ada_kernel,
        grid=(n // bn,),
        in_specs=[
            pl.BlockSpec((N_COND, D_MODEL), lambda j: (0, 0)),
            pl.BlockSpec((D_MODEL, bn), lambda j: (0, j)),
            pl.BlockSpec((1, bn), lambda j: (0, j)),
        ],
        out_specs=pl.BlockSpec((N_COND, bn), lambda j: (0, j)),
        out_shape=jax.ShapeDtypeStruct((N_COND, n), F32),
        compiler_params=pltpu.CompilerParams(
            dimension_semantics=("parallel",), vmem_limit_bytes=VMEM_LIMIT),
        name="ada_mod",
    )(cond, w_ada, b_ada)


O_CQ = 3 * D_CONV
O_KR = O_CQ + Q_LORA + KV_LORA
O_GATE = O_KR + QK_ROPE
PREP_STEPS = 4


IN_COLS = O_GATE + 2 * D_MODEL
TAIL_ROWS = IN_COLS // 2
SMALL_BLK = 512
NT = (((1,), (1,)), ((), ()))


def _dot_nt(a, bt):
    return lax.dot_general(a, bt, NT, preferred_element_type=F32)


def _prep_kernel(wct_ref, wst_ref, wtail_ref, wco_ref, wo_ref, wmix_ref,
                 c3_ref, sm_ref, g_ref, co_ref, o_ref, mix_ref):
    c3_ref[...] = wct_ref[...].astype(BF16)
    n_lat = Q_LORA + KV_LORA
    sm_ref[0:n_lat, :] = wst_ref[0:n_lat, :].astype(BF16)
    sm_ref[n_lat:, :] = jnp.zeros((LANE, sm_ref.shape[1]), BF16)
    sm_ref[n_lat + ROPE_LANE0:n_lat + ROPE_LANE0 + QK_ROPE, :] = (
        wst_ref[n_lat:n_lat + QK_ROPE, :].astype(BF16))
    g_ref[...] = wtail_ref[O_GATE - TAIL_ROWS:, :].astype(BF16)
    co_ref[...] = wco_ref[...].astype(BF16)
    o_ref[...] = wo_ref[...].astype(BF16)
    mix_ref[...] = wmix_ref[...].astype(BF16)


def _prep_weights(w_in_t, w_conv_out, w_o, w_mix_out):
    cb = D_MODEL // PREP_STEPS
    col = lambda rows, blk=0: pl.BlockSpec((rows, cb), lambda i: (blk, i))
    n_o = N_HEADS * V_HEAD
    return pl.pallas_call(
        _prep_kernel,
        grid=(PREP_STEPS,),
        in_specs=[col(O_CQ), col(SMALL_BLK, O_CQ // SMALL_BLK), col(TAIL_ROWS, 1),
                  col(D_CONV), col(n_o), col(D_MODEL)],
        out_specs=[col(O_CQ), col(SMALL_COLS), col(2 * D_MODEL), col(D_CONV), col(n_o), col(D_MODEL)],
        out_shape=[jax.ShapeDtypeStruct((O_CQ, D_MODEL), BF16),
                   jax.ShapeDtypeStruct((SMALL_COLS, D_MODEL), BF16),
                   jax.ShapeDtypeStruct((2 * D_MODEL, D_MODEL), BF16),
                   jax.ShapeDtypeStruct((D_CONV, D_MODEL), BF16),
                   jax.ShapeDtypeStruct((n_o, D_MODEL), BF16),
                   jax.ShapeDtypeStruct((D_MODEL, D_MODEL), BF16)],
        compiler_params=pltpu.CompilerParams(
            dimension_semantics=("parallel",), vmem_limit_bytes=VMEM_LIMIT),
        name="weight_casts",
    )(w_in_t, w_in_t, w_in_t, w_conv_out, w_o, w_mix_out)


def _stream_maps(tm):
    n_prompt = T_P // tm
    return (lambda i, *_: (jnp.minimum(i, n_prompt - 1), 0),
            lambda i, *_: (jnp.maximum(i - n_prompt, 0), 0))


def _inproj_kernel(xp_ref, xs_ref, mod_ref, n1_ref, wc_ref, ws_ref, cw_ref, qn_ref, kvn_ref, wuq_ref,
                   rope_ref, zc_ref, q_ref, ckv_ref, krs_ref, nckv_ref, nkr_ref):
    i = pl.program_id(0)
    is_sample = i >= T_P // TM_IN
    seq = jnp.where(is_sample, DEC_SEQ, SEQ)
    mod = mod_ref[pl.ds(_mod_row(i, TM_IN), 1), :]
    shift1 = mod[:, 0:D_MODEL]
    scale1 = mod[:, D_MODEL:2 * D_MODEL]
    x = jnp.where(is_sample, xs_ref[...], xp_ref[...])
    h = ((x * _rms(x)) * n1_ref[...]) * (1.0 + scale1) + shift1
    hb = h.astype(BF16)

    sm = _dot_nt(hb, ws_ref[...])
    cq = sm[:, 0:Q_LORA]
    ckv_raw = sm[:, Q_LORA:Q_LORA + KV_LORA]
    krs = sm[:, Q_LORA + KV_LORA:SMALL_COLS]
    cqn = (cq * _rms(cq)) * qn_ref[...]
    q = _dot(cqn.astype(BF16), wuq_ref[...])
    ckv = (ckv_raw * _rms(ckv_raw)) * kvn_ref[...]
    ckv_ref[...] = ckv

    @pl.when(jnp.logical_not(is_sample))
    def _():
        nckv_ref[...] = ckv
        nkr_ref[...] = krs[:, ROPE_LANE0:ROPE_LANE0 + QK_ROPE]

    cos = rope_ref[0]
    sin_lo = rope_ref[1]
    sin_hi = rope_ref[2]

    def rot(v):
        return v * cos + pltpu.roll(v, 8, 1) * sin_lo + pltpu.roll(v, LANE - 8, 1) * sin_hi

    krs_ref[...] = rot(krs)
    for hh in range(N_HEADS):
        q_ref[:, LANE * hh:LANE * (hh + 1)] = rot(q[:, LANE * hh:LANE * (hh + 1)]).astype(BF16)

    pos = lax.broadcasted_iota(jnp.int32, (TM_IN, 1), 0) & (seq - 1)
    first = pos == 0
    last = pos == seq - 1
    for j in range(D_CONV // CONV_CHUNK):
        c0 = j * CONV_CHUNK
        bg = _dot_nt(hb, wc_ref[c0:c0 + CONV_CHUNK, :])
        cg = _dot_nt(hb, wc_ref[D_CONV + c0:D_CONV + c0 + CONV_CHUNK, :])
        ui = _dot_nt(hb, wc_ref[2 * D_CONV + c0:2 * D_CONV + c0 + CONV_CHUNK, :])
        u = cg * ui
        u_prev = jnp.where(first, 0.0, pltpu.roll(u, 1, 0))
        u_next = jnp.where(last, 0.0, pltpu.roll(u, TM_IN - 1, 0))
        cw = cw_ref[:, c0:c0 + CONV_CHUNK]
        conv = u_prev * cw[0:1] + u * cw[1:2] + u_next * cw[2:3]
        zc_ref[:, c0:c0 + CONV_CHUNK] = (bg * conv).astype(BF16)


def _inproj(xp, xs, mod, norm1, w_conv3_t, w_small_t, conv_w, q_norm, kv_norm, w_uq_slot, rope_tabs):
    n_prompt = T_P // TM_IN
    const = lambda i: (0, 0)
    pmap, smap = _stream_maps(TM_IN)
    return pl.pallas_call(
        _inproj_kernel,
        grid=(T // TM_IN,),
        in_specs=[
            pl.BlockSpec((TM_IN, D_MODEL), pmap),
            pl.BlockSpec((TM_IN, D_MODEL), smap),
            pl.BlockSpec((N_COND, 6 * D_MODEL), const),
            pl.BlockSpec((1, D_MODEL), const),
            pl.BlockSpec((O_CQ, D_MODEL), const),
            pl.BlockSpec((SMALL_COLS, D_MODEL), const),
            pl.BlockSpec((3, D_CONV), const),
            pl.BlockSpec((1, Q_LORA), const),
            pl.BlockSpec((1, KV_LORA), const),
            pl.BlockSpec((Q_LORA, N_HEADS * LANE), const),
            pl.BlockSpec((None, 3, TM_IN, LANE),
                         lambda i: (jnp.where(i >= n_prompt, 1, 0), 0, 0, 0)),
        ],
        out_specs=[
            pl.BlockSpec((TM_IN, D_CONV), lambda i: (i, 0)),
            pl.BlockSpec((TM_IN, N_HEADS * LANE), lambda i: (i, 0)),
            pl.BlockSpec((TM_IN, KV_LORA), lambda i: (i, 0)),
            pl.BlockSpec((TM_IN, LANE), lambda i: (i, 0)),
            pl.BlockSpec((TM_IN, KV_LORA), pmap),
            pl.BlockSpec((TM_IN, QK_ROPE), pmap),
        ],
        out_shape=[
            jax.ShapeDtypeStruct((T, D_CONV), BF16),
            jax.ShapeDtypeStruct((T, N_HEADS * LANE), BF16),
            jax.ShapeDtypeStruct((T, KV_LORA), F32),
            jax.ShapeDtypeStruct((T, LANE), F32),
            jax.ShapeDtypeStruct((T_P, KV_LORA), F32),
            jax.ShapeDtypeStruct((T_P, QK_ROPE), F32),
        ],
        compiler_params=pltpu.CompilerParams(
            dimension_semantics=("arbitrary",), vmem_limit_bytes=VMEM_LIMIT),
        name="in_proj",
    )(xp, xs, mod, norm1, w_conv3_t, w_small_t, conv_w, q_norm, kv_norm, w_uq_slot, rope_tabs)


def _attn_kernel(*refs, n_src):
    q_ref = refs[0]
    ckv_refs = refs[1:1 + n_src]
    krs_refs = refs[1 + n_src:1 + 2 * n_src]
    wukv_ref = refs[1 + 2 * n_src]
    o_ref = refs[2 + 2 * n_src]
    kf_scr, v_scr = refs[3 + 2 * n_src:]

    @pl.when(pl.program_id(1) == 0)
    def _():
        off = 0
        for c_ref, k_ref in zip(ckv_refs, krs_refs):
            m = c_ref.shape[0]
            kv = _dot(c_ref[...].astype(BF16), wukv_ref[...])
            krs = k_ref[...]
            for hh in range(N_HEADS):
                kf_scr[hh, off:off + m, :] = (kv[:, LANE * hh:LANE * (hh + 1)] + krs).astype(BF16)
            v_scr[off:off + m, :] = kv[:, N_HEADS * LANE:].astype(BF16)
            off += m

    for pair in range(N_HEADS // 2):
        acc = None
        for hh in (2 * pair, 2 * pair + 1):
            qh = q_ref[:, LANE * hh:LANE * (hh + 1)]
            s = lax.dot_general(qh, kf_scr[hh], (((1,), (1,)), ((), ())),
                                preferred_element_type=F32) * ATTN_SCALE
            e = jnp.exp(s - jnp.max(s, axis=-1, keepdims=True))
            p = (e / jnp.sum(e, axis=-1, keepdims=True)).astype(BF16)
            part = _dot(p, v_scr[:, LANE * hh:LANE * (hh + 1)])
            acc = part if acc is None else acc + part
        o_ref[:, LANE * pair:LANE * (pair + 1)] = acc.astype(BF16)


def _attention(q, ckv, krs, cache_ckv, cache_krs, w_ukv_slot):
    kv_cols = 2 * N_HEADS * LANE
    cp = pltpu.CompilerParams(dimension_semantics=("parallel", "arbitrary"),
                              vmem_limit_bytes=VMEM_LIMIT)
    o_prompt = pl.pallas_call(
        functools.partial(_attn_kernel, n_src=1),
        grid=(BATCH, SEQ // Q_BLK),
        in_specs=[
            pl.BlockSpec((Q_BLK, N_HEADS * LANE), lambda b, j: (b * (SEQ // Q_BLK) + j, 0)),
            pl.BlockSpec((SEQ, KV_LORA), lambda b, j: (b, 0)),
            pl.BlockSpec((SEQ, LANE), lambda b, j: (b, 0)),
            pl.BlockSpec((KV_LORA, kv_cols), lambda b, j: (0, 0)),
        ],
        out_specs=pl.BlockSpec((Q_BLK, N_HEADS * V_HEAD), lambda b, j: (b * (SEQ // Q_BLK) + j, 0)),
        out_shape=jax.ShapeDtypeStruct((T_P, N_HEADS * V_HEAD), BF16),
        scratch_shapes=[pltpu.VMEM((N_HEADS, SEQ, LANE), BF16),
                        pltpu.VMEM((SEQ, N_HEADS * LANE), BF16)],
        compiler_params=cp,
        name="attn_prompt",
    )(q, ckv, krs, w_ukv_slot)

    m_all = DEC_SEQ + PAST_LEN
    nq = DEC_SEQ // Q_BLK
    q0 = T_P // Q_BLK
    s0 = T_P // DEC_SEQ
    o_sample = pl.pallas_call(
        functools.partial(_attn_kernel, n_src=2),
        grid=(DEC_BATCH, nq),
        in_specs=[
            pl.BlockSpec((Q_BLK, N_HEADS * LANE), lambda b, j: (q0 + b * nq + j, 0)),
            pl.BlockSpec((DEC_SEQ, KV_LORA), lambda b, j: (s0 + b, 0)),
            pl.BlockSpec((None, PAST_LEN, KV_LORA), lambda b, j: (b, 0, 0)),
            pl.BlockSpec((DEC_SEQ, LANE), lambda b, j: (s0 + b, 0)),
            pl.BlockSpec((None, PAST_LEN, LANE), lambda b, j: (b, 0, 0)),
            pl.BlockSpec((KV_LORA, kv_cols), lambda b, j: (0, 0)),
        ],
        out_specs=pl.BlockSpec((Q_BLK, N_HEADS * V_HEAD), lambda b, j: (b * nq + j, 0)),
        out_shape=jax.ShapeDtypeStruct((T_S, N_HEADS * V_HEAD), BF16),
        scratch_shapes=[pltpu.VMEM((N_HEADS, m_all, LANE), BF16),
                        pltpu.VMEM((m_all, N_HEADS * LANE), BF16)],
        compiler_params=cp,
        name="attn_sample",
    )(q, ckv, cache_ckv, krs, cache_krs, w_ukv_slot)
    return o_prompt, o_sample


def _route(logits):
    lane = lax.broadcasted_iota(jnp.int32, logits.shape, 1)
    neg = -jnp.inf
    big = jnp.int32(1 << 20)
    gmask = (lane >= N_EXPERTS) & (lane < N_EXPERTS + N_GROUPS)
    gl = jnp.where(gmask, logits, neg)
    gmax = jnp.max(gl, axis=-1, keepdims=True)
    gsum = jnp.sum(jnp.where(gmask, jnp.exp(gl - gmax), 0.0), axis=-1, keepdims=True)
    p_g = 1.0 / gsum
    g_idx = jnp.min(jnp.where(gl == gmax, lane, big), axis=-1, keepdims=True) - N_EXPERTS

    emask = (lane < N_EXPERTS) & ((lane >> 3) == g_idx)
    el = jnp.where(emask, logits, neg)
    m1 = jnp.max(el, axis=-1, keepdims=True)
    i1 = jnp.min(jnp.where(el == m1, lane, big), axis=-1, keepdims=True)
    el2 = jnp.where(lane == i1, neg, el)
    m2 = jnp.max(el2, axis=-1, keepdims=True)
    i2 = jnp.min(jnp.where(el2 == m2, lane, big), axis=-1, keepdims=True)
    z = jnp.sum(jnp.where(emask, jnp.exp(el - m1), 0.0), axis=-1, keepdims=True)
    p1 = 1.0 / z
    p2 = jnp.exp(m2 - m1) / z
    tot = p1 + p2
    w1 = p_g * p1 / tot
    w2 = p_g * p2 / tot
    return jnp.where(lane == i1, w1, 0.0) + jnp.where(lane == i2, w2, 0.0), g_idx


def _post_kernel(xp_ref, xs_ref, mod_ref, n1_ref, wg_ref, zc_ref, op_ref, os_ref, wco_ref, wo_ref,
                 wmix_ref, n2_ref, wr_ref, x1_ref, h3_ref, meta_ref, cnt_ref, gr_ref):
    i = pl.program_id(0)
    is_sample = i >= T_P // TM_POST
    x = jnp.where(is_sample, xs_ref[...], xp_ref[...])
    o = jnp.where(is_sample, os_ref[...], op_ref[...])
    mod = mod_ref[pl.ds(_mod_row(i, TM_POST), 1), :]
    shift1 = mod[:, 0:D_MODEL]
    scale1 = mod[:, D_MODEL:2 * D_MODEL]
    gate1 = mod[:, 2 * D_MODEL:3 * D_MODEL]
    shift2 = mod[:, 3 * D_MODEL:4 * D_MODEL]
    scale2 = mod[:, 4 * D_MODEL:5 * D_MODEL]
    h = ((x * _rms(x)) * n1_ref[...]) * (1.0 + scale1) + shift1
    g = _dot_nt(h.astype(BF16), wg_ref[...])
    y_conv = _dot(zc_ref[...], wco_ref[...])
    y_mla = _dot(o, wo_ref[...])
    merged = (jax.nn.sigmoid(g[:, 0:D_MODEL]) * y_conv
              + jax.nn.sigmoid(g[:, D_MODEL:2 * D_MODEL]) * y_mla)
    y = _dot(merged.astype(BF16), wmix_ref[...])
    x1 = x + gate1 * y
    x1_ref[...] = x1
    h2 = ((x1 * _rms(x1)) * n2_ref[...]) * (1.0 + scale2) + shift2
    h2_hi = h2.astype(BF16)
    h2_lo = (h2 - h2_hi.astype(F32)).astype(BF16)
    hh = _dot(h2_hi, wr_ref[...])
    logits = hh[:, 0:LANE] + hh[:, LANE:2 * LANE] + _dot(h2_lo, wr_ref[:, 0:LANE])
    comb, g_idx = _route(logits)

    lane = lax.broadcasted_iota(jnp.int32, comb.shape, 1)
    onehot = lane == g_idx + N_EXPERTS
    r_i = lax.broadcasted_iota(jnp.int32, (TM_POST, TM_POST), 0)
    c_i = lax.broadcasted_iota(jnp.int32, (TM_POST, TM_POST), 1)
    lower = jnp.where(c_i < r_i, 1.0, 0.0).astype(BF16)
    before = _dot(lower, jnp.where(onehot, 1.0, 0.0).astype(BF16))
    rank = jnp.sum(jnp.where(onehot, before, 0.0), axis=-1, keepdims=True)
    counts = jnp.sum(jnp.where(onehot, 1.0, 0.0), axis=0, keepdims=True)
    cnt_ref[...] = jnp.broadcast_to(counts, cnt_ref.shape)

    meta_ref[...] = comb
    idx = jnp.where(lane == GID_LANE, g_idx.astype(F32), 0.0) + jnp.where(lane == RANK_LANE, rank, 0.0)
    idx_hi = idx.astype(BF16)
    idx_lo = (idx - idx_hi.astype(F32)).astype(BF16)
    s_row = lax.broadcasted_iota(jnp.int32, (8, LANE), 0)
    s_lane = lax.broadcasted_iota(jnp.int32, (8, LANE), 1)
    sel = jnp.where(s_lane == GID_LANE + s_row, 1.0, 0.0).astype(BF16)
    gr_ref[...] = _dot_nt(sel, idx_hi) + _dot_nt(sel, idx_lo)
    for c in range(N_SLAB):
        h3_ref[pl.ds(c, TM_POST, stride=N_SLAB), :] = h2[:, LANE * c:LANE * (c + 1)]


def _post(xp, xs, mod, norm1, w_gates_t, zc, o_p, o_s, w_conv_out, w_o, w_mix_out, norm2, w_route):
    const = lambda i: (0, 0)
    row = lambda i: (i, 0)
    pmap, smap = _stream_maps(TM_POST)
    return pl.pallas_call(
        _post_kernel,
        grid=(T // TM_POST,),
        in_specs=[
            pl.BlockSpec((TM_POST, D_MODEL), pmap),
            pl.BlockSpec((TM_POST, D_MODEL), smap),
            pl.BlockSpec((N_COND, 6 * D_MODEL), const),
            pl.BlockSpec((1, D_MODEL), const),
            pl.BlockSpec((2 * D_MODEL, D_MODEL), const),
            pl.BlockSpec((TM_POST, D_CONV), row),
            pl.BlockSpec((TM_POST, N_HEADS * V_HEAD), pmap),
            pl.BlockSpec((TM_POST, N_HEADS * V_HEAD), smap),
            pl.BlockSpec((D_CONV, D_MODEL), const),
            pl.BlockSpec((N_HEADS * V_HEAD, D_MODEL), const),
            pl.BlockSpec((D_MODEL, D_MODEL), const),
            pl.BlockSpec((1, D_MODEL), const),
            pl.BlockSpec((D_MODEL, 2 * LANE), const),
        ],
        out_specs=[
            pl.BlockSpec((TM_POST, D_MODEL), row),
            pl.BlockSpec((TM_POST * N_SLAB, LANE), row),
            pl.BlockSpec((TM_POST, LANE), row),
            pl.BlockSpec((None, 8, LANE), lambda i: (i, 0, 0)),
            pl.BlockSpec((8, TM_POST), lambda i: (0, i)),
        ],
        out_shape=[
            jax.ShapeDtypeStruct((T, D_MODEL), F32),
            jax.ShapeDtypeStruct((T * N_SLAB, LANE), F32),
            jax.ShapeDtypeStruct((T, LANE), F32),
            jax.ShapeDtypeStruct((T // TM_POST, 8, LANE), F32),
            jax.ShapeDtypeStruct((8, T), F32),
        ],
        compiler_params=pltpu.CompilerParams(
            dimension_semantics=("parallel",), vmem_limit_bytes=VMEM_LIMIT),
        name="post_mixer",
    )(xp, xs, mod, norm1, w_gates_t, zc, o_p, o_s, w_conv_out, w_o, w_mix_out, norm2, w_route)


def _dispatch_kernel(pos_ref, h3_ref, m_ref, hs_ref, ms_ref, src_ref, xg_ref):
    i = pl.program_id(0)

    @pl.when(i == 0)
    def _():
        def invert(t, carry):
            src_ref[pos_ref[t]] = t
            return carry

        lax.fori_loop(0, T, invert, 0, unroll=8)

    base = i * TM_MOE

    def body(r, carry):
        tok = src_ref[base + r]
        xg_ref[_slab(r), :] = h3_ref[_slab(tok), :]
        ms_ref[pl.ds(r, 1), :] = m_ref[pl.ds(tok, 1), :]
        return carry

    lax.fori_loop(0, TM_MOE, body, 0, unroll=8)
    for c in range(N_SLAB):
        hs_ref[:, LANE * c:LANE * (c + 1)] = xg_ref[pl.ds(c, TM_MOE, stride=N_SLAB), :].astype(BF16)


def _dispatch(pos, h3, meta):
    n_slab = N_SLAB
    return pl.pallas_call(
        _dispatch_kernel,
        grid_spec=pltpu.PrefetchScalarGridSpec(
            num_scalar_prefetch=1,
            grid=(T // TM_MOE,),
            in_specs=[pl.BlockSpec((T * n_slab, LANE), lambda i, pos: (0, 0),
                                   pipeline_mode=pl.Buffered(1)),
                      pl.BlockSpec((T, LANE), lambda i, pos: (0, 0), pipeline_mode=pl.Buffered(1))],
            out_specs=[pl.BlockSpec((TM_MOE, D_MODEL), lambda i, pos: (i, 0)),
                       pl.BlockSpec((TM_MOE, LANE), lambda i, pos: (i, 0))],
            scratch_shapes=[pltpu.SMEM((T,), jnp.int32),
                            pltpu.VMEM((TM_MOE * n_slab, LANE), F32)],
        ),
        out_shape=[jax.ShapeDtypeStruct((T, D_MODEL), BF16),
                   jax.ShapeDtypeStruct((T, LANE), F32)],
        compiler_params=pltpu.CompilerParams(
            dimension_semantics=("arbitrary",), vmem_limit_bytes=VMEM_LIMIT),
        name="moe_dispatch",
    )(pos, h3, meta)


def _moe_kernel(vt_ref, vg_ref, vlo_ref, vhi_ref, vfirst_ref, vlast_ref, vvalid_ref,
                hs_ref, ms_ref, wup_ref, wgate_ref, wdown_ref, y3_ref, acc_ref):
    v = pl.program_id(0)
    j = pl.program_id(1)
    valid = vvalid_ref[v] == 1
    lo = vlo_ref[v]
    hi = vhi_ref[v]
    e0 = vg_ref[v] * EXP_PER_GROUP + j * MOE_EPS
    full = (hi - lo) * 4 >= TM_MOE * 3

    @pl.when(valid & (j == 0) & (vfirst_ref[v] == 1))
    def _():
        acc_ref[...] = jnp.zeros_like(acc_ref)

    def expert_rows(r0, rows):
        w_in2 = jnp.concatenate(
            [w[k].astype(BF16) for k in range(MOE_EPS) for w in (wup_ref, wgate_ref)], axis=1)
        ag = _dot(hs_ref[r0:r0 + rows, :], w_in2)
        comb = ms_ref[r0:r0 + rows, :]
        lane = lax.broadcasted_iota(jnp.int32, comb.shape, 1)
        acts = []
        for k in range(MOE_EPS):
            a = ag[:, 2 * k * D_EXPERT:(2 * k + 1) * D_EXPERT]
            g = ag[:, (2 * k + 1) * D_EXPERT:(2 * k + 2) * D_EXPERT]
            cw = jnp.sum(jnp.where(lane == e0 + k, comb, 0.0), axis=-1, keepdims=True)
            acts.append(((g * jax.nn.sigmoid(g)) * a * cw).astype(BF16))
        w_out = jnp.concatenate([wdown_ref[k].astype(BF16) for k in range(MOE_EPS)], axis=0)
        acc_ref[r0:r0 + rows, :] += _dot(jnp.concatenate(acts, axis=1), w_out)

    @pl.when(valid & full)
    def _():
        expert_rows(0, TM_MOE)

    @pl.when(valid & jnp.logical_not(full))
    def _():
        for s in range(TM_MOE // MOE_SUB):
            r0 = s * MOE_SUB

            @pl.when((lo < r0 + MOE_SUB) & (hi > r0))
            def _():
                expert_rows(r0, MOE_SUB)

    @pl.when(valid & (j == EXP_PER_GROUP // MOE_EPS - 1) & (vlast_ref[v] == 1))
    def _():
        for c in range(N_SLAB):
            y3_ref[pl.ds(c, TM_MOE, stride=N_SLAB), :] = acc_ref[:, LANE * c:LANE * (c + 1)]


def _moe(sched, hs, ms, w_up, w_gate, w_down):
    steps = EXP_PER_GROUP // MOE_EPS
    wmap = lambda v, j, vt, vg, vlo, vhi, vfirst, vlast, vvalid: (
        vg[v] * steps + jnp.where(vvalid[v] == 1, j, steps - 1), 0, 0)
    tmap = lambda v, j, vt, *_: (vt[v], 0)
    n_slab = D_MODEL // LANE
    return pl.pallas_call(
        _moe_kernel,
        grid_spec=pltpu.PrefetchScalarGridSpec(
            num_scalar_prefetch=7,
            grid=(N_VISITS, steps),
            in_specs=[
                pl.BlockSpec((TM_MOE, D_MODEL), tmap),
                pl.BlockSpec((TM_MOE, LANE), tmap),
                pl.BlockSpec((MOE_EPS, D_MODEL, D_EXPERT), wmap),
                pl.BlockSpec((MOE_EPS, D_MODEL, D_EXPERT), wmap),
                pl.BlockSpec((MOE_EPS, D_EXPERT, D_MODEL), wmap),
            ],
            out_specs=pl.BlockSpec((TM_MOE * n_slab, LANE), tmap),
            scratch_shapes=[pltpu.VMEM((TM_MOE, D_MODEL), F32)],
        ),
        out_shape=jax.ShapeDtypeStruct((T * n_slab, LANE), F32),
        compiler_params=pltpu.CompilerParams(
            dimension_semantics=("arbitrary", "arbitrary"), vmem_limit_bytes=VMEM_LIMIT),
        name="moe_grouped",
    )(*sched, hs, ms, w_up, w_gate, w_down)


def _final_kernel(pos_ref, ys_ref, x1_ref, mod_ref, fn_ref, yp_ref, ysm_ref, g_ref):
    i = pl.program_id(0)
    base = i * TM_FINAL

    def body(r, carry):
        g_ref[_slab(r), :] = ys_ref[_slab(pos_ref[base + r]), :]
        return carry

    lax.fori_loop(0, TM_FINAL, body, 0, unroll=8)
    mod = mod_ref[pl.ds(_mod_row(i, TM_FINAL), 1), :]
    gate2 = mod[:, 5 * D_MODEL:6 * D_MODEL]
    moe = jnp.concatenate([g_ref[pl.ds(c, TM_FINAL, stride=N_SLAB), :] for c in range(N_SLAB)], axis=1)
    x2 = x1_ref[...] + gate2 * moe
    y = (x2 * _rms(x2)) * fn_ref[...]
    is_sample = i >= T_P // TM_FINAL

    @pl.when(jnp.logical_not(is_sample))
    def _():
        yp_ref[...] = y

    @pl.when(is_sample)
    def _():
        ysm_ref[...] = y


def _final(pos, ys, x1, mod, final_norm):
    n_slab = D_MODEL // LANE
    pmap, smap = _stream_maps(TM_FINAL)
    return pl.pallas_call(
        _final_kernel,
        grid_spec=pltpu.PrefetchScalarGridSpec(
            num_scalar_prefetch=1,
            grid=(T // TM_FINAL,),
            in_specs=[
                pl.BlockSpec((T * n_slab, LANE), lambda i, pos: (0, 0), pipeline_mode=pl.Buffered(1)),
                pl.BlockSpec((TM_FINAL, D_MODEL), lambda i, pos: (i, 0)),
                pl.BlockSpec((N_COND, 6 * D_MODEL), lambda i, pos: (0, 0)),
                pl.BlockSpec((1, D_MODEL), lambda i, pos: (0, 0)),
            ],
            out_specs=[pl.BlockSpec((TM_FINAL, D_MODEL), pmap),
                       pl.BlockSpec((TM_FINAL, D_MODEL), smap)],
            scratch_shapes=[pltpu.VMEM((TM_FINAL * n_slab, LANE), F32)],
        ),
        out_shape=[jax.ShapeDtypeStruct((T_P, D_MODEL), F32),
                   jax.ShapeDtypeStruct((T_S, D_MODEL), F32)],
        compiler_params=pltpu.CompilerParams(
            dimension_semantics=("arbitrary",), vmem_limit_bytes=VMEM_LIMIT),
        name="moe_unsort_final",
    )(pos, ys, x1, mod, final_norm)


def _schedule(gr, cnt):
    n_tiles_post = T // TM_POST
    counts = cnt[:, 0, N_EXPERTS:N_EXPERTS + N_GROUPS].astype(jnp.int32)
    gtot = jnp.sum(counts, axis=0)
    goff = jnp.cumsum(gtot) - gtot
    tile_base = goff[None, :] + jnp.cumsum(counts, axis=0) - counts
    gid = gr[0].astype(jnp.int32).reshape(n_tiles_post, TM_POST)
    rank = gr[1].astype(jnp.int32).reshape(n_tiles_post, TM_POST)
    pos = rank
    for grp in range(N_GROUPS):
        pos = pos + jnp.where(gid == grp, tile_base[:, grp:grp + 1], 0)
    pos = pos.reshape(T)

    n_tiles = T // TM_MOE
    t_lo = (jnp.arange(n_tiles, dtype=jnp.int32) * TM_MOE)[:, None]
    lo = jnp.clip(goff[None, :] - t_lo, 0, TM_MOE)
    hi = jnp.clip(goff[None, :] + gtot[None, :] - t_lo, 0, TM_MOE)
    ok = (hi > lo).reshape(-1)
    slot = jnp.cumsum(ok.astype(jnp.int32)) - 1
    n_ok = slot[-1] + 1
    sel = (slot[None, :] == jnp.arange(N_VISITS, dtype=jnp.int32)[:, None]) & ok[None, :]

    def pick(vals):
        return jnp.sum(jnp.where(sel, vals.reshape(-1)[None, :], 0), axis=1).astype(jnp.int32)

    pair_tile = jnp.broadcast_to(jnp.arange(n_tiles, dtype=jnp.int32)[:, None], (n_tiles, N_GROUPS))
    pair_group = jnp.broadcast_to(jnp.arange(N_GROUPS, dtype=jnp.int32)[None, :], (n_tiles, N_GROUPS))
    vt, vg, vlo, vhi = pick(pair_tile), pick(pair_group), pick(lo), pick(hi)
    valid = jnp.arange(N_VISITS, dtype=jnp.int32) < n_ok
    last = jnp.maximum(n_ok - 1, 0)
    vt = jnp.where(valid, vt, vt[last])
    vg = jnp.where(valid, vg, vg[last])
    change = (vt[1:] != vt[:-1]).astype(jnp.int32)
    one = jnp.ones((1,), jnp.int32)
    first = jnp.concatenate([one, change])
    idx = jnp.arange(N_VISITS, dtype=jnp.int32)
    last = jnp.where(idx == n_ok - 1, 1, jnp.concatenate([change, one]))
    return pos, (vt, vg, vlo, vhi, first, last, valid.astype(jnp.int32))


def _rope_tables():
    n = np.arange(DEC_SEQ)
    pos = np.stack([n // GRID_W, n % GRID_W], axis=1).astype(np.float32)
    half = ROPE_AXIS // 2
    inv = (1.0 / (ROPE_BASE ** (np.arange(0, ROPE_AXIS, 2, dtype=np.float32) / ROPE_AXIS))).astype(np.float32)
    ang = (pos[:, :, None] * inv[None, None, :]).astype(np.float32)
    cos = np.cos(ang).astype(np.float32)
    sin = np.sin(ang).astype(np.float32)
    tabs = np.zeros((2, 3, DEC_SEQ, LANE), np.float32)
    tabs[:, 0] = 1.0
    for a in range(2):
        lo = ROPE_LANE0 + a * ROPE_AXIS
        tabs[1, 0, :, lo:lo + half] = cos[:, a]
        tabs[1, 0, :, lo + half:lo + 2 * half] = cos[:, a]
        tabs[1, 1, :, lo + half:lo + 2 * half] = sin[:, a]
        tabs[1, 2, :, lo:lo + half] = -sin[:, a]
    return jnp.asarray(tabs)


def kernel(x_prompt, x_sample, cache_ckv, cache_krope, c, c_ctx, norm1, w_ada, b_ada, w_in, conv_w,
           w_conv_out, q_norm, w_uq, kv_norm, w_ukv, w_o, w_mix_out, norm2, w_grp, w_exp, w_up,
           w_gate, w_down, final_norm):
    l = 0
    xp = x_prompt.reshape(T_P, D_MODEL)
    xs = x_sample.reshape(T_S, D_MODEL)
    cond = jnp.concatenate(
        [c_ctx[None, :], c, jnp.zeros((N_COND - 1 - DEC_BATCH, D_MODEL), F32)], axis=0)
    mod = _ada(cond, w_ada[l], b_ada[l][None, :])

    w_in_t = w_in[l].T
    w_conv3_t, w_small_t, w_gates_t, w_co_b, w_o_b, w_mix_b = _prep_weights(
        w_in_t, w_conv_out[l], w_o[l], w_mix_out[l])
    w_uq_slot = jnp.pad(w_uq[l].reshape(Q_LORA, N_HEADS, QK_NOPE + QK_ROPE),
                        ((0, 0), (0, 0), (0, LANE - QK_NOPE - QK_ROPE))
                        ).reshape(Q_LORA, N_HEADS * LANE).astype(BF16)
    wkv = w_ukv[l].reshape(KV_LORA, N_HEADS, QK_NOPE + V_HEAD)
    wk_slot = jnp.pad(wkv[:, :, :QK_NOPE], ((0, 0), (0, 0), (0, LANE - QK_NOPE)))
    wv = wkv[:, :, QK_NOPE:].reshape(KV_LORA, N_HEADS // 2, 2, V_HEAD)
    zero = jnp.zeros_like(wv[:, :, 0])
    wv_slot = jnp.stack([jnp.concatenate([wv[:, :, 0], zero], axis=-1),
                         jnp.concatenate([zero, wv[:, :, 1]], axis=-1)], axis=2)
    w_ukv_slot = jnp.concatenate([wk_slot.reshape(KV_LORA, N_HEADS * LANE),
                                  wv_slot.reshape(KV_LORA, N_HEADS * LANE)], axis=1).astype(BF16)
    w_route = jnp.pad(jnp.concatenate([w_exp[l], w_grp[l]], axis=1),
                      ((0, 0), (0, LANE - N_EXPERTS - N_GROUPS)))
    w_route_hi = w_route.astype(BF16)
    w_route_lo = (w_route - w_route_hi.astype(F32)).astype(BF16)
    w_route2 = jnp.concatenate([w_route_hi, w_route_lo], axis=1)
    cache_krs = jnp.pad(cache_krope[:, l], ((0, 0), (0, 0), (ROPE_LANE0, LANE - ROPE_LANE0 - QK_ROPE)))

    zc, q, ckv, krs, nckv, nkr = _inproj(xp, xs, mod, norm1[l][None, :], w_conv3_t, w_small_t, conv_w[l],
                              q_norm[l][None, :], kv_norm[l][None, :], w_uq_slot, _rope_tables())
    o_p, o_s = _attention(q, ckv, krs, cache_ckv[:, l], cache_krs, w_ukv_slot)
    x1, h3, meta, cnt, gr = _post(xp, xs, mod, norm1[l][None, :], w_gates_t, zc, o_p, o_s, w_co_b, w_o_b,
                                  w_mix_b, norm2[l][None, :], w_route2)
    pos, sched = _schedule(gr, cnt)
    hs, ms = _dispatch(pos, h3, meta)
    ys = _moe(sched, hs, ms, w_up[l], w_gate[l], w_down[l])
    yp, ysm = _final(pos, ys, x1, mod, final_norm[None, :])

    y_prompt = yp.reshape(BATCH, SEQ, D_MODEL)
    y_sample = ysm.reshape(DEC_BATCH, DEC_SEQ, D_MODEL)
    new_ckv = nckv.reshape(BATCH, 1, SEQ, KV_LORA)
    new_krope = nkr.reshape(BATCH, 1, SEQ, QK_ROPE)
    return (y_prompt, y_sample, new_ckv, new_krope)
```

```python
import functools

import numpy as np
import jax
import jax.numpy as jnp
from jax import lax
from jax.experimental import pallas as pl
from jax.experimental.pallas import tpu as pltpu

F32 = jnp.float32
BF16 = jnp.bfloat16

D_MODEL = 1024
BATCH = 16
SEQ = 256
DEC_BATCH = 2
DEC_SEQ = 1024
PAST_LEN = 256
GRID_W = 64
N_HEADS = 8
QK_NOPE = 64
QK_ROPE = 32
V_HEAD = 64
Q_LORA = 256
KV_LORA = 128
ROPE_AXIS = QK_ROPE // 2
ROPE_BASE = 10000.0
ATTN_SCALE = (QK_NOPE + QK_ROPE) ** -0.5
D_CONV = D_MODEL
N_GROUPS = 4
EXP_PER_GROUP = 8
N_EXPERTS = N_GROUPS * EXP_PER_GROUP
D_EXPERT = 256
EPS = 1e-6

T_P = BATCH * SEQ
T_S = DEC_BATCH * DEC_SEQ
T = T_P + T_S
N_COND = 8
LANE = 128
ROPE_LANE0 = QK_NOPE
SMALL_COLS = Q_LORA + KV_LORA + LANE
VMEM_LIMIT = 56 * 1024 * 1024

TM_IN = 1024
TM_POST = 512
TM_MOE = T // N_GROUPS
TM_FINAL = 512
N_SLAB = D_MODEL // LANE
MOE_SUB = 256
MOE_EPS = 2
N_VISITS = T // TM_MOE + N_GROUPS - 1
GID_LANE = 40
RANK_LANE = 41
Q_BLK = 256
Q_BLK_S = 512
PROMPT_SEQS = 2
CONV_CHUNK = 512


def _dot(a, b):
    return jnp.dot(a, b, preferred_element_type=F32)


def _rms(x):
    return lax.rsqrt(jnp.mean(x * x, axis=-1, keepdims=True) + EPS)


def _slab(t):
    return pl.ds(pl.multiple_of(t * N_SLAB, N_SLAB), N_SLAB)


def _mod_row(i, tm):
    n_prompt = T_P // tm
    return jnp.where(i >= n_prompt, 1 + ((i - n_prompt) * tm) // DEC_SEQ, 0)


def _ada_kernel(cond_ref, w_ref, b_ref, o_ref):
    c = cond_ref[...]
    a = (c * jax.nn.sigmoid(c)).astype(BF16)
    o_ref[...] = _dot(a, w_ref[...].astype(BF16)) + b_ref[...]


def _ada(cond, w_ada, b_ada):
    n = 6 * D_MODEL
    bn = 1536
    return pl.pallas_call(
        _ada_kernel,
        grid=(n // bn,),
        in_specs=[
            pl.BlockSpec((N_COND, D_MODEL), lambda j: (0, 0)),
            pl.BlockSpec((D_MODEL, bn), lambda j: (0, j)),
            pl.BlockSpec((1, bn), lambda j: (0, j)),
        ],
        out_specs=pl.BlockSpec((N_COND, bn), lambda j: (0, j)),
        out_shape=jax.ShapeDtypeStruct((N_COND, n), F32),
        compiler_params=pltpu.CompilerParams(
            dimension_semantics=("parallel",), vmem_limit_bytes=VMEM_LIMIT),
        name="ada_mod",
    )(cond, w_ada, b_ada)


O_CQ = 3 * D_CONV
O_KR = O_CQ + Q_LORA + KV_LORA
O_GATE = O_KR + QK_ROPE
PREP_STEPS = 4


IN_COLS = O_GATE + 2 * D_MODEL
TAIL_ROWS = IN_COLS // 2
SMALL_BLK = 512
NT = (((1,), (1,)), ((), ()))


def _dot_nt(a, bt):
    return lax.dot_general(a, bt, NT, preferred_element_type=F32)


def _prep_kernel(wct_ref, wst_ref, c3_ref, sm_ref):
    c3_ref[...] = wct_ref[...].astype(BF16)
    n_lat = Q_LORA + KV_LORA
    sm_ref[0:n_lat, :] = wst_ref[0:n_lat, :].astype(BF16)
    sm_ref[n_lat:, :] = jnp.zeros((LANE, sm_ref.shape[1]), BF16)
    sm_ref[n_lat + ROPE_LANE0:n_lat + ROPE_LANE0 + QK_ROPE, :] = (
        wst_ref[n_lat:n_lat + QK_ROPE, :].astype(BF16))


def _prep_weights(w_in_t):
    cb = D_MODEL // PREP_STEPS
    col = lambda rows, blk=0: pl.BlockSpec((rows, cb), lambda i: (blk, i))
    return pl.pallas_call(
        _prep_kernel,
        grid=(PREP_STEPS,),
        in_specs=[col(O_CQ), col(SMALL_BLK, O_CQ // SMALL_BLK)],
        out_specs=[col(O_CQ), col(SMALL_COLS)],
        out_shape=[jax.ShapeDtypeStruct((O_CQ, D_MODEL), BF16),
                   jax.ShapeDtypeStruct((SMALL_COLS, D_MODEL), BF16)],
        compiler_params=pltpu.CompilerParams(
            dimension_semantics=("parallel",), vmem_limit_bytes=VMEM_LIMIT),
        name="weight_casts",
    )(w_in_t, w_in_t)


def _stream_maps(tm):
    n_prompt = T_P // tm
    return (lambda i, *_: (jnp.minimum(i, n_prompt - 1), 0),
            lambda i, *_: (jnp.maximum(i - n_prompt, 0), 0))


def _inproj_kernel(xp_ref, xs_ref, mod_ref, n1_ref, wc_ref, ws_ref, cw_ref, qn_ref, kvn_ref, wuq_ref,
                   rope_ref, zc_ref, q_ref, ckv_ref, krs_ref, nckv_ref, nkr_ref):
    i = pl.program_id(0)
    is_sample = i >= T_P // TM_IN
    seq = jnp.where(is_sample, DEC_SEQ, SEQ)
    mod = mod_ref[pl.ds(_mod_row(i, TM_IN), 1), :]
    shift1 = mod[:, 0:D_MODEL]
    scale1 = mod[:, D_MODEL:2 * D_MODEL]
    x = jnp.where(is_sample, xs_ref[...], xp_ref[...])
    h = ((x * _rms(x)) * n1_ref[...]) * (1.0 + scale1) + shift1
    hb = h.astype(BF16)

    sm = _dot_nt(hb, ws_ref[...])
    cq = sm[:, 0:Q_LORA]
    ckv_raw = sm[:, Q_LORA:Q_LORA + KV_LORA]
    krs = sm[:, Q_LORA + KV_LORA:SMALL_COLS]
    cqn = (cq * _rms(cq)) * qn_ref[...]
    q = _dot(cqn.astype(BF16), wuq_ref[...])
    ckv = (ckv_raw * _rms(ckv_raw)) * kvn_ref[...]
    ckv_ref[...] = ckv

    @pl.when(jnp.logical_not(is_sample))
    def _():
        nckv_ref[...] = ckv
        nkr_ref[...] = krs[:, ROPE_LANE0:ROPE_LANE0 + QK_ROPE]

    cos = rope_ref[0]
    sin_lo = rope_ref[1]
    sin_hi = rope_ref[2]

    def rot(v):
        return v * cos + pltpu.roll(v, 8, 1) * sin_lo + pltpu.roll(v, LANE - 8, 1) * sin_hi

    krs_ref[...] = rot(krs)
    for hh in range(N_HEADS):
        q_ref[:, LANE * hh:LANE * (hh + 1)] = rot(q[:, LANE * hh:LANE * (hh + 1)]).astype(BF16)

    pos = lax.broadcasted_iota(jnp.int32, (TM_IN, 1), 0) & (seq - 1)
    first = pos == 0
    last = pos == seq - 1
    for j in range(D_CONV // CONV_CHUNK):
        c0 = j * CONV_CHUNK
        bg = _dot_nt(hb, wc_ref[c0:c0 + CONV_CHUNK, :])
        cg = _dot_nt(hb, wc_ref[D_CONV + c0:D_CONV + c0 + CONV_CHUNK, :])
        ui = _dot_nt(hb, wc_ref[2 * D_CONV + c0:2 * D_CONV + c0 + CONV_CHUNK, :])
        u = cg * ui
        u_prev = jnp.where(first, 0.0, pltpu.roll(u, 1, 0))
        u_next = jnp.where(last, 0.0, pltpu.roll(u, TM_IN - 1, 0))
        cw = cw_ref[:, c0:c0 + CONV_CHUNK]
        conv = u_prev * cw[0:1] + u * cw[1:2] + u_next * cw[2:3]
        zc_ref[:, c0:c0 + CONV_CHUNK] = (bg * conv).astype(BF16)


def _inproj(xp, xs, mod, norm1, w_conv3_t, w_small_t, conv_w, q_norm, kv_norm, w_uq_slot, rope_tabs):
    n_prompt = T_P // TM_IN
    const = lambda i: (0, 0)
    pmap, smap = _stream_maps(TM_IN)
    return pl.pallas_call(
        _inproj_kernel,
        grid=(T // TM_IN,),
        in_specs=[
            pl.BlockSpec((TM_IN, D_MODEL), pmap),
            pl.BlockSpec((TM_IN, D_MODEL), smap),
            pl.BlockSpec((N_COND, 6 * D_MODEL), const),
            pl.BlockSpec((1, D_MODEL), const),
            pl.BlockSpec((O_CQ, D_MODEL), const),
            pl.BlockSpec((SMALL_COLS, D_MODEL), const),
            pl.BlockSpec((3, D_CONV), const),
            pl.BlockSpec((1, Q_LORA), const),
            pl.BlockSpec((1, KV_LORA), const),
            pl.BlockSpec((Q_LORA, N_HEADS * LANE), const),
            pl.BlockSpec((None, 3, TM_IN, LANE),
                         lambda i: (jnp.where(i >= n_prompt, 1, 0), 0, 0, 0)),
        ],
        out_specs=[
            pl.BlockSpec((TM_IN, D_CONV), lambda i: (i, 0)),
            pl.BlockSpec((TM_IN, N_HEADS * LANE), lambda i: (i, 0)),
            pl.BlockSpec((TM_IN, KV_LORA), lambda i: (i, 0)),
            pl.BlockSpec((TM_IN, LANE), lambda i: (i, 0)),
            pl.BlockSpec((TM_IN, KV_LORA), pmap),
            pl.BlockSpec((TM_IN, QK_ROPE), pmap),
        ],
        out_shape=[
            jax.ShapeDtypeStruct((T, D_CONV), BF16),
            jax.ShapeDtypeStruct((T, N_HEADS * LANE), BF16),
            jax.ShapeDtypeStruct((T, KV_LORA), F32),
            jax.ShapeDtypeStruct((T, LANE), F32),
            jax.ShapeDtypeStruct((T_P, KV_LORA), F32),
            jax.ShapeDtypeStruct((T_P, QK_ROPE), F32),
        ],
        compiler_params=pltpu.CompilerParams(
            dimension_semantics=("arbitrary",), vmem_limit_bytes=VMEM_LIMIT),
        name="in_proj",
    )(xp, xs, mod, norm1, w_conv3_t, w_small_t, conv_w, q_norm, kv_norm, w_uq_slot, rope_tabs)


def _fill_kv(ckv, krs, wukv_ref, kf_scr, v_scr, off):
    m = ckv.shape[0]
    kv = _dot(ckv.astype(BF16), wukv_ref[...])
    for hh in range(N_HEADS):
        kf_scr[hh, off:off + m, :] = (kv[:, LANE * hh:LANE * (hh + 1)] + krs).astype(BF16)
    v_scr[off:off + m, :] = kv[:, N_HEADS * LANE:].astype(BF16)


def _attend(q_ref, r0, rows, kf_scr, v_scr, o_ref):
    for pair in range(N_HEADS // 2):
        acc = None
        for hh in (2 * pair, 2 * pair + 1):
            qh = q_ref[r0:r0 + rows, LANE * hh:LANE * (hh + 1)]
            s = _dot_nt(qh, kf_scr[hh]) * ATTN_SCALE
            e = jnp.exp(s - jnp.max(s, axis=-1, keepdims=True))
            p = (e / jnp.sum(e, axis=-1, keepdims=True)).astype(BF16)
            part = _dot(p, v_scr[:, LANE * hh:LANE * (hh + 1)])
            acc = part if acc is None else acc + part
        o_ref[r0:r0 + rows, LANE * pair:LANE * (pair + 1)] = acc.astype(BF16)


def _attn_prompt_kernel(q_ref, ckv_ref, krs_ref, wukv_ref, wg_ref, wco_ref, wo_ref, wmix_ref,
                        o_ref, g_ref, co_ref, ob_ref, mix_ref, kf_scr, v_scr):
    for s in range(PROMPT_SEQS):
        r0 = s * SEQ
        _fill_kv(ckv_ref[r0:r0 + SEQ, :], krs_ref[r0:r0 + SEQ, :], wukv_ref, kf_scr, v_scr, 0)
        _attend(q_ref, r0, SEQ, kf_scr, v_scr, o_ref)
    g_ref[...] = wg_ref[...].astype(BF16)
    co_ref[...] = wco_ref[...].astype(BF16)
    ob_ref[...] = wo_ref[...].astype(BF16)
    mix_ref[...] = wmix_ref[...].astype(BF16)


def _attn_sample_kernel(q_ref, ckv_ref, cckv_ref, krs_ref, ckrs_ref, wukv_ref, o_ref, kf_scr, v_scr):
    @pl.when(pl.program_id(1) == 0)
    def _():
        _fill_kv(ckv_ref[...], krs_ref[...], wukv_ref, kf_scr, v_scr, 0)
        _fill_kv(cckv_ref[...], ckrs_ref[...], wukv_ref, kf_scr, v_scr, DEC_SEQ)

    _attend(q_ref, 0, Q_BLK_S, kf_scr, v_scr, o_ref)


def _attention(q, ckv, krs, cache_ckv, cache_krs, w_ukv_slot, w_in_t, w_conv_out, w_o, w_mix_out):
    kv_cols = 2 * N_HEADS * LANE
    n_o = N_HEADS * V_HEAD
    steps = BATCH // PROMPT_SEQS
    rows = PROMPT_SEQS * SEQ
    share = lambda n: pl.BlockSpec((n // steps, D_MODEL), lambda b: (b, 0))
    gate_rows = 2 * D_MODEL // steps
    o_prompt, w_gates_t, w_co_b, w_o_b, w_mix_b = pl.pallas_call(
        _attn_prompt_kernel,
        grid=(steps,),
        in_specs=[
            pl.BlockSpec((rows, N_HEADS * LANE), lambda b: (b, 0)),
            pl.BlockSpec((rows, KV_LORA), lambda b: (b, 0)),
            pl.BlockSpec((rows, LANE), lambda b: (b, 0)),
            pl.BlockSpec((KV_LORA, kv_cols), lambda b: (0, 0)),
            pl.BlockSpec((pl.Element(gate_rows), pl.Element(D_MODEL)),
                         lambda b: (pl.multiple_of(O_GATE + b * gate_rows, 32), 0)),
            share(D_CONV), share(n_o), share(D_MODEL),
        ],
        out_specs=[pl.BlockSpec((rows, n_o), lambda b: (b, 0)),
                   share(2 * D_MODEL), share(D_CONV), share(n_o), share(D_MODEL)],
        out_shape=[jax.ShapeDtypeStruct((T_P, n_o), BF16),
                   jax.ShapeDtypeStruct((2 * D_MODEL, D_MODEL), BF16),
                   jax.ShapeDtypeStruct((D_CONV, D_MODEL), BF16),
                   jax.ShapeDtypeStruct((n_o, D_MODEL), BF16),
                   jax.ShapeDtypeStruct((D_MODEL, D_MODEL), BF16)],
        scratch_shapes=[pltpu.VMEM((N_HEADS, SEQ, LANE), BF16),
                        pltpu.VMEM((SEQ, N_HEADS * LANE), BF16)],
        compiler_params=pltpu.CompilerParams(dimension_semantics=("arbitrary",),
                                             vmem_limit_bytes=VMEM_LIMIT),
        name="attn_prompt",
    )(q, ckv, krs, w_ukv_slot, w_in_t, w_conv_out, w_o, w_mix_out)

    cp = pltpu.CompilerParams(dimension_semantics=("parallel", "arbitrary"),
                              vmem_limit_bytes=VMEM_LIMIT)

    m_all = DEC_SEQ + PAST_LEN
    nq = DEC_SEQ // Q_BLK_S
    q0 = T_P // Q_BLK_S
    s0 = T_P // DEC_SEQ
    o_sample = pl.pallas_call(
        _attn_sample_kernel,
        grid=(DEC_BATCH, nq),
        in_specs=[
            pl.BlockSpec((Q_BLK_S, N_HEADS * LANE), lambda b, j: (q0 + b * nq + j, 0)),
            pl.BlockSpec((DEC_SEQ, KV_LORA), lambda b, j: (s0 + b, 0)),
            pl.BlockSpec((None, PAST_LEN, KV_LORA), lambda b, j: (b, 0, 0)),
            pl.BlockSpec((DEC_SEQ, LANE), lambda b, j: (s0 + b, 0)),
            pl.BlockSpec((None, PAST_LEN, LANE), lambda b, j: (b, 0, 0)),
            pl.BlockSpec((KV_LORA, kv_cols), lambda b, j: (0, 0)),
        ],
        out_specs=pl.BlockSpec((Q_BLK_S, N_HEADS * V_HEAD), lambda b, j: (b * nq + j, 0)),
        out_shape=jax.ShapeDtypeStruct((T_S, N_HEADS * V_HEAD), BF16),
        scratch_shapes=[pltpu.VMEM((N_HEADS, m_all, LANE), BF16),
                        pltpu.VMEM((m_all, N_HEADS * LANE), BF16)],
        compiler_params=cp,
        name="attn_sample",
    )(q, ckv, cache_ckv, krs, cache_krs, w_ukv_slot)
    return o_prompt, o_sample, w_gates_t, w_co_b, w_o_b, w_mix_b


def _route(logits):
    lane = lax.broadcasted_iota(jnp.int32, logits.shape, 1)
    neg = -jnp.inf
    big = jnp.int32(1 << 20)
    gmask = (lane >= N_EXPERTS) & (lane < N_EXPERTS + N_GROUPS)
    gl = jnp.where(gmask, logits, neg)
    gmax = jnp.max(gl, axis=-1, keepdims=True)
    gsum = jnp.sum(jnp.where(gmask, jnp.exp(gl - gmax), 0.0), axis=-1, keepdims=True)
    p_g = 1.0 / gsum
    g_idx = jnp.min(jnp.where(gl == gmax, lane, big), axis=-1, keepdims=True) - N_EXPERTS

    emask = (lane < N_EXPERTS) & ((lane >> 3) == g_idx)
    el = jnp.where(emask, logits, neg)
    m1 = jnp.max(el, axis=-1, keepdims=True)
    i1 = jnp.min(jnp.where(el == m1, lane, big), axis=-1, keepdims=True)
    el2 = jnp.where(lane == i1, neg, el)
    m2 = jnp.max(el2, axis=-1, keepdims=True)
    i2 = jnp.min(jnp.where(el2 == m2, lane, big), axis=-1, keepdims=True)
    z = jnp.sum(jnp.where(emask, jnp.exp(el - m1), 0.0), axis=-1, keepdims=True)
    p1 = 1.0 / z
    p2 = jnp.exp(m2 - m1) / z
    tot = p1 + p2
    w1 = p_g * p1 / tot
    w2 = p_g * p2 / tot
    return jnp.where(lane == i1, w1, 0.0) + jnp.where(lane == i2, w2, 0.0), g_idx


def _post_kernel(xp_ref, xs_ref, mod_ref, n1_ref, wg_ref, zc_ref, op_ref, os_ref, wco_ref, wo_ref,
                 wmix_ref, n2_ref, wr_ref, x1_ref, h3_ref, meta_ref, cnt_ref, gr_ref):
    i = pl.program_id(0)
    is_sample = i >= T_P // TM_POST
    x = jnp.where(is_sample, xs_ref[...], xp_ref[...])
    o = jnp.where(is_sample, os_ref[...], op_ref[...])
    mod = mod_ref[pl.ds(_mod_row(i, TM_POST), 1), :]
    shift1 = mod[:, 0:D_MODEL]
    scale1 = mod[:, D_MODEL:2 * D_MODEL]
    gate1 = mod[:, 2 * D_MODEL:3 * D_MODEL]
    shift2 = mod[:, 3 * D_MODEL:4 * D_MODEL]
    scale2 = mod[:, 4 * D_MODEL:5 * D_MODEL]
    h = ((x * _rms(x)) * n1_ref[...]) * (1.0 + scale1) + shift1
    g = _dot_nt(h.astype(BF16), wg_ref[...])
    y_conv = _dot(zc_ref[...], wco_ref[...])
    y_mla = _dot(o, wo_ref[...])
    merged = (jax.nn.sigmoid(g[:, 0:D_MODEL]) * y_conv
              + jax.nn.sigmoid(g[:, D_MODEL:2 * D_MODEL]) * y_mla)
    y = _dot(merged.astype(BF16), wmix_ref[...])
    x1 = x + gate1 * y
    x1_ref[...] = x1
    h2 = ((x1 * _rms(x1)) * n2_ref[...]) * (1.0 + scale2) + shift2
    h2_hi = h2.astype(BF16)
    h2_lo = (h2 - h2_hi.astype(F32)).astype(BF16)
    hh = _dot(h2_hi, wr_ref[...])
    logits = hh[:, 0:LANE] + hh[:, LANE:2 * LANE] + _dot(h2_lo, wr_ref[:, 0:LANE])
    comb, g_idx = _route(logits)

    lane = lax.broadcasted_iota(jnp.int32, comb.shape, 1)
    onehot = lane == g_idx + N_EXPERTS
    r_i = lax.broadcasted_iota(jnp.int32, (TM_POST, TM_POST), 0)
    c_i = lax.broadcasted_iota(jnp.int32, (TM_POST, TM_POST), 1)
    lower = jnp.where(c_i < r_i, 1.0, 0.0).astype(BF16)
    before = _dot(lower, jnp.where(onehot, 1.0, 0.0).astype(BF16))
    rank = jnp.sum(jnp.where(onehot, before, 0.0), axis=-1, keepdims=True)
    counts = jnp.sum(jnp.where(onehot, 1.0, 0.0), axis=0, keepdims=True)
    cnt_ref[...] = jnp.broadcast_to(counts, cnt_ref.shape)

    meta_ref[...] = comb
    idx = jnp.where(lane == GID_LANE, g_idx.astype(F32), 0.0) + jnp.where(lane == RANK_LANE, rank, 0.0)
    idx_hi = idx.astype(BF16)
    idx_lo = (idx - idx_hi.astype(F32)).astype(BF16)
    s_row = lax.broadcasted_iota(jnp.int32, (8, LANE), 0)
    s_lane = lax.broadcasted_iota(jnp.int32, (8, LANE), 1)
    sel = jnp.where(s_lane == GID_LANE + s_row, 1.0, 0.0).astype(BF16)
    gr_ref[...] = _dot_nt(sel, idx_hi) + _dot_nt(sel, idx_lo)
    for c in range(N_SLAB):
        h3_ref[pl.ds(c, TM_POST, stride=N_SLAB), :] = h2[:, LANE * c:LANE * (c + 1)]


def _post(xp, xs, mod, norm1, w_gates_t, zc, o_p, o_s, w_conv_out, w_o, w_mix_out, norm2, w_route):
    const = lambda i: (0, 0)
    row = lambda i: (i, 0)
    pmap, smap = _stream_maps(TM_POST)
    return pl.pallas_call(
        _post_kernel,
        grid=(T // TM_POST,),
        in_specs=[
            pl.BlockSpec((TM_POST, D_MODEL), pmap),
            pl.BlockSpec((TM_POST, D_MODEL), smap),
            pl.BlockSpec((N_COND, 6 * D_MODEL), const),
            pl.BlockSpec((1, D_MODEL), const),
            pl.BlockSpec((2 * D_MODEL, D_MODEL), const),
            pl.BlockSpec((TM_POST, D_CONV), row),
            pl.BlockSpec((TM_POST, N_HEADS * V_HEAD), pmap),
            pl.BlockSpec((TM_POST, N_HEADS * V_HEAD), smap),
            pl.BlockSpec((D_CONV, D_MODEL), const),
            pl.BlockSpec((N_HEADS * V_HEAD, D_MODEL), const),
            pl.BlockSpec((D_MODEL, D_MODEL), const),
            pl.BlockSpec((1, D_MODEL), const),
            pl.BlockSpec((D_MODEL, 2 * LANE), const),
        ],
        out_specs=[
            pl.BlockSpec((TM_POST, D_MODEL), row),
            pl.BlockSpec((TM_POST * N_SLAB, LANE), row),
            pl.BlockSpec((TM_POST, LANE), row),
            pl.BlockSpec((None, 8, LANE), lambda i: (i, 0, 0)),
            pl.BlockSpec((8, TM_POST), lambda i: (0, i)),
        ],
        out_shape=[
            jax.ShapeDtypeStruct((T, D_MODEL), F32),
            jax.ShapeDtypeStruct((T * N_SLAB, LANE), F32),
            jax.ShapeDtypeStruct((T, LANE), F32),
            jax.ShapeDtypeStruct((T // TM_POST, 8, LANE), F32),
            jax.ShapeDtypeStruct((8, T), F32),
        ],
        compiler_params=pltpu.CompilerParams(
            dimension_semantics=("parallel",), vmem_limit_bytes=VMEM_LIMIT),
        name="post_mixer",
    )(xp, xs, mod, norm1, w_gates_t, zc, o_p, o_s, w_conv_out, w_o, w_mix_out, norm2, w_route)


def _dispatch_kernel(pos_ref, h3_ref, m_ref, hs_ref, ms_ref, src_ref, xg_ref):
    i = pl.program_id(0)

    @pl.when(i == 0)
    def _():
        def invert(t, carry):
            src_ref[pos_ref[t]] = t
            return carry

        lax.fori_loop(0, T, invert, 0, unroll=8)

    base = i * TM_MOE

    def body(r, carry):
        tok = src_ref[base + r]
        xg_ref[_slab(r), :] = h3_ref[_slab(tok), :]
        ms_ref[pl.ds(r, 1), :] = m_ref[pl.ds(tok, 1), :]
        return carry

    lax.fori_loop(0, TM_MOE, body, 0, unroll=8)
    for c in range(N_SLAB):
        hs_ref[:, LANE * c:LANE * (c + 1)] = xg_ref[pl.ds(c, TM_MOE, stride=N_SLAB), :].astype(BF16)


def _dispatch(pos, h3, meta):
    n_slab = N_SLAB
    return pl.pallas_call(
        _dispatch_kernel,
        grid_spec=pltpu.PrefetchScalarGridSpec(
            num_scalar_prefetch=1,
            grid=(T // TM_MOE,),
            in_specs=[pl.BlockSpec((T * n_slab, LANE), lambda i, pos: (0, 0),
                                   pipeline_mode=pl.Buffered(1)),
                      pl.BlockSpec((T, LANE), lambda i, pos: (0, 0), pipeline_mode=pl.Buffered(1))],
            out_specs=[pl.BlockSpec((TM_MOE, D_MODEL), lambda i, pos: (i, 0)),
                       pl.BlockSpec((TM_MOE, LANE), lambda i, pos: (i, 0))],
            scratch_shapes=[pltpu.SMEM((T,), jnp.int32),
                            pltpu.VMEM((TM_MOE * n_slab, LANE), F32)],
        ),
        out_shape=[jax.ShapeDtypeStruct((T, D_MODEL), BF16),
                   jax.ShapeDtypeStruct((T, LANE), F32)],
        compiler_params=pltpu.CompilerParams(
            dimension_semantics=("arbitrary",), vmem_limit_bytes=VMEM_LIMIT),
        name="moe_dispatch",
    )(pos, h3, meta)


def _moe_kernel(vt_ref, vg_ref, vlo_ref, vhi_ref, vfirst_ref, vlast_ref, vvalid_ref,
                hs_ref, ms_ref, wup_ref, wgate_ref, wdown_ref, y3_ref, acc_ref):
    v = pl.program_id(0)
    j = pl.program_id(1)
    valid = vvalid_ref[v] == 1
    lo = vlo_ref[v]
    hi = vhi_ref[v]
    e0 = vg_ref[v] * EXP_PER_GROUP + j * MOE_EPS
    full = (hi - lo) * 4 >= TM_MOE * 3

    @pl.when(valid & (j == 0) & (vfirst_ref[v] == 1))
    def _():
        acc_ref[...] = jnp.zeros_like(acc_ref)

    def expert_rows(r0, rows):
        w_in2 = jnp.concatenate(
            [w[k].astype(BF16) for k in range(MOE_EPS) for w in (wup_ref, wgate_ref)], axis=1)
        ag = _dot(hs_ref[r0:r0 + rows, :], w_in2)
        comb = ms_ref[r0:r0 + rows, :]
        lane = lax.broadcasted_iota(jnp.int32, comb.shape, 1)
        acts = []
        for k in range(MOE_EPS):
            a = ag[:, 2 * k * D_EXPERT:(2 * k + 1) * D_EXPERT]
            g = ag[:, (2 * k + 1) * D_EXPERT:(2 * k + 2) * D_EXPERT]
            cw = jnp.sum(jnp.where(lane == e0 + k, comb, 0.0), axis=-1, keepdims=True)
            acts.append(((g * jax.nn.sigmoid(g)) * a * cw).astype(BF16))
        w_out = jnp.concatenate([wdown_ref[k].astype(BF16) for k in range(MOE_EPS)], axis=0)
        acc_ref[r0:r0 + rows, :] += _dot(jnp.concatenate(acts, axis=1), w_out)

    @pl.when(valid & full)
    def _():
        expert_rows(0, TM_MOE)

    @pl.when(valid & jnp.logical_not(full))
    def _():
        for s in range(TM_MOE // MOE_SUB):
            r0 = s * MOE_SUB

            @pl.when((lo < r0 + MOE_SUB) & (hi > r0))
            def _():
                expert_rows(r0, MOE_SUB)

    @pl.when(valid & (j == EXP_PER_GROUP // MOE_EPS - 1) & (vlast_ref[v] == 1))
    def _():
        for c in range(N_SLAB):
            y3_ref[pl.ds(c, TM_MOE, stride=N_SLAB), :] = acc_ref[:, LANE * c:LANE * (c + 1)]


def _moe(sched, hs, ms, w_up, w_gate, w_down):
    steps = EXP_PER_GROUP // MOE_EPS
    wmap = lambda v, j, vt, vg, vlo, vhi, vfirst, vlast, vvalid: (
        vg[v] * steps + jnp.where(vvalid[v] == 1, j, steps - 1), 0, 0)
    tmap = lambda v, j, vt, *_: (vt[v], 0)
    n_slab = D_MODEL // LANE
    return pl.pallas_call(
        _moe_kernel,
        grid_spec=pltpu.PrefetchScalarGridSpec(
            num_scalar_prefetch=7,
            grid=(N_VISITS, steps),
            in_specs=[
                pl.BlockSpec((TM_MOE, D_MODEL), tmap),
                pl.BlockSpec((TM_MOE, LANE), tmap),
                pl.BlockSpec((MOE_EPS, D_MODEL, D_EXPERT), wmap),
                pl.BlockSpec((MOE_EPS, D_MODEL, D_EXPERT), wmap),
                pl.BlockSpec((MOE_EPS, D_EXPERT, D_MODEL), wmap),
            ],
            out_specs=pl.BlockSpec((TM_MOE * n_slab, LANE), tmap),
            scratch_shapes=[pltpu.VMEM((TM_MOE, D_MODEL), F32)],
        ),
        out_shape=jax.ShapeDtypeStruct((T * n_slab, LANE), F32),
        compiler_params=pltpu.CompilerParams(
            dimension_semantics=("arbitrary", "arbitrary"), vmem_limit_bytes=VMEM_LIMIT),
        name="moe_grouped",
    )(*sched, hs, ms, w_up, w_gate, w_down)


def _final_kernel(pos_ref, ys_ref, x1_ref, mod_ref, fn_ref, yp_ref, ysm_ref, g_ref):
    i = pl.program_id(0)
    base = i * TM_FINAL

    def body(r, carry):
        g_ref[_slab(r), :] = ys_ref[_slab(pos_ref[base + r]), :]
        return carry

    lax.fori_loop(0, TM_FINAL, body, 0, unroll=8)
    mod = mod_ref[pl.ds(_mod_row(i, TM_FINAL), 1), :]
    gate2 = mod[:, 5 * D_MODEL:6 * D_MODEL]
    moe = jnp.concatenate([g_ref[pl.ds(c, TM_FINAL, stride=N_SLAB), :] for c in range(N_SLAB)], axis=1)
    x2 = x1_ref[...] + gate2 * moe
    y = (x2 * _rms(x2)) * fn_ref[...]
    is_sample = i >= T_P // TM_FINAL

    @pl.when(jnp.logical_not(is_sample))
    def _():
        yp_ref[...] = y

    @pl.when(is_sample)
    def _():
        ysm_ref[...] = y


def _final(pos, ys, x1, mod, final_norm):
    n_slab = D_MODEL // LANE
    pmap, smap = _stream_maps(TM_FINAL)
    return pl.pallas_call(
        _final_kernel,
        grid_spec=pltpu.PrefetchScalarGridSpec(
            num_scalar_prefetch=1,
            grid=(T // TM_FINAL,),
            in_specs=[
                pl.BlockSpec((T * n_slab, LANE), lambda i, pos: (0, 0), pipeline_mode=pl.Buffered(1)),
                pl.BlockSpec((TM_FINAL, D_MODEL), lambda i, pos: (i, 0)),
                pl.BlockSpec((N_COND, 6 * D_MODEL), lambda i, pos: (0, 0)),
                pl.BlockSpec((1, D_MODEL), lambda i, pos: (0, 0)),
            ],
            out_specs=[pl.BlockSpec((TM_FINAL, D_MODEL), pmap),
                       pl.BlockSpec((TM_FINAL, D_MODEL), smap)],
            scratch_shapes=[pltpu.VMEM((TM_FINAL * n_slab, LANE), F32)],
        ),
        out_shape=[jax.ShapeDtypeStruct((T_P, D_MODEL), F32),
                   jax.ShapeDtypeStruct((T_S, D_MODEL), F32)],
        compiler_params=pltpu.CompilerParams(
            dimension_semantics=("arbitrary",), vmem_limit_bytes=VMEM_LIMIT),
        name="moe_unsort_final",
    )(pos, ys, x1, mod, final_norm)


def _schedule(gr, cnt):
    n_tiles_post = T // TM_POST
    counts = cnt[:, 0, N_EXPERTS:N_EXPERTS + N_GROUPS].astype(jnp.int32)
    gtot = jnp.sum(counts, axis=0)
    goff = jnp.cumsum(gtot) - gtot
    tile_base = goff[None, :] + jnp.cumsum(counts, axis=0) - counts
    gid = gr[0].astype(jnp.int32).reshape(n_tiles_post, TM_POST)
    rank = gr[1].astype(jnp.int32).reshape(n_tiles_post, TM_POST)
    pos = rank
    for grp in range(N_GROUPS):
        pos = pos + jnp.where(gid == grp, tile_base[:, grp:grp + 1], 0)
    pos = pos.reshape(T)

    n_tiles = T // TM_MOE
    t_lo = (jnp.arange(n_tiles, dtype=jnp.int32) * TM_MOE)[:, None]
    lo = jnp.clip(goff[None, :] - t_lo, 0, TM_MOE)
    hi = jnp.clip(goff[None, :] + gtot[None, :] - t_lo, 0, TM_MOE)
    ok = (hi > lo).reshape(-1)
    slot = jnp.cumsum(ok.astype(jnp.int32)) - 1
    n_ok = slot[-1] + 1
    sel = (slot[None, :] == jnp.arange(N_VISITS, dtype=jnp.int32)[:, None]) & ok[None, :]

    def pick(vals):
        return jnp.sum(jnp.where(sel, vals.reshape(-1)[None, :], 0), axis=1).astype(jnp.int32)

    pair_tile = jnp.broadcast_to(jnp.arange(n_tiles, dtype=jnp.int32)[:, None], (n_tiles, N_GROUPS))
    pair_group = jnp.broadcast_to(jnp.arange(N_GROUPS, dtype=jnp.int32)[None, :], (n_tiles, N_GROUPS))
    vt, vg, vlo, vhi = pick(pair_tile), pick(pair_group), pick(lo), pick(hi)
    valid = jnp.arange(N_VISITS, dtype=jnp.int32) < n_ok
    last = jnp.maximum(n_ok - 1, 0)
    vt = jnp.where(valid, vt, vt[last])
    vg = jnp.where(valid, vg, vg[last])
    change = (vt[1:] != vt[:-1]).astype(jnp.int32)
    one = jnp.ones((1,), jnp.int32)
    first = jnp.concatenate([one, change])
    idx = jnp.arange(N_VISITS, dtype=jnp.int32)
    last = jnp.where(idx == n_ok - 1, 1, jnp.concatenate([change, one]))
    return pos, (vt, vg, vlo, vhi, first, last, valid.astype(jnp.int32))


def _rope_tables():
    n = np.arange(DEC_SEQ)
    pos = np.stack([n // GRID_W, n % GRID_W], axis=1).astype(np.float32)
    half = ROPE_AXIS // 2
    inv = (1.0 / (ROPE_BASE ** (np.arange(0, ROPE_AXIS, 2, dtype=np.float32) / ROPE_AXIS))).astype(np.float32)
    ang = (pos[:, :, None] * inv[None, None, :]).astype(np.float32)
    cos = np.cos(ang).astype(np.float32)
    sin = np.sin(ang).astype(np.float32)
    tabs = np.zeros((2, 3, DEC_SEQ, LANE), np.float32)
    tabs[:, 0] = 1.0
    for a in range(2):
        lo = ROPE_LANE0 + a * ROPE_AXIS
        tabs[1, 0, :, lo:lo + half] = cos[:, a]
        tabs[1, 0, :, lo + half:lo + 2 * half] = cos[:, a]
        tabs[1, 1, :, lo + half:lo + 2 * half] = sin[:, a]
        tabs[1, 2, :, lo:lo + half] = -sin[:, a]
    return jnp.asarray(tabs)


def kernel(x_prompt, x_sample, cache_ckv, cache_krope, c, c_ctx, norm1, w_ada, b_ada, w_in, conv_w,
           w_conv_out, q_norm, w_uq, kv_norm, w_ukv, w_o, w_mix_out, norm2, w_grp, w_exp, w_up,
           w_gate, w_down, final_norm):
    l = 0
    xp = x_prompt.reshape(T_P, D_MODEL)
    xs = x_sample.reshape(T_S, D_MODEL)
    cond = jnp.concatenate(
        [c_ctx[None, :], c, jnp.zeros((N_COND - 1 - DEC_BATCH, D_MODEL), F32)], axis=0)
    mod = _ada(cond, w_ada[l], b_ada[l][None, :])

    w_in_t = w_in[l].T
    w_conv3_t, w_small_t = _prep_weights(w_in_t)
    w_uq_slot = jnp.pad(w_uq[l].reshape(Q_LORA, N_HEADS, QK_NOPE + QK_ROPE),
                        ((0, 0), (0, 0), (0, LANE - QK_NOPE - QK_ROPE))
                        ).reshape(Q_LORA, N_HEADS * LANE).astype(BF16)
    wkv = w_ukv[l].reshape(KV_LORA, N_HEADS, QK_NOPE + V_HEAD)
    wk_slot = jnp.pad(wkv[:, :, :QK_NOPE], ((0, 0), (0, 0), (0, LANE - QK_NOPE)))
    wv = wkv[:, :, QK_NOPE:].reshape(KV_LORA, N_HEADS // 2, 2, V_HEAD)
    zero = jnp.zeros_like(wv[:, :, 0])
    wv_slot = jnp.stack([jnp.concatenate([wv[:, :, 0], zero], axis=-1),
                         jnp.concatenate([zero, wv[:, :, 1]], axis=-1)], axis=2)
    w_ukv_slot = jnp.concatenate([wk_slot.reshape(KV_LORA, N_HEADS * LANE),
                                  wv_slot.reshape(KV_LORA, N_HEADS * LANE)], axis=1).astype(BF16)
    w_route = jnp.pad(jnp.concatenate([w_exp[l], w_grp[l]], axis=1),
                      ((0, 0), (0, LANE - N_EXPERTS - N_GROUPS)))
    w_route_hi = w_route.astype(BF16)
    w_route_lo = (w_route - w_route_hi.astype(F32)).astype(BF16)
    w_route2 = jnp.concatenate([w_route_hi, w_route_lo], axis=1)
    cache_krs = jnp.pad(cache_krope[:, l], ((0, 0), (0, 0), (ROPE_LANE0, LANE - ROPE_LANE0 - QK_ROPE)))

    zc, q, ckv, krs, nckv, nkr = _inproj(xp, xs, mod, norm1[l][None, :], w_conv3_t, w_small_t, conv_w[l],
                              q_norm[l][None, :], kv_norm[l][None, :], w_uq_slot, _rope_tables())
    o_p, o_s, w_gates_t, w_co_b, w_o_b, w_mix_b = _attention(
        q, ckv, krs, cache_ckv[:, l], cache_krs, w_ukv_slot, w_in_t, w_conv_out[l], w_o[l], w_mix_out[l])
    x1, h3, meta, cnt, gr = _post(xp, xs, mod, norm1[l][None, :], w_gates_t, zc, o_p, o_s, w_co_b, w_o_b,
                                  w_mix_b, norm2[l][None, :], w_route2)
    pos, sched = _schedule(gr, cnt)
    hs, ms = _dispatch(pos, h3, meta)
    ys = _moe(sched, hs, ms, w_up[l], w_gate[l], w_down[l])
    yp, ysm = _final(pos, ys, x1, mod, final_norm[None, :])

    y_prompt = yp.reshape(BATCH, SEQ, D_MODEL)
    y_sample = ysm.reshape(DEC_BATCH, DEC_SEQ, D_MODEL)
    new_ckv = nckv.reshape(BATCH, 1, SEQ, KV_LORA)
    new_krope = nkr.reshape(BATCH, 1, SEQ, QK_ROPE)
    return (y_prompt, y_sample, new_ckv, new_krope)
```

```python
import functools

import numpy as np
import jax
import jax.numpy as jnp
from jax import lax
from jax.experimental import pallas as pl
from jax.experimental.pallas import tpu as pltpu

F32 = jnp.float32
BF16 = jnp.bfloat16

D_MODEL = 1024
BATCH = 16
SEQ = 256
DEC_BATCH = 2
DEC_SEQ = 1024
PAST_LEN = 256
GRID_W = 64
N_HEADS = 8
QK_NOPE = 64
QK_ROPE = 32
V_HEAD = 64
Q_LORA = 256
KV_LORA = 128
ROPE_AXIS = QK_ROPE // 2
ROPE_BASE = 10000.0
ATTN_SCALE = (QK_NOPE + QK_ROPE) ** -0.5
D_CONV = D_MODEL
N_GROUPS = 4
EXP_PER_GROUP = 8
N_EXPERTS = N_GROUPS * EXP_PER_GROUP
D_EXPERT = 256
EPS = 1e-6

T_P = BATCH * SEQ
T_S = DEC_BATCH * DEC_SEQ
T = T_P + T_S
N_COND = 8
LANE = 128
ROPE_LANE0 = QK_NOPE
SMALL_COLS = Q_LORA + KV_LORA + LANE
VMEM_LIMIT = 56 * 1024 * 1024

TM_IN = 1024
TM_POST = 512
TM_MOE = T // N_GROUPS
TM_FINAL = 512
N_SLAB = D_MODEL // LANE
MOE_SUB = 256
MOE_EPS = 2
N_VISITS = T // TM_MOE + N_GROUPS - 1
GID_LANE = 40
RANK_LANE = 41
Q_BLK = 256
Q_BLK_S = 512
PROMPT_SEQS = 2
CONV_CHUNK = 256


def _dot(a, b):
    return jnp.dot(a, b, preferred_element_type=F32)


def _rms(x):
    return lax.rsqrt(jnp.mean(x * x, axis=-1, keepdims=True) + EPS)


def _slab(t):
    return pl.ds(pl.multiple_of(t * N_SLAB, N_SLAB), N_SLAB)


def _mod_row(i, tm):
    n_prompt = T_P // tm
    return jnp.where(i >= n_prompt, 1 + ((i - n_prompt) * tm) // DEC_SEQ, 0)


def _ada_kernel(cond_ref, w_ref, b_ref, o_ref):
    c = cond_ref[...]
    a = (c * jax.nn.sigmoid(c)).astype(BF16)
    o_ref[...] = _dot(a, w_ref[...].astype(BF16)) + b_ref[...]


def _ada(cond, w_ada, b_ada):
    n = 6 * D_MODEL
    bn = 1536
    return pl.pallas_call(
        _ada_kernel,
        grid=(n // bn,),
        in_specs=[
            pl.BlockSpec((N_COND, D_MODEL), lambda j: (0, 0)),
            pl.BlockSpec((D_MODEL, bn), lambda j: (0, j)),
            pl.BlockSpec((1, bn), lambda j: (0, j)),
        ],
        out_specs=pl.BlockSpec((N_COND, bn), lambda j: (0, j)),
        out_shape=jax.ShapeDtypeStruct((N_COND, n), F32),
        compiler_params=pltpu.CompilerParams(
            dimension_semantics=("parallel",), vmem_limit_bytes=VMEM_LIMIT),
        name="ada_mod",
    )(cond, w_ada, b_ada)


O_CQ = 3 * D_CONV
O_KR = O_CQ + Q_LORA + KV_LORA
O_GATE = O_KR + QK_ROPE
PREP_STEPS = 4


IN_COLS = O_GATE + 2 * D_MODEL
TAIL_ROWS = IN_COLS // 2
SMALL_BLK = 512
NT = (((1,), (1,)), ((), ()))


def _dot_nt(a, bt):
    return lax.dot_general(a, bt, NT, preferred_element_type=F32)


def _prep_kernel(wct_ref, wst_ref, c3_ref, sm_ref):
    c3_ref[...] = wct_ref[...].astype(BF16)
    n_lat = Q_LORA + KV_LORA
    sm_ref[0:n_lat, :] = wst_ref[0:n_lat, :].astype(BF16)
    sm_ref[n_lat:, :] = jnp.zeros((LANE, sm_ref.shape[1]), BF16)
    sm_ref[n_lat + ROPE_LANE0:n_lat + ROPE_LANE0 + QK_ROPE, :] = (
        wst_ref[n_lat:n_lat + QK_ROPE, :].astype(BF16))


def _prep_weights(w_in_t):
    cb = D_MODEL // PREP_STEPS
    col = lambda rows, blk=0: pl.BlockSpec((rows, cb), lambda i: (blk, i))
    return pl.pallas_call(
        _prep_kernel,
        grid=(PREP_STEPS,),
        in_specs=[col(O_CQ), col(SMALL_BLK, O_CQ // SMALL_BLK)],
        out_specs=[col(O_CQ), col(SMALL_COLS)],
        out_shape=[jax.ShapeDtypeStruct((O_CQ, D_MODEL), BF16),
                   jax.ShapeDtypeStruct((SMALL_COLS, D_MODEL), BF16)],
        compiler_params=pltpu.CompilerParams(
            dimension_semantics=("parallel",), vmem_limit_bytes=VMEM_LIMIT),
        name="weight_casts",
    )(w_in_t, w_in_t)


def _stream_maps(tm):
    n_prompt = T_P // tm
    return (lambda i, *_: (jnp.minimum(i, n_prompt - 1), 0),
            lambda i, *_: (jnp.maximum(i - n_prompt, 0), 0))


def _inproj_kernel(xp_ref, xs_ref, mod_ref, n1_ref, wc_ref, ws_ref, cw_ref, qn_ref, kvn_ref, wuq_ref,
                   rope_ref, zc_ref, q_ref, ckv_ref, krs_ref, nckv_ref, nkr_ref):
    i = pl.program_id(0)
    is_sample = i >= T_P // TM_IN
    seq = jnp.where(is_sample, DEC_SEQ, SEQ)
    mod = mod_ref[pl.ds(_mod_row(i, TM_IN), 1), :]
    shift1 = mod[:, 0:D_MODEL]
    scale1 = mod[:, D_MODEL:2 * D_MODEL]
    x = jnp.where(is_sample, xs_ref[...], xp_ref[...])
    h = ((x * _rms(x)) * n1_ref[...]) * (1.0 + scale1) + shift1
    hb = h.astype(BF16)

    sm = _dot_nt(hb, ws_ref[...])
    cq = sm[:, 0:Q_LORA]
    ckv_raw = sm[:, Q_LORA:Q_LORA + KV_LORA]
    krs = sm[:, Q_LORA + KV_LORA:SMALL_COLS]
    cqn = (cq * _rms(cq)) * qn_ref[...]
    q = _dot(cqn.astype(BF16), wuq_ref[...])
    ckv = (ckv_raw * _rms(ckv_raw)) * kvn_ref[...]
    ckv_ref[...] = ckv

    @pl.when(jnp.logical_not(is_sample))
    def _():
        nckv_ref[...] = ckv
        nkr_ref[...] = krs[:, ROPE_LANE0:ROPE_LANE0 + QK_ROPE]

    cos = rope_ref[0]
    sin_lo = rope_ref[1]
    sin_hi = rope_ref[2]

    def rot(v):
        return v * cos + pltpu.roll(v, 8, 1) * sin_lo + pltpu.roll(v, LANE - 8, 1) * sin_hi

    krs_ref[...] = rot(krs)
    for hh in range(N_HEADS):
        q_ref[:, LANE * hh:LANE * (hh + 1)] = rot(q[:, LANE * hh:LANE * (hh + 1)]).astype(BF16)

    pos = lax.broadcasted_iota(jnp.int32, (TM_IN, 1), 0) & (seq - 1)
    first = pos == 0
    last = pos == seq - 1
    for j in range(D_CONV // CONV_CHUNK):
        c0 = j * CONV_CHUNK
        bg = _dot_nt(hb, wc_ref[c0:c0 + CONV_CHUNK, :])
        cg = _dot_nt(hb, wc_ref[D_CONV + c0:D_CONV + c0 + CONV_CHUNK, :])
        ui = _dot_nt(hb, wc_ref[2 * D_CONV + c0:2 * D_CONV + c0 + CONV_CHUNK, :])
        u = cg * ui
        u_prev = jnp.where(first, 0.0, pltpu.roll(u, 1, 0))
        u_next = jnp.where(last, 0.0, pltpu.roll(u, TM_IN - 1, 0))
        cw = cw_ref[:, c0:c0 + CONV_CHUNK]
        conv = u_prev * cw[0:1] + u * cw[1:2] + u_next * cw[2:3]
        zc_ref[:, c0:c0 + CONV_CHUNK] = (bg * conv).astype(BF16)


def _inproj(xp, xs, mod, norm1, w_conv3_t, w_small_t, conv_w, q_norm, kv_norm, w_uq_slot, rope_tabs):
    n_prompt = T_P // TM_IN
    const = lambda i: (0, 0)
    pmap, smap = _stream_maps(TM_IN)
    return pl.pallas_call(
        _inproj_kernel,
        grid=(T // TM_IN,),
        in_specs=[
            pl.BlockSpec((TM_IN, D_MODEL), pmap),
            pl.BlockSpec((TM_IN, D_MODEL), smap),
            pl.BlockSpec((N_COND, 6 * D_MODEL), const),
            pl.BlockSpec((1, D_MODEL), const),
            pl.BlockSpec((O_CQ, D_MODEL), const),
            pl.BlockSpec((SMALL_COLS, D_MODEL), const),
            pl.BlockSpec((3, D_CONV), const),
            pl.BlockSpec((1, Q_LORA), const),
            pl.BlockSpec((1, KV_LORA), const),
            pl.BlockSpec((Q_LORA, N_HEADS * LANE), const),
            pl.BlockSpec((None, 3, TM_IN, LANE),
                         lambda i: (jnp.where(i >= n_prompt, 1, 0), 0, 0, 0)),
        ],
        out_specs=[
            pl.BlockSpec((TM_IN, D_CONV), lambda i: (i, 0)),
            pl.BlockSpec((TM_IN, N_HEADS * LANE), lambda i: (i, 0)),
            pl.BlockSpec((TM_IN, KV_LORA), lambda i: (i, 0)),
            pl.BlockSpec((TM_IN, LANE), lambda i: (i, 0)),
            pl.BlockSpec((TM_IN, KV_LORA), pmap),
            pl.BlockSpec((TM_IN, QK_ROPE), pmap),
        ],
        out_shape=[
            jax.ShapeDtypeStruct((T, D_CONV), BF16),
            jax.ShapeDtypeStruct((T, N_HEADS * LANE), BF16),
            jax.ShapeDtypeStruct((T, KV_LORA), F32),
            jax.ShapeDtypeStruct((T, LANE), F32),
            jax.ShapeDtypeStruct((T_P, KV_LORA), F32),
            jax.ShapeDtypeStruct((T_P, QK_ROPE), F32),
        ],
        compiler_params=pltpu.CompilerParams(
            dimension_semantics=("arbitrary",), vmem_limit_bytes=VMEM_LIMIT),
        name="in_proj",
    )(xp, xs, mod, norm1, w_conv3_t, w_small_t, conv_w, q_norm, kv_norm, w_uq_slot, rope_tabs)


def _fill_kv(ckv, krs, wukv_ref, kf_scr, v_scr, off):
    m = ckv.shape[0]
    kv = _dot(ckv.astype(BF16), wukv_ref[...])
    for hh in range(N_HEADS):
        kf_scr[hh, off:off + m, :] = (kv[:, LANE * hh:LANE * (hh + 1)] + krs).astype(BF16)
    v_scr[off:off + m, :] = kv[:, N_HEADS * LANE:].astype(BF16)


def _attend(q_ref, r0, rows, kf_scr, v_scr, o_ref):
    for pair in range(N_HEADS // 2):
        acc = None
        for hh in (2 * pair, 2 * pair + 1):
            qh = q_ref[r0:r0 + rows, LANE * hh:LANE * (hh + 1)]
            s = _dot_nt(qh, kf_scr[hh]) * ATTN_SCALE
            e = jnp.exp(s - jnp.max(s, axis=-1, keepdims=True))
            p = (e / jnp.sum(e, axis=-1, keepdims=True)).astype(BF16)
            part = _dot(p, v_scr[:, LANE * hh:LANE * (hh + 1)])
            acc = part if acc is None else acc + part
        o_ref[r0:r0 + rows, LANE * pair:LANE * (pair + 1)] = acc.astype(BF16)


def _attn_prompt_kernel(q_ref, ckv_ref, krs_ref, wukv_ref, wg_ref, wco_ref, wo_ref, wmix_ref,
                        o_ref, g_ref, co_ref, ob_ref, mix_ref, kf_scr, v_scr):
    for s in range(PROMPT_SEQS):
        r0 = s * SEQ
        _fill_kv(ckv_ref[r0:r0 + SEQ, :], krs_ref[r0:r0 + SEQ, :], wukv_ref, kf_scr, v_scr, 0)
        _attend(q_ref, r0, SEQ, kf_scr, v_scr, o_ref)
    g_ref[...] = wg_ref[...].astype(BF16)
    co_ref[...] = wco_ref[...].astype(BF16)
    ob_ref[...] = wo_ref[...].astype(BF16)
    mix_ref[...] = wmix_ref[...].astype(BF16)


def _attn_sample_kernel(q_ref, ckv_ref, cckv_ref, krs_ref, ckrs_ref, wukv_ref, o_ref, kf_scr, v_scr):
    @pl.when(pl.program_id(1) == 0)
    def _():
        _fill_kv(ckv_ref[...], krs_ref[...], wukv_ref, kf_scr, v_scr, 0)
        _fill_kv(cckv_ref[...], ckrs_ref[...], wukv_ref, kf_scr, v_scr, DEC_SEQ)

    _attend(q_ref, 0, Q_BLK_S, kf_scr, v_scr, o_ref)


def _attention(q, ckv, krs, cache_ckv, cache_krs, w_ukv_slot, w_in_t, w_conv_out, w_o, w_mix_out):
    kv_cols = 2 * N_HEADS * LANE
    n_o = N_HEADS * V_HEAD
    steps = BATCH // PROMPT_SEQS
    rows = PROMPT_SEQS * SEQ
    share = lambda n: pl.BlockSpec((n // steps, D_MODEL), lambda b: (b, 0))
    gate_rows = 2 * D_MODEL // steps
    o_prompt, w_gates_t, w_co_b, w_o_b, w_mix_b = pl.pallas_call(
        _attn_prompt_kernel,
        grid=(steps,),
        in_specs=[
            pl.BlockSpec((rows, N_HEADS * LANE), lambda b: (b, 0)),
            pl.BlockSpec((rows, KV_LORA), lambda b: (b, 0)),
            pl.BlockSpec((rows, LANE), lambda b: (b, 0)),
            pl.BlockSpec((KV_LORA, kv_cols), lambda b: (0, 0)),
            pl.BlockSpec((pl.Element(gate_rows), pl.Element(D_MODEL)),
                         lambda b: (pl.multiple_of(O_GATE + b * gate_rows, 32), 0)),
            share(D_CONV), share(n_o), share(D_MODEL),
        ],
        out_specs=[pl.BlockSpec((rows, n_o), lambda b: (b, 0)),
                   share(2 * D_MODEL), share(D_CONV), share(n_o), share(D_MODEL)],
        out_shape=[jax.ShapeDtypeStruct((T_P, n_o), BF16),
                   jax.ShapeDtypeStruct((2 * D_MODEL, D_MODEL), BF16),
                   jax.ShapeDtypeStruct((D_CONV, D_MODEL), BF16),
                   jax.ShapeDtypeStruct((n_o, D_MODEL), BF16),
                   jax.ShapeDtypeStruct((D_MODEL, D_MODEL), BF16)],
        scratch_shapes=[pltpu.VMEM((N_HEADS, SEQ, LANE), BF16),
                        pltpu.VMEM((SEQ, N_HEADS * LANE), BF16)],
        compiler_params=pltpu.CompilerParams(dimension_semantics=("arbitrary",),
                                             vmem_limit_bytes=VMEM_LIMIT),
        name="attn_prompt",
    )(q, ckv, krs, w_ukv_slot, w_in_t, w_conv_out, w_o, w_mix_out)

    cp = pltpu.CompilerParams(dimension_semantics=("parallel", "arbitrary"),
                              vmem_limit_bytes=VMEM_LIMIT)

    m_all = DEC_SEQ + PAST_LEN
    nq = DEC_SEQ // Q_BLK_S
    q0 = T_P // Q_BLK_S
    s0 = T_P // DEC_SEQ
    o_sample = pl.pallas_call(
        _attn_sample_kernel,
        grid=(DEC_BATCH, nq),
        in_specs=[
            pl.BlockSpec((Q_BLK_S, N_HEADS * LANE), lambda b, j: (q0 + b * nq + j, 0)),
            pl.BlockSpec((DEC_SEQ, KV_LORA), lambda b, j: (s0 + b, 0)),
            pl.BlockSpec((None, PAST_LEN, KV_LORA), lambda b, j: (b, 0, 0)),
            pl.BlockSpec((DEC_SEQ, LANE), lambda b, j: (s0 + b, 0)),
            pl.BlockSpec((None, PAST_LEN, LANE), lambda b, j: (b, 0, 0)),
            pl.BlockSpec((KV_LORA, kv_cols), lambda b, j: (0, 0)),
        ],
        out_specs=pl.BlockSpec((Q_BLK_S, N_HEADS * V_HEAD), lambda b, j: (b * nq + j, 0)),
        out_shape=jax.ShapeDtypeStruct((T_S, N_HEADS * V_HEAD), BF16),
        scratch_shapes=[pltpu.VMEM((N_HEADS, m_all, LANE), BF16),
                        pltpu.VMEM((m_all, N_HEADS * LANE), BF16)],
        compiler_params=cp,
        name="attn_sample",
    )(q, ckv, cache_ckv, krs, cache_krs, w_ukv_slot)
    return o_prompt, o_sample, w_gates_t, w_co_b, w_o_b, w_mix_b


def _route(logits):
    lane = lax.broadcasted_iota(jnp.int32, logits.shape, 1)
    neg = -jnp.inf
    big = jnp.int32(1 << 20)
    gmask = (lane >= N_EXPERTS) & (lane < N_EXPERTS + N_GROUPS)
    gl = jnp.where(gmask, logits, neg)
    gmax = jnp.max(gl, axis=-1, keepdims=True)
    gsum = jnp.sum(jnp.where(gmask, jnp.exp(gl - gmax), 0.0), axis=-1, keepdims=True)
    p_g = 1.0 / gsum
    g_idx = jnp.min(jnp.where(gl == gmax, lane, big), axis=-1, keepdims=True) - N_EXPERTS

    emask = (lane < N_EXPERTS) & ((lane >> 3) == g_idx)
    el = jnp.where(emask, logits, neg)
    m1 = jnp.max(el, axis=-1, keepdims=True)
    i1 = jnp.min(jnp.where(el == m1, lane, big), axis=-1, keepdims=True)
    el2 = jnp.where(lane == i1, neg, el)
    m2 = jnp.max(el2, axis=-1, keepdims=True)
    i2 = jnp.min(jnp.where(el2 == m2, lane, big), axis=-1, keepdims=True)
    z = jnp.sum(jnp.where(emask, jnp.exp(el - m1), 0.0), axis=-1, keepdims=True)
    p1 = 1.0 / z
    p2 = jnp.exp(m2 - m1) / z
    tot = p1 + p2
    w1 = p_g * p1 / tot
    w2 = p_g * p2 / tot
    return jnp.where(lane == i1, w1, 0.0) + jnp.where(lane == i2, w2, 0.0), g_idx


def _post_kernel(xp_ref, xs_ref, mod_ref, n1_ref, wg_ref, zc_ref, op_ref, os_ref, wco_ref, wo_ref,
                 wmix_ref, n2_ref, wr_ref, x1_ref, h3_ref, meta_ref, cnt_ref, gr_ref):
    i = pl.program_id(0)
    is_sample = i >= T_P // TM_POST
    x = jnp.where(is_sample, xs_ref[...], xp_ref[...])
    o = jnp.where(is_sample, os_ref[...], op_ref[...])
    mod = mod_ref[pl.ds(_mod_row(i, TM_POST), 1), :]
    shift1 = mod[:, 0:D_MODEL]
    scale1 = mod[:, D_MODEL:2 * D_MODEL]
    gate1 = mod[:, 2 * D_MODEL:3 * D_MODEL]
    shift2 = mod[:, 3 * D_MODEL:4 * D_MODEL]
    scale2 = mod[:, 4 * D_MODEL:5 * D_MODEL]
    y_conv = _dot(zc_ref[...], wco_ref[...])
    y_mla = _dot(o, wo_ref[...])
    h = ((x * _rms(x)) * n1_ref[...]) * (1.0 + scale1) + shift1
    g = _dot_nt(h.astype(BF16), wg_ref[...])
    merged = (jax.nn.sigmoid(g[:, 0:D_MODEL]) * y_conv
              + jax.nn.sigmoid(g[:, D_MODEL:2 * D_MODEL]) * y_mla)
    y = _dot(merged.astype(BF16), wmix_ref[...])
    x1 = x + gate1 * y
    x1_ref[...] = x1
    h2 = ((x1 * _rms(x1)) * n2_ref[...]) * (1.0 + scale2) + shift2
    h2_hi = h2.astype(BF16)
    h2_lo = (h2 - h2_hi.astype(F32)).astype(BF16)
    hh = _dot(h2_hi, wr_ref[...])
    logits = hh[:, 0:LANE] + hh[:, LANE:2 * LANE] + _dot(h2_lo, wr_ref[:, 0:LANE])
    comb, g_idx = _route(logits)

    lane = lax.broadcasted_iota(jnp.int32, comb.shape, 1)
    onehot = lane == g_idx + N_EXPERTS
    r_i = lax.broadcasted_iota(jnp.int32, (TM_POST, TM_POST), 0)
    c_i = lax.broadcasted_iota(jnp.int32, (TM_POST, TM_POST), 1)
    lower = jnp.where(c_i < r_i, 1.0, 0.0).astype(BF16)
    before = _dot(lower, jnp.where(onehot, 1.0, 0.0).astype(BF16))
    rank = jnp.sum(jnp.where(onehot, before, 0.0), axis=-1, keepdims=True)
    counts = jnp.sum(jnp.where(onehot, 1.0, 0.0), axis=0, keepdims=True)
    cnt_ref[...] = jnp.broadcast_to(counts, cnt_ref.shape)

    meta_ref[...] = comb
    idx = jnp.where(lane == GID_LANE, g_idx.astype(F32), 0.0) + jnp.where(lane == RANK_LANE, rank, 0.0)
    idx_hi = idx.astype(BF16)
    idx_lo = (idx - idx_hi.astype(F32)).astype(BF16)
    s_row = lax.broadcasted_iota(jnp.int32, (8, LANE), 0)
    s_lane = lax.broadcasted_iota(jnp.int32, (8, LANE), 1)
    sel = jnp.where(s_lane == GID_LANE + s_row, 1.0, 0.0).astype(BF16)
    gr_ref[...] = _dot_nt(sel, idx_hi) + _dot_nt(sel, idx_lo)
    for c in range(N_SLAB):
        h3_ref[pl.ds(c, TM_POST, stride=N_SLAB), :] = h2[:, LANE * c:LANE * (c + 1)]


def _post(xp, xs, mod, norm1, w_gates_t, zc, o_p, o_s, w_conv_out, w_o, w_mix_out, norm2, w_route):
    const = lambda i: (0, 0)
    row = lambda i: (i, 0)
    pmap, smap = _stream_maps(TM_POST)
    return pl.pallas_call(
        _post_kernel,
        grid=(T // TM_POST,),
        in_specs=[
            pl.BlockSpec((TM_POST, D_MODEL), pmap),
            pl.BlockSpec((TM_POST, D_MODEL), smap),
            pl.BlockSpec((N_COND, 6 * D_MODEL), const),
            pl.BlockSpec((1, D_MODEL), const),
            pl.BlockSpec((2 * D_MODEL, D_MODEL), const),
            pl.BlockSpec((TM_POST, D_CONV), row),
            pl.BlockSpec((TM_POST, N_HEADS * V_HEAD), pmap),
            pl.BlockSpec((TM_POST, N_HEADS * V_HEAD), smap),
            pl.BlockSpec((D_CONV, D_MODEL), const),
            pl.BlockSpec((N_HEADS * V_HEAD, D_MODEL), const),
            pl.BlockSpec((D_MODEL, D_MODEL), const),
            pl.BlockSpec((1, D_MODEL), const),
            pl.BlockSpec((D_MODEL, 2 * LANE), const),
        ],
        out_specs=[
            pl.BlockSpec((TM_POST, D_MODEL), row),
            pl.BlockSpec((TM_POST * N_SLAB, LANE), row),
            pl.BlockSpec((TM_POST, LANE), row),
            pl.BlockSpec((None, 8, LANE), lambda i: (i, 0, 0)),
            pl.BlockSpec((8, TM_POST), lambda i: (0, i)),
        ],
        out_shape=[
            jax.ShapeDtypeStruct((T, D_MODEL), F32),
            jax.ShapeDtypeStruct((T * N_SLAB, LANE), F32),
            jax.ShapeDtypeStruct((T, LANE), F32),
            jax.ShapeDtypeStruct((T // TM_POST, 8, LANE), F32),
            jax.ShapeDtypeStruct((8, T), F32),
        ],
        compiler_params=pltpu.CompilerParams(
            dimension_semantics=("parallel",), vmem_limit_bytes=VMEM_LIMIT),
        name="post_mixer",
    )(xp, xs, mod, norm1, w_gates_t, zc, o_p, o_s, w_conv_out, w_o, w_mix_out, norm2, w_route)


def _dispatch_kernel(pos_ref, h3_ref, m_ref, hs_ref, ms_ref, src_ref, xg_ref):
    i = pl.program_id(0)

    @pl.when(i == 0)
    def _():
        def invert(t, carry):
            src_ref[pos_ref[t]] = t
            return carry

        lax.fori_loop(0, T, invert, 0, unroll=8)

    base = i * TM_MOE

    def body(r, carry):
        tok = src_ref[base + r]
        xg_ref[_slab(r), :] = h3_ref[_slab(tok), :]
        ms_ref[pl.ds(r, 1), :] = m_ref[pl.ds(tok, 1), :]
        return carry

    lax.fori_loop(0, TM_MOE, body, 0, unroll=8)
    for c in range(N_SLAB):
        hs_ref[:, LANE * c:LANE * (c + 1)] = xg_ref[pl.ds(c, TM_MOE, stride=N_SLAB), :].astype(BF16)


def _dispatch(pos, h3, meta):
    n_slab = N_SLAB
    return pl.pallas_call(
        _dispatch_kernel,
        grid_spec=pltpu.PrefetchScalarGridSpec(
            num_scalar_prefetch=1,
            grid=(T // TM_MOE,),
            in_specs=[pl.BlockSpec((T * n_slab, LANE), lambda i, pos: (0, 0),
                                   pipeline_mode=pl.Buffered(1)),
                      pl.BlockSpec((T, LANE), lambda i, pos: (0, 0), pipeline_mode=pl.Buffered(1))],
            out_specs=[pl.BlockSpec((TM_MOE, D_MODEL), lambda i, pos: (i, 0)),
                       pl.BlockSpec((TM_MOE, LANE), lambda i, pos: (i, 0))],
            scratch_shapes=[pltpu.SMEM((T,), jnp.int32),
                            pltpu.VMEM((TM_MOE * n_slab, LANE), F32)],
        ),
        out_shape=[jax.ShapeDtypeStruct((T, D_MODEL), BF16),
                   jax.ShapeDtypeStruct((T, LANE), F32)],
        compiler_params=pltpu.CompilerParams(
            dimension_semantics=("arbitrary",), vmem_limit_bytes=VMEM_LIMIT),
        name="moe_dispatch",
    )(pos, h3, meta)


def _moe_kernel(vt_ref, vg_ref, vlo_ref, vhi_ref, vfirst_ref, vlast_ref, vvalid_ref,
                hs_ref, ms_ref, wup_ref, wgate_ref, wdown_ref, y3_ref, acc_ref):
    v = pl.program_id(0)
    j = pl.program_id(1)
    valid = vvalid_ref[v] == 1
    lo = vlo_ref[v]
    hi = vhi_ref[v]
    e0 = vg_ref[v] * EXP_PER_GROUP + j * MOE_EPS
    full = (hi - lo) * 4 >= TM_MOE * 3

    @pl.when(valid & (j == 0) & (vfirst_ref[v] == 1))
    def _():
        acc_ref[...] = jnp.zeros_like(acc_ref)

    def expert_rows(r0, rows):
        w_in2 = jnp.concatenate(
            [w[k].astype(BF16) for k in range(MOE_EPS) for w in (wup_ref, wgate_ref)], axis=1)
        ag = _dot(hs_ref[r0:r0 + rows, :], w_in2)
        comb = ms_ref[r0:r0 + rows, :]
        lane = lax.broadcasted_iota(jnp.int32, comb.shape, 1)
        acts = []
        for k in range(MOE_EPS):
            a = ag[:, 2 * k * D_EXPERT:(2 * k + 1) * D_EXPERT]
            g = ag[:, (2 * k + 1) * D_EXPERT:(2 * k + 2) * D_EXPERT]
            cw = jnp.sum(jnp.where(lane == e0 + k, comb, 0.0), axis=-1, keepdims=True)
            acts.append(((g * jax.nn.sigmoid(g)) * a * cw).astype(BF16))
        w_out = jnp.concatenate([wdown_ref[k].astype(BF16) for k in range(MOE_EPS)], axis=0)
        acc_ref[r0:r0 + rows, :] += _dot(jnp.concatenate(acts, axis=1), w_out)

    @pl.when(valid & full)
    def _():
        expert_rows(0, TM_MOE)

    @pl.when(valid & jnp.logical_not(full))
    def _():
        for s in range(TM_MOE // MOE_SUB):
            r0 = s * MOE_SUB

            @pl.when((lo < r0 + MOE_SUB) & (hi > r0))
            def _():
                expert_rows(r0, MOE_SUB)

    @pl.when(valid & (j == EXP_PER_GROUP // MOE_EPS - 1) & (vlast_ref[v] == 1))
    def _():
        for c in range(N_SLAB):
            y3_ref[pl.ds(c, TM_MOE, stride=N_SLAB), :] = acc_ref[:, LANE * c:LANE * (c + 1)]


def _moe(sched, hs, ms, w_up, w_gate, w_down):
    steps = EXP_PER_GROUP // MOE_EPS
    wmap = lambda v, j, vt, vg, vlo, vhi, vfirst, vlast, vvalid: (
        vg[v] * steps + jnp.where(vvalid[v] == 1, j, steps - 1), 0, 0)
    tmap = lambda v, j, vt, *_: (vt[v], 0)
    n_slab = D_MODEL // LANE
    return pl.pallas_call(
        _moe_kernel,
        grid_spec=pltpu.PrefetchScalarGridSpec(
            num_scalar_prefetch=7,
            grid=(N_VISITS, steps),
            in_specs=[
                pl.BlockSpec((TM_MOE, D_MODEL), tmap),
                pl.BlockSpec((TM_MOE, LANE), tmap),
                pl.BlockSpec((MOE_EPS, D_MODEL, D_EXPERT), wmap),
                pl.BlockSpec((MOE_EPS, D_MODEL, D_EXPERT), wmap),
                pl.BlockSpec((MOE_EPS, D_EXPERT, D_MODEL), wmap),
            ],
            out_specs=pl.BlockSpec((TM_MOE * n_slab, LANE), tmap),
            scratch_shapes=[pltpu.VMEM((TM_MOE, D_MODEL), F32)],
        ),
        out_shape=jax.ShapeDtypeStruct((T * n_slab, LANE), F32),
        compiler_params=pltpu.CompilerParams(
            dimension_semantics=("arbitrary", "arbitrary"), vmem_limit_bytes=VMEM_LIMIT),
        name="moe_grouped",
    )(*sched, hs, ms, w_up, w_gate, w_down)


def _final_kernel(pos_ref, ys_ref, x1_ref, mod_ref, fn_ref, yp_ref, ysm_ref, g_ref):
    i = pl.program_id(0)
    base = i * TM_FINAL

    def body(r, carry):
        g_ref[_slab(r), :] = ys_ref[_slab(pos_ref[base + r]), :]
        return carry

    lax.fori_loop(0, TM_FINAL, body, 0, unroll=8)
    mod = mod_ref[pl.ds(_mod_row(i, TM_FINAL), 1), :]
    gate2 = mod[:, 5 * D_MODEL:6 * D_MODEL]
    moe = jnp.concatenate([g_ref[pl.ds(c, TM_FINAL, stride=N_SLAB), :] for c in range(N_SLAB)], axis=1)
    x2 = x1_ref[...] + gate2 * moe
    y = (x2 * _rms(x2)) * fn_ref[...]
    is_sample = i >= T_P // TM_FINAL

    @pl.when(jnp.logical_not(is_sample))
    def _():
        yp_ref[...] = y

    @pl.when(is_sample)
    def _():
        ysm_ref[...] = y


def _final(pos, ys, x1, mod, final_norm):
    n_slab = D_MODEL // LANE
    pmap, smap = _stream_maps(TM_FINAL)
    return pl.pallas_call(
        _final_kernel,
        grid_spec=pltpu.PrefetchScalarGridSpec(
            num_scalar_prefetch=1,
            grid=(T // TM_FINAL,),
            in_specs=[
                pl.BlockSpec((T * n_slab, LANE), lambda i, pos: (0, 0), pipeline_mode=pl.Buffered(1)),
                pl.BlockSpec((TM_FINAL, D_MODEL), lambda i, pos: (i, 0)),
                pl.BlockSpec((N_COND, 6 * D_MODEL), lambda i, pos: (0, 0)),
                pl.BlockSpec((1, D_MODEL), lambda i, pos: (0, 0)),
            ],
            out_specs=[pl.BlockSpec((TM_FINAL, D_MODEL), pmap),
                       pl.BlockSpec((TM_FINAL, D_MODEL), smap)],
            scratch_shapes=[pltpu.VMEM((TM_FINAL * n_slab, LANE), F32)],
        ),
        out_shape=[jax.ShapeDtypeStruct((T_P, D_MODEL), F32),
                   jax.ShapeDtypeStruct((T_S, D_MODEL), F32)],
        compiler_params=pltpu.CompilerParams(
            dimension_semantics=("arbitrary",), vmem_limit_bytes=VMEM_LIMIT),
        name="moe_unsort_final",
    )(pos, ys, x1, mod, final_norm)


def _schedule(gr, cnt):
    n_tiles_post = T // TM_POST
    counts = cnt[:, 0, N_EXPERTS:N_EXPERTS + N_GROUPS].astype(jnp.int32)
    gtot = jnp.sum(counts, axis=0)
    goff = jnp.cumsum(gtot) - gtot
    tile_base = goff[None, :] + jnp.cumsum(counts, axis=0) - counts
    gid = gr[0].astype(jnp.int32).reshape(n_tiles_post, TM_POST)
    rank = gr[1].astype(jnp.int32).reshape(n_tiles_post, TM_POST)
    pos = rank
    for grp in range(N_GROUPS):
        pos = pos + jnp.where(gid == grp, tile_base[:, grp:grp + 1], 0)
    pos = pos.reshape(T)

    n_tiles = T // TM_MOE
    t_lo = (jnp.arange(n_tiles, dtype=jnp.int32) * TM_MOE)[:, None]
    lo = jnp.clip(goff[None, :] - t_lo, 0, TM_MOE)
    hi = jnp.clip(goff[None, :] + gtot[None, :] - t_lo, 0, TM_MOE)
    ok = (hi > lo).reshape(-1)
    slot = jnp.cumsum(ok.astype(jnp.int32)) - 1
    n_ok = slot[-1] + 1
    sel = (slot[None, :] == jnp.arange(N_VISITS, dtype=jnp.int32)[:, None]) & ok[None, :]

    def pick(vals):
        return jnp.sum(jnp.where(sel, vals.reshape(-1)[None, :], 0), axis=1).astype(jnp.int32)

    pair_tile = jnp.broadcast_to(jnp.arange(n_tiles, dtype=jnp.int32)[:, None], (n_tiles, N_GROUPS))
    pair_group = jnp.broadcast_to(jnp.arange(N_GROUPS, dtype=jnp.int32)[None, :], (n_tiles, N_GROUPS))
    vt, vg, vlo, vhi = pick(pair_tile), pick(pair_group), pick(lo), pick(hi)
    valid = jnp.arange(N_VISITS, dtype=jnp.int32) < n_ok
    last = jnp.maximum(n_ok - 1, 0)
    vt = jnp.where(valid, vt, vt[last])
    vg = jnp.where(valid, vg, vg[last])
    change = (vt[1:] != vt[:-1]).astype(jnp.int32)
    one = jnp.ones((1,), jnp.int32)
    first = jnp.concatenate([one, change])
    idx = jnp.arange(N_VISITS, dtype=jnp.int32)
    last = jnp.where(idx == n_ok - 1, 1, jnp.concatenate([change, one]))
    return pos, (vt, vg, vlo, vhi, first, last, valid.astype(jnp.int32))


def _rope_tables():
    n = np.arange(DEC_SEQ)
    pos = np.stack([n // GRID_W, n % GRID_W], axis=1).astype(np.float32)
    half = ROPE_AXIS // 2
    inv = (1.0 / (ROPE_BASE ** (np.arange(0, ROPE_AXIS, 2, dtype=np.float32) / ROPE_AXIS))).astype(np.float32)
    ang = (pos[:, :, None] * inv[None, None, :]).astype(np.float32)
    cos = np.cos(ang).astype(np.float32)
    sin = np.sin(ang).astype(np.float32)
    tabs = np.zeros((2, 3, DEC_SEQ, LANE), np.float32)
    tabs[:, 0] = 1.0
    for a in range(2):
        lo = ROPE_LANE0 + a * ROPE_AXIS
        tabs[1, 0, :, lo:lo + half] = cos[:, a]
        tabs[1, 0, :, lo + half:lo + 2 * half] = cos[:, a]
        tabs[1, 1, :, lo + half:lo + 2 * half] = sin[:, a]
        tabs[1, 2, :, lo:lo + half] = -sin[:, a]
    return jnp.asarray(tabs)


def kernel(x_prompt, x_sample, cache_ckv, cache_krope, c, c_ctx, norm1, w_ada, b_ada, w_in, conv_w,
           w_conv_out, q_norm, w_uq, kv_norm, w_ukv, w_o, w_mix_out, norm2, w_grp, w_exp, w_up,
           w_gate, w_down, final_norm):
    l = 0
    xp = x_prompt.reshape(T_P, D_MODEL)
    xs = x_sample.reshape(T_S, D_MODEL)
    cond = jnp.concatenate(
        [c_ctx[None, :], c, jnp.zeros((N_COND - 1 - DEC_BATCH, D_MODEL), F32)], axis=0)
    mod = _ada(cond, w_ada[l], b_ada[l][None, :])

    w_in_t = w_in[l].T
    w_conv3_t, w_small_t = _prep_weights(w_in_t)
    w_uq_slot = jnp.pad(w_uq[l].reshape(Q_LORA, N_HEADS, QK_NOPE + QK_ROPE),
                        ((0, 0), (0, 0), (0, LANE - QK_NOPE - QK_ROPE))
                        ).reshape(Q_LORA, N_HEADS * LANE).astype(BF16)
    wkv = w_ukv[l].reshape(KV_LORA, N_HEADS, QK_NOPE + V_HEAD)
    wk_slot = jnp.pad(wkv[:, :, :QK_NOPE], ((0, 0), (0, 0), (0, LANE - QK_NOPE)))
    wv = wkv[:, :, QK_NOPE:].reshape(KV_LORA, N_HEADS // 2, 2, V_HEAD)
    zero = jnp.zeros_like(wv[:, :, 0])
    wv_slot = jnp.stack([jnp.concatenate([wv[:, :, 0], zero], axis=-1),
                         jnp.concatenate([zero, wv[:, :, 1]], axis=-1)], axis=2)
    w_ukv_slot = jnp.concatenate([wk_slot.reshape(KV_LORA, N_HEADS * LANE),
                                  wv_slot.reshape(KV_LORA, N_HEADS * LANE)], axis=1).astype(BF16)
    w_route = jnp.pad(jnp.concatenate([w_exp[l], w_grp[l]], axis=1),
                      ((0, 0), (0, LANE - N_EXPERTS - N_GROUPS)))
    w_route_hi = w_route.astype(BF16)
    w_route_lo = (w_route - w_route_hi.astype(F32)).astype(BF16)
    w_route2 = jnp.concatenate([w_route_hi, w_route_lo], axis=1)
    cache_krs = jnp.pad(cache_krope[:, l], ((0, 0), (0, 0), (ROPE_LANE0, LANE - ROPE_LANE0 - QK_ROPE)))

    zc, q, ckv, krs, nckv, nkr = _inproj(xp, xs, mod, norm1[l][None, :], w_conv3_t, w_small_t, conv_w[l],
                              q_norm[l][None, :], kv_norm[l][None, :], w_uq_slot, _rope_tables())
    o_p, o_s, w_gates_t, w_co_b, w_o_b, w_mix_b = _attention(
        q, ckv, krs, cache_ckv[:, l], cache_krs, w_ukv_slot, w_in_t, w_conv_out[l], w_o[l], w_mix_out[l])
    x1, h3, meta, cnt, gr = _post(xp, xs, mod, norm1[l][None, :], w_gates_t, zc, o_p, o_s, w_co_b, w_o_b,
                                  w_mix_b, norm2[l][None, :], w_route2)
    pos, sched = _schedule(gr, cnt)
    hs, ms = _dispatch(pos, h3, meta)
    ys = _moe(sched, hs, ms, w_up[l], w_gate[l], w_down[l])
    yp, ysm = _final(pos, ys, x1, mod, final_norm[None, :])

    y_prompt = yp.reshape(BATCH, SEQ, D_MODEL)
    y_sample = ysm.reshape(DEC_BATCH, DEC_SEQ, D_MODEL)
    new_ckv = nckv.reshape(BATCH, 1, SEQ, KV_LORA)
    new_krope = nkr.reshape(BATCH, 1, SEQ, QK_ROPE)
    return (y_prompt, y_sample, new_ckv, new_krope)
```

```python
import functools

import numpy as np
import jax
import jax.numpy as jnp
from jax import lax
from jax.experimental import pallas as pl
from jax.experimental.pallas import tpu as pltpu

F32 = jnp.float32
BF16 = jnp.bfloat16

D_MODEL = 1024
BATCH = 16
SEQ = 256
DEC_BATCH = 2
DEC_SEQ = 1024
PAST_LEN = 256
GRID_W = 64
N_HEADS = 8
QK_NOPE = 64
QK_ROPE = 32
V_HEAD = 64
Q_LORA = 256
KV_LORA = 128
ROPE_AXIS = QK_ROPE // 2
ROPE_BASE = 10000.0
ATTN_SCALE = (QK_NOPE + QK_ROPE) ** -0.5
D_CONV = D_MODEL
N_GROUPS = 4
EXP_PER_GROUP = 8
N_EXPERTS = N_GROUPS * EXP_PER_GROUP
D_EXPERT = 256
EPS = 1e-6

T_P = BATCH * SEQ
T_S = DEC_BATCH * DEC_SEQ
T = T_P + T_S
N_COND = 8
LANE = 128
ROPE_LANE0 = QK_NOPE
SMALL_COLS = Q_LORA + KV_LORA + LANE
VMEM_LIMIT = 56 * 1024 * 1024

TM_IN = 1024
TM_POST = 512
TM_MOE = T // N_GROUPS
TM_FINAL = 512
N_SLAB = D_MODEL // LANE
MOE_SUB = 256
MOE_EPS = 2
N_VISITS = T // TM_MOE + N_GROUPS - 1
GID_LANE = 40
RANK_LANE = 41
Q_BLK = 256
Q_BLK_S = 512
PROMPT_SEQS = 2
CONV_CHUNK = 256


def _dot(a, b):
    return jnp.dot(a, b, preferred_element_type=F32)


def _rms(x):
    return lax.rsqrt(jnp.mean(x * x, axis=-1, keepdims=True) + EPS)


def _slab(t):
    return pl.ds(pl.multiple_of(t * N_SLAB, N_SLAB), N_SLAB)


def _mod_row(i, tm):
    n_prompt = T_P // tm
    return jnp.where(i >= n_prompt, 1 + ((i - n_prompt) * tm) // DEC_SEQ, 0)


def _ada_kernel(cond_ref, w_ref, b_ref, o_ref):
    c = cond_ref[...]
    a = (c * jax.nn.sigmoid(c)).astype(BF16)
    o_ref[...] = _dot(a, w_ref[...].astype(BF16)) + b_ref[...]


def _ada(cond, w_ada, b_ada):
    n = 6 * D_MODEL
    bn = 1536
    return pl.pallas_call(
        _ada_kernel,
        grid=(n // bn,),
        in_specs=[
            pl.BlockSpec((N_COND, D_MODEL), lambda j: (0, 0)),
            pl.BlockSpec((D_MODEL, bn), lambda j: (0, j)),
            pl.BlockSpec((1, bn), lambda j: (0, j)),
        ],
        out_specs=pl.BlockSpec((N_COND, bn), lambda j: (0, j)),
        out_shape=jax.ShapeDtypeStruct((N_COND, n), F32),
        compiler_params=pltpu.CompilerParams(
            dimension_semantics=("parallel",), vmem_limit_bytes=VMEM_LIMIT),
        name="ada_mod",
    )(cond, w_ada, b_ada)


O_CQ = 3 * D_CONV
O_KR = O_CQ + Q_LORA + KV_LORA
O_GATE = O_KR + QK_ROPE
PREP_STEPS = 4


IN_COLS = O_GATE + 2 * D_MODEL
TAIL_ROWS = IN_COLS // 2
SMALL_BLK = 512
NT = (((1,), (1,)), ((), ()))


def _dot_nt(a, bt):
    return lax.dot_general(a, bt, NT, preferred_element_type=F32)


def _prep_kernel(wct_ref, wst_ref, c3_ref, sm_ref):
    c3_ref[...] = wct_ref[...].astype(BF16)
    n_lat = Q_LORA + KV_LORA
    sm_ref[0:n_lat, :] = wst_ref[0:n_lat, :].astype(BF16)
    sm_ref[n_lat:, :] = jnp.zeros((LANE, sm_ref.shape[1]), BF16)
    sm_ref[n_lat + ROPE_LANE0:n_lat + ROPE_LANE0 + QK_ROPE, :] = (
        wst_ref[n_lat:n_lat + QK_ROPE, :].astype(BF16))


def _prep_weights(w_in_t):
    cb = D_MODEL // PREP_STEPS
    col = lambda rows, blk=0: pl.BlockSpec((rows, cb), lambda i: (blk, i))
    return pl.pallas_call(
        _prep_kernel,
        grid=(PREP_STEPS,),
        in_specs=[col(O_CQ), col(SMALL_BLK, O_CQ // SMALL_BLK)],
        out_specs=[col(O_CQ), col(SMALL_COLS)],
        out_shape=[jax.ShapeDtypeStruct((O_CQ, D_MODEL), BF16),
                   jax.ShapeDtypeStruct((SMALL_COLS, D_MODEL), BF16)],
        compiler_params=pltpu.CompilerParams(
            dimension_semantics=("parallel",), vmem_limit_bytes=VMEM_LIMIT),
        name="weight_casts",
    )(w_in_t, w_in_t)


def _stream_maps(tm):
    n_prompt = T_P // tm
    return (lambda i, *_: (jnp.minimum(i, n_prompt - 1), 0),
            lambda i, *_: (jnp.maximum(i - n_prompt, 0), 0))


def _inproj_kernel(xp_ref, xs_ref, mod_ref, n1_ref, wc_ref, ws_ref, cw_ref, qn_ref, kvn_ref, wuq_ref,
                   rope_ref, zc_ref, q_ref, ckv_ref, krs_ref, nckv_ref, nkr_ref):
    i = pl.program_id(0)
    is_sample = i >= T_P // TM_IN
    seq = jnp.where(is_sample, DEC_SEQ, SEQ)
    mod = mod_ref[pl.ds(_mod_row(i, TM_IN), 1), :]
    shift1 = mod[:, 0:D_MODEL]
    scale1 = mod[:, D_MODEL:2 * D_MODEL]
    x = jnp.where(is_sample, xs_ref[...], xp_ref[...])
    h = ((x * _rms(x)) * n1_ref[...]) * (1.0 + scale1) + shift1
    hb = h.astype(BF16)

    sm = _dot_nt(hb, ws_ref[...])
    cq = sm[:, 0:Q_LORA]
    ckv_raw = sm[:, Q_LORA:Q_LORA + KV_LORA]
    krs = sm[:, Q_LORA + KV_LORA:SMALL_COLS]
    cqn = (cq * _rms(cq)) * qn_ref[...]
    q = _dot(cqn.astype(BF16), wuq_ref[...])
    ckv = (ckv_raw * _rms(ckv_raw)) * kvn_ref[...]
    ckv_ref[...] = ckv

    @pl.when(jnp.logical_not(is_sample))
    def _():
        nckv_ref[...] = ckv
        nkr_ref[...] = krs[:, ROPE_LANE0:ROPE_LANE0 + QK_ROPE]

    cos = rope_ref[0]
    sin_lo = rope_ref[1]
    sin_hi = rope_ref[2]

    def rot(v):
        return v * cos + pltpu.roll(v, 8, 1) * sin_lo + pltpu.roll(v, LANE - 8, 1) * sin_hi

    krs_ref[...] = rot(krs)
    for hh in range(N_HEADS):
        q_ref[:, LANE * hh:LANE * (hh + 1)] = rot(q[:, LANE * hh:LANE * (hh + 1)]).astype(BF16)

    pos = lax.broadcasted_iota(jnp.int32, (TM_IN, 1), 0) & (seq - 1)
    first = pos == 0
    last = pos == seq - 1
    for j in range(D_CONV // CONV_CHUNK):
        c0 = j * CONV_CHUNK
        bg = _dot_nt(hb, wc_ref[c0:c0 + CONV_CHUNK, :])
        cg = _dot_nt(hb, wc_ref[D_CONV + c0:D_CONV + c0 + CONV_CHUNK, :])
        ui = _dot_nt(hb, wc_ref[2 * D_CONV + c0:2 * D_CONV + c0 + CONV_CHUNK, :])
        u = cg * ui
        u_prev = jnp.where(first, 0.0, pltpu.roll(u, 1, 0))
        u_next = jnp.where(last, 0.0, pltpu.roll(u, TM_IN - 1, 0))
        cw = cw_ref[:, c0:c0 + CONV_CHUNK]
        conv = u_prev * cw[0:1] + u * cw[1:2] + u_next * cw[2:3]
        zc_ref[:, c0:c0 + CONV_CHUNK] = (bg * conv).astype(BF16)


def _inproj(xp, xs, mod, norm1, w_conv3_t, w_small_t, conv_w, q_norm, kv_norm, w_uq_slot, rope_tabs):
    n_prompt = T_P // TM_IN
    const = lambda i: (0, 0)
    pmap, smap = _stream_maps(TM_IN)
    return pl.pallas_call(
        _inproj_kernel,
        grid=(T // TM_IN,),
        in_specs=[
            pl.BlockSpec((TM_IN, D_MODEL), pmap),
            pl.BlockSpec((TM_IN, D_MODEL), smap),
            pl.BlockSpec((N_COND, 6 * D_MODEL), const),
            pl.BlockSpec((1, D_MODEL), const),
            pl.BlockSpec((O_CQ, D_MODEL), const),
            pl.BlockSpec((SMALL_COLS, D_MODEL), const),
            pl.BlockSpec((3, D_CONV), const),
            pl.BlockSpec((1, Q_LORA), const),
            pl.BlockSpec((1, KV_LORA), const),
            pl.BlockSpec((Q_LORA, N_HEADS * LANE), const),
            pl.BlockSpec((None, 3, TM_IN, LANE),
                         lambda i: (jnp.where(i >= n_prompt, 1, 0), 0, 0, 0)),
        ],
        out_specs=[
            pl.BlockSpec((TM_IN, D_CONV), lambda i: (i, 0)),
            pl.BlockSpec((TM_IN, N_HEADS * LANE), lambda i: (i, 0)),
            pl.BlockSpec((TM_IN, KV_LORA), lambda i: (i, 0)),
            pl.BlockSpec((TM_IN, LANE), lambda i: (i, 0)),
            pl.BlockSpec((TM_IN, KV_LORA), pmap),
            pl.BlockSpec((TM_IN, QK_ROPE), pmap),
        ],
        out_shape=[
            jax.ShapeDtypeStruct((T, D_CONV), BF16),
            jax.ShapeDtypeStruct((T, N_HEADS * LANE), BF16),
            jax.ShapeDtypeStruct((T, KV_LORA), F32),
            jax.ShapeDtypeStruct((T, LANE), F32),
            jax.ShapeDtypeStruct((T_P, KV_LORA), F32),
            jax.ShapeDtypeStruct((T_P, QK_ROPE), F32),
        ],
        compiler_params=pltpu.CompilerParams(
            dimension_semantics=("arbitrary",), vmem_limit_bytes=VMEM_LIMIT),
        name="in_proj",
    )(xp, xs, mod, norm1, w_conv3_t, w_small_t, conv_w, q_norm, kv_norm, w_uq_slot, rope_tabs)


def _fill_kv(ckv, krs, wukv_ref, kf_scr, v_scr, off):
    m = ckv.shape[0]
    kv = _dot(ckv.astype(BF16), wukv_ref[...])
    for hh in range(N_HEADS):
        kf_scr[hh, off:off + m, :] = (kv[:, LANE * hh:LANE * (hh + 1)] + krs).astype(BF16)
    v_scr[off:off + m, :] = kv[:, N_HEADS * LANE:].astype(BF16)


def _attend(q_ref, r0, rows, kf_scr, v_scr, o_ref):
    for pair in range(N_HEADS // 2):
        acc = None
        for hh in (2 * pair, 2 * pair + 1):
            qh = q_ref[r0:r0 + rows, LANE * hh:LANE * (hh + 1)]
            s = _dot_nt(qh, kf_scr[hh]) * ATTN_SCALE
            e = jnp.exp(s - jnp.max(s, axis=-1, keepdims=True))
            p = (e / jnp.sum(e, axis=-1, keepdims=True)).astype(BF16)
            part = _dot(p, v_scr[:, LANE * hh:LANE * (hh + 1)])
            acc = part if acc is None else acc + part
        o_ref[r0:r0 + rows, LANE * pair:LANE * (pair + 1)] = acc.astype(BF16)


def _attn_prompt_kernel(q_ref, ckv_ref, krs_ref, wukv_ref, o_ref, kf_scr, v_scr):
    for s in range(PROMPT_SEQS):
        r0 = s * SEQ
        _fill_kv(ckv_ref[r0:r0 + SEQ, :], krs_ref[r0:r0 + SEQ, :], wukv_ref, kf_scr, v_scr, 0)
        _attend(q_ref, r0, SEQ, kf_scr, v_scr, o_ref)


def _attn_sample_kernel(q_ref, ckv_ref, cckv_ref, krs_ref, ckrs_ref, wukv_ref, wg_ref, wco_ref, wo_ref,
                        wmix_ref, o_ref, g_ref, co_ref, ob_ref, mix_ref, kf_scr, v_scr):
    @pl.when(pl.program_id(1) == 0)
    def _():
        _fill_kv(ckv_ref[...], krs_ref[...], wukv_ref, kf_scr, v_scr, 0)
        _fill_kv(cckv_ref[...], ckrs_ref[...], wukv_ref, kf_scr, v_scr, DEC_SEQ)

    _attend(q_ref, 0, Q_BLK_S, kf_scr, v_scr, o_ref)
    g_ref[...] = wg_ref[...].astype(BF16)
    co_ref[...] = wco_ref[...].astype(BF16)
    ob_ref[...] = wo_ref[...].astype(BF16)
    mix_ref[...] = wmix_ref[...].astype(BF16)


def _attention(q, ckv, krs, cache_ckv, cache_krs, w_ukv_slot, w_in_t, w_conv_out, w_o, w_mix_out):
    kv_cols = 2 * N_HEADS * LANE
    n_o = N_HEADS * V_HEAD
    steps = BATCH // PROMPT_SEQS
    rows = PROMPT_SEQS * SEQ
    o_prompt = pl.pallas_call(
        _attn_prompt_kernel,
        grid=(steps,),
        in_specs=[
            pl.BlockSpec((rows, N_HEADS * LANE), lambda b: (b, 0)),
            pl.BlockSpec((rows, KV_LORA), lambda b: (b, 0)),
            pl.BlockSpec((rows, LANE), lambda b: (b, 0)),
            pl.BlockSpec((KV_LORA, kv_cols), lambda b: (0, 0)),
        ],
        out_specs=pl.BlockSpec((rows, n_o), lambda b: (b, 0)),
        out_shape=jax.ShapeDtypeStruct((T_P, n_o), BF16),
        scratch_shapes=[pltpu.VMEM((N_HEADS, SEQ, LANE), BF16),
                        pltpu.VMEM((SEQ, N_HEADS * LANE), BF16)],
        compiler_params=pltpu.CompilerParams(dimension_semantics=("parallel",),
                                             vmem_limit_bytes=VMEM_LIMIT),
        name="attn_prompt",
    )(q, ckv, krs, w_ukv_slot)

    m_all = DEC_SEQ + PAST_LEN
    nq = DEC_SEQ // Q_BLK_S
    q0 = T_P // Q_BLK_S
    s0 = T_P // DEC_SEQ
    n_step = DEC_BATCH * nq
    share = lambda n: pl.BlockSpec((n // n_step, D_MODEL), lambda b, j: (b * nq + j, 0))
    gate_rows = 2 * D_MODEL // n_step
    o_sample, w_gates_t, w_co_b, w_o_b, w_mix_b = pl.pallas_call(
        _attn_sample_kernel,
        grid=(DEC_BATCH, nq),
        in_specs=[
            pl.BlockSpec((Q_BLK_S, N_HEADS * LANE), lambda b, j: (q0 + b * nq + j, 0)),
            pl.BlockSpec((DEC_SEQ, KV_LORA), lambda b, j: (s0 + b, 0)),
            pl.BlockSpec((None, PAST_LEN, KV_LORA), lambda b, j: (b, 0, 0)),
            pl.BlockSpec((DEC_SEQ, LANE), lambda b, j: (s0 + b, 0)),
            pl.BlockSpec((None, PAST_LEN, LANE), lambda b, j: (b, 0, 0)),
            pl.BlockSpec((KV_LORA, kv_cols), lambda b, j: (0, 0)),
            pl.BlockSpec((pl.Element(gate_rows), pl.Element(D_MODEL)),
                         lambda b, j: (pl.multiple_of(O_GATE + (b * nq + j) * gate_rows, 32), 0)),
            share(D_CONV), share(n_o), share(D_MODEL),
        ],
        out_specs=[pl.BlockSpec((Q_BLK_S, n_o), lambda b, j: (b * nq + j, 0)),
                   share(2 * D_MODEL), share(D_CONV), share(n_o), share(D_MODEL)],
        out_shape=[jax.ShapeDtypeStruct((T_S, n_o), BF16),
                   jax.ShapeDtypeStruct((2 * D_MODEL, D_MODEL), BF16),
                   jax.ShapeDtypeStruct((D_CONV, D_MODEL), BF16),
                   jax.ShapeDtypeStruct((n_o, D_MODEL), BF16),
                   jax.ShapeDtypeStruct((D_MODEL, D_MODEL), BF16)],
        scratch_shapes=[pltpu.VMEM((N_HEADS, m_all, LANE), BF16),
                        pltpu.VMEM((m_all, N_HEADS * LANE), BF16)],
        compiler_params=pltpu.CompilerParams(dimension_semantics=("arbitrary", "arbitrary"),
                                             vmem_limit_bytes=VMEM_LIMIT),
        name="attn_sample",
    )(q, ckv, cache_ckv, krs, cache_krs, w_ukv_slot, w_in_t, w_conv_out, w_o, w_mix_out)
    return o_prompt, o_sample, w_gates_t, w_co_b, w_o_b, w_mix_b


def _route(logits):
    lane = lax.broadcasted_iota(jnp.int32, logits.shape, 1)
    neg = -jnp.inf
    big = jnp.int32(1 << 20)
    gmask = (lane >= N_EXPERTS) & (lane < N_EXPERTS + N_GROUPS)
    gl = jnp.where(gmask, logits, neg)
    gmax = jnp.max(gl, axis=-1, keepdims=True)
    gsum = jnp.sum(jnp.where(gmask, jnp.exp(gl - gmax), 0.0), axis=-1, keepdims=True)
    p_g = 1.0 / gsum
    g_idx = jnp.min(jnp.where(gl == gmax, lane, big), axis=-1, keepdims=True) - N_EXPERTS

    emask = (lane < N_EXPERTS) & ((lane >> 3) == g_idx)
    el = jnp.where(emask, logits, neg)
    m1 = jnp.max(el, axis=-1, keepdims=True)
    i1 = jnp.min(jnp.where(el == m1, lane, big), axis=-1, keepdims=True)
    el2 = jnp.where(lane == i1, neg, el)
    m2 = jnp.max(el2, axis=-1, keepdims=True)
    i2 = jnp.min(jnp.where(el2 == m2, lane, big), axis=-1, keepdims=True)
    z = jnp.sum(jnp.where(emask, jnp.exp(el - m1), 0.0), axis=-1, keepdims=True)
    p1 = 1.0 / z
    p2 = jnp.exp(m2 - m1) / z
    tot = p1 + p2
    w1 = p_g * p1 / tot
    w2 = p_g * p2 / tot
    return jnp.where(lane == i1, w1, 0.0) + jnp.where(lane == i2, w2, 0.0), g_idx


def _post_kernel(xp_ref, xs_ref, mod_ref, n1_ref, wg_ref, zc_ref, op_ref, os_ref, wco_ref, wo_ref,
                 wmix_ref, n2_ref, wr_ref, x1_ref, h3_ref, meta_ref, cnt_ref, gr_ref):
    i = pl.program_id(0)
    is_sample = i >= T_P // TM_POST
    x = jnp.where(is_sample, xs_ref[...], xp_ref[...])
    o = jnp.where(is_sample, os_ref[...], op_ref[...])
    mod = mod_ref[pl.ds(_mod_row(i, TM_POST), 1), :]
    shift1 = mod[:, 0:D_MODEL]
    scale1 = mod[:, D_MODEL:2 * D_MODEL]
    gate1 = mod[:, 2 * D_MODEL:3 * D_MODEL]
    shift2 = mod[:, 3 * D_MODEL:4 * D_MODEL]
    scale2 = mod[:, 4 * D_MODEL:5 * D_MODEL]
    y_conv = _dot(zc_ref[...], wco_ref[...])
    y_mla = _dot(o, wo_ref[...])
    h = ((x * _rms(x)) * n1_ref[...]) * (1.0 + scale1) + shift1
    g = _dot_nt(h.astype(BF16), wg_ref[...])
    merged = (jax.nn.sigmoid(g[:, 0:D_MODEL]) * y_conv
              + jax.nn.sigmoid(g[:, D_MODEL:2 * D_MODEL]) * y_mla)
    y = _dot(merged.astype(BF16), wmix_ref[...])
    x1 = x + gate1 * y
    x1_ref[...] = x1
    h2 = ((x1 * _rms(x1)) * n2_ref[...]) * (1.0 + scale2) + shift2
    h2_hi = h2.astype(BF16)
    h2_lo = (h2 - h2_hi.astype(F32)).astype(BF16)
    hh = _dot(h2_hi, wr_ref[...])
    logits = hh[:, 0:LANE] + hh[:, LANE:2 * LANE] + _dot(h2_lo, wr_ref[:, 0:LANE])
    comb, g_idx = _route(logits)

    lane = lax.broadcasted_iota(jnp.int32, comb.shape, 1)
    onehot = lane == g_idx + N_EXPERTS
    r_i = lax.broadcasted_iota(jnp.int32, (TM_POST, TM_POST), 0)
    c_i = lax.broadcasted_iota(jnp.int32, (TM_POST, TM_POST), 1)
    lower = jnp.where(c_i < r_i, 1.0, 0.0).astype(BF16)
    before = _dot(lower, jnp.where(onehot, 1.0, 0.0).astype(BF16))
    rank = jnp.sum(jnp.where(onehot, before, 0.0), axis=-1, keepdims=True)
    counts = jnp.sum(jnp.where(onehot, 1.0, 0.0), axis=0, keepdims=True)
    cnt_ref[...] = jnp.broadcast_to(counts, cnt_ref.shape)

    meta_ref[...] = comb
    idx = jnp.where(lane == GID_LANE, g_idx.astype(F32), 0.0) + jnp.where(lane == RANK_LANE, rank, 0.0)
    idx_hi = idx.astype(BF16)
    idx_lo = (idx - idx_hi.astype(F32)).astype(BF16)
    s_row = lax.broadcasted_iota(jnp.int32, (8, LANE), 0)
    s_lane = lax.broadcasted_iota(jnp.int32, (8, LANE), 1)
    sel = jnp.where(s_lane == GID_LANE + s_row, 1.0, 0.0).astype(BF16)
    gr_ref[...] = _dot_nt(sel, idx_hi) + _dot_nt(sel, idx_lo)
    for c in range(N_SLAB):
        h3_ref[pl.ds(c, TM_POST, stride=N_SLAB), :] = h2[:, LANE * c:LANE * (c + 1)]


def _post(xp, xs, mod, norm1, w_gates_t, zc, o_p, o_s, w_conv_out, w_o, w_mix_out, norm2, w_route):
    const = lambda i: (0, 0)
    row = lambda i: (i, 0)
    pmap, smap = _stream_maps(TM_POST)
    return pl.pallas_call(
        _post_kernel,
        grid=(T // TM_POST,),
        in_specs=[
            pl.BlockSpec((TM_POST, D_MODEL), pmap),
            pl.BlockSpec((TM_POST, D_MODEL), smap),
            pl.BlockSpec((N_COND, 6 * D_MODEL), const),
            pl.BlockSpec((1, D_MODEL), const),
            pl.BlockSpec((2 * D_MODEL, D_MODEL), const),
            pl.BlockSpec((TM_POST, D_CONV), row),
            pl.BlockSpec((TM_POST, N_HEADS * V_HEAD), pmap),
            pl.BlockSpec((TM_POST, N_HEADS * V_HEAD), smap),
            pl.BlockSpec((D_CONV, D_MODEL), const),
            pl.BlockSpec((N_HEADS * V_HEAD, D_MODEL), const),
            pl.BlockSpec((D_MODEL, D_MODEL), const),
            pl.BlockSpec((1, D_MODEL), const),
            pl.BlockSpec((D_MODEL, 2 * LANE), const),
        ],
        out_specs=[
            pl.BlockSpec((TM_POST, D_MODEL), row),
            pl.BlockSpec((TM_POST * N_SLAB, LANE), row),
            pl.BlockSpec((TM_POST, LANE), row),
            pl.BlockSpec((None, 8, LANE), lambda i: (i, 0, 0)),
            pl.BlockSpec((8, TM_POST), lambda i: (0, i)),
        ],
        out_shape=[
            jax.ShapeDtypeStruct((T, D_MODEL), F32),
            jax.ShapeDtypeStruct((T * N_SLAB, LANE), F32),
            jax.ShapeDtypeStruct((T, LANE), F32),
            jax.ShapeDtypeStruct((T // TM_POST, 8, LANE), F32),
            jax.ShapeDtypeStruct((8, T), F32),
        ],
        compiler_params=pltpu.CompilerParams(
            dimension_semantics=("parallel",), vmem_limit_bytes=VMEM_LIMIT),
        name="post_mixer",
    )(xp, xs, mod, norm1, w_gates_t, zc, o_p, o_s, w_conv_out, w_o, w_mix_out, norm2, w_route)


def _dispatch_kernel(pos_ref, h3_ref, m_ref, hs_ref, ms_ref, src_ref, xg_ref):
    i = pl.program_id(0)

    @pl.when(i == 0)
    def _():
        def invert(t, carry):
            src_ref[pos_ref[t]] = t
            return carry

        lax.fori_loop(0, T, invert, 0, unroll=8)

    base = i * TM_MOE

    def body(r, carry):
        tok = src_ref[base + r]
        xg_ref[_slab(r), :] = h3_ref[_slab(tok), :]
        ms_ref[pl.ds(r, 1), :] = m_ref[pl.ds(tok, 1), :]
        return carry

    lax.fori_loop(0, TM_MOE, body, 0, unroll=8)
    for c in range(N_SLAB):
        hs_ref[:, LANE * c:LANE * (c + 1)] = xg_ref[pl.ds(c, TM_MOE, stride=N_SLAB), :].astype(BF16)


def _dispatch(pos, h3, meta):
    n_slab = N_SLAB
    return pl.pallas_call(
        _dispatch_kernel,
        grid_spec=pltpu.PrefetchScalarGridSpec(
            num_scalar_prefetch=1,
            grid=(T // TM_MOE,),
            in_specs=[pl.BlockSpec((T * n_slab, LANE), lambda i, pos: (0, 0),
                                   pipeline_mode=pl.Buffered(1)),
                      pl.BlockSpec((T, LANE), lambda i, pos: (0, 0), pipeline_mode=pl.Buffered(1))],
            out_specs=[pl.BlockSpec((TM_MOE, D_MODEL), lambda i, pos: (i, 0)),
                       pl.BlockSpec((TM_MOE, LANE), lambda i, pos: (i, 0))],
            scratch_shapes=[pltpu.SMEM((T,), jnp.int32),
                            pltpu.VMEM((TM_MOE * n_slab, LANE), F32)],
        ),
        out_shape=[jax.ShapeDtypeStruct((T, D_MODEL), BF16),
                   jax.ShapeDtypeStruct((T, LANE), F32)],
        compiler_params=pltpu.CompilerParams(
            dimension_semantics=("arbitrary",), vmem_limit_bytes=VMEM_LIMIT),
        name="moe_dispatch",
    )(pos, h3, meta)


def _moe_kernel(vt_ref, vg_ref, vlo_ref, vhi_ref, vfirst_ref, vlast_ref, vvalid_ref,
                hs_ref, ms_ref, wup_ref, wgate_ref, wdown_ref, y3_ref, acc_ref):
    v = pl.program_id(0)
    j = pl.program_id(1)
    valid = vvalid_ref[v] == 1
    lo = vlo_ref[v]
    hi = vhi_ref[v]
    e0 = vg_ref[v] * EXP_PER_GROUP + j * MOE_EPS
    full = (hi - lo) * 4 >= TM_MOE * 3

    @pl.when(valid & (j == 0) & (vfirst_ref[v] == 1))
    def _():
        acc_ref[...] = jnp.zeros_like(acc_ref)

    def expert_rows(r0, rows):
        w_in2 = jnp.concatenate(
            [w[k].astype(BF16) for k in range(MOE_EPS) for w in (wup_ref, wgate_ref)], axis=1)
        ag = _dot(hs_ref[r0:r0 + rows, :], w_in2)
        comb = ms_ref[r0:r0 + rows, :]
        lane = lax.broadcasted_iota(jnp.int32, comb.shape, 1)
        acts = []
        for k in range(MOE_EPS):
            a = ag[:, 2 * k * D_EXPERT:(2 * k + 1) * D_EXPERT]
            g = ag[:, (2 * k + 1) * D_EXPERT:(2 * k + 2) * D_EXPERT]
            cw = jnp.sum(jnp.where(lane == e0 + k, comb, 0.0), axis=-1, keepdims=True)
            acts.append(((g * jax.nn.sigmoid(g)) * a * cw).astype(BF16))
        w_out = jnp.concatenate([wdown_ref[k].astype(BF16) for k in range(MOE_EPS)], axis=0)
        acc_ref[r0:r0 + rows, :] += _dot(jnp.concatenate(acts, axis=1), w_out)

    @pl.when(valid & full)
    def _():
        expert_rows(0, TM_MOE)

    @pl.when(valid & jnp.logical_not(full))
    def _():
        for s in range(TM_MOE // MOE_SUB):
            r0 = s * MOE_SUB

            @pl.when((lo < r0 + MOE_SUB) & (hi > r0))
            def _():
                expert_rows(r0, MOE_SUB)

    @pl.when(valid & (j == EXP_PER_GROUP // MOE_EPS - 1) & (vlast_ref[v] == 1))
    def _():
        for c in range(N_SLAB):
            y3_ref[pl.ds(c, TM_MOE, stride=N_SLAB), :] = acc_ref[:, LANE * c:LANE * (c + 1)]


def _moe(sched, hs, ms, w_up, w_gate, w_down):
    steps = EXP_PER_GROUP // MOE_EPS
    wmap = lambda v, j, vt, vg, vlo, vhi, vfirst, vlast, vvalid: (
        vg[v] * steps + jnp.where(vvalid[v] == 1, j, steps - 1), 0, 0)
    tmap = lambda v, j, vt, *_: (vt[v], 0)
    n_slab = D_MODEL // LANE
    return pl.pallas_call(
        _moe_kernel,
        grid_spec=pltpu.PrefetchScalarGridSpec(
            num_scalar_prefetch=7,
            grid=(N_VISITS, steps),
            in_specs=[
                pl.BlockSpec((TM_MOE, D_MODEL), tmap),
                pl.BlockSpec((TM_MOE, LANE), tmap),
                pl.BlockSpec((MOE_EPS, D_MODEL, D_EXPERT), wmap),
                pl.BlockSpec((MOE_EPS, D_MODEL, D_EXPERT), wmap),
                pl.BlockSpec((MOE_EPS, D_EXPERT, D_MODEL), wmap),
            ],
            out_specs=pl.BlockSpec((TM_MOE * n_slab, LANE), tmap),
            scratch_shapes=[pltpu.VMEM((TM_MOE, D_MODEL), F32)],
        ),
        out_shape=jax.ShapeDtypeStruct((T * n_slab, LANE), F32),
        compiler_params=pltpu.CompilerParams(
            dimension_semantics=("arbitrary", "arbitrary"), vmem_limit_bytes=VMEM_LIMIT),
        name="moe_grouped",
    )(*sched, hs, ms, w_up, w_gate, w_down)


def _final_kernel(pos_ref, ys_ref, x1_ref, mod_ref, fn_ref, yp_ref, ysm_ref, g_ref):
    i = pl.program_id(0)
    base = i * TM_FINAL

    def body(r, carry):
        g_ref[_slab(r), :] = ys_ref[_slab(pos_ref[base + r]), :]
        return carry

    lax.fori_loop(0, TM_FINAL, body, 0, unroll=8)
    mod = mod_ref[pl.ds(_mod_row(i, TM_FINAL), 1), :]
    gate2 = mod[:, 5 * D_MODEL:6 * D_MODEL]
    moe = jnp.concatenate([g_ref[pl.ds(c, TM_FINAL, stride=N_SLAB), :] for c in range(N_SLAB)], axis=1)
    x2 = x1_ref[...] + gate2 * moe
    y = (x2 * _rms(x2)) * fn_ref[...]
    is_sample = i >= T_P // TM_FINAL

    @pl.when(jnp.logical_not(is_sample))
    def _():
        yp_ref[...] = y

    @pl.when(is_sample)
    def _():
        ysm_ref[...] = y


def _final(pos, ys, x1, mod, final_norm):
    n_slab = D_MODEL // LANE
    pmap, smap = _stream_maps(TM_FINAL)
    return pl.pallas_call(
        _final_kernel,
        grid_spec=pltpu.PrefetchScalarGridSpec(
            num_scalar_prefetch=1,
            grid=(T // TM_FINAL,),
            in_specs=[
                pl.BlockSpec((T * n_slab, LANE), lambda i, pos: (0, 0), pipeline_mode=pl.Buffered(1)),
                pl.BlockSpec((TM_FINAL, D_MODEL), lambda i, pos: (i, 0)),
                pl.BlockSpec((N_COND, 6 * D_MODEL), lambda i, pos: (0, 0)),
                pl.BlockSpec((1, D_MODEL), lambda i, pos: (0, 0)),
            ],
            out_specs=[pl.BlockSpec((TM_FINAL, D_MODEL), pmap),
                       pl.BlockSpec((TM_FINAL, D_MODEL), smap)],
            scratch_shapes=[pltpu.VMEM((TM_FINAL * n_slab, LANE), F32)],
        ),
        out_shape=[jax.ShapeDtypeStruct((T_P, D_MODEL), F32),
                   jax.ShapeDtypeStruct((T_S, D_MODEL), F32)],
        compiler_params=pltpu.CompilerParams(
            dimension_semantics=("arbitrary",), vmem_limit_bytes=VMEM_LIMIT),
        name="moe_unsort_final",
    )(pos, ys, x1, mod, final_norm)


def _schedule(gr, cnt):
    n_tiles_post = T // TM_POST
    counts = cnt[:, 0, N_EXPERTS:N_EXPERTS + N_GROUPS].astype(jnp.int32)
    gtot = jnp.sum(counts, axis=0)
    goff = jnp.cumsum(gtot) - gtot
    tile_base = goff[None, :] + jnp.cumsum(counts, axis=0) - counts
    gid = gr[0].astype(jnp.int32).reshape(n_tiles_post, TM_POST)
    rank = gr[1].astype(jnp.int32).reshape(n_tiles_post, TM_POST)
    pos = rank
    for grp in range(N_GROUPS):
        pos = pos + jnp.where(gid == grp, tile_base[:, grp:grp + 1], 0)
    pos = pos.reshape(T)

    n_tiles = T // TM_MOE
    t_lo = (jnp.arange(n_tiles, dtype=jnp.int32) * TM_MOE)[:, None]
    lo = jnp.clip(goff[None, :] - t_lo, 0, TM_MOE)
    hi = jnp.clip(goff[None, :] + gtot[None, :] - t_lo, 0, TM_MOE)
    ok = (hi > lo).reshape(-1)
    slot = jnp.cumsum(ok.astype(jnp.int32)) - 1
    n_ok = slot[-1] + 1
    sel = (slot[None, :] == jnp.arange(N_VISITS, dtype=jnp.int32)[:, None]) & ok[None, :]

    def pick(vals):
        return jnp.sum(jnp.where(sel, vals.reshape(-1)[None, :], 0), axis=1).astype(jnp.int32)

    pair_tile = jnp.broadcast_to(jnp.arange(n_tiles, dtype=jnp.int32)[:, None], (n_tiles, N_GROUPS))
    pair_group = jnp.broadcast_to(jnp.arange(N_GROUPS, dtype=jnp.int32)[None, :], (n_tiles, N_GROUPS))
    vt, vg, vlo, vhi = pick(pair_tile), pick(pair_group), pick(lo), pick(hi)
    valid = jnp.arange(N_VISITS, dtype=jnp.int32) < n_ok
    last = jnp.maximum(n_ok - 1, 0)
    vt = jnp.where(valid, vt, vt[last])
    vg = jnp.where(valid, vg, vg[last])
    change = (vt[1:] != vt[:-1]).astype(jnp.int32)
    one = jnp.ones((1,), jnp.int32)
    first = jnp.concatenate([one, change])
    idx = jnp.arange(N_VISITS, dtype=jnp.int32)
    last = jnp.where(idx == n_ok - 1, 1, jnp.concatenate([change, one]))
    return pos, (vt, vg, vlo, vhi, first, last, valid.astype(jnp.int32))


def _rope_tables():
    n = np.arange(DEC_SEQ)
    pos = np.stack([n // GRID_W, n % GRID_W], axis=1).astype(np.float32)
    half = ROPE_AXIS // 2
    inv = (1.0 / (ROPE_BASE ** (np.arange(0, ROPE_AXIS, 2, dtype=np.float32) / ROPE_AXIS))).astype(np.float32)
    ang = (pos[:, :, None] * inv[None, None, :]).astype(np.float32)
    cos = np.cos(ang).astype(np.float32)
    sin = np.sin(ang).astype(np.float32)
    tabs = np.zeros((2, 3, DEC_SEQ, LANE), np.float32)
    tabs[:, 0] = 1.0
    for a in range(2):
        lo = ROPE_LANE0 + a * ROPE_AXIS
        tabs[1, 0, :, lo:lo + half] = cos[:, a]
        tabs[1, 0, :, lo + half:lo + 2 * half] = cos[:, a]
        tabs[1, 1, :, lo + half:lo + 2 * half] = sin[:, a]
        tabs[1, 2, :, lo:lo + half] = -sin[:, a]
    return jnp.asarray(tabs)


def kernel(x_prompt, x_sample, cache_ckv, cache_krope, c, c_ctx, norm1, w_ada, b_ada, w_in, conv_w,
           w_conv_out, q_norm, w_uq, kv_norm, w_ukv, w_o, w_mix_out, norm2, w_grp, w_exp, w_up,
           w_gate, w_down, final_norm):
    l = 0
    xp = x_prompt.reshape(T_P, D_MODEL)
    xs = x_sample.reshape(T_S, D_MODEL)
    cond = jnp.concatenate(
        [c_ctx[None, :], c, jnp.zeros((N_COND - 1 - DEC_BATCH, D_MODEL), F32)], axis=0)
    mod = _ada(cond, w_ada[l], b_ada[l][None, :])

    w_in_t = w_in[l].T
    w_conv3_t, w_small_t = _prep_weights(w_in_t)
    w_uq_slot = jnp.pad(w_uq[l].reshape(Q_LORA, N_HEADS, QK_NOPE + QK_ROPE),
                        ((0, 0), (0, 0), (0, LANE - QK_NOPE - QK_ROPE))
                        ).reshape(Q_LORA, N_HEADS * LANE).astype(BF16)
    wkv = w_ukv[l].reshape(KV_LORA, N_HEADS, QK_NOPE + V_HEAD)
    wk_slot = jnp.pad(wkv[:, :, :QK_NOPE], ((0, 0), (0, 0), (0, LANE - QK_NOPE)))
    wv = wkv[:, :, QK_NOPE:].reshape(KV_LORA, N_HEADS // 2, 2, V_HEAD)
    zero = jnp.zeros_like(wv[:, :, 0])
    wv_slot = jnp.stack([jnp.concatenate([wv[:, :, 0], zero], axis=-1),
                         jnp.concatenate([zero, wv[:, :, 1]], axis=-1)], axis=2)
    w_ukv_slot = jnp.concatenate([wk_slot.reshape(KV_LORA, N_HEADS * LANE),
                                  wv_slot.reshape(KV_LORA, N_HEADS * LANE)], axis=1).astype(BF16)
    w_route = jnp.pad(jnp.concatenate([w_exp[l], w_grp[l]], axis=1),
                      ((0, 0), (0, LANE - N_EXPERTS - N_GROUPS)))
    w_route_hi = w_route.astype(BF16)
    w_route_lo = (w_route - w_route_hi.astype(F32)).astype(BF16)
    w_route2 = jnp.concatenate([w_route_hi, w_route_lo], axis=1)
    cache_krs = jnp.pad(cache_krope[:, l], ((0, 0), (0, 0), (ROPE_LANE0, LANE - ROPE_LANE0 - QK_ROPE)))

    zc, q, ckv, krs, nckv, nkr = _inproj(xp, xs, mod, norm1[l][None, :], w_conv3_t, w_small_t, conv_w[l],
                              q_norm[l][None, :], kv_norm[l][None, :], w_uq_slot, _rope_tables())
    o_p, o_s, w_gates_t, w_co_b, w_o_b, w_mix_b = _attention(
        q, ckv, krs, cache_ckv[:, l], cache_krs, w_ukv_slot, w_in_t, w_conv_out[l], w_o[l], w_mix_out[l])
    x1, h3, meta, cnt, gr = _post(xp, xs, mod, norm1[l][None, :], w_gates_t, zc, o_p, o_s, w_co_b, w_o_b,
                                  w_mix_b, norm2[l][None, :], w_route2)
    pos, sched = _schedule(gr, cnt)
    hs, ms = _dispatch(pos, h3, meta)
    ys = _moe(sched, hs, ms, w_up[l], w_gate[l], w_down[l])
    yp, ysm = _final(pos, ys, x1, mod, final_norm[None, :])

    y_prompt = yp.reshape(BATCH, SEQ, D_MODEL)
    y_sample = ysm.reshape(DEC_BATCH, DEC_SEQ, D_MODEL)
    new_ckv = nckv.reshape(BATCH, 1, SEQ, KV_LORA)
    new_krope = nkr.reshape(BATCH, 1, SEQ, QK_ROPE)
    return (y_prompt, y_sample, new_ckv, new_krope)
```

```python
import functools

import numpy as np
import jax
import jax.numpy as jnp
from jax import lax
from jax.experimental import pallas as pl
from jax.experimental.pallas import tpu as pltpu

F32 = jnp.float32
BF16 = jnp.bfloat16

D_MODEL = 1024
BATCH = 16
SEQ = 256
DEC_BATCH = 2
DEC_SEQ = 1024
PAST_LEN = 256
GRID_W = 64
N_HEADS = 8
QK_NOPE = 64
QK_ROPE = 32
V_HEAD = 64
Q_LORA = 256
KV_LORA = 128
ROPE_AXIS = QK_ROPE // 2
ROPE_BASE = 10000.0
ATTN_SCALE = (QK_NOPE + QK_ROPE) ** -0.5
D_CONV = D_MODEL
N_GROUPS = 4
EXP_PER_GROUP = 8
N_EXPERTS = N_GROUPS * EXP_PER_GROUP
D_EXPERT = 256
EPS = 1e-6

T_P = BATCH * SEQ
T_S = DEC_BATCH * DEC_SEQ
T = T_P + T_S
N_COND = 8
LANE = 128
ROPE_LANE0 = QK_NOPE
SMALL_COLS = Q_LORA + KV_LORA + LANE
VMEM_LIMIT = 56 * 1024 * 1024

TM_IN = 1024
TM_POST = 512
TM_MOE = T // N_GROUPS
TM_FINAL = 512
TM_DISP = 1024
N_SLAB = D_MODEL // LANE
MOE_SUB = 256
MOE_EPS = 2
N_VISITS = T // TM_MOE + N_GROUPS - 1
GID_LANE = 40
RANK_LANE = 41
Q_BLK = 256
Q_BLK_S = 512
PROMPT_SEQS = 2
CONV_CHUNK = 256


def _dot(a, b):
    return jnp.dot(a, b, preferred_element_type=F32)


def _rms(x):
    return lax.rsqrt(jnp.mean(x * x, axis=-1, keepdims=True) + EPS)


def _slab(t):
    return pl.ds(pl.multiple_of(t * N_SLAB, N_SLAB), N_SLAB)


def _mod_row(i, tm):
    n_prompt = T_P // tm
    return jnp.where(i >= n_prompt, 1 + ((i - n_prompt) * tm) // DEC_SEQ, 0)


def _ada_kernel(cond_ref, w_ref, b_ref, o_ref):
    c = cond_ref[...]
    a = (c * jax.nn.sigmoid(c)).astype(BF16)
    o_ref[...] = _dot(a, w_ref[...].astype(BF16)) + b_ref[...]


def _ada(cond, w_ada, b_ada):
    n = 6 * D_MODEL
    bn = 1536
    return pl.pallas_call(
        _ada_kernel,
        grid=(n // bn,),
        in_specs=[
            pl.BlockSpec((N_COND, D_MODEL), lambda j: (0, 0)),
            pl.BlockSpec((D_MODEL, bn), lambda j: (0, j)),
            pl.BlockSpec((1, bn), lambda j: (0, j)),
        ],
        out_specs=pl.BlockSpec((N_COND, bn), lambda j: (0, j)),
        out_shape=jax.ShapeDtypeStruct((N_COND, n), F32),
        compiler_params=pltpu.CompilerParams(
            dimension_semantics=("parallel",), vmem_limit_bytes=VMEM_LIMIT),
        name="ada_mod",
    )(cond, w_ada, b_ada)


O_CQ = 3 * D_CONV
O_KR = O_CQ + Q_LORA + KV_LORA
O_GATE = O_KR + QK_ROPE
PREP_STEPS = 4


IN_COLS = O_GATE + 2 * D_MODEL
TAIL_ROWS = IN_COLS // 2
SMALL_BLK = 512
NT = (((1,), (1,)), ((), ()))


def _dot_nt(a, bt):
    return lax.dot_general(a, bt, NT, preferred_element_type=F32)


def _prep_kernel(wct_ref, wst_ref, c3_ref, sm_ref):
    c3_ref[...] = wct_ref[...].astype(BF16)
    n_lat = Q_LORA + KV_LORA
    sm_ref[0:n_lat, :] = wst_ref[0:n_lat, :].astype(BF16)
    sm_ref[n_lat:, :] = jnp.zeros((LANE, sm_ref.shape[1]), BF16)
    sm_ref[n_lat + ROPE_LANE0:n_lat + ROPE_LANE0 + QK_ROPE, :] = (
        wst_ref[n_lat:n_lat + QK_ROPE, :].astype(BF16))


def _prep_weights(w_in_t):
    cb = D_MODEL // PREP_STEPS
    col = lambda rows, blk=0: pl.BlockSpec((rows, cb), lambda i: (blk, i))
    return pl.pallas_call(
        _prep_kernel,
        grid=(PREP_STEPS,),
        in_specs=[col(O_CQ), col(SMALL_BLK, O_CQ // SMALL_BLK)],
        out_specs=[col(O_CQ), col(SMALL_COLS)],
        out_shape=[jax.ShapeDtypeStruct((O_CQ, D_MODEL), BF16),
                   jax.ShapeDtypeStruct((SMALL_COLS, D_MODEL), BF16)],
        compiler_params=pltpu.CompilerParams(
            dimension_semantics=("parallel",), vmem_limit_bytes=VMEM_LIMIT),
        name="weight_casts",
    )(w_in_t, w_in_t)


def _stream_maps(tm):
    n_prompt = T_P // tm
    return (lambda i, *_: (jnp.minimum(i, n_prompt - 1), 0),
            lambda i, *_: (jnp.maximum(i - n_prompt, 0), 0))


def _inproj_kernel(xp_ref, xs_ref, mod_ref, n1_ref, wc_ref, ws_ref, cw_ref, qn_ref, kvn_ref, wuq_ref,
                   rope_ref, zc_ref, q_ref, ckv_ref, krs_ref, nckv_ref, nkr_ref):
    i = pl.program_id(0)
    is_sample = i >= T_P // TM_IN
    seq = jnp.where(is_sample, DEC_SEQ, SEQ)
    mod = mod_ref[pl.ds(_mod_row(i, TM_IN), 1), :]
    shift1 = mod[:, 0:D_MODEL]
    scale1 = mod[:, D_MODEL:2 * D_MODEL]
    x = jnp.where(is_sample, xs_ref[...], xp_ref[...])
    h = ((x * _rms(x)) * n1_ref[...]) * (1.0 + scale1) + shift1
    hb = h.astype(BF16)

    sm = _dot_nt(hb, ws_ref[...])
    cq = sm[:, 0:Q_LORA]
    ckv_raw = sm[:, Q_LORA:Q_LORA + KV_LORA]
    krs = sm[:, Q_LORA + KV_LORA:SMALL_COLS]
    cqn = (cq * _rms(cq)) * qn_ref[...]
    q = _dot(cqn.astype(BF16), wuq_ref[...])
    ckv = (ckv_raw * _rms(ckv_raw)) * kvn_ref[...]
    ckv_ref[...] = ckv

    @pl.when(jnp.logical_not(is_sample))
    def _():
        nckv_ref[...] = ckv
        nkr_ref[...] = krs[:, ROPE_LANE0:ROPE_LANE0 + QK_ROPE]

    cos = rope_ref[0]
    sin_lo = rope_ref[1]
    sin_hi = rope_ref[2]

    def rot(v):
        return v * cos + pltpu.roll(v, 8, 1) * sin_lo + pltpu.roll(v, LANE - 8, 1) * sin_hi

    krs_ref[...] = rot(krs)
    for hh in range(N_HEADS):
        q_ref[:, LANE * hh:LANE * (hh + 1)] = rot(q[:, LANE * hh:LANE * (hh + 1)]).astype(BF16)

    pos = lax.broadcasted_iota(jnp.int32, (TM_IN, 1), 0) & (seq - 1)
    first = pos == 0
    last = pos == seq - 1
    for j in range(D_CONV // CONV_CHUNK):
        c0 = j * CONV_CHUNK
        bg = _dot_nt(hb, wc_ref[c0:c0 + CONV_CHUNK, :])
        cg = _dot_nt(hb, wc_ref[D_CONV + c0:D_CONV + c0 + CONV_CHUNK, :])
        ui = _dot_nt(hb, wc_ref[2 * D_CONV + c0:2 * D_CONV + c0 + CONV_CHUNK, :])
        u = cg * ui
        u_prev = jnp.where(first, 0.0, pltpu.roll(u, 1, 0))
        u_next = jnp.where(last, 0.0, pltpu.roll(u, TM_IN - 1, 0))
        cw = cw_ref[:, c0:c0 + CONV_CHUNK]
        conv = u_prev * cw[0:1] + u * cw[1:2] + u_next * cw[2:3]
        zc_ref[:, c0:c0 + CONV_CHUNK] = (bg * conv).astype(BF16)


def _inproj(xp, xs, mod, norm1, w_conv3_t, w_small_t, conv_w, q_norm, kv_norm, w_uq_slot, rope_tabs):
    n_prompt = T_P // TM_IN
    const = lambda i: (0, 0)
    pmap, smap = _stream_maps(TM_IN)
    return pl.pallas_call(
        _inproj_kernel,
        grid=(T // TM_IN,),
        in_specs=[
            pl.BlockSpec((TM_IN, D_MODEL), pmap),
            pl.BlockSpec((TM_IN, D_MODEL), smap),
            pl.BlockSpec((N_COND, 6 * D_MODEL), const),
            pl.BlockSpec((1, D_MODEL), const),
            pl.BlockSpec((O_CQ, D_MODEL), const),
            pl.BlockSpec((SMALL_COLS, D_MODEL), const),
            pl.BlockSpec((3, D_CONV), const),
            pl.BlockSpec((1, Q_LORA), const),
            pl.BlockSpec((1, KV_LORA), const),
            pl.BlockSpec((Q_LORA, N_HEADS * LANE), const),
            pl.BlockSpec((None, 3, TM_IN, LANE),
                         lambda i: (jnp.where(i >= n_prompt, 1, 0), 0, 0, 0)),
        ],
        out_specs=[
            pl.BlockSpec((TM_IN, D_CONV), lambda i: (i, 0)),
            pl.BlockSpec((TM_IN, N_HEADS * LANE), lambda i: (i, 0)),
            pl.BlockSpec((TM_IN, KV_LORA), lambda i: (i, 0)),
            pl.BlockSpec((TM_IN, LANE), lambda i: (i, 0)),
            pl.BlockSpec((TM_IN, KV_LORA), pmap),
            pl.BlockSpec((TM_IN, QK_ROPE), pmap),
        ],
        out_shape=[
            jax.ShapeDtypeStruct((T, D_CONV), BF16),
            jax.ShapeDtypeStruct((T, N_HEADS * LANE), BF16),
            jax.ShapeDtypeStruct((T, KV_LORA), F32),
            jax.ShapeDtypeStruct((T, LANE), F32),
            jax.ShapeDtypeStruct((T_P, KV_LORA), F32),
            jax.ShapeDtypeStruct((T_P, QK_ROPE), F32),
        ],
        compiler_params=pltpu.CompilerParams(
            dimension_semantics=("arbitrary",), vmem_limit_bytes=VMEM_LIMIT),
        name="in_proj",
    )(xp, xs, mod, norm1, w_conv3_t, w_small_t, conv_w, q_norm, kv_norm, w_uq_slot, rope_tabs)


def _fill_kv(ckv, krs, wukv_ref, kf_scr, v_scr, off):
    m = ckv.shape[0]
    kv = _dot(ckv.astype(BF16), wukv_ref[...])
    for hh in range(N_HEADS):
        kf_scr[hh, off:off + m, :] = (kv[:, LANE * hh:LANE * (hh + 1)] + krs).astype(BF16)
    v_scr[off:off + m, :] = kv[:, N_HEADS * LANE:].astype(BF16)


def _attend(q_ref, r0, rows, kf_scr, v_scr, o_ref):
    for pair in range(N_HEADS // 2):
        acc = None
        for hh in (2 * pair, 2 * pair + 1):
            qh = q_ref[r0:r0 + rows, LANE * hh:LANE * (hh + 1)]
            s = _dot_nt(qh, kf_scr[hh]) * ATTN_SCALE
            e = jnp.exp(s - jnp.max(s, axis=-1, keepdims=True))
            p = (e / jnp.sum(e, axis=-1, keepdims=True)).astype(BF16)
            part = _dot(p, v_scr[:, LANE * hh:LANE * (hh + 1)])
            acc = part if acc is None else acc + part
        o_ref[r0:r0 + rows, LANE * pair:LANE * (pair + 1)] = acc.astype(BF16)


def _attn_prompt_kernel(q_ref, ckv_ref, krs_ref, wukv_ref, o_ref, kf_scr, v_scr):
    for s in range(PROMPT_SEQS):
        r0 = s * SEQ
        _fill_kv(ckv_ref[r0:r0 + SEQ, :], krs_ref[r0:r0 + SEQ, :], wukv_ref, kf_scr, v_scr, 0)
        _attend(q_ref, r0, SEQ, kf_scr, v_scr, o_ref)


def _attn_sample_kernel(q_ref, ckv_ref, cckv_ref, krs_ref, ckrs_ref, wukv_ref, wg_ref, wco_ref, wo_ref,
                        wmix_ref, o_ref, g_ref, co_ref, ob_ref, mix_ref, kf_scr, v_scr):
    @pl.when(pl.program_id(1) == 0)
    def _():
        _fill_kv(ckv_ref[...], krs_ref[...], wukv_ref, kf_scr, v_scr, 0)
        _fill_kv(cckv_ref[...], ckrs_ref[...], wukv_ref, kf_scr, v_scr, DEC_SEQ)

    _attend(q_ref, 0, Q_BLK_S, kf_scr, v_scr, o_ref)
    g_ref[...] = wg_ref[...].astype(BF16)
    co_ref[...] = wco_ref[...].astype(BF16)
    ob_ref[...] = wo_ref[...].astype(BF16)
    mix_ref[...] = wmix_ref[...].astype(BF16)


def _attention(q, ckv, krs, cache_ckv, cache_krs, w_ukv_slot, w_in_t, w_conv_out, w_o, w_mix_out):
    kv_cols = 2 * N_HEADS * LANE
    n_o = N_HEADS * V_HEAD
    steps = BATCH // PROMPT_SEQS
    rows = PROMPT_SEQS * SEQ
    o_prompt = pl.pallas_call(
        _attn_prompt_kernel,
        grid=(steps,),
        in_specs=[
            pl.BlockSpec((rows, N_HEADS * LANE), lambda b: (b, 0)),
            pl.BlockSpec((rows, KV_LORA), lambda b: (b, 0)),
            pl.BlockSpec((rows, LANE), lambda b: (b, 0)),
            pl.BlockSpec((KV_LORA, kv_cols), lambda b: (0, 0)),
        ],
        out_specs=pl.BlockSpec((rows, n_o), lambda b: (b, 0)),
        out_shape=jax.ShapeDtypeStruct((T_P, n_o), BF16),
        scratch_shapes=[pltpu.VMEM((N_HEADS, SEQ, LANE), BF16),
                        pltpu.VMEM((SEQ, N_HEADS * LANE), BF16)],
        compiler_params=pltpu.CompilerParams(dimension_semantics=("parallel",),
                                             vmem_limit_bytes=VMEM_LIMIT),
        name="attn_prompt",
    )(q, ckv, krs, w_ukv_slot)

    m_all = DEC_SEQ + PAST_LEN
    nq = DEC_SEQ // Q_BLK_S
    q0 = T_P // Q_BLK_S
    s0 = T_P // DEC_SEQ
    n_step = DEC_BATCH * nq
    share = lambda n: pl.BlockSpec((n // n_step, D_MODEL), lambda b, j: (b * nq + j, 0))
    gate_rows = 2 * D_MODEL // n_step
    o_sample, w_gates_t, w_co_b, w_o_b, w_mix_b = pl.pallas_call(
        _attn_sample_kernel,
        grid=(DEC_BATCH, nq),
        in_specs=[
            pl.BlockSpec((Q_BLK_S, N_HEADS * LANE), lambda b, j: (q0 + b * nq + j, 0)),
            pl.BlockSpec((DEC_SEQ, KV_LORA), lambda b, j: (s0 + b, 0)),
            pl.BlockSpec((None, PAST_LEN, KV_LORA), lambda b, j: (b, 0, 0)),
            pl.BlockSpec((DEC_SEQ, LANE), lambda b, j: (s0 + b, 0)),
            pl.BlockSpec((None, PAST_LEN, LANE), lambda b, j: (b, 0, 0)),
            pl.BlockSpec((KV_LORA, kv_cols), lambda b, j: (0, 0)),
            pl.BlockSpec((pl.Element(gate_rows), pl.Element(D_MODEL)),
                         lambda b, j: (pl.multiple_of(O_GATE + (b * nq + j) * gate_rows, 32), 0)),
            share(D_CONV), share(n_o), share(D_MODEL),
        ],
        out_specs=[pl.BlockSpec((Q_BLK_S, n_o), lambda b, j: (b * nq + j, 0)),
                   share(2 * D_MODEL), share(D_CONV), share(n_o), share(D_MODEL)],
        out_shape=[jax.ShapeDtypeStruct((T_S, n_o), BF16),
                   jax.ShapeDtypeStruct((2 * D_MODEL, D_MODEL), BF16),
                   jax.ShapeDtypeStruct((D_CONV, D_MODEL), BF16),
                   jax.ShapeDtypeStruct((n_o, D_MODEL), BF16),
                   jax.ShapeDtypeStruct((D_MODEL, D_MODEL), BF16)],
        scratch_shapes=[pltpu.VMEM((N_HEADS, m_all, LANE), BF16),
                        pltpu.VMEM((m_all, N_HEADS * LANE), BF16)],
        compiler_params=pltpu.CompilerParams(dimension_semantics=("arbitrary", "arbitrary"),
                                             vmem_limit_bytes=VMEM_LIMIT),
        name="attn_sample",
    )(q, ckv, cache_ckv, krs, cache_krs, w_ukv_slot, w_in_t, w_conv_out, w_o, w_mix_out)
    return o_prompt, o_sample, w_gates_t, w_co_b, w_o_b, w_mix_b


def _route(logits):
    lane = lax.broadcasted_iota(jnp.int32, logits.shape, 1)
    neg = -jnp.inf
    big = jnp.int32(1 << 20)
    gmask = (lane >= N_EXPERTS) & (lane < N_EXPERTS + N_GROUPS)
    gl = jnp.where(gmask, logits, neg)
    gmax = jnp.max(gl, axis=-1, keepdims=True)
    gsum = jnp.sum(jnp.where(gmask, jnp.exp(gl - gmax), 0.0), axis=-1, keepdims=True)
    p_g = 1.0 / gsum
    g_idx = jnp.min(jnp.where(gl == gmax, lane, big), axis=-1, keepdims=True) - N_EXPERTS

    emask = (lane < N_EXPERTS) & ((lane >> 3) == g_idx)
    el = jnp.where(emask, logits, neg)
    m1 = jnp.max(el, axis=-1, keepdims=True)
    i1 = jnp.min(jnp.where(el == m1, lane, big), axis=-1, keepdims=True)
    el2 = jnp.where(lane == i1, neg, el)
    m2 = jnp.max(el2, axis=-1, keepdims=True)
    i2 = jnp.min(jnp.where(el2 == m2, lane, big), axis=-1, keepdims=True)
    z = jnp.sum(jnp.where(emask, jnp.exp(el - m1), 0.0), axis=-1, keepdims=True)
    p1 = 1.0 / z
    p2 = jnp.exp(m2 - m1) / z
    tot = p1 + p2
    w1 = p_g * p1 / tot
    w2 = p_g * p2 / tot
    return jnp.where(lane == i1, w1, 0.0) + jnp.where(lane == i2, w2, 0.0), g_idx


def _post_kernel(xp_ref, xs_ref, mod_ref, n1_ref, wg_ref, zc_ref, op_ref, os_ref, wco_ref, wo_ref,
                 wmix_ref, n2_ref, wr_ref, x1_ref, h3_ref, meta_ref, cnt_ref, gr_ref):
    i = pl.program_id(0)
    is_sample = i >= T_P // TM_POST
    x = jnp.where(is_sample, xs_ref[...], xp_ref[...])
    o = jnp.where(is_sample, os_ref[...], op_ref[...])
    mod = mod_ref[pl.ds(_mod_row(i, TM_POST), 1), :]
    shift1 = mod[:, 0:D_MODEL]
    scale1 = mod[:, D_MODEL:2 * D_MODEL]
    gate1 = mod[:, 2 * D_MODEL:3 * D_MODEL]
    shift2 = mod[:, 3 * D_MODEL:4 * D_MODEL]
    scale2 = mod[:, 4 * D_MODEL:5 * D_MODEL]
    y_conv = _dot(zc_ref[...], wco_ref[...])
    y_mla = _dot(o, wo_ref[...])
    h = ((x * _rms(x)) * n1_ref[...]) * (1.0 + scale1) + shift1
    g = _dot_nt(h.astype(BF16), wg_ref[...])
    merged = (jax.nn.sigmoid(g[:, 0:D_MODEL]) * y_conv
              + jax.nn.sigmoid(g[:, D_MODEL:2 * D_MODEL]) * y_mla)
    y = _dot(merged.astype(BF16), wmix_ref[...])
    x1 = x + gate1 * y
    x1_ref[...] = x1
    h2 = ((x1 * _rms(x1)) * n2_ref[...]) * (1.0 + scale2) + shift2
    h2_hi = h2.astype(BF16)
    h2_lo = (h2 - h2_hi.astype(F32)).astype(BF16)
    hh = _dot(h2_hi, wr_ref[...])
    logits = hh[:, 0:LANE] + hh[:, LANE:2 * LANE] + _dot(h2_lo, wr_ref[:, 0:LANE])
    comb, g_idx = _route(logits)

    lane = lax.broadcasted_iota(jnp.int32, comb.shape, 1)
    onehot = lane == g_idx + N_EXPERTS
    r_i = lax.broadcasted_iota(jnp.int32, (TM_POST, TM_POST), 0)
    c_i = lax.broadcasted_iota(jnp.int32, (TM_POST, TM_POST), 1)
    lower = jnp.where(c_i < r_i, 1.0, 0.0).astype(BF16)
    before = _dot(lower, jnp.where(onehot, 1.0, 0.0).astype(BF16))
    rank = jnp.sum(jnp.where(onehot, before, 0.0), axis=-1, keepdims=True)
    counts = jnp.sum(jnp.where(onehot, 1.0, 0.0), axis=0, keepdims=True)
    cnt_ref[...] = jnp.broadcast_to(counts, cnt_ref.shape)

    meta_ref[...] = comb
    idx = jnp.where(lane == GID_LANE, g_idx.astype(F32), 0.0) + jnp.where(lane == RANK_LANE, rank, 0.0)
    idx_hi = idx.astype(BF16)
    idx_lo = (idx - idx_hi.astype(F32)).astype(BF16)
    s_row = lax.broadcasted_iota(jnp.int32, (8, LANE), 0)
    s_lane = lax.broadcasted_iota(jnp.int32, (8, LANE), 1)
    sel = jnp.where(s_lane == GID_LANE + s_row, 1.0, 0.0).astype(BF16)
    gr_ref[...] = _dot_nt(sel, idx_hi) + _dot_nt(sel, idx_lo)
    for c in range(N_SLAB):
        h3_ref[pl.ds(c, TM_POST, stride=N_SLAB), :] = h2[:, LANE * c:LANE * (c + 1)]


def _post(xp, xs, mod, norm1, w_gates_t, zc, o_p, o_s, w_conv_out, w_o, w_mix_out, norm2, w_route):
    const = lambda i: (0, 0)
    row = lambda i: (i, 0)
    pmap, smap = _stream_maps(TM_POST)
    return pl.pallas_call(
        _post_kernel,
        grid=(T // TM_POST,),
        in_specs=[
            pl.BlockSpec((TM_POST, D_MODEL), pmap),
            pl.BlockSpec((TM_POST, D_MODEL), smap),
            pl.BlockSpec((N_COND, 6 * D_MODEL), const),
            pl.BlockSpec((1, D_MODEL), const),
            pl.BlockSpec((2 * D_MODEL, D_MODEL), const),
            pl.BlockSpec((TM_POST, D_CONV), row),
            pl.BlockSpec((TM_POST, N_HEADS * V_HEAD), pmap),
            pl.BlockSpec((TM_POST, N_HEADS * V_HEAD), smap),
            pl.BlockSpec((D_CONV, D_MODEL), const),
            pl.BlockSpec((N_HEADS * V_HEAD, D_MODEL), const),
            pl.BlockSpec((D_MODEL, D_MODEL), const),
            pl.BlockSpec((1, D_MODEL), const),
            pl.BlockSpec((D_MODEL, 2 * LANE), const),
        ],
        out_specs=[
            pl.BlockSpec((TM_POST, D_MODEL), row),
            pl.BlockSpec((TM_POST * N_SLAB, LANE), row),
            pl.BlockSpec((TM_POST, LANE), row),
            pl.BlockSpec((None, 8, LANE), lambda i: (i, 0, 0)),
            pl.BlockSpec((8, TM_POST), lambda i: (0, i)),
        ],
        out_shape=[
            jax.ShapeDtypeStruct((T, D_MODEL), F32),
            jax.ShapeDtypeStruct((T * N_SLAB, LANE), F32),
            jax.ShapeDtypeStruct((T, LANE), F32),
            jax.ShapeDtypeStruct((T // TM_POST, 8, LANE), F32),
            jax.ShapeDtypeStruct((8, T), F32),
        ],
        compiler_params=pltpu.CompilerParams(
            dimension_semantics=("parallel",), vmem_limit_bytes=VMEM_LIMIT),
        name="post_mixer",
    )(xp, xs, mod, norm1, w_gates_t, zc, o_p, o_s, w_conv_out, w_o, w_mix_out, norm2, w_route)


DISP_IN = T // TM_DISP
DISP_OUT = T // TM_MOE


def _dispatch_kernel(pos_ref, h3_ref, m_ref, hs_ref, ms_ref, xs_ref, mss_ref):
    i = pl.program_id(0)

    @pl.when(i < DISP_IN)
    def _():
        base = i * TM_DISP

        def body(r, carry):
            p = pos_ref[base + r]
            xs_ref[_slab(p), :] = h3_ref[_slab(r), :]
            mss_ref[pl.ds(p, 1), :] = m_ref[pl.ds(r, 1), :]
            return carry

        lax.fori_loop(0, TM_DISP, body, 0, unroll=8)

    for k in range(DISP_OUT):
        @pl.when(i == DISP_IN + k)
        def _():
            row0 = k * TM_MOE
            for c in range(N_SLAB):
                hs_ref[:, LANE * c:LANE * (c + 1)] = (
                    xs_ref[pl.ds(row0 * N_SLAB + c, TM_MOE, stride=N_SLAB), :].astype(BF16))
            ms_ref[...] = mss_ref[row0:row0 + TM_MOE, :]


def _dispatch(pos, h3, meta):
    n_slab = N_SLAB
    in_map = lambda i, pos: (jnp.minimum(i, DISP_IN - 1), 0)
    out_map = lambda i, pos: (jnp.maximum(i - DISP_IN, 0), 0)
    return pl.pallas_call(
        _dispatch_kernel,
        grid_spec=pltpu.PrefetchScalarGridSpec(
            num_scalar_prefetch=1,
            grid=(DISP_IN + DISP_OUT,),
            in_specs=[pl.BlockSpec((TM_DISP * n_slab, LANE), in_map),
                      pl.BlockSpec((TM_DISP, LANE), in_map)],
            out_specs=[pl.BlockSpec((TM_MOE, D_MODEL), out_map),
                       pl.BlockSpec((TM_MOE, LANE), out_map)],
            scratch_shapes=[pltpu.VMEM((T * n_slab, LANE), F32),
                            pltpu.VMEM((T, LANE), F32)],
        ),
        out_shape=[jax.ShapeDtypeStruct((T, D_MODEL), BF16),
                   jax.ShapeDtypeStruct((T, LANE), F32)],
        compiler_params=pltpu.CompilerParams(
            dimension_semantics=("arbitrary",), vmem_limit_bytes=VMEM_LIMIT),
        name="moe_dispatch",
    )(pos, h3, meta)


def _moe_kernel(vt_ref, vg_ref, vlo_ref, vhi_ref, vfirst_ref, vlast_ref, vvalid_ref,
                hs_ref, ms_ref, wup_ref, wgate_ref, wdown_ref, y3_ref, acc_ref):
    v = pl.program_id(0)
    j = pl.program_id(1)
    valid = vvalid_ref[v] == 1
    lo = vlo_ref[v]
    hi = vhi_ref[v]
    e0 = vg_ref[v] * EXP_PER_GROUP + j * MOE_EPS
    full = (hi - lo) * 4 >= TM_MOE * 3

    @pl.when(valid & (j == 0) & (vfirst_ref[v] == 1))
    def _():
        acc_ref[...] = jnp.zeros_like(acc_ref)

    def expert_rows(r0, rows):
        w_in2 = jnp.concatenate(
            [w[k].astype(BF16) for k in range(MOE_EPS) for w in (wup_ref, wgate_ref)], axis=1)
        ag = _dot(hs_ref[r0:r0 + rows, :], w_in2)
        comb = ms_ref[r0:r0 + rows, :]
        lane = lax.broadcasted_iota(jnp.int32, comb.shape, 1)
        acts = []
        for k in range(MOE_EPS):
            a = ag[:, 2 * k * D_EXPERT:(2 * k + 1) * D_EXPERT]
            g = ag[:, (2 * k + 1) * D_EXPERT:(2 * k + 2) * D_EXPERT]
            cw = jnp.sum(jnp.where(lane == e0 + k, comb, 0.0), axis=-1, keepdims=True)
            acts.append(((g * jax.nn.sigmoid(g)) * a * cw).astype(BF16))
        w_out = jnp.concatenate([wdown_ref[k].astype(BF16) for k in range(MOE_EPS)], axis=0)
        acc_ref[r0:r0 + rows, :] += _dot(jnp.concatenate(acts, axis=1), w_out)

    @pl.when(valid & full)
    def _():
        expert_rows(0, TM_MOE)

    @pl.when(valid & jnp.logical_not(full))
    def _():
        for s in range(TM_MOE // MOE_SUB):
            r0 = s * MOE_SUB

            @pl.when((lo < r0 + MOE_SUB) & (hi > r0))
            def _():
                expert_rows(r0, MOE_SUB)

    @pl.when(valid & (j == EXP_PER_GROUP // MOE_EPS - 1) & (vlast_ref[v] == 1))
    def _():
        for c in range(N_SLAB):
            y3_ref[pl.ds(c, TM_MOE, stride=N_SLAB), :] = acc_ref[:, LANE * c:LANE * (c + 1)]


def _moe(sched, hs, ms, w_up, w_gate, w_down):
    steps = EXP_PER_GROUP // MOE_EPS
    wmap = lambda v, j, vt, vg, vlo, vhi, vfirst, vlast, vvalid: (
        vg[v] * steps + jnp.where(vvalid[v] == 1, j, steps - 1), 0, 0)
    tmap = lambda v, j, vt, *_: (vt[v], 0)
    n_slab = D_MODEL // LANE
    return pl.pallas_call(
        _moe_kernel,
        grid_spec=pltpu.PrefetchScalarGridSpec(
            num_scalar_prefetch=7,
            grid=(N_VISITS, steps),
            in_specs=[
                pl.BlockSpec((TM_MOE, D_MODEL), tmap),
                pl.BlockSpec((TM_MOE, LANE), tmap),
                pl.BlockSpec((MOE_EPS, D_MODEL, D_EXPERT), wmap),
                pl.BlockSpec((MOE_EPS, D_MODEL, D_EXPERT), wmap),
                pl.BlockSpec((MOE_EPS, D_EXPERT, D_MODEL), wmap),
            ],
            out_specs=pl.BlockSpec((TM_MOE * n_slab, LANE), tmap),
            scratch_shapes=[pltpu.VMEM((TM_MOE, D_MODEL), F32)],
        ),
        out_shape=jax.ShapeDtypeStruct((T * n_slab, LANE), F32),
        compiler_params=pltpu.CompilerParams(
            dimension_semantics=("arbitrary", "arbitrary"), vmem_limit_bytes=VMEM_LIMIT),
        name="moe_grouped",
    )(*sched, hs, ms, w_up, w_gate, w_down)


def _final_kernel(pos_ref, ys_ref, x1_ref, mod_ref, fn_ref, yp_ref, ysm_ref, g_ref):
    i = pl.program_id(0)
    base = i * TM_FINAL

    def body(r, carry):
        g_ref[_slab(r), :] = ys_ref[_slab(pos_ref[base + r]), :]
        return carry

    lax.fori_loop(0, TM_FINAL, body, 0, unroll=8)
    mod = mod_ref[pl.ds(_mod_row(i, TM_FINAL), 1), :]
    gate2 = mod[:, 5 * D_MODEL:6 * D_MODEL]
    moe = jnp.concatenate([g_ref[pl.ds(c, TM_FINAL, stride=N_SLAB), :] for c in range(N_SLAB)], axis=1)
    x2 = x1_ref[...] + gate2 * moe
    y = (x2 * _rms(x2)) * fn_ref[...]
    is_sample = i >= T_P // TM_FINAL

    @pl.when(jnp.logical_not(is_sample))
    def _():
        yp_ref[...] = y

    @pl.when(is_sample)
    def _():
        ysm_ref[...] = y


def _final(pos, ys, x1, mod, final_norm):
    n_slab = D_MODEL // LANE
    pmap, smap = _stream_maps(TM_FINAL)
    return pl.pallas_call(
        _final_kernel,
        grid_spec=pltpu.PrefetchScalarGridSpec(
            num_scalar_prefetch=1,
            grid=(T // TM_FINAL,),
            in_specs=[
                pl.BlockSpec((T * n_slab, LANE), lambda i, pos: (0, 0), pipeline_mode=pl.Buffered(1)),
                pl.BlockSpec((TM_FINAL, D_MODEL), lambda i, pos: (i, 0)),
                pl.BlockSpec((N_COND, 6 * D_MODEL), lambda i, pos: (0, 0)),
                pl.BlockSpec((1, D_MODEL), lambda i, pos: (0, 0)),
            ],
            out_specs=[pl.BlockSpec((TM_FINAL, D_MODEL), pmap),
                       pl.BlockSpec((TM_FINAL, D_MODEL), smap)],
            scratch_shapes=[pltpu.VMEM((TM_FINAL * n_slab, LANE), F32)],
        ),
        out_shape=[jax.ShapeDtypeStruct((T_P, D_MODEL), F32),
                   jax.ShapeDtypeStruct((T_S, D_MODEL), F32)],
        compiler_params=pltpu.CompilerParams(
            dimension_semantics=("arbitrary",), vmem_limit_bytes=VMEM_LIMIT),
        name="moe_unsort_final",
    )(pos, ys, x1, mod, final_norm)


def _schedule(gr, cnt):
    n_tiles_post = T // TM_POST
    counts = cnt[:, 0, N_EXPERTS:N_EXPERTS + N_GROUPS].astype(jnp.int32)
    gtot = jnp.sum(counts, axis=0)
    goff = jnp.cumsum(gtot) - gtot
    tile_base = goff[None, :] + jnp.cumsum(counts, axis=0) - counts
    gid = gr[0].astype(jnp.int32).reshape(n_tiles_post, TM_POST)
    rank = gr[1].astype(jnp.int32).reshape(n_tiles_post, TM_POST)
    pos = rank
    for grp in range(N_GROUPS):
        pos = pos + jnp.where(gid == grp, tile_base[:, grp:grp + 1], 0)
    pos = pos.reshape(T)

    n_tiles = T // TM_MOE
    t_lo = (jnp.arange(n_tiles, dtype=jnp.int32) * TM_MOE)[:, None]
    lo = jnp.clip(goff[None, :] - t_lo, 0, TM_MOE)
    hi = jnp.clip(goff[None, :] + gtot[None, :] - t_lo, 0, TM_MOE)
    ok = (hi > lo).reshape(-1)
    slot = jnp.cumsum(ok.astype(jnp.int32)) - 1
    n_ok = slot[-1] + 1
    sel = (slot[None, :] == jnp.arange(N_VISITS, dtype=jnp.int32)[:, None]) & ok[None, :]

    def pick(vals):
        return jnp.sum(jnp.where(sel, vals.reshape(-1)[None, :], 0), axis=1).astype(jnp.int32)

    pair_tile = jnp.broadcast_to(jnp.arange(n_tiles, dtype=jnp.int32)[:, None], (n_tiles, N_GROUPS))
    pair_group = jnp.broadcast_to(jnp.arange(N_GROUPS, dtype=jnp.int32)[None, :], (n_tiles, N_GROUPS))
    vt, vg, vlo, vhi = pick(pair_tile), pick(pair_group), pick(lo), pick(hi)
    valid = jnp.arange(N_VISITS, dtype=jnp.int32) < n_ok
    last = jnp.maximum(n_ok - 1, 0)
    vt = jnp.where(valid, vt, vt[last])
    vg = jnp.where(valid, vg, vg[last])
    change = (vt[1:] != vt[:-1]).astype(jnp.int32)
    one = jnp.ones((1,), jnp.int32)
    first = jnp.concatenate([one, change])
    idx = jnp.arange(N_VISITS, dtype=jnp.int32)
    last = jnp.where(idx == n_ok - 1, 1, jnp.concatenate([change, one]))
    return pos, (vt, vg, vlo, vhi, first, last, valid.astype(jnp.int32))


def _rope_tables():
    n = np.arange(DEC_SEQ)
    pos = np.stack([n // GRID_W, n % GRID_W], axis=1).astype(np.float32)
    half = ROPE_AXIS // 2
    inv = (1.0 / (ROPE_BASE ** (np.arange(0, ROPE_AXIS, 2, dtype=np.float32) / ROPE_AXIS))).astype(np.float32)
    ang = (pos[:, :, None] * inv[None, None, :]).astype(np.float32)
    cos = np.cos(ang).astype(np.float32)
    sin = np.sin(ang).astype(np.float32)
    tabs = np.zeros((2, 3, DEC_SEQ, LANE), np.float32)
    tabs[:, 0] = 1.0
    for a in range(2):
        lo = ROPE_LANE0 + a * ROPE_AXIS
        tabs[1, 0, :, lo:lo + half] = cos[:, a]
        tabs[1, 0, :, lo + half:lo + 2 * half] = cos[:, a]
        tabs[1, 1, :, lo + half:lo + 2 * half] = sin[:, a]
        tabs[1, 2, :, lo:lo + half] = -sin[:, a]
    return jnp.asarray(tabs)


def kernel(x_prompt, x_sample, cache_ckv, cache_krope, c, c_ctx, norm1, w_ada, b_ada, w_in, conv_w,
           w_conv_out, q_norm, w_uq, kv_norm, w_ukv, w_o, w_mix_out, norm2, w_grp, w_exp, w_up,
           w_gate, w_down, final_norm):
    l = 0
    xp = x_prompt.reshape(T_P, D_MODEL)
    xs = x_sample.reshape(T_S, D_MODEL)
    cond = jnp.concatenate(
        [c_ctx[None, :], c, jnp.zeros((N_COND - 1 - DEC_BATCH, D_MODEL), F32)], axis=0)
    mod = _ada(cond, w_ada[l], b_ada[l][None, :])

    w_in_t = w_in[l].T
    w_conv3_t, w_small_t = _prep_weights(w_in_t)
    w_uq_slot = jnp.pad(w_uq[l].reshape(Q_LORA, N_HEADS, QK_NOPE + QK_ROPE),
                        ((0, 0), (0, 0), (0, LANE - QK_NOPE - QK_ROPE))
                        ).reshape(Q_LORA, N_HEADS * LANE).astype(BF16)
    wkv = w_ukv[l].reshape(KV_LORA, N_HEADS, QK_NOPE + V_HEAD)
    wk_slot = jnp.pad(wkv[:, :, :QK_NOPE], ((0, 0), (0, 0), (0, LANE - QK_NOPE)))
    wv = wkv[:, :, QK_NOPE:].reshape(KV_LORA, N_HEADS // 2, 2, V_HEAD)
    zero = jnp.zeros_like(wv[:, :, 0])
    wv_slot = jnp.stack([jnp.concatenate([wv[:, :, 0], zero], axis=-1),
                         jnp.concatenate([zero, wv[:, :, 1]], axis=-1)], axis=2)
    w_ukv_slot = jnp.concatenate([wk_slot.reshape(KV_LORA, N_HEADS * LANE),
                                  wv_slot.reshape(KV_LORA, N_HEADS * LANE)], axis=1).astype(BF16)
    w_route = jnp.pad(jnp.concatenate([w_exp[l], w_grp[l]], axis=1),
                      ((0, 0), (0, LANE - N_EXPERTS - N_GROUPS)))
    w_route_hi = w_route.astype(BF16)
    w_route_lo = (w_route - w_route_hi.astype(F32)).astype(BF16)
    w_route2 = jnp.concatenate([w_route_hi, w_route_lo], axis=1)
    cache_krs = jnp.pad(cache_krope[:, l], ((0, 0), (0, 0), (ROPE_LANE0, LANE - ROPE_LANE0 - QK_ROPE)))

    zc, q, ckv, krs, nckv, nkr = _inproj(xp, xs, mod, norm1[l][None, :], w_conv3_t, w_small_t, conv_w[l],
                              q_norm[l][None, :], kv_norm[l][None, :], w_uq_slot, _rope_tables())
    o_p, o_s, w_gates_t, w_co_b, w_o_b, w_mix_b = _attention(
        q, ckv, krs, cache_ckv[:, l], cache_krs, w_ukv_slot, w_in_t, w_conv_out[l], w_o[l], w_mix_out[l])
    x1, h3, meta, cnt, gr = _post(xp, xs, mod, norm1[l][None, :], w_gates_t, zc, o_p, o_s, w_co_b, w_o_b,
                                  w_mix_b, norm2[l][None, :], w_route2)
    pos, sched = _schedule(gr, cnt)
    hs, ms = _dispatch(pos, h3, meta)
    ys = _moe(sched, hs, ms, w_up[l], w_gate[l], w_down[l])
    yp, ysm = _final(pos, ys, x1, mod, final_norm[None, :])

    y_prompt = yp.reshape(BATCH, SEQ, D_MODEL)
    y_sample = ysm.reshape(DEC_BATCH, DEC_SEQ, D_MODEL)
    new_ckv = nckv.reshape(BATCH, 1, SEQ, KV_LORA)
    new_krope = nkr.reshape(BATCH, 1, SEQ, QK_ROPE)
    return (y_prompt, y_sample, new_ckv, new_krope)
```

```python
import functools

import numpy as np
import jax
import jax.numpy as jnp
from jax import lax
from jax.experimental import pallas as pl
from jax.experimental.pallas import tpu as pltpu

F32 = jnp.float32
BF16 = jnp.bfloat16

D_MODEL = 1024
BATCH = 16
SEQ = 256
DEC_BATCH = 2
DEC_SEQ = 1024
PAST_LEN = 256
GRID_W = 64
N_HEADS = 8
QK_NOPE = 64
QK_ROPE = 32
V_HEAD = 64
Q_LORA = 256
KV_LORA = 128
ROPE_AXIS = QK_ROPE // 2
ROPE_BASE = 10000.0
ATTN_SCALE = (QK_NOPE + QK_ROPE) ** -0.5
D_CONV = D_MODEL
N_GROUPS = 4
EXP_PER_GROUP = 8
N_EXPERTS = N_GROUPS * EXP_PER_GROUP
D_EXPERT = 256
EPS = 1e-6

T_P = BATCH * SEQ
T_S = DEC_BATCH * DEC_SEQ
T = T_P + T_S
N_COND = 8
LANE = 128
ROPE_LANE0 = QK_NOPE
SMALL_COLS = Q_LORA + KV_LORA + LANE
VMEM_LIMIT = 56 * 1024 * 1024

TM_IN = 1024
TM_POST = 512
TM_MOE = T // N_GROUPS
TM_FINAL = 512
TM_DISP = 1024
N_SLAB = D_MODEL // LANE
MOE_SUB = 256
MOE_EPS = 2
N_VISITS = T // TM_MOE + N_GROUPS - 1
GID_LANE = 40
RANK_LANE = 41
Q_BLK = 256
Q_BLK_S = 512
PROMPT_SEQS = 2
CONV_CHUNK = 256


def _dot(a, b):
    return jnp.dot(a, b, preferred_element_type=F32)


def _rms(x):
    return lax.rsqrt(jnp.mean(x * x, axis=-1, keepdims=True) + EPS)


def _slab(t):
    return pl.ds(pl.multiple_of(t * N_SLAB, N_SLAB), N_SLAB)


def _mod_row(i, tm):
    n_prompt = T_P // tm
    return jnp.where(i >= n_prompt, 1 + ((i - n_prompt) * tm) // DEC_SEQ, 0)


def _ada_kernel(cond_ref, w_ref, b_ref, o_ref):
    c = cond_ref[...]
    a = (c * jax.nn.sigmoid(c)).astype(BF16)
    o_ref[...] = _dot(a, w_ref[...].astype(BF16)) + b_ref[...]


def _ada(cond, w_ada, b_ada):
    n = 6 * D_MODEL
    bn = 1536
    return pl.pallas_call(
        _ada_kernel,
        grid=(n // bn,),
        in_specs=[
            pl.BlockSpec((N_COND, D_MODEL), lambda j: (0, 0)),
            pl.BlockSpec((D_MODEL, bn), lambda j: (0, j)),
            pl.BlockSpec((1, bn), lambda j: (0, j)),
        ],
        out_specs=pl.BlockSpec((N_COND, bn), lambda j: (0, j)),
        out_shape=jax.ShapeDtypeStruct((N_COND, n), F32),
        compiler_params=pltpu.CompilerParams(
            dimension_semantics=("parallel",), vmem_limit_bytes=VMEM_LIMIT),
        name="ada_mod",
    )(cond, w_ada, b_ada)


O_CQ = 3 * D_CONV
O_KR = O_CQ + Q_LORA + KV_LORA
O_GATE = O_KR + QK_ROPE
PREP_STEPS = 4


IN_COLS = O_GATE + 2 * D_MODEL
TAIL_ROWS = IN_COLS // 2
SMALL_BLK = 512
NT = (((1,), (1,)), ((), ()))


def _dot_nt(a, bt):
    return lax.dot_general(a, bt, NT, preferred_element_type=F32)


def _prep_kernel(wct_ref, wst_ref, c3_ref, sm_ref):
    c3_ref[...] = wct_ref[...].astype(BF16)
    n_lat = Q_LORA + KV_LORA
    sm_ref[0:n_lat, :] = wst_ref[0:n_lat, :].astype(BF16)
    sm_ref[n_lat:, :] = jnp.zeros((LANE, sm_ref.shape[1]), BF16)
    sm_ref[n_lat + ROPE_LANE0:n_lat + ROPE_LANE0 + QK_ROPE, :] = (
        wst_ref[n_lat:n_lat + QK_ROPE, :].astype(BF16))


def _prep_weights(w_in_t):
    cb = D_MODEL // PREP_STEPS
    col = lambda rows, blk=0: pl.BlockSpec((rows, cb), lambda i: (blk, i))
    return pl.pallas_call(
        _prep_kernel,
        grid=(PREP_STEPS,),
        in_specs=[col(O_CQ), col(SMALL_BLK, O_CQ // SMALL_BLK)],
        out_specs=[col(O_CQ), col(SMALL_COLS)],
        out_shape=[jax.ShapeDtypeStruct((O_CQ, D_MODEL), BF16),
                   jax.ShapeDtypeStruct((SMALL_COLS, D_MODEL), BF16)],
        compiler_params=pltpu.CompilerParams(
            dimension_semantics=("parallel",), vmem_limit_bytes=VMEM_LIMIT),
        name="weight_casts",
    )(w_in_t, w_in_t)


def _stream_maps(tm):
    n_prompt = T_P // tm
    return (lambda i, *_: (jnp.minimum(i, n_prompt - 1), 0),
            lambda i, *_: (jnp.maximum(i - n_prompt, 0), 0))


def _inproj_kernel(xp_ref, xs_ref, mod_ref, n1_ref, wc_ref, ws_ref, cw_ref, qn_ref, kvn_ref, wuq_ref,
                   rope_ref, zc_ref, q_ref, ckv_ref, krs_ref, nckv_ref, nkr_ref):
    i = pl.program_id(0)
    is_sample = i >= T_P // TM_IN
    seq = jnp.where(is_sample, DEC_SEQ, SEQ)
    mod = mod_ref[pl.ds(_mod_row(i, TM_IN), 1), :]
    shift1 = mod[:, 0:D_MODEL]
    scale1 = mod[:, D_MODEL:2 * D_MODEL]
    x = jnp.where(is_sample, xs_ref[...], xp_ref[...])
    h = ((x * _rms(x)) * n1_ref[...]) * (1.0 + scale1) + shift1
    hb = h.astype(BF16)

    sm = _dot_nt(hb, ws_ref[...])
    cq = sm[:, 0:Q_LORA]
    ckv_raw = sm[:, Q_LORA:Q_LORA + KV_LORA]
    krs = sm[:, Q_LORA + KV_LORA:SMALL_COLS]
    cqn = (cq * _rms(cq)) * qn_ref[...]
    q = _dot(cqn.astype(BF16), wuq_ref[...])
    ckv = (ckv_raw * _rms(ckv_raw)) * kvn_ref[...]
    ckv_ref[...] = ckv

    @pl.when(jnp.logical_not(is_sample))
    def _():
        nckv_ref[...] = ckv
        nkr_ref[...] = krs[:, ROPE_LANE0:ROPE_LANE0 + QK_ROPE]

    cos = rope_ref[0]
    sin_lo = rope_ref[1]
    sin_hi = rope_ref[2]

    def rot(v):
        return v * cos + pltpu.roll(v, 8, 1) * sin_lo + pltpu.roll(v, LANE - 8, 1) * sin_hi

    krs_ref[...] = rot(krs)
    for hh in range(N_HEADS):
        q_ref[:, LANE * hh:LANE * (hh + 1)] = rot(q[:, LANE * hh:LANE * (hh + 1)]).astype(BF16)

    pos = lax.broadcasted_iota(jnp.int32, (TM_IN, 1), 0) & (seq - 1)
    first = pos == 0
    last = pos == seq - 1
    for j in range(D_CONV // CONV_CHUNK):
        c0 = j * CONV_CHUNK
        bg = _dot_nt(hb, wc_ref[c0:c0 + CONV_CHUNK, :])
        cg = _dot_nt(hb, wc_ref[D_CONV + c0:D_CONV + c0 + CONV_CHUNK, :])
        ui = _dot_nt(hb, wc_ref[2 * D_CONV + c0:2 * D_CONV + c0 + CONV_CHUNK, :])
        u = cg * ui
        u_prev = jnp.where(first, 0.0, pltpu.roll(u, 1, 0))
        u_next = jnp.where(last, 0.0, pltpu.roll(u, TM_IN - 1, 0))
        cw = cw_ref[:, c0:c0 + CONV_CHUNK]
        conv = u_prev * cw[0:1] + u * cw[1:2] + u_next * cw[2:3]
        zc_ref[:, c0:c0 + CONV_CHUNK] = (bg * conv).astype(BF16)


def _inproj(xp, xs, mod, norm1, w_conv3_t, w_small_t, conv_w, q_norm, kv_norm, w_uq_slot, rope_tabs):
    n_prompt = T_P // TM_IN
    const = lambda i: (0, 0)
    pmap, smap = _stream_maps(TM_IN)
    return pl.pallas_call(
        _inproj_kernel,
        grid=(T // TM_IN,),
        in_specs=[
            pl.BlockSpec((TM_IN, D_MODEL), pmap),
            pl.BlockSpec((TM_IN, D_MODEL), smap),
            pl.BlockSpec((N_COND, 6 * D_MODEL), const),
            pl.BlockSpec((1, D_MODEL), const),
            pl.BlockSpec((O_CQ, D_MODEL), const),
            pl.BlockSpec((SMALL_COLS, D_MODEL), const),
            pl.BlockSpec((3, D_CONV), const),
            pl.BlockSpec((1, Q_LORA), const),
            pl.BlockSpec((1, KV_LORA), const),
            pl.BlockSpec((Q_LORA, N_HEADS * LANE), const),
            pl.BlockSpec((None, 3, TM_IN, LANE),
                         lambda i: (jnp.where(i >= n_prompt, 1, 0), 0, 0, 0)),
        ],
        out_specs=[
            pl.BlockSpec((TM_IN, D_CONV), lambda i: (i, 0)),
            pl.BlockSpec((TM_IN, N_HEADS * LANE), lambda i: (i, 0)),
            pl.BlockSpec((TM_IN, KV_LORA), lambda i: (i, 0)),
            pl.BlockSpec((TM_IN, LANE), lambda i: (i, 0)),
            pl.BlockSpec((TM_IN, KV_LORA), pmap),
            pl.BlockSpec((TM_IN, QK_ROPE), pmap),
        ],
        out_shape=[
            jax.ShapeDtypeStruct((T, D_CONV), BF16),
            jax.ShapeDtypeStruct((T, N_HEADS * LANE), BF16),
            jax.ShapeDtypeStruct((T, KV_LORA), F32),
            jax.ShapeDtypeStruct((T, LANE), F32),
            jax.ShapeDtypeStruct((T_P, KV_LORA), F32),
            jax.ShapeDtypeStruct((T_P, QK_ROPE), F32),
        ],
        compiler_params=pltpu.CompilerParams(
            dimension_semantics=("arbitrary",), vmem_limit_bytes=VMEM_LIMIT),
        name="in_proj",
    )(xp, xs, mod, norm1, w_conv3_t, w_small_t, conv_w, q_norm, kv_norm, w_uq_slot, rope_tabs)


def _fill_kv(ckv, krs, wukv_ref, kf_scr, v_scr, off):
    m = ckv.shape[0]
    kv = _dot(ckv.astype(BF16), wukv_ref[...])
    for hh in range(N_HEADS):
        kf_scr[hh, off:off + m, :] = (kv[:, LANE * hh:LANE * (hh + 1)] + krs).astype(BF16)
    v_scr[off:off + m, :] = kv[:, N_HEADS * LANE:].astype(BF16)


def _attend(q_ref, r0, rows, kf_scr, v_scr, o_ref):
    for pair in range(N_HEADS // 2):
        acc = None
        for hh in (2 * pair, 2 * pair + 1):
            qh = q_ref[r0:r0 + rows, LANE * hh:LANE * (hh + 1)]
            s = _dot_nt(qh, kf_scr[hh]) * ATTN_SCALE
            e = jnp.exp(s - jnp.max(s, axis=-1, keepdims=True))
            p = (e / jnp.sum(e, axis=-1, keepdims=True)).astype(BF16)
            part = _dot(p, v_scr[:, LANE * hh:LANE * (hh + 1)])
            acc = part if acc is None else acc + part
        o_ref[r0:r0 + rows, LANE * pair:LANE * (pair + 1)] = acc.astype(BF16)


def _attn_prompt_kernel(q_ref, ckv_ref, krs_ref, wukv_ref, o_ref, kf_scr, v_scr):
    for s in range(PROMPT_SEQS):
        r0 = s * SEQ
        _fill_kv(ckv_ref[r0:r0 + SEQ, :], krs_ref[r0:r0 + SEQ, :], wukv_ref, kf_scr, v_scr, 0)
        _attend(q_ref, r0, SEQ, kf_scr, v_scr, o_ref)


def _attn_sample_kernel(q_ref, ckv_ref, cckv_ref, krs_ref, ckrs_ref, wukv_ref, wg_ref, wco_ref, wo_ref,
                        wmix_ref, o_ref, g_ref, co_ref, ob_ref, mix_ref, kf_scr, v_scr):
    @pl.when(pl.program_id(1) == 0)
    def _():
        _fill_kv(ckv_ref[...], krs_ref[...], wukv_ref, kf_scr, v_scr, 0)
        _fill_kv(cckv_ref[...], ckrs_ref[...], wukv_ref, kf_scr, v_scr, DEC_SEQ)

    _attend(q_ref, 0, Q_BLK_S, kf_scr, v_scr, o_ref)
    g_ref[...] = wg_ref[...].astype(BF16)
    co_ref[...] = wco_ref[...].astype(BF16)
    ob_ref[...] = wo_ref[...].astype(BF16)
    mix_ref[...] = wmix_ref[...].astype(BF16)


def _attention(q, ckv, krs, cache_ckv, cache_krs, w_ukv_slot, w_in_t, w_conv_out, w_o, w_mix_out):
    kv_cols = 2 * N_HEADS * LANE
    n_o = N_HEADS * V_HEAD
    steps = BATCH // PROMPT_SEQS
    rows = PROMPT_SEQS * SEQ
    o_prompt = pl.pallas_call(
        _attn_prompt_kernel,
        grid=(steps,),
        in_specs=[
            pl.BlockSpec((rows, N_HEADS * LANE), lambda b: (b, 0)),
            pl.BlockSpec((rows, KV_LORA), lambda b: (b, 0)),
            pl.BlockSpec((rows, LANE), lambda b: (b, 0)),
            pl.BlockSpec((KV_LORA, kv_cols), lambda b: (0, 0)),
        ],
        out_specs=pl.BlockSpec((rows, n_o), lambda b: (b, 0)),
        out_shape=jax.ShapeDtypeStruct((T_P, n_o), BF16),
        scratch_shapes=[pltpu.VMEM((N_HEADS, SEQ, LANE), BF16),
                        pltpu.VMEM((SEQ, N_HEADS * LANE), BF16)],
        compiler_params=pltpu.CompilerParams(dimension_semantics=("parallel",),
                                             vmem_limit_bytes=VMEM_LIMIT),
        name="attn_prompt",
    )(q, ckv, krs, w_ukv_slot)

    m_all = DEC_SEQ + PAST_LEN
    nq = DEC_SEQ // Q_BLK_S
    q0 = T_P // Q_BLK_S
    s0 = T_P // DEC_SEQ
    n_step = DEC_BATCH * nq
    share = lambda n: pl.BlockSpec((n // n_step, D_MODEL), lambda b, j: (b * nq + j, 0))
    gate_rows = 2 * D_MODEL // n_step
    o_sample, w_gates_t, w_co_b, w_o_b, w_mix_b = pl.pallas_call(
        _attn_sample_kernel,
        grid=(DEC_BATCH, nq),
        in_specs=[
            pl.BlockSpec((Q_BLK_S, N_HEADS * LANE), lambda b, j: (q0 + b * nq + j, 0)),
            pl.BlockSpec((DEC_SEQ, KV_LORA), lambda b, j: (s0 + b, 0)),
            pl.BlockSpec((None, PAST_LEN, KV_LORA), lambda b, j: (b, 0, 0)),
            pl.BlockSpec((DEC_SEQ, LANE), lambda b, j: (s0 + b, 0)),
            pl.BlockSpec((None, PAST_LEN, LANE), lambda b, j: (b, 0, 0)),
            pl.BlockSpec((KV_LORA, kv_cols), lambda b, j: (0, 0)),
            pl.BlockSpec((pl.Element(gate_rows), pl.Element(D_MODEL)),
                         lambda b, j: (pl.multiple_of(O_GATE + (b * nq + j) * gate_rows, 32), 0)),
            share(D_CONV), share(n_o), share(D_MODEL),
        ],
        out_specs=[pl.BlockSpec((Q_BLK_S, n_o), lambda b, j: (b * nq + j, 0)),
                   share(2 * D_MODEL), share(D_CONV), share(n_o), share(D_MODEL)],
        out_shape=[jax.ShapeDtypeStruct((T_S, n_o), BF16),
                   jax.ShapeDtypeStruct((2 * D_MODEL, D_MODEL), BF16),
                   jax.ShapeDtypeStruct((D_CONV, D_MODEL), BF16),
                   jax.ShapeDtypeStruct((n_o, D_MODEL), BF16),
                   jax.ShapeDtypeStruct((D_MODEL, D_MODEL), BF16)],
        scratch_shapes=[pltpu.VMEM((N_HEADS, m_all, LANE), BF16),
                        pltpu.VMEM((m_all, N_HEADS * LANE), BF16)],
        compiler_params=pltpu.CompilerParams(dimension_semantics=("arbitrary", "arbitrary"),
                                             vmem_limit_bytes=VMEM_LIMIT),
        name="attn_sample",
    )(q, ckv, cache_ckv, krs, cache_krs, w_ukv_slot, w_in_t, w_conv_out, w_o, w_mix_out)
    return o_prompt, o_sample, w_gates_t, w_co_b, w_o_b, w_mix_b


def _route(logits):
    lane = lax.broadcasted_iota(jnp.int32, logits.shape, 1)
    neg = -jnp.inf
    big = jnp.int32(1 << 20)
    gmask = (lane >= N_EXPERTS) & (lane < N_EXPERTS + N_GROUPS)
    gl = jnp.where(gmask, logits, neg)
    gmax = jnp.max(gl, axis=-1, keepdims=True)
    gsum = jnp.sum(jnp.where(gmask, jnp.exp(gl - gmax), 0.0), axis=-1, keepdims=True)
    p_g = 1.0 / gsum
    g_idx = jnp.min(jnp.where(gl == gmax, lane, big), axis=-1, keepdims=True) - N_EXPERTS

    emask = (lane < N_EXPERTS) & ((lane >> 3) == g_idx)
    el = jnp.where(emask, logits, neg)
    m1 = jnp.max(el, axis=-1, keepdims=True)
    i1 = jnp.min(jnp.where(el == m1, lane, big), axis=-1, keepdims=True)
    el2 = jnp.where(lane == i1, neg, el)
    m2 = jnp.max(el2, axis=-1, keepdims=True)
    i2 = jnp.min(jnp.where(el2 == m2, lane, big), axis=-1, keepdims=True)
    z = jnp.sum(jnp.where(emask, jnp.exp(el - m1), 0.0), axis=-1, keepdims=True)
    p1 = 1.0 / z
    p2 = jnp.exp(m2 - m1) / z
    tot = p1 + p2
    w1 = p_g * p1 / tot
    w2 = p_g * p2 / tot
    return jnp.where(lane == i1, w1, 0.0) + jnp.where(lane == i2, w2, 0.0), g_idx


def _post_kernel(xp_ref, xs_ref, mod_ref, n1_ref, wg_ref, zc_ref, op_ref, os_ref, wco_ref, wo_ref,
                 wmix_ref, n2_ref, wr_ref, x1_ref, h3_ref, meta_ref, cnt_ref, gr_ref):
    i = pl.program_id(0)
    is_sample = i >= T_P // TM_POST
    x = jnp.where(is_sample, xs_ref[...], xp_ref[...])
    o = jnp.where(is_sample, os_ref[...], op_ref[...])
    mod = mod_ref[pl.ds(_mod_row(i, TM_POST), 1), :]
    shift1 = mod[:, 0:D_MODEL]
    scale1 = mod[:, D_MODEL:2 * D_MODEL]
    gate1 = mod[:, 2 * D_MODEL:3 * D_MODEL]
    shift2 = mod[:, 3 * D_MODEL:4 * D_MODEL]
    scale2 = mod[:, 4 * D_MODEL:5 * D_MODEL]
    y_conv = _dot(zc_ref[...], wco_ref[...])
    y_mla = _dot(o, wo_ref[...])
    h = ((x * _rms(x)) * n1_ref[...]) * (1.0 + scale1) + shift1
    g = _dot_nt(h.astype(BF16), wg_ref[...])
    merged = (jax.nn.sigmoid(g[:, 0:D_MODEL]) * y_conv
              + jax.nn.sigmoid(g[:, D_MODEL:2 * D_MODEL]) * y_mla)
    y = _dot(merged.astype(BF16), wmix_ref[...])
    x1 = x + gate1 * y
    x1_ref[...] = x1
    h2 = ((x1 * _rms(x1)) * n2_ref[...]) * (1.0 + scale2) + shift2
    h2_hi = h2.astype(BF16)
    h2_lo = (h2 - h2_hi.astype(F32)).astype(BF16)
    hh = _dot(h2_hi, wr_ref[...])
    logits = hh[:, 0:LANE] + hh[:, LANE:2 * LANE] + _dot(h2_lo, wr_ref[:, 0:LANE])
    comb, g_idx = _route(logits)

    lane = lax.broadcasted_iota(jnp.int32, comb.shape, 1)
    onehot = lane == g_idx + N_EXPERTS
    r_i = lax.broadcasted_iota(jnp.int32, (TM_POST, TM_POST), 0)
    c_i = lax.broadcasted_iota(jnp.int32, (TM_POST, TM_POST), 1)
    lower = jnp.where(c_i < r_i, 1.0, 0.0).astype(BF16)
    before = _dot(lower, jnp.where(onehot, 1.0, 0.0).astype(BF16))
    rank = jnp.sum(jnp.where(onehot, before, 0.0), axis=-1, keepdims=True)
    counts = jnp.sum(jnp.where(onehot, 1.0, 0.0), axis=0, keepdims=True)
    cnt_ref[...] = jnp.broadcast_to(counts, cnt_ref.shape)

    meta_ref[...] = comb
    idx = jnp.where(lane == GID_LANE, g_idx.astype(F32), 0.0) + jnp.where(lane == RANK_LANE, rank, 0.0)
    idx_hi = idx.astype(BF16)
    idx_lo = (idx - idx_hi.astype(F32)).astype(BF16)
    s_row = lax.broadcasted_iota(jnp.int32, (8, LANE), 0)
    s_lane = lax.broadcasted_iota(jnp.int32, (8, LANE), 1)
    sel = jnp.where(s_lane == GID_LANE + s_row, 1.0, 0.0).astype(BF16)
    gr_ref[...] = _dot_nt(sel, idx_hi) + _dot_nt(sel, idx_lo)
    for c in range(N_SLAB):
        h3_ref[pl.ds(c, TM_POST, stride=N_SLAB), :] = h2[:, LANE * c:LANE * (c + 1)]


def _post(xp, xs, mod, norm1, w_gates_t, zc, o_p, o_s, w_conv_out, w_o, w_mix_out, norm2, w_route):
    const = lambda i: (0, 0)
    row = lambda i: (i, 0)
    pmap, smap = _stream_maps(TM_POST)
    return pl.pallas_call(
        _post_kernel,
        grid=(T // TM_POST,),
        in_specs=[
            pl.BlockSpec((TM_POST, D_MODEL), pmap),
            pl.BlockSpec((TM_POST, D_MODEL), smap),
            pl.BlockSpec((N_COND, 6 * D_MODEL), const),
            pl.BlockSpec((1, D_MODEL), const),
            pl.BlockSpec((2 * D_MODEL, D_MODEL), const),
            pl.BlockSpec((TM_POST, D_CONV), row),
            pl.BlockSpec((TM_POST, N_HEADS * V_HEAD), pmap),
            pl.BlockSpec((TM_POST, N_HEADS * V_HEAD), smap),
            pl.BlockSpec((D_CONV, D_MODEL), const),
            pl.BlockSpec((N_HEADS * V_HEAD, D_MODEL), const),
            pl.BlockSpec((D_MODEL, D_MODEL), const),
            pl.BlockSpec((1, D_MODEL), const),
            pl.BlockSpec((D_MODEL, 2 * LANE), const),
        ],
        out_specs=[
            pl.BlockSpec((TM_POST, D_MODEL), row),
            pl.BlockSpec((TM_POST * N_SLAB, LANE), row),
            pl.BlockSpec((TM_POST, LANE), row),
            pl.BlockSpec((None, 8, LANE), lambda i: (i, 0, 0)),
            pl.BlockSpec((8, TM_POST), lambda i: (0, i)),
        ],
        out_shape=[
            jax.ShapeDtypeStruct((T, D_MODEL), F32),
            jax.ShapeDtypeStruct((T * N_SLAB, LANE), F32),
            jax.ShapeDtypeStruct((T, LANE), F32),
            jax.ShapeDtypeStruct((T // TM_POST, 8, LANE), F32),
            jax.ShapeDtypeStruct((8, T), F32),
        ],
        compiler_params=pltpu.CompilerParams(
            dimension_semantics=("parallel",), vmem_limit_bytes=VMEM_LIMIT),
        name="post_mixer",
    )(xp, xs, mod, norm1, w_gates_t, zc, o_p, o_s, w_conv_out, w_o, w_mix_out, norm2, w_route)


DISP_IN = T // TM_DISP
DISP_OUT = T // TM_MOE


def _dispatch_kernel(pos_ref, h3_ref, m_ref, hs_ref, ms_ref, xs_ref, mss_ref):
    i = pl.program_id(0)

    @pl.when(i < DISP_IN)
    def _():
        base = i * TM_DISP

        def body(r, carry):
            p = pos_ref[base + r]
            xs_ref[_slab(p), :] = h3_ref[_slab(r), :]
            mss_ref[pl.ds(p, 1), :] = m_ref[pl.ds(r, 1), :]
            return carry

        lax.fori_loop(0, TM_DISP, body, 0, unroll=8)

    @pl.when(i >= DISP_IN)
    def _():
        row0 = pl.multiple_of((i - DISP_IN) * TM_MOE, TM_MOE)
        for c in range(N_SLAB):
            hs_ref[:, LANE * c:LANE * (c + 1)] = (
                xs_ref[pl.ds(row0 * N_SLAB + c, TM_MOE, stride=N_SLAB), :].astype(BF16))
        ms_ref[...] = mss_ref[pl.ds(row0, TM_MOE), :]


def _dispatch(pos, h3, meta):
    n_slab = N_SLAB
    in_map = lambda i, pos: (jnp.minimum(i, DISP_IN - 1), 0)
    out_map = lambda i, pos: (jnp.maximum(i - DISP_IN, 0), 0)
    return pl.pallas_call(
        _dispatch_kernel,
        grid_spec=pltpu.PrefetchScalarGridSpec(
            num_scalar_prefetch=1,
            grid=(DISP_IN + DISP_OUT,),
            in_specs=[pl.BlockSpec((TM_DISP * n_slab, LANE), in_map),
                      pl.BlockSpec((TM_DISP, LANE), in_map)],
            out_specs=[pl.BlockSpec((TM_MOE, D_MODEL), out_map),
                       pl.BlockSpec((TM_MOE, LANE), out_map)],
            scratch_shapes=[pltpu.VMEM((T * n_slab, LANE), F32),
                            pltpu.VMEM((T, LANE), F32)],
        ),
        out_shape=[jax.ShapeDtypeStruct((T, D_MODEL), BF16),
                   jax.ShapeDtypeStruct((T, LANE), F32)],
        compiler_params=pltpu.CompilerParams(
            dimension_semantics=("arbitrary",), vmem_limit_bytes=VMEM_LIMIT),
        name="moe_dispatch",
    )(pos, h3, meta)


def _moe_kernel(vt_ref, vg_ref, vlo_ref, vhi_ref, vfirst_ref, vlast_ref, vvalid_ref,
                hs_ref, ms_ref, wup_ref, wgate_ref, wdown_ref, y3_ref, acc_ref):
    v = pl.program_id(0)
    j = pl.program_id(1)
    valid = vvalid_ref[v] == 1
    lo = vlo_ref[v]
    hi = vhi_ref[v]
    e0 = vg_ref[v] * EXP_PER_GROUP + j * MOE_EPS
    full = (hi - lo) * 4 >= TM_MOE * 3

    @pl.when(valid & (j == 0) & (vfirst_ref[v] == 1))
    def _():
        acc_ref[...] = jnp.zeros_like(acc_ref)

    def expert_rows(r0, rows):
        w_in2 = jnp.concatenate(
            [w[k].astype(BF16) for k in range(MOE_EPS) for w in (wup_ref, wgate_ref)], axis=1)
        ag = _dot(hs_ref[pl.ds(r0, rows), :], w_in2)
        comb = ms_ref[pl.ds(r0, rows), :]
        lane = lax.broadcasted_iota(jnp.int32, comb.shape, 1)
        acts = []
        for k in range(MOE_EPS):
            a = ag[:, 2 * k * D_EXPERT:(2 * k + 1) * D_EXPERT]
            g = ag[:, (2 * k + 1) * D_EXPERT:(2 * k + 2) * D_EXPERT]
            cw = jnp.sum(jnp.where(lane == e0 + k, comb, 0.0), axis=-1, keepdims=True)
            acts.append(((g * jax.nn.sigmoid(g)) * a * cw).astype(BF16))
        w_out = jnp.concatenate([wdown_ref[k].astype(BF16) for k in range(MOE_EPS)], axis=0)
        acc_ref[pl.ds(r0, rows), :] += _dot(jnp.concatenate(acts, axis=1), w_out)

    @pl.when(valid & full)
    def _():
        expert_rows(0, TM_MOE)

    @pl.when(valid & jnp.logical_not(full))
    def _():
        def sub_block(s, carry):
            expert_rows(pl.multiple_of(s * MOE_SUB, MOE_SUB), MOE_SUB)
            return carry

        lax.fori_loop(lo // MOE_SUB, (hi + MOE_SUB - 1) // MOE_SUB, sub_block, 0)

    @pl.when(valid & (j == EXP_PER_GROUP // MOE_EPS - 1) & (vlast_ref[v] == 1))
    def _():
        for c in range(N_SLAB):
            y3_ref[pl.ds(c, TM_MOE, stride=N_SLAB), :] = acc_ref[:, LANE * c:LANE * (c + 1)]


def _moe(sched, hs, ms, w_up, w_gate, w_down):
    steps = EXP_PER_GROUP // MOE_EPS
    wmap = lambda v, j, vt, vg, vlo, vhi, vfirst, vlast, vvalid: (
        vg[v] * steps + jnp.where(vvalid[v] == 1, j, steps - 1), 0, 0)
    tmap = lambda v, j, vt, *_: (vt[v], 0)
    n_slab = D_MODEL // LANE
    return pl.pallas_call(
        _moe_kernel,
        grid_spec=pltpu.PrefetchScalarGridSpec(
            num_scalar_prefetch=7,
            grid=(N_VISITS, steps),
            in_specs=[
                pl.BlockSpec((TM_MOE, D_MODEL), tmap),
                pl.BlockSpec((TM_MOE, LANE), tmap),
                pl.BlockSpec((MOE_EPS, D_MODEL, D_EXPERT), wmap),
                pl.BlockSpec((MOE_EPS, D_MODEL, D_EXPERT), wmap),
                pl.BlockSpec((MOE_EPS, D_EXPERT, D_MODEL), wmap),
            ],
            out_specs=pl.BlockSpec((TM_MOE * n_slab, LANE), tmap),
            scratch_shapes=[pltpu.VMEM((TM_MOE, D_MODEL), F32)],
        ),
        out_shape=jax.ShapeDtypeStruct((T * n_slab, LANE), F32),
        compiler_params=pltpu.CompilerParams(
            dimension_semantics=("arbitrary", "arbitrary"), vmem_limit_bytes=VMEM_LIMIT),
        name="moe_grouped",
    )(*sched, hs, ms, w_up, w_gate, w_down)


def _final_kernel(pos_ref, ys_ref, x1_ref, mod_ref, fn_ref, yp_ref, ysm_ref, g_ref):
    i = pl.program_id(0)
    base = i * TM_FINAL

    def body(r, carry):
        g_ref[_slab(r), :] = ys_ref[_slab(pos_ref[base + r]), :]
        return carry

    lax.fori_loop(0, TM_FINAL, body, 0, unroll=8)
    mod = mod_ref[pl.ds(_mod_row(i, TM_FINAL), 1), :]
    gate2 = mod[:, 5 * D_MODEL:6 * D_MODEL]
    moe = jnp.concatenate([g_ref[pl.ds(c, TM_FINAL, stride=N_SLAB), :] for c in range(N_SLAB)], axis=1)
    x2 = x1_ref[...] + gate2 * moe
    y = (x2 * _rms(x2)) * fn_ref[...]
    is_sample = i >= T_P // TM_FINAL

    @pl.when(jnp.logical_not(is_sample))
    def _():
        yp_ref[...] = y

    @pl.when(is_sample)
    def _():
        ysm_ref[...] = y


def _final(pos, ys, x1, mod, final_norm):
    n_slab = D_MODEL // LANE
    pmap, smap = _stream_maps(TM_FINAL)
    return pl.pallas_call(
        _final_kernel,
        grid_spec=pltpu.PrefetchScalarGridSpec(
            num_scalar_prefetch=1,
            grid=(T // TM_FINAL,),
            in_specs=[
                pl.BlockSpec((T * n_slab, LANE), lambda i, pos: (0, 0), pipeline_mode=pl.Buffered(1)),
                pl.BlockSpec((TM_FINAL, D_MODEL), lambda i, pos: (i, 0)),
                pl.BlockSpec((N_COND, 6 * D_MODEL), lambda i, pos: (0, 0)),
                pl.BlockSpec((1, D_MODEL), lambda i, pos: (0, 0)),
            ],
            out_specs=[pl.BlockSpec((TM_FINAL, D_MODEL), pmap),
                       pl.BlockSpec((TM_FINAL, D_MODEL), smap)],
            scratch_shapes=[pltpu.VMEM((TM_FINAL * n_slab, LANE), F32)],
        ),
        out_shape=[jax.ShapeDtypeStruct((T_P, D_MODEL), F32),
                   jax.ShapeDtypeStruct((T_S, D_MODEL), F32)],
        compiler_params=pltpu.CompilerParams(
            dimension_semantics=("arbitrary",), vmem_limit_bytes=VMEM_LIMIT),
        name="moe_unsort_final",
    )(pos, ys, x1, mod, final_norm)


def _schedule(gr, cnt):
    n_tiles_post = T // TM_POST
    counts = cnt[:, 0, N_EXPERTS:N_EXPERTS + N_GROUPS].astype(jnp.int32)
    gtot = jnp.sum(counts, axis=0)
    goff = jnp.cumsum(gtot) - gtot
    tile_base = goff[None, :] + jnp.cumsum(counts, axis=0) - counts
    gid = gr[0].astype(jnp.int32).reshape(n_tiles_post, TM_POST)
    rank = gr[1].astype(jnp.int32).reshape(n_tiles_post, TM_POST)
    pos = rank
    for grp in range(N_GROUPS):
        pos = pos + jnp.where(gid == grp, tile_base[:, grp:grp + 1], 0)
    pos = pos.reshape(T)

    n_tiles = T // TM_MOE
    t_lo = (jnp.arange(n_tiles, dtype=jnp.int32) * TM_MOE)[:, None]
    lo = jnp.clip(goff[None, :] - t_lo, 0, TM_MOE)
    hi = jnp.clip(goff[None, :] + gtot[None, :] - t_lo, 0, TM_MOE)
    ok = (hi > lo).reshape(-1)
    slot = jnp.cumsum(ok.astype(jnp.int32)) - 1
    n_ok = slot[-1] + 1
    sel = (slot[None, :] == jnp.arange(N_VISITS, dtype=jnp.int32)[:, None]) & ok[None, :]

    def pick(vals):
        return jnp.sum(jnp.where(sel, vals.reshape(-1)[None, :], 0), axis=1).astype(jnp.int32)

    pair_tile = jnp.broadcast_to(jnp.arange(n_tiles, dtype=jnp.int32)[:, None], (n_tiles, N_GROUPS))
    pair_group = jnp.broadcast_to(jnp.arange(N_GROUPS, dtype=jnp.int32)[None, :], (n_tiles, N_GROUPS))
    vt, vg, vlo, vhi = pick(pair_tile), pick(pair_group), pick(lo), pick(hi)
    valid = jnp.arange(N_VISITS, dtype=jnp.int32) < n_ok
    last = jnp.maximum(n_ok - 1, 0)
    vt = jnp.where(valid, vt, vt[last])
    vg = jnp.where(valid, vg, vg[last])
    change = (vt[1:] != vt[:-1]).astype(jnp.int32)
    one = jnp.ones((1,), jnp.int32)
    first = jnp.concatenate([one, change])
    idx = jnp.arange(N_VISITS, dtype=jnp.int32)
    last = jnp.where(idx == n_ok - 1, 1, jnp.concatenate([change, one]))
    return pos, (vt, vg, vlo, vhi, first, last, valid.astype(jnp.int32))


def _rope_tables():
    n = np.arange(DEC_SEQ)
    pos = np.stack([n // GRID_W, n % GRID_W], axis=1).astype(np.float32)
    half = ROPE_AXIS // 2
    inv = (1.0 / (ROPE_BASE ** (np.arange(0, ROPE_AXIS, 2, dtype=np.float32) / ROPE_AXIS))).astype(np.float32)
    ang = (pos[:, :, None] * inv[None, None, :]).astype(np.float32)
    cos = np.cos(ang).astype(np.float32)
    sin = np.sin(ang).astype(np.float32)
    tabs = np.zeros((2, 3, DEC_SEQ, LANE), np.float32)
    tabs[:, 0] = 1.0
    for a in range(2):
        lo = ROPE_LANE0 + a * ROPE_AXIS
        tabs[1, 0, :, lo:lo + half] = cos[:, a]
        tabs[1, 0, :, lo + half:lo + 2 * half] = cos[:, a]
        tabs[1, 1, :, lo + half:lo + 2 * half] = sin[:, a]
        tabs[1, 2, :, lo:lo + half] = -sin[:, a]
    return jnp.asarray(tabs)


def kernel(x_prompt, x_sample, cache_ckv, cache_krope, c, c_ctx, norm1, w_ada, b_ada, w_in, conv_w,
           w_conv_out, q_norm, w_uq, kv_norm, w_ukv, w_o, w_mix_out, norm2, w_grp, w_exp, w_up,
           w_gate, w_down, final_norm):
    l = 0
    xp = x_prompt.reshape(T_P, D_MODEL)
    xs = x_sample.reshape(T_S, D_MODEL)
    cond = jnp.concatenate(
        [c_ctx[None, :], c, jnp.zeros((N_COND - 1 - DEC_BATCH, D_MODEL), F32)], axis=0)
    mod = _ada(cond, w_ada[l], b_ada[l][None, :])

    w_in_t = w_in[l].T
    w_conv3_t, w_small_t = _prep_weights(w_in_t)
    w_uq_slot = jnp.pad(w_uq[l].reshape(Q_LORA, N_HEADS, QK_NOPE + QK_ROPE),
                        ((0, 0), (0, 0), (0, LANE - QK_NOPE - QK_ROPE))
                        ).reshape(Q_LORA, N_HEADS * LANE).astype(BF16)
    wkv = w_ukv[l].reshape(KV_LORA, N_HEADS, QK_NOPE + V_HEAD)
    wk_slot = jnp.pad(wkv[:, :, :QK_NOPE], ((0, 0), (0, 0), (0, LANE - QK_NOPE)))
    wv = wkv[:, :, QK_NOPE:].reshape(KV_LORA, N_HEADS // 2, 2, V_HEAD)
    zero = jnp.zeros_like(wv[:, :, 0])
    wv_slot = jnp.stack([jnp.concatenate([wv[:, :, 0], zero], axis=-1),
                         jnp.concatenate([zero, wv[:, :, 1]], axis=-1)], axis=2)
    w_ukv_slot = jnp.concatenate([wk_slot.reshape(KV_LORA, N_HEADS * LANE),
                                  wv_slot.reshape(KV_LORA, N_HEADS * LANE)], axis=1).astype(BF16)
    w_route = jnp.pad(jnp.concatenate([w_exp[l], w_grp[l]], axis=1),
                      ((0, 0), (0, LANE - N_EXPERTS - N_GROUPS)))
    w_route_hi = w_route.astype(BF16)
    w_route_lo = (w_route - w_route_hi.astype(F32)).astype(BF16)
    w_route2 = jnp.concatenate([w_route_hi, w_route_lo], axis=1)
    cache_krs = jnp.pad(cache_krope[:, l], ((0, 0), (0, 0), (ROPE_LANE0, LANE - ROPE_LANE0 - QK_ROPE)))

    zc, q, ckv, krs, nckv, nkr = _inproj(xp, xs, mod, norm1[l][None, :], w_conv3_t, w_small_t, conv_w[l],
                              q_norm[l][None, :], kv_norm[l][None, :], w_uq_slot, _rope_tables())
    o_p, o_s, w_gates_t, w_co_b, w_o_b, w_mix_b = _attention(
        q, ckv, krs, cache_ckv[:, l], cache_krs, w_ukv_slot, w_in_t, w_conv_out[l], w_o[l], w_mix_out[l])
    x1, h3, meta, cnt, gr = _post(xp, xs, mod, norm1[l][None, :], w_gates_t, zc, o_p, o_s, w_co_b, w_o_b,
                                  w_mix_b, norm2[l][None, :], w_route2)
    pos, sched = _schedule(gr, cnt)
    hs, ms = _dispatch(pos, h3, meta)
    ys = _moe(sched, hs, ms, w_up[l], w_gate[l], w_down[l])
    yp, ysm = _final(pos, ys, x1, mod, final_norm[None, :])

    y_prompt = yp.reshape(BATCH, SEQ, D_MODEL)
    y_sample = ysm.reshape(DEC_BATCH, DEC_SEQ, D_MODEL)
    new_ckv = nckv.reshape(BATCH, 1, SEQ, KV_LORA)
    new_krope = nkr.reshape(BATCH, 1, SEQ, QK_ROPE)
    return (y_prompt, y_sample, new_ckv, new_krope)
```

```python
import functools

import numpy as np
import jax
import jax.numpy as jnp
from jax import lax
from jax.experimental import pallas as pl
from jax.experimental.pallas import tpu as pltpu

F32 = jnp.float32
BF16 = jnp.bfloat16

D_MODEL = 1024
BATCH = 16
SEQ = 256
DEC_BATCH = 2
DEC_SEQ = 1024
PAST_LEN = 256
GRID_W = 64
N_HEADS = 8
QK_NOPE = 64
QK_ROPE = 32
V_HEAD = 64
Q_LORA = 256
KV_LORA = 128
ROPE_AXIS = QK_ROPE // 2
ROPE_BASE = 10000.0
ATTN_SCALE = (QK_NOPE + QK_ROPE) ** -0.5
D_CONV = D_MODEL
N_GROUPS = 4
EXP_PER_GROUP = 8
N_EXPERTS = N_GROUPS * EXP_PER_GROUP
D_EXPERT = 256
EPS = 1e-6

T_P = BATCH * SEQ
T_S = DEC_BATCH * DEC_SEQ
T = T_P + T_S
N_COND = 8
LANE = 128
ROPE_LANE0 = QK_NOPE
SMALL_COLS = Q_LORA + KV_LORA + LANE
VMEM_LIMIT = 56 * 1024 * 1024

TM_IN = 1024
TM_POST = 512
TM_MOE = T // N_GROUPS
TM_FINAL = 512
TM_DISP = 1024
N_SLAB = D_MODEL // LANE
MOE_SUB = 256
MOE_EPS = 2
N_VISITS = T // TM_MOE + N_GROUPS - 1
GID_LANE = 40
RANK_LANE = 41
Q_BLK = 256
Q_BLK_S = 512
PROMPT_SEQS = 2
CONV_CHUNK = 256


def _dot(a, b):
    return jnp.dot(a, b, preferred_element_type=F32)


def _rms(x):
    return lax.rsqrt(jnp.mean(x * x, axis=-1, keepdims=True) + EPS)


def _slab(t):
    return pl.ds(pl.multiple_of(t * N_SLAB, N_SLAB), N_SLAB)


def _mod_row(i, tm):
    n_prompt = T_P // tm
    return jnp.where(i >= n_prompt, 1 + ((i - n_prompt) * tm) // DEC_SEQ, 0)


def _ada_kernel(cond_ref, w_ref, b_ref, o_ref):
    c = cond_ref[...]
    a = (c * jax.nn.sigmoid(c)).astype(BF16)
    o_ref[...] = _dot(a, w_ref[...].astype(BF16)) + b_ref[...]


def _ada(cond, w_ada, b_ada):
    n = 6 * D_MODEL
    bn = 1536
    return pl.pallas_call(
        _ada_kernel,
        grid=(n // bn,),
        in_specs=[
            pl.BlockSpec((N_COND, D_MODEL), lambda j: (0, 0)),
            pl.BlockSpec((D_MODEL, bn), lambda j: (0, j)),
            pl.BlockSpec((1, bn), lambda j: (0, j)),
        ],
        out_specs=pl.BlockSpec((N_COND, bn), lambda j: (0, j)),
        out_shape=jax.ShapeDtypeStruct((N_COND, n), F32),
        compiler_params=pltpu.CompilerParams(
            dimension_semantics=("parallel",), vmem_limit_bytes=VMEM_LIMIT),
        name="ada_mod",
    )(cond, w_ada, b_ada)


O_CQ = 3 * D_CONV
O_KR = O_CQ + Q_LORA + KV_LORA
O_GATE = O_KR + QK_ROPE
PREP_STEPS = 4


IN_COLS = O_GATE + 2 * D_MODEL
TAIL_ROWS = IN_COLS // 2
SMALL_BLK = 512
NT = (((1,), (1,)), ((), ()))


def _dot_nt(a, bt):
    return lax.dot_general(a, bt, NT, preferred_element_type=F32)


def _stream_maps(tm):
    n_prompt = T_P // tm
    return (lambda i, *_: (jnp.minimum(i, n_prompt - 1), 0),
            lambda i, *_: (jnp.maximum(i - n_prompt, 0), 0))


def _inproj_kernel(xp_ref, xs_ref, mod_ref, n1_ref, wc_ref, ws_ref, cw_ref, qn_ref, kvn_ref, wuq_ref,
                   rope_ref, zc_ref, q_ref, ckv_ref, krs_ref, nckv_ref, nkr_ref):
    i = pl.program_id(0)
    is_sample = i >= T_P // TM_IN
    seq = jnp.where(is_sample, DEC_SEQ, SEQ)
    mod = mod_ref[pl.ds(_mod_row(i, TM_IN), 1), :]
    shift1 = mod[:, 0:D_MODEL]
    scale1 = mod[:, D_MODEL:2 * D_MODEL]
    x = jnp.where(is_sample, xs_ref[...], xp_ref[...])
    h = ((x * _rms(x)) * n1_ref[...]) * (1.0 + scale1) + shift1
    hb = h.astype(BF16)

    n_lat = Q_LORA + KV_LORA
    w_small = jnp.concatenate([
        ws_ref[0:n_lat, :].astype(BF16),
        jnp.zeros((ROPE_LANE0, D_MODEL), BF16),
        ws_ref[n_lat:n_lat + QK_ROPE, :].astype(BF16),
        jnp.zeros((LANE - ROPE_LANE0 - QK_ROPE, D_MODEL), BF16)], axis=0)
    sm = _dot_nt(hb, w_small)
    cq = sm[:, 0:Q_LORA]
    ckv_raw = sm[:, Q_LORA:Q_LORA + KV_LORA]
    krs = sm[:, Q_LORA + KV_LORA:SMALL_COLS]
    cqn = (cq * _rms(cq)) * qn_ref[...]
    q = _dot(cqn.astype(BF16), wuq_ref[...])
    ckv = (ckv_raw * _rms(ckv_raw)) * kvn_ref[...]
    ckv_ref[...] = ckv

    @pl.when(jnp.logical_not(is_sample))
    def _():
        nckv_ref[...] = ckv
        nkr_ref[...] = krs[:, ROPE_LANE0:ROPE_LANE0 + QK_ROPE]

    cos = rope_ref[0]
    sin_lo = rope_ref[1]
    sin_hi = rope_ref[2]

    def rot(v):
        return v * cos + pltpu.roll(v, 8, 1) * sin_lo + pltpu.roll(v, LANE - 8, 1) * sin_hi

    krs_ref[...] = rot(krs)
    for hh in range(N_HEADS):
        q_ref[:, LANE * hh:LANE * (hh + 1)] = rot(q[:, LANE * hh:LANE * (hh + 1)]).astype(BF16)

    pos = lax.broadcasted_iota(jnp.int32, (TM_IN, 1), 0) & (seq - 1)
    first = pos == 0
    last = pos == seq - 1
    for j in range(D_CONV // CONV_CHUNK):
        c0 = j * CONV_CHUNK
        bg = _dot_nt(hb, wc_ref[c0:c0 + CONV_CHUNK, :].astype(BF16))
        cg = _dot_nt(hb, wc_ref[D_CONV + c0:D_CONV + c0 + CONV_CHUNK, :].astype(BF16))
        ui = _dot_nt(hb, wc_ref[2 * D_CONV + c0:2 * D_CONV + c0 + CONV_CHUNK, :].astype(BF16))
        u = cg * ui
        u_prev = jnp.where(first, 0.0, pltpu.roll(u, 1, 0))
        u_next = jnp.where(last, 0.0, pltpu.roll(u, TM_IN - 1, 0))
        cw = cw_ref[:, c0:c0 + CONV_CHUNK]
        conv = u_prev * cw[0:1] + u * cw[1:2] + u_next * cw[2:3]
        zc_ref[:, c0:c0 + CONV_CHUNK] = (bg * conv).astype(BF16)


def _inproj(xp, xs, mod, norm1, w_in_t, conv_w, q_norm, kv_norm, w_uq_slot, rope_tabs):
    n_prompt = T_P // TM_IN
    const = lambda i: (0, 0)
    pmap, smap = _stream_maps(TM_IN)
    once = pl.Buffered(1)
    return pl.pallas_call(
        _inproj_kernel,
        grid=(T // TM_IN,),
        in_specs=[
            pl.BlockSpec((TM_IN, D_MODEL), pmap),
            pl.BlockSpec((TM_IN, D_MODEL), smap),
            pl.BlockSpec((N_COND, 6 * D_MODEL), const),
            pl.BlockSpec((1, D_MODEL), const),
            pl.BlockSpec((O_CQ, D_MODEL), const, pipeline_mode=once),
            pl.BlockSpec((SMALL_BLK, D_MODEL), lambda i: (O_CQ // SMALL_BLK, 0), pipeline_mode=once),
            pl.BlockSpec((3, D_CONV), const),
            pl.BlockSpec((1, Q_LORA), const),
            pl.BlockSpec((1, KV_LORA), const),
            pl.BlockSpec((Q_LORA, N_HEADS * LANE), const),
            pl.BlockSpec((None, 3, TM_IN, LANE),
                         lambda i: (jnp.where(i >= n_prompt, 1, 0), 0, 0, 0)),
        ],
        out_specs=[
            pl.BlockSpec((TM_IN, D_CONV), lambda i: (i, 0)),
            pl.BlockSpec((TM_IN, N_HEADS * LANE), lambda i: (i, 0)),
            pl.BlockSpec((TM_IN, KV_LORA), lambda i: (i, 0)),
            pl.BlockSpec((TM_IN, LANE), lambda i: (i, 0)),
            pl.BlockSpec((TM_IN, KV_LORA), pmap),
            pl.BlockSpec((TM_IN, QK_ROPE), pmap),
        ],
        out_shape=[
            jax.ShapeDtypeStruct((T, D_CONV), BF16),
            jax.ShapeDtypeStruct((T, N_HEADS * LANE), BF16),
            jax.ShapeDtypeStruct((T, KV_LORA), F32),
            jax.ShapeDtypeStruct((T, LANE), F32),
            jax.ShapeDtypeStruct((T_P, KV_LORA), F32),
            jax.ShapeDtypeStruct((T_P, QK_ROPE), F32),
        ],
        compiler_params=pltpu.CompilerParams(
            dimension_semantics=("arbitrary",), vmem_limit_bytes=VMEM_LIMIT),
        name="in_proj",
    )(xp, xs, mod, norm1, w_in_t, w_in_t, conv_w, q_norm, kv_norm, w_uq_slot, rope_tabs)


def _fill_kv(ckv, krs, wukv_ref, kf_scr, v_scr, off):
    m = ckv.shape[0]
    kv = _dot(ckv.astype(BF16), wukv_ref[...])
    for hh in range(N_HEADS):
        kf_scr[hh, off:off + m, :] = (kv[:, LANE * hh:LANE * (hh + 1)] + krs).astype(BF16)
    v_scr[off:off + m, :] = kv[:, N_HEADS * LANE:].astype(BF16)


def _attend(q_ref, r0, rows, kf_scr, v_scr, o_ref):
    for pair in range(N_HEADS // 2):
        acc = None
        for hh in (2 * pair, 2 * pair + 1):
            qh = q_ref[r0:r0 + rows, LANE * hh:LANE * (hh + 1)]
            s = _dot_nt(qh, kf_scr[hh]) * ATTN_SCALE
            e = jnp.exp(s - jnp.max(s, axis=-1, keepdims=True))
            p = (e / jnp.sum(e, axis=-1, keepdims=True)).astype(BF16)
            part = _dot(p, v_scr[:, LANE * hh:LANE * (hh + 1)])
            acc = part if acc is None else acc + part
        o_ref[r0:r0 + rows, LANE * pair:LANE * (pair + 1)] = acc.astype(BF16)


def _attn_prompt_kernel(q_ref, ckv_ref, krs_ref, wukv_ref, o_ref, kf_scr, v_scr):
    for s in range(PROMPT_SEQS):
        r0 = s * SEQ
        _fill_kv(ckv_ref[r0:r0 + SEQ, :], krs_ref[r0:r0 + SEQ, :], wukv_ref, kf_scr, v_scr, 0)
        _attend(q_ref, r0, SEQ, kf_scr, v_scr, o_ref)


def _attn_sample_kernel(q_ref, ckv_ref, cckv_ref, krs_ref, ckrs_ref, wukv_ref, wg_ref, wco_ref, wo_ref,
                        wmix_ref, o_ref, g_ref, co_ref, ob_ref, mix_ref, kf_scr, v_scr):
    @pl.when(pl.program_id(1) == 0)
    def _():
        _fill_kv(ckv_ref[...], krs_ref[...], wukv_ref, kf_scr, v_scr, 0)
        _fill_kv(cckv_ref[...], ckrs_ref[...], wukv_ref, kf_scr, v_scr, DEC_SEQ)

    _attend(q_ref, 0, Q_BLK_S, kf_scr, v_scr, o_ref)
    g_ref[...] = wg_ref[...].astype(BF16)
    co_ref[...] = wco_ref[...].astype(BF16)
    ob_ref[...] = wo_ref[...].astype(BF16)
    mix_ref[...] = wmix_ref[...].astype(BF16)


def _attention(q, ckv, krs, cache_ckv, cache_krs, w_ukv_slot, w_in_t, w_conv_out, w_o, w_mix_out):
    kv_cols = 2 * N_HEADS * LANE
    n_o = N_HEADS * V_HEAD
    steps = BATCH // PROMPT_SEQS
    rows = PROMPT_SEQS * SEQ
    o_prompt = pl.pallas_call(
        _attn_prompt_kernel,
        grid=(steps,),
        in_specs=[
            pl.BlockSpec((rows, N_HEADS * LANE), lambda b: (b, 0)),
            pl.BlockSpec((rows, KV_LORA), lambda b: (b, 0)),
            pl.BlockSpec((rows, LANE), lambda b: (b, 0)),
            pl.BlockSpec((KV_LORA, kv_cols), lambda b: (0, 0)),
        ],
        out_specs=pl.BlockSpec((rows, n_o), lambda b: (b, 0)),
        out_shape=jax.ShapeDtypeStruct((T_P, n_o), BF16),
        scratch_shapes=[pltpu.VMEM((N_HEADS, SEQ, LANE), BF16),
                        pltpu.VMEM((SEQ, N_HEADS * LANE), BF16)],
        compiler_params=pltpu.CompilerParams(dimension_semantics=("parallel",),
                                             vmem_limit_bytes=VMEM_LIMIT),
        name="attn_prompt",
    )(q, ckv, krs, w_ukv_slot)

    m_all = DEC_SEQ + PAST_LEN
    nq = DEC_SEQ // Q_BLK_S
    q0 = T_P // Q_BLK_S
    s0 = T_P // DEC_SEQ
    n_step = DEC_BATCH * nq
    share = lambda n: pl.BlockSpec((n // n_step, D_MODEL), lambda b, j: (b * nq + j, 0))
    gate_rows = 2 * D_MODEL // n_step
    o_sample, w_gates_t, w_co_b, w_o_b, w_mix_b = pl.pallas_call(
        _attn_sample_kernel,
        grid=(DEC_BATCH, nq),
        in_specs=[
            pl.BlockSpec((Q_BLK_S, N_HEADS * LANE), lambda b, j: (q0 + b * nq + j, 0)),
            pl.BlockSpec((DEC_SEQ, KV_LORA), lambda b, j: (s0 + b, 0)),
            pl.BlockSpec((None, PAST_LEN, KV_LORA), lambda b, j: (b, 0, 0)),
            pl.BlockSpec((DEC_SEQ, LANE), lambda b, j: (s0 + b, 0)),
            pl.BlockSpec((None, PAST_LEN, LANE), lambda b, j: (b, 0, 0)),
            pl.BlockSpec((KV_LORA, kv_cols), lambda b, j: (0, 0)),
            pl.BlockSpec((pl.Element(gate_rows), pl.Element(D_MODEL)),
                         lambda b, j: (pl.multiple_of(O_GATE + (b * nq + j) * gate_rows, 32), 0)),
            share(D_CONV), share(n_o), share(D_MODEL),
        ],
        out_specs=[pl.BlockSpec((Q_BLK_S, n_o), lambda b, j: (b * nq + j, 0)),
                   share(2 * D_MODEL), share(D_CONV), share(n_o), share(D_MODEL)],
        out_shape=[jax.ShapeDtypeStruct((T_S, n_o), BF16),
                   jax.ShapeDtypeStruct((2 * D_MODEL, D_MODEL), BF16),
                   jax.ShapeDtypeStruct((D_CONV, D_MODEL), BF16),
                   jax.ShapeDtypeStruct((n_o, D_MODEL), BF16),
                   jax.ShapeDtypeStruct((D_MODEL, D_MODEL), BF16)],
        scratch_shapes=[pltpu.VMEM((N_HEADS, m_all, LANE), BF16),
                        pltpu.VMEM((m_all, N_HEADS * LANE), BF16)],
        compiler_params=pltpu.CompilerParams(dimension_semantics=("arbitrary", "arbitrary"),
                                             vmem_limit_bytes=VMEM_LIMIT),
        name="attn_sample",
    )(q, ckv, cache_ckv, krs, cache_krs, w_ukv_slot, w_in_t, w_conv_out, w_o, w_mix_out)
    return o_prompt, o_sample, w_gates_t, w_co_b, w_o_b, w_mix_b


def _route(logits):
    lane = lax.broadcasted_iota(jnp.int32, logits.shape, 1)
    neg = -jnp.inf
    big = jnp.int32(1 << 20)
    gmask = (lane >= N_EXPERTS) & (lane < N_EXPERTS + N_GROUPS)
    gl = jnp.where(gmask, logits, neg)
    gmax = jnp.max(gl, axis=-1, keepdims=True)
    gsum = jnp.sum(jnp.where(gmask, jnp.exp(gl - gmax), 0.0), axis=-1, keepdims=True)
    p_g = 1.0 / gsum
    g_idx = jnp.min(jnp.where(gl == gmax, lane, big), axis=-1, keepdims=True) - N_EXPERTS

    emask = (lane < N_EXPERTS) & ((lane >> 3) == g_idx)
    el = jnp.where(emask, logits, neg)
    m1 = jnp.max(el, axis=-1, keepdims=True)
    i1 = jnp.min(jnp.where(el == m1, lane, big), axis=-1, keepdims=True)
    el2 = jnp.where(lane == i1, neg, el)
    m2 = jnp.max(el2, axis=-1, keepdims=True)
    i2 = jnp.min(jnp.where(el2 == m2, lane, big), axis=-1, keepdims=True)
    z = jnp.sum(jnp.where(emask, jnp.exp(el - m1), 0.0), axis=-1, keepdims=True)
    p1 = 1.0 / z
    p2 = jnp.exp(m2 - m1) / z
    tot = p1 + p2
    w1 = p_g * p1 / tot
    w2 = p_g * p2 / tot
    return jnp.where(lane == i1, w1, 0.0) + jnp.where(lane == i2, w2, 0.0), g_idx


def _post_kernel(xp_ref, xs_ref, mod_ref, n1_ref, wg_ref, zc_ref, op_ref, os_ref, wco_ref, wo_ref,
                 wmix_ref, n2_ref, wr_ref, x1_ref, h3_ref, meta_ref, cnt_ref, gr_ref):
    i = pl.program_id(0)
    is_sample = i >= T_P // TM_POST
    x = jnp.where(is_sample, xs_ref[...], xp_ref[...])
    o = jnp.where(is_sample, os_ref[...], op_ref[...])
    mod = mod_ref[pl.ds(_mod_row(i, TM_POST), 1), :]
    shift1 = mod[:, 0:D_MODEL]
    scale1 = mod[:, D_MODEL:2 * D_MODEL]
    gate1 = mod[:, 2 * D_MODEL:3 * D_MODEL]
    shift2 = mod[:, 3 * D_MODEL:4 * D_MODEL]
    scale2 = mod[:, 4 * D_MODEL:5 * D_MODEL]
    y_conv = _dot(zc_ref[...], wco_ref[...])
    y_mla = _dot(o, wo_ref[...])
    h = ((x * _rms(x)) * n1_ref[...]) * (1.0 + scale1) + shift1
    g = _dot_nt(h.astype(BF16), wg_ref[...])
    merged = (jax.nn.sigmoid(g[:, 0:D_MODEL]) * y_conv
              + jax.nn.sigmoid(g[:, D_MODEL:2 * D_MODEL]) * y_mla)
    y = _dot(merged.astype(BF16), wmix_ref[...])
    x1 = x + gate1 * y
    x1_ref[...] = x1
    h2 = ((x1 * _rms(x1)) * n2_ref[...]) * (1.0 + scale2) + shift2
    h2_hi = h2.astype(BF16)
    h2_lo = (h2 - h2_hi.astype(F32)).astype(BF16)
    hh = _dot(h2_hi, wr_ref[...])
    logits = hh[:, 0:LANE] + hh[:, LANE:2 * LANE] + _dot(h2_lo, wr_ref[:, 0:LANE])
    comb, g_idx = _route(logits)

    lane = lax.broadcasted_iota(jnp.int32, comb.shape, 1)
    onehot = lane == g_idx + N_EXPERTS
    r_i = lax.broadcasted_iota(jnp.int32, (TM_POST, TM_POST), 0)
    c_i = lax.broadcasted_iota(jnp.int32, (TM_POST, TM_POST), 1)
    lower = jnp.where(c_i < r_i, 1.0, 0.0).astype(BF16)
    before = _dot(lower, jnp.where(onehot, 1.0, 0.0).astype(BF16))
    rank = jnp.sum(jnp.where(onehot, before, 0.0), axis=-1, keepdims=True)
    counts = jnp.sum(jnp.where(onehot, 1.0, 0.0), axis=0, keepdims=True)
    cnt_ref[...] = jnp.broadcast_to(counts, cnt_ref.shape)

    meta_ref[...] = comb
    idx = jnp.where(lane == GID_LANE, g_idx.astype(F32), 0.0) + jnp.where(lane == RANK_LANE, rank, 0.0)
    idx_hi = idx.astype(BF16)
    idx_lo = (idx - idx_hi.astype(F32)).astype(BF16)
    s_row = lax.broadcasted_iota(jnp.int32, (8, LANE), 0)
    s_lane = lax.broadcasted_iota(jnp.int32, (8, LANE), 1)
    sel = jnp.where(s_lane == GID_LANE + s_row, 1.0, 0.0).astype(BF16)
    gr_ref[...] = _dot_nt(sel, idx_hi) + _dot_nt(sel, idx_lo)
    for c in range(N_SLAB):
        h3_ref[pl.ds(c, TM_POST, stride=N_SLAB), :] = h2[:, LANE * c:LANE * (c + 1)]


def _post(xp, xs, mod, norm1, w_gates_t, zc, o_p, o_s, w_conv_out, w_o, w_mix_out, norm2, w_route):
    const = lambda i: (0, 0)
    row = lambda i: (i, 0)
    pmap, smap = _stream_maps(TM_POST)
    return pl.pallas_call(
        _post_kernel,
        grid=(T // TM_POST,),
        in_specs=[
            pl.BlockSpec((TM_POST, D_MODEL), pmap),
            pl.BlockSpec((TM_POST, D_MODEL), smap),
            pl.BlockSpec((N_COND, 6 * D_MODEL), const),
            pl.BlockSpec((1, D_MODEL), const),
            pl.BlockSpec((2 * D_MODEL, D_MODEL), const),
            pl.BlockSpec((TM_POST, D_CONV), row),
            pl.BlockSpec((TM_POST, N_HEADS * V_HEAD), pmap),
            pl.BlockSpec((TM_POST, N_HEADS * V_HEAD), smap),
            pl.BlockSpec((D_CONV, D_MODEL), const),
            pl.BlockSpec((N_HEADS * V_HEAD, D_MODEL), const),
            pl.BlockSpec((D_MODEL, D_MODEL), const),
            pl.BlockSpec((1, D_MODEL), const),
            pl.BlockSpec((D_MODEL, 2 * LANE), const),
        ],
        out_specs=[
            pl.BlockSpec((TM_POST, D_MODEL), row),
            pl.BlockSpec((TM_POST * N_SLAB, LANE), row),
            pl.BlockSpec((TM_POST, LANE), row),
            pl.BlockSpec((None, 8, LANE), lambda i: (i, 0, 0)),
            pl.BlockSpec((8, TM_POST), lambda i: (0, i)),
        ],
        out_shape=[
            jax.ShapeDtypeStruct((T, D_MODEL), F32),
            jax.ShapeDtypeStruct((T * N_SLAB, LANE), F32),
            jax.ShapeDtypeStruct((T, LANE), F32),
            jax.ShapeDtypeStruct((T // TM_POST, 8, LANE), F32),
            jax.ShapeDtypeStruct((8, T), F32),
        ],
        compiler_params=pltpu.CompilerParams(
            dimension_semantics=("parallel",), vmem_limit_bytes=VMEM_LIMIT),
        name="post_mixer",
    )(xp, xs, mod, norm1, w_gates_t, zc, o_p, o_s, w_conv_out, w_o, w_mix_out, norm2, w_route)


DISP_IN = T // TM_DISP
DISP_OUT = T // TM_MOE


def _dispatch_kernel(pos_ref, h3_ref, m_ref, hs_ref, ms_ref, xs_ref, mss_ref):
    i = pl.program_id(0)

    @pl.when(i < DISP_IN)
    def _():
        base = i * TM_DISP

        def body(r, carry):
            p = pos_ref[base + r]
            xs_ref[_slab(p), :] = h3_ref[_slab(r), :]
            mss_ref[pl.ds(p, 1), :] = m_ref[pl.ds(r, 1), :]
            return carry

        lax.fori_loop(0, TM_DISP, body, 0, unroll=8)

    @pl.when(i >= DISP_IN)
    def _():
        row0 = pl.multiple_of((i - DISP_IN) * TM_MOE, TM_MOE)
        for c in range(N_SLAB):
            hs_ref[:, LANE * c:LANE * (c + 1)] = (
                xs_ref[pl.ds(row0 * N_SLAB + c, TM_MOE, stride=N_SLAB), :].astype(BF16))
        ms_ref[...] = mss_ref[pl.ds(row0, TM_MOE), :]


def _dispatch(pos, h3, meta):
    n_slab = N_SLAB
    in_map = lambda i, pos: (jnp.minimum(i, DISP_IN - 1), 0)
    out_map = lambda i, pos: (jnp.maximum(i - DISP_IN, 0), 0)
    return pl.pallas_call(
        _dispatch_kernel,
        grid_spec=pltpu.PrefetchScalarGridSpec(
            num_scalar_prefetch=1,
            grid=(DISP_IN + DISP_OUT,),
            in_specs=[pl.BlockSpec((TM_DISP * n_slab, LANE), in_map),
                      pl.BlockSpec((TM_DISP, LANE), in_map)],
            out_specs=[pl.BlockSpec((TM_MOE, D_MODEL), out_map),
                       pl.BlockSpec((TM_MOE, LANE), out_map)],
            scratch_shapes=[pltpu.VMEM((T * n_slab, LANE), F32),
                            pltpu.VMEM((T, LANE), F32)],
        ),
        out_shape=[jax.ShapeDtypeStruct((T, D_MODEL), BF16),
                   jax.ShapeDtypeStruct((T, LANE), F32)],
        compiler_params=pltpu.CompilerParams(
            dimension_semantics=("arbitrary",), vmem_limit_bytes=VMEM_LIMIT),
        name="moe_dispatch",
    )(pos, h3, meta)


def _moe_kernel(vt_ref, vg_ref, vlo_ref, vhi_ref, vfirst_ref, vlast_ref, vvalid_ref,
                hs_ref, ms_ref, wup_ref, wgate_ref, wdown_ref, y3_ref, acc_ref):
    v = pl.program_id(0)
    j = pl.program_id(1)
    valid = vvalid_ref[v] == 1
    lo = vlo_ref[v]
    hi = vhi_ref[v]
    e0 = vg_ref[v] * EXP_PER_GROUP + j * MOE_EPS
    full = (hi - lo) * 4 >= TM_MOE * 3

    @pl.when(valid & (j == 0) & (vfirst_ref[v] == 1))
    def _():
        acc_ref[...] = jnp.zeros_like(acc_ref)

    def expert_rows(r0, rows):
        w_in2 = jnp.concatenate(
            [w[k].astype(BF16) for k in range(MOE_EPS) for w in (wup_ref, wgate_ref)], axis=1)
        ag = _dot(hs_ref[pl.ds(r0, rows), :], w_in2)
        comb = ms_ref[pl.ds(r0, rows), :]
        lane = lax.broadcasted_iota(jnp.int32, comb.shape, 1)
        acts = []
        for k in range(MOE_EPS):
            a = ag[:, 2 * k * D_EXPERT:(2 * k + 1) * D_EXPERT]
            g = ag[:, (2 * k + 1) * D_EXPERT:(2 * k + 2) * D_EXPERT]
            cw = jnp.sum(jnp.where(lane == e0 + k, comb, 0.0), axis=-1, keepdims=True)
            acts.append(((g * jax.nn.sigmoid(g)) * a * cw).astype(BF16))
        w_out = jnp.concatenate([wdown_ref[k].astype(BF16) for k in range(MOE_EPS)], axis=0)
        acc_ref[pl.ds(r0, rows), :] += _dot(jnp.concatenate(acts, axis=1), w_out)

    @pl.when(valid & full)
    def _():
        expert_rows(0, TM_MOE)

    @pl.when(valid & jnp.logical_not(full))
    def _():
        def sub_block(s, carry):
            expert_rows(pl.multiple_of(s * MOE_SUB, MOE_SUB), MOE_SUB)
            return carry

        lax.fori_loop(lo // MOE_SUB, (hi + MOE_SUB - 1) // MOE_SUB, sub_block, 0)

    @pl.when(valid & (j == EXP_PER_GROUP // MOE_EPS - 1) & (vlast_ref[v] == 1))
    def _():
        for c in range(N_SLAB):
            y3_ref[pl.ds(c, TM_MOE, stride=N_SLAB), :] = acc_ref[:, LANE * c:LANE * (c + 1)]


def _moe(sched, hs, ms, w_up, w_gate, w_down):
    steps = EXP_PER_GROUP // MOE_EPS
    wmap = lambda v, j, vt, vg, vlo, vhi, vfirst, vlast, vvalid: (
        vg[v] * steps + jnp.where(vvalid[v] == 1, j, steps - 1), 0, 0)
    tmap = lambda v, j, vt, *_: (vt[v], 0)
    n_slab = D_MODEL // LANE
    return pl.pallas_call(
        _moe_kernel,
        grid_spec=pltpu.PrefetchScalarGridSpec(
            num_scalar_prefetch=7,
            grid=(N_VISITS, steps),
            in_specs=[
                pl.BlockSpec((TM_MOE, D_MODEL), tmap),
                pl.BlockSpec((TM_MOE, LANE), tmap),
                pl.BlockSpec((MOE_EPS, D_MODEL, D_EXPERT), wmap),
                pl.BlockSpec((MOE_EPS, D_MODEL, D_EXPERT), wmap),
                pl.BlockSpec((MOE_EPS, D_EXPERT, D_MODEL), wmap),
            ],
            out_specs=pl.BlockSpec((TM_MOE * n_slab, LANE), tmap),
            scratch_shapes=[pltpu.VMEM((TM_MOE, D_MODEL), F32)],
        ),
        out_shape=jax.ShapeDtypeStruct((T * n_slab, LANE), F32),
        compiler_params=pltpu.CompilerParams(
            dimension_semantics=("arbitrary", "arbitrary"), vmem_limit_bytes=VMEM_LIMIT),
        name="moe_grouped",
    )(*sched, hs, ms, w_up, w_gate, w_down)


def _final_kernel(pos_ref, ys_ref, x1_ref, mod_ref, fn_ref, yp_ref, ysm_ref, g_ref):
    i = pl.program_id(0)
    base = i * TM_FINAL

    def body(r, carry):
        g_ref[_slab(r), :] = ys_ref[_slab(pos_ref[base + r]), :]
        return carry

    lax.fori_loop(0, TM_FINAL, body, 0, unroll=8)
    mod = mod_ref[pl.ds(_mod_row(i, TM_FINAL), 1), :]
    gate2 = mod[:, 5 * D_MODEL:6 * D_MODEL]
    moe = jnp.concatenate([g_ref[pl.ds(c, TM_FINAL, stride=N_SLAB), :] for c in range(N_SLAB)], axis=1)
    x2 = x1_ref[...] + gate2 * moe
    y = (x2 * _rms(x2)) * fn_ref[...]
    is_sample = i >= T_P // TM_FINAL

    @pl.when(jnp.logical_not(is_sample))
    def _():
        yp_ref[...] = y

    @pl.when(is_sample)
    def _():
        ysm_ref[...] = y


def _final(pos, ys, x1, mod, final_norm):
    n_slab = D_MODEL // LANE
    pmap, smap = _stream_maps(TM_FINAL)
    return pl.pallas_call(
        _final_kernel,
        grid_spec=pltpu.PrefetchScalarGridSpec(
            num_scalar_prefetch=1,
            grid=(T // TM_FINAL,),
            in_specs=[
                pl.BlockSpec((T * n_slab, LANE), lambda i, pos: (0, 0), pipeline_mode=pl.Buffered(1)),
                pl.BlockSpec((TM_FINAL, D_MODEL), lambda i, pos: (i, 0)),
                pl.BlockSpec((N_COND, 6 * D_MODEL), lambda i, pos: (0, 0)),
                pl.BlockSpec((1, D_MODEL), lambda i, pos: (0, 0)),
            ],
            out_specs=[pl.BlockSpec((TM_FINAL, D_MODEL), pmap),
                       pl.BlockSpec((TM_FINAL, D_MODEL), smap)],
            scratch_shapes=[pltpu.VMEM((TM_FINAL * n_slab, LANE), F32)],
        ),
        out_shape=[jax.ShapeDtypeStruct((T_P, D_MODEL), F32),
                   jax.ShapeDtypeStruct((T_S, D_MODEL), F32)],
        compiler_params=pltpu.CompilerParams(
            dimension_semantics=("arbitrary",), vmem_limit_bytes=VMEM_LIMIT),
        name="moe_unsort_final",
    )(pos, ys, x1, mod, final_norm)


def _schedule(gr, cnt):
    n_tiles_post = T // TM_POST
    counts = cnt[:, 0, N_EXPERTS:N_EXPERTS + N_GROUPS].astype(jnp.int32)
    gtot = jnp.sum(counts, axis=0)
    goff = jnp.cumsum(gtot) - gtot
    tile_base = goff[None, :] + jnp.cumsum(counts, axis=0) - counts
    gid = gr[0].astype(jnp.int32).reshape(n_tiles_post, TM_POST)
    rank = gr[1].astype(jnp.int32).reshape(n_tiles_post, TM_POST)
    pos = rank
    for grp in range(N_GROUPS):
        pos = pos + jnp.where(gid == grp, tile_base[:, grp:grp + 1], 0)
    pos = pos.reshape(T)

    n_tiles = T // TM_MOE
    t_lo = (jnp.arange(n_tiles, dtype=jnp.int32) * TM_MOE)[:, None]
    lo = jnp.clip(goff[None, :] - t_lo, 0, TM_MOE)
    hi = jnp.clip(goff[None, :] + gtot[None, :] - t_lo, 0, TM_MOE)
    ok = (hi > lo).reshape(-1)
    slot = jnp.cumsum(ok.astype(jnp.int32)) - 1
    n_ok = slot[-1] + 1
    sel = (slot[None, :] == jnp.arange(N_VISITS, dtype=jnp.int32)[:, None]) & ok[None, :]

    def pick(vals):
        return jnp.sum(jnp.where(sel, vals.reshape(-1)[None, :], 0), axis=1).astype(jnp.int32)

    pair_tile = jnp.broadcast_to(jnp.arange(n_tiles, dtype=jnp.int32)[:, None], (n_tiles, N_GROUPS))
    pair_group = jnp.broadcast_to(jnp.arange(N_GROUPS, dtype=jnp.int32)[None, :], (n_tiles, N_GROUPS))
    vt, vg, vlo, vhi = pick(pair_tile), pick(pair_group), pick(lo), pick(hi)
    valid = jnp.arange(N_VISITS, dtype=jnp.int32) < n_ok
    last = jnp.maximum(n_ok - 1, 0)
    vt = jnp.where(valid, vt, vt[last])
    vg = jnp.where(valid, vg, vg[last])
    change = (vt[1:] != vt[:-1]).astype(jnp.int32)
    one = jnp.ones((1,), jnp.int32)
    first = jnp.concatenate([one, change])
    idx = jnp.arange(N_VISITS, dtype=jnp.int32)
    last = jnp.where(idx == n_ok - 1, 1, jnp.concatenate([change, one]))
    return pos, (vt, vg, vlo, vhi, first, last, valid.astype(jnp.int32))


def _rope_tables():
    n = np.arange(DEC_SEQ)
    pos = np.stack([n // GRID_W, n % GRID_W], axis=1).astype(np.float32)
    half = ROPE_AXIS // 2
    inv = (1.0 / (ROPE_BASE ** (np.arange(0, ROPE_AXIS, 2, dtype=np.float32) / ROPE_AXIS))).astype(np.float32)
    ang = (pos[:, :, None] * inv[None, None, :]).astype(np.float32)
    cos = np.cos(ang).astype(np.float32)
    sin = np.sin(ang).astype(np.float32)
    tabs = np.zeros((2, 3, DEC_SEQ, LANE), np.float32)
    tabs[:, 0] = 1.0
    for a in range(2):
        lo = ROPE_LANE0 + a * ROPE_AXIS
        tabs[1, 0, :, lo:lo + half] = cos[:, a]
        tabs[1, 0, :, lo + half:lo + 2 * half] = cos[:, a]
        tabs[1, 1, :, lo + half:lo + 2 * half] = sin[:, a]
        tabs[1, 2, :, lo:lo + half] = -sin[:, a]
    return jnp.asarray(tabs)


def kernel(x_prompt, x_sample, cache_ckv, cache_krope, c, c_ctx, norm1, w_ada, b_ada, w_in, conv_w,
           w_conv_out, q_norm, w_uq, kv_norm, w_ukv, w_o, w_mix_out, norm2, w_grp, w_exp, w_up,
           w_gate, w_down, final_norm):
    l = 0
    xp = x_prompt.reshape(T_P, D_MODEL)
    xs = x_sample.reshape(T_S, D_MODEL)
    cond = jnp.concatenate(
        [c_ctx[None, :], c, jnp.zeros((N_COND - 1 - DEC_BATCH, D_MODEL), F32)], axis=0)
    mod = _ada(cond, w_ada[l], b_ada[l][None, :])

    w_in_t = w_in[l].T
    w_uq_slot = jnp.pad(w_uq[l].reshape(Q_LORA, N_HEADS, QK_NOPE + QK_ROPE),
                        ((0, 0), (0, 0), (0, LANE - QK_NOPE - QK_ROPE))
                        ).reshape(Q_LORA, N_HEADS * LANE).astype(BF16)
    wkv = w_ukv[l].reshape(KV_LORA, N_HEADS, QK_NOPE + V_HEAD)
    wk_slot = jnp.pad(wkv[:, :, :QK_NOPE], ((0, 0), (0, 0), (0, LANE - QK_NOPE)))
    wv = wkv[:, :, QK_NOPE:].reshape(KV_LORA, N_HEADS // 2, 2, V_HEAD)
    zero = jnp.zeros_like(wv[:, :, 0])
    wv_slot = jnp.stack([jnp.concatenate([wv[:, :, 0], zero], axis=-1),
                         jnp.concatenate([zero, wv[:, :, 1]], axis=-1)], axis=2)
    w_ukv_slot = jnp.concatenate([wk_slot.reshape(KV_LORA, N_HEADS * LANE),
                                  wv_slot.reshape(KV_LORA, N_HEADS * LANE)], axis=1).astype(BF16)
    w_route = jnp.pad(jnp.concatenate([w_exp[l], w_grp[l]], axis=1),
                      ((0, 0), (0, LANE - N_EXPERTS - N_GROUPS)))
    w_route_hi = w_route.astype(BF16)
    w_route_lo = (w_route - w_route_hi.astype(F32)).astype(BF16)
    w_route2 = jnp.concatenate([w_route_hi, w_route_lo], axis=1)
    cache_krs = jnp.pad(cache_krope[:, l], ((0, 0), (0, 0), (ROPE_LANE0, LANE - ROPE_LANE0 - QK_ROPE)))

    zc, q, ckv, krs, nckv, nkr = _inproj(xp, xs, mod, norm1[l][None, :], w_in_t, conv_w[l],
                              q_norm[l][None, :], kv_norm[l][None, :], w_uq_slot, _rope_tables())
    o_p, o_s, w_gates_t, w_co_b, w_o_b, w_mix_b = _attention(
        q, ckv, krs, cache_ckv[:, l], cache_krs, w_ukv_slot, w_in_t, w_conv_out[l], w_o[l], w_mix_out[l])
    x1, h3, meta, cnt, gr = _post(xp, xs, mod, norm1[l][None, :], w_gates_t, zc, o_p, o_s, w_co_b, w_o_b,
                                  w_mix_b, norm2[l][None, :], w_route2)
    pos, sched = _schedule(gr, cnt)
    hs, ms = _dispatch(pos, h3, meta)
    ys = _moe(sched, hs, ms, w_up[l], w_gate[l], w_down[l])
    yp, ysm = _final(pos, ys, x1, mod, final_norm[None, :])

    y_prompt = yp.reshape(BATCH, SEQ, D_MODEL)
    y_sample = ysm.reshape(DEC_BATCH, DEC_SEQ, D_MODEL)
    new_ckv = nckv.reshape(BATCH, 1, SEQ, KV_LORA)
    new_krope = nkr.reshape(BATCH, 1, SEQ, QK_ROPE)
    return (y_prompt, y_sample, new_ckv, new_krope)
```

```python
import functools

import numpy as np
import jax
import jax.numpy as jnp
from jax import lax
from jax.experimental import pallas as pl
from jax.experimental.pallas import tpu as pltpu

F32 = jnp.float32
BF16 = jnp.bfloat16

D_MODEL = 1024
BATCH = 16
SEQ = 256
DEC_BATCH = 2
DEC_SEQ = 1024
PAST_LEN = 256
GRID_W = 64
N_HEADS = 8
QK_NOPE = 64
QK_ROPE = 32
V_HEAD = 64
Q_LORA = 256
KV_LORA = 128
ROPE_AXIS = QK_ROPE // 2
ROPE_BASE = 10000.0
ATTN_SCALE = (QK_NOPE + QK_ROPE) ** -0.5
D_CONV = D_MODEL
N_GROUPS = 4
EXP_PER_GROUP = 8
N_EXPERTS = N_GROUPS * EXP_PER_GROUP
D_EXPERT = 256
EPS = 1e-6

T_P = BATCH * SEQ
T_S = DEC_BATCH * DEC_SEQ
T = T_P + T_S
N_COND = 8
LANE = 128
ROPE_LANE0 = QK_NOPE
SMALL_COLS = Q_LORA + KV_LORA + LANE
VMEM_LIMIT = 56 * 1024 * 1024

TM_IN = 1024
TM_POST = 512
TM_MOE = T // N_GROUPS
TM_FINAL = 512
TM_DISP = 1024
N_SLAB = D_MODEL // LANE
MOE_SUB = 256
MOE_EPS = 2
N_VISITS = T // TM_MOE + N_GROUPS - 1
GID_LANE = 40
RANK_LANE = 41
Q_BLK = 256
Q_BLK_S = 512
PROMPT_SEQS = 2
CONV_CHUNK = 256


def _dot(a, b):
    return jnp.dot(a, b, preferred_element_type=F32)


def _rms(x):
    return lax.rsqrt(jnp.mean(x * x, axis=-1, keepdims=True) + EPS)


def _slab(t):
    return pl.ds(pl.multiple_of(t * N_SLAB, N_SLAB), N_SLAB)


def _mod_row(i, tm):
    n_prompt = T_P // tm
    return jnp.where(i >= n_prompt, 1 + ((i - n_prompt) * tm) // DEC_SEQ, 0)


def _ada_kernel(cctx_ref, c_ref, w_ref, b_ref, o_ref):
    c = jnp.concatenate([cctx_ref[...], c_ref[...],
                         jnp.zeros((N_COND - 1 - DEC_BATCH, D_MODEL), F32)], axis=0)
    a = (c * jax.nn.sigmoid(c)).astype(BF16)
    o_ref[...] = _dot(a, w_ref[...].astype(BF16)) + b_ref[...]


def _ada(c_ctx, c, w_ada, b_ada):
    n = 6 * D_MODEL
    bn = 1536
    return pl.pallas_call(
        _ada_kernel,
        grid=(n // bn,),
        in_specs=[
            pl.BlockSpec((1, D_MODEL), lambda j: (0, 0)),
            pl.BlockSpec((DEC_BATCH, D_MODEL), lambda j: (0, 0)),
            pl.BlockSpec((D_MODEL, bn), lambda j: (0, j)),
            pl.BlockSpec((1, bn), lambda j: (0, j)),
        ],
        out_specs=pl.BlockSpec((N_COND, bn), lambda j: (0, j)),
        out_shape=jax.ShapeDtypeStruct((N_COND, n), F32),
        compiler_params=pltpu.CompilerParams(
            dimension_semantics=("parallel",), vmem_limit_bytes=VMEM_LIMIT),
        name="ada_mod",
    )(c_ctx, c, w_ada, b_ada)


O_CQ = 3 * D_CONV
O_KR = O_CQ + Q_LORA + KV_LORA
O_GATE = O_KR + QK_ROPE
PREP_STEPS = 4


IN_COLS = O_GATE + 2 * D_MODEL
TAIL_ROWS = IN_COLS // 2
SMALL_BLK = 512
NT = (((1,), (1,)), ((), ()))


def _dot_nt(a, bt):
    return lax.dot_general(a, bt, NT, preferred_element_type=F32)


def _stream_maps(tm):
    n_prompt = T_P // tm
    return (lambda i, *_: (jnp.minimum(i, n_prompt - 1), 0),
            lambda i, *_: (jnp.maximum(i - n_prompt, 0), 0))


def _inproj_kernel(xp_ref, xs_ref, mod_ref, n1_ref, wc_ref, ws_ref, cw_ref, qn_ref, kvn_ref, wuq_ref,
                   rope_ref, zc_ref, q_ref, ckv_ref, krs_ref, nckv_ref, nkr_ref):
    i = pl.program_id(0)
    is_sample = i >= T_P // TM_IN
    seq = jnp.where(is_sample, DEC_SEQ, SEQ)
    mod = mod_ref[pl.ds(_mod_row(i, TM_IN), 1), :]
    shift1 = mod[:, 0:D_MODEL]
    scale1 = mod[:, D_MODEL:2 * D_MODEL]
    x = jnp.where(is_sample, xs_ref[...], xp_ref[...])
    h = ((x * _rms(x)) * n1_ref[...]) * (1.0 + scale1) + shift1
    hb = h.astype(BF16)

    n_lat = Q_LORA + KV_LORA
    w_small = jnp.concatenate([
        ws_ref[0:n_lat, :].astype(BF16),
        jnp.zeros((ROPE_LANE0, D_MODEL), BF16),
        ws_ref[n_lat:n_lat + QK_ROPE, :].astype(BF16),
        jnp.zeros((LANE - ROPE_LANE0 - QK_ROPE, D_MODEL), BF16)], axis=0)
    sm = _dot_nt(hb, w_small)
    cq = sm[:, 0:Q_LORA]
    ckv_raw = sm[:, Q_LORA:Q_LORA + KV_LORA]
    krs = sm[:, Q_LORA + KV_LORA:SMALL_COLS]
    cqn = (cq * _rms(cq)) * qn_ref[...]
    q = _dot(cqn.astype(BF16), wuq_ref[...])
    ckv = (ckv_raw * _rms(ckv_raw)) * kvn_ref[...]
    ckv_ref[...] = ckv

    @pl.when(jnp.logical_not(is_sample))
    def _():
        nckv_ref[...] = ckv
        for s in range(TM_IN // SEQ):
            kt = krs[s * SEQ:(s + 1) * SEQ, :].T
            nkr_ref[s] = kt[ROPE_LANE0:ROPE_LANE0 + QK_ROPE, :]

    cos = rope_ref[0]
    sin_lo = rope_ref[1]
    sin_hi = rope_ref[2]

    def rot(v):
        return v * cos + pltpu.roll(v, 8, 1) * sin_lo + pltpu.roll(v, LANE - 8, 1) * sin_hi

    krs_ref[...] = rot(krs)
    for hh in range(N_HEADS):
        q_ref[:, LANE * hh:LANE * (hh + 1)] = rot(q[:, LANE * hh:LANE * (hh + 1)]).astype(BF16)

    pos = lax.broadcasted_iota(jnp.int32, (TM_IN, 1), 0) & (seq - 1)
    first = pos == 0
    last = pos == seq - 1
    for j in range(D_CONV // CONV_CHUNK):
        c0 = j * CONV_CHUNK
        bg = _dot_nt(hb, wc_ref[c0:c0 + CONV_CHUNK, :].astype(BF16))
        cg = _dot_nt(hb, wc_ref[D_CONV + c0:D_CONV + c0 + CONV_CHUNK, :].astype(BF16))
        ui = _dot_nt(hb, wc_ref[2 * D_CONV + c0:2 * D_CONV + c0 + CONV_CHUNK, :].astype(BF16))
        u = cg * ui
        u_prev = jnp.where(first, 0.0, pltpu.roll(u, 1, 0))
        u_next = jnp.where(last, 0.0, pltpu.roll(u, TM_IN - 1, 0))
        cw = cw_ref[:, c0:c0 + CONV_CHUNK]
        conv = u_prev * cw[0:1] + u * cw[1:2] + u_next * cw[2:3]
        zc_ref[:, c0:c0 + CONV_CHUNK] = (bg * conv).astype(BF16)


def _inproj(xp, xs, mod, norm1, w_in_t, conv_w, q_norm, kv_norm, w_uq_slot, rope_tabs):
    n_prompt = T_P // TM_IN
    const = lambda i: (0, 0)
    pmap, smap = _stream_maps(TM_IN)
    once = pl.Buffered(1)
    return pl.pallas_call(
        _inproj_kernel,
        grid=(T // TM_IN,),
        in_specs=[
            pl.BlockSpec((TM_IN, D_MODEL), pmap),
            pl.BlockSpec((TM_IN, D_MODEL), smap),
            pl.BlockSpec((N_COND, 6 * D_MODEL), const),
            pl.BlockSpec((1, D_MODEL), const),
            pl.BlockSpec((O_CQ, D_MODEL), const, pipeline_mode=once),
            pl.BlockSpec((SMALL_BLK, D_MODEL), lambda i: (O_CQ // SMALL_BLK, 0), pipeline_mode=once),
            pl.BlockSpec((3, D_CONV), const),
            pl.BlockSpec((1, Q_LORA), const),
            pl.BlockSpec((1, KV_LORA), const),
            pl.BlockSpec((Q_LORA, N_HEADS * LANE), const),
            pl.BlockSpec((None, 3, TM_IN, LANE),
                         lambda i: (jnp.where(i >= n_prompt, 1, 0), 0, 0, 0)),
        ],
        out_specs=[
            pl.BlockSpec((TM_IN, D_CONV), lambda i: (i, 0)),
            pl.BlockSpec((TM_IN, N_HEADS * LANE), lambda i: (i, 0)),
            pl.BlockSpec((TM_IN, KV_LORA), lambda i: (i, 0)),
            pl.BlockSpec((TM_IN, LANE), lambda i: (i, 0)),
            pl.BlockSpec((TM_IN, KV_LORA), pmap),
            pl.BlockSpec((TM_IN // SEQ, QK_ROPE, SEQ), lambda i: (jnp.minimum(i, n_prompt - 1), 0, 0)),
        ],
        out_shape=[
            jax.ShapeDtypeStruct((T, D_CONV), BF16),
            jax.ShapeDtypeStruct((T, N_HEADS * LANE), BF16),
            jax.ShapeDtypeStruct((T, KV_LORA), F32),
            jax.ShapeDtypeStruct((T, LANE), F32),
            jax.ShapeDtypeStruct((T_P, KV_LORA), F32),
            jax.ShapeDtypeStruct((BATCH, QK_ROPE, SEQ), F32),
        ],
        compiler_params=pltpu.CompilerParams(
            dimension_semantics=("arbitrary",), vmem_limit_bytes=VMEM_LIMIT),
        name="in_proj",
    )(xp, xs, mod, norm1, w_in_t, w_in_t, conv_w, q_norm, kv_norm, w_uq_slot, rope_tabs)


def _fill_kv(ckv, krs, wukv_ref, kf_scr, v_scr, off):
    m = ckv.shape[0]
    kv = _dot(ckv.astype(BF16), wukv_ref[...])
    for hh in range(N_HEADS):
        kf_scr[hh, off:off + m, :] = (kv[:, LANE * hh:LANE * (hh + 1)] + krs).astype(BF16)
    v_scr[off:off + m, :] = kv[:, N_HEADS * LANE:].astype(BF16)


def _attend(q_ref, r0, rows, kf_scr, v_scr, o_ref):
    for pair in range(N_HEADS // 2):
        acc = None
        for hh in (2 * pair, 2 * pair + 1):
            qh = q_ref[r0:r0 + rows, LANE * hh:LANE * (hh + 1)]
            s = _dot_nt(qh, kf_scr[hh]) * ATTN_SCALE
            e = jnp.exp(s - jnp.max(s, axis=-1, keepdims=True))
            p = (e / jnp.sum(e, axis=-1, keepdims=True)).astype(BF16)
            part = _dot(p, v_scr[:, LANE * hh:LANE * (hh + 1)])
            acc = part if acc is None else acc + part
        o_ref[r0:r0 + rows, LANE * pair:LANE * (pair + 1)] = acc.astype(BF16)


def _attn_prompt_kernel(q_ref, ckv_ref, krs_ref, wukv_ref, o_ref, kf_scr, v_scr):
    for s in range(PROMPT_SEQS):
        r0 = s * SEQ
        _fill_kv(ckv_ref[r0:r0 + SEQ, :], krs_ref[r0:r0 + SEQ, :], wukv_ref, kf_scr, v_scr, 0)
        _attend(q_ref, r0, SEQ, kf_scr, v_scr, o_ref)


def _attn_sample_kernel(q_ref, ckv_ref, cckv_ref, krs_ref, ckrs_ref, wukv_ref, wg_ref, wco_ref, wo_ref,
                        wmix_ref, o_ref, g_ref, co_ref, ob_ref, mix_ref, kf_scr, v_scr):
    @pl.when(pl.program_id(1) == 0)
    def _():
        _fill_kv(ckv_ref[...], krs_ref[...], wukv_ref, kf_scr, v_scr, 0)
        _fill_kv(cckv_ref[...], ckrs_ref[...], wukv_ref, kf_scr, v_scr, DEC_SEQ)

    _attend(q_ref, 0, Q_BLK_S, kf_scr, v_scr, o_ref)
    g_ref[...] = wg_ref[...].astype(BF16)
    co_ref[...] = wco_ref[...].astype(BF16)
    ob_ref[...] = wo_ref[...].astype(BF16)
    mix_ref[...] = wmix_ref[...].astype(BF16)


def _attention(q, ckv, krs, cache_ckv, cache_krs, w_ukv_slot, w_in_t, w_conv_out, w_o, w_mix_out):
    kv_cols = 2 * N_HEADS * LANE
    n_o = N_HEADS * V_HEAD
    steps = BATCH // PROMPT_SEQS
    rows = PROMPT_SEQS * SEQ
    o_prompt = pl.pallas_call(
        _attn_prompt_kernel,
        grid=(steps,),
        in_specs=[
            pl.BlockSpec((rows, N_HEADS * LANE), lambda b: (b, 0)),
            pl.BlockSpec((rows, KV_LORA), lambda b: (b, 0)),
            pl.BlockSpec((rows, LANE), lambda b: (b, 0)),
            pl.BlockSpec((KV_LORA, kv_cols), lambda b: (0, 0)),
        ],
        out_specs=pl.BlockSpec((rows, n_o), lambda b: (b, 0)),
        out_shape=jax.ShapeDtypeStruct((T_P, n_o), BF16),
        scratch_shapes=[pltpu.VMEM((N_HEADS, SEQ, LANE), BF16),
                        pltpu.VMEM((SEQ, N_HEADS * LANE), BF16)],
        compiler_params=pltpu.CompilerParams(dimension_semantics=("parallel",),
                                             vmem_limit_bytes=VMEM_LIMIT),
        name="attn_prompt",
    )(q, ckv, krs, w_ukv_slot)

    m_all = DEC_SEQ + PAST_LEN
    nq = DEC_SEQ // Q_BLK_S
    q0 = T_P // Q_BLK_S
    s0 = T_P // DEC_SEQ
    n_step = DEC_BATCH * nq
    share = lambda n: pl.BlockSpec((n // n_step, D_MODEL), lambda b, j: (b * nq + j, 0))
    gate_rows = 2 * D_MODEL // n_step
    o_sample, w_gates_t, w_co_b, w_o_b, w_mix_b = pl.pallas_call(
        _attn_sample_kernel,
        grid=(DEC_BATCH, nq),
        in_specs=[
            pl.BlockSpec((Q_BLK_S, N_HEADS * LANE), lambda b, j: (q0 + b * nq + j, 0)),
            pl.BlockSpec((DEC_SEQ, KV_LORA), lambda b, j: (s0 + b, 0)),
            pl.BlockSpec((None, PAST_LEN, KV_LORA), lambda b, j: (b, 0, 0)),
            pl.BlockSpec((DEC_SEQ, LANE), lambda b, j: (s0 + b, 0)),
            pl.BlockSpec((None, PAST_LEN, LANE), lambda b, j: (b, 0, 0)),
            pl.BlockSpec((KV_LORA, kv_cols), lambda b, j: (0, 0)),
            pl.BlockSpec((pl.Element(gate_rows), pl.Element(D_MODEL)),
                         lambda b, j: (pl.multiple_of(O_GATE + (b * nq + j) * gate_rows, 32), 0)),
            share(D_CONV), share(n_o), share(D_MODEL),
        ],
        out_specs=[pl.BlockSpec((Q_BLK_S, n_o), lambda b, j: (b * nq + j, 0)),
                   share(2 * D_MODEL), share(D_CONV), share(n_o), share(D_MODEL)],
        out_shape=[jax.ShapeDtypeStruct((T_S, n_o), BF16),
                   jax.ShapeDtypeStruct((2 * D_MODEL, D_MODEL), BF16),
                   jax.ShapeDtypeStruct((D_CONV, D_MODEL), BF16),
                   jax.ShapeDtypeStruct((n_o, D_MODEL), BF16),
                   jax.ShapeDtypeStruct((D_MODEL, D_MODEL), BF16)],
        scratch_shapes=[pltpu.VMEM((N_HEADS, m_all, LANE), BF16),
                        pltpu.VMEM((m_all, N_HEADS * LANE), BF16)],
        compiler_params=pltpu.CompilerParams(dimension_semantics=("arbitrary", "arbitrary"),
                                             vmem_limit_bytes=VMEM_LIMIT),
        name="attn_sample",
    )(q, ckv, cache_ckv, krs, cache_krs, w_ukv_slot, w_in_t, w_conv_out, w_o, w_mix_out)
    return o_prompt, o_sample, w_gates_t, w_co_b, w_o_b, w_mix_b


def _route(logits):
    lane = lax.broadcasted_iota(jnp.int32, logits.shape, 1)
    neg = -jnp.inf
    big = jnp.int32(1 << 20)
    gmask = (lane >= N_EXPERTS) & (lane < N_EXPERTS + N_GROUPS)
    gl = jnp.where(gmask, logits, neg)
    gmax = jnp.max(gl, axis=-1, keepdims=True)
    gsum = jnp.sum(jnp.where(gmask, jnp.exp(gl - gmax), 0.0), axis=-1, keepdims=True)
    p_g = 1.0 / gsum
    g_idx = jnp.min(jnp.where(gl == gmax, lane, big), axis=-1, keepdims=True) - N_EXPERTS

    emask = (lane < N_EXPERTS) & ((lane >> 3) == g_idx)
    el = jnp.where(emask, logits, neg)
    m1 = jnp.max(el, axis=-1, keepdims=True)
    i1 = jnp.min(jnp.where(el == m1, lane, big), axis=-1, keepdims=True)
    el2 = jnp.where(lane == i1, neg, el)
    m2 = jnp.max(el2, axis=-1, keepdims=True)
    i2 = jnp.min(jnp.where(el2 == m2, lane, big), axis=-1, keepdims=True)
    z = jnp.sum(jnp.where(emask, jnp.exp(el - m1), 0.0), axis=-1, keepdims=True)
    p1 = 1.0 / z
    p2 = jnp.exp(m2 - m1) / z
    tot = p1 + p2
    w1 = p_g * p1 / tot
    w2 = p_g * p2 / tot
    return jnp.where(lane == i1, w1, 0.0) + jnp.where(lane == i2, w2, 0.0), g_idx


def _post_kernel(xp_ref, xs_ref, mod_ref, n1_ref, wg_ref, zc_ref, op_ref, os_ref, wco_ref, wo_ref,
                 wmix_ref, n2_ref, wr_ref, x1_ref, h3_ref, meta_ref, cnt_ref, gr_ref):
    i = pl.program_id(0)
    is_sample = i >= T_P // TM_POST
    x = jnp.where(is_sample, xs_ref[...], xp_ref[...])
    o = jnp.where(is_sample, os_ref[...], op_ref[...])
    mod = mod_ref[pl.ds(_mod_row(i, TM_POST), 1), :]
    shift1 = mod[:, 0:D_MODEL]
    scale1 = mod[:, D_MODEL:2 * D_MODEL]
    gate1 = mod[:, 2 * D_MODEL:3 * D_MODEL]
    shift2 = mod[:, 3 * D_MODEL:4 * D_MODEL]
    scale2 = mod[:, 4 * D_MODEL:5 * D_MODEL]
    y_conv = _dot(zc_ref[...], wco_ref[...])
    y_mla = _dot(o, wo_ref[...])
    h = ((x * _rms(x)) * n1_ref[...]) * (1.0 + scale1) + shift1
    g = _dot_nt(h.astype(BF16), wg_ref[...])
    merged = (jax.nn.sigmoid(g[:, 0:D_MODEL]) * y_conv
              + jax.nn.sigmoid(g[:, D_MODEL:2 * D_MODEL]) * y_mla)
    y = _dot(merged.astype(BF16), wmix_ref[...])
    x1 = x + gate1 * y
    x1_ref[...] = x1
    h2 = ((x1 * _rms(x1)) * n2_ref[...]) * (1.0 + scale2) + shift2
    h2_hi = h2.astype(BF16)
    h2_lo = (h2 - h2_hi.astype(F32)).astype(BF16)
    hh = _dot(h2_hi, wr_ref[...])
    logits = hh[:, 0:LANE] + hh[:, LANE:2 * LANE] + _dot(h2_lo, wr_ref[:, 0:LANE])
    comb, g_idx = _route(logits)

    lane = lax.broadcasted_iota(jnp.int32, comb.shape, 1)
    onehot = lane == g_idx + N_EXPERTS
    r_i = lax.broadcasted_iota(jnp.int32, (TM_POST, TM_POST), 0)
    c_i = lax.broadcasted_iota(jnp.int32, (TM_POST, TM_POST), 1)
    lower = jnp.where(c_i < r_i, 1.0, 0.0).astype(BF16)
    before = _dot(lower, jnp.where(onehot, 1.0, 0.0).astype(BF16))
    rank = jnp.sum(jnp.where(onehot, before, 0.0), axis=-1, keepdims=True)
    counts = jnp.sum(jnp.where(onehot, 1.0, 0.0), axis=0, keepdims=True)
    cnt_ref[...] = jnp.broadcast_to(counts, cnt_ref.shape)

    meta_ref[...] = comb
    idx = jnp.where(lane == GID_LANE, g_idx.astype(F32), 0.0) + jnp.where(lane == RANK_LANE, rank, 0.0)
    idx_hi = idx.astype(BF16)
    idx_lo = (idx - idx_hi.astype(F32)).astype(BF16)
    s_row = lax.broadcasted_iota(jnp.int32, (8, LANE), 0)
    s_lane = lax.broadcasted_iota(jnp.int32, (8, LANE), 1)
    sel = jnp.where(s_lane == GID_LANE + s_row, 1.0, 0.0).astype(BF16)
    gr_ref[...] = _dot_nt(sel, idx_hi) + _dot_nt(sel, idx_lo)
    for c in range(N_SLAB):
        h3_ref[pl.ds(c, TM_POST, stride=N_SLAB), :] = h2[:, LANE * c:LANE * (c + 1)]


def _post(xp, xs, mod, norm1, w_gates_t, zc, o_p, o_s, w_conv_out, w_o, w_mix_out, norm2, w_route):
    const = lambda i: (0, 0)
    row = lambda i: (i, 0)
    pmap, smap = _stream_maps(TM_POST)
    return pl.pallas_call(
        _post_kernel,
        grid=(T // TM_POST,),
        in_specs=[
            pl.BlockSpec((TM_POST, D_MODEL), pmap),
            pl.BlockSpec((TM_POST, D_MODEL), smap),
            pl.BlockSpec((N_COND, 6 * D_MODEL), const),
            pl.BlockSpec((1, D_MODEL), const),
            pl.BlockSpec((2 * D_MODEL, D_MODEL), const),
            pl.BlockSpec((TM_POST, D_CONV), row),
            pl.BlockSpec((TM_POST, N_HEADS * V_HEAD), pmap),
            pl.BlockSpec((TM_POST, N_HEADS * V_HEAD), smap),
            pl.BlockSpec((D_CONV, D_MODEL), const),
            pl.BlockSpec((N_HEADS * V_HEAD, D_MODEL), const),
            pl.BlockSpec((D_MODEL, D_MODEL), const),
            pl.BlockSpec((1, D_MODEL), const),
            pl.BlockSpec((D_MODEL, 2 * LANE), const),
        ],
        out_specs=[
            pl.BlockSpec((TM_POST, D_MODEL), row),
            pl.BlockSpec((TM_POST * N_SLAB, LANE), row),
            pl.BlockSpec((TM_POST, LANE), row),
            pl.BlockSpec((None, 8, LANE), lambda i: (i, 0, 0)),
            pl.BlockSpec((8, TM_POST), lambda i: (0, i)),
        ],
        out_shape=[
            jax.ShapeDtypeStruct((T, D_MODEL), F32),
            jax.ShapeDtypeStruct((T * N_SLAB, LANE), F32),
            jax.ShapeDtypeStruct((T, LANE), F32),
            jax.ShapeDtypeStruct((T // TM_POST, 8, LANE), F32),
            jax.ShapeDtypeStruct((8, T), F32),
        ],
        compiler_params=pltpu.CompilerParams(
            dimension_semantics=("parallel",), vmem_limit_bytes=VMEM_LIMIT),
        name="post_mixer",
    )(xp, xs, mod, norm1, w_gates_t, zc, o_p, o_s, w_conv_out, w_o, w_mix_out, norm2, w_route)


DISP_IN = T // TM_DISP
DISP_OUT = T // TM_MOE


def _dispatch_kernel(pos_ref, h3_ref, m_ref, hs_ref, ms_ref, xs_ref, mss_ref):
    i = pl.program_id(0)

    @pl.when(i < DISP_IN)
    def _():
        base = i * TM_DISP

        def body(r, carry):
            p = pos_ref[base + r]
            xs_ref[_slab(p), :] = h3_ref[_slab(r), :]
            mss_ref[pl.ds(p, 1), :] = m_ref[pl.ds(r, 1), :]
            return carry

        lax.fori_loop(0, TM_DISP, body, 0, unroll=8)

    @pl.when(i >= DISP_IN)
    def _():
        row0 = pl.multiple_of((i - DISP_IN) * TM_MOE, TM_MOE)
        for c in range(N_SLAB):
            hs_ref[:, LANE * c:LANE * (c + 1)] = (
                xs_ref[pl.ds(row0 * N_SLAB + c, TM_MOE, stride=N_SLAB), :].astype(BF16))
        ms_ref[...] = mss_ref[pl.ds(row0, TM_MOE), :]


def _dispatch(pos, h3, meta):
    n_slab = N_SLAB
    in_map = lambda i, pos: (jnp.minimum(i, DISP_IN - 1), 0)
    out_map = lambda i, pos: (jnp.maximum(i - DISP_IN, 0), 0)
    return pl.pallas_call(
        _dispatch_kernel,
        grid_spec=pltpu.PrefetchScalarGridSpec(
            num_scalar_prefetch=1,
            grid=(DISP_IN + DISP_OUT,),
            in_specs=[pl.BlockSpec((TM_DISP * n_slab, LANE), in_map),
                      pl.BlockSpec((TM_DISP, LANE), in_map)],
            out_specs=[pl.BlockSpec((TM_MOE, D_MODEL), out_map),
                       pl.BlockSpec((TM_MOE, LANE), out_map)],
            scratch_shapes=[pltpu.VMEM((T * n_slab, LANE), F32),
                            pltpu.VMEM((T, LANE), F32)],
        ),
        out_shape=[jax.ShapeDtypeStruct((T, D_MODEL), BF16),
                   jax.ShapeDtypeStruct((T, LANE), F32)],
        compiler_params=pltpu.CompilerParams(
            dimension_semantics=("arbitrary",), vmem_limit_bytes=VMEM_LIMIT),
        name="moe_dispatch",
    )(pos, h3, meta)


def _moe_kernel(sched_ref, hs_ref, ms_ref, wup_ref, wgate_ref, wdown_ref, y3_ref, acc_ref):
    v = pl.program_id(0)
    j = pl.program_id(1)
    valid = sched_ref[V_VALID, v] == 1
    lo = sched_ref[V_LO, v]
    hi = sched_ref[V_HI, v]
    e0 = sched_ref[V_GROUP, v] * EXP_PER_GROUP + j * MOE_EPS
    full = (hi - lo) * 4 >= TM_MOE * 3

    @pl.when(valid & (j == 0) & (sched_ref[V_FIRST, v] == 1))
    def _():
        acc_ref[...] = jnp.zeros_like(acc_ref)

    def expert_rows(r0, rows):
        w_in2 = jnp.concatenate(
            [w[k].astype(BF16) for k in range(MOE_EPS) for w in (wup_ref, wgate_ref)], axis=1)
        ag = _dot(hs_ref[pl.ds(r0, rows), :], w_in2)
        comb = ms_ref[pl.ds(r0, rows), :]
        lane = lax.broadcasted_iota(jnp.int32, comb.shape, 1)
        acts = []
        for k in range(MOE_EPS):
            a = ag[:, 2 * k * D_EXPERT:(2 * k + 1) * D_EXPERT]
            g = ag[:, (2 * k + 1) * D_EXPERT:(2 * k + 2) * D_EXPERT]
            cw = jnp.sum(jnp.where(lane == e0 + k, comb, 0.0), axis=-1, keepdims=True)
            acts.append(((g * jax.nn.sigmoid(g)) * a * cw).astype(BF16))
        w_out = jnp.concatenate([wdown_ref[k].astype(BF16) for k in range(MOE_EPS)], axis=0)
        acc_ref[pl.ds(r0, rows), :] += _dot(jnp.concatenate(acts, axis=1), w_out)

    @pl.when(valid & full)
    def _():
        expert_rows(0, TM_MOE)

    @pl.when(valid & jnp.logical_not(full))
    def _():
        def sub_block(s, carry):
            expert_rows(pl.multiple_of(s * MOE_SUB, MOE_SUB), MOE_SUB)
            return carry

        lax.fori_loop(lo // MOE_SUB, (hi + MOE_SUB - 1) // MOE_SUB, sub_block, 0)

    @pl.when(valid & (j == EXP_PER_GROUP // MOE_EPS - 1) & (sched_ref[V_LAST, v] == 1))
    def _():
        for c in range(N_SLAB):
            y3_ref[pl.ds(c, TM_MOE, stride=N_SLAB), :] = acc_ref[:, LANE * c:LANE * (c + 1)]


def _moe(sched, hs, ms, w_up, w_gate, w_down):
    steps = EXP_PER_GROUP // MOE_EPS
    wmap = lambda v, j, sched: (
        sched[V_GROUP, v] * steps + jnp.where(sched[V_VALID, v] == 1, j, steps - 1), 0, 0)
    tmap = lambda v, j, sched: (sched[V_TILE, v], 0)
    n_slab = D_MODEL // LANE
    return pl.pallas_call(
        _moe_kernel,
        grid_spec=pltpu.PrefetchScalarGridSpec(
            num_scalar_prefetch=1,
            grid=(N_VISITS, steps),
            in_specs=[
                pl.BlockSpec((TM_MOE, D_MODEL), tmap),
                pl.BlockSpec((TM_MOE, LANE), tmap),
                pl.BlockSpec((MOE_EPS, D_MODEL, D_EXPERT), wmap),
                pl.BlockSpec((MOE_EPS, D_MODEL, D_EXPERT), wmap),
                pl.BlockSpec((MOE_EPS, D_EXPERT, D_MODEL), wmap),
            ],
            out_specs=pl.BlockSpec((TM_MOE * n_slab, LANE), tmap),
            scratch_shapes=[pltpu.VMEM((TM_MOE, D_MODEL), F32)],
        ),
        out_shape=jax.ShapeDtypeStruct((T * n_slab, LANE), F32),
        compiler_params=pltpu.CompilerParams(
            dimension_semantics=("arbitrary", "arbitrary"), vmem_limit_bytes=VMEM_LIMIT),
        name="moe_grouped",
    )(sched, hs, ms, w_up, w_gate, w_down)


def _final_kernel(pos_ref, ys_ref, x1_ref, mod_ref, fn_ref, yp_ref, ysm_ref, g_ref):
    i = pl.program_id(0)
    base = i * TM_FINAL

    def body(r, carry):
        g_ref[_slab(r), :] = ys_ref[_slab(pos_ref[base + r]), :]
        return carry

    lax.fori_loop(0, TM_FINAL, body, 0, unroll=8)
    mod = mod_ref[pl.ds(_mod_row(i, TM_FINAL), 1), :]
    gate2 = mod[:, 5 * D_MODEL:6 * D_MODEL]
    moe = jnp.concatenate([g_ref[pl.ds(c, TM_FINAL, stride=N_SLAB), :] for c in range(N_SLAB)], axis=1)
    x2 = x1_ref[...] + gate2 * moe
    y = (x2 * _rms(x2)) * fn_ref[...]
    is_sample = i >= T_P // TM_FINAL

    @pl.when(jnp.logical_not(is_sample))
    def _():
        yp_ref[...] = y

    @pl.when(is_sample)
    def _():
        ysm_ref[...] = y


def _final(pos, ys, x1, mod, final_norm):
    n_slab = D_MODEL // LANE
    pmap, smap = _stream_maps(TM_FINAL)
    return pl.pallas_call(
        _final_kernel,
        grid_spec=pltpu.PrefetchScalarGridSpec(
            num_scalar_prefetch=1,
            grid=(T // TM_FINAL,),
            in_specs=[
                pl.BlockSpec((T * n_slab, LANE), lambda i, pos: (0, 0), pipeline_mode=pl.Buffered(1)),
                pl.BlockSpec((TM_FINAL, D_MODEL), lambda i, pos: (i, 0)),
                pl.BlockSpec((N_COND, 6 * D_MODEL), lambda i, pos: (0, 0)),
                pl.BlockSpec((1, D_MODEL), lambda i, pos: (0, 0)),
            ],
            out_specs=[pl.BlockSpec((TM_FINAL, D_MODEL), pmap),
                       pl.BlockSpec((TM_FINAL, D_MODEL), smap)],
            scratch_shapes=[pltpu.VMEM((TM_FINAL * n_slab, LANE), F32)],
        ),
        out_shape=[jax.ShapeDtypeStruct((T_P, D_MODEL), F32),
                   jax.ShapeDtypeStruct((T_S, D_MODEL), F32)],
        compiler_params=pltpu.CompilerParams(
            dimension_semantics=("arbitrary",), vmem_limit_bytes=VMEM_LIMIT),
        name="moe_unsort_final",
    )(pos, ys, x1, mod, final_norm)


def _plan_kernel(gr_ref, cnt_ref, pos_ref, sched_ref):
    n_post = T // TM_POST
    lane = lax.broadcasted_iota(jnp.int32, (1, LANE), 1)
    grp_lane = lambda v, g: v[:, N_EXPERTS + g:N_EXPERTS + g + 1]
    counts = [cnt_ref[k, 0:1, :] for k in range(n_post)]
    gtot_v = counts[0]
    for k in range(1, n_post):
        gtot_v = gtot_v + counts[k]
    gtot = [grp_lane(gtot_v, g) for g in range(N_GROUPS)]
    goff = [jnp.zeros((1, 1), F32)]
    for g in range(1, N_GROUPS):
        goff.append(goff[-1] + gtot[g - 1])

    before = [jnp.zeros((1, 1), F32) for _ in range(N_GROUPS)]
    rows_per_tile = TM_POST // LANE
    for k in range(n_post):
        gid = gr_ref[0:1, TM_POST * k:TM_POST * (k + 1)]
        pos = gr_ref[1:2, TM_POST * k:TM_POST * (k + 1)]
        for g in range(N_GROUPS):
            pos = pos + jnp.where(gid == float(g), goff[g] + before[g], 0.0)
            before[g] = before[g] + grp_lane(counts[k], g)
        for r in range(rows_per_tile):
            pos_ref[rows_per_tile * k + r:rows_per_tile * k + r + 1, :] = (
                pos[:, LANE * r:LANE * (r + 1)].astype(jnp.int32))

    zero = jnp.zeros((1, LANE), F32)
    rows = {name: zero for name in ("vt", "vg", "vlo", "vhi")}
    slot = jnp.zeros((1, 1), F32)
    last_t = jnp.zeros((1, 1), F32)
    last_g = jnp.zeros((1, 1), F32)
    for i in range(T // TM_MOE):
        for g in range(N_GROUPS):
            lo = jnp.clip(goff[g] - float(TM_MOE * i), 0.0, float(TM_MOE))
            hi = jnp.clip(goff[g] + gtot[g] - float(TM_MOE * i), 0.0, float(TM_MOE))
            ok = hi > lo
            here = ok & (lane == slot.astype(jnp.int32))
            rows["vt"] = jnp.where(here, float(i), rows["vt"])
            rows["vg"] = jnp.where(here, float(g), rows["vg"])
            rows["vlo"] = jnp.where(here, lo, rows["vlo"])
            rows["vhi"] = jnp.where(here, hi, rows["vhi"])
            last_t = jnp.where(ok, float(i), last_t)
            last_g = jnp.where(ok, float(g), last_g)
            slot = slot + jnp.where(ok, 1.0, 0.0)
    n_ok = slot.astype(jnp.int32)
    valid = lane < n_ok
    vt = jnp.where(valid, rows["vt"], last_t)
    vg = jnp.where(valid, rows["vg"], last_g)
    prev_t = pltpu.roll(jnp.broadcast_to(vt, (8, LANE)), 1, 1)[0:1, :]
    next_t = pltpu.roll(jnp.broadcast_to(vt, (8, LANE)), LANE - 1, 1)[0:1, :]
    first = jnp.where((lane == 0) | (vt != prev_t), 1.0, 0.0)
    last = jnp.where((lane == n_ok - 1) | (vt != next_t), 1.0, 0.0)
    table = [vt, vg, rows["vlo"], rows["vhi"], first, last, jnp.where(valid, 1.0, 0.0), zero]
    for r, row in enumerate(table):
        sched_ref[r:r + 1, :] = row.astype(jnp.int32)


V_TILE, V_GROUP, V_LO, V_HI, V_FIRST, V_LAST, V_VALID = range(7)


def _plan(gr, cnt):
    pos, sched = pl.pallas_call(
        _plan_kernel,
        out_shape=[jax.ShapeDtypeStruct((T // LANE, LANE), jnp.int32),
                   jax.ShapeDtypeStruct((8, LANE), jnp.int32)],
        name="moe_plan",
    )(gr, cnt)
    return pos.reshape(T), sched


def _rope_tables():
    n = np.arange(DEC_SEQ)
    pos = np.stack([n // GRID_W, n % GRID_W], axis=1).astype(np.float32)
    half = ROPE_AXIS // 2
    inv = (1.0 / (ROPE_BASE ** (np.arange(0, ROPE_AXIS, 2, dtype=np.float32) / ROPE_AXIS))).astype(np.float32)
    ang = (pos[:, :, None] * inv[None, None, :]).astype(np.float32)
    cos = np.cos(ang).astype(np.float32)
    sin = np.sin(ang).astype(np.float32)
    tabs = np.zeros((2, 3, DEC_SEQ, LANE), np.float32)
    tabs[:, 0] = 1.0
    for a in range(2):
        lo = ROPE_LANE0 + a * ROPE_AXIS
        tabs[1, 0, :, lo:lo + half] = cos[:, a]
        tabs[1, 0, :, lo + half:lo + 2 * half] = cos[:, a]
        tabs[1, 1, :, lo + half:lo + 2 * half] = sin[:, a]
        tabs[1, 2, :, lo:lo + half] = -sin[:, a]
    return jnp.asarray(tabs)


def kernel(x_prompt, x_sample, cache_ckv, cache_krope, c, c_ctx, norm1, w_ada, b_ada, w_in, conv_w,
           w_conv_out, q_norm, w_uq, kv_norm, w_ukv, w_o, w_mix_out, norm2, w_grp, w_exp, w_up,
           w_gate, w_down, final_norm):
    l = 0
    xp = x_prompt.reshape(T_P, D_MODEL)
    xs = x_sample.reshape(T_S, D_MODEL)
    mod = _ada(c_ctx[None, :], c, w_ada[l], b_ada[l][None, :])

    w_in_t = w_in[l].T
    w_uq_slot = jnp.pad(w_uq[l].reshape(Q_LORA, N_HEADS, QK_NOPE + QK_ROPE),
                        ((0, 0), (0, 0), (0, LANE - QK_NOPE - QK_ROPE))
                        ).reshape(Q_LORA, N_HEADS * LANE).astype(BF16)
    wkv = w_ukv[l].reshape(KV_LORA, N_HEADS, QK_NOPE + V_HEAD)
    wk_slot = jnp.pad(wkv[:, :, :QK_NOPE], ((0, 0), (0, 0), (0, LANE - QK_NOPE)))
    wv = wkv[:, :, QK_NOPE:].reshape(KV_LORA, N_HEADS // 2, 2, V_HEAD)
    zero = jnp.zeros_like(wv[:, :, 0])
    wv_slot = jnp.stack([jnp.concatenate([wv[:, :, 0], zero], axis=-1),
                         jnp.concatenate([zero, wv[:, :, 1]], axis=-1)], axis=2)
    w_ukv_slot = jnp.concatenate([wk_slot.reshape(KV_LORA, N_HEADS * LANE),
                                  wv_slot.reshape(KV_LORA, N_HEADS * LANE)], axis=1).astype(BF16)
    w_route = jnp.pad(jnp.concatenate([w_exp[l], w_grp[l]], axis=1),
                      ((0, 0), (0, LANE - N_EXPERTS - N_GROUPS)))
    w_route_hi = w_route.astype(BF16)
    w_route_lo = (w_route - w_route_hi.astype(F32)).astype(BF16)
    w_route2 = jnp.concatenate([w_route_hi, w_route_lo], axis=1)
    cache_krs = jnp.pad(cache_krope[:, l], ((0, 0), (0, 0), (ROPE_LANE0, LANE - ROPE_LANE0 - QK_ROPE)))

    zc, q, ckv, krs, nckv, nkr = _inproj(xp, xs, mod, norm1[l][None, :], w_in_t, conv_w[l],
                              q_norm[l][None, :], kv_norm[l][None, :], w_uq_slot, _rope_tables())
    o_p, o_s, w_gates_t, w_co_b, w_o_b, w_mix_b = _attention(
        q, ckv, krs, cache_ckv[:, l], cache_krs, w_ukv_slot, w_in_t, w_conv_out[l], w_o[l], w_mix_out[l])
    x1, h3, meta, cnt, gr = _post(xp, xs, mod, norm1[l][None, :], w_gates_t, zc, o_p, o_s, w_co_b, w_o_b,
                                  w_mix_b, norm2[l][None, :], w_route2)
    pos, sched = _plan(gr, cnt)
    hs, ms = _dispatch(pos, h3, meta)
    ys = _moe(sched, hs, ms, w_up[l], w_gate[l], w_down[l])
    yp, ysm = _final(pos, ys, x1, mod, final_norm[None, :])

    y_prompt = yp.reshape(BATCH, SEQ, D_MODEL)
    y_sample = ysm.reshape(DEC_BATCH, DEC_SEQ, D_MODEL)
    new_ckv = nckv.reshape(BATCH, 1, SEQ, KV_LORA)
    new_krope = jnp.swapaxes(nkr, 1, 2).reshape(BATCH, 1, SEQ, QK_ROPE)
    return (y_prompt, y_sample, new_ckv, new_krope)
```

```python
import functools

import numpy as np
import jax
import jax.numpy as jnp
from jax import lax
from jax.experimental import pallas as pl
from jax.experimental.pallas import tpu as pltpu

F32 = jnp.float32
BF16 = jnp.bfloat16

D_MODEL = 1024
BATCH = 16
SEQ = 256
DEC_BATCH = 2
DEC_SEQ = 1024
PAST_LEN = 256
GRID_W = 64
N_HEADS = 8
QK_NOPE = 64
QK_ROPE = 32
V_HEAD = 64
Q_LORA = 256
KV_LORA = 128
ROPE_AXIS = QK_ROPE // 2
ROPE_BASE = 10000.0
ATTN_SCALE = (QK_NOPE + QK_ROPE) ** -0.5
D_CONV = D_MODEL
N_GROUPS = 4
EXP_PER_GROUP = 8
N_EXPERTS = N_GROUPS * EXP_PER_GROUP
D_EXPERT = 256
EPS = 1e-6

T_P = BATCH * SEQ
T_S = DEC_BATCH * DEC_SEQ
T = T_P + T_S
N_COND = 8
LANE = 128
ROPE_LANE0 = QK_NOPE
SMALL_COLS = Q_LORA + KV_LORA + LANE
VMEM_LIMIT = 56 * 1024 * 1024

TM_IN = 1024
TM_POST = 512
TM_MOE = T // N_GROUPS
TM_FINAL = 512
TM_DISP = 1024
N_SLAB = D_MODEL // LANE
MOE_SUB = 256
MOE_EPS = 2
N_VISITS = T // TM_MOE + N_GROUPS - 1
GID_LANE = 40
RANK_LANE = 41
Q_BLK = 256
Q_BLK_S = 512
PROMPT_SEQS = 2
CONV_CHUNK = 256


def _dot(a, b):
    return jnp.dot(a, b, preferred_element_type=F32)


def _rms(x):
    return lax.rsqrt(jnp.mean(x * x, axis=-1, keepdims=True) + EPS)


def _slab(t):
    return pl.ds(pl.multiple_of(t * N_SLAB, N_SLAB), N_SLAB)


def _mod_row(i, tm):
    n_prompt = T_P // tm
    return jnp.where(i >= n_prompt, 1 + ((i - n_prompt) * tm) // DEC_SEQ, 0)


def _ada_kernel(cctx_ref, c_ref, w_ref, b_ref, o_ref):
    c = jnp.concatenate([cctx_ref[...], c_ref[...],
                         jnp.zeros((N_COND - 1 - DEC_BATCH, D_MODEL), F32)], axis=0)
    a = (c * jax.nn.sigmoid(c)).astype(BF16)
    o_ref[...] = _dot(a, w_ref[...].astype(BF16)) + b_ref[...]


def _ada(c_ctx, c, w_ada, b_ada):
    n = 6 * D_MODEL
    bn = 1536
    return pl.pallas_call(
        _ada_kernel,
        grid=(n // bn,),
        in_specs=[
            pl.BlockSpec((1, D_MODEL), lambda j: (0, 0)),
            pl.BlockSpec((DEC_BATCH, D_MODEL), lambda j: (0, 0)),
            pl.BlockSpec((D_MODEL, bn), lambda j: (0, j)),
            pl.BlockSpec((1, bn), lambda j: (0, j)),
        ],
        out_specs=pl.BlockSpec((N_COND, bn), lambda j: (0, j)),
        out_shape=jax.ShapeDtypeStruct((N_COND, n), F32),
        compiler_params=pltpu.CompilerParams(
            dimension_semantics=("parallel",), vmem_limit_bytes=VMEM_LIMIT),
        name="ada_mod",
    )(c_ctx, c, w_ada, b_ada)


O_CQ = 3 * D_CONV
O_KR = O_CQ + Q_LORA + KV_LORA
O_GATE = O_KR + QK_ROPE
PREP_STEPS = 4


IN_COLS = O_GATE + 2 * D_MODEL
TAIL_ROWS = IN_COLS // 2
SMALL_BLK = 512
NT = (((1,), (1,)), ((), ()))


def _dot_nt(a, bt):
    return lax.dot_general(a, bt, NT, preferred_element_type=F32)


def _stream_maps(tm):
    n_prompt = T_P // tm
    return (lambda i, *_: (jnp.minimum(i, n_prompt - 1), 0),
            lambda i, *_: (jnp.maximum(i - n_prompt, 0), 0))


def _inproj_kernel(xp_ref, xs_ref, mod_ref, n1_ref, wc_ref, ws_ref, cw_ref, qn_ref, kvn_ref, wuq_ref,
                   rope_ref, zc_ref, q_ref, ckv_ref, krs_ref, nckv_ref, nkr_ref):
    i = pl.program_id(0)
    is_sample = i >= T_P // TM_IN
    seq = jnp.where(is_sample, DEC_SEQ, SEQ)
    mod = mod_ref[pl.ds(_mod_row(i, TM_IN), 1), :]
    shift1 = mod[:, 0:D_MODEL]
    scale1 = mod[:, D_MODEL:2 * D_MODEL]
    x = jnp.where(is_sample, xs_ref[...], xp_ref[...])
    h = ((x * _rms(x)) * n1_ref[...]) * (1.0 + scale1) + shift1
    hb = h.astype(BF16)

    n_lat = Q_LORA + KV_LORA
    w_small = jnp.concatenate([
        ws_ref[0:n_lat, :].astype(BF16),
        jnp.zeros((ROPE_LANE0, D_MODEL), BF16),
        ws_ref[n_lat:n_lat + QK_ROPE, :].astype(BF16),
        jnp.zeros((LANE - ROPE_LANE0 - QK_ROPE, D_MODEL), BF16)], axis=0)
    sm = _dot_nt(hb, w_small)
    cq = sm[:, 0:Q_LORA]
    ckv_raw = sm[:, Q_LORA:Q_LORA + KV_LORA]
    krs = sm[:, Q_LORA + KV_LORA:SMALL_COLS]
    cqn = (cq * _rms(cq)) * qn_ref[...]
    q = _dot(cqn.astype(BF16), wuq_ref[...])
    ckv = (ckv_raw * _rms(ckv_raw)) * kvn_ref[...]
    ckv_ref[...] = ckv

    @pl.when(jnp.logical_not(is_sample))
    def _():
        nckv_ref[...] = ckv
        for s in range(TM_IN // SEQ):
            kt = krs[s * SEQ:(s + 1) * SEQ, :].T
            nkr_ref[s] = kt[ROPE_LANE0:ROPE_LANE0 + QK_ROPE, :]

    cos = rope_ref[0]
    sin_lo = rope_ref[1]
    sin_hi = rope_ref[2]

    def rot(v):
        return v * cos + pltpu.roll(v, 8, 1) * sin_lo + pltpu.roll(v, LANE - 8, 1) * sin_hi

    krs_ref[...] = rot(krs)
    for hh in range(N_HEADS):
        q_ref[:, LANE * hh:LANE * (hh + 1)] = rot(q[:, LANE * hh:LANE * (hh + 1)]).astype(BF16)

    pos = lax.broadcasted_iota(jnp.int32, (TM_IN, 1), 0) & (seq - 1)
    first = pos == 0
    last = pos == seq - 1
    for j in range(D_CONV // CONV_CHUNK):
        c0 = j * CONV_CHUNK
        bg = _dot_nt(hb, wc_ref[c0:c0 + CONV_CHUNK, :].astype(BF16))
        cg = _dot_nt(hb, wc_ref[D_CONV + c0:D_CONV + c0 + CONV_CHUNK, :].astype(BF16))
        ui = _dot_nt(hb, wc_ref[2 * D_CONV + c0:2 * D_CONV + c0 + CONV_CHUNK, :].astype(BF16))
        u = cg * ui
        u_prev = jnp.where(first, 0.0, pltpu.roll(u, 1, 0))
        u_next = jnp.where(last, 0.0, pltpu.roll(u, TM_IN - 1, 0))
        cw = cw_ref[:, c0:c0 + CONV_CHUNK]
        conv = u_prev * cw[0:1] + u * cw[1:2] + u_next * cw[2:3]
        zc_ref[:, c0:c0 + CONV_CHUNK] = (bg * conv).astype(BF16)


def _inproj(xp, xs, mod, norm1, w_in_t, conv_w, q_norm, kv_norm, w_uq_slot, rope_tabs):
    n_prompt = T_P // TM_IN
    const = lambda i: (0, 0)
    pmap, smap = _stream_maps(TM_IN)
    once = pl.Buffered(1)
    return pl.pallas_call(
        _inproj_kernel,
        grid=(T // TM_IN,),
        in_specs=[
            pl.BlockSpec((TM_IN, D_MODEL), pmap),
            pl.BlockSpec((TM_IN, D_MODEL), smap),
            pl.BlockSpec((N_COND, 6 * D_MODEL), const),
            pl.BlockSpec((1, D_MODEL), const),
            pl.BlockSpec((O_CQ, D_MODEL), const, pipeline_mode=once),
            pl.BlockSpec((SMALL_BLK, D_MODEL), lambda i: (O_CQ // SMALL_BLK, 0), pipeline_mode=once),
            pl.BlockSpec((3, D_CONV), const),
            pl.BlockSpec((1, Q_LORA), const),
            pl.BlockSpec((1, KV_LORA), const),
            pl.BlockSpec((Q_LORA, N_HEADS * LANE), const),
            pl.BlockSpec((None, 3, TM_IN, LANE),
                         lambda i: (jnp.where(i >= n_prompt, 1, 0), 0, 0, 0)),
        ],
        out_specs=[
            pl.BlockSpec((TM_IN, D_CONV), lambda i: (i, 0)),
            pl.BlockSpec((TM_IN, N_HEADS * LANE), lambda i: (i, 0)),
            pl.BlockSpec((TM_IN, KV_LORA), lambda i: (i, 0)),
            pl.BlockSpec((TM_IN, LANE), lambda i: (i, 0)),
            pl.BlockSpec((TM_IN, KV_LORA), pmap),
            pl.BlockSpec((TM_IN // SEQ, QK_ROPE, SEQ), lambda i: (jnp.minimum(i, n_prompt - 1), 0, 0)),
        ],
        out_shape=[
            jax.ShapeDtypeStruct((T, D_CONV), BF16),
            jax.ShapeDtypeStruct((T, N_HEADS * LANE), BF16),
            jax.ShapeDtypeStruct((T, KV_LORA), F32),
            jax.ShapeDtypeStruct((T, LANE), F32),
            jax.ShapeDtypeStruct((T_P, KV_LORA), F32),
            jax.ShapeDtypeStruct((BATCH, QK_ROPE, SEQ), F32),
        ],
        compiler_params=pltpu.CompilerParams(
            dimension_semantics=("arbitrary",), vmem_limit_bytes=VMEM_LIMIT),
        name="in_proj",
    )(xp, xs, mod, norm1, w_in_t, w_in_t, conv_w, q_norm, kv_norm, w_uq_slot, rope_tabs)


def _fill_kv(ckv, krs, wukv_ref, kf_scr, v_scr, off):
    m = ckv.shape[0]
    kv = _dot(ckv.astype(BF16), wukv_ref[...])
    for hh in range(N_HEADS):
        kf_scr[hh, off:off + m, :] = (kv[:, LANE * hh:LANE * (hh + 1)] + krs).astype(BF16)
    v_scr[off:off + m, :] = kv[:, N_HEADS * LANE:].astype(BF16)


def _attend(q_ref, r0, rows, kf_scr, v_scr, o_ref):
    for pair in range(N_HEADS // 2):
        acc = None
        for hh in (2 * pair, 2 * pair + 1):
            qh = q_ref[r0:r0 + rows, LANE * hh:LANE * (hh + 1)]
            s = _dot_nt(qh, kf_scr[hh]) * ATTN_SCALE
            e = jnp.exp(s - jnp.max(s, axis=-1, keepdims=True))
            p = (e / jnp.sum(e, axis=-1, keepdims=True)).astype(BF16)
            part = _dot(p, v_scr[:, LANE * hh:LANE * (hh + 1)])
            acc = part if acc is None else acc + part
        o_ref[r0:r0 + rows, LANE * pair:LANE * (pair + 1)] = acc.astype(BF16)


def _attn_prompt_kernel(q_ref, ckv_ref, krs_ref, wukv_ref, o_ref):
    for s in range(PROMPT_SEQS):
        r0 = s * SEQ
        kv = _dot(ckv_ref[r0:r0 + SEQ, :].astype(BF16), wukv_ref[...])
        krs = krs_ref[r0:r0 + SEQ, :]
        for pair in range(N_HEADS // 2):
            acc = None
            for hh in (2 * pair, 2 * pair + 1):
                kf = (kv[:, LANE * hh:LANE * (hh + 1)] + krs).astype(BF16)
                vh = kv[:, LANE * (N_HEADS + hh):LANE * (N_HEADS + hh + 1)].astype(BF16)
                sc = _dot_nt(q_ref[r0:r0 + SEQ, LANE * hh:LANE * (hh + 1)], kf) * ATTN_SCALE
                e = jnp.exp(sc - jnp.max(sc, axis=-1, keepdims=True))
                p = (e / jnp.sum(e, axis=-1, keepdims=True)).astype(BF16)
                part = _dot(p, vh)
                acc = part if acc is None else acc + part
            o_ref[r0:r0 + SEQ, LANE * pair:LANE * (pair + 1)] = acc.astype(BF16)


def _attn_sample_kernel(q_ref, ckv_ref, cckv_ref, krs_ref, ckrs_ref, wukv_ref, wg_ref, wco_ref, wo_ref,
                        wmix_ref, o_ref, g_ref, co_ref, ob_ref, mix_ref, kf_scr, v_scr):
    @pl.when(pl.program_id(1) == 0)
    def _():
        _fill_kv(ckv_ref[...], krs_ref[...], wukv_ref, kf_scr, v_scr, 0)
        _fill_kv(cckv_ref[...], ckrs_ref[...], wukv_ref, kf_scr, v_scr, DEC_SEQ)

    _attend(q_ref, 0, Q_BLK_S, kf_scr, v_scr, o_ref)
    g_ref[...] = wg_ref[...].astype(BF16)
    co_ref[...] = wco_ref[...].astype(BF16)
    ob_ref[...] = wo_ref[...].astype(BF16)
    mix_ref[...] = wmix_ref[...].astype(BF16)


def _attention(q, ckv, krs, cache_ckv, cache_krs, w_ukv_slot, w_in_t, w_conv_out, w_o, w_mix_out):
    kv_cols = 2 * N_HEADS * LANE
    n_o = N_HEADS * V_HEAD
    steps = BATCH // PROMPT_SEQS
    rows = PROMPT_SEQS * SEQ
    o_prompt = pl.pallas_call(
        _attn_prompt_kernel,
        grid=(steps,),
        in_specs=[
            pl.BlockSpec((rows, N_HEADS * LANE), lambda b: (b, 0)),
            pl.BlockSpec((rows, KV_LORA), lambda b: (b, 0)),
            pl.BlockSpec((rows, LANE), lambda b: (b, 0)),
            pl.BlockSpec((KV_LORA, kv_cols), lambda b: (0, 0)),
        ],
        out_specs=pl.BlockSpec((rows, n_o), lambda b: (b, 0)),
        out_shape=jax.ShapeDtypeStruct((T_P, n_o), BF16),
        compiler_params=pltpu.CompilerParams(dimension_semantics=("parallel",),
                                             vmem_limit_bytes=VMEM_LIMIT),
        name="attn_prompt",
    )(q, ckv, krs, w_ukv_slot)

    m_all = DEC_SEQ + PAST_LEN
    nq = DEC_SEQ // Q_BLK_S
    q0 = T_P // Q_BLK_S
    s0 = T_P // DEC_SEQ
    n_step = DEC_BATCH * nq
    share = lambda n: pl.BlockSpec((n // n_step, D_MODEL), lambda b, j: (b * nq + j, 0))
    gate_rows = 2 * D_MODEL // n_step
    o_sample, w_gates_t, w_co_b, w_o_b, w_mix_b = pl.pallas_call(
        _attn_sample_kernel,
        grid=(DEC_BATCH, nq),
        in_specs=[
            pl.BlockSpec((Q_BLK_S, N_HEADS * LANE), lambda b, j: (q0 + b * nq + j, 0)),
            pl.BlockSpec((DEC_SEQ, KV_LORA), lambda b, j: (s0 + b, 0)),
            pl.BlockSpec((None, PAST_LEN, KV_LORA), lambda b, j: (b, 0, 0)),
            pl.BlockSpec((DEC_SEQ, LANE), lambda b, j: (s0 + b, 0)),
            pl.BlockSpec((None, PAST_LEN, LANE), lambda b, j: (b, 0, 0)),
            pl.BlockSpec((KV_LORA, kv_cols), lambda b, j: (0, 0)),
            pl.BlockSpec((pl.Element(gate_rows), pl.Element(D_MODEL)),
                         lambda b, j: (pl.multiple_of(O_GATE + (b * nq + j) * gate_rows, 32), 0)),
            share(D_CONV), share(n_o), share(D_MODEL),
        ],
        out_specs=[pl.BlockSpec((Q_BLK_S, n_o), lambda b, j: (b * nq + j, 0)),
                   share(2 * D_MODEL), share(D_CONV), share(n_o), share(D_MODEL)],
        out_shape=[jax.ShapeDtypeStruct((T_S, n_o), BF16),
                   jax.ShapeDtypeStruct((2 * D_MODEL, D_MODEL), BF16),
                   jax.ShapeDtypeStruct((D_CONV, D_MODEL), BF16),
                   jax.ShapeDtypeStruct((n_o, D_MODEL), BF16),
                   jax.ShapeDtypeStruct((D_MODEL, D_MODEL), BF16)],
        scratch_shapes=[pltpu.VMEM((N_HEADS, m_all, LANE), BF16),
                        pltpu.VMEM((m_all, N_HEADS * LANE), BF16)],
        compiler_params=pltpu.CompilerParams(dimension_semantics=("arbitrary", "arbitrary"),
                                             vmem_limit_bytes=VMEM_LIMIT),
        name="attn_sample",
    )(q, ckv, cache_ckv, krs, cache_krs, w_ukv_slot, w_in_t, w_conv_out, w_o, w_mix_out)
    return o_prompt, o_sample, w_gates_t, w_co_b, w_o_b, w_mix_b


def _route(logits):
    lane = lax.broadcasted_iota(jnp.int32, logits.shape, 1)
    neg = -jnp.inf
    big = jnp.int32(1 << 20)
    gmask = (lane >= N_EXPERTS) & (lane < N_EXPERTS + N_GROUPS)
    gl = jnp.where(gmask, logits, neg)
    gmax = jnp.max(gl, axis=-1, keepdims=True)
    gsum = jnp.sum(jnp.where(gmask, jnp.exp(gl - gmax), 0.0), axis=-1, keepdims=True)
    p_g = 1.0 / gsum
    g_idx = jnp.min(jnp.where(gl == gmax, lane, big), axis=-1, keepdims=True) - N_EXPERTS

    emask = (lane < N_EXPERTS) & ((lane >> 3) == g_idx)
    el = jnp.where(emask, logits, neg)
    m1 = jnp.max(el, axis=-1, keepdims=True)
    i1 = jnp.min(jnp.where(el == m1, lane, big), axis=-1, keepdims=True)
    el2 = jnp.where(lane == i1, neg, el)
    m2 = jnp.max(el2, axis=-1, keepdims=True)
    i2 = jnp.min(jnp.where(el2 == m2, lane, big), axis=-1, keepdims=True)
    z = jnp.sum(jnp.where(emask, jnp.exp(el - m1), 0.0), axis=-1, keepdims=True)
    p1 = 1.0 / z
    p2 = jnp.exp(m2 - m1) / z
    tot = p1 + p2
    w1 = p_g * p1 / tot
    w2 = p_g * p2 / tot
    return jnp.where(lane == i1, w1, 0.0) + jnp.where(lane == i2, w2, 0.0), g_idx


def _post_kernel(xp_ref, xs_ref, mod_ref, n1_ref, wg_ref, zc_ref, op_ref, os_ref, wco_ref, wo_ref,
                 wmix_ref, n2_ref, wr_ref, x1_ref, h3_ref, meta_ref, cnt_ref, gr_ref):
    i = pl.program_id(0)
    is_sample = i >= T_P // TM_POST
    x = jnp.where(is_sample, xs_ref[...], xp_ref[...])
    o = jnp.where(is_sample, os_ref[...], op_ref[...])
    mod = mod_ref[pl.ds(_mod_row(i, TM_POST), 1), :]
    shift1 = mod[:, 0:D_MODEL]
    scale1 = mod[:, D_MODEL:2 * D_MODEL]
    gate1 = mod[:, 2 * D_MODEL:3 * D_MODEL]
    shift2 = mod[:, 3 * D_MODEL:4 * D_MODEL]
    scale2 = mod[:, 4 * D_MODEL:5 * D_MODEL]
    y_conv = _dot(zc_ref[...], wco_ref[...])
    y_mla = _dot(o, wo_ref[...])
    h = ((x * _rms(x)) * n1_ref[...]) * (1.0 + scale1) + shift1
    g = _dot_nt(h.astype(BF16), wg_ref[...])
    merged = (jax.nn.sigmoid(g[:, 0:D_MODEL]) * y_conv
              + jax.nn.sigmoid(g[:, D_MODEL:2 * D_MODEL]) * y_mla)
    y = _dot(merged.astype(BF16), wmix_ref[...])
    x1 = x + gate1 * y
    x1_ref[...] = x1
    h2 = ((x1 * _rms(x1)) * n2_ref[...]) * (1.0 + scale2) + shift2
    h2_hi = h2.astype(BF16)
    h2_lo = (h2 - h2_hi.astype(F32)).astype(BF16)
    hh = _dot(h2_hi, wr_ref[...])
    logits = hh[:, 0:LANE] + hh[:, LANE:2 * LANE] + _dot(h2_lo, wr_ref[:, 0:LANE])
    comb, g_idx = _route(logits)

    lane = lax.broadcasted_iota(jnp.int32, comb.shape, 1)
    onehot = lane == g_idx + N_EXPERTS
    r_i = lax.broadcasted_iota(jnp.int32, (TM_POST, TM_POST), 0)
    c_i = lax.broadcasted_iota(jnp.int32, (TM_POST, TM_POST), 1)
    lower = jnp.where(c_i < r_i, 1.0, 0.0).astype(BF16)
    before = _dot(lower, jnp.where(onehot, 1.0, 0.0).astype(BF16))
    rank = jnp.sum(jnp.where(onehot, before, 0.0), axis=-1, keepdims=True)
    counts = jnp.sum(jnp.where(onehot, 1.0, 0.0), axis=0, keepdims=True)
    cnt_ref[...] = jnp.broadcast_to(counts, cnt_ref.shape)

    meta_ref[...] = comb
    idx = jnp.where(lane == GID_LANE, g_idx.astype(F32), 0.0) + jnp.where(lane == RANK_LANE, rank, 0.0)
    idx_hi = idx.astype(BF16)
    idx_lo = (idx - idx_hi.astype(F32)).astype(BF16)
    s_row = lax.broadcasted_iota(jnp.int32, (8, LANE), 0)
    s_lane = lax.broadcasted_iota(jnp.int32, (8, LANE), 1)
    sel = jnp.where(s_lane == GID_LANE + s_row, 1.0, 0.0).astype(BF16)
    gr_ref[...] = _dot_nt(sel, idx_hi) + _dot_nt(sel, idx_lo)
    for c in range(N_SLAB):
        h3_ref[pl.ds(c, TM_POST, stride=N_SLAB), :] = h2[:, LANE * c:LANE * (c + 1)]


def _post(xp, xs, mod, norm1, w_gates_t, zc, o_p, o_s, w_conv_out, w_o, w_mix_out, norm2, w_route):
    const = lambda i: (0, 0)
    row = lambda i: (i, 0)
    pmap, smap = _stream_maps(TM_POST)
    return pl.pallas_call(
        _post_kernel,
        grid=(T // TM_POST,),
        in_specs=[
            pl.BlockSpec((TM_POST, D_MODEL), pmap),
            pl.BlockSpec((TM_POST, D_MODEL), smap),
            pl.BlockSpec((N_COND, 6 * D_MODEL), const),
            pl.BlockSpec((1, D_MODEL), const),
            pl.BlockSpec((2 * D_MODEL, D_MODEL), const),
            pl.BlockSpec((TM_POST, D_CONV), row),
            pl.BlockSpec((TM_POST, N_HEADS * V_HEAD), pmap),
            pl.BlockSpec((TM_POST, N_HEADS * V_HEAD), smap),
            pl.BlockSpec((D_CONV, D_MODEL), const),
            pl.BlockSpec((N_HEADS * V_HEAD, D_MODEL), const),
            pl.BlockSpec((D_MODEL, D_MODEL), const),
            pl.BlockSpec((1, D_MODEL), const),
            pl.BlockSpec((D_MODEL, 2 * LANE), const),
        ],
        out_specs=[
            pl.BlockSpec((TM_POST, D_MODEL), row),
            pl.BlockSpec((TM_POST * N_SLAB, LANE), row),
            pl.BlockSpec((TM_POST, LANE), row),
            pl.BlockSpec((None, 8, LANE), lambda i: (i, 0, 0)),
            pl.BlockSpec((8, TM_POST), lambda i: (0, i)),
        ],
        out_shape=[
            jax.ShapeDtypeStruct((T, D_MODEL), F32),
            jax.ShapeDtypeStruct((T * N_SLAB, LANE), F32),
            jax.ShapeDtypeStruct((T, LANE), F32),
            jax.ShapeDtypeStruct((T // TM_POST, 8, LANE), F32),
            jax.ShapeDtypeStruct((8, T), F32),
        ],
        compiler_params=pltpu.CompilerParams(
            dimension_semantics=("parallel",), vmem_limit_bytes=VMEM_LIMIT),
        name="post_mixer",
    )(xp, xs, mod, norm1, w_gates_t, zc, o_p, o_s, w_conv_out, w_o, w_mix_out, norm2, w_route)


DISP_IN = T // TM_DISP
DISP_OUT = T // TM_MOE


def _dispatch_kernel(pos_ref, h3_ref, m_ref, hs_ref, ms_ref, xs_ref, mss_ref):
    i = pl.program_id(0)

    @pl.when(i < DISP_IN)
    def _():
        base = i * TM_DISP

        def body(r, carry):
            p = pos_ref[base + r]
            xs_ref[_slab(p), :] = h3_ref[_slab(r), :]
            mss_ref[pl.ds(p, 1), :] = m_ref[pl.ds(r, 1), :]
            return carry

        lax.fori_loop(0, TM_DISP, body, 0, unroll=8)

    @pl.when(i >= DISP_IN)
    def _():
        row0 = pl.multiple_of((i - DISP_IN) * TM_MOE, TM_MOE)
        for c in range(N_SLAB):
            hs_ref[:, LANE * c:LANE * (c + 1)] = (
                xs_ref[pl.ds(row0 * N_SLAB + c, TM_MOE, stride=N_SLAB), :].astype(BF16))
        ms_ref[...] = mss_ref[pl.ds(row0, TM_MOE), :]


def _dispatch(pos, h3, meta):
    n_slab = N_SLAB
    in_map = lambda i, pos: (jnp.minimum(i, DISP_IN - 1), 0)
    out_map = lambda i, pos: (jnp.maximum(i - DISP_IN, 0), 0)
    return pl.pallas_call(
        _dispatch_kernel,
        grid_spec=pltpu.PrefetchScalarGridSpec(
            num_scalar_prefetch=1,
            grid=(DISP_IN + DISP_OUT,),
            in_specs=[pl.BlockSpec((TM_DISP * n_slab, LANE), in_map),
                      pl.BlockSpec((TM_DISP, LANE), in_map)],
            out_specs=[pl.BlockSpec((TM_MOE, D_MODEL), out_map),
                       pl.BlockSpec((TM_MOE, LANE), out_map)],
            scratch_shapes=[pltpu.VMEM((T * n_slab, LANE), F32),
                            pltpu.VMEM((T, LANE), F32)],
        ),
        out_shape=[jax.ShapeDtypeStruct((T, D_MODEL), BF16),
                   jax.ShapeDtypeStruct((T, LANE), F32)],
        compiler_params=pltpu.CompilerParams(
            dimension_semantics=("arbitrary",), vmem_limit_bytes=VMEM_LIMIT),
        name="moe_dispatch",
    )(pos, h3, meta)


def _moe_kernel(sched_ref, hs_ref, ms_ref, wup_ref, wgate_ref, wdown_ref, y3_ref, acc_ref):
    v = pl.program_id(0)
    j = pl.program_id(1)
    valid = sched_ref[V_VALID, v] == 1
    lo = sched_ref[V_LO, v]
    hi = sched_ref[V_HI, v]
    e0 = sched_ref[V_GROUP, v] * EXP_PER_GROUP + j * MOE_EPS
    full = (hi - lo) * 4 >= TM_MOE * 3

    @pl.when(valid & (j == 0) & (sched_ref[V_FIRST, v] == 1))
    def _():
        acc_ref[...] = jnp.zeros_like(acc_ref)

    def expert_rows(r0, rows):
        w_in2 = jnp.concatenate(
            [w[k].astype(BF16) for k in range(MOE_EPS) for w in (wup_ref, wgate_ref)], axis=1)
        ag = _dot(hs_ref[pl.ds(r0, rows), :], w_in2)
        comb = ms_ref[pl.ds(r0, rows), :]
        lane = lax.broadcasted_iota(jnp.int32, comb.shape, 1)
        acts = []
        for k in range(MOE_EPS):
            a = ag[:, 2 * k * D_EXPERT:(2 * k + 1) * D_EXPERT]
            g = ag[:, (2 * k + 1) * D_EXPERT:(2 * k + 2) * D_EXPERT]
            cw = jnp.sum(jnp.where(lane == e0 + k, comb, 0.0), axis=-1, keepdims=True)
            acts.append(((g * jax.nn.sigmoid(g)) * a * cw).astype(BF16))
        w_out = jnp.concatenate([wdown_ref[k].astype(BF16) for k in range(MOE_EPS)], axis=0)
        acc_ref[pl.ds(r0, rows), :] += _dot(jnp.concatenate(acts, axis=1), w_out)

    @pl.when(valid & full)
    def _():
        expert_rows(0, TM_MOE)

    @pl.when(valid & jnp.logical_not(full))
    def _():
        def sub_block(s, carry):
            expert_rows(pl.multiple_of(s * MOE_SUB, MOE_SUB), MOE_SUB)
            return carry

        lax.fori_loop(lo // MOE_SUB, (hi + MOE_SUB - 1) // MOE_SUB, sub_block, 0)

    @pl.when(valid & (j == EXP_PER_GROUP // MOE_EPS - 1) & (sched_ref[V_LAST, v] == 1))
    def _():
        for c in range(N_SLAB):
            y3_ref[pl.ds(c, TM_MOE, stride=N_SLAB), :] = acc_ref[:, LANE * c:LANE * (c + 1)]


def _moe(sched, hs, ms, w_up, w_gate, w_down):
    steps = EXP_PER_GROUP // MOE_EPS
    wmap = lambda v, j, sched: (
        sched[V_GROUP, v] * steps + jnp.where(sched[V_VALID, v] == 1, j, steps - 1), 0, 0)
    tmap = lambda v, j, sched: (sched[V_TILE, v], 0)
    n_slab = D_MODEL // LANE
    return pl.pallas_call(
        _moe_kernel,
        grid_spec=pltpu.PrefetchScalarGridSpec(
            num_scalar_prefetch=1,
            grid=(N_VISITS, steps),
            in_specs=[
                pl.BlockSpec((TM_MOE, D_MODEL), tmap),
                pl.BlockSpec((TM_MOE, LANE), tmap),
                pl.BlockSpec((MOE_EPS, D_MODEL, D_EXPERT), wmap),
                pl.BlockSpec((MOE_EPS, D_MODEL, D_EXPERT), wmap),
                pl.BlockSpec((MOE_EPS, D_EXPERT, D_MODEL), wmap),
            ],
            out_specs=pl.BlockSpec((TM_MOE * n_slab, LANE), tmap),
            scratch_shapes=[pltpu.VMEM((TM_MOE, D_MODEL), F32)],
        ),
        out_shape=jax.ShapeDtypeStruct((T * n_slab, LANE), F32),
        compiler_params=pltpu.CompilerParams(
            dimension_semantics=("arbitrary", "arbitrary"), vmem_limit_bytes=VMEM_LIMIT),
        name="moe_grouped",
    )(sched, hs, ms, w_up, w_gate, w_down)


def _final_kernel(pos_ref, ys_ref, x1_ref, mod_ref, fn_ref, yp_ref, ysm_ref, g_ref):
    i = pl.program_id(0)
    base = i * TM_FINAL

    def body(r, carry):
        g_ref[_slab(r), :] = ys_ref[_slab(pos_ref[base + r]), :]
        return carry

    lax.fori_loop(0, TM_FINAL, body, 0, unroll=8)
    mod = mod_ref[pl.ds(_mod_row(i, TM_FINAL), 1), :]
    gate2 = mod[:, 5 * D_MODEL:6 * D_MODEL]
    moe = jnp.concatenate([g_ref[pl.ds(c, TM_FINAL, stride=N_SLAB), :] for c in range(N_SLAB)], axis=1)
    x2 = x1_ref[...] + gate2 * moe
    y = (x2 * _rms(x2)) * fn_ref[...]
    is_sample = i >= T_P // TM_FINAL

    @pl.when(jnp.logical_not(is_sample))
    def _():
        yp_ref[...] = y

    @pl.when(is_sample)
    def _():
        ysm_ref[...] = y


def _final(pos, ys, x1, mod, final_norm):
    n_slab = D_MODEL // LANE
    pmap, smap = _stream_maps(TM_FINAL)
    return pl.pallas_call(
        _final_kernel,
        grid_spec=pltpu.PrefetchScalarGridSpec(
            num_scalar_prefetch=1,
            grid=(T // TM_FINAL,),
            in_specs=[
                pl.BlockSpec((T * n_slab, LANE), lambda i, pos: (0, 0), pipeline_mode=pl.Buffered(1)),
                pl.BlockSpec((TM_FINAL, D_MODEL), lambda i, pos: (i, 0)),
                pl.BlockSpec((N_COND, 6 * D_MODEL), lambda i, pos: (0, 0)),
                pl.BlockSpec((1, D_MODEL), lambda i, pos: (0, 0)),
            ],
            out_specs=[pl.BlockSpec((TM_FINAL, D_MODEL), pmap),
                       pl.BlockSpec((TM_FINAL, D_MODEL), smap)],
            scratch_shapes=[pltpu.VMEM((TM_FINAL * n_slab, LANE), F32)],
        ),
        out_shape=[jax.ShapeDtypeStruct((T_P, D_MODEL), F32),
                   jax.ShapeDtypeStruct((T_S, D_MODEL), F32)],
        compiler_params=pltpu.CompilerParams(
            dimension_semantics=("arbitrary",), vmem_limit_bytes=VMEM_LIMIT),
        name="moe_unsort_final",
    )(pos, ys, x1, mod, final_norm)


def _plan_kernel(gr_ref, cnt_ref, pos_ref, sched_ref):
    n_post = T // TM_POST
    lane = lax.broadcasted_iota(jnp.int32, (1, LANE), 1)
    grp_lane = lambda v, g: v[:, N_EXPERTS + g:N_EXPERTS + g + 1]
    counts = [cnt_ref[k, 0:1, :] for k in range(n_post)]
    gtot_v = counts[0]
    for k in range(1, n_post):
        gtot_v = gtot_v + counts[k]
    gtot = [grp_lane(gtot_v, g) for g in range(N_GROUPS)]
    goff = [jnp.zeros((1, 1), F32)]
    for g in range(1, N_GROUPS):
        goff.append(goff[-1] + gtot[g - 1])

    before = [jnp.zeros((1, 1), F32) for _ in range(N_GROUPS)]
    rows_per_tile = TM_POST // LANE
    for k in range(n_post):
        gid = gr_ref[0:1, TM_POST * k:TM_POST * (k + 1)]
        pos = gr_ref[1:2, TM_POST * k:TM_POST * (k + 1)]
        for g in range(N_GROUPS):
            pos = pos + jnp.where(gid == float(g), goff[g] + before[g], 0.0)
            before[g] = before[g] + grp_lane(counts[k], g)
        for r in range(rows_per_tile):
            pos_ref[rows_per_tile * k + r:rows_per_tile * k + r + 1, :] = (
                pos[:, LANE * r:LANE * (r + 1)].astype(jnp.int32))

    zero = jnp.zeros((1, LANE), F32)
    rows = {name: zero for name in ("vt", "vg", "vlo", "vhi")}
    slot = jnp.zeros((1, 1), F32)
    last_t = jnp.zeros((1, 1), F32)
    last_g = jnp.zeros((1, 1), F32)
    for i in range(T // TM_MOE):
        for g in range(N_GROUPS):
            lo = jnp.clip(goff[g] - float(TM_MOE * i), 0.0, float(TM_MOE))
            hi = jnp.clip(goff[g] + gtot[g] - float(TM_MOE * i), 0.0, float(TM_MOE))
            ok = hi > lo
            here = ok & (lane == slot.astype(jnp.int32))
            rows["vt"] = jnp.where(here, float(i), rows["vt"])
            rows["vg"] = jnp.where(here, float(g), rows["vg"])
            rows["vlo"] = jnp.where(here, lo, rows["vlo"])
            rows["vhi"] = jnp.where(here, hi, rows["vhi"])
            last_t = jnp.where(ok, float(i), last_t)
            last_g = jnp.where(ok, float(g), last_g)
            slot = slot + jnp.where(ok, 1.0, 0.0)
    n_ok = slot.astype(jnp.int32)
    valid = lane < n_ok
    vt = jnp.where(valid, rows["vt"], last_t)
    vg = jnp.where(valid, rows["vg"], last_g)
    prev_t = pltpu.roll(jnp.broadcast_to(vt, (8, LANE)), 1, 1)[0:1, :]
    next_t = pltpu.roll(jnp.broadcast_to(vt, (8, LANE)), LANE - 1, 1)[0:1, :]
    first = jnp.where((lane == 0) | (vt != prev_t), 1.0, 0.0)
    last = jnp.where((lane == n_ok - 1) | (vt != next_t), 1.0, 0.0)
    table = [vt, vg, rows["vlo"], rows["vhi"], first, last, jnp.where(valid, 1.0, 0.0), zero]
    for r, row in enumerate(table):
        sched_ref[r:r + 1, :] = row.astype(jnp.int32)


V_TILE, V_GROUP, V_LO, V_HI, V_FIRST, V_LAST, V_VALID = range(7)


def _plan(gr, cnt):
    pos, sched = pl.pallas_call(
        _plan_kernel,
        out_shape=[jax.ShapeDtypeStruct((T // LANE, LANE), jnp.int32),
                   jax.ShapeDtypeStruct((8, LANE), jnp.int32)],
        name="moe_plan",
    )(gr, cnt)
    return pos.reshape(T), sched


def _rope_tables():
    n = np.arange(DEC_SEQ)
    pos = np.stack([n // GRID_W, n % GRID_W], axis=1).astype(np.float32)
    half = ROPE_AXIS // 2
    inv = (1.0 / (ROPE_BASE ** (np.arange(0, ROPE_AXIS, 2, dtype=np.float32) / ROPE_AXIS))).astype(np.float32)
    ang = (pos[:, :, None] * inv[None, None, :]).astype(np.float32)
    cos = np.cos(ang).astype(np.float32)
    sin = np.sin(ang).astype(np.float32)
    tabs = np.zeros((2, 3, DEC_SEQ, LANE), np.float32)
    tabs[:, 0] = 1.0
    for a in range(2):
        lo = ROPE_LANE0 + a * ROPE_AXIS
        tabs[1, 0, :, lo:lo + half] = cos[:, a]
        tabs[1, 0, :, lo + half:lo + 2 * half] = cos[:, a]
        tabs[1, 1, :, lo + half:lo + 2 * half] = sin[:, a]
        tabs[1, 2, :, lo:lo + half] = -sin[:, a]
    return jnp.asarray(tabs)


def kernel(x_prompt, x_sample, cache_ckv, cache_krope, c, c_ctx, norm1, w_ada, b_ada, w_in, conv_w,
           w_conv_out, q_norm, w_uq, kv_norm, w_ukv, w_o, w_mix_out, norm2, w_grp, w_exp, w_up,
           w_gate, w_down, final_norm):
    l = 0
    xp = x_prompt.reshape(T_P, D_MODEL)
    xs = x_sample.reshape(T_S, D_MODEL)
    mod = _ada(c_ctx[None, :], c, w_ada[l], b_ada[l][None, :])

    w_in_t = w_in[l].T
    w_uq_slot = jnp.pad(w_uq[l].reshape(Q_LORA, N_HEADS, QK_NOPE + QK_ROPE),
                        ((0, 0), (0, 0), (0, LANE - QK_NOPE - QK_ROPE))
                        ).reshape(Q_LORA, N_HEADS * LANE).astype(BF16)
    wkv = w_ukv[l].reshape(KV_LORA, N_HEADS, QK_NOPE + V_HEAD)
    wk_slot = jnp.pad(wkv[:, :, :QK_NOPE], ((0, 0), (0, 0), (0, LANE - QK_NOPE)))
    wv = wkv[:, :, QK_NOPE:].reshape(KV_LORA, N_HEADS // 2, 2, V_HEAD)
    zero = jnp.zeros_like(wv[:, :, 0])
    wv_slot = jnp.stack([jnp.concatenate([wv[:, :, 0], zero], axis=-1),
                         jnp.concatenate([zero, wv[:, :, 1]], axis=-1)], axis=2)
    w_ukv_slot = jnp.concatenate([wk_slot.reshape(KV_LORA, N_HEADS * LANE),
                                  wv_slot.reshape(KV_LORA, N_HEADS * LANE)], axis=1).astype(BF16)
    w_route = jnp.pad(jnp.concatenate([w_exp[l], w_grp[l]], axis=1),
                      ((0, 0), (0, LANE - N_EXPERTS - N_GROUPS)))
    w_route_hi = w_route.astype(BF16)
    w_route_lo = (w_route - w_route_hi.astype(F32)).astype(BF16)
    w_route2 = jnp.concatenate([w_route_hi, w_route_lo], axis=1)
    cache_krs = jnp.pad(cache_krope[:, l], ((0, 0), (0, 0), (ROPE_LANE0, LANE - ROPE_LANE0 - QK_ROPE)))

    zc, q, ckv, krs, nckv, nkr = _inproj(xp, xs, mod, norm1[l][None, :], w_in_t, conv_w[l],
                              q_norm[l][None, :], kv_norm[l][None, :], w_uq_slot, _rope_tables())
    o_p, o_s, w_gates_t, w_co_b, w_o_b, w_mix_b = _attention(
        q, ckv, krs, cache_ckv[:, l], cache_krs, w_ukv_slot, w_in_t, w_conv_out[l], w_o[l], w_mix_out[l])
    x1, h3, meta, cnt, gr = _post(xp, xs, mod, norm1[l][None, :], w_gates_t, zc, o_p, o_s, w_co_b, w_o_b,
                                  w_mix_b, norm2[l][None, :], w_route2)
    pos, sched = _plan(gr, cnt)
    hs, ms = _dispatch(pos, h3, meta)
    ys = _moe(sched, hs, ms, w_up[l], w_gate[l], w_down[l])
    yp, ysm = _final(pos, ys, x1, mod, final_norm[None, :])

    y_prompt = yp.reshape(BATCH, SEQ, D_MODEL)
    y_sample = ysm.reshape(DEC_BATCH, DEC_SEQ, D_MODEL)
    new_ckv = nckv.reshape(BATCH, 1, SEQ, KV_LORA)
    new_krope = jnp.swapaxes(nkr, 1, 2).reshape(BATCH, 1, SEQ, QK_ROPE)
    return (y_prompt, y_sample, new_ckv, new_krope)
```

```python
import functools

import numpy as np
import jax
import jax.numpy as jnp
from jax import lax
from jax.experimental import pallas as pl
from jax.experimental.pallas import tpu as pltpu

F32 = jnp.float32
BF16 = jnp.bfloat16

D_MODEL = 1024
BATCH = 16
SEQ = 256
DEC_BATCH = 2
DEC_SEQ = 1024
PAST_LEN = 256
GRID_W = 64
N_HEADS = 8
QK_NOPE = 64
QK_ROPE = 32
V_HEAD = 64
Q_LORA = 256
KV_LORA = 128
ROPE_AXIS = QK_ROPE // 2
ROPE_BASE = 10000.0
ATTN_SCALE = (QK_NOPE + QK_ROPE) ** -0.5
D_CONV = D_MODEL
N_GROUPS = 4
EXP_PER_GROUP = 8
N_EXPERTS = N_GROUPS * EXP_PER_GROUP
D_EXPERT = 256
EPS = 1e-6

T_P = BATCH * SEQ
T_S = DEC_BATCH * DEC_SEQ
T = T_P + T_S
N_COND = 8
LANE = 128
ROPE_LANE0 = QK_NOPE
SMALL_COLS = Q_LORA + KV_LORA + LANE
VMEM_LIMIT = 56 * 1024 * 1024

TM_IN = 1024
TM_POST = 512
TM_MOE = T // N_GROUPS
TM_FINAL = 512
TM_DISP = 1024
N_SLAB = D_MODEL // LANE
MOE_SUB = 256
MOE_EPS = 2
N_VISITS = T // TM_MOE + N_GROUPS - 1
GID_LANE = 40
RANK_LANE = 41
Q_BLK = 256
Q_BLK_S = 512
PROMPT_SEQS = 2
CONV_CHUNK = 256


def _dot(a, b):
    return jnp.dot(a, b, preferred_element_type=F32)


def _rms(x):
    return lax.rsqrt(jnp.mean(x * x, axis=-1, keepdims=True) + EPS)


def _slab(t):
    return pl.ds(pl.multiple_of(t * N_SLAB, N_SLAB), N_SLAB)


def _mod_row(i, tm):
    n_prompt = T_P // tm
    return jnp.where(i >= n_prompt, 1 + ((i - n_prompt) * tm) // DEC_SEQ, 0)


def _ada_kernel(cctx_ref, c_ref, w_ref, b_ref, o_ref):
    c = jnp.concatenate([cctx_ref[...], c_ref[...],
                         jnp.zeros((N_COND - 1 - DEC_BATCH, D_MODEL), F32)], axis=0)
    a = (c * jax.nn.sigmoid(c)).astype(BF16)
    o_ref[...] = _dot(a, w_ref[...].astype(BF16)) + b_ref[...]


def _ada(c_ctx, c, w_ada, b_ada):
    n = 6 * D_MODEL
    bn = 1536
    return pl.pallas_call(
        _ada_kernel,
        grid=(n // bn,),
        in_specs=[
            pl.BlockSpec((1, D_MODEL), lambda j: (0, 0)),
            pl.BlockSpec((DEC_BATCH, D_MODEL), lambda j: (0, 0)),
            pl.BlockSpec((D_MODEL, bn), lambda j: (0, j)),
            pl.BlockSpec((1, bn), lambda j: (0, j)),
        ],
        out_specs=pl.BlockSpec((N_COND, bn), lambda j: (0, j)),
        out_shape=jax.ShapeDtypeStruct((N_COND, n), F32),
        compiler_params=pltpu.CompilerParams(
            dimension_semantics=("parallel",), vmem_limit_bytes=VMEM_LIMIT),
        name="ada_mod",
    )(c_ctx, c, w_ada, b_ada)


O_CQ = 3 * D_CONV
O_KR = O_CQ + Q_LORA + KV_LORA
O_GATE = O_KR + QK_ROPE
PREP_STEPS = 4


IN_COLS = O_GATE + 2 * D_MODEL
TAIL_ROWS = IN_COLS // 2
SMALL_BLK = 512
NT = (((1,), (1,)), ((), ()))


def _dot_nt(a, bt):
    return lax.dot_general(a, bt, NT, preferred_element_type=F32)


def _stream_maps(tm):
    n_prompt = T_P // tm
    return (lambda i, *_: (jnp.minimum(i, n_prompt - 1), 0),
            lambda i, *_: (jnp.maximum(i - n_prompt, 0), 0))


def _inproj_kernel(xp_ref, xs_ref, mod_ref, n1_ref, wc_ref, ws_ref, cw_ref, qn_ref, kvn_ref, wuq_ref,
                   rope_ref, zc_ref, q_ref, ckv_ref, krs_ref, nckv_ref, nkr_ref):
    i = pl.program_id(0)
    is_sample = i >= T_P // TM_IN
    seq = jnp.where(is_sample, DEC_SEQ, SEQ)
    mod = mod_ref[pl.ds(_mod_row(i, TM_IN), 1), :]
    shift1 = mod[:, 0:D_MODEL]
    scale1 = mod[:, D_MODEL:2 * D_MODEL]
    x = jnp.where(is_sample, xs_ref[...], xp_ref[...])
    h = ((x * _rms(x)) * n1_ref[...]) * (1.0 + scale1) + shift1
    hb = h.astype(BF16)

    n_lat = Q_LORA + KV_LORA
    w_small = jnp.concatenate([
        ws_ref[0:n_lat, :].astype(BF16),
        jnp.zeros((ROPE_LANE0, D_MODEL), BF16),
        ws_ref[n_lat:n_lat + QK_ROPE, :].astype(BF16),
        jnp.zeros((LANE - ROPE_LANE0 - QK_ROPE, D_MODEL), BF16)], axis=0)
    sm = _dot_nt(hb, w_small)
    cq = sm[:, 0:Q_LORA]
    ckv_raw = sm[:, Q_LORA:Q_LORA + KV_LORA]
    krs = sm[:, Q_LORA + KV_LORA:SMALL_COLS]
    cqn = (cq * _rms(cq)) * qn_ref[...]
    q = _dot(cqn.astype(BF16), wuq_ref[...])
    ckv = (ckv_raw * _rms(ckv_raw)) * kvn_ref[...]
    ckv_ref[...] = ckv

    @pl.when(jnp.logical_not(is_sample))
    def _():
        nckv_ref[...] = ckv
        for s in range(TM_IN // SEQ):
            kt = krs[s * SEQ:(s + 1) * SEQ, :].T
            nkr_ref[s] = kt[ROPE_LANE0:ROPE_LANE0 + QK_ROPE, :]

    cos = rope_ref[0]
    sin_lo = rope_ref[1]
    sin_hi = rope_ref[2]

    def rot(v):
        return v * cos + pltpu.roll(v, 8, 1) * sin_lo + pltpu.roll(v, LANE - 8, 1) * sin_hi

    krs_ref[...] = rot(krs)
    for hh in range(N_HEADS):
        q_ref[:, LANE * hh:LANE * (hh + 1)] = rot(q[:, LANE * hh:LANE * (hh + 1)]).astype(BF16)

    pos = lax.broadcasted_iota(jnp.int32, (TM_IN, 1), 0) & (seq - 1)
    first = pos == 0
    last = pos == seq - 1
    for j in range(D_CONV // CONV_CHUNK):
        c0 = j * CONV_CHUNK
        bg = _dot_nt(hb, wc_ref[c0:c0 + CONV_CHUNK, :].astype(BF16))
        cg = _dot_nt(hb, wc_ref[D_CONV + c0:D_CONV + c0 + CONV_CHUNK, :].astype(BF16))
        ui = _dot_nt(hb, wc_ref[2 * D_CONV + c0:2 * D_CONV + c0 + CONV_CHUNK, :].astype(BF16))
        u = cg * ui
        u_prev = jnp.where(first, 0.0, pltpu.roll(u, 1, 0))
        u_next = jnp.where(last, 0.0, pltpu.roll(u, TM_IN - 1, 0))
        cw = cw_ref[:, c0:c0 + CONV_CHUNK]
        conv = u_prev * cw[0:1] + u * cw[1:2] + u_next * cw[2:3]
        zc_ref[:, c0:c0 + CONV_CHUNK] = (bg * conv).astype(BF16)


def _inproj(xp, xs, mod, norm1, w_in_t, conv_w, q_norm, kv_norm, w_uq_slot, rope_tabs):
    n_prompt = T_P // TM_IN
    const = lambda i: (0, 0)
    pmap, smap = _stream_maps(TM_IN)
    once = pl.Buffered(1)
    return pl.pallas_call(
        _inproj_kernel,
        grid=(T // TM_IN,),
        in_specs=[
            pl.BlockSpec((TM_IN, D_MODEL), pmap),
            pl.BlockSpec((TM_IN, D_MODEL), smap),
            pl.BlockSpec((N_COND, 6 * D_MODEL), const),
            pl.BlockSpec((1, D_MODEL), const),
            pl.BlockSpec((O_CQ, D_MODEL), const, pipeline_mode=once),
            pl.BlockSpec((SMALL_BLK, D_MODEL), lambda i: (O_CQ // SMALL_BLK, 0), pipeline_mode=once),
            pl.BlockSpec((3, D_CONV), const),
            pl.BlockSpec((1, Q_LORA), const),
            pl.BlockSpec((1, KV_LORA), const),
            pl.BlockSpec((Q_LORA, N_HEADS * LANE), const),
            pl.BlockSpec((None, 3, TM_IN, LANE),
                         lambda i: (jnp.where(i >= n_prompt, 1, 0), 0, 0, 0)),
        ],
        out_specs=[
            pl.BlockSpec((TM_IN, D_CONV), lambda i: (i, 0)),
            pl.BlockSpec((TM_IN, N_HEADS * LANE), lambda i: (i, 0)),
            pl.BlockSpec((TM_IN, KV_LORA), lambda i: (i, 0)),
            pl.BlockSpec((TM_IN, LANE), lambda i: (i, 0)),
            pl.BlockSpec((TM_IN, KV_LORA), pmap),
            pl.BlockSpec((TM_IN // SEQ, QK_ROPE, SEQ), lambda i: (jnp.minimum(i, n_prompt - 1), 0, 0)),
        ],
        out_shape=[
            jax.ShapeDtypeStruct((T, D_CONV), BF16),
            jax.ShapeDtypeStruct((T, N_HEADS * LANE), BF16),
            jax.ShapeDtypeStruct((T, KV_LORA), F32),
            jax.ShapeDtypeStruct((T, LANE), F32),
            jax.ShapeDtypeStruct((T_P, KV_LORA), F32),
            jax.ShapeDtypeStruct((BATCH, QK_ROPE, SEQ), F32),
        ],
        compiler_params=pltpu.CompilerParams(
            dimension_semantics=("arbitrary",), vmem_limit_bytes=VMEM_LIMIT),
        name="in_proj",
    )(xp, xs, mod, norm1, w_in_t, w_in_t, conv_w, q_norm, kv_norm, w_uq_slot, rope_tabs)


def _fill_kv(ckv, krs, wukv_ref, kf_scr, v_scr, off):
    m = ckv.shape[0]
    kv = _dot(ckv.astype(BF16), wukv_ref[...])
    for hh in range(N_HEADS):
        kf_scr[hh, off:off + m, :] = (kv[:, LANE * hh:LANE * (hh + 1)] + krs).astype(BF16)
    v_scr[off:off + m, :] = kv[:, N_HEADS * LANE:].astype(BF16)


def _attend(q_ref, r0, rows, kf_scr, v_scr, o_ref):
    for pair in range(N_HEADS // 2):
        acc = None
        for hh in (2 * pair, 2 * pair + 1):
            qh = q_ref[r0:r0 + rows, LANE * hh:LANE * (hh + 1)]
            s = _dot_nt(qh, kf_scr[hh]) * ATTN_SCALE
            e = jnp.exp(s - jnp.max(s, axis=-1, keepdims=True))
            p = (e / jnp.sum(e, axis=-1, keepdims=True)).astype(BF16)
            part = _dot(p, v_scr[:, LANE * hh:LANE * (hh + 1)])
            acc = part if acc is None else acc + part
        o_ref[r0:r0 + rows, LANE * pair:LANE * (pair + 1)] = acc.astype(BF16)


def _attn_prompt_kernel(q_ref, ckv_ref, krs_ref, wukv_ref, o_ref):
    for s in range(PROMPT_SEQS):
        r0 = s * SEQ
        kv = _dot(ckv_ref[r0:r0 + SEQ, :].astype(BF16), wukv_ref[...])
        krs = krs_ref[r0:r0 + SEQ, :]
        for pair in range(N_HEADS // 2):
            acc = None
            for hh in (2 * pair, 2 * pair + 1):
                kf = (kv[:, LANE * hh:LANE * (hh + 1)] + krs).astype(BF16)
                vh = kv[:, LANE * (N_HEADS + hh):LANE * (N_HEADS + hh + 1)].astype(BF16)
                sc = _dot_nt(q_ref[r0:r0 + SEQ, LANE * hh:LANE * (hh + 1)], kf) * ATTN_SCALE
                e = jnp.exp(sc - jnp.max(sc, axis=-1, keepdims=True))
                p = (e / jnp.sum(e, axis=-1, keepdims=True)).astype(BF16)
                part = _dot(p, vh)
                acc = part if acc is None else acc + part
            o_ref[r0:r0 + SEQ, LANE * pair:LANE * (pair + 1)] = acc.astype(BF16)


def _attn_sample_kernel(q_ref, ckv_ref, cckv_ref, krs_ref, ckrs_ref, wukv_ref, wg_ref, wco_ref, wo_ref,
                        wmix_ref, o_ref, g_ref, co_ref, ob_ref, mix_ref, kf_scr, v_scr):
    @pl.when(pl.program_id(1) == 0)
    def _():
        _fill_kv(ckv_ref[...], krs_ref[...], wukv_ref, kf_scr, v_scr, 0)
        _fill_kv(cckv_ref[...], ckrs_ref[...], wukv_ref, kf_scr, v_scr, DEC_SEQ)

    _attend(q_ref, 0, Q_BLK_S, kf_scr, v_scr, o_ref)
    g_ref[...] = wg_ref[...].astype(BF16)
    co_ref[...] = wco_ref[...].astype(BF16)
    ob_ref[...] = wo_ref[...].astype(BF16)
    mix_ref[...] = wmix_ref[...].astype(BF16)


def _attention(q, ckv, krs, cache_ckv, cache_krs, w_ukv_slot, w_in_t, w_conv_out, w_o, w_mix_out):
    kv_cols = 2 * N_HEADS * LANE
    n_o = N_HEADS * V_HEAD
    steps = BATCH // PROMPT_SEQS
    rows = PROMPT_SEQS * SEQ
    o_prompt = pl.pallas_call(
        _attn_prompt_kernel,
        grid=(steps,),
        in_specs=[
            pl.BlockSpec((rows, N_HEADS * LANE), lambda b: (b, 0)),
            pl.BlockSpec((rows, KV_LORA), lambda b: (b, 0)),
            pl.BlockSpec((rows, LANE), lambda b: (b, 0)),
            pl.BlockSpec((KV_LORA, kv_cols), lambda b: (0, 0)),
        ],
        out_specs=pl.BlockSpec((rows, n_o), lambda b: (b, 0)),
        out_shape=jax.ShapeDtypeStruct((T_P, n_o), BF16),
        compiler_params=pltpu.CompilerParams(dimension_semantics=("parallel",),
                                             vmem_limit_bytes=VMEM_LIMIT),
        name="attn_prompt",
    )(q, ckv, krs, w_ukv_slot)

    m_all = DEC_SEQ + PAST_LEN
    nq = DEC_SEQ // Q_BLK_S
    q0 = T_P // Q_BLK_S
    s0 = T_P // DEC_SEQ
    n_step = DEC_BATCH * nq
    share = lambda n: pl.BlockSpec((n // n_step, D_MODEL), lambda b, j: (b * nq + j, 0))
    gate_rows = 2 * D_MODEL // n_step
    o_sample, w_gates_t, w_co_b, w_o_b, w_mix_b = pl.pallas_call(
        _attn_sample_kernel,
        grid=(DEC_BATCH, nq),
        in_specs=[
            pl.BlockSpec((Q_BLK_S, N_HEADS * LANE), lambda b, j: (q0 + b * nq + j, 0)),
            pl.BlockSpec((DEC_SEQ, KV_LORA), lambda b, j: (s0 + b, 0)),
            pl.BlockSpec((None, PAST_LEN, KV_LORA), lambda b, j: (b, 0, 0)),
            pl.BlockSpec((DEC_SEQ, LANE), lambda b, j: (s0 + b, 0)),
            pl.BlockSpec((None, PAST_LEN, LANE), lambda b, j: (b, 0, 0)),
            pl.BlockSpec((KV_LORA, kv_cols), lambda b, j: (0, 0)),
            pl.BlockSpec((pl.Element(gate_rows), pl.Element(D_MODEL)),
                         lambda b, j: (pl.multiple_of(O_GATE + (b * nq + j) * gate_rows, 32), 0)),
            share(D_CONV), share(n_o), share(D_MODEL),
        ],
        out_specs=[pl.BlockSpec((Q_BLK_S, n_o), lambda b, j: (b * nq + j, 0)),
                   share(2 * D_MODEL), share(D_CONV), share(n_o), share(D_MODEL)],
        out_shape=[jax.ShapeDtypeStruct((T_S, n_o), BF16),
                   jax.ShapeDtypeStruct((2 * D_MODEL, D_MODEL), BF16),
                   jax.ShapeDtypeStruct((D_CONV, D_MODEL), BF16),
                   jax.ShapeDtypeStruct((n_o, D_MODEL), BF16),
                   jax.ShapeDtypeStruct((D_MODEL, D_MODEL), BF16)],
        scratch_shapes=[pltpu.VMEM((N_HEADS, m_all, LANE), BF16),
                        pltpu.VMEM((m_all, N_HEADS * LANE), BF16)],
        compiler_params=pltpu.CompilerParams(dimension_semantics=("arbitrary", "arbitrary"),
                                             vmem_limit_bytes=VMEM_LIMIT),
        name="attn_sample",
    )(q, ckv, cache_ckv, krs, cache_krs, w_ukv_slot, w_in_t, w_conv_out, w_o, w_mix_out)
    return o_prompt, o_sample, w_gates_t, w_co_b, w_o_b, w_mix_b


def _route(logits):
    lane = lax.broadcasted_iota(jnp.int32, logits.shape, 1)
    neg = -jnp.inf
    big = jnp.int32(1 << 20)
    gmask = (lane >= N_EXPERTS) & (lane < N_EXPERTS + N_GROUPS)
    gl = jnp.where(gmask, logits, neg)
    gmax = jnp.max(gl, axis=-1, keepdims=True)
    gsum = jnp.sum(jnp.where(gmask, jnp.exp(gl - gmax), 0.0), axis=-1, keepdims=True)
    p_g = 1.0 / gsum
    g_idx = jnp.min(jnp.where(gl == gmax, lane, big), axis=-1, keepdims=True) - N_EXPERTS

    emask = (lane < N_EXPERTS) & ((lane >> 3) == g_idx)
    el = jnp.where(emask, logits, neg)
    m1 = jnp.max(el, axis=-1, keepdims=True)
    i1 = jnp.min(jnp.where(el == m1, lane, big), axis=-1, keepdims=True)
    el2 = jnp.where(lane == i1, neg, el)
    m2 = jnp.max(el2, axis=-1, keepdims=True)
    i2 = jnp.min(jnp.where(el2 == m2, lane, big), axis=-1, keepdims=True)
    z = jnp.sum(jnp.where(emask, jnp.exp(el - m1), 0.0), axis=-1, keepdims=True)
    p1 = 1.0 / z
    p2 = jnp.exp(m2 - m1) / z
    tot = p1 + p2
    w1 = p_g * p1 / tot
    w2 = p_g * p2 / tot
    return jnp.where(lane == i1, w1, 0.0) + jnp.where(lane == i2, w2, 0.0), g_idx


def _post_kernel(xp_ref, xs_ref, mod_ref, n1_ref, wg_ref, zc_ref, op_ref, os_ref, wco_ref, wo_ref,
                 wmix_ref, n2_ref, wr_ref, x1_ref, h3_ref, meta_ref, cnt_ref, gr_ref):
    i = pl.program_id(0)
    is_sample = i >= T_P // TM_POST
    x = jnp.where(is_sample, xs_ref[...], xp_ref[...])
    o = jnp.where(is_sample, os_ref[...], op_ref[...])
    mod = mod_ref[pl.ds(_mod_row(i, TM_POST), 1), :]
    shift1 = mod[:, 0:D_MODEL]
    scale1 = mod[:, D_MODEL:2 * D_MODEL]
    gate1 = mod[:, 2 * D_MODEL:3 * D_MODEL]
    shift2 = mod[:, 3 * D_MODEL:4 * D_MODEL]
    scale2 = mod[:, 4 * D_MODEL:5 * D_MODEL]
    y_conv = _dot(zc_ref[...], wco_ref[...])
    y_mla = _dot(o, wo_ref[...])
    h = ((x * _rms(x)) * n1_ref[...]) * (1.0 + scale1) + shift1
    g = _dot_nt(h.astype(BF16), wg_ref[...])
    merged = (jax.nn.sigmoid(g[:, 0:D_MODEL]) * y_conv
              + jax.nn.sigmoid(g[:, D_MODEL:2 * D_MODEL]) * y_mla)
    y = _dot(merged.astype(BF16), wmix_ref[...])
    x1 = x + gate1 * y
    x1_ref[...] = x1
    h2 = ((x1 * _rms(x1)) * n2_ref[...]) * (1.0 + scale2) + shift2
    h2_hi = h2.astype(BF16)
    h2_lo = (h2 - h2_hi.astype(F32)).astype(BF16)
    hh = _dot(h2_hi, wr_ref[...])
    logits = hh[:, 0:LANE] + hh[:, LANE:2 * LANE] + _dot(h2_lo, wr_ref[:, 0:LANE])
    comb, g_idx = _route(logits)

    lane = lax.broadcasted_iota(jnp.int32, comb.shape, 1)
    onehot = lane == g_idx + N_EXPERTS
    r_i = lax.broadcasted_iota(jnp.int32, (TM_POST, TM_POST), 0)
    c_i = lax.broadcasted_iota(jnp.int32, (TM_POST, TM_POST), 1)
    lower = jnp.where(c_i < r_i, 1.0, 0.0).astype(BF16)
    before = _dot(lower, jnp.where(onehot, 1.0, 0.0).astype(BF16))
    rank = jnp.sum(jnp.where(onehot, before, 0.0), axis=-1, keepdims=True)
    counts = jnp.sum(jnp.where(onehot, 1.0, 0.0), axis=0, keepdims=True)
    cnt_ref[...] = jnp.broadcast_to(counts, cnt_ref.shape)

    meta_ref[...] = comb
    idx = jnp.where(lane == GID_LANE, g_idx.astype(F32), 0.0) + jnp.where(lane == RANK_LANE, rank, 0.0)
    idx_hi = idx.astype(BF16)
    idx_lo = (idx - idx_hi.astype(F32)).astype(BF16)
    s_row = lax.broadcasted_iota(jnp.int32, (8, LANE), 0)
    s_lane = lax.broadcasted_iota(jnp.int32, (8, LANE), 1)
    sel = jnp.where(s_lane == GID_LANE + s_row, 1.0, 0.0).astype(BF16)
    gr_ref[...] = _dot_nt(sel, idx_hi) + _dot_nt(sel, idx_lo)
    for c in range(N_SLAB):
        h3_ref[pl.ds(c, TM_POST, stride=N_SLAB), :] = h2[:, LANE * c:LANE * (c + 1)]


def _post(xp, xs, mod, norm1, w_gates_t, zc, o_p, o_s, w_conv_out, w_o, w_mix_out, norm2, w_route):
    const = lambda i: (0, 0)
    row = lambda i: (i, 0)
    pmap, smap = _stream_maps(TM_POST)
    return pl.pallas_call(
        _post_kernel,
        grid=(T // TM_POST,),
        in_specs=[
            pl.BlockSpec((TM_POST, D_MODEL), pmap),
            pl.BlockSpec((TM_POST, D_MODEL), smap),
            pl.BlockSpec((N_COND, 6 * D_MODEL), const),
            pl.BlockSpec((1, D_MODEL), const),
            pl.BlockSpec((2 * D_MODEL, D_MODEL), const),
            pl.BlockSpec((TM_POST, D_CONV), row),
            pl.BlockSpec((TM_POST, N_HEADS * V_HEAD), pmap),
            pl.BlockSpec((TM_POST, N_HEADS * V_HEAD), smap),
            pl.BlockSpec((D_CONV, D_MODEL), const),
            pl.BlockSpec((N_HEADS * V_HEAD, D_MODEL), const),
            pl.BlockSpec((D_MODEL, D_MODEL), const),
            pl.BlockSpec((1, D_MODEL), const),
            pl.BlockSpec((D_MODEL, 2 * LANE), const),
        ],
        out_specs=[
            pl.BlockSpec((TM_POST, D_MODEL), row),
            pl.BlockSpec((TM_POST * N_SLAB, LANE), row),
            pl.BlockSpec((TM_POST, LANE), row),
            pl.BlockSpec((None, 8, LANE), lambda i: (i, 0, 0)),
            pl.BlockSpec((8, TM_POST), lambda i: (0, i)),
        ],
        out_shape=[
            jax.ShapeDtypeStruct((T, D_MODEL), F32),
            jax.ShapeDtypeStruct((T * N_SLAB, LANE), F32),
            jax.ShapeDtypeStruct((T, LANE), F32),
            jax.ShapeDtypeStruct((T // TM_POST, 8, LANE), F32),
            jax.ShapeDtypeStruct((8, T), F32),
        ],
        compiler_params=pltpu.CompilerParams(
            dimension_semantics=("parallel",), vmem_limit_bytes=VMEM_LIMIT),
        name="post_mixer",
    )(xp, xs, mod, norm1, w_gates_t, zc, o_p, o_s, w_conv_out, w_o, w_mix_out, norm2, w_route)


DISP_IN = T // TM_DISP
DISP_OUT = T // TM_MOE


def _dispatch_kernel(pos_ref, h3_ref, m_ref, hs_ref, ms_ref, src_ref, xs_ref, mss_ref):
    i = pl.program_id(0)

    @pl.when(i < DISP_IN)
    def _():
        base = i * TM_DISP

        def body(r, carry):
            p = pos_ref[base + r]
            xs_ref[_slab(p), :] = h3_ref[_slab(r), :]
            mss_ref[pl.ds(p, 1), :] = m_ref[pl.ds(r, 1), :]
            src_ref[p] = base + r
            return carry

        lax.fori_loop(0, TM_DISP, body, 0, unroll=8)

    @pl.when(i >= DISP_IN)
    def _():
        row0 = pl.multiple_of((i - DISP_IN) * TM_MOE, TM_MOE)
        for c in range(N_SLAB):
            hs_ref[:, LANE * c:LANE * (c + 1)] = (
                xs_ref[pl.ds(row0 * N_SLAB + c, TM_MOE, stride=N_SLAB), :].astype(BF16))
        ms_ref[...] = mss_ref[pl.ds(row0, TM_MOE), :]


def _dispatch(pos, h3, meta):
    n_slab = N_SLAB
    in_map = lambda i, pos: (jnp.minimum(i, DISP_IN - 1), 0)
    out_map = lambda i, pos: (jnp.maximum(i - DISP_IN, 0), 0)
    return pl.pallas_call(
        _dispatch_kernel,
        grid_spec=pltpu.PrefetchScalarGridSpec(
            num_scalar_prefetch=1,
            grid=(DISP_IN + DISP_OUT,),
            in_specs=[pl.BlockSpec((TM_DISP * n_slab, LANE), in_map),
                      pl.BlockSpec((TM_DISP, LANE), in_map)],
            out_specs=[pl.BlockSpec((TM_MOE, D_MODEL), out_map),
                       pl.BlockSpec((TM_MOE, LANE), out_map),
                       pl.BlockSpec(memory_space=pltpu.SMEM)],
            scratch_shapes=[pltpu.VMEM((T * n_slab, LANE), F32),
                            pltpu.VMEM((T, LANE), F32)],
        ),
        out_shape=[jax.ShapeDtypeStruct((T, D_MODEL), BF16),
                   jax.ShapeDtypeStruct((T, LANE), F32),
                   jax.ShapeDtypeStruct((T,), jnp.int32)],
        compiler_params=pltpu.CompilerParams(
            dimension_semantics=("arbitrary",), vmem_limit_bytes=VMEM_LIMIT),
        name="moe_dispatch",
    )(pos, h3, meta)


def _moe_kernel(sched_ref, hs_ref, ms_ref, wup_ref, wgate_ref, wdown_ref, y3_ref, acc_ref):
    v = pl.program_id(0)
    j = pl.program_id(1)
    valid = sched_ref[V_VALID, v] == 1
    lo = sched_ref[V_LO, v]
    hi = sched_ref[V_HI, v]
    e0 = sched_ref[V_GROUP, v] * EXP_PER_GROUP + j * MOE_EPS
    full = (hi - lo) * 4 >= TM_MOE * 3

    @pl.when(valid & (j == 0) & (sched_ref[V_FIRST, v] == 1))
    def _():
        acc_ref[...] = jnp.zeros_like(acc_ref)

    def expert_rows(r0, rows):
        w_in2 = jnp.concatenate(
            [w[k].astype(BF16) for k in range(MOE_EPS) for w in (wup_ref, wgate_ref)], axis=1)
        ag = _dot(hs_ref[pl.ds(r0, rows), :], w_in2)
        comb = ms_ref[pl.ds(r0, rows), :]
        lane = lax.broadcasted_iota(jnp.int32, comb.shape, 1)
        acts = []
        for k in range(MOE_EPS):
            a = ag[:, 2 * k * D_EXPERT:(2 * k + 1) * D_EXPERT]
            g = ag[:, (2 * k + 1) * D_EXPERT:(2 * k + 2) * D_EXPERT]
            cw = jnp.sum(jnp.where(lane == e0 + k, comb, 0.0), axis=-1, keepdims=True)
            acts.append(((g * jax.nn.sigmoid(g)) * a * cw).astype(BF16))
        w_out = jnp.concatenate([wdown_ref[k].astype(BF16) for k in range(MOE_EPS)], axis=0)
        acc_ref[pl.ds(r0, rows), :] += _dot(jnp.concatenate(acts, axis=1), w_out)

    @pl.when(valid & full)
    def _():
        expert_rows(0, TM_MOE)

    @pl.when(valid & jnp.logical_not(full))
    def _():
        def sub_block(s, carry):
            expert_rows(pl.multiple_of(s * MOE_SUB, MOE_SUB), MOE_SUB)
            return carry

        lax.fori_loop(lo // MOE_SUB, (hi + MOE_SUB - 1) // MOE_SUB, sub_block, 0)

    @pl.when(valid & (j == EXP_PER_GROUP // MOE_EPS - 1) & (sched_ref[V_LAST, v] == 1))
    def _():
        for c in range(N_SLAB):
            y3_ref[pl.ds(c, TM_MOE, stride=N_SLAB), :] = acc_ref[:, LANE * c:LANE * (c + 1)]


def _moe(sched, hs, ms, w_up, w_gate, w_down):
    steps = EXP_PER_GROUP // MOE_EPS
    wmap = lambda v, j, sched: (
        sched[V_GROUP, v] * steps + jnp.where(sched[V_VALID, v] == 1, j, steps - 1), 0, 0)
    tmap = lambda v, j, sched: (sched[V_TILE, v], 0)
    n_slab = D_MODEL // LANE
    return pl.pallas_call(
        _moe_kernel,
        grid_spec=pltpu.PrefetchScalarGridSpec(
            num_scalar_prefetch=1,
            grid=(N_VISITS, steps),
            in_specs=[
                pl.BlockSpec((TM_MOE, D_MODEL), tmap),
                pl.BlockSpec((TM_MOE, LANE), tmap),
                pl.BlockSpec((MOE_EPS, D_MODEL, D_EXPERT), wmap),
                pl.BlockSpec((MOE_EPS, D_MODEL, D_EXPERT), wmap),
                pl.BlockSpec((MOE_EPS, D_EXPERT, D_MODEL), wmap),
            ],
            out_specs=pl.BlockSpec((TM_MOE * n_slab, LANE), tmap),
            scratch_shapes=[pltpu.VMEM((TM_MOE, D_MODEL), F32)],
        ),
        out_shape=jax.ShapeDtypeStruct((T * n_slab, LANE), F32),
        compiler_params=pltpu.CompilerParams(
            dimension_semantics=("arbitrary", "arbitrary"), vmem_limit_bytes=VMEM_LIMIT),
        name="moe_grouped",
    )(sched, hs, ms, w_up, w_gate, w_down)


def _final_kernel(src_ref, ys_ref, x1_ref, mod_ref, fn_ref, yp_ref, ysm_ref, g_ref):
    i = pl.program_id(0)

    @pl.when(i < DISP_OUT)
    def _():
        base = i * TM_MOE

        def body(r, carry):
            g_ref[_slab(src_ref[base + r]), :] = ys_ref[_slab(r), :]
            return carry

        lax.fori_loop(0, TM_MOE, body, 0, unroll=8)

    k = i - DISP_OUT
    is_sample = k >= T_P // TM_FINAL

    def token_tile():
        row0 = pl.multiple_of(k * TM_FINAL, TM_FINAL)
        mod = mod_ref[pl.ds(_mod_row(k, TM_FINAL), 1), :]
        gate2 = mod[:, 5 * D_MODEL:6 * D_MODEL]
        moe = jnp.concatenate(
            [g_ref[pl.ds(row0 * N_SLAB + c, TM_FINAL, stride=N_SLAB), :] for c in range(N_SLAB)], axis=1)
        x2 = x1_ref[...] + gate2 * moe
        return (x2 * _rms(x2)) * fn_ref[...]

    @pl.when((k >= 0) & jnp.logical_not(is_sample))
    def _():
        yp_ref[...] = token_tile()

    @pl.when((k >= 0) & is_sample)
    def _():
        ysm_ref[...] = token_tile()


def _final(src, ys, x1, mod, final_norm):
    n_slab = D_MODEL // LANE
    n_prompt = T_P // TM_FINAL
    tok = lambda i: jnp.maximum(i - DISP_OUT, 0)
    return pl.pallas_call(
        _final_kernel,
        grid_spec=pltpu.PrefetchScalarGridSpec(
            num_scalar_prefetch=1,
            grid=(DISP_OUT + T // TM_FINAL,),
            in_specs=[
                pl.BlockSpec((TM_MOE * n_slab, LANE), lambda i, src: (jnp.minimum(i, DISP_OUT - 1), 0)),
                pl.BlockSpec((TM_FINAL, D_MODEL), lambda i, src: (tok(i), 0)),
                pl.BlockSpec((N_COND, 6 * D_MODEL), lambda i, src: (0, 0)),
                pl.BlockSpec((1, D_MODEL), lambda i, src: (0, 0)),
            ],
            out_specs=[pl.BlockSpec((TM_FINAL, D_MODEL), lambda i, src: (jnp.minimum(tok(i), n_prompt - 1), 0)),
                       pl.BlockSpec((TM_FINAL, D_MODEL), lambda i, src: (jnp.maximum(tok(i) - n_prompt, 0), 0))],
            scratch_shapes=[pltpu.VMEM((T * n_slab, LANE), F32)],
        ),
        out_shape=[jax.ShapeDtypeStruct((T_P, D_MODEL), F32),
                   jax.ShapeDtypeStruct((T_S, D_MODEL), F32)],
        compiler_params=pltpu.CompilerParams(
            dimension_semantics=("arbitrary",), vmem_limit_bytes=VMEM_LIMIT),
        name="moe_unsort_final",
    )(src, ys, x1, mod, final_norm)


def _plan_kernel(gr_ref, cnt_ref, pos_ref, sched_ref):
    n_post = T // TM_POST
    lane = lax.broadcasted_iota(jnp.int32, (1, LANE), 1)
    grp_lane = lambda v, g: v[:, N_EXPERTS + g:N_EXPERTS + g + 1]
    counts = [cnt_ref[k, 0:1, :] for k in range(n_post)]
    gtot_v = counts[0]
    for k in range(1, n_post):
        gtot_v = gtot_v + counts[k]
    gtot = [grp_lane(gtot_v, g) for g in range(N_GROUPS)]
    goff = [jnp.zeros((1, 1), F32)]
    for g in range(1, N_GROUPS):
        goff.append(goff[-1] + gtot[g - 1])

    before = [jnp.zeros((1, 1), F32) for _ in range(N_GROUPS)]
    rows_per_tile = TM_POST // LANE
    for k in range(n_post):
        gid = gr_ref[0:1, TM_POST * k:TM_POST * (k + 1)]
        pos = gr_ref[1:2, TM_POST * k:TM_POST * (k + 1)]
        for g in range(N_GROUPS):
            pos = pos + jnp.where(gid == float(g), goff[g] + before[g], 0.0)
            before[g] = before[g] + grp_lane(counts[k], g)
        for r in range(rows_per_tile):
            pos_ref[rows_per_tile * k + r:rows_per_tile * k + r + 1, :] = (
                pos[:, LANE * r:LANE * (r + 1)].astype(jnp.int32))

    zero = jnp.zeros((1, LANE), F32)
    rows = {name: zero for name in ("vt", "vg", "vlo", "vhi")}
    slot = jnp.zeros((1, 1), F32)
    last_t = jnp.zeros((1, 1), F32)
    last_g = jnp.zeros((1, 1), F32)
    for i in range(T // TM_MOE):
        for g in range(N_GROUPS):
            lo = jnp.clip(goff[g] - float(TM_MOE * i), 0.0, float(TM_MOE))
            hi = jnp.clip(goff[g] + gtot[g] - float(TM_MOE * i), 0.0, float(TM_MOE))
            ok = hi > lo
            here = ok & (lane == slot.astype(jnp.int32))
            rows["vt"] = jnp.where(here, float(i), rows["vt"])
            rows["vg"] = jnp.where(here, float(g), rows["vg"])
            rows["vlo"] = jnp.where(here, lo, rows["vlo"])
            rows["vhi"] = jnp.where(here, hi, rows["vhi"])
            last_t = jnp.where(ok, float(i), last_t)
            last_g = jnp.where(ok, float(g), last_g)
            slot = slot + jnp.where(ok, 1.0, 0.0)
    n_ok = slot.astype(jnp.int32)
    valid = lane < n_ok
    vt = jnp.where(valid, rows["vt"], last_t)
    vg = jnp.where(valid, rows["vg"], last_g)
    prev_t = pltpu.roll(jnp.broadcast_to(vt, (8, LANE)), 1, 1)[0:1, :]
    next_t = pltpu.roll(jnp.broadcast_to(vt, (8, LANE)), LANE - 1, 1)[0:1, :]
    first = jnp.where((lane == 0) | (vt != prev_t), 1.0, 0.0)
    last = jnp.where((lane == n_ok - 1) | (vt != next_t), 1.0, 0.0)
    table = [vt, vg, rows["vlo"], rows["vhi"], first, last, jnp.where(valid, 1.0, 0.0), zero]
    for r, row in enumerate(table):
        sched_ref[r:r + 1, :] = row.astype(jnp.int32)


V_TILE, V_GROUP, V_LO, V_HI, V_FIRST, V_LAST, V_VALID = range(7)


def _plan(gr, cnt):
    pos, sched = pl.pallas_call(
        _plan_kernel,
        out_shape=[jax.ShapeDtypeStruct((T // LANE, LANE), jnp.int32),
                   jax.ShapeDtypeStruct((8, LANE), jnp.int32)],
        name="moe_plan",
    )(gr, cnt)
    return pos.reshape(T), sched


def _rope_tables():
    n = np.arange(DEC_SEQ)
    pos = np.stack([n // GRID_W, n % GRID_W], axis=1).astype(np.float32)
    half = ROPE_AXIS // 2
    inv = (1.0 / (ROPE_BASE ** (np.arange(0, ROPE_AXIS, 2, dtype=np.float32) / ROPE_AXIS))).astype(np.float32)
    ang = (pos[:, :, None] * inv[None, None, :]).astype(np.float32)
    cos = np.cos(ang).astype(np.float32)
    sin = np.sin(ang).astype(np.float32)
    tabs = np.zeros((2, 3, DEC_SEQ, LANE), np.float32)
    tabs[:, 0] = 1.0
    for a in range(2):
        lo = ROPE_LANE0 + a * ROPE_AXIS
        tabs[1, 0, :, lo:lo + half] = cos[:, a]
        tabs[1, 0, :, lo + half:lo + 2 * half] = cos[:, a]
        tabs[1, 1, :, lo + half:lo + 2 * half] = sin[:, a]
        tabs[1, 2, :, lo:lo + half] = -sin[:, a]
    return jnp.asarray(tabs)


def kernel(x_prompt, x_sample, cache_ckv, cache_krope, c, c_ctx, norm1, w_ada, b_ada, w_in, conv_w,
           w_conv_out, q_norm, w_uq, kv_norm, w_ukv, w_o, w_mix_out, norm2, w_grp, w_exp, w_up,
           w_gate, w_down, final_norm):
    l = 0
    xp = x_prompt.reshape(T_P, D_MODEL)
    xs = x_sample.reshape(T_S, D_MODEL)
    mod = _ada(c_ctx[None, :], c, w_ada[l], b_ada[l][None, :])

    w_in_t = w_in[l].T
    w_uq_slot = jnp.pad(w_uq[l].reshape(Q_LORA, N_HEADS, QK_NOPE + QK_ROPE),
                        ((0, 0), (0, 0), (0, LANE - QK_NOPE - QK_ROPE))
                        ).reshape(Q_LORA, N_HEADS * LANE).astype(BF16)
    wkv = w_ukv[l].reshape(KV_LORA, N_HEADS, QK_NOPE + V_HEAD)
    wk_slot = jnp.pad(wkv[:, :, :QK_NOPE], ((0, 0), (0, 0), (0, LANE - QK_NOPE)))
    wv = wkv[:, :, QK_NOPE:].reshape(KV_LORA, N_HEADS // 2, 2, V_HEAD)
    zero = jnp.zeros_like(wv[:, :, 0])
    wv_slot = jnp.stack([jnp.concatenate([wv[:, :, 0], zero], axis=-1),
                         jnp.concatenate([zero, wv[:, :, 1]], axis=-1)], axis=2)
    w_ukv_slot = jnp.concatenate([wk_slot.reshape(KV_LORA, N_HEADS * LANE),
                                  wv_slot.reshape(KV_LORA, N_HEADS * LANE)], axis=1).astype(BF16)
    w_route = jnp.pad(jnp.concatenate([w_exp[l], w_grp[l]], axis=1),
                      ((0, 0), (0, LANE - N_EXPERTS - N_GROUPS)))
    w_route_hi = w_route.astype(BF16)
    w_route_lo = (w_route - w_route_hi.astype(F32)).astype(BF16)
    w_route2 = jnp.concatenate([w_route_hi, w_route_lo], axis=1)
    cache_krs = jnp.pad(cache_krope[:, l], ((0, 0), (0, 0), (ROPE_LANE0, LANE - ROPE_LANE0 - QK_ROPE)))

    zc, q, ckv, krs, nckv, nkr = _inproj(xp, xs, mod, norm1[l][None, :], w_in_t, conv_w[l],
                              q_norm[l][None, :], kv_norm[l][None, :], w_uq_slot, _rope_tables())
    o_p, o_s, w_gates_t, w_co_b, w_o_b, w_mix_b = _attention(
        q, ckv, krs, cache_ckv[:, l], cache_krs, w_ukv_slot, w_in_t, w_conv_out[l], w_o[l], w_mix_out[l])
    x1, h3, meta, cnt, gr = _post(xp, xs, mod, norm1[l][None, :], w_gates_t, zc, o_p, o_s, w_co_b, w_o_b,
                                  w_mix_b, norm2[l][None, :], w_route2)
    pos, sched = _plan(gr, cnt)
    hs, ms, src = _dispatch(pos, h3, meta)
    ys = _moe(sched, hs, ms, w_up[l], w_gate[l], w_down[l])
    yp, ysm = _final(src, ys, x1, mod, final_norm[None, :])

    y_prompt = yp.reshape(BATCH, SEQ, D_MODEL)
    y_sample = ysm.reshape(DEC_BATCH, DEC_SEQ, D_MODEL)
    new_ckv = nckv.reshape(BATCH, 1, SEQ, KV_LORA)
    new_krope = jnp.swapaxes(nkr, 1, 2).reshape(BATCH, 1, SEQ, QK_ROPE)
    return (y_prompt, y_sample, new_ckv, new_krope)
```

```python
import numpy as np
import jax
import jax.numpy as jnp
from jax import lax
from jax.experimental import pallas as pl
from jax.experimental.pallas import tpu as pltpu

F32 = jnp.float32
BF16 = jnp.bfloat16

D_MODEL = 1024
BATCH = 16
SEQ = 256
DEC_BATCH = 2
DEC_SEQ = 1024
PAST_LEN = 256
GRID_W = 64
N_HEADS = 8
QK_NOPE = 64
QK_ROPE = 32
V_HEAD = 64
Q_LORA = 256
KV_LORA = 128
ROPE_AXIS = QK_ROPE // 2
ROPE_BASE = 10000.0
ATTN_SCALE = (QK_NOPE + QK_ROPE) ** -0.5
D_CONV = D_MODEL
N_GROUPS = 4
EXP_PER_GROUP = 8
N_EXPERTS = N_GROUPS * EXP_PER_GROUP
D_EXPERT = 256
EPS = 1e-6

T_P = BATCH * SEQ
T_S = DEC_BATCH * DEC_SEQ
T = T_P + T_S
N_COND = 8
LANE = 128
ROPE_LANE0 = QK_NOPE
SMALL_COLS = Q_LORA + KV_LORA + LANE
VMEM_LIMIT = 56 * 1024 * 1024

TM_IN = 1024
TM_POST = 512
TM_MOE = T // N_GROUPS
TM_FINAL = 512
TM_DISP = 1024
N_SLAB = D_MODEL // LANE
MOE_SUB = 256
MOE_EPS = 2
N_VISITS = T // TM_MOE + N_GROUPS - 1
GID_LANE = 40
RANK_LANE = 41
Q_BLK_S = 512
PROMPT_SEQS = 2
CONV_CHUNK = 256


def _dot(a, b):
    return jnp.dot(a, b, preferred_element_type=F32)


def _rms(x):
    return lax.rsqrt(jnp.mean(x * x, axis=-1, keepdims=True) + EPS)


def _slab(t):
    return pl.ds(pl.multiple_of(t * N_SLAB, N_SLAB), N_SLAB)


def _mod_row(i, tm):
    n_prompt = T_P // tm
    return jnp.where(i >= n_prompt, 1 + ((i - n_prompt) * tm) // DEC_SEQ, 0)


def _ada_kernel(cctx_ref, c_ref, w_ref, b_ref, o_ref):
    c = jnp.concatenate([cctx_ref[...], c_ref[...],
                         jnp.zeros((N_COND - 1 - DEC_BATCH, D_MODEL), F32)], axis=0)
    a = (c * jax.nn.sigmoid(c)).astype(BF16)
    o_ref[...] = _dot(a, w_ref[...].astype(BF16)) + b_ref[...]


def _ada(c_ctx, c, w_ada, b_ada):
    n = 6 * D_MODEL
    bn = 1536
    return pl.pallas_call(
        _ada_kernel,
        grid=(n // bn,),
        in_specs=[
            pl.BlockSpec((1, D_MODEL), lambda j: (0, 0)),
            pl.BlockSpec((DEC_BATCH, D_MODEL), lambda j: (0, 0)),
            pl.BlockSpec((D_MODEL, bn), lambda j: (0, j)),
            pl.BlockSpec((1, bn), lambda j: (0, j)),
        ],
        out_specs=pl.BlockSpec((N_COND, bn), lambda j: (0, j)),
        out_shape=jax.ShapeDtypeStruct((N_COND, n), F32),
        compiler_params=pltpu.CompilerParams(
            dimension_semantics=("parallel",), vmem_limit_bytes=VMEM_LIMIT),
        name="ada_mod",
    )(c_ctx, c, w_ada, b_ada)


O_CQ = 3 * D_CONV
O_KR = O_CQ + Q_LORA + KV_LORA
O_GATE = O_KR + QK_ROPE
SMALL_BLK = 512
NT = (((1,), (1,)), ((), ()))


def _dot_nt(a, bt):
    return lax.dot_general(a, bt, NT, preferred_element_type=F32)


def _stream_maps(tm):
    n_prompt = T_P // tm
    return (lambda i, *_: (jnp.minimum(i, n_prompt - 1), 0),
            lambda i, *_: (jnp.maximum(i - n_prompt, 0), 0))


def _inproj_kernel(xp_ref, xs_ref, mod_ref, n1_ref, wc_ref, ws_ref, cw_ref, qn_ref, kvn_ref, wuq_ref,
                   rope_ref, zc_ref, q_ref, ckv_ref, krs_ref, nckv_ref, nkr_ref):
    i = pl.program_id(0)
    is_sample = i >= T_P // TM_IN
    seq = jnp.where(is_sample, DEC_SEQ, SEQ)
    mod = mod_ref[pl.ds(_mod_row(i, TM_IN), 1), :]
    shift1 = mod[:, 0:D_MODEL]
    scale1 = mod[:, D_MODEL:2 * D_MODEL]
    x = jnp.where(is_sample, xs_ref[...], xp_ref[...])
    h = ((x * _rms(x)) * n1_ref[...]) * (1.0 + scale1) + shift1
    hb = h.astype(BF16)

    n_lat = Q_LORA + KV_LORA
    w_small = jnp.concatenate([
        ws_ref[0:n_lat, :].astype(BF16),
        jnp.zeros((ROPE_LANE0, D_MODEL), BF16),
        ws_ref[n_lat:n_lat + QK_ROPE, :].astype(BF16),
        jnp.zeros((LANE - ROPE_LANE0 - QK_ROPE, D_MODEL), BF16)], axis=0)
    sm = _dot_nt(hb, w_small)
    cq = sm[:, 0:Q_LORA]
    ckv_raw = sm[:, Q_LORA:Q_LORA + KV_LORA]
    krs = sm[:, Q_LORA + KV_LORA:SMALL_COLS]
    cqn = (cq * _rms(cq)) * qn_ref[...]
    q = _dot(cqn.astype(BF16), wuq_ref[...])
    ckv = (ckv_raw * _rms(ckv_raw)) * kvn_ref[...]
    ckv_ref[...] = ckv

    @pl.when(jnp.logical_not(is_sample))
    def _():
        nckv_ref[...] = ckv
        for s in range(TM_IN // SEQ):
            kt = krs[s * SEQ:(s + 1) * SEQ, :].T
            nkr_ref[s] = kt[ROPE_LANE0:ROPE_LANE0 + QK_ROPE, :]

    cos = rope_ref[0]
    sin_lo = rope_ref[1]
    sin_hi = rope_ref[2]

    def rot(v):
        return v * cos + pltpu.roll(v, 8, 1) * sin_lo + pltpu.roll(v, LANE - 8, 1) * sin_hi

    krs_ref[...] = rot(krs)
    for hh in range(N_HEADS):
        q_ref[:, LANE * hh:LANE * (hh + 1)] = rot(q[:, LANE * hh:LANE * (hh + 1)]).astype(BF16)

    pos = lax.broadcasted_iota(jnp.int32, (TM_IN, 1), 0) & (seq - 1)
    first = pos == 0
    last = pos == seq - 1
    for j in range(D_CONV // CONV_CHUNK):
        c0 = j * CONV_CHUNK
        bg = _dot_nt(hb, wc_ref[c0:c0 + CONV_CHUNK, :].astype(BF16))
        cg = _dot_nt(hb, wc_ref[D_CONV + c0:D_CONV + c0 + CONV_CHUNK, :].astype(BF16))
        ui = _dot_nt(hb, wc_ref[2 * D_CONV + c0:2 * D_CONV + c0 + CONV_CHUNK, :].astype(BF16))
        u = cg * ui
        u_prev = jnp.where(first, 0.0, pltpu.roll(u, 1, 0))
        u_next = jnp.where(last, 0.0, pltpu.roll(u, TM_IN - 1, 0))
        cw = cw_ref[:, c0:c0 + CONV_CHUNK]
        conv = u_prev * cw[0:1] + u * cw[1:2] + u_next * cw[2:3]
        zc_ref[:, c0:c0 + CONV_CHUNK] = (bg * conv).astype(BF16)


def _inproj(xp, xs, mod, norm1, w_in_t, conv_w, q_norm, kv_norm, w_uq_slot, rope_tabs):
    n_prompt = T_P // TM_IN
    const = lambda i: (0, 0)
    pmap, smap = _stream_maps(TM_IN)
    once = pl.Buffered(1)
    return pl.pallas_call(
        _inproj_kernel,
        grid=(T // TM_IN,),
        in_specs=[
            pl.BlockSpec((TM_IN, D_MODEL), pmap),
            pl.BlockSpec((TM_IN, D_MODEL), smap),
            pl.BlockSpec((N_COND, 6 * D_MODEL), const),
            pl.BlockSpec((1, D_MODEL), const),
            pl.BlockSpec((O_CQ, D_MODEL), const, pipeline_mode=once),
            pl.BlockSpec((SMALL_BLK, D_MODEL), lambda i: (O_CQ // SMALL_BLK, 0), pipeline_mode=once),
            pl.BlockSpec((3, D_CONV), const),
            pl.BlockSpec((1, Q_LORA), const),
            pl.BlockSpec((1, KV_LORA), const),
            pl.BlockSpec((Q_LORA, N_HEADS * LANE), const),
            pl.BlockSpec((None, 3, TM_IN, LANE),
                         lambda i: (jnp.where(i >= n_prompt, 1, 0), 0, 0, 0)),
        ],
        out_specs=[
            pl.BlockSpec((TM_IN, D_CONV), lambda i: (i, 0)),
            pl.BlockSpec((TM_IN, N_HEADS * LANE), lambda i: (i, 0)),
            pl.BlockSpec((TM_IN, KV_LORA), lambda i: (i, 0)),
            pl.BlockSpec((TM_IN, LANE), lambda i: (i, 0)),
            pl.BlockSpec((TM_IN, KV_LORA), pmap),
            pl.BlockSpec((TM_IN // SEQ, QK_ROPE, SEQ), lambda i: (jnp.minimum(i, n_prompt - 1), 0, 0)),
        ],
        out_shape=[
            jax.ShapeDtypeStruct((T, D_CONV), BF16),
            jax.ShapeDtypeStruct((T, N_HEADS * LANE), BF16),
            jax.ShapeDtypeStruct((T, KV_LORA), F32),
            jax.ShapeDtypeStruct((T, LANE), F32),
            jax.ShapeDtypeStruct((T_P, KV_LORA), F32),
            jax.ShapeDtypeStruct((BATCH, QK_ROPE, SEQ), F32),
        ],
        compiler_params=pltpu.CompilerParams(
            dimension_semantics=("arbitrary",), vmem_limit_bytes=VMEM_LIMIT),
        name="in_proj",
    )(xp, xs, mod, norm1, w_in_t, w_in_t, conv_w, q_norm, kv_norm, w_uq_slot, rope_tabs)


def _fill_kv(ckv, krs, wukv_ref, kf_scr, v_scr, off):
    m = ckv.shape[0]
    kv = _dot(ckv.astype(BF16), wukv_ref[...])
    for hh in range(N_HEADS):
        kf_scr[hh, off:off + m, :] = (kv[:, LANE * hh:LANE * (hh + 1)] + krs).astype(BF16)
    v_scr[off:off + m, :] = kv[:, N_HEADS * LANE:].astype(BF16)


def _attend(q_ref, r0, rows, kf_scr, v_scr, o_ref):
    for pair in range(N_HEADS // 2):
        acc = None
        for hh in (2 * pair, 2 * pair + 1):
            qh = q_ref[r0:r0 + rows, LANE * hh:LANE * (hh + 1)]
            s = _dot_nt(qh, kf_scr[hh]) * ATTN_SCALE
            e = jnp.exp(s - jnp.max(s, axis=-1, keepdims=True))
            p = (e / jnp.sum(e, axis=-1, keepdims=True)).astype(BF16)
            part = _dot(p, v_scr[:, LANE * hh:LANE * (hh + 1)])
            acc = part if acc is None else acc + part
        o_ref[r0:r0 + rows, LANE * pair:LANE * (pair + 1)] = acc.astype(BF16)


def _attn_prompt_kernel(q_ref, ckv_ref, krs_ref, wukv_ref, o_ref):
    for s in range(PROMPT_SEQS):
        r0 = s * SEQ
        kv = _dot(ckv_ref[r0:r0 + SEQ, :].astype(BF16), wukv_ref[...])
        krs = krs_ref[r0:r0 + SEQ, :]
        for pair in range(N_HEADS // 2):
            acc = None
            for hh in (2 * pair, 2 * pair + 1):
                kf = (kv[:, LANE * hh:LANE * (hh + 1)] + krs).astype(BF16)
                vh = kv[:, LANE * (N_HEADS + hh):LANE * (N_HEADS + hh + 1)].astype(BF16)
                sc = _dot_nt(q_ref[r0:r0 + SEQ, LANE * hh:LANE * (hh + 1)], kf) * ATTN_SCALE
                e = jnp.exp(sc - jnp.max(sc, axis=-1, keepdims=True))
                p = (e / jnp.sum(e, axis=-1, keepdims=True)).astype(BF16)
                part = _dot(p, vh)
                acc = part if acc is None else acc + part
            o_ref[r0:r0 + SEQ, LANE * pair:LANE * (pair + 1)] = acc.astype(BF16)


def _attn_sample_kernel(q_ref, ckv_ref, cckv_ref, krs_ref, ckrs_ref, wukv_ref, wg_ref, wco_ref, wo_ref,
                        wmix_ref, o_ref, g_ref, co_ref, ob_ref, mix_ref, kf_scr, v_scr):
    @pl.when(pl.program_id(1) == 0)
    def _():
        _fill_kv(ckv_ref[...], krs_ref[...], wukv_ref, kf_scr, v_scr, 0)
        _fill_kv(cckv_ref[...], ckrs_ref[...], wukv_ref, kf_scr, v_scr, DEC_SEQ)

    _attend(q_ref, 0, Q_BLK_S, kf_scr, v_scr, o_ref)
    g_ref[...] = wg_ref[...].astype(BF16)
    co_ref[...] = wco_ref[...].astype(BF16)
    ob_ref[...] = wo_ref[...].astype(BF16)
    mix_ref[...] = wmix_ref[...].astype(BF16)


def _attention(q, ckv, krs, cache_ckv, cache_krs, w_ukv_slot, w_in_t, w_conv_out, w_o, w_mix_out):
    kv_cols = 2 * N_HEADS * LANE
    n_o = N_HEADS * V_HEAD
    steps = BATCH // PROMPT_SEQS
    rows = PROMPT_SEQS * SEQ
    o_prompt = pl.pallas_call(
        _attn_prompt_kernel,
        grid=(steps,),
        in_specs=[
            pl.BlockSpec((rows, N_HEADS * LANE), lambda b: (b, 0)),
            pl.BlockSpec((rows, KV_LORA), lambda b: (b, 0)),
            pl.BlockSpec((rows, LANE), lambda b: (b, 0)),
            pl.BlockSpec((KV_LORA, kv_cols), lambda b: (0, 0)),
        ],
        out_specs=pl.BlockSpec((rows, n_o), lambda b: (b, 0)),
        out_shape=jax.ShapeDtypeStruct((T_P, n_o), BF16),
        compiler_params=pltpu.CompilerParams(dimension_semantics=("parallel",),
                                             vmem_limit_bytes=VMEM_LIMIT),
        name="attn_prompt",
    )(q, ckv, krs, w_ukv_slot)

    m_all = DEC_SEQ + PAST_LEN
    nq = DEC_SEQ // Q_BLK_S
    q0 = T_P // Q_BLK_S
    s0 = T_P // DEC_SEQ
    n_step = DEC_BATCH * nq
    share = lambda n: pl.BlockSpec((n // n_step, D_MODEL), lambda b, j: (b * nq + j, 0))
    gate_rows = 2 * D_MODEL // n_step
    o_sample, w_gates_t, w_co_b, w_o_b, w_mix_b = pl.pallas_call(
        _attn_sample_kernel,
        grid=(DEC_BATCH, nq),
        in_specs=[
            pl.BlockSpec((Q_BLK_S, N_HEADS * LANE), lambda b, j: (q0 + b * nq + j, 0)),
            pl.BlockSpec((DEC_SEQ, KV_LORA), lambda b, j: (s0 + b, 0)),
            pl.BlockSpec((None, PAST_LEN, KV_LORA), lambda b, j: (b, 0, 0)),
            pl.BlockSpec((DEC_SEQ, LANE), lambda b, j: (s0 + b, 0)),
            pl.BlockSpec((None, PAST_LEN, LANE), lambda b, j: (b, 0, 0)),
            pl.BlockSpec((KV_LORA, kv_cols), lambda b, j: (0, 0)),
            pl.BlockSpec((pl.Element(gate_rows), pl.Element(D_MODEL)),
                         lambda b, j: (pl.multiple_of(O_GATE + (b * nq + j) * gate_rows, 32), 0)),
            share(D_CONV), share(n_o), share(D_MODEL),
        ],
        out_specs=[pl.BlockSpec((Q_BLK_S, n_o), lambda b, j: (b * nq + j, 0)),
                   share(2 * D_MODEL), share(D_CONV), share(n_o), share(D_MODEL)],
        out_shape=[jax.ShapeDtypeStruct((T_S, n_o), BF16),
                   jax.ShapeDtypeStruct((2 * D_MODEL, D_MODEL), BF16),
                   jax.ShapeDtypeStruct((D_CONV, D_MODEL), BF16),
                   jax.ShapeDtypeStruct((n_o, D_MODEL), BF16),
                   jax.ShapeDtypeStruct((D_MODEL, D_MODEL), BF16)],
        scratch_shapes=[pltpu.VMEM((N_HEADS, m_all, LANE), BF16),
                        pltpu.VMEM((m_all, N_HEADS * LANE), BF16)],
        compiler_params=pltpu.CompilerParams(dimension_semantics=("arbitrary", "arbitrary"),
                                             vmem_limit_bytes=VMEM_LIMIT),
        name="attn_sample",
    )(q, ckv, cache_ckv, krs, cache_krs, w_ukv_slot, w_in_t, w_conv_out, w_o, w_mix_out)
    return o_prompt, o_sample, w_gates_t, w_co_b, w_o_b, w_mix_b


def _route(logits):
    lane = lax.broadcasted_iota(jnp.int32, logits.shape, 1)
    neg = -jnp.inf
    big = jnp.int32(1 << 20)
    gmask = (lane >= N_EXPERTS) & (lane < N_EXPERTS + N_GROUPS)
    gl = jnp.where(gmask, logits, neg)
    gmax = jnp.max(gl, axis=-1, keepdims=True)
    gsum = jnp.sum(jnp.where(gmask, jnp.exp(gl - gmax), 0.0), axis=-1, keepdims=True)
    p_g = 1.0 / gsum
    g_idx = jnp.min(jnp.where(gl == gmax, lane, big), axis=-1, keepdims=True) - N_EXPERTS

    emask = (lane < N_EXPERTS) & ((lane >> 3) == g_idx)
    el = jnp.where(emask, logits, neg)
    m1 = jnp.max(el, axis=-1, keepdims=True)
    i1 = jnp.min(jnp.where(el == m1, lane, big), axis=-1, keepdims=True)
    el2 = jnp.where(lane == i1, neg, el)
    m2 = jnp.max(el2, axis=-1, keepdims=True)
    i2 = jnp.min(jnp.where(el2 == m2, lane, big), axis=-1, keepdims=True)
    z = jnp.sum(jnp.where(emask, jnp.exp(el - m1), 0.0), axis=-1, keepdims=True)
    p1 = 1.0 / z
    p2 = jnp.exp(m2 - m1) / z
    tot = p1 + p2
    w1 = p_g * p1 / tot
    w2 = p_g * p2 / tot
    return jnp.where(lane == i1, w1, 0.0) + jnp.where(lane == i2, w2, 0.0), g_idx


def _post_kernel(xp_ref, xs_ref, mod_ref, n1_ref, wg_ref, zc_ref, op_ref, os_ref, wco_ref, wo_ref,
                 wmix_ref, n2_ref, wr_ref, x1_ref, h3_ref, meta_ref, cnt_ref, gr_ref):
    i = pl.program_id(0)
    is_sample = i >= T_P // TM_POST
    x = jnp.where(is_sample, xs_ref[...], xp_ref[...])
    o = jnp.where(is_sample, os_ref[...], op_ref[...])
    mod = mod_ref[pl.ds(_mod_row(i, TM_POST), 1), :]
    shift1 = mod[:, 0:D_MODEL]
    scale1 = mod[:, D_MODEL:2 * D_MODEL]
    gate1 = mod[:, 2 * D_MODEL:3 * D_MODEL]
    shift2 = mod[:, 3 * D_MODEL:4 * D_MODEL]
    scale2 = mod[:, 4 * D_MODEL:5 * D_MODEL]
    y_conv = _dot(zc_ref[...], wco_ref[...])
    y_mla = _dot(o, wo_ref[...])
    h = ((x * _rms(x)) * n1_ref[...]) * (1.0 + scale1) + shift1
    g = _dot_nt(h.astype(BF16), wg_ref[...])
    merged = (jax.nn.sigmoid(g[:, 0:D_MODEL]) * y_conv
              + jax.nn.sigmoid(g[:, D_MODEL:2 * D_MODEL]) * y_mla)
    y = _dot(merged.astype(BF16), wmix_ref[...])
    x1 = x + gate1 * y
    x1_ref[...] = x1
    h2 = ((x1 * _rms(x1)) * n2_ref[...]) * (1.0 + scale2) + shift2
    h2_hi = h2.astype(BF16)
    h2_lo = (h2 - h2_hi.astype(F32)).astype(BF16)
    hh = _dot(h2_hi, wr_ref[...])
    logits = hh[:, 0:LANE] + hh[:, LANE:2 * LANE] + _dot(h2_lo, wr_ref[:, 0:LANE])
    comb, g_idx = _route(logits)

    lane = lax.broadcasted_iota(jnp.int32, comb.shape, 1)
    onehot = lane == g_idx + N_EXPERTS
    r_i = lax.broadcasted_iota(jnp.int32, (TM_POST, TM_POST), 0)
    c_i = lax.broadcasted_iota(jnp.int32, (TM_POST, TM_POST), 1)
    lower = jnp.where(c_i < r_i, 1.0, 0.0).astype(BF16)
    before = _dot(lower, jnp.where(onehot, 1.0, 0.0).astype(BF16))
    rank = jnp.sum(jnp.where(onehot, before, 0.0), axis=-1, keepdims=True)
    counts = jnp.sum(jnp.where(onehot, 1.0, 0.0), axis=0, keepdims=True)
    cnt_ref[...] = jnp.broadcast_to(counts, cnt_ref.shape)

    meta_ref[...] = comb
    idx = jnp.where(lane == GID_LANE, g_idx.astype(F32), 0.0) + jnp.where(lane == RANK_LANE, rank, 0.0)
    idx_hi = idx.astype(BF16)
    idx_lo = (idx - idx_hi.astype(F32)).astype(BF16)
    s_row = lax.broadcasted_iota(jnp.int32, (8, LANE), 0)
    s_lane = lax.broadcasted_iota(jnp.int32, (8, LANE), 1)
    sel = jnp.where(s_lane == GID_LANE + s_row, 1.0, 0.0).astype(BF16)
    gr_ref[...] = _dot_nt(sel, idx_hi) + _dot_nt(sel, idx_lo)
    for c in range(N_SLAB):
        h3_ref[pl.ds(c, TM_POST, stride=N_SLAB), :] = h2[:, LANE * c:LANE * (c + 1)]


def _post(xp, xs, mod, norm1, w_gates_t, zc, o_p, o_s, w_conv_out, w_o, w_mix_out, norm2, w_route):
    const = lambda i: (0, 0)
    row = lambda i: (i, 0)
    pmap, smap = _stream_maps(TM_POST)
    return pl.pallas_call(
        _post_kernel,
        grid=(T // TM_POST,),
        in_specs=[
            pl.BlockSpec((TM_POST, D_MODEL), pmap),
            pl.BlockSpec((TM_POST, D_MODEL), smap),
            pl.BlockSpec((N_COND, 6 * D_MODEL), const),
            pl.BlockSpec((1, D_MODEL), const),
            pl.BlockSpec((2 * D_MODEL, D_MODEL), const),
            pl.BlockSpec((TM_POST, D_CONV), row),
            pl.BlockSpec((TM_POST, N_HEADS * V_HEAD), pmap),
            pl.BlockSpec((TM_POST, N_HEADS * V_HEAD), smap),
            pl.BlockSpec((D_CONV, D_MODEL), const),
            pl.BlockSpec((N_HEADS * V_HEAD, D_MODEL), const),
            pl.BlockSpec((D_MODEL, D_MODEL), const),
            pl.BlockSpec((1, D_MODEL), const),
            pl.BlockSpec((D_MODEL, 2 * LANE), const),
        ],
        out_specs=[
            pl.BlockSpec((TM_POST, D_MODEL), row),
            pl.BlockSpec((TM_POST * N_SLAB, LANE), row),
            pl.BlockSpec((TM_POST, LANE), row),
            pl.BlockSpec((None, 8, LANE), lambda i: (i, 0, 0)),
            pl.BlockSpec((8, TM_POST), lambda i: (0, i)),
        ],
        out_shape=[
            jax.ShapeDtypeStruct((T, D_MODEL), F32),
            jax.ShapeDtypeStruct((T * N_SLAB, LANE), F32),
            jax.ShapeDtypeStruct((T, LANE), F32),
            jax.ShapeDtypeStruct((T // TM_POST, 8, LANE), F32),
            jax.ShapeDtypeStruct((8, T), F32),
        ],
        compiler_params=pltpu.CompilerParams(
            dimension_semantics=("parallel",), vmem_limit_bytes=VMEM_LIMIT),
        name="post_mixer",
    )(xp, xs, mod, norm1, w_gates_t, zc, o_p, o_s, w_conv_out, w_o, w_mix_out, norm2, w_route)


DISP_IN = T // TM_DISP
DISP_OUT = T // TM_MOE


def _dispatch_kernel(pos_ref, h3_ref, m_ref, hs_ref, ms_ref, xs_ref, mss_ref):
    i = pl.program_id(0)

    @pl.when(i < DISP_IN)
    def _():
        base = i * TM_DISP

        def body(r, carry):
            p = pos_ref[base + r]
            xs_ref[_slab(p), :] = h3_ref[_slab(r), :]
            mss_ref[pl.ds(p, 1), :] = m_ref[pl.ds(r, 1), :]
            return carry

        lax.fori_loop(0, TM_DISP, body, 0, unroll=8)

    @pl.when(i >= DISP_IN)
    def _():
        row0 = pl.multiple_of((i - DISP_IN) * TM_MOE, TM_MOE)
        for c in range(N_SLAB):
            hs_ref[:, LANE * c:LANE * (c + 1)] = (
                xs_ref[pl.ds(row0 * N_SLAB + c, TM_MOE, stride=N_SLAB), :].astype(BF16))
        ms_ref[...] = mss_ref[pl.ds(row0, TM_MOE), :]


def _dispatch(pos, h3, meta):
    n_slab = N_SLAB
    in_map = lambda i, pos: (jnp.minimum(i, DISP_IN - 1), 0)
    out_map = lambda i, pos: (jnp.maximum(i - DISP_IN, 0), 0)
    return pl.pallas_call(
        _dispatch_kernel,
        grid_spec=pltpu.PrefetchScalarGridSpec(
            num_scalar_prefetch=1,
            grid=(DISP_IN + DISP_OUT,),
            in_specs=[pl.BlockSpec((TM_DISP * n_slab, LANE), in_map),
                      pl.BlockSpec((TM_DISP, LANE), in_map)],
            out_specs=[pl.BlockSpec((TM_MOE, D_MODEL), out_map),
                       pl.BlockSpec((TM_MOE, LANE), out_map)],
            scratch_shapes=[pltpu.VMEM((T * n_slab, LANE), F32),
                            pltpu.VMEM((T, LANE), F32)],
        ),
        out_shape=[jax.ShapeDtypeStruct((T, D_MODEL), BF16),
                   jax.ShapeDtypeStruct((T, LANE), F32)],
        compiler_params=pltpu.CompilerParams(
            dimension_semantics=("arbitrary",), vmem_limit_bytes=VMEM_LIMIT),
        name="moe_dispatch",
    )(pos, h3, meta)


def _moe_kernel(sched_ref, hs_ref, ms_ref, wup_ref, wgate_ref, wdown_ref, y3_ref, acc_ref):
    v = pl.program_id(0)
    j = pl.program_id(1)
    valid = sched_ref[V_VALID, v] == 1
    lo = sched_ref[V_LO, v]
    hi = sched_ref[V_HI, v]
    e0 = sched_ref[V_GROUP, v] * EXP_PER_GROUP + j * MOE_EPS
    full = (hi - lo) * 4 >= TM_MOE * 3

    @pl.when(valid & (j == 0) & (sched_ref[V_FIRST, v] == 1))
    def _():
        acc_ref[...] = jnp.zeros_like(acc_ref)

    def expert_rows(r0, rows):
        w_in2 = jnp.concatenate(
            [w[k].astype(BF16) for k in range(MOE_EPS) for w in (wup_ref, wgate_ref)], axis=1)
        ag = _dot(hs_ref[pl.ds(r0, rows), :], w_in2)
        comb = ms_ref[pl.ds(r0, rows), :]
        lane = lax.broadcasted_iota(jnp.int32, comb.shape, 1)
        acts = []
        for k in range(MOE_EPS):
            a = ag[:, 2 * k * D_EXPERT:(2 * k + 1) * D_EXPERT]
            g = ag[:, (2 * k + 1) * D_EXPERT:(2 * k + 2) * D_EXPERT]
            cw = jnp.sum(jnp.where(lane == e0 + k, comb, 0.0), axis=-1, keepdims=True)
            acts.append(((g * jax.nn.sigmoid(g)) * a * cw).astype(BF16))
        w_out = jnp.concatenate([wdown_ref[k].astype(BF16) for k in range(MOE_EPS)], axis=0)
        acc_ref[pl.ds(r0, rows), :] += _dot(jnp.concatenate(acts, axis=1), w_out)

    @pl.when(valid & full)
    def _():
        expert_rows(0, TM_MOE)

    @pl.when(valid & jnp.logical_not(full))
    def _():
        def sub_block(s, carry):
            expert_rows(pl.multiple_of(s * MOE_SUB, MOE_SUB), MOE_SUB)
            return carry

        lax.fori_loop(lo // MOE_SUB, (hi + MOE_SUB - 1) // MOE_SUB, sub_block, 0)

    @pl.when(valid & (j == EXP_PER_GROUP // MOE_EPS - 1) & (sched_ref[V_LAST, v] == 1))
    def _():
        for c in range(N_SLAB):
            y3_ref[pl.ds(c, TM_MOE, stride=N_SLAB), :] = acc_ref[:, LANE * c:LANE * (c + 1)]


def _moe(sched, hs, ms, w_up, w_gate, w_down):
    steps = EXP_PER_GROUP // MOE_EPS
    wmap = lambda v, j, sched: (
        sched[V_GROUP, v] * steps + jnp.where(sched[V_VALID, v] == 1, j, steps - 1), 0, 0)
    tmap = lambda v, j, sched: (sched[V_TILE, v], 0)
    n_slab = D_MODEL // LANE
    return pl.pallas_call(
        _moe_kernel,
        grid_spec=pltpu.PrefetchScalarGridSpec(
            num_scalar_prefetch=1,
            grid=(N_VISITS, steps),
            in_specs=[
                pl.BlockSpec((TM_MOE, D_MODEL), tmap),
                pl.BlockSpec((TM_MOE, LANE), tmap),
                pl.BlockSpec((MOE_EPS, D_MODEL, D_EXPERT), wmap),
                pl.BlockSpec((MOE_EPS, D_MODEL, D_EXPERT), wmap),
                pl.BlockSpec((MOE_EPS, D_EXPERT, D_MODEL), wmap),
            ],
            out_specs=pl.BlockSpec((TM_MOE * n_slab, LANE), tmap),
            scratch_shapes=[pltpu.VMEM((TM_MOE, D_MODEL), F32)],
        ),
        out_shape=jax.ShapeDtypeStruct((T * n_slab, LANE), F32),
        compiler_params=pltpu.CompilerParams(
            dimension_semantics=("arbitrary", "arbitrary"), vmem_limit_bytes=VMEM_LIMIT),
        name="moe_grouped",
    )(sched, hs, ms, w_up, w_gate, w_down)


def _final_kernel(pos_ref, ys_ref, x1_ref, mod_ref, fn_ref, yp_ref, ysm_ref, g_ref):
    i = pl.program_id(0)
    base = i * TM_FINAL

    def body(r, carry):
        g_ref[_slab(r), :] = ys_ref[_slab(pos_ref[base + r]), :]
        return carry

    lax.fori_loop(0, TM_FINAL, body, 0, unroll=8)
    mod = mod_ref[pl.ds(_mod_row(i, TM_FINAL), 1), :]
    gate2 = mod[:, 5 * D_MODEL:6 * D_MODEL]
    moe = jnp.concatenate([g_ref[pl.ds(c, TM_FINAL, stride=N_SLAB), :] for c in range(N_SLAB)], axis=1)
    x2 = x1_ref[...] + gate2 * moe
    y = (x2 * _rms(x2)) * fn_ref[...]
    is_sample = i >= T_P // TM_FINAL

    @pl.when(jnp.logical_not(is_sample))
    def _():
        yp_ref[...] = y

    @pl.when(is_sample)
    def _():
        ysm_ref[...] = y


def _final(pos, ys, x1, mod, final_norm):
    n_slab = D_MODEL // LANE
    pmap, smap = _stream_maps(TM_FINAL)
    return pl.pallas_call(
        _final_kernel,
        grid_spec=pltpu.PrefetchScalarGridSpec(
            num_scalar_prefetch=1,
            grid=(T // TM_FINAL,),
            in_specs=[
                pl.BlockSpec((T * n_slab, LANE), lambda i, pos: (0, 0), pipeline_mode=pl.Buffered(1)),
                pl.BlockSpec((TM_FINAL, D_MODEL), lambda i, pos: (i, 0)),
                pl.BlockSpec((N_COND, 6 * D_MODEL), lambda i, pos: (0, 0)),
                pl.BlockSpec((1, D_MODEL), lambda i, pos: (0, 0)),
            ],
            out_specs=[pl.BlockSpec((TM_FINAL, D_MODEL), pmap),
                       pl.BlockSpec((TM_FINAL, D_MODEL), smap)],
            scratch_shapes=[pltpu.VMEM((TM_FINAL * n_slab, LANE), F32)],
        ),
        out_shape=[jax.ShapeDtypeStruct((T_P, D_MODEL), F32),
                   jax.ShapeDtypeStruct((T_S, D_MODEL), F32)],
        compiler_params=pltpu.CompilerParams(
            dimension_semantics=("arbitrary",), vmem_limit_bytes=VMEM_LIMIT),
        name="moe_unsort_final",
    )(pos, ys, x1, mod, final_norm)


def _plan_kernel(gr_ref, cnt_ref, pos_ref, sched_ref):
    n_post = T // TM_POST
    lane = lax.broadcasted_iota(jnp.int32, (1, LANE), 1)
    grp_lane = lambda v, g: v[:, N_EXPERTS + g:N_EXPERTS + g + 1]
    counts = [cnt_ref[k, 0:1, :] for k in range(n_post)]
    gtot_v = counts[0]
    for k in range(1, n_post):
        gtot_v = gtot_v + counts[k]
    gtot = [grp_lane(gtot_v, g) for g in range(N_GROUPS)]
    goff = [jnp.zeros((1, 1), F32)]
    for g in range(1, N_GROUPS):
        goff.append(goff[-1] + gtot[g - 1])

    before = [jnp.zeros((1, 1), F32) for _ in range(N_GROUPS)]
    rows_per_tile = TM_POST // LANE
    for k in range(n_post):
        gid = gr_ref[0:1, TM_POST * k:TM_POST * (k + 1)]
        pos = gr_ref[1:2, TM_POST * k:TM_POST * (k + 1)]
        for g in range(N_GROUPS):
            pos = pos + jnp.where(gid == float(g), goff[g] + before[g], 0.0)
            before[g] = before[g] + grp_lane(counts[k], g)
        for r in range(rows_per_tile):
            pos_ref[rows_per_tile * k + r:rows_per_tile * k + r + 1, :] = (
                pos[:, LANE * r:LANE * (r + 1)].astype(jnp.int32))

    zero = jnp.zeros((1, LANE), F32)
    rows = {name: zero for name in ("vt", "vg", "vlo", "vhi")}
    slot = jnp.zeros((1, 1), F32)
    last_t = jnp.zeros((1, 1), F32)
    last_g = jnp.zeros((1, 1), F32)
    for i in range(T // TM_MOE):
        for g in range(N_GROUPS):
            lo = jnp.clip(goff[g] - float(TM_MOE * i), 0.0, float(TM_MOE))
            hi = jnp.clip(goff[g] + gtot[g] - float(TM_MOE * i), 0.0, float(TM_MOE))
            ok = hi > lo
            here = ok & (lane == slot.astype(jnp.int32))
            rows["vt"] = jnp.where(here, float(i), rows["vt"])
            rows["vg"] = jnp.where(here, float(g), rows["vg"])
            rows["vlo"] = jnp.where(here, lo, rows["vlo"])
            rows["vhi"] = jnp.where(here, hi, rows["vhi"])
            last_t = jnp.where(ok, float(i), last_t)
            last_g = jnp.where(ok, float(g), last_g)
            slot = slot + jnp.where(ok, 1.0, 0.0)
    n_ok = slot.astype(jnp.int32)
    valid = lane < n_ok
    vt = jnp.where(valid, rows["vt"], last_t)
    vg = jnp.where(valid, rows["vg"], last_g)
    prev_t = pltpu.roll(jnp.broadcast_to(vt, (8, LANE)), 1, 1)[0:1, :]
    next_t = pltpu.roll(jnp.broadcast_to(vt, (8, LANE)), LANE - 1, 1)[0:1, :]
    first = jnp.where((lane == 0) | (vt != prev_t), 1.0, 0.0)
    last = jnp.where((lane == n_ok - 1) | (vt != next_t), 1.0, 0.0)
    table = [vt, vg, rows["vlo"], rows["vhi"], first, last, jnp.where(valid, 1.0, 0.0), zero]
    for r, row in enumerate(table):
        sched_ref[r:r + 1, :] = row.astype(jnp.int32)


V_TILE, V_GROUP, V_LO, V_HI, V_FIRST, V_LAST, V_VALID = range(7)


def _plan(gr, cnt):
    pos, sched = pl.pallas_call(
        _plan_kernel,
        out_shape=[jax.ShapeDtypeStruct((T // LANE, LANE), jnp.int32),
                   jax.ShapeDtypeStruct((8, LANE), jnp.int32)],
        name="moe_plan",
    )(gr, cnt)
    return pos.reshape(T), sched


def _rope_tables():
    n = np.arange(DEC_SEQ)
    pos = np.stack([n // GRID_W, n % GRID_W], axis=1).astype(np.float32)
    half = ROPE_AXIS // 2
    inv = (1.0 / (ROPE_BASE ** (np.arange(0, ROPE_AXIS, 2, dtype=np.float32) / ROPE_AXIS))).astype(np.float32)
    ang = (pos[:, :, None] * inv[None, None, :]).astype(np.float32)
    cos = np.cos(ang).astype(np.float32)
    sin = np.sin(ang).astype(np.float32)
    tabs = np.zeros((2, 3, DEC_SEQ, LANE), np.float32)
    tabs[:, 0] = 1.0
    for a in range(2):
        lo = ROPE_LANE0 + a * ROPE_AXIS
        tabs[1, 0, :, lo:lo + half] = cos[:, a]
        tabs[1, 0, :, lo + half:lo + 2 * half] = cos[:, a]
        tabs[1, 1, :, lo + half:lo + 2 * half] = sin[:, a]
        tabs[1, 2, :, lo:lo + half] = -sin[:, a]
    return jnp.asarray(tabs)


def kernel(x_prompt, x_sample, cache_ckv, cache_krope, c, c_ctx, norm1, w_ada, b_ada, w_in, conv_w,
           w_conv_out, q_norm, w_uq, kv_norm, w_ukv, w_o, w_mix_out, norm2, w_grp, w_exp, w_up,
           w_gate, w_down, final_norm):
    l = 0
    xp = x_prompt.reshape(T_P, D_MODEL)
    xs = x_sample.reshape(T_S, D_MODEL)
    mod = _ada(c_ctx[None, :], c, w_ada[l], b_ada[l][None, :])

    w_in_t = w_in[l].T
    w_uq_slot = jnp.pad(w_uq[l].reshape(Q_LORA, N_HEADS, QK_NOPE + QK_ROPE),
                        ((0, 0), (0, 0), (0, LANE - QK_NOPE - QK_ROPE))
                        ).reshape(Q_LORA, N_HEADS * LANE).astype(BF16)
    wkv = w_ukv[l].reshape(KV_LORA, N_HEADS, QK_NOPE + V_HEAD)
    wk_slot = jnp.pad(wkv[:, :, :QK_NOPE], ((0, 0), (0, 0), (0, LANE - QK_NOPE)))
    wv = wkv[:, :, QK_NOPE:].reshape(KV_LORA, N_HEADS // 2, 2, V_HEAD)
    zero = jnp.zeros_like(wv[:, :, 0])
    wv_slot = jnp.stack([jnp.concatenate([wv[:, :, 0], zero], axis=-1),
                         jnp.concatenate([zero, wv[:, :, 1]], axis=-1)], axis=2)
    w_ukv_slot = jnp.concatenate([wk_slot.reshape(KV_LORA, N_HEADS * LANE),
                                  wv_slot.reshape(KV_LORA, N_HEADS * LANE)], axis=1).astype(BF16)
    w_route = jnp.pad(jnp.concatenate([w_exp[l], w_grp[l]], axis=1),
                      ((0, 0), (0, LANE - N_EXPERTS - N_GROUPS)))
    w_route_hi = w_route.astype(BF16)
    w_route_lo = (w_route - w_route_hi.astype(F32)).astype(BF16)
    w_route2 = jnp.concatenate([w_route_hi, w_route_lo], axis=1)
    cache_krs = jnp.pad(cache_krope[:, l], ((0, 0), (0, 0), (ROPE_LANE0, LANE - ROPE_LANE0 - QK_ROPE)))

    zc, q, ckv, krs, nckv, nkr = _inproj(xp, xs, mod, norm1[l][None, :], w_in_t, conv_w[l],
                              q_norm[l][None, :], kv_norm[l][None, :], w_uq_slot, _rope_tables())
    o_p, o_s, w_gates_t, w_co_b, w_o_b, w_mix_b = _attention(
        q, ckv, krs, cache_ckv[:, l], cache_krs, w_ukv_slot, w_in_t, w_conv_out[l], w_o[l], w_mix_out[l])
    x1, h3, meta, cnt, gr = _post(xp, xs, mod, norm1[l][None, :], w_gates_t, zc, o_p, o_s, w_co_b, w_o_b,
                                  w_mix_b, norm2[l][None, :], w_route2)
    pos, sched = _plan(gr, cnt)
    hs, ms = _dispatch(pos, h3, meta)
    ys = _moe(sched, hs, ms, w_up[l], w_gate[l], w_down[l])
    yp, ysm = _final(pos, ys, x1, mod, final_norm[None, :])

    y_prompt = yp.reshape(BATCH, SEQ, D_MODEL)
    y_sample = ysm.reshape(DEC_BATCH, DEC_SEQ, D_MODEL)
    new_ckv = nckv.reshape(BATCH, 1, SEQ, KV_LORA)
    new_krope = jnp.swapaxes(nkr, 1, 2).reshape(BATCH, 1, SEQ, QK_ROPE)
    return (y_prompt, y_sample, new_ckv, new_krope)
```

```python
import numpy as np
import jax
import jax.numpy as jnp
from jax import lax
from jax.experimental import pallas as pl
from jax.experimental.pallas import tpu as pltpu

F32 = jnp.float32
BF16 = jnp.bfloat16

D_MODEL = 1024
BATCH = 16
SEQ = 256
DEC_BATCH = 2
DEC_SEQ = 1024
PAST_LEN = 256
GRID_W = 64
N_HEADS = 8
QK_NOPE = 64
QK_ROPE = 32
V_HEAD = 64
Q_LORA = 256
KV_LORA = 128
ROPE_AXIS = QK_ROPE // 2
ROPE_BASE = 10000.0
ATTN_SCALE = (QK_NOPE + QK_ROPE) ** -0.5
D_CONV = D_MODEL
N_GROUPS = 4
EXP_PER_GROUP = 8
N_EXPERTS = N_GROUPS * EXP_PER_GROUP
D_EXPERT = 256
EPS = 1e-6

T_P = BATCH * SEQ
T_S = DEC_BATCH * DEC_SEQ
T = T_P + T_S
N_COND = 8
LANE = 128
ROPE_LANE0 = QK_NOPE
SMALL_COLS = Q_LORA + KV_LORA + LANE
VMEM_LIMIT = 56 * 1024 * 1024

TM_IN = 1024
TM_POST = 512
TM_MOE = T // N_GROUPS
TM_FINAL = 512
TM_DISP = 1536
N_SLAB = D_MODEL // LANE
MOE_SUB = 256
MOE_EPS = 2
N_VISITS = T // TM_MOE + N_GROUPS - 1
GID_LANE = 40
RANK_LANE = 41
Q_BLK_S = 512
PROMPT_SEQS = 2
CONV_CHUNK = 256


def _dot(a, b):
    return jnp.dot(a, b, preferred_element_type=F32)


def _rms(x):
    return lax.rsqrt(jnp.mean(x * x, axis=-1, keepdims=True) + EPS)


def _slab(t):
    return pl.ds(pl.multiple_of(t * N_SLAB, N_SLAB), N_SLAB)


def _mod_row(i, tm):
    n_prompt = T_P // tm
    return jnp.where(i >= n_prompt, 1 + ((i - n_prompt) * tm) // DEC_SEQ, 0)


def _ada_kernel(cctx_ref, c_ref, w_ref, b_ref, o_ref):
    c = jnp.concatenate([cctx_ref[...], c_ref[...],
                         jnp.zeros((N_COND - 1 - DEC_BATCH, D_MODEL), F32)], axis=0)
    a = (c * jax.nn.sigmoid(c)).astype(BF16)
    o_ref[...] = _dot(a, w_ref[...].astype(BF16)) + b_ref[...]


def _ada(c_ctx, c, w_ada, b_ada):
    n = 6 * D_MODEL
    bn = 1536
    return pl.pallas_call(
        _ada_kernel,
        grid=(n // bn,),
        in_specs=[
            pl.BlockSpec((1, D_MODEL), lambda j: (0, 0)),
            pl.BlockSpec((DEC_BATCH, D_MODEL), lambda j: (0, 0)),
            pl.BlockSpec((D_MODEL, bn), lambda j: (0, j)),
            pl.BlockSpec((1, bn), lambda j: (0, j)),
        ],
        out_specs=pl.BlockSpec((N_COND, bn), lambda j: (0, j)),
        out_shape=jax.ShapeDtypeStruct((N_COND, n), F32),
        compiler_params=pltpu.CompilerParams(
            dimension_semantics=("parallel",), vmem_limit_bytes=VMEM_LIMIT),
        name="ada_mod",
    )(c_ctx, c, w_ada, b_ada)


O_CQ = 3 * D_CONV
O_KR = O_CQ + Q_LORA + KV_LORA
O_GATE = O_KR + QK_ROPE
SMALL_BLK = 512
NT = (((1,), (1,)), ((), ()))


def _dot_nt(a, bt):
    return lax.dot_general(a, bt, NT, preferred_element_type=F32)


def _stream_maps(tm):
    n_prompt = T_P // tm
    return (lambda i, *_: (jnp.minimum(i, n_prompt - 1), 0),
            lambda i, *_: (jnp.maximum(i - n_prompt, 0), 0))


def _inproj_kernel(xp_ref, xs_ref, mod_ref, n1_ref, wc_ref, ws_ref, cw_ref, qn_ref, kvn_ref, wuq_ref,
                   rope_ref, zc_ref, q_ref, ckv_ref, krs_ref, nckv_ref, nkr_ref):
    i = pl.program_id(0)
    is_sample = i >= T_P // TM_IN
    seq = jnp.where(is_sample, DEC_SEQ, SEQ)
    mod = mod_ref[pl.ds(_mod_row(i, TM_IN), 1), :]
    shift1 = mod[:, 0:D_MODEL]
    scale1 = mod[:, D_MODEL:2 * D_MODEL]
    x = jnp.where(is_sample, xs_ref[...], xp_ref[...])
    h = ((x * _rms(x)) * n1_ref[...]) * (1.0 + scale1) + shift1
    hb = h.astype(BF16)

    n_lat = Q_LORA + KV_LORA
    w_small = jnp.concatenate([
        ws_ref[0:n_lat, :].astype(BF16),
        jnp.zeros((ROPE_LANE0, D_MODEL), BF16),
        ws_ref[n_lat:n_lat + QK_ROPE, :].astype(BF16),
        jnp.zeros((LANE - ROPE_LANE0 - QK_ROPE, D_MODEL), BF16)], axis=0)
    sm = _dot_nt(hb, w_small)
    cq = sm[:, 0:Q_LORA]
    ckv_raw = sm[:, Q_LORA:Q_LORA + KV_LORA]
    krs = sm[:, Q_LORA + KV_LORA:SMALL_COLS]
    cqn = (cq * _rms(cq)) * qn_ref[...]
    q = _dot(cqn.astype(BF16), wuq_ref[...])
    ckv = (ckv_raw * _rms(ckv_raw)) * kvn_ref[...]
    ckv_ref[...] = ckv

    @pl.when(jnp.logical_not(is_sample))
    def _():
        nckv_ref[...] = ckv
        for s in range(TM_IN // SEQ):
            kt = krs[s * SEQ:(s + 1) * SEQ, :].T
            nkr_ref[s] = kt[ROPE_LANE0:ROPE_LANE0 + QK_ROPE, :]

    cos = rope_ref[0]
    sin_lo = rope_ref[1]
    sin_hi = rope_ref[2]

    def rot(v):
        return v * cos + pltpu.roll(v, 8, 1) * sin_lo + pltpu.roll(v, LANE - 8, 1) * sin_hi

    krs_ref[...] = rot(krs)
    for hh in range(N_HEADS):
        q_ref[:, LANE * hh:LANE * (hh + 1)] = rot(q[:, LANE * hh:LANE * (hh + 1)]).astype(BF16)

    pos = lax.broadcasted_iota(jnp.int32, (TM_IN, 1), 0) & (seq - 1)
    first = pos == 0
    last = pos == seq - 1
    for j in range(D_CONV // CONV_CHUNK):
        c0 = j * CONV_CHUNK
        bg = _dot_nt(hb, wc_ref[c0:c0 + CONV_CHUNK, :].astype(BF16))
        cg = _dot_nt(hb, wc_ref[D_CONV + c0:D_CONV + c0 + CONV_CHUNK, :].astype(BF16))
        ui = _dot_nt(hb, wc_ref[2 * D_CONV + c0:2 * D_CONV + c0 + CONV_CHUNK, :].astype(BF16))
        u = cg * ui
        u_prev = jnp.where(first, 0.0, pltpu.roll(u, 1, 0))
        u_next = jnp.where(last, 0.0, pltpu.roll(u, TM_IN - 1, 0))
        cw = cw_ref[:, c0:c0 + CONV_CHUNK]
        conv = u_prev * cw[0:1] + u * cw[1:2] + u_next * cw[2:3]
        zc_ref[:, c0:c0 + CONV_CHUNK] = (bg * conv).astype(BF16)


def _inproj(xp, xs, mod, norm1, w_in_t, conv_w, q_norm, kv_norm, w_uq_slot, rope_tabs):
    n_prompt = T_P // TM_IN
    const = lambda i: (0, 0)
    pmap, smap = _stream_maps(TM_IN)
    once = pl.Buffered(1)
    return pl.pallas_call(
        _inproj_kernel,
        grid=(T // TM_IN,),
        in_specs=[
            pl.BlockSpec((TM_IN, D_MODEL), pmap),
            pl.BlockSpec((TM_IN, D_MODEL), smap),
            pl.BlockSpec((N_COND, 6 * D_MODEL), const),
            pl.BlockSpec((1, D_MODEL), const),
            pl.BlockSpec((O_CQ, D_MODEL), const, pipeline_mode=once),
            pl.BlockSpec((SMALL_BLK, D_MODEL), lambda i: (O_CQ // SMALL_BLK, 0), pipeline_mode=once),
            pl.BlockSpec((3, D_CONV), const),
            pl.BlockSpec((1, Q_LORA), const),
            pl.BlockSpec((1, KV_LORA), const),
            pl.BlockSpec((Q_LORA, N_HEADS * LANE), const),
            pl.BlockSpec((None, 3, TM_IN, LANE),
                         lambda i: (jnp.where(i >= n_prompt, 1, 0), 0, 0, 0)),
        ],
        out_specs=[
            pl.BlockSpec((TM_IN, D_CONV), lambda i: (i, 0)),
            pl.BlockSpec((TM_IN, N_HEADS * LANE), lambda i: (i, 0)),
            pl.BlockSpec((TM_IN, KV_LORA), lambda i: (i, 0)),
            pl.BlockSpec((TM_IN, LANE), lambda i: (i, 0)),
            pl.BlockSpec((TM_IN, KV_LORA), pmap),
            pl.BlockSpec((TM_IN // SEQ, QK_ROPE, SEQ), lambda i: (jnp.minimum(i, n_prompt - 1), 0, 0)),
        ],
        out_shape=[
            jax.ShapeDtypeStruct((T, D_CONV), BF16),
            jax.ShapeDtypeStruct((T, N_HEADS * LANE), BF16),
            jax.ShapeDtypeStruct((T, KV_LORA), F32),
            jax.ShapeDtypeStruct((T, LANE), F32),
            jax.ShapeDtypeStruct((T_P, KV_LORA), F32),
            jax.ShapeDtypeStruct((BATCH, QK_ROPE, SEQ), F32),
        ],
        compiler_params=pltpu.CompilerParams(
            dimension_semantics=("arbitrary",), vmem_limit_bytes=VMEM_LIMIT),
        name="in_proj",
    )(xp, xs, mod, norm1, w_in_t, w_in_t, conv_w, q_norm, kv_norm, w_uq_slot, rope_tabs)


def _fill_kv(ckv, krs, wukv_ref, kf_scr, v_scr, off):
    m = ckv.shape[0]
    kv = _dot(ckv.astype(BF16), wukv_ref[...])
    for hh in range(N_HEADS):
        kf_scr[hh, off:off + m, :] = (kv[:, LANE * hh:LANE * (hh + 1)] + krs).astype(BF16)
    v_scr[off:off + m, :] = kv[:, N_HEADS * LANE:].astype(BF16)


def _attend(q_ref, r0, rows, kf_scr, v_scr, o_ref):
    for pair in range(N_HEADS // 2):
        acc = None
        for hh in (2 * pair, 2 * pair + 1):
            qh = q_ref[r0:r0 + rows, LANE * hh:LANE * (hh + 1)]
            s = _dot_nt(qh, kf_scr[hh]) * ATTN_SCALE
            e = jnp.exp(s - jnp.max(s, axis=-1, keepdims=True))
            p = (e / jnp.sum(e, axis=-1, keepdims=True)).astype(BF16)
            part = _dot(p, v_scr[:, LANE * hh:LANE * (hh + 1)])
            acc = part if acc is None else acc + part
        o_ref[r0:r0 + rows, LANE * pair:LANE * (pair + 1)] = acc.astype(BF16)


def _attn_prompt_kernel(q_ref, ckv_ref, krs_ref, wukv_ref, o_ref):
    for s in range(PROMPT_SEQS):
        r0 = s * SEQ
        kv = _dot(ckv_ref[r0:r0 + SEQ, :].astype(BF16), wukv_ref[...])
        krs = krs_ref[r0:r0 + SEQ, :]
        for pair in range(N_HEADS // 2):
            acc = None
            for hh in (2 * pair, 2 * pair + 1):
                kf = (kv[:, LANE * hh:LANE * (hh + 1)] + krs).astype(BF16)
                vh = kv[:, LANE * (N_HEADS + hh):LANE * (N_HEADS + hh + 1)].astype(BF16)
                sc = _dot_nt(q_ref[r0:r0 + SEQ, LANE * hh:LANE * (hh + 1)], kf) * ATTN_SCALE
                e = jnp.exp(sc - jnp.max(sc, axis=-1, keepdims=True))
                p = (e / jnp.sum(e, axis=-1, keepdims=True)).astype(BF16)
                part = _dot(p, vh)
                acc = part if acc is None else acc + part
            o_ref[r0:r0 + SEQ, LANE * pair:LANE * (pair + 1)] = acc.astype(BF16)


def _attn_sample_kernel(q_ref, ckv_ref, cckv_ref, krs_ref, ckrs_ref, wukv_ref, wg_ref, wco_ref, wo_ref,
                        wmix_ref, o_ref, g_ref, co_ref, ob_ref, mix_ref, kf_scr, v_scr):
    @pl.when(pl.program_id(1) == 0)
    def _():
        _fill_kv(ckv_ref[...], krs_ref[...], wukv_ref, kf_scr, v_scr, 0)
        _fill_kv(cckv_ref[...], ckrs_ref[...], wukv_ref, kf_scr, v_scr, DEC_SEQ)

    _attend(q_ref, 0, Q_BLK_S, kf_scr, v_scr, o_ref)
    g_ref[...] = wg_ref[...].astype(BF16)
    co_ref[...] = wco_ref[...].astype(BF16)
    ob_ref[...] = wo_ref[...].astype(BF16)
    mix_ref[...] = wmix_ref[...].astype(BF16)


def _attention(q, ckv, krs, cache_ckv, cache_krs, w_ukv_slot, w_in_t, w_conv_out, w_o, w_mix_out):
    kv_cols = 2 * N_HEADS * LANE
    n_o = N_HEADS * V_HEAD
    steps = BATCH // PROMPT_SEQS
    rows = PROMPT_SEQS * SEQ
    o_prompt = pl.pallas_call(
        _attn_prompt_kernel,
        grid=(steps,),
        in_specs=[
            pl.BlockSpec((rows, N_HEADS * LANE), lambda b: (b, 0)),
            pl.BlockSpec((rows, KV_LORA), lambda b: (b, 0)),
            pl.BlockSpec((rows, LANE), lambda b: (b, 0)),
            pl.BlockSpec((KV_LORA, kv_cols), lambda b: (0, 0)),
        ],
        out_specs=pl.BlockSpec((rows, n_o), lambda b: (b, 0)),
        out_shape=jax.ShapeDtypeStruct((T_P, n_o), BF16),
        compiler_params=pltpu.CompilerParams(dimension_semantics=("parallel",),
                                             vmem_limit_bytes=VMEM_LIMIT),
        name="attn_prompt",
    )(q, ckv, krs, w_ukv_slot)

    m_all = DEC_SEQ + PAST_LEN
    nq = DEC_SEQ // Q_BLK_S
    q0 = T_P // Q_BLK_S
    s0 = T_P // DEC_SEQ
    n_step = DEC_BATCH * nq
    share = lambda n: pl.BlockSpec((n // n_step, D_MODEL), lambda b, j: (b * nq + j, 0))
    gate_rows = 2 * D_MODEL // n_step
    o_sample, w_gates_t, w_co_b, w_o_b, w_mix_b = pl.pallas_call(
        _attn_sample_kernel,
        grid=(DEC_BATCH, nq),
        in_specs=[
            pl.BlockSpec((Q_BLK_S, N_HEADS * LANE), lambda b, j: (q0 + b * nq + j, 0)),
            pl.BlockSpec((DEC_SEQ, KV_LORA), lambda b, j: (s0 + b, 0)),
            pl.BlockSpec((None, PAST_LEN, KV_LORA), lambda b, j: (b, 0, 0)),
            pl.BlockSpec((DEC_SEQ, LANE), lambda b, j: (s0 + b, 0)),
            pl.BlockSpec((None, PAST_LEN, LANE), lambda b, j: (b, 0, 0)),
            pl.BlockSpec((KV_LORA, kv_cols), lambda b, j: (0, 0)),
            pl.BlockSpec((pl.Element(gate_rows), pl.Element(D_MODEL)),
                         lambda b, j: (pl.multiple_of(O_GATE + (b * nq + j) * gate_rows, 32), 0)),
            share(D_CONV), share(n_o), share(D_MODEL),
        ],
        out_specs=[pl.BlockSpec((Q_BLK_S, n_o), lambda b, j: (b * nq + j, 0)),
                   share(2 * D_MODEL), share(D_CONV), share(n_o), share(D_MODEL)],
        out_shape=[jax.ShapeDtypeStruct((T_S, n_o), BF16),
                   jax.ShapeDtypeStruct((2 * D_MODEL, D_MODEL), BF16),
                   jax.ShapeDtypeStruct((D_CONV, D_MODEL), BF16),
                   jax.ShapeDtypeStruct((n_o, D_MODEL), BF16),
                   jax.ShapeDtypeStruct((D_MODEL, D_MODEL), BF16)],
        scratch_shapes=[pltpu.VMEM((N_HEADS, m_all, LANE), BF16),
                        pltpu.VMEM((m_all, N_HEADS * LANE), BF16)],
        compiler_params=pltpu.CompilerParams(dimension_semantics=("arbitrary", "arbitrary"),
                                             vmem_limit_bytes=VMEM_LIMIT),
        name="attn_sample",
    )(q, ckv, cache_ckv, krs, cache_krs, w_ukv_slot, w_in_t, w_conv_out, w_o, w_mix_out)
    return o_prompt, o_sample, w_gates_t, w_co_b, w_o_b, w_mix_b


def _route(logits):
    lane = lax.broadcasted_iota(jnp.int32, logits.shape, 1)
    neg = -jnp.inf
    big = jnp.int32(1 << 20)
    gmask = (lane >= N_EXPERTS) & (lane < N_EXPERTS + N_GROUPS)
    gl = jnp.where(gmask, logits, neg)
    gmax = jnp.max(gl, axis=-1, keepdims=True)
    gsum = jnp.sum(jnp.where(gmask, jnp.exp(gl - gmax), 0.0), axis=-1, keepdims=True)
    p_g = 1.0 / gsum
    g_idx = jnp.min(jnp.where(gl == gmax, lane, big), axis=-1, keepdims=True) - N_EXPERTS

    emask = (lane < N_EXPERTS) & ((lane >> 3) == g_idx)
    el = jnp.where(emask, logits, neg)
    m1 = jnp.max(el, axis=-1, keepdims=True)
    i1 = jnp.min(jnp.where(el == m1, lane, big), axis=-1, keepdims=True)
    el2 = jnp.where(lane == i1, neg, el)
    m2 = jnp.max(el2, axis=-1, keepdims=True)
    i2 = jnp.min(jnp.where(el2 == m2, lane, big), axis=-1, keepdims=True)
    z = jnp.sum(jnp.where(emask, jnp.exp(el - m1), 0.0), axis=-1, keepdims=True)
    p1 = 1.0 / z
    p2 = jnp.exp(m2 - m1) / z
    tot = p1 + p2
    w1 = p_g * p1 / tot
    w2 = p_g * p2 / tot
    return jnp.where(lane == i1, w1, 0.0) + jnp.where(lane == i2, w2, 0.0), g_idx


def _post_kernel(xp_ref, xs_ref, mod_ref, n1_ref, wg_ref, zc_ref, op_ref, os_ref, wco_ref, wo_ref,
                 wmix_ref, n2_ref, wr_ref, x1_ref, h3_ref, meta_ref, cnt_ref, gr_ref):
    i = pl.program_id(0)
    is_sample = i >= T_P // TM_POST
    x = jnp.where(is_sample, xs_ref[...], xp_ref[...])
    o = jnp.where(is_sample, os_ref[...], op_ref[...])
    mod = mod_ref[pl.ds(_mod_row(i, TM_POST), 1), :]
    shift1 = mod[:, 0:D_MODEL]
    scale1 = mod[:, D_MODEL:2 * D_MODEL]
    gate1 = mod[:, 2 * D_MODEL:3 * D_MODEL]
    shift2 = mod[:, 3 * D_MODEL:4 * D_MODEL]
    scale2 = mod[:, 4 * D_MODEL:5 * D_MODEL]
    y_conv = _dot(zc_ref[...], wco_ref[...])
    y_mla = _dot(o, wo_ref[...])
    h = ((x * _rms(x)) * n1_ref[...]) * (1.0 + scale1) + shift1
    g = _dot_nt(h.astype(BF16), wg_ref[...])
    merged = (jax.nn.sigmoid(g[:, 0:D_MODEL]) * y_conv
              + jax.nn.sigmoid(g[:, D_MODEL:2 * D_MODEL]) * y_mla)
    y = _dot(merged.astype(BF16), wmix_ref[...])
    x1 = x + gate1 * y
    x1_ref[...] = x1
    h2 = ((x1 * _rms(x1)) * n2_ref[...]) * (1.0 + scale2) + shift2
    h2_hi = h2.astype(BF16)
    h2_lo = (h2 - h2_hi.astype(F32)).astype(BF16)
    hh = _dot(h2_hi, wr_ref[...])
    logits = hh[:, 0:LANE] + hh[:, LANE:2 * LANE] + _dot(h2_lo, wr_ref[:, 0:LANE])
    comb, g_idx = _route(logits)

    lane = lax.broadcasted_iota(jnp.int32, comb.shape, 1)
    onehot = lane == g_idx + N_EXPERTS
    r_i = lax.broadcasted_iota(jnp.int32, (TM_POST, TM_POST), 0)
    c_i = lax.broadcasted_iota(jnp.int32, (TM_POST, TM_POST), 1)
    lower = jnp.where(c_i < r_i, 1.0, 0.0).astype(BF16)
    before = _dot(lower, jnp.where(onehot, 1.0, 0.0).astype(BF16))
    rank = jnp.sum(jnp.where(onehot, before, 0.0), axis=-1, keepdims=True)
    counts = jnp.sum(jnp.where(onehot, 1.0, 0.0), axis=0, keepdims=True)
    cnt_ref[...] = jnp.broadcast_to(counts, cnt_ref.shape)

    meta_ref[...] = comb
    idx = jnp.where(lane == GID_LANE, g_idx.astype(F32), 0.0) + jnp.where(lane == RANK_LANE, rank, 0.0)
    idx_hi = idx.astype(BF16)
    idx_lo = (idx - idx_hi.astype(F32)).astype(BF16)
    s_row = lax.broadcasted_iota(jnp.int32, (8, LANE), 0)
    s_lane = lax.broadcasted_iota(jnp.int32, (8, LANE), 1)
    sel = jnp.where(s_lane == GID_LANE + s_row, 1.0, 0.0).astype(BF16)
    gr_ref[...] = _dot_nt(sel, idx_hi) + _dot_nt(sel, idx_lo)
    for c in range(N_SLAB):
        h3_ref[pl.ds(c, TM_POST, stride=N_SLAB), :] = h2[:, LANE * c:LANE * (c + 1)]


def _post(xp, xs, mod, norm1, w_gates_t, zc, o_p, o_s, w_conv_out, w_o, w_mix_out, norm2, w_route):
    const = lambda i: (0, 0)
    row = lambda i: (i, 0)
    pmap, smap = _stream_maps(TM_POST)
    return pl.pallas_call(
        _post_kernel,
        grid=(T // TM_POST,),
        in_specs=[
            pl.BlockSpec((TM_POST, D_MODEL), pmap),
            pl.BlockSpec((TM_POST, D_MODEL), smap),
            pl.BlockSpec((N_COND, 6 * D_MODEL), const),
            pl.BlockSpec((1, D_MODEL), const),
            pl.BlockSpec((2 * D_MODEL, D_MODEL), const),
            pl.BlockSpec((TM_POST, D_CONV), row),
            pl.BlockSpec((TM_POST, N_HEADS * V_HEAD), pmap),
            pl.BlockSpec((TM_POST, N_HEADS * V_HEAD), smap),
            pl.BlockSpec((D_CONV, D_MODEL), const),
            pl.BlockSpec((N_HEADS * V_HEAD, D_MODEL), const),
            pl.BlockSpec((D_MODEL, D_MODEL), const),
            pl.BlockSpec((1, D_MODEL), const),
            pl.BlockSpec((D_MODEL, 2 * LANE), const),
        ],
        out_specs=[
            pl.BlockSpec((TM_POST, D_MODEL), row),
            pl.BlockSpec((TM_POST * N_SLAB, LANE), row),
            pl.BlockSpec((TM_POST, LANE), row),
            pl.BlockSpec((None, 8, LANE), lambda i: (i, 0, 0)),
            pl.BlockSpec((8, TM_POST), lambda i: (0, i)),
        ],
        out_shape=[
            jax.ShapeDtypeStruct((T, D_MODEL), F32),
            jax.ShapeDtypeStruct((T * N_SLAB, LANE), F32),
            jax.ShapeDtypeStruct((T, LANE), F32),
            jax.ShapeDtypeStruct((T // TM_POST, 8, LANE), F32),
            jax.ShapeDtypeStruct((8, T), F32),
        ],
        compiler_params=pltpu.CompilerParams(
            dimension_semantics=("parallel",), vmem_limit_bytes=VMEM_LIMIT),
        name="post_mixer",
    )(xp, xs, mod, norm1, w_gates_t, zc, o_p, o_s, w_conv_out, w_o, w_mix_out, norm2, w_route)


DISP_IN = T // TM_DISP
DISP_OUT = T // TM_MOE


def _dispatch_kernel(pos_ref, h3_ref, m_ref, hs_ref, ms_ref, xs_ref, mss_ref):
    i = pl.program_id(0)

    @pl.when(i < DISP_IN)
    def _():
        base = i * TM_DISP

        def body(r, carry):
            p = pos_ref[base + r]
            xs_ref[_slab(p), :] = h3_ref[_slab(r), :]
            mss_ref[pl.ds(p, 1), :] = m_ref[pl.ds(r, 1), :]
            return carry

        lax.fori_loop(0, TM_DISP, body, 0, unroll=8)

    @pl.when(i >= DISP_IN)
    def _():
        row0 = pl.multiple_of((i - DISP_IN) * TM_MOE, TM_MOE)
        for c in range(N_SLAB):
            hs_ref[:, LANE * c:LANE * (c + 1)] = (
                xs_ref[pl.ds(row0 * N_SLAB + c, TM_MOE, stride=N_SLAB), :].astype(BF16))
        ms_ref[...] = mss_ref[pl.ds(row0, TM_MOE), :]


def _dispatch(pos, h3, meta):
    n_slab = N_SLAB
    in_map = lambda i, pos: (jnp.minimum(i, DISP_IN - 1), 0)
    out_map = lambda i, pos: (jnp.maximum(i - DISP_IN, 0), 0)
    return pl.pallas_call(
        _dispatch_kernel,
        grid_spec=pltpu.PrefetchScalarGridSpec(
            num_scalar_prefetch=1,
            grid=(DISP_IN + DISP_OUT,),
            in_specs=[pl.BlockSpec((TM_DISP * n_slab, LANE), in_map),
                      pl.BlockSpec((TM_DISP, LANE), in_map)],
            out_specs=[pl.BlockSpec((TM_MOE, D_MODEL), out_map),
                       pl.BlockSpec((TM_MOE, LANE), out_map)],
            scratch_shapes=[pltpu.VMEM((T * n_slab, LANE), F32),
                            pltpu.VMEM((T, LANE), F32)],
        ),
        out_shape=[jax.ShapeDtypeStruct((T, D_MODEL), BF16),
                   jax.ShapeDtypeStruct((T, LANE), F32)],
        compiler_params=pltpu.CompilerParams(
            dimension_semantics=("arbitrary",), vmem_limit_bytes=VMEM_LIMIT),
        name="moe_dispatch",
    )(pos, h3, meta)


def _moe_kernel(sched_ref, hs_ref, ms_ref, wup_ref, wgate_ref, wdown_ref, y3_ref, acc_ref):
    v = pl.program_id(0)
    j = pl.program_id(1)
    valid = sched_ref[V_VALID, v] == 1
    lo = sched_ref[V_LO, v]
    hi = sched_ref[V_HI, v]
    e0 = sched_ref[V_GROUP, v] * EXP_PER_GROUP + j * MOE_EPS
    full = (hi - lo) * 4 >= TM_MOE * 3

    @pl.when(valid & (j == 0) & (sched_ref[V_FIRST, v] == 1))
    def _():
        acc_ref[...] = jnp.zeros_like(acc_ref)

    def expert_rows(r0, rows):
        w_in2 = jnp.concatenate(
            [w[k].astype(BF16) for k in range(MOE_EPS) for w in (wup_ref, wgate_ref)], axis=1)
        ag = _dot(hs_ref[pl.ds(r0, rows), :], w_in2)
        comb = ms_ref[pl.ds(r0, rows), :]
        lane = lax.broadcasted_iota(jnp.int32, comb.shape, 1)
        acts = []
        for k in range(MOE_EPS):
            a = ag[:, 2 * k * D_EXPERT:(2 * k + 1) * D_EXPERT]
            g = ag[:, (2 * k + 1) * D_EXPERT:(2 * k + 2) * D_EXPERT]
            cw = jnp.sum(jnp.where(lane == e0 + k, comb, 0.0), axis=-1, keepdims=True)
            acts.append(((g * jax.nn.sigmoid(g)) * a * cw).astype(BF16))
        w_out = jnp.concatenate([wdown_ref[k].astype(BF16) for k in range(MOE_EPS)], axis=0)
        acc_ref[pl.ds(r0, rows), :] += _dot(jnp.concatenate(acts, axis=1), w_out)

    @pl.when(valid & full)
    def _():
        expert_rows(0, TM_MOE)

    @pl.when(valid & jnp.logical_not(full))
    def _():
        def sub_block(s, carry):
            expert_rows(pl.multiple_of(s * MOE_SUB, MOE_SUB), MOE_SUB)
            return carry

        lax.fori_loop(lo // MOE_SUB, (hi + MOE_SUB - 1) // MOE_SUB, sub_block, 0)

    @pl.when(valid & (j == EXP_PER_GROUP // MOE_EPS - 1) & (sched_ref[V_LAST, v] == 1))
    def _():
        for c in range(N_SLAB):
            y3_ref[pl.ds(c, TM_MOE, stride=N_SLAB), :] = acc_ref[:, LANE * c:LANE * (c + 1)]


def _moe(sched, hs, ms, w_up, w_gate, w_down):
    steps = EXP_PER_GROUP // MOE_EPS
    wmap = lambda v, j, sched: (
        sched[V_GROUP, v] * steps + jnp.where(sched[V_VALID, v] == 1, j, steps - 1), 0, 0)
    tmap = lambda v, j, sched: (sched[V_TILE, v], 0)
    n_slab = D_MODEL // LANE
    return pl.pallas_call(
        _moe_kernel,
        grid_spec=pltpu.PrefetchScalarGridSpec(
            num_scalar_prefetch=1,
            grid=(N_VISITS, steps),
            in_specs=[
                pl.BlockSpec((TM_MOE, D_MODEL), tmap),
                pl.BlockSpec((TM_MOE, LANE), tmap),
                pl.BlockSpec((MOE_EPS, D_MODEL, D_EXPERT), wmap),
                pl.BlockSpec((MOE_EPS, D_MODEL, D_EXPERT), wmap),
                pl.BlockSpec((MOE_EPS, D_EXPERT, D_MODEL), wmap),
            ],
            out_specs=pl.BlockSpec((TM_MOE * n_slab, LANE), tmap),
            scratch_shapes=[pltpu.VMEM((TM_MOE, D_MODEL), F32)],
        ),
        out_shape=jax.ShapeDtypeStruct((T * n_slab, LANE), F32),
        compiler_params=pltpu.CompilerParams(
            dimension_semantics=("arbitrary", "arbitrary"), vmem_limit_bytes=VMEM_LIMIT),
        name="moe_grouped",
    )(sched, hs, ms, w_up, w_gate, w_down)


def _final_kernel(pos_ref, ys_ref, x1_ref, mod_ref, fn_ref, yp_ref, ysm_ref, g_ref):
    i = pl.program_id(0)
    base = i * TM_FINAL

    def body(r, carry):
        g_ref[_slab(r), :] = ys_ref[_slab(pos_ref[base + r]), :]
        return carry

    lax.fori_loop(0, TM_FINAL, body, 0, unroll=8)
    mod = mod_ref[pl.ds(_mod_row(i, TM_FINAL), 1), :]
    gate2 = mod[:, 5 * D_MODEL:6 * D_MODEL]
    moe = jnp.concatenate([g_ref[pl.ds(c, TM_FINAL, stride=N_SLAB), :] for c in range(N_SLAB)], axis=1)
    x2 = x1_ref[...] + gate2 * moe
    y = (x2 * _rms(x2)) * fn_ref[...]
    is_sample = i >= T_P // TM_FINAL

    @pl.when(jnp.logical_not(is_sample))
    def _():
        yp_ref[...] = y

    @pl.when(is_sample)
    def _():
        ysm_ref[...] = y


def _final(pos, ys, x1, mod, final_norm):
    n_slab = D_MODEL // LANE
    pmap, smap = _stream_maps(TM_FINAL)
    return pl.pallas_call(
        _final_kernel,
        grid_spec=pltpu.PrefetchScalarGridSpec(
            num_scalar_prefetch=1,
            grid=(T // TM_FINAL,),
            in_specs=[
                pl.BlockSpec((T * n_slab, LANE), lambda i, pos: (0, 0), pipeline_mode=pl.Buffered(1)),
                pl.BlockSpec((TM_FINAL, D_MODEL), lambda i, pos: (i, 0)),
                pl.BlockSpec((N_COND, 6 * D_MODEL), lambda i, pos: (0, 0)),
                pl.BlockSpec((1, D_MODEL), lambda i, pos: (0, 0)),
            ],
            out_specs=[pl.BlockSpec((TM_FINAL, D_MODEL), pmap),
                       pl.BlockSpec((TM_FINAL, D_MODEL), smap)],
            scratch_shapes=[pltpu.VMEM((TM_FINAL * n_slab, LANE), F32)],
        ),
        out_shape=[jax.ShapeDtypeStruct((T_P, D_MODEL), F32),
                   jax.ShapeDtypeStruct((T_S, D_MODEL), F32)],
        compiler_params=pltpu.CompilerParams(
            dimension_semantics=("arbitrary",), vmem_limit_bytes=VMEM_LIMIT),
        name="moe_unsort_final",
    )(pos, ys, x1, mod, final_norm)


def _plan_kernel(gr_ref, cnt_ref, pos_ref, sched_ref):
    n_post = T // TM_POST
    lane = lax.broadcasted_iota(jnp.int32, (1, LANE), 1)
    grp_lane = lambda v, g: v[:, N_EXPERTS + g:N_EXPERTS + g + 1]
    counts = [cnt_ref[k, 0:1, :] for k in range(n_post)]
    gtot_v = counts[0]
    for k in range(1, n_post):
        gtot_v = gtot_v + counts[k]
    gtot = [grp_lane(gtot_v, g) for g in range(N_GROUPS)]
    goff = [jnp.zeros((1, 1), F32)]
    for g in range(1, N_GROUPS):
        goff.append(goff[-1] + gtot[g - 1])

    before = [jnp.zeros((1, 1), F32) for _ in range(N_GROUPS)]
    rows_per_tile = TM_POST // LANE
    for k in range(n_post):
        gid = gr_ref[0:1, TM_POST * k:TM_POST * (k + 1)]
        pos = gr_ref[1:2, TM_POST * k:TM_POST * (k + 1)]
        for g in range(N_GROUPS):
            pos = pos + jnp.where(gid == float(g), goff[g] + before[g], 0.0)
            before[g] = before[g] + grp_lane(counts[k], g)
        for r in range(rows_per_tile):
            pos_ref[rows_per_tile * k + r:rows_per_tile * k + r + 1, :] = (
                pos[:, LANE * r:LANE * (r + 1)].astype(jnp.int32))

    zero = jnp.zeros((1, LANE), F32)
    rows = {name: zero for name in ("vt", "vg", "vlo", "vhi")}
    slot = jnp.zeros((1, 1), F32)
    last_t = jnp.zeros((1, 1), F32)
    last_g = jnp.zeros((1, 1), F32)
    for i in range(T // TM_MOE):
        for g in range(N_GROUPS):
            lo = jnp.clip(goff[g] - float(TM_MOE * i), 0.0, float(TM_MOE))
            hi = jnp.clip(goff[g] + gtot[g] - float(TM_MOE * i), 0.0, float(TM_MOE))
            ok = hi > lo
            here = ok & (lane == slot.astype(jnp.int32))
            rows["vt"] = jnp.where(here, float(i), rows["vt"])
            rows["vg"] = jnp.where(here, float(g), rows["vg"])
            rows["vlo"] = jnp.where(here, lo, rows["vlo"])
            rows["vhi"] = jnp.where(here, hi, rows["vhi"])
            last_t = jnp.where(ok, float(i), last_t)
            last_g = jnp.where(ok, float(g), last_g)
            slot = slot + jnp.where(ok, 1.0, 0.0)
    n_ok = slot.astype(jnp.int32)
    valid = lane < n_ok
    vt = jnp.where(valid, rows["vt"], last_t)
    vg = jnp.where(valid, rows["vg"], last_g)
    prev_t = pltpu.roll(jnp.broadcast_to(vt, (8, LANE)), 1, 1)[0:1, :]
    next_t = pltpu.roll(jnp.broadcast_to(vt, (8, LANE)), LANE - 1, 1)[0:1, :]
    first = jnp.where((lane == 0) | (vt != prev_t), 1.0, 0.0)
    last = jnp.where((lane == n_ok - 1) | (vt != next_t), 1.0, 0.0)
    table = [vt, vg, rows["vlo"], rows["vhi"], first, last, jnp.where(valid, 1.0, 0.0), zero]
    for r, row in enumerate(table):
        sched_ref[r:r + 1, :] = row.astype(jnp.int32)


V_TILE, V_GROUP, V_LO, V_HI, V_FIRST, V_LAST, V_VALID = range(7)


def _plan(gr, cnt):
    pos, sched = pl.pallas_call(
        _plan_kernel,
        out_shape=[jax.ShapeDtypeStruct((T // LANE, LANE), jnp.int32),
                   jax.ShapeDtypeStruct((8, LANE), jnp.int32)],
        name="moe_plan",
    )(gr, cnt)
    return pos.reshape(T), sched


def _rope_tables():
    n = np.arange(DEC_SEQ)
    pos = np.stack([n // GRID_W, n % GRID_W], axis=1).astype(np.float32)
    half = ROPE_AXIS // 2
    inv = (1.0 / (ROPE_BASE ** (np.arange(0, ROPE_AXIS, 2, dtype=np.float32) / ROPE_AXIS))).astype(np.float32)
    ang = (pos[:, :, None] * inv[None, None, :]).astype(np.float32)
    cos = np.cos(ang).astype(np.float32)
    sin = np.sin(ang).astype(np.float32)
    tabs = np.zeros((2, 3, DEC_SEQ, LANE), np.float32)
    tabs[:, 0] = 1.0
    for a in range(2):
        lo = ROPE_LANE0 + a * ROPE_AXIS
        tabs[1, 0, :, lo:lo + half] = cos[:, a]
        tabs[1, 0, :, lo + half:lo + 2 * half] = cos[:, a]
        tabs[1, 1, :, lo + half:lo + 2 * half] = sin[:, a]
        tabs[1, 2, :, lo:lo + half] = -sin[:, a]
    return jnp.asarray(tabs)


def kernel(x_prompt, x_sample, cache_ckv, cache_krope, c, c_ctx, norm1, w_ada, b_ada, w_in, conv_w,
           w_conv_out, q_norm, w_uq, kv_norm, w_ukv, w_o, w_mix_out, norm2, w_grp, w_exp, w_up,
           w_gate, w_down, final_norm):
    l = 0
    xp = x_prompt.reshape(T_P, D_MODEL)
    xs = x_sample.reshape(T_S, D_MODEL)
    mod = _ada(c_ctx[None, :], c, w_ada[l], b_ada[l][None, :])

    w_in_t = w_in[l].T
    w_uq_slot = jnp.pad(w_uq[l].reshape(Q_LORA, N_HEADS, QK_NOPE + QK_ROPE),
                        ((0, 0), (0, 0), (0, LANE - QK_NOPE - QK_ROPE))
                        ).reshape(Q_LORA, N_HEADS * LANE).astype(BF16)
    wkv = w_ukv[l].reshape(KV_LORA, N_HEADS, QK_NOPE + V_HEAD)
    wk_slot = jnp.pad(wkv[:, :, :QK_NOPE], ((0, 0), (0, 0), (0, LANE - QK_NOPE)))
    wv = wkv[:, :, QK_NOPE:].reshape(KV_LORA, N_HEADS // 2, 2, V_HEAD)
    zero = jnp.zeros_like(wv[:, :, 0])
    wv_slot = jnp.stack([jnp.concatenate([wv[:, :, 0], zero], axis=-1),
                         jnp.concatenate([zero, wv[:, :, 1]], axis=-1)], axis=2)
    w_ukv_slot = jnp.concatenate([wk_slot.reshape(KV_LORA, N_HEADS * LANE),
                                  wv_slot.reshape(KV_LORA, N_HEADS * LANE)], axis=1).astype(BF16)
    w_route = jnp.pad(jnp.concatenate([w_exp[l], w_grp[l]], axis=1),
                      ((0, 0), (0, LANE - N_EXPERTS - N_GROUPS)))
    w_route_hi = w_route.astype(BF16)
    w_route_lo = (w_route - w_route_hi.astype(F32)).astype(BF16)
    w_route2 = jnp.concatenate([w_route_hi, w_route_lo], axis=1)
    cache_krs = jnp.pad(cache_krope[:, l], ((0, 0), (0, 0), (ROPE_LANE0, LANE - ROPE_LANE0 - QK_ROPE)))

    zc, q, ckv, krs, nckv, nkr = _inproj(xp, xs, mod, norm1[l][None, :], w_in_t, conv_w[l],
                              q_norm[l][None, :], kv_norm[l][None, :], w_uq_slot, _rope_tables())
    o_p, o_s, w_gates_t, w_co_b, w_o_b, w_mix_b = _attention(
        q, ckv, krs, cache_ckv[:, l], cache_krs, w_ukv_slot, w_in_t, w_conv_out[l], w_o[l], w_mix_out[l])
    x1, h3, meta, cnt, gr = _post(xp, xs, mod, norm1[l][None, :], w_gates_t, zc, o_p, o_s, w_co_b, w_o_b,
                                  w_mix_b, norm2[l][None, :], w_route2)
    pos, sched = _plan(gr, cnt)
    hs, ms = _dispatch(pos, h3, meta)
    ys = _moe(sched, hs, ms, w_up[l], w_gate[l], w_down[l])
    yp, ysm = _final(pos, ys, x1, mod, final_norm[None, :])

    y_prompt = yp.reshape(BATCH, SEQ, D_MODEL)
    y_sample = ysm.reshape(DEC_BATCH, DEC_SEQ, D_MODEL)
    new_ckv = nckv.reshape(BATCH, 1, SEQ, KV_LORA)
    new_krope = jnp.swapaxes(nkr, 1, 2).reshape(BATCH, 1, SEQ, QK_ROPE)
    return (y_prompt, y_sample, new_ckv, new_krope)
```

```python
import numpy as np
import jax
import jax.numpy as jnp
from jax import lax
from jax.experimental import pallas as pl
from jax.experimental.pallas import tpu as pltpu

F32 = jnp.float32
BF16 = jnp.bfloat16

D_MODEL = 1024
BATCH = 16
SEQ = 256
DEC_BATCH = 2
DEC_SEQ = 1024
PAST_LEN = 256
GRID_W = 64
N_HEADS = 8
QK_NOPE = 64
QK_ROPE = 32
V_HEAD = 64
Q_LORA = 256
KV_LORA = 128
ROPE_AXIS = QK_ROPE // 2
ROPE_BASE = 10000.0
ATTN_SCALE = (QK_NOPE + QK_ROPE) ** -0.5
D_CONV = D_MODEL
N_GROUPS = 4
EXP_PER_GROUP = 8
N_EXPERTS = N_GROUPS * EXP_PER_GROUP
D_EXPERT = 256
EPS = 1e-6

T_P = BATCH * SEQ
T_S = DEC_BATCH * DEC_SEQ
T = T_P + T_S
N_COND = 8
LANE = 128
ROPE_LANE0 = QK_NOPE
SMALL_COLS = Q_LORA + KV_LORA + LANE
VMEM_LIMIT = 56 * 1024 * 1024

TM_IN = 1024
TM_POST = 512
TM_MOE = T // N_GROUPS
TM_FINAL = 512
TM_DISP = 1024
N_SLAB = D_MODEL // LANE
MOE_SUB = 256
MOE_EPS = 2
N_VISITS = T // TM_MOE + N_GROUPS - 1
GID_LANE = 40
RANK_LANE = 41
Q_BLK_S = 512
PROMPT_SEQS = 2
CONV_CHUNK = 256


def _dot(a, b):
    return jnp.dot(a, b, preferred_element_type=F32)


def _rms(x):
    return lax.rsqrt(jnp.mean(x * x, axis=-1, keepdims=True) + EPS)


def _slab(t):
    return pl.ds(pl.multiple_of(t * N_SLAB, N_SLAB), N_SLAB)


def _mod_row(i, tm):
    n_prompt = T_P // tm
    return jnp.where(i >= n_prompt, 1 + ((i - n_prompt) * tm) // DEC_SEQ, 0)


def _ada_kernel(cctx_ref, c_ref, w_ref, b_ref, o_ref):
    c = jnp.concatenate([cctx_ref[...], c_ref[...],
                         jnp.zeros((N_COND - 1 - DEC_BATCH, D_MODEL), F32)], axis=0)
    a = (c * jax.nn.sigmoid(c)).astype(BF16)
    o_ref[...] = _dot(a, w_ref[...].astype(BF16)) + b_ref[...]


def _ada(c_ctx, c, w_ada, b_ada):
    n = 6 * D_MODEL
    bn = 1536
    return pl.pallas_call(
        _ada_kernel,
        grid=(n // bn,),
        in_specs=[
            pl.BlockSpec((1, D_MODEL), lambda j: (0, 0)),
            pl.BlockSpec((DEC_BATCH, D_MODEL), lambda j: (0, 0)),
            pl.BlockSpec((D_MODEL, bn), lambda j: (0, j)),
            pl.BlockSpec((1, bn), lambda j: (0, j)),
        ],
        out_specs=pl.BlockSpec((N_COND, bn), lambda j: (0, j)),
        out_shape=jax.ShapeDtypeStruct((N_COND, n), F32),
        compiler_params=pltpu.CompilerParams(
            dimension_semantics=("parallel",), vmem_limit_bytes=VMEM_LIMIT),
        name="ada_mod",
    )(c_ctx, c, w_ada, b_ada)


O_CQ = 3 * D_CONV
O_KR = O_CQ + Q_LORA + KV_LORA
O_GATE = O_KR + QK_ROPE
SMALL_BLK = 512
NT = (((1,), (1,)), ((), ()))


def _dot_nt(a, bt):
    return lax.dot_general(a, bt, NT, preferred_element_type=F32)


def _stream_maps(tm):
    n_prompt = T_P // tm
    return (lambda i, *_: (jnp.minimum(i, n_prompt - 1), 0),
            lambda i, *_: (jnp.maximum(i - n_prompt, 0), 0))


def _inproj_kernel(xp_ref, xs_ref, mod_ref, n1_ref, wc_ref, ws_ref, cw_ref, qn_ref, kvn_ref, wuq_ref,
                   rope_ref, zc_ref, q_ref, ckv_ref, krs_ref, nckv_ref, nkr_ref):
    i = pl.program_id(0)
    is_sample = i >= T_P // TM_IN
    seq = jnp.where(is_sample, DEC_SEQ, SEQ)
    mod = mod_ref[pl.ds(_mod_row(i, TM_IN), 1), :]
    shift1 = mod[:, 0:D_MODEL]
    scale1 = mod[:, D_MODEL:2 * D_MODEL]
    x = jnp.where(is_sample, xs_ref[...], xp_ref[...])
    h = ((x * _rms(x)) * n1_ref[...]) * (1.0 + scale1) + shift1
    hb = h.astype(BF16)

    n_lat = Q_LORA + KV_LORA
    w_small = jnp.concatenate([
        ws_ref[0:n_lat, :].astype(BF16),
        jnp.zeros((ROPE_LANE0, D_MODEL), BF16),
        ws_ref[n_lat:n_lat + QK_ROPE, :].astype(BF16),
        jnp.zeros((LANE - ROPE_LANE0 - QK_ROPE, D_MODEL), BF16)], axis=0)
    sm = _dot_nt(hb, w_small)
    cq = sm[:, 0:Q_LORA]
    ckv_raw = sm[:, Q_LORA:Q_LORA + KV_LORA]
    krs = sm[:, Q_LORA + KV_LORA:SMALL_COLS]
    cqn = (cq * _rms(cq)) * qn_ref[...]
    q = _dot(cqn.astype(BF16), wuq_ref[...])
    ckv = (ckv_raw * _rms(ckv_raw)) * kvn_ref[...]
    ckv_ref[...] = ckv

    @pl.when(jnp.logical_not(is_sample))
    def _():
        nckv_ref[...] = ckv
        for s in range(TM_IN // SEQ):
            kt = krs[s * SEQ:(s + 1) * SEQ, :].T
            nkr_ref[s] = kt[ROPE_LANE0:ROPE_LANE0 + QK_ROPE, :]

    cos = rope_ref[0]
    sin_lo = rope_ref[1]
    sin_hi = rope_ref[2]

    def rot(v):
        return v * cos + pltpu.roll(v, 8, 1) * sin_lo + pltpu.roll(v, LANE - 8, 1) * sin_hi

    krs_ref[...] = rot(krs)
    for hh in range(N_HEADS):
        q_ref[:, LANE * hh:LANE * (hh + 1)] = rot(q[:, LANE * hh:LANE * (hh + 1)]).astype(BF16)

    pos = lax.broadcasted_iota(jnp.int32, (TM_IN, 1), 0) & (seq - 1)
    first = pos == 0
    last = pos == seq - 1
    for j in range(D_CONV // CONV_CHUNK):
        c0 = j * CONV_CHUNK
        bg = _dot_nt(hb, wc_ref[c0:c0 + CONV_CHUNK, :].astype(BF16))
        cg = _dot_nt(hb, wc_ref[D_CONV + c0:D_CONV + c0 + CONV_CHUNK, :].astype(BF16))
        ui = _dot_nt(hb, wc_ref[2 * D_CONV + c0:2 * D_CONV + c0 + CONV_CHUNK, :].astype(BF16))
        u = cg * ui
        u_prev = jnp.where(first, 0.0, pltpu.roll(u, 1, 0))
        u_next = jnp.where(last, 0.0, pltpu.roll(u, TM_IN - 1, 0))
        cw = cw_ref[:, c0:c0 + CONV_CHUNK]
        conv = u_prev * cw[0:1] + u * cw[1:2] + u_next * cw[2:3]
        zc_ref[:, c0:c0 + CONV_CHUNK] = (bg * conv).astype(BF16)


def _inproj(xp, xs, mod, norm1, w_in_t, conv_w, q_norm, kv_norm, w_uq_slot, rope_tabs):
    n_prompt = T_P // TM_IN
    const = lambda i: (0, 0)
    pmap, smap = _stream_maps(TM_IN)
    once = pl.Buffered(1)
    return pl.pallas_call(
        _inproj_kernel,
        grid=(T // TM_IN,),
        in_specs=[
            pl.BlockSpec((TM_IN, D_MODEL), pmap),
            pl.BlockSpec((TM_IN, D_MODEL), smap),
            pl.BlockSpec((N_COND, 6 * D_MODEL), const),
            pl.BlockSpec((1, D_MODEL), const),
            pl.BlockSpec((O_CQ, D_MODEL), const, pipeline_mode=once),
            pl.BlockSpec((SMALL_BLK, D_MODEL), lambda i: (O_CQ // SMALL_BLK, 0), pipeline_mode=once),
            pl.BlockSpec((3, D_CONV), const),
            pl.BlockSpec((1, Q_LORA), const),
            pl.BlockSpec((1, KV_LORA), const),
            pl.BlockSpec((Q_LORA, N_HEADS * LANE), const),
            pl.BlockSpec((None, 3, TM_IN, LANE),
                         lambda i: (jnp.where(i >= n_prompt, 1, 0), 0, 0, 0)),
        ],
        out_specs=[
            pl.BlockSpec((TM_IN, D_CONV), lambda i: (i, 0)),
            pl.BlockSpec((TM_IN, N_HEADS * LANE), lambda i: (i, 0)),
            pl.BlockSpec((TM_IN, KV_LORA), lambda i: (i, 0)),
            pl.BlockSpec((TM_IN, LANE), lambda i: (i, 0)),
            pl.BlockSpec((TM_IN, KV_LORA), pmap),
            pl.BlockSpec((TM_IN // SEQ, QK_ROPE, SEQ), lambda i: (jnp.minimum(i, n_prompt - 1), 0, 0)),
        ],
        out_shape=[
            jax.ShapeDtypeStruct((T, D_CONV), BF16),
            jax.ShapeDtypeStruct((T, N_HEADS * LANE), BF16),
            jax.ShapeDtypeStruct((T, KV_LORA), F32),
            jax.ShapeDtypeStruct((T, LANE), F32),
            jax.ShapeDtypeStruct((T_P, KV_LORA), F32),
            jax.ShapeDtypeStruct((BATCH, QK_ROPE, SEQ), F32),
        ],
        compiler_params=pltpu.CompilerParams(
            dimension_semantics=("arbitrary",), vmem_limit_bytes=VMEM_LIMIT),
        name="in_proj",
    )(xp, xs, mod, norm1, w_in_t, w_in_t, conv_w, q_norm, kv_norm, w_uq_slot, rope_tabs)


def _fill_kv(ckv, krs, wukv_ref, kf_scr, v_scr, off):
    m = ckv.shape[0]
    kv = _dot(ckv.astype(BF16), wukv_ref[...])
    for hh in range(N_HEADS):
        kf_scr[hh, off:off + m, :] = (kv[:, LANE * hh:LANE * (hh + 1)] + krs).astype(BF16)
    v_scr[off:off + m, :] = kv[:, N_HEADS * LANE:].astype(BF16)


def _attend(q_ref, r0, rows, kf_scr, v_scr, o_ref):
    for pair in range(N_HEADS // 2):
        acc = None
        for hh in (2 * pair, 2 * pair + 1):
            qh = q_ref[r0:r0 + rows, LANE * hh:LANE * (hh + 1)]
            s = _dot_nt(qh, kf_scr[hh]) * ATTN_SCALE
            e = jnp.exp(s - jnp.max(s, axis=-1, keepdims=True))
            p = (e / jnp.sum(e, axis=-1, keepdims=True)).astype(BF16)
            part = _dot(p, v_scr[:, LANE * hh:LANE * (hh + 1)])
            acc = part if acc is None else acc + part
        o_ref[r0:r0 + rows, LANE * pair:LANE * (pair + 1)] = acc.astype(BF16)


def _attn_prompt_kernel(q_ref, ckv_ref, krs_ref, wukv_ref, o_ref):
    for s in range(PROMPT_SEQS):
        r0 = s * SEQ
        kv = _dot(ckv_ref[r0:r0 + SEQ, :].astype(BF16), wukv_ref[...])
        krs = krs_ref[r0:r0 + SEQ, :]
        for pair in range(N_HEADS // 2):
            acc = None
            for hh in (2 * pair, 2 * pair + 1):
                kf = (kv[:, LANE * hh:LANE * (hh + 1)] + krs).astype(BF16)
                vh = kv[:, LANE * (N_HEADS + hh):LANE * (N_HEADS + hh + 1)].astype(BF16)
                sc = _dot_nt(q_ref[r0:r0 + SEQ, LANE * hh:LANE * (hh + 1)], kf) * ATTN_SCALE
                e = jnp.exp(sc - jnp.max(sc, axis=-1, keepdims=True))
                p = (e / jnp.sum(e, axis=-1, keepdims=True)).astype(BF16)
                part = _dot(p, vh)
                acc = part if acc is None else acc + part
            o_ref[r0:r0 + SEQ, LANE * pair:LANE * (pair + 1)] = acc.astype(BF16)


def _attn_sample_kernel(q_ref, ckv_ref, cckv_ref, krs_ref, ckrs_ref, wukv_ref, wg_ref, wco_ref, wo_ref,
                        wmix_ref, o_ref, g_ref, co_ref, ob_ref, mix_ref, kf_scr, v_scr):
    @pl.when(pl.program_id(1) == 0)
    def _():
        _fill_kv(ckv_ref[...], krs_ref[...], wukv_ref, kf_scr, v_scr, 0)
        _fill_kv(cckv_ref[...], ckrs_ref[...], wukv_ref, kf_scr, v_scr, DEC_SEQ)

    _attend(q_ref, 0, Q_BLK_S, kf_scr, v_scr, o_ref)
    g_ref[...] = wg_ref[...].astype(BF16)
    co_ref[...] = wco_ref[...].astype(BF16)
    ob_ref[...] = wo_ref[...].astype(BF16)
    mix_ref[...] = wmix_ref[...].astype(BF16)


def _attention(q, ckv, krs, cache_ckv, cache_krs, w_ukv_slot, w_in_t, w_conv_out, w_o, w_mix_out):
    kv_cols = 2 * N_HEADS * LANE
    n_o = N_HEADS * V_HEAD
    steps = BATCH // PROMPT_SEQS
    rows = PROMPT_SEQS * SEQ
    o_prompt = pl.pallas_call(
        _attn_prompt_kernel,
        grid=(steps,),
        in_specs=[
            pl.BlockSpec((rows, N_HEADS * LANE), lambda b: (b, 0)),
            pl.BlockSpec((rows, KV_LORA), lambda b: (b, 0)),
            pl.BlockSpec((rows, LANE), lambda b: (b, 0)),
            pl.BlockSpec((KV_LORA, kv_cols), lambda b: (0, 0)),
        ],
        out_specs=pl.BlockSpec((rows, n_o), lambda b: (b, 0)),
        out_shape=jax.ShapeDtypeStruct((T_P, n_o), BF16),
        compiler_params=pltpu.CompilerParams(dimension_semantics=("parallel",),
                                             vmem_limit_bytes=VMEM_LIMIT),
        name="attn_prompt",
    )(q, ckv, krs, w_ukv_slot)

    m_all = DEC_SEQ + PAST_LEN
    nq = DEC_SEQ // Q_BLK_S
    q0 = T_P // Q_BLK_S
    s0 = T_P // DEC_SEQ
    n_step = DEC_BATCH * nq
    share = lambda n: pl.BlockSpec((n // n_step, D_MODEL), lambda b, j: (b * nq + j, 0))
    gate_rows = 2 * D_MODEL // n_step
    o_sample, w_gates_t, w_co_b, w_o_b, w_mix_b = pl.pallas_call(
        _attn_sample_kernel,
        grid=(DEC_BATCH, nq),
        in_specs=[
            pl.BlockSpec((Q_BLK_S, N_HEADS * LANE), lambda b, j: (q0 + b * nq + j, 0)),
            pl.BlockSpec((DEC_SEQ, KV_LORA), lambda b, j: (s0 + b, 0)),
            pl.BlockSpec((None, PAST_LEN, KV_LORA), lambda b, j: (b, 0, 0)),
            pl.BlockSpec((DEC_SEQ, LANE), lambda b, j: (s0 + b, 0)),
            pl.BlockSpec((None, PAST_LEN, LANE), lambda b, j: (b, 0, 0)),
            pl.BlockSpec((KV_LORA, kv_cols), lambda b, j: (0, 0)),
            pl.BlockSpec((pl.Element(gate_rows), pl.Element(D_MODEL)),
                         lambda b, j: (pl.multiple_of(O_GATE + (b * nq + j) * gate_rows, 32), 0)),
            share(D_CONV), share(n_o), share(D_MODEL),
        ],
        out_specs=[pl.BlockSpec((Q_BLK_S, n_o), lambda b, j: (b * nq + j, 0)),
                   share(2 * D_MODEL), share(D_CONV), share(n_o), share(D_MODEL)],
        out_shape=[jax.ShapeDtypeStruct((T_S, n_o), BF16),
                   jax.ShapeDtypeStruct((2 * D_MODEL, D_MODEL), BF16),
                   jax.ShapeDtypeStruct((D_CONV, D_MODEL), BF16),
                   jax.ShapeDtypeStruct((n_o, D_MODEL), BF16),
                   jax.ShapeDtypeStruct((D_MODEL, D_MODEL), BF16)],
        scratch_shapes=[pltpu.VMEM((N_HEADS, m_all, LANE), BF16),
                        pltpu.VMEM((m_all, N_HEADS * LANE), BF16)],
        compiler_params=pltpu.CompilerParams(dimension_semantics=("arbitrary", "arbitrary"),
                                             vmem_limit_bytes=VMEM_LIMIT),
        name="attn_sample",
    )(q, ckv, cache_ckv, krs, cache_krs, w_ukv_slot, w_in_t, w_conv_out, w_o, w_mix_out)
    return o_prompt, o_sample, w_gates_t, w_co_b, w_o_b, w_mix_b


def _route(logits):
    lane = lax.broadcasted_iota(jnp.int32, logits.shape, 1)
    neg = -jnp.inf
    big = jnp.int32(1 << 20)
    gmask = (lane >= N_EXPERTS) & (lane < N_EXPERTS + N_GROUPS)
    gl = jnp.where(gmask, logits, neg)
    gmax = jnp.max(gl, axis=-1, keepdims=True)
    gsum = jnp.sum(jnp.where(gmask, jnp.exp(gl - gmax), 0.0), axis=-1, keepdims=True)
    p_g = 1.0 / gsum
    g_idx = jnp.min(jnp.where(gl == gmax, lane, big), axis=-1, keepdims=True) - N_EXPERTS

    emask = (lane < N_EXPERTS) & ((lane >> 3) == g_idx)
    el = jnp.where(emask, logits, neg)
    m1 = jnp.max(el, axis=-1, keepdims=True)
    i1 = jnp.min(jnp.where(el == m1, lane, big), axis=-1, keepdims=True)
    el2 = jnp.where(lane == i1, neg, el)
    m2 = jnp.max(el2, axis=-1, keepdims=True)
    i2 = jnp.min(jnp.where(el2 == m2, lane, big), axis=-1, keepdims=True)
    z = jnp.sum(jnp.where(emask, jnp.exp(el - m1), 0.0), axis=-1, keepdims=True)
    p1 = 1.0 / z
    p2 = jnp.exp(m2 - m1) / z
    tot = p1 + p2
    w1 = p_g * p1 / tot
    w2 = p_g * p2 / tot
    return jnp.where(lane == i1, w1, 0.0) + jnp.where(lane == i2, w2, 0.0), g_idx


def _post_kernel(xp_ref, xs_ref, mod_ref, n1_ref, wg_ref, zc_ref, op_ref, os_ref, wco_ref, wo_ref,
                 wmix_ref, n2_ref, wr_ref, x1_ref, h3_ref, meta_ref, cnt_ref, gr_ref):
    i = pl.program_id(0)
    is_sample = i >= T_P // TM_POST
    x = jnp.where(is_sample, xs_ref[...], xp_ref[...])
    o = jnp.where(is_sample, os_ref[...], op_ref[...])
    mod = mod_ref[pl.ds(_mod_row(i, TM_POST), 1), :]
    shift1 = mod[:, 0:D_MODEL]
    scale1 = mod[:, D_MODEL:2 * D_MODEL]
    gate1 = mod[:, 2 * D_MODEL:3 * D_MODEL]
    shift2 = mod[:, 3 * D_MODEL:4 * D_MODEL]
    scale2 = mod[:, 4 * D_MODEL:5 * D_MODEL]
    y_conv = _dot(zc_ref[...], wco_ref[...])
    y_mla = _dot(o, wo_ref[...])
    h = ((x * _rms(x)) * n1_ref[...]) * (1.0 + scale1) + shift1
    g = _dot_nt(h.astype(BF16), wg_ref[...])
    merged = (jax.nn.sigmoid(g[:, 0:D_MODEL]) * y_conv
              + jax.nn.sigmoid(g[:, D_MODEL:2 * D_MODEL]) * y_mla)
    y = _dot(merged.astype(BF16), wmix_ref[...])
    x1 = x + gate1 * y
    x1_ref[...] = x1
    h2 = ((x1 * _rms(x1)) * n2_ref[...]) * (1.0 + scale2) + shift2
    h2_hi = h2.astype(BF16)
    h2_lo = (h2 - h2_hi.astype(F32)).astype(BF16)
    hh = _dot(h2_hi, wr_ref[...])
    logits = hh[:, 0:LANE] + hh[:, LANE:2 * LANE] + _dot(h2_lo, wr_ref[:, 0:LANE])
    comb, g_idx = _route(logits)

    lane = lax.broadcasted_iota(jnp.int32, comb.shape, 1)
    onehot = lane == g_idx + N_EXPERTS
    r_i = lax.broadcasted_iota(jnp.int32, (TM_POST, TM_POST), 0)
    c_i = lax.broadcasted_iota(jnp.int32, (TM_POST, TM_POST), 1)
    lower = jnp.where(c_i < r_i, 1.0, 0.0).astype(BF16)
    before = _dot(lower, jnp.where(onehot, 1.0, 0.0).astype(BF16))
    rank = jnp.sum(jnp.where(onehot, before, 0.0), axis=-1, keepdims=True)
    counts = jnp.sum(jnp.where(onehot, 1.0, 0.0), axis=0, keepdims=True)
    cnt_ref[...] = jnp.broadcast_to(counts, cnt_ref.shape)

    meta_ref[...] = comb
    idx = jnp.where(lane == GID_LANE, g_idx.astype(F32), 0.0) + jnp.where(lane == RANK_LANE, rank, 0.0)
    idx_hi = idx.astype(BF16)
    idx_lo = (idx - idx_hi.astype(F32)).astype(BF16)
    s_row = lax.broadcasted_iota(jnp.int32, (8, LANE), 0)
    s_lane = lax.broadcasted_iota(jnp.int32, (8, LANE), 1)
    sel = jnp.where(s_lane == GID_LANE + s_row, 1.0, 0.0).astype(BF16)
    gr_ref[...] = _dot_nt(sel, idx_hi) + _dot_nt(sel, idx_lo)
    for c in range(N_SLAB):
        h3_ref[pl.ds(c, TM_POST, stride=N_SLAB), :] = h2[:, LANE * c:LANE * (c + 1)]


def _post(xp, xs, mod, norm1, w_gates_t, zc, o_p, o_s, w_conv_out, w_o, w_mix_out, norm2, w_route):
    const = lambda i: (0, 0)
    row = lambda i: (i, 0)
    pmap, smap = _stream_maps(TM_POST)
    return pl.pallas_call(
        _post_kernel,
        grid=(T // TM_POST,),
        in_specs=[
            pl.BlockSpec((TM_POST, D_MODEL), pmap),
            pl.BlockSpec((TM_POST, D_MODEL), smap),
            pl.BlockSpec((N_COND, 6 * D_MODEL), const),
            pl.BlockSpec((1, D_MODEL), const),
            pl.BlockSpec((2 * D_MODEL, D_MODEL), const),
            pl.BlockSpec((TM_POST, D_CONV), row),
            pl.BlockSpec((TM_POST, N_HEADS * V_HEAD), pmap),
            pl.BlockSpec((TM_POST, N_HEADS * V_HEAD), smap),
            pl.BlockSpec((D_CONV, D_MODEL), const),
            pl.BlockSpec((N_HEADS * V_HEAD, D_MODEL), const),
            pl.BlockSpec((D_MODEL, D_MODEL), const),
            pl.BlockSpec((1, D_MODEL), const),
            pl.BlockSpec((D_MODEL, 2 * LANE), const),
        ],
        out_specs=[
            pl.BlockSpec((TM_POST, D_MODEL), row),
            pl.BlockSpec((TM_POST * N_SLAB, LANE), row),
            pl.BlockSpec((TM_POST, LANE), row),
            pl.BlockSpec((None, 8, LANE), lambda i: (i, 0, 0)),
            pl.BlockSpec((8, TM_POST), lambda i: (0, i)),
        ],
        out_shape=[
            jax.ShapeDtypeStruct((T, D_MODEL), F32),
            jax.ShapeDtypeStruct((T * N_SLAB, LANE), F32),
            jax.ShapeDtypeStruct((T, LANE), F32),
            jax.ShapeDtypeStruct((T // TM_POST, 8, LANE), F32),
            jax.ShapeDtypeStruct((8, T), F32),
        ],
        compiler_params=pltpu.CompilerParams(
            dimension_semantics=("parallel",), vmem_limit_bytes=VMEM_LIMIT),
        name="post_mixer",
    )(xp, xs, mod, norm1, w_gates_t, zc, o_p, o_s, w_conv_out, w_o, w_mix_out, norm2, w_route)


DISP_IN = T // TM_DISP
DISP_OUT = T // TM_MOE


def _dispatch_kernel(pos_ref, h3_ref, m_ref, hs_ref, ms_ref, xs_ref, mss_ref):
    i = pl.program_id(0)

    @pl.when(i < DISP_IN)
    def _():
        base = i * TM_DISP

        def body(r, carry):
            p = pos_ref[base + r]
            xs_ref[_slab(p), :] = h3_ref[_slab(r), :]
            mss_ref[pl.ds(p, 1), :] = m_ref[pl.ds(r, 1), :]
            return carry

        lax.fori_loop(0, TM_DISP, body, 0, unroll=8)

    @pl.when(i >= DISP_IN)
    def _():
        row0 = pl.multiple_of((i - DISP_IN) * TM_MOE, TM_MOE)
        for c in range(N_SLAB):
            hs_ref[:, LANE * c:LANE * (c + 1)] = (
                xs_ref[pl.ds(row0 * N_SLAB + c, TM_MOE, stride=N_SLAB), :].astype(BF16))
        ms_ref[...] = mss_ref[pl.ds(row0, TM_MOE), :]


def _dispatch(pos, h3, meta):
    n_slab = N_SLAB
    in_map = lambda i, pos: (jnp.minimum(i, DISP_IN - 1), 0)
    out_map = lambda i, pos: (jnp.maximum(i - DISP_IN, 0), 0)
    return pl.pallas_call(
        _dispatch_kernel,
        grid_spec=pltpu.PrefetchScalarGridSpec(
            num_scalar_prefetch=1,
            grid=(DISP_IN + DISP_OUT,),
            in_specs=[pl.BlockSpec((TM_DISP * n_slab, LANE), in_map),
                      pl.BlockSpec((TM_DISP, LANE), in_map)],
            out_specs=[pl.BlockSpec((TM_MOE, D_MODEL), out_map),
                       pl.BlockSpec((TM_MOE, LANE), out_map)],
            scratch_shapes=[pltpu.VMEM((T * n_slab, LANE), F32),
                            pltpu.VMEM((T, LANE), F32)],
        ),
        out_shape=[jax.ShapeDtypeStruct((T, D_MODEL), BF16),
                   jax.ShapeDtypeStruct((T, LANE), F32)],
        compiler_params=pltpu.CompilerParams(
            dimension_semantics=("arbitrary",), vmem_limit_bytes=VMEM_LIMIT),
        name="moe_dispatch",
    )(pos, h3, meta)


def _moe_kernel(sched_ref, hs_ref, ms_ref, wup_ref, wgate_ref, wdown_ref, y3_ref, acc_ref):
    v = pl.program_id(0)
    j = pl.program_id(1)
    valid = sched_ref[V_VALID, v] == 1
    lo = sched_ref[V_LO, v]
    hi = sched_ref[V_HI, v]
    e0 = sched_ref[V_GROUP, v] * EXP_PER_GROUP + j * MOE_EPS
    full = (hi - lo) * 4 >= TM_MOE * 3

    @pl.when(valid & (j == 0) & (sched_ref[V_FIRST, v] == 1))
    def _():
        acc_ref[...] = jnp.zeros_like(acc_ref)

    def expert_rows(r0, rows):
        w_in2 = jnp.concatenate(
            [w[k].astype(BF16) for k in range(MOE_EPS) for w in (wup_ref, wgate_ref)], axis=1)
        ag = _dot(hs_ref[pl.ds(r0, rows), :], w_in2)
        comb = ms_ref[pl.ds(r0, rows), :]
        lane = lax.broadcasted_iota(jnp.int32, comb.shape, 1)
        acts = []
        for k in range(MOE_EPS):
            a = ag[:, 2 * k * D_EXPERT:(2 * k + 1) * D_EXPERT]
            g = ag[:, (2 * k + 1) * D_EXPERT:(2 * k + 2) * D_EXPERT]
            cw = jnp.sum(jnp.where(lane == e0 + k, comb, 0.0), axis=-1, keepdims=True)
            acts.append(((g * jax.nn.sigmoid(g)) * a * cw).astype(BF16))
        w_out = jnp.concatenate([wdown_ref[k].astype(BF16) for k in range(MOE_EPS)], axis=0)
        acc_ref[pl.ds(r0, rows), :] += _dot(jnp.concatenate(acts, axis=1), w_out)

    @pl.when(valid & full)
    def _():
        expert_rows(0, TM_MOE)

    @pl.when(valid & jnp.logical_not(full))
    def _():
        def sub_block(s, carry):
            expert_rows(pl.multiple_of(s * MOE_SUB, MOE_SUB), MOE_SUB)
            return carry

        lax.fori_loop(lo // MOE_SUB, (hi + MOE_SUB - 1) // MOE_SUB, sub_block, 0)

    @pl.when(valid & (j == EXP_PER_GROUP // MOE_EPS - 1) & (sched_ref[V_LAST, v] == 1))
    def _():
        for c in range(N_SLAB):
            y3_ref[pl.ds(c, TM_MOE, stride=N_SLAB), :] = acc_ref[:, LANE * c:LANE * (c + 1)]


def _moe(sched, hs, ms, w_up, w_gate, w_down):
    steps = EXP_PER_GROUP // MOE_EPS
    wmap = lambda v, j, sched: (
        sched[V_GROUP, v] * steps + jnp.where(sched[V_VALID, v] == 1, j, steps - 1), 0, 0)
    tmap = lambda v, j, sched: (sched[V_TILE, v], 0)
    n_slab = D_MODEL // LANE
    return pl.pallas_call(
        _moe_kernel,
        grid_spec=pltpu.PrefetchScalarGridSpec(
            num_scalar_prefetch=1,
            grid=(N_VISITS, steps),
            in_specs=[
                pl.BlockSpec((TM_MOE, D_MODEL), tmap),
                pl.BlockSpec((TM_MOE, LANE), tmap),
                pl.BlockSpec((MOE_EPS, D_MODEL, D_EXPERT), wmap),
                pl.BlockSpec((MOE_EPS, D_MODEL, D_EXPERT), wmap),
                pl.BlockSpec((MOE_EPS, D_EXPERT, D_MODEL), wmap),
            ],
            out_specs=pl.BlockSpec((TM_MOE * n_slab, LANE), tmap),
            scratch_shapes=[pltpu.VMEM((TM_MOE, D_MODEL), F32)],
        ),
        out_shape=jax.ShapeDtypeStruct((T * n_slab, LANE), F32),
        compiler_params=pltpu.CompilerParams(
            dimension_semantics=("arbitrary", "arbitrary"), vmem_limit_bytes=VMEM_LIMIT),
        name="moe_grouped",
    )(sched, hs, ms, w_up, w_gate, w_down)


def _final_kernel(pos_ref, ys_ref, x1_ref, mod_ref, fn_ref, yp_ref, ysm_ref, g_ref):
    i = pl.program_id(0)
    base = i * TM_FINAL

    def body(r, carry):
        g_ref[_slab(r), :] = ys_ref[_slab(pos_ref[base + r]), :]
        return carry

    lax.fori_loop(0, TM_FINAL, body, 0, unroll=8)
    mod = mod_ref[pl.ds(_mod_row(i, TM_FINAL), 1), :]
    gate2 = mod[:, 5 * D_MODEL:6 * D_MODEL]
    moe = jnp.concatenate([g_ref[pl.ds(c, TM_FINAL, stride=N_SLAB), :] for c in range(N_SLAB)], axis=1)
    x2 = x1_ref[...] + gate2 * moe
    y = (x2 * _rms(x2)) * fn_ref[...]
    is_sample = i >= T_P // TM_FINAL

    @pl.when(jnp.logical_not(is_sample))
    def _():
        yp_ref[...] = y

    @pl.when(is_sample)
    def _():
        ysm_ref[...] = y


def _final(pos, ys, x1, mod, final_norm):
    n_slab = D_MODEL // LANE
    pmap, smap = _stream_maps(TM_FINAL)
    return pl.pallas_call(
        _final_kernel,
        grid_spec=pltpu.PrefetchScalarGridSpec(
            num_scalar_prefetch=1,
            grid=(T // TM_FINAL,),
            in_specs=[
                pl.BlockSpec((T * n_slab, LANE), lambda i, pos: (0, 0), pipeline_mode=pl.Buffered(1)),
                pl.BlockSpec((TM_FINAL, D_MODEL), lambda i, pos: (i, 0)),
                pl.BlockSpec((N_COND, 6 * D_MODEL), lambda i, pos: (0, 0)),
                pl.BlockSpec((1, D_MODEL), lambda i, pos: (0, 0)),
            ],
            out_specs=[pl.BlockSpec((TM_FINAL, D_MODEL), pmap),
                       pl.BlockSpec((TM_FINAL, D_MODEL), smap)],
            scratch_shapes=[pltpu.VMEM((TM_FINAL * n_slab, LANE), F32)],
        ),
        out_shape=[jax.ShapeDtypeStruct((T_P, D_MODEL), F32),
                   jax.ShapeDtypeStruct((T_S, D_MODEL), F32)],
        compiler_params=pltpu.CompilerParams(
            dimension_semantics=("arbitrary",), vmem_limit_bytes=VMEM_LIMIT),
        name="moe_unsort_final",
    )(pos, ys, x1, mod, final_norm)


def _plan_kernel(gr_ref, cnt_ref, pos_ref, sched_ref):
    n_post = T // TM_POST
    lane = lax.broadcasted_iota(jnp.int32, (1, LANE), 1)
    grp_lane = lambda v, g: v[:, N_EXPERTS + g:N_EXPERTS + g + 1]
    counts = [cnt_ref[k, 0:1, :] for k in range(n_post)]
    gtot_v = counts[0]
    for k in range(1, n_post):
        gtot_v = gtot_v + counts[k]
    gtot = [grp_lane(gtot_v, g) for g in range(N_GROUPS)]
    goff = [jnp.zeros((1, 1), F32)]
    for g in range(1, N_GROUPS):
        goff.append(goff[-1] + gtot[g - 1])

    before = [jnp.zeros((1, 1), F32) for _ in range(N_GROUPS)]
    rows_per_tile = TM_POST // LANE
    for k in range(n_post):
        gid = gr_ref[0:1, TM_POST * k:TM_POST * (k + 1)]
        pos = gr_ref[1:2, TM_POST * k:TM_POST * (k + 1)]
        for g in range(N_GROUPS):
            pos = pos + jnp.where(gid == float(g), goff[g] + before[g], 0.0)
            before[g] = before[g] + grp_lane(counts[k], g)
        for r in range(rows_per_tile):
            pos_ref[rows_per_tile * k + r:rows_per_tile * k + r + 1, :] = (
                pos[:, LANE * r:LANE * (r + 1)].astype(jnp.int32))

    zero = jnp.zeros((1, LANE), F32)
    rows = {name: zero for name in ("vt", "vg", "vlo", "vhi")}
    slot = jnp.zeros((1, 1), F32)
    last_t = jnp.zeros((1, 1), F32)
    last_g = jnp.zeros((1, 1), F32)
    for i in range(T // TM_MOE):
        for g in range(N_GROUPS):
            lo = jnp.clip(goff[g] - float(TM_MOE * i), 0.0, float(TM_MOE))
            hi = jnp.clip(goff[g] + gtot[g] - float(TM_MOE * i), 0.0, float(TM_MOE))
            ok = hi > lo
            here = ok & (lane == slot.astype(jnp.int32))
            rows["vt"] = jnp.where(here, float(i), rows["vt"])
            rows["vg"] = jnp.where(here, float(g), rows["vg"])
            rows["vlo"] = jnp.where(here, lo, rows["vlo"])
            rows["vhi"] = jnp.where(here, hi, rows["vhi"])
            last_t = jnp.where(ok, float(i), last_t)
            last_g = jnp.where(ok, float(g), last_g)
            slot = slot + jnp.where(ok, 1.0, 0.0)
    n_ok = slot.astype(jnp.int32)
    valid = lane < n_ok
    vt = jnp.where(valid, rows["vt"], last_t)
    vg = jnp.where(valid, rows["vg"], last_g)
    prev_t = pltpu.roll(jnp.broadcast_to(vt, (8, LANE)), 1, 1)[0:1, :]
    next_t = pltpu.roll(jnp.broadcast_to(vt, (8, LANE)), LANE - 1, 1)[0:1, :]
    first = jnp.where((lane == 0) | (vt != prev_t), 1.0, 0.0)
    last = jnp.where((lane == n_ok - 1) | (vt != next_t), 1.0, 0.0)
    table = [vt, vg, rows["vlo"], rows["vhi"], first, last, jnp.where(valid, 1.0, 0.0), zero]
    for r, row in enumerate(table):
        sched_ref[r:r + 1, :] = row.astype(jnp.int32)


V_TILE, V_GROUP, V_LO, V_HI, V_FIRST, V_LAST, V_VALID = range(7)


def _plan(gr, cnt):
    pos, sched = pl.pallas_call(
        _plan_kernel,
        out_shape=[jax.ShapeDtypeStruct((T // LANE, LANE), jnp.int32),
                   jax.ShapeDtypeStruct((8, LANE), jnp.int32)],
        name="moe_plan",
    )(gr, cnt)
    return pos.reshape(T), sched


def _rope_tables():
    n = np.arange(DEC_SEQ)
    pos = np.stack([n // GRID_W, n % GRID_W], axis=1).astype(np.float32)
    half = ROPE_AXIS // 2
    inv = (1.0 / (ROPE_BASE ** (np.arange(0, ROPE_AXIS, 2, dtype=np.float32) / ROPE_AXIS))).astype(np.float32)
    ang = (pos[:, :, None] * inv[None, None, :]).astype(np.float32)
    cos = np.cos(ang).astype(np.float32)
    sin = np.sin(ang).astype(np.float32)
    tabs = np.zeros((2, 3, DEC_SEQ, LANE), np.float32)
    tabs[:, 0] = 1.0
    for a in range(2):
        lo = ROPE_LANE0 + a * ROPE_AXIS
        tabs[1, 0, :, lo:lo + half] = cos[:, a]
        tabs[1, 0, :, lo + half:lo + 2 * half] = cos[:, a]
        tabs[1, 1, :, lo + half:lo + 2 * half] = sin[:, a]
        tabs[1, 2, :, lo:lo + half] = -sin[:, a]
    return jnp.asarray(tabs)


def kernel(x_prompt, x_sample, cache_ckv, cache_krope, c, c_ctx, norm1, w_ada, b_ada, w_in, conv_w,
           w_conv_out, q_norm, w_uq, kv_norm, w_ukv, w_o, w_mix_out, norm2, w_grp, w_exp, w_up,
           w_gate, w_down, final_norm):
    l = 0
    xp = x_prompt.reshape(T_P, D_MODEL)
    xs = x_sample.reshape(T_S, D_MODEL)
    mod = _ada(c_ctx[None, :], c, w_ada[l], b_ada[l][None, :])

    w_in_t = w_in[l].T
    w_uq_slot = jnp.pad(w_uq[l].reshape(Q_LORA, N_HEADS, QK_NOPE + QK_ROPE),
                        ((0, 0), (0, 0), (0, LANE - QK_NOPE - QK_ROPE))
                        ).reshape(Q_LORA, N_HEADS * LANE).astype(BF16)
    wkv = w_ukv[l].reshape(KV_LORA, N_HEADS, QK_NOPE + V_HEAD)
    wk_slot = jnp.pad(wkv[:, :, :QK_NOPE], ((0, 0), (0, 0), (0, LANE - QK_NOPE)))
    wv = wkv[:, :, QK_NOPE:].reshape(KV_LORA, N_HEADS // 2, 2, V_HEAD)
    zero = jnp.zeros_like(wv[:, :, 0])
    wv_slot = jnp.stack([jnp.concatenate([wv[:, :, 0], zero], axis=-1),
                         jnp.concatenate([zero, wv[:, :, 1]], axis=-1)], axis=2)
    w_ukv_slot = jnp.concatenate([wk_slot.reshape(KV_LORA, N_HEADS * LANE),
                                  wv_slot.reshape(KV_LORA, N_HEADS * LANE)], axis=1).astype(BF16)
    w_route = jnp.pad(jnp.concatenate([w_exp[l], w_grp[l]], axis=1),
                      ((0, 0), (0, LANE - N_EXPERTS - N_GROUPS)))
    w_route_hi = w_route.astype(BF16)
    w_route_lo = (w_route - w_route_hi.astype(F32)).astype(BF16)
    w_route2 = jnp.concatenate([w_route_hi, w_route_lo], axis=1)
    cache_krs = jnp.pad(cache_krope[:, l], ((0, 0), (0, 0), (ROPE_LANE0, LANE - ROPE_LANE0 - QK_ROPE)))

    zc, q, ckv, krs, nckv, nkr = _inproj(xp, xs, mod, norm1[l][None, :], w_in_t, conv_w[l],
                              q_norm[l][None, :], kv_norm[l][None, :], w_uq_slot, _rope_tables())
    o_p, o_s, w_gates_t, w_co_b, w_o_b, w_mix_b = _attention(
        q, ckv, krs, cache_ckv[:, l], cache_krs, w_ukv_slot, w_in_t, w_conv_out[l], w_o[l], w_mix_out[l])
    x1, h3, meta, cnt, gr = _post(xp, xs, mod, norm1[l][None, :], w_gates_t, zc, o_p, o_s, w_co_b, w_o_b,
                                  w_mix_b, norm2[l][None, :], w_route2)
    pos, sched = _plan(gr, cnt)
    hs, ms = _dispatch(pos, h3, meta)
    ys = _moe(sched, hs, ms, w_up[l], w_gate[l], w_down[l])
    yp, ysm = _final(pos, ys, x1, mod, final_norm[None, :])

    y_prompt = yp.reshape(BATCH, SEQ, D_MODEL)
    y_sample = ysm.reshape(DEC_BATCH, DEC_SEQ, D_MODEL)
    new_ckv = nckv.reshape(BATCH, 1, SEQ, KV_LORA)
    new_krope = jnp.swapaxes(nkr, 1, 2).reshape(BATCH, 1, SEQ, QK_ROPE)
    return (y_prompt, y_sample, new_ckv, new_krope)
```

```python
import numpy as np
import jax
import jax.numpy as jnp
from jax import lax
from jax.experimental import pallas as pl
from jax.experimental.pallas import tpu as pltpu

F32 = jnp.float32
BF16 = jnp.bfloat16

D_MODEL = 1024
BATCH = 16
SEQ = 256
DEC_BATCH = 2
DEC_SEQ = 1024
PAST_LEN = 256
GRID_W = 64
N_HEADS = 8
QK_NOPE = 64
QK_ROPE = 32
V_HEAD = 64
Q_LORA = 256
KV_LORA = 128
ROPE_AXIS = QK_ROPE // 2
ROPE_BASE = 10000.0
ATTN_SCALE = (QK_NOPE + QK_ROPE) ** -0.5
D_CONV = D_MODEL
N_GROUPS = 4
EXP_PER_GROUP = 8
N_EXPERTS = N_GROUPS * EXP_PER_GROUP
D_EXPERT = 256
EPS = 1e-6

T_P = BATCH * SEQ
T_S = DEC_BATCH * DEC_SEQ
T = T_P + T_S
N_COND = 8
LANE = 128
ROPE_LANE0 = QK_NOPE
SMALL_COLS = Q_LORA + KV_LORA + LANE
VMEM_LIMIT = 56 * 1024 * 1024

TM_IN = 1024
TM_POST = 512
TM_MOE = T // N_GROUPS
TM_FINAL = 512
TM_DISP = 1024
N_SLAB = D_MODEL // LANE
MOE_SUB = 256
MOE_EPS = 2
N_VISITS = T // TM_MOE + N_GROUPS - 1
GID_LANE = 40
RANK_LANE = 41
Q_BLK_S = 512
PROMPT_SEQS = 2
CONV_CHUNK = 256


def _dot(a, b):
    return jnp.dot(a, b, preferred_element_type=F32)


def _rms(x):
    return lax.rsqrt(jnp.mean(x * x, axis=-1, keepdims=True) + EPS)


def _slab(t):
    return pl.ds(pl.multiple_of(t * N_SLAB, N_SLAB), N_SLAB)


def _mod_row(i, tm):
    n_prompt = T_P // tm
    return jnp.where(i >= n_prompt, 1 + ((i - n_prompt) * tm) // DEC_SEQ, 0)


def _ada_kernel(cctx_ref, c_ref, w_ref, b_ref, o_ref):
    c = jnp.concatenate([cctx_ref[...], c_ref[...],
                         jnp.zeros((N_COND - 1 - DEC_BATCH, D_MODEL), F32)], axis=0)
    a = (c * jax.nn.sigmoid(c)).astype(BF16)
    o_ref[...] = _dot(a, w_ref[...].astype(BF16)) + b_ref[...]


def _ada(c_ctx, c, w_ada, b_ada):
    n = 6 * D_MODEL
    bn = 1536
    return pl.pallas_call(
        _ada_kernel,
        grid=(n // bn,),
        in_specs=[
            pl.BlockSpec((1, D_MODEL), lambda j: (0, 0)),
            pl.BlockSpec((DEC_BATCH, D_MODEL), lambda j: (0, 0)),
            pl.BlockSpec((D_MODEL, bn), lambda j: (0, j)),
            pl.BlockSpec((1, bn), lambda j: (0, j)),
        ],
        out_specs=pl.BlockSpec((N_COND, bn), lambda j: (0, j)),
        out_shape=jax.ShapeDtypeStruct((N_COND, n), F32),
        compiler_params=pltpu.CompilerParams(
            dimension_semantics=("parallel",), vmem_limit_bytes=VMEM_LIMIT),
        name="ada_mod",
    )(c_ctx, c, w_ada, b_ada)


O_CQ = 3 * D_CONV
O_KR = O_CQ + Q_LORA + KV_LORA
O_GATE = O_KR + QK_ROPE
SMALL_BLK = 512
NT = (((1,), (1,)), ((), ()))


def _dot_nt(a, bt):
    return lax.dot_general(a, bt, NT, preferred_element_type=F32)


def _stream_maps(tm):
    n_prompt = T_P // tm
    return (lambda i, *_: (jnp.minimum(i, n_prompt - 1), 0),
            lambda i, *_: (jnp.maximum(i - n_prompt, 0), 0))


def _inproj_kernel(xp_ref, xs_ref, mod_ref, n1_ref, wc_ref, ws_ref, cw_ref, qn_ref, kvn_ref, wuq_ref,
                   rope_ref, zc_ref, q_ref, ckv_ref, krs_ref, nckv_ref, nkr_ref):
    i = pl.program_id(0)
    is_sample = i >= T_P // TM_IN
    seq = jnp.where(is_sample, DEC_SEQ, SEQ)
    mod = mod_ref[pl.ds(_mod_row(i, TM_IN), 1), :]
    shift1 = mod[:, 0:D_MODEL]
    scale1 = mod[:, D_MODEL:2 * D_MODEL]
    x = jnp.where(is_sample, xs_ref[...], xp_ref[...])
    h = ((x * _rms(x)) * n1_ref[...]) * (1.0 + scale1) + shift1
    hb = h.astype(BF16)

    n_lat = Q_LORA + KV_LORA
    w_small = jnp.concatenate([
        ws_ref[0:n_lat, :].astype(BF16),
        jnp.zeros((ROPE_LANE0, D_MODEL), BF16),
        ws_ref[n_lat:n_lat + QK_ROPE, :].astype(BF16),
        jnp.zeros((LANE - ROPE_LANE0 - QK_ROPE, D_MODEL), BF16)], axis=0)
    sm = _dot_nt(hb, w_small)
    cq = sm[:, 0:Q_LORA]
    ckv_raw = sm[:, Q_LORA:Q_LORA + KV_LORA]
    krs = sm[:, Q_LORA + KV_LORA:SMALL_COLS]
    cqn = (cq * _rms(cq)) * qn_ref[...]
    q = _dot(cqn.astype(BF16), wuq_ref[...])
    ckv = (ckv_raw * _rms(ckv_raw)) * kvn_ref[...]
    ckv_ref[...] = ckv

    @pl.when(jnp.logical_not(is_sample))
    def _():
        nckv_ref[...] = ckv
        for s in range(TM_IN // SEQ):
            kt = krs[s * SEQ:(s + 1) * SEQ, :].T
            nkr_ref[s] = kt[ROPE_LANE0:ROPE_LANE0 + QK_ROPE, :]

    cos = rope_ref[0]
    sin_lo = rope_ref[1]
    sin_hi = rope_ref[2]

    def rot(v):
        return v * cos + pltpu.roll(v, 8, 1) * sin_lo + pltpu.roll(v, LANE - 8, 1) * sin_hi

    krs_ref[...] = rot(krs)
    for hh in range(N_HEADS):
        q_ref[:, LANE * hh:LANE * (hh + 1)] = rot(q[:, LANE * hh:LANE * (hh + 1)]).astype(BF16)

    pos = lax.broadcasted_iota(jnp.int32, (TM_IN, 1), 0) & (seq - 1)
    first = pos == 0
    last = pos == seq - 1
    for j in range(D_CONV // CONV_CHUNK):
        c0 = j * CONV_CHUNK
        bg = _dot_nt(hb, wc_ref[c0:c0 + CONV_CHUNK, :].astype(BF16))
        cg = _dot_nt(hb, wc_ref[D_CONV + c0:D_CONV + c0 + CONV_CHUNK, :].astype(BF16))
        ui = _dot_nt(hb, wc_ref[2 * D_CONV + c0:2 * D_CONV + c0 + CONV_CHUNK, :].astype(BF16))
        u = cg * ui
        u_prev = jnp.where(first, 0.0, pltpu.roll(u, 1, 0))
        u_next = jnp.where(last, 0.0, pltpu.roll(u, TM_IN - 1, 0))
        cw = cw_ref[:, c0:c0 + CONV_CHUNK]
        conv = u_prev * cw[0:1] + u * cw[1:2] + u_next * cw[2:3]
        zc_ref[:, c0:c0 + CONV_CHUNK] = (bg * conv).astype(BF16)


def _inproj(xp, xs, mod, norm1, w_in_t, conv_w, q_norm, kv_norm, w_uq_slot, rope_tabs):
    n_prompt = T_P // TM_IN
    const = lambda i: (0, 0)
    pmap, smap = _stream_maps(TM_IN)
    once = pl.Buffered(1)
    return pl.pallas_call(
        _inproj_kernel,
        grid=(T // TM_IN,),
        in_specs=[
            pl.BlockSpec((TM_IN, D_MODEL), pmap),
            pl.BlockSpec((TM_IN, D_MODEL), smap),
            pl.BlockSpec((N_COND, 6 * D_MODEL), const),
            pl.BlockSpec((1, D_MODEL), const),
            pl.BlockSpec((O_CQ, D_MODEL), const, pipeline_mode=once),
            pl.BlockSpec((SMALL_BLK, D_MODEL), lambda i: (O_CQ // SMALL_BLK, 0), pipeline_mode=once),
            pl.BlockSpec((3, D_CONV), const),
            pl.BlockSpec((1, Q_LORA), const),
            pl.BlockSpec((1, KV_LORA), const),
            pl.BlockSpec((Q_LORA, N_HEADS * LANE), const),
            pl.BlockSpec((None, 3, TM_IN, LANE),
                         lambda i: (jnp.where(i >= n_prompt, 1, 0), 0, 0, 0)),
        ],
        out_specs=[
            pl.BlockSpec((TM_IN, D_CONV), lambda i: (i, 0)),
            pl.BlockSpec((TM_IN, N_HEADS * LANE), lambda i: (i, 0)),
            pl.BlockSpec((TM_IN, KV_LORA), lambda i: (i, 0)),
            pl.BlockSpec((TM_IN, LANE), lambda i: (i, 0)),
            pl.BlockSpec((TM_IN, KV_LORA), pmap),
            pl.BlockSpec((TM_IN // SEQ, QK_ROPE, SEQ), lambda i: (jnp.minimum(i, n_prompt - 1), 0, 0)),
        ],
        out_shape=[
            jax.ShapeDtypeStruct((T, D_CONV), BF16),
            jax.ShapeDtypeStruct((T, N_HEADS * LANE), BF16),
            jax.ShapeDtypeStruct((T, KV_LORA), F32),
            jax.ShapeDtypeStruct((T, LANE), F32),
            jax.ShapeDtypeStruct((T_P, KV_LORA), F32),
            jax.ShapeDtypeStruct((BATCH, QK_ROPE, SEQ), F32),
        ],
        compiler_params=pltpu.CompilerParams(
            dimension_semantics=("arbitrary",), vmem_limit_bytes=VMEM_LIMIT),
        name="in_proj",
    )(xp, xs, mod, norm1, w_in_t, w_in_t, conv_w, q_norm, kv_norm, w_uq_slot, rope_tabs)


def _fill_kv(ckv, krs, wukv_ref, kf_scr, v_scr, off):
    m = ckv.shape[0]
    kv = _dot(ckv.astype(BF16), wukv_ref[...])
    for hh in range(N_HEADS):
        kf_scr[hh, off:off + m, :] = (kv[:, LANE * hh:LANE * (hh + 1)] + krs).astype(BF16)
    v_scr[off:off + m, :] = kv[:, N_HEADS * LANE:].astype(BF16)


def _attend(q_ref, r0, rows, kf_scr, v_scr, o_ref):
    for pair in range(N_HEADS // 2):
        acc = None
        for hh in (2 * pair, 2 * pair + 1):
            qh = q_ref[r0:r0 + rows, LANE * hh:LANE * (hh + 1)]
            s = _dot_nt(qh, kf_scr[hh]) * ATTN_SCALE
            e = jnp.exp(s - jnp.max(s, axis=-1, keepdims=True))
            p = (e / jnp.sum(e, axis=-1, keepdims=True)).astype(BF16)
            part = _dot(p, v_scr[:, LANE * hh:LANE * (hh + 1)])
            acc = part if acc is None else acc + part
        o_ref[r0:r0 + rows, LANE * pair:LANE * (pair + 1)] = acc.astype(BF16)


def _attn_prompt_kernel(q_ref, ckv_ref, krs_ref, wukv_ref, o_ref):
    for s in range(PROMPT_SEQS):
        r0 = s * SEQ
        kv = _dot(ckv_ref[r0:r0 + SEQ, :].astype(BF16), wukv_ref[...])
        krs = krs_ref[r0:r0 + SEQ, :]
        for pair in range(N_HEADS // 2):
            acc = None
            for hh in (2 * pair, 2 * pair + 1):
                kf = (kv[:, LANE * hh:LANE * (hh + 1)] + krs).astype(BF16)
                vh = kv[:, LANE * (N_HEADS + hh):LANE * (N_HEADS + hh + 1)].astype(BF16)
                sc = _dot_nt(q_ref[r0:r0 + SEQ, LANE * hh:LANE * (hh + 1)], kf) * ATTN_SCALE
                e = jnp.exp(sc - jnp.max(sc, axis=-1, keepdims=True))
                p = (e / jnp.sum(e, axis=-1, keepdims=True)).astype(BF16)
                part = _dot(p, vh)
                acc = part if acc is None else acc + part
            o_ref[r0:r0 + SEQ, LANE * pair:LANE * (pair + 1)] = acc.astype(BF16)


def _attn_sample_kernel(q_ref, ckv_ref, cckv_ref, krs_ref, ckrs_ref, wukv_ref, wg_ref, wco_ref, wo_ref,
                        wmix_ref, o_ref, g_ref, co_ref, ob_ref, mix_ref, kf_scr, v_scr):
    @pl.when(pl.program_id(1) == 0)
    def _():
        _fill_kv(ckv_ref[...], krs_ref[...], wukv_ref, kf_scr, v_scr, 0)
        _fill_kv(cckv_ref[...], ckrs_ref[...], wukv_ref, kf_scr, v_scr, DEC_SEQ)

    _attend(q_ref, 0, Q_BLK_S, kf_scr, v_scr, o_ref)
    g_ref[...] = wg_ref[...].astype(BF16)
    co_ref[...] = wco_ref[...].astype(BF16)
    ob_ref[...] = wo_ref[...].astype(BF16)
    mix_ref[...] = wmix_ref[...].astype(BF16)


def _attention(q, ckv, krs, cache_ckv, cache_krs, w_ukv_slot, w_in_t, w_conv_out, w_o, w_mix_out):
    kv_cols = 2 * N_HEADS * LANE
    n_o = N_HEADS * V_HEAD
    steps = BATCH // PROMPT_SEQS
    rows = PROMPT_SEQS * SEQ
    o_prompt = pl.pallas_call(
        _attn_prompt_kernel,
        grid=(steps,),
        in_specs=[
            pl.BlockSpec((rows, N_HEADS * LANE), lambda b: (b, 0)),
            pl.BlockSpec((rows, KV_LORA), lambda b: (b, 0)),
            pl.BlockSpec((rows, LANE), lambda b: (b, 0)),
            pl.BlockSpec((KV_LORA, kv_cols), lambda b: (0, 0)),
        ],
        out_specs=pl.BlockSpec((rows, n_o), lambda b: (b, 0)),
        out_shape=jax.ShapeDtypeStruct((T_P, n_o), BF16),
        compiler_params=pltpu.CompilerParams(dimension_semantics=("parallel",),
                                             vmem_limit_bytes=VMEM_LIMIT),
        name="attn_prompt",
    )(q, ckv, krs, w_ukv_slot)

    m_all = DEC_SEQ + PAST_LEN
    nq = DEC_SEQ // Q_BLK_S
    q0 = T_P // Q_BLK_S
    s0 = T_P // DEC_SEQ
    n_step = DEC_BATCH * nq
    share = lambda n: pl.BlockSpec((n // n_step, D_MODEL), lambda b, j: (b * nq + j, 0))
    gate_rows = 2 * D_MODEL // n_step
    o_sample, w_gates_t, w_co_b, w_o_b, w_mix_b = pl.pallas_call(
        _attn_sample_kernel,
        grid=(DEC_BATCH, nq),
        in_specs=[
            pl.BlockSpec((Q_BLK_S, N_HEADS * LANE), lambda b, j: (q0 + b * nq + j, 0)),
            pl.BlockSpec((DEC_SEQ, KV_LORA), lambda b, j: (s0 + b, 0)),
            pl.BlockSpec((None, PAST_LEN, KV_LORA), lambda b, j: (b, 0, 0)),
            pl.BlockSpec((DEC_SEQ, LANE), lambda b, j: (s0 + b, 0)),
            pl.BlockSpec((None, PAST_LEN, LANE), lambda b, j: (b, 0, 0)),
            pl.BlockSpec((KV_LORA, kv_cols), lambda b, j: (0, 0)),
            pl.BlockSpec((pl.Element(gate_rows), pl.Element(D_MODEL)),
                         lambda b, j: (pl.multiple_of(O_GATE + (b * nq + j) * gate_rows, 32), 0)),
            share(D_CONV), share(n_o), share(D_MODEL),
        ],
        out_specs=[pl.BlockSpec((Q_BLK_S, n_o), lambda b, j: (b * nq + j, 0)),
                   share(2 * D_MODEL), share(D_CONV), share(n_o), share(D_MODEL)],
        out_shape=[jax.ShapeDtypeStruct((T_S, n_o), BF16),
                   jax.ShapeDtypeStruct((2 * D_MODEL, D_MODEL), BF16),
                   jax.ShapeDtypeStruct((D_CONV, D_MODEL), BF16),
                   jax.ShapeDtypeStruct((n_o, D_MODEL), BF16),
                   jax.ShapeDtypeStruct((D_MODEL, D_MODEL), BF16)],
        scratch_shapes=[pltpu.VMEM((N_HEADS, m_all, LANE), BF16),
                        pltpu.VMEM((m_all, N_HEADS * LANE), BF16)],
        compiler_params=pltpu.CompilerParams(dimension_semantics=("arbitrary", "arbitrary"),
                                             vmem_limit_bytes=VMEM_LIMIT),
        name="attn_sample",
    )(q, ckv, cache_ckv, krs, cache_krs, w_ukv_slot, w_in_t, w_conv_out, w_o, w_mix_out)
    return o_prompt, o_sample, w_gates_t, w_co_b, w_o_b, w_mix_b


def _route(logits):
    lane = lax.broadcasted_iota(jnp.int32, logits.shape, 1)
    neg = -jnp.inf
    big = jnp.int32(1 << 20)
    gmask = (lane >= N_EXPERTS) & (lane < N_EXPERTS + N_GROUPS)
    gl = jnp.where(gmask, logits, neg)
    gmax = jnp.max(gl, axis=-1, keepdims=True)
    gsum = jnp.sum(jnp.where(gmask, jnp.exp(gl - gmax), 0.0), axis=-1, keepdims=True)
    p_g = 1.0 / gsum
    g_idx = jnp.min(jnp.where(gl == gmax, lane, big), axis=-1, keepdims=True) - N_EXPERTS

    emask = (lane < N_EXPERTS) & ((lane >> 3) == g_idx)
    el = jnp.where(emask, logits, neg)
    m1 = jnp.max(el, axis=-1, keepdims=True)
    i1 = jnp.min(jnp.where(el == m1, lane, big), axis=-1, keepdims=True)
    el2 = jnp.where(lane == i1, neg, el)
    m2 = jnp.max(el2, axis=-1, keepdims=True)
    i2 = jnp.min(jnp.where(el2 == m2, lane, big), axis=-1, keepdims=True)
    z = jnp.sum(jnp.where(emask, jnp.exp(el - m1), 0.0), axis=-1, keepdims=True)
    p1 = 1.0 / z
    p2 = jnp.exp(m2 - m1) / z
    tot = p1 + p2
    w1 = p_g * p1 / tot
    w2 = p_g * p2 / tot
    return jnp.where(lane == i1, w1, 0.0) + jnp.where(lane == i2, w2, 0.0), g_idx


def _post_kernel(xp_ref, xs_ref, mod_ref, n1_ref, wg_ref, zc_ref, op_ref, os_ref, wco_ref, wo_ref,
                 wmix_ref, n2_ref, wr_ref, x1_ref, h3_ref, meta_ref, cnt_ref, gr_ref):
    i = pl.program_id(0)
    is_sample = i >= T_P // TM_POST
    x = jnp.where(is_sample, xs_ref[...], xp_ref[...])
    o = jnp.where(is_sample, os_ref[...], op_ref[...])
    mod = mod_ref[pl.ds(_mod_row(i, TM_POST), 1), :]
    shift1 = mod[:, 0:D_MODEL]
    scale1 = mod[:, D_MODEL:2 * D_MODEL]
    gate1 = mod[:, 2 * D_MODEL:3 * D_MODEL]
    shift2 = mod[:, 3 * D_MODEL:4 * D_MODEL]
    scale2 = mod[:, 4 * D_MODEL:5 * D_MODEL]
    y_conv = _dot(zc_ref[...], wco_ref[...])
    y_mla = _dot(o, wo_ref[...])
    h = ((x * _rms(x)) * n1_ref[...]) * (1.0 + scale1) + shift1
    g = _dot_nt(h.astype(BF16), wg_ref[...])
    merged = (jax.nn.sigmoid(g[:, 0:D_MODEL]) * y_conv
              + jax.nn.sigmoid(g[:, D_MODEL:2 * D_MODEL]) * y_mla)
    y = _dot(merged.astype(BF16), wmix_ref[...])
    x1 = x + gate1 * y
    x1_ref[...] = x1
    h2 = ((x1 * _rms(x1)) * n2_ref[...]) * (1.0 + scale2) + shift2
    h2_hi = h2.astype(BF16)
    h2_lo = (h2 - h2_hi.astype(F32)).astype(BF16)
    hh = _dot(h2_hi, wr_ref[...])
    logits = hh[:, 0:LANE] + hh[:, LANE:2 * LANE] + _dot(h2_lo, wr_ref[:, 0:LANE])
    comb, g_idx = _route(logits)

    lane = lax.broadcasted_iota(jnp.int32, comb.shape, 1)
    onehot = lane == g_idx + N_EXPERTS
    r_i = lax.broadcasted_iota(jnp.int32, (TM_POST, TM_POST), 0)
    c_i = lax.broadcasted_iota(jnp.int32, (TM_POST, TM_POST), 1)
    lower = jnp.where(c_i < r_i, 1.0, 0.0).astype(BF16)
    before = _dot(lower, jnp.where(onehot, 1.0, 0.0).astype(BF16))
    rank = jnp.sum(jnp.where(onehot, before, 0.0), axis=-1, keepdims=True)
    counts = jnp.sum(jnp.where(onehot, 1.0, 0.0), axis=0, keepdims=True)
    cnt_ref[...] = jnp.broadcast_to(counts, cnt_ref.shape)

    meta_ref[...] = comb
    idx = jnp.where(lane == GID_LANE, g_idx.astype(F32), 0.0) + jnp.where(lane == RANK_LANE, rank, 0.0)
    idx_hi = idx.astype(BF16)
    idx_lo = (idx - idx_hi.astype(F32)).astype(BF16)
    s_row = lax.broadcasted_iota(jnp.int32, (8, LANE), 0)
    s_lane = lax.broadcasted_iota(jnp.int32, (8, LANE), 1)
    sel = jnp.where(s_lane == GID_LANE + s_row, 1.0, 0.0).astype(BF16)
    gr_ref[...] = _dot_nt(sel, idx_hi) + _dot_nt(sel, idx_lo)
    for c in range(N_SLAB):
        h3_ref[pl.ds(c, TM_POST, stride=N_SLAB), :] = h2[:, LANE * c:LANE * (c + 1)]


def _post(xp, xs, mod, norm1, w_gates_t, zc, o_p, o_s, w_conv_out, w_o, w_mix_out, norm2, w_route):
    const = lambda i: (0, 0)
    row = lambda i: (i, 0)
    pmap, smap = _stream_maps(TM_POST)
    return pl.pallas_call(
        _post_kernel,
        grid=(T // TM_POST,),
        in_specs=[
            pl.BlockSpec((TM_POST, D_MODEL), pmap),
            pl.BlockSpec((TM_POST, D_MODEL), smap),
            pl.BlockSpec((N_COND, 6 * D_MODEL), const),
            pl.BlockSpec((1, D_MODEL), const),
            pl.BlockSpec((2 * D_MODEL, D_MODEL), const),
            pl.BlockSpec((TM_POST, D_CONV), row),
            pl.BlockSpec((TM_POST, N_HEADS * V_HEAD), pmap),
            pl.BlockSpec((TM_POST, N_HEADS * V_HEAD), smap),
            pl.BlockSpec((D_CONV, D_MODEL), const),
            pl.BlockSpec((N_HEADS * V_HEAD, D_MODEL), const),
            pl.BlockSpec((D_MODEL, D_MODEL), const),
            pl.BlockSpec((1, D_MODEL), const),
            pl.BlockSpec((D_MODEL, 2 * LANE), const),
        ],
        out_specs=[
            pl.BlockSpec((TM_POST, D_MODEL), row),
            pl.BlockSpec((TM_POST * N_SLAB, LANE), row),
            pl.BlockSpec((TM_POST, LANE), row),
            pl.BlockSpec((None, 8, LANE), lambda i: (i, 0, 0)),
            pl.BlockSpec((8, TM_POST), lambda i: (0, i)),
        ],
        out_shape=[
            jax.ShapeDtypeStruct((T, D_MODEL), F32),
            jax.ShapeDtypeStruct((T * N_SLAB, LANE), F32),
            jax.ShapeDtypeStruct((T, LANE), F32),
            jax.ShapeDtypeStruct((T // TM_POST, 8, LANE), F32),
            jax.ShapeDtypeStruct((8, T), F32),
        ],
        compiler_params=pltpu.CompilerParams(
            dimension_semantics=("parallel",), vmem_limit_bytes=VMEM_LIMIT),
        name="post_mixer",
    )(xp, xs, mod, norm1, w_gates_t, zc, o_p, o_s, w_conv_out, w_o, w_mix_out, norm2, w_route)


DISP_IN = T // TM_DISP
DISP_OUT = T // TM_MOE


def _dispatch_kernel(pos_ref, h3_ref, m_ref, hs_ref, ms_ref, xs_ref, mss_ref):
    i = pl.program_id(0)

    @pl.when(i < DISP_IN)
    def _():
        def body(r, carry):
            p = pos_ref[0, r]
            xs_ref[_slab(p), :] = h3_ref[_slab(r), :]
            mss_ref[pl.ds(p, 1), :] = m_ref[pl.ds(r, 1), :]
            return carry

        lax.fori_loop(0, TM_DISP, body, 0, unroll=8)

    @pl.when(i >= DISP_IN)
    def _():
        row0 = pl.multiple_of((i - DISP_IN) * TM_MOE, TM_MOE)
        for c in range(N_SLAB):
            hs_ref[:, LANE * c:LANE * (c + 1)] = (
                xs_ref[pl.ds(row0 * N_SLAB + c, TM_MOE, stride=N_SLAB), :].astype(BF16))
        ms_ref[...] = mss_ref[pl.ds(row0, TM_MOE), :]


def _dispatch(pos, h3, meta):
    n_slab = N_SLAB
    in_map = lambda i: (jnp.minimum(i, DISP_IN - 1), 0)
    out_map = lambda i: (jnp.maximum(i - DISP_IN, 0), 0)
    return pl.pallas_call(
        _dispatch_kernel,
        grid=(DISP_IN + DISP_OUT,),
        in_specs=[pl.BlockSpec((None, 1, TM_DISP), lambda i: (jnp.minimum(i, DISP_IN - 1), 0, 0),
                               memory_space=pltpu.SMEM),
                  pl.BlockSpec((TM_DISP * n_slab, LANE), in_map),
                  pl.BlockSpec((TM_DISP, LANE), in_map)],
        out_specs=[pl.BlockSpec((TM_MOE, D_MODEL), out_map),
                   pl.BlockSpec((TM_MOE, LANE), out_map)],
        scratch_shapes=[pltpu.VMEM((T * n_slab, LANE), F32),
                        pltpu.VMEM((T, LANE), F32)],
        out_shape=[jax.ShapeDtypeStruct((T, D_MODEL), BF16),
                   jax.ShapeDtypeStruct((T, LANE), F32)],
        compiler_params=pltpu.CompilerParams(
            dimension_semantics=("arbitrary",), vmem_limit_bytes=VMEM_LIMIT),
        name="moe_dispatch",
    )(pos.reshape(DISP_IN, 1, TM_DISP), h3, meta)


def _moe_kernel(sched_ref, hs_ref, ms_ref, wup_ref, wgate_ref, wdown_ref, y3_ref, acc_ref):
    v = pl.program_id(0)
    j = pl.program_id(1)
    valid = sched_ref[V_VALID, v] == 1
    lo = sched_ref[V_LO, v]
    hi = sched_ref[V_HI, v]
    e0 = sched_ref[V_GROUP, v] * EXP_PER_GROUP + j * MOE_EPS
    full = (hi - lo) * 4 >= TM_MOE * 3

    @pl.when(valid & (j == 0) & (sched_ref[V_FIRST, v] == 1))
    def _():
        acc_ref[...] = jnp.zeros_like(acc_ref)

    def expert_rows(r0, rows):
        w_in2 = jnp.concatenate(
            [w[k].astype(BF16) for k in range(MOE_EPS) for w in (wup_ref, wgate_ref)], axis=1)
        ag = _dot(hs_ref[pl.ds(r0, rows), :], w_in2)
        comb = ms_ref[pl.ds(r0, rows), :]
        lane = lax.broadcasted_iota(jnp.int32, comb.shape, 1)
        acts = []
        for k in range(MOE_EPS):
            a = ag[:, 2 * k * D_EXPERT:(2 * k + 1) * D_EXPERT]
            g = ag[:, (2 * k + 1) * D_EXPERT:(2 * k + 2) * D_EXPERT]
            cw = jnp.sum(jnp.where(lane == e0 + k, comb, 0.0), axis=-1, keepdims=True)
            acts.append(((g * jax.nn.sigmoid(g)) * a * cw).astype(BF16))
        w_out = jnp.concatenate([wdown_ref[k].astype(BF16) for k in range(MOE_EPS)], axis=0)
        acc_ref[pl.ds(r0, rows), :] += _dot(jnp.concatenate(acts, axis=1), w_out)

    @pl.when(valid & full)
    def _():
        expert_rows(0, TM_MOE)

    @pl.when(valid & jnp.logical_not(full))
    def _():
        def sub_block(s, carry):
            expert_rows(pl.multiple_of(s * MOE_SUB, MOE_SUB), MOE_SUB)
            return carry

        lax.fori_loop(lo // MOE_SUB, (hi + MOE_SUB - 1) // MOE_SUB, sub_block, 0)

    @pl.when(valid & (j == EXP_PER_GROUP // MOE_EPS - 1) & (sched_ref[V_LAST, v] == 1))
    def _():
        for c in range(N_SLAB):
            y3_ref[pl.ds(c, TM_MOE, stride=N_SLAB), :] = acc_ref[:, LANE * c:LANE * (c + 1)]


def _moe(sched, hs, ms, w_up, w_gate, w_down):
    steps = EXP_PER_GROUP // MOE_EPS
    wmap = lambda v, j, sched: (
        sched[V_GROUP, v] * steps + jnp.where(sched[V_VALID, v] == 1, j, steps - 1), 0, 0)
    tmap = lambda v, j, sched: (sched[V_TILE, v], 0)
    n_slab = D_MODEL // LANE
    return pl.pallas_call(
        _moe_kernel,
        grid_spec=pltpu.PrefetchScalarGridSpec(
            num_scalar_prefetch=1,
            grid=(N_VISITS, steps),
            in_specs=[
                pl.BlockSpec((TM_MOE, D_MODEL), tmap),
                pl.BlockSpec((TM_MOE, LANE), tmap),
                pl.BlockSpec((MOE_EPS, D_MODEL, D_EXPERT), wmap),
                pl.BlockSpec((MOE_EPS, D_MODEL, D_EXPERT), wmap),
                pl.BlockSpec((MOE_EPS, D_EXPERT, D_MODEL), wmap),
            ],
            out_specs=pl.BlockSpec((TM_MOE * n_slab, LANE), tmap),
            scratch_shapes=[pltpu.VMEM((TM_MOE, D_MODEL), F32)],
        ),
        out_shape=jax.ShapeDtypeStruct((T * n_slab, LANE), F32),
        compiler_params=pltpu.CompilerParams(
            dimension_semantics=("arbitrary", "arbitrary"), vmem_limit_bytes=VMEM_LIMIT),
        name="moe_grouped",
    )(sched, hs, ms, w_up, w_gate, w_down)


def _final_kernel(pos_ref, ys_ref, x1_ref, mod_ref, fn_ref, yp_ref, ysm_ref, g_ref):
    i = pl.program_id(0)

    def body(r, carry):
        g_ref[_slab(r), :] = ys_ref[_slab(pos_ref[0, r]), :]
        return carry

    lax.fori_loop(0, TM_FINAL, body, 0, unroll=8)
    mod = mod_ref[pl.ds(_mod_row(i, TM_FINAL), 1), :]
    gate2 = mod[:, 5 * D_MODEL:6 * D_MODEL]
    moe = jnp.concatenate([g_ref[pl.ds(c, TM_FINAL, stride=N_SLAB), :] for c in range(N_SLAB)], axis=1)
    x2 = x1_ref[...] + gate2 * moe
    y = (x2 * _rms(x2)) * fn_ref[...]
    is_sample = i >= T_P // TM_FINAL

    @pl.when(jnp.logical_not(is_sample))
    def _():
        yp_ref[...] = y

    @pl.when(is_sample)
    def _():
        ysm_ref[...] = y


def _final(pos, ys, x1, mod, final_norm):
    n_slab = D_MODEL // LANE
    pmap, smap = _stream_maps(TM_FINAL)
    return pl.pallas_call(
        _final_kernel,
        grid=(T // TM_FINAL,),
        in_specs=[
            pl.BlockSpec((None, 1, TM_FINAL), lambda i: (i, 0, 0), memory_space=pltpu.SMEM),
            pl.BlockSpec((T * n_slab, LANE), lambda i: (0, 0), pipeline_mode=pl.Buffered(1)),
            pl.BlockSpec((TM_FINAL, D_MODEL), lambda i: (i, 0)),
            pl.BlockSpec((N_COND, 6 * D_MODEL), lambda i: (0, 0)),
            pl.BlockSpec((1, D_MODEL), lambda i: (0, 0)),
        ],
        out_specs=[pl.BlockSpec((TM_FINAL, D_MODEL), pmap),
                   pl.BlockSpec((TM_FINAL, D_MODEL), smap)],
        scratch_shapes=[pltpu.VMEM((TM_FINAL * n_slab, LANE), F32)],
        out_shape=[jax.ShapeDtypeStruct((T_P, D_MODEL), F32),
                   jax.ShapeDtypeStruct((T_S, D_MODEL), F32)],
        compiler_params=pltpu.CompilerParams(
            dimension_semantics=("arbitrary",), vmem_limit_bytes=VMEM_LIMIT),
        name="moe_unsort_final",
    )(pos.reshape(T // TM_FINAL, 1, TM_FINAL), ys, x1, mod, final_norm)


def _plan_kernel(gr_ref, cnt_ref, pos_ref, sched_ref):
    n_post = T // TM_POST
    lane = lax.broadcasted_iota(jnp.int32, (1, LANE), 1)
    grp_lane = lambda v, g: v[:, N_EXPERTS + g:N_EXPERTS + g + 1]
    counts = [cnt_ref[k, 0:1, :] for k in range(n_post)]
    gtot_v = counts[0]
    for k in range(1, n_post):
        gtot_v = gtot_v + counts[k]
    gtot = [grp_lane(gtot_v, g) for g in range(N_GROUPS)]
    goff = [jnp.zeros((1, 1), F32)]
    for g in range(1, N_GROUPS):
        goff.append(goff[-1] + gtot[g - 1])

    before = [jnp.zeros((1, 1), F32) for _ in range(N_GROUPS)]
    rows_per_tile = TM_POST // LANE
    for k in range(n_post):
        gid = gr_ref[0:1, TM_POST * k:TM_POST * (k + 1)]
        pos = gr_ref[1:2, TM_POST * k:TM_POST * (k + 1)]
        for g in range(N_GROUPS):
            pos = pos + jnp.where(gid == float(g), goff[g] + before[g], 0.0)
            before[g] = before[g] + grp_lane(counts[k], g)
        for r in range(rows_per_tile):
            pos_ref[rows_per_tile * k + r:rows_per_tile * k + r + 1, :] = (
                pos[:, LANE * r:LANE * (r + 1)].astype(jnp.int32))

    zero = jnp.zeros((1, LANE), F32)
    rows = {name: zero for name in ("vt", "vg", "vlo", "vhi")}
    slot = jnp.zeros((1, 1), F32)
    last_t = jnp.zeros((1, 1), F32)
    last_g = jnp.zeros((1, 1), F32)
    for i in range(T // TM_MOE):
        for g in range(N_GROUPS):
            lo = jnp.clip(goff[g] - float(TM_MOE * i), 0.0, float(TM_MOE))
            hi = jnp.clip(goff[g] + gtot[g] - float(TM_MOE * i), 0.0, float(TM_MOE))
            ok = hi > lo
            here = ok & (lane == slot.astype(jnp.int32))
            rows["vt"] = jnp.where(here, float(i), rows["vt"])
            rows["vg"] = jnp.where(here, float(g), rows["vg"])
            rows["vlo"] = jnp.where(here, lo, rows["vlo"])
            rows["vhi"] = jnp.where(here, hi, rows["vhi"])
            last_t = jnp.where(ok, float(i), last_t)
            last_g = jnp.where(ok, float(g), last_g)
            slot = slot + jnp.where(ok, 1.0, 0.0)
    n_ok = slot.astype(jnp.int32)
    valid = lane < n_ok
    vt = jnp.where(valid, rows["vt"], last_t)
    vg = jnp.where(valid, rows["vg"], last_g)
    prev_t = pltpu.roll(jnp.broadcast_to(vt, (8, LANE)), 1, 1)[0:1, :]
    next_t = pltpu.roll(jnp.broadcast_to(vt, (8, LANE)), LANE - 1, 1)[0:1, :]
    first = jnp.where((lane == 0) | (vt != prev_t), 1.0, 0.0)
    last = jnp.where((lane == n_ok - 1) | (vt != next_t), 1.0, 0.0)
    table = [vt, vg, rows["vlo"], rows["vhi"], first, last, jnp.where(valid, 1.0, 0.0), zero]
    for r, row in enumerate(table):
        sched_ref[r:r + 1, :] = row.astype(jnp.int32)


V_TILE, V_GROUP, V_LO, V_HI, V_FIRST, V_LAST, V_VALID = range(7)


def _plan(gr, cnt):
    pos, sched = pl.pallas_call(
        _plan_kernel,
        out_shape=[jax.ShapeDtypeStruct((T // LANE, LANE), jnp.int32),
                   jax.ShapeDtypeStruct((8, LANE), jnp.int32)],
        name="moe_plan",
    )(gr, cnt)
    return pos.reshape(T), sched


def _rope_tables():
    n = np.arange(DEC_SEQ)
    pos = np.stack([n // GRID_W, n % GRID_W], axis=1).astype(np.float32)
    half = ROPE_AXIS // 2
    inv = (1.0 / (ROPE_BASE ** (np.arange(0, ROPE_AXIS, 2, dtype=np.float32) / ROPE_AXIS))).astype(np.float32)
    ang = (pos[:, :, None] * inv[None, None, :]).astype(np.float32)
    cos = np.cos(ang).astype(np.float32)
    sin = np.sin(ang).astype(np.float32)
    tabs = np.zeros((2, 3, DEC_SEQ, LANE), np.float32)
    tabs[:, 0] = 1.0
    for a in range(2):
        lo = ROPE_LANE0 + a * ROPE_AXIS
        tabs[1, 0, :, lo:lo + half] = cos[:, a]
        tabs[1, 0, :, lo + half:lo + 2 * half] = cos[:, a]
        tabs[1, 1, :, lo + half:lo + 2 * half] = sin[:, a]
        tabs[1, 2, :, lo:lo + half] = -sin[:, a]
    return jnp.asarray(tabs)


def kernel(x_prompt, x_sample, cache_ckv, cache_krope, c, c_ctx, norm1, w_ada, b_ada, w_in, conv_w,
           w_conv_out, q_norm, w_uq, kv_norm, w_ukv, w_o, w_mix_out, norm2, w_grp, w_exp, w_up,
           w_gate, w_down, final_norm):
    l = 0
    xp = x_prompt.reshape(T_P, D_MODEL)
    xs = x_sample.reshape(T_S, D_MODEL)
    mod = _ada(c_ctx[None, :], c, w_ada[l], b_ada[l][None, :])

    w_in_t = w_in[l].T
    w_uq_slot = jnp.pad(w_uq[l].reshape(Q_LORA, N_HEADS, QK_NOPE + QK_ROPE),
                        ((0, 0), (0, 0), (0, LANE - QK_NOPE - QK_ROPE))
                        ).reshape(Q_LORA, N_HEADS * LANE).astype(BF16)
    wkv = w_ukv[l].reshape(KV_LORA, N_HEADS, QK_NOPE + V_HEAD)
    wk_slot = jnp.pad(wkv[:, :, :QK_NOPE], ((0, 0), (0, 0), (0, LANE - QK_NOPE)))
    wv = wkv[:, :, QK_NOPE:].reshape(KV_LORA, N_HEADS // 2, 2, V_HEAD)
    zero = jnp.zeros_like(wv[:, :, 0])
    wv_slot = jnp.stack([jnp.concatenate([wv[:, :, 0], zero], axis=-1),
                         jnp.concatenate([zero, wv[:, :, 1]], axis=-1)], axis=2)
    w_ukv_slot = jnp.concatenate([wk_slot.reshape(KV_LORA, N_HEADS * LANE),
                                  wv_slot.reshape(KV_LORA, N_HEADS * LANE)], axis=1).astype(BF16)
    w_route = jnp.pad(jnp.concatenate([w_exp[l], w_grp[l]], axis=1),
                      ((0, 0), (0, LANE - N_EXPERTS - N_GROUPS)))
    w_route_hi = w_route.astype(BF16)
    w_route_lo = (w_route - w_route_hi.astype(F32)).astype(BF16)
    w_route2 = jnp.concatenate([w_route_hi, w_route_lo], axis=1)
    cache_krs = jnp.pad(cache_krope[:, l], ((0, 0), (0, 0), (ROPE_LANE0, LANE - ROPE_LANE0 - QK_ROPE)))

    zc, q, ckv, krs, nckv, nkr = _inproj(xp, xs, mod, norm1[l][None, :], w_in_t, conv_w[l],
                              q_norm[l][None, :], kv_norm[l][None, :], w_uq_slot, _rope_tables())
    o_p, o_s, w_gates_t, w_co_b, w_o_b, w_mix_b = _attention(
        q, ckv, krs, cache_ckv[:, l], cache_krs, w_ukv_slot, w_in_t, w_conv_out[l], w_o[l], w_mix_out[l])
    x1, h3, meta, cnt, gr = _post(xp, xs, mod, norm1[l][None, :], w_gates_t, zc, o_p, o_s, w_co_b, w_o_b,
                                  w_mix_b, norm2[l][None, :], w_route2)
    pos, sched = _plan(gr, cnt)
    hs, ms = _dispatch(pos, h3, meta)
    ys = _moe(sched, hs, ms, w_up[l], w_gate[l], w_down[l])
    yp, ysm = _final(pos, ys, x1, mod, final_norm[None, :])

    y_prompt = yp.reshape(BATCH, SEQ, D_MODEL)
    y_sample = ysm.reshape(DEC_BATCH, DEC_SEQ, D_MODEL)
    new_ckv = nckv.reshape(BATCH, 1, SEQ, KV_LORA)
    new_krope = jnp.swapaxes(nkr, 1, 2).reshape(BATCH, 1, SEQ, QK_ROPE)
    return (y_prompt, y_sample, new_ckv, new_krope)
```

```python
import numpy as np
import jax
import jax.numpy as jnp
from jax import lax
from jax.experimental import pallas as pl
from jax.experimental.pallas import tpu as pltpu

F32 = jnp.float32
BF16 = jnp.bfloat16

D_MODEL = 1024
BATCH = 16
SEQ = 256
DEC_BATCH = 2
DEC_SEQ = 1024
PAST_LEN = 256
GRID_W = 64
N_HEADS = 8
QK_NOPE = 64
QK_ROPE = 32
V_HEAD = 64
Q_LORA = 256
KV_LORA = 128
ROPE_AXIS = QK_ROPE // 2
ROPE_BASE = 10000.0
ATTN_SCALE = (QK_NOPE + QK_ROPE) ** -0.5
D_CONV = D_MODEL
N_GROUPS = 4
EXP_PER_GROUP = 8
N_EXPERTS = N_GROUPS * EXP_PER_GROUP
D_EXPERT = 256
EPS = 1e-6

T_P = BATCH * SEQ
T_S = DEC_BATCH * DEC_SEQ
T = T_P + T_S
N_COND = 8
LANE = 128
ROPE_LANE0 = QK_NOPE
SMALL_COLS = Q_LORA + KV_LORA + LANE
VMEM_LIMIT = 56 * 1024 * 1024

TM_IN = 1024
TM_POST = 512
TM_MOE = T // N_GROUPS
TM_FINAL = 512
TM_DISP = 1024
N_SLAB = D_MODEL // LANE
MOE_SUB = 256
MOE_EPS = 2
N_VISITS = T // TM_MOE + N_GROUPS - 1
GID_LANE = 40
RANK_LANE = 41
Q_BLK_S = 512
PROMPT_SEQS = 4
CONV_CHUNK = 256


def _dot(a, b):
    return jnp.dot(a, b, preferred_element_type=F32)


def _rms(x):
    return lax.rsqrt(jnp.mean(x * x, axis=-1, keepdims=True) + EPS)


def _slab(t):
    return pl.ds(pl.multiple_of(t * N_SLAB, N_SLAB), N_SLAB)


def _mod_row(i, tm):
    n_prompt = T_P // tm
    return jnp.where(i >= n_prompt, 1 + ((i - n_prompt) * tm) // DEC_SEQ, 0)


def _ada_kernel(cctx_ref, c_ref, w_ref, b_ref, o_ref):
    c = jnp.concatenate([cctx_ref[...], c_ref[...],
                         jnp.zeros((N_COND - 1 - DEC_BATCH, D_MODEL), F32)], axis=0)
    a = (c * jax.nn.sigmoid(c)).astype(BF16)
    o_ref[...] = _dot(a, w_ref[...].astype(BF16)) + b_ref[...]


def _ada(c_ctx, c, w_ada, b_ada):
    n = 6 * D_MODEL
    bn = 1536
    return pl.pallas_call(
        _ada_kernel,
        grid=(n // bn,),
        in_specs=[
            pl.BlockSpec((1, D_MODEL), lambda j: (0, 0)),
            pl.BlockSpec((DEC_BATCH, D_MODEL), lambda j: (0, 0)),
            pl.BlockSpec((D_MODEL, bn), lambda j: (0, j)),
            pl.BlockSpec((1, bn), lambda j: (0, j)),
        ],
        out_specs=pl.BlockSpec((N_COND, bn), lambda j: (0, j)),
        out_shape=jax.ShapeDtypeStruct((N_COND, n), F32),
        compiler_params=pltpu.CompilerParams(
            dimension_semantics=("parallel",), vmem_limit_bytes=VMEM_LIMIT),
        name="ada_mod",
    )(c_ctx, c, w_ada, b_ada)


O_CQ = 3 * D_CONV
O_KR = O_CQ + Q_LORA + KV_LORA
O_GATE = O_KR + QK_ROPE
SMALL_BLK = 512
NT = (((1,), (1,)), ((), ()))


def _dot_nt(a, bt):
    return lax.dot_general(a, bt, NT, preferred_element_type=F32)


def _stream_maps(tm):
    n_prompt = T_P // tm
    return (lambda i, *_: (jnp.minimum(i, n_prompt - 1), 0),
            lambda i, *_: (jnp.maximum(i - n_prompt, 0), 0))


def _inproj_kernel(xp_ref, xs_ref, mod_ref, n1_ref, wc_ref, ws_ref, cw_ref, qn_ref, kvn_ref, wuq_ref,
                   rope_ref, zc_ref, q_ref, ckv_ref, krs_ref, nckv_ref, nkr_ref):
    i = pl.program_id(0)
    is_sample = i >= T_P // TM_IN
    seq = jnp.where(is_sample, DEC_SEQ, SEQ)
    mod = mod_ref[pl.ds(_mod_row(i, TM_IN), 1), :]
    shift1 = mod[:, 0:D_MODEL]
    scale1 = mod[:, D_MODEL:2 * D_MODEL]
    x = jnp.where(is_sample, xs_ref[...], xp_ref[...])
    h = ((x * _rms(x)) * n1_ref[...]) * (1.0 + scale1) + shift1
    hb = h.astype(BF16)

    n_lat = Q_LORA + KV_LORA
    w_small = jnp.concatenate([
        ws_ref[0:n_lat, :].astype(BF16),
        jnp.zeros((ROPE_LANE0, D_MODEL), BF16),
        ws_ref[n_lat:n_lat + QK_ROPE, :].astype(BF16),
        jnp.zeros((LANE - ROPE_LANE0 - QK_ROPE, D_MODEL), BF16)], axis=0)
    sm = _dot_nt(hb, w_small)
    cq = sm[:, 0:Q_LORA]
    ckv_raw = sm[:, Q_LORA:Q_LORA + KV_LORA]
    krs = sm[:, Q_LORA + KV_LORA:SMALL_COLS]
    cqn = (cq * _rms(cq)) * qn_ref[...]
    q = _dot(cqn.astype(BF16), wuq_ref[...])
    ckv = (ckv_raw * _rms(ckv_raw)) * kvn_ref[...]
    ckv_ref[...] = ckv

    @pl.when(jnp.logical_not(is_sample))
    def _():
        nckv_ref[...] = ckv
        for s in range(TM_IN // SEQ):
            kt = krs[s * SEQ:(s + 1) * SEQ, :].T
            nkr_ref[s] = kt[ROPE_LANE0:ROPE_LANE0 + QK_ROPE, :]

    cos = rope_ref[0]
    sin_lo = rope_ref[1]
    sin_hi = rope_ref[2]

    def rot(v):
        return v * cos + pltpu.roll(v, 8, 1) * sin_lo + pltpu.roll(v, LANE - 8, 1) * sin_hi

    krs_ref[...] = rot(krs)
    for hh in range(N_HEADS):
        q_ref[:, LANE * hh:LANE * (hh + 1)] = rot(q[:, LANE * hh:LANE * (hh + 1)]).astype(BF16)

    pos = lax.broadcasted_iota(jnp.int32, (TM_IN, 1), 0) & (seq - 1)
    first = pos == 0
    last = pos == seq - 1
    for j in range(D_CONV // CONV_CHUNK):
        c0 = j * CONV_CHUNK
        bg = _dot_nt(hb, wc_ref[c0:c0 + CONV_CHUNK, :].astype(BF16))
        cg = _dot_nt(hb, wc_ref[D_CONV + c0:D_CONV + c0 + CONV_CHUNK, :].astype(BF16))
        ui = _dot_nt(hb, wc_ref[2 * D_CONV + c0:2 * D_CONV + c0 + CONV_CHUNK, :].astype(BF16))
        u = cg * ui
        u_prev = jnp.where(first, 0.0, pltpu.roll(u, 1, 0))
        u_next = jnp.where(last, 0.0, pltpu.roll(u, TM_IN - 1, 0))
        cw = cw_ref[:, c0:c0 + CONV_CHUNK]
        conv = u_prev * cw[0:1] + u * cw[1:2] + u_next * cw[2:3]
        zc_ref[:, c0:c0 + CONV_CHUNK] = (bg * conv).astype(BF16)


def _inproj(xp, xs, mod, norm1, w_in_t, conv_w, q_norm, kv_norm, w_uq_slot, rope_tabs):
    n_prompt = T_P // TM_IN
    const = lambda i: (0, 0)
    pmap, smap = _stream_maps(TM_IN)
    once = pl.Buffered(1)
    return pl.pallas_call(
        _inproj_kernel,
        grid=(T // TM_IN,),
        in_specs=[
            pl.BlockSpec((TM_IN, D_MODEL), pmap),
            pl.BlockSpec((TM_IN, D_MODEL), smap),
            pl.BlockSpec((N_COND, 6 * D_MODEL), const),
            pl.BlockSpec((1, D_MODEL), const),
            pl.BlockSpec((O_CQ, D_MODEL), const, pipeline_mode=once),
            pl.BlockSpec((SMALL_BLK, D_MODEL), lambda i: (O_CQ // SMALL_BLK, 0), pipeline_mode=once),
            pl.BlockSpec((3, D_CONV), const),
            pl.BlockSpec((1, Q_LORA), const),
            pl.BlockSpec((1, KV_LORA), const),
            pl.BlockSpec((Q_LORA, N_HEADS * LANE), const),
            pl.BlockSpec((None, 3, TM_IN, LANE),
                         lambda i: (jnp.where(i >= n_prompt, 1, 0), 0, 0, 0)),
        ],
        out_specs=[
            pl.BlockSpec((TM_IN, D_CONV), lambda i: (i, 0)),
            pl.BlockSpec((TM_IN, N_HEADS * LANE), lambda i: (i, 0)),
            pl.BlockSpec((TM_IN, KV_LORA), lambda i: (i, 0)),
            pl.BlockSpec((TM_IN, LANE), lambda i: (i, 0)),
            pl.BlockSpec((TM_IN, KV_LORA), pmap),
            pl.BlockSpec((TM_IN // SEQ, QK_ROPE, SEQ), lambda i: (jnp.minimum(i, n_prompt - 1), 0, 0)),
        ],
        out_shape=[
            jax.ShapeDtypeStruct((T, D_CONV), BF16),
            jax.ShapeDtypeStruct((T, N_HEADS * LANE), BF16),
            jax.ShapeDtypeStruct((T, KV_LORA), F32),
            jax.ShapeDtypeStruct((T, LANE), F32),
            jax.ShapeDtypeStruct((T_P, KV_LORA), F32),
            jax.ShapeDtypeStruct((BATCH, QK_ROPE, SEQ), F32),
        ],
        compiler_params=pltpu.CompilerParams(
            dimension_semantics=("arbitrary",), vmem_limit_bytes=VMEM_LIMIT),
        name="in_proj",
    )(xp, xs, mod, norm1, w_in_t, w_in_t, conv_w, q_norm, kv_norm, w_uq_slot, rope_tabs)


def _fill_kv(ckv, krs, wukv_ref, kf_scr, v_scr, off):
    m = ckv.shape[0]
    kv = _dot(ckv.astype(BF16), wukv_ref[...])
    for hh in range(N_HEADS):
        kf_scr[hh, off:off + m, :] = (kv[:, LANE * hh:LANE * (hh + 1)] + krs).astype(BF16)
    v_scr[off:off + m, :] = kv[:, N_HEADS * LANE:].astype(BF16)


def _attend(q_ref, r0, rows, kf_scr, v_scr, o_ref):
    for pair in range(N_HEADS // 2):
        acc = None
        for hh in (2 * pair, 2 * pair + 1):
            qh = q_ref[r0:r0 + rows, LANE * hh:LANE * (hh + 1)]
            s = _dot_nt(qh, kf_scr[hh]) * ATTN_SCALE
            e = jnp.exp(s - jnp.max(s, axis=-1, keepdims=True))
            p = (e / jnp.sum(e, axis=-1, keepdims=True)).astype(BF16)
            part = _dot(p, v_scr[:, LANE * hh:LANE * (hh + 1)])
            acc = part if acc is None else acc + part
        o_ref[r0:r0 + rows, LANE * pair:LANE * (pair + 1)] = acc.astype(BF16)


def _attn_prompt_kernel(q_ref, ckv_ref, krs_ref, wukv_ref, o_ref):
    for s in range(PROMPT_SEQS):
        r0 = s * SEQ
        kv = _dot(ckv_ref[r0:r0 + SEQ, :].astype(BF16), wukv_ref[...])
        krs = krs_ref[r0:r0 + SEQ, :]
        for pair in range(N_HEADS // 2):
            acc = None
            for hh in (2 * pair, 2 * pair + 1):
                kf = (kv[:, LANE * hh:LANE * (hh + 1)] + krs).astype(BF16)
                vh = kv[:, LANE * (N_HEADS + hh):LANE * (N_HEADS + hh + 1)].astype(BF16)
                sc = _dot_nt(q_ref[r0:r0 + SEQ, LANE * hh:LANE * (hh + 1)], kf) * ATTN_SCALE
                e = jnp.exp(sc - jnp.max(sc, axis=-1, keepdims=True))
                p = (e / jnp.sum(e, axis=-1, keepdims=True)).astype(BF16)
                part = _dot(p, vh)
                acc = part if acc is None else acc + part
            o_ref[r0:r0 + SEQ, LANE * pair:LANE * (pair + 1)] = acc.astype(BF16)


def _attn_sample_kernel(q_ref, ckv_ref, cckv_ref, krs_ref, ckrs_ref, wukv_ref, wg_ref, wco_ref, wo_ref,
                        wmix_ref, o_ref, g_ref, co_ref, ob_ref, mix_ref, kf_scr, v_scr):
    @pl.when(pl.program_id(1) == 0)
    def _():
        _fill_kv(ckv_ref[...], krs_ref[...], wukv_ref, kf_scr, v_scr, 0)
        _fill_kv(cckv_ref[...], ckrs_ref[...], wukv_ref, kf_scr, v_scr, DEC_SEQ)

    _attend(q_ref, 0, Q_BLK_S, kf_scr, v_scr, o_ref)
    g_ref[...] = wg_ref[...].astype(BF16)
    co_ref[...] = wco_ref[...].astype(BF16)
    ob_ref[...] = wo_ref[...].astype(BF16)
    mix_ref[...] = wmix_ref[...].astype(BF16)


def _attention(q, ckv, krs, cache_ckv, cache_krs, w_ukv_slot, w_in_t, w_conv_out, w_o, w_mix_out):
    kv_cols = 2 * N_HEADS * LANE
    n_o = N_HEADS * V_HEAD
    steps = BATCH // PROMPT_SEQS
    rows = PROMPT_SEQS * SEQ
    o_prompt = pl.pallas_call(
        _attn_prompt_kernel,
        grid=(steps,),
        in_specs=[
            pl.BlockSpec((rows, N_HEADS * LANE), lambda b: (b, 0)),
            pl.BlockSpec((rows, KV_LORA), lambda b: (b, 0)),
            pl.BlockSpec((rows, LANE), lambda b: (b, 0)),
            pl.BlockSpec((KV_LORA, kv_cols), lambda b: (0, 0)),
        ],
        out_specs=pl.BlockSpec((rows, n_o), lambda b: (b, 0)),
        out_shape=jax.ShapeDtypeStruct((T_P, n_o), BF16),
        compiler_params=pltpu.CompilerParams(dimension_semantics=("parallel",),
                                             vmem_limit_bytes=VMEM_LIMIT),
        name="attn_prompt",
    )(q, ckv, krs, w_ukv_slot)

    m_all = DEC_SEQ + PAST_LEN
    nq = DEC_SEQ // Q_BLK_S
    q0 = T_P // Q_BLK_S
    s0 = T_P // DEC_SEQ
    n_step = DEC_BATCH * nq
    share = lambda n: pl.BlockSpec((n // n_step, D_MODEL), lambda b, j: (b * nq + j, 0))
    gate_rows = 2 * D_MODEL // n_step
    o_sample, w_gates_t, w_co_b, w_o_b, w_mix_b = pl.pallas_call(
        _attn_sample_kernel,
        grid=(DEC_BATCH, nq),
        in_specs=[
            pl.BlockSpec((Q_BLK_S, N_HEADS * LANE), lambda b, j: (q0 + b * nq + j, 0)),
            pl.BlockSpec((DEC_SEQ, KV_LORA), lambda b, j: (s0 + b, 0)),
            pl.BlockSpec((None, PAST_LEN, KV_LORA), lambda b, j: (b, 0, 0)),
            pl.BlockSpec((DEC_SEQ, LANE), lambda b, j: (s0 + b, 0)),
            pl.BlockSpec((None, PAST_LEN, LANE), lambda b, j: (b, 0, 0)),
            pl.BlockSpec((KV_LORA, kv_cols), lambda b, j: (0, 0)),
            pl.BlockSpec((pl.Element(gate_rows), pl.Element(D_MODEL)),
                         lambda b, j: (pl.multiple_of(O_GATE + (b * nq + j) * gate_rows, 32), 0)),
            share(D_CONV), share(n_o), share(D_MODEL),
        ],
        out_specs=[pl.BlockSpec((Q_BLK_S, n_o), lambda b, j: (b * nq + j, 0)),
                   share(2 * D_MODEL), share(D_CONV), share(n_o), share(D_MODEL)],
        out_shape=[jax.ShapeDtypeStruct((T_S, n_o), BF16),
                   jax.ShapeDtypeStruct((2 * D_MODEL, D_MODEL), BF16),
                   jax.ShapeDtypeStruct((D_CONV, D_MODEL), BF16),
                   jax.ShapeDtypeStruct((n_o, D_MODEL), BF16),
                   jax.ShapeDtypeStruct((D_MODEL, D_MODEL), BF16)],
        scratch_shapes=[pltpu.VMEM((N_HEADS, m_all, LANE), BF16),
                        pltpu.VMEM((m_all, N_HEADS * LANE), BF16)],
        compiler_params=pltpu.CompilerParams(dimension_semantics=("arbitrary", "arbitrary"),
                                             vmem_limit_bytes=VMEM_LIMIT),
        name="attn_sample",
    )(q, ckv, cache_ckv, krs, cache_krs, w_ukv_slot, w_in_t, w_conv_out, w_o, w_mix_out)
    return o_prompt, o_sample, w_gates_t, w_co_b, w_o_b, w_mix_b


def _route(logits):
    lane = lax.broadcasted_iota(jnp.int32, logits.shape, 1)
    neg = -jnp.inf
    big = jnp.int32(1 << 20)
    gmask = (lane >= N_EXPERTS) & (lane < N_EXPERTS + N_GROUPS)
    gl = jnp.where(gmask, logits, neg)
    gmax = jnp.max(gl, axis=-1, keepdims=True)
    gsum = jnp.sum(jnp.where(gmask, jnp.exp(gl - gmax), 0.0), axis=-1, keepdims=True)
    p_g = 1.0 / gsum
    g_idx = jnp.min(jnp.where(gl == gmax, lane, big), axis=-1, keepdims=True) - N_EXPERTS

    emask = (lane < N_EXPERTS) & ((lane >> 3) == g_idx)
    el = jnp.where(emask, logits, neg)
    m1 = jnp.max(el, axis=-1, keepdims=True)
    i1 = jnp.min(jnp.where(el == m1, lane, big), axis=-1, keepdims=True)
    el2 = jnp.where(lane == i1, neg, el)
    m2 = jnp.max(el2, axis=-1, keepdims=True)
    i2 = jnp.min(jnp.where(el2 == m2, lane, big), axis=-1, keepdims=True)
    z = jnp.sum(jnp.where(emask, jnp.exp(el - m1), 0.0), axis=-1, keepdims=True)
    p1 = 1.0 / z
    p2 = jnp.exp(m2 - m1) / z
    tot = p1 + p2
    w1 = p_g * p1 / tot
    w2 = p_g * p2 / tot
    return jnp.where(lane == i1, w1, 0.0) + jnp.where(lane == i2, w2, 0.0), g_idx


def _post_kernel(xp_ref, xs_ref, mod_ref, n1_ref, wg_ref, zc_ref, op_ref, os_ref, wco_ref, wo_ref,
                 wmix_ref, n2_ref, wr_ref, x1_ref, h3_ref, meta_ref, cnt_ref, gr_ref):
    i = pl.program_id(0)
    is_sample = i >= T_P // TM_POST
    x = jnp.where(is_sample, xs_ref[...], xp_ref[...])
    o = jnp.where(is_sample, os_ref[...], op_ref[...])
    mod = mod_ref[pl.ds(_mod_row(i, TM_POST), 1), :]
    shift1 = mod[:, 0:D_MODEL]
    scale1 = mod[:, D_MODEL:2 * D_MODEL]
    gate1 = mod[:, 2 * D_MODEL:3 * D_MODEL]
    shift2 = mod[:, 3 * D_MODEL:4 * D_MODEL]
    scale2 = mod[:, 4 * D_MODEL:5 * D_MODEL]
    y_conv = _dot(zc_ref[...], wco_ref[...])
    y_mla = _dot(o, wo_ref[...])
    h = ((x * _rms(x)) * n1_ref[...]) * (1.0 + scale1) + shift1
    g = _dot_nt(h.astype(BF16), wg_ref[...])
    merged = (jax.nn.sigmoid(g[:, 0:D_MODEL]) * y_conv
              + jax.nn.sigmoid(g[:, D_MODEL:2 * D_MODEL]) * y_mla)
    y = _dot(merged.astype(BF16), wmix_ref[...])
    x1 = x + gate1 * y
    x1_ref[...] = x1
    h2 = ((x1 * _rms(x1)) * n2_ref[...]) * (1.0 + scale2) + shift2
    h2_hi = h2.astype(BF16)
    h2_lo = (h2 - h2_hi.astype(F32)).astype(BF16)
    hh = _dot(h2_hi, wr_ref[...])
    logits = hh[:, 0:LANE] + hh[:, LANE:2 * LANE] + _dot(h2_lo, wr_ref[:, 0:LANE])
    comb, g_idx = _route(logits)

    lane = lax.broadcasted_iota(jnp.int32, comb.shape, 1)
    onehot = lane == g_idx + N_EXPERTS
    r_i = lax.broadcasted_iota(jnp.int32, (TM_POST, TM_POST), 0)
    c_i = lax.broadcasted_iota(jnp.int32, (TM_POST, TM_POST), 1)
    lower = jnp.where(c_i < r_i, 1.0, 0.0).astype(BF16)
    before = _dot(lower, jnp.where(onehot, 1.0, 0.0).astype(BF16))
    rank = jnp.sum(jnp.where(onehot, before, 0.0), axis=-1, keepdims=True)
    counts = jnp.sum(jnp.where(onehot, 1.0, 0.0), axis=0, keepdims=True)
    cnt_ref[...] = jnp.broadcast_to(counts, cnt_ref.shape)

    meta_ref[...] = comb
    idx = jnp.where(lane == GID_LANE, g_idx.astype(F32), 0.0) + jnp.where(lane == RANK_LANE, rank, 0.0)
    idx_hi = idx.astype(BF16)
    idx_lo = (idx - idx_hi.astype(F32)).astype(BF16)
    s_row = lax.broadcasted_iota(jnp.int32, (8, LANE), 0)
    s_lane = lax.broadcasted_iota(jnp.int32, (8, LANE), 1)
    sel = jnp.where(s_lane == GID_LANE + s_row, 1.0, 0.0).astype(BF16)
    gr_ref[...] = _dot_nt(sel, idx_hi) + _dot_nt(sel, idx_lo)
    for c in range(N_SLAB):
        h3_ref[pl.ds(c, TM_POST, stride=N_SLAB), :] = h2[:, LANE * c:LANE * (c + 1)]


def _post(xp, xs, mod, norm1, w_gates_t, zc, o_p, o_s, w_conv_out, w_o, w_mix_out, norm2, w_route):
    const = lambda i: (0, 0)
    row = lambda i: (i, 0)
    pmap, smap = _stream_maps(TM_POST)
    return pl.pallas_call(
        _post_kernel,
        grid=(T // TM_POST,),
        in_specs=[
            pl.BlockSpec((TM_POST, D_MODEL), pmap),
            pl.BlockSpec((TM_POST, D_MODEL), smap),
            pl.BlockSpec((N_COND, 6 * D_MODEL), const),
            pl.BlockSpec((1, D_MODEL), const),
            pl.BlockSpec((2 * D_MODEL, D_MODEL), const),
            pl.BlockSpec((TM_POST, D_CONV), row),
            pl.BlockSpec((TM_POST, N_HEADS * V_HEAD), pmap),
            pl.BlockSpec((TM_POST, N_HEADS * V_HEAD), smap),
            pl.BlockSpec((D_CONV, D_MODEL), const),
            pl.BlockSpec((N_HEADS * V_HEAD, D_MODEL), const),
            pl.BlockSpec((D_MODEL, D_MODEL), const),
            pl.BlockSpec((1, D_MODEL), const),
            pl.BlockSpec((D_MODEL, 2 * LANE), const),
        ],
        out_specs=[
            pl.BlockSpec((TM_POST, D_MODEL), row),
            pl.BlockSpec((TM_POST * N_SLAB, LANE), row),
            pl.BlockSpec((TM_POST, LANE), row),
            pl.BlockSpec((None, 8, LANE), lambda i: (i, 0, 0)),
            pl.BlockSpec((8, TM_POST), lambda i: (0, i)),
        ],
        out_shape=[
            jax.ShapeDtypeStruct((T, D_MODEL), F32),
            jax.ShapeDtypeStruct((T * N_SLAB, LANE), F32),
            jax.ShapeDtypeStruct((T, LANE), F32),
            jax.ShapeDtypeStruct((T // TM_POST, 8, LANE), F32),
            jax.ShapeDtypeStruct((8, T), F32),
        ],
        compiler_params=pltpu.CompilerParams(
            dimension_semantics=("parallel",), vmem_limit_bytes=VMEM_LIMIT),
        name="post_mixer",
    )(xp, xs, mod, norm1, w_gates_t, zc, o_p, o_s, w_conv_out, w_o, w_mix_out, norm2, w_route)


DISP_IN = T // TM_DISP
DISP_OUT = T // TM_MOE


def _dispatch_kernel(pos_ref, h3_ref, m_ref, hs_ref, ms_ref, xs_ref, mss_ref):
    i = pl.program_id(0)

    @pl.when(i < DISP_IN)
    def _():
        def body(r, carry):
            p = pos_ref[0, r]
            xs_ref[_slab(p), :] = h3_ref[_slab(r), :]
            mss_ref[pl.ds(p, 1), :] = m_ref[pl.ds(r, 1), :]
            return carry

        lax.fori_loop(0, TM_DISP, body, 0, unroll=8)

    @pl.when(i >= DISP_IN)
    def _():
        row0 = pl.multiple_of((i - DISP_IN) * TM_MOE, TM_MOE)
        for c in range(N_SLAB):
            hs_ref[:, LANE * c:LANE * (c + 1)] = (
                xs_ref[pl.ds(row0 * N_SLAB + c, TM_MOE, stride=N_SLAB), :].astype(BF16))
        ms_ref[...] = mss_ref[pl.ds(row0, TM_MOE), :]


def _dispatch(pos, h3, meta):
    n_slab = N_SLAB
    in_map = lambda i: (jnp.minimum(i, DISP_IN - 1), 0)
    out_map = lambda i: (jnp.maximum(i - DISP_IN, 0), 0)
    return pl.pallas_call(
        _dispatch_kernel,
        grid=(DISP_IN + DISP_OUT,),
        in_specs=[pl.BlockSpec((None, 1, TM_DISP), lambda i: (jnp.minimum(i, DISP_IN - 1), 0, 0),
                               memory_space=pltpu.SMEM),
                  pl.BlockSpec((TM_DISP * n_slab, LANE), in_map),
                  pl.BlockSpec((TM_DISP, LANE), in_map)],
        out_specs=[pl.BlockSpec((TM_MOE, D_MODEL), out_map),
                   pl.BlockSpec((TM_MOE, LANE), out_map)],
        scratch_shapes=[pltpu.VMEM((T * n_slab, LANE), F32),
                        pltpu.VMEM((T, LANE), F32)],
        out_shape=[jax.ShapeDtypeStruct((T, D_MODEL), BF16),
                   jax.ShapeDtypeStruct((T, LANE), F32)],
        compiler_params=pltpu.CompilerParams(
            dimension_semantics=("arbitrary",), vmem_limit_bytes=VMEM_LIMIT),
        name="moe_dispatch",
    )(pos.reshape(DISP_IN, 1, TM_DISP), h3, meta)


def _moe_kernel(sched_ref, hs_ref, ms_ref, wup_ref, wgate_ref, wdown_ref, y3_ref, acc_ref):
    v = pl.program_id(0)
    j = pl.program_id(1)
    valid = sched_ref[V_VALID, v] == 1
    lo = sched_ref[V_LO, v]
    hi = sched_ref[V_HI, v]
    e0 = sched_ref[V_GROUP, v] * EXP_PER_GROUP + j * MOE_EPS
    full = (hi - lo) * 4 >= TM_MOE * 3

    @pl.when(valid & (j == 0) & (sched_ref[V_FIRST, v] == 1))
    def _():
        acc_ref[...] = jnp.zeros_like(acc_ref)

    def expert_rows(r0, rows):
        w_in2 = jnp.concatenate(
            [w[k].astype(BF16) for k in range(MOE_EPS) for w in (wup_ref, wgate_ref)], axis=1)
        ag = _dot(hs_ref[pl.ds(r0, rows), :], w_in2)
        comb = ms_ref[pl.ds(r0, rows), :]
        lane = lax.broadcasted_iota(jnp.int32, comb.shape, 1)
        acts = []
        for k in range(MOE_EPS):
            a = ag[:, 2 * k * D_EXPERT:(2 * k + 1) * D_EXPERT]
            g = ag[:, (2 * k + 1) * D_EXPERT:(2 * k + 2) * D_EXPERT]
            cw = jnp.sum(jnp.where(lane == e0 + k, comb, 0.0), axis=-1, keepdims=True)
            acts.append(((g * jax.nn.sigmoid(g)) * a * cw).astype(BF16))
        w_out = jnp.concatenate([wdown_ref[k].astype(BF16) for k in range(MOE_EPS)], axis=0)
        acc_ref[pl.ds(r0, rows), :] += _dot(jnp.concatenate(acts, axis=1), w_out)

    @pl.when(valid & full)
    def _():
        expert_rows(0, TM_MOE)

    @pl.when(valid & jnp.logical_not(full))
    def _():
        def sub_block(s, carry):
            expert_rows(pl.multiple_of(s * MOE_SUB, MOE_SUB), MOE_SUB)
            return carry

        lax.fori_loop(lo // MOE_SUB, (hi + MOE_SUB - 1) // MOE_SUB, sub_block, 0)

    @pl.when(valid & (j == EXP_PER_GROUP // MOE_EPS - 1) & (sched_ref[V_LAST, v] == 1))
    def _():
        for c in range(N_SLAB):
            y3_ref[pl.ds(c, TM_MOE, stride=N_SLAB), :] = acc_ref[:, LANE * c:LANE * (c + 1)]


def _moe(sched, hs, ms, w_up, w_gate, w_down):
    steps = EXP_PER_GROUP // MOE_EPS
    wmap = lambda v, j, sched: (
        sched[V_GROUP, v] * steps + jnp.where(sched[V_VALID, v] == 1, j, steps - 1), 0, 0)
    tmap = lambda v, j, sched: (sched[V_TILE, v], 0)
    n_slab = D_MODEL // LANE
    return pl.pallas_call(
        _moe_kernel,
        grid_spec=pltpu.PrefetchScalarGridSpec(
            num_scalar_prefetch=1,
            grid=(N_VISITS, steps),
            in_specs=[
                pl.BlockSpec((TM_MOE, D_MODEL), tmap),
                pl.BlockSpec((TM_MOE, LANE), tmap),
                pl.BlockSpec((MOE_EPS, D_MODEL, D_EXPERT), wmap),
                pl.BlockSpec((MOE_EPS, D_MODEL, D_EXPERT), wmap),
                pl.BlockSpec((MOE_EPS, D_EXPERT, D_MODEL), wmap),
            ],
            out_specs=pl.BlockSpec((TM_MOE * n_slab, LANE), tmap),
            scratch_shapes=[pltpu.VMEM((TM_MOE, D_MODEL), F32)],
        ),
        out_shape=jax.ShapeDtypeStruct((T * n_slab, LANE), F32),
        compiler_params=pltpu.CompilerParams(
            dimension_semantics=("arbitrary", "arbitrary"), vmem_limit_bytes=VMEM_LIMIT),
        name="moe_grouped",
    )(sched, hs, ms, w_up, w_gate, w_down)


def _final_kernel(pos_ref, ys_ref, x1_ref, mod_ref, fn_ref, yp_ref, ysm_ref, g_ref):
    i = pl.program_id(0)

    def body(r, carry):
        g_ref[_slab(r), :] = ys_ref[_slab(pos_ref[0, r]), :]
        return carry

    lax.fori_loop(0, TM_FINAL, body, 0, unroll=8)
    mod = mod_ref[pl.ds(_mod_row(i, TM_FINAL), 1), :]
    gate2 = mod[:, 5 * D_MODEL:6 * D_MODEL]
    moe = jnp.concatenate([g_ref[pl.ds(c, TM_FINAL, stride=N_SLAB), :] for c in range(N_SLAB)], axis=1)
    x2 = x1_ref[...] + gate2 * moe
    y = (x2 * _rms(x2)) * fn_ref[...]
    is_sample = i >= T_P // TM_FINAL

    @pl.when(jnp.logical_not(is_sample))
    def _():
        yp_ref[...] = y

    @pl.when(is_sample)
    def _():
        ysm_ref[...] = y


def _final(pos, ys, x1, mod, final_norm):
    n_slab = D_MODEL // LANE
    pmap, smap = _stream_maps(TM_FINAL)
    return pl.pallas_call(
        _final_kernel,
        grid=(T // TM_FINAL,),
        in_specs=[
            pl.BlockSpec((None, 1, TM_FINAL), lambda i: (i, 0, 0), memory_space=pltpu.SMEM),
            pl.BlockSpec((T * n_slab, LANE), lambda i: (0, 0), pipeline_mode=pl.Buffered(1)),
            pl.BlockSpec((TM_FINAL, D_MODEL), lambda i: (i, 0)),
            pl.BlockSpec((N_COND, 6 * D_MODEL), lambda i: (0, 0)),
            pl.BlockSpec((1, D_MODEL), lambda i: (0, 0)),
        ],
        out_specs=[pl.BlockSpec((TM_FINAL, D_MODEL), pmap),
                   pl.BlockSpec((TM_FINAL, D_MODEL), smap)],
        scratch_shapes=[pltpu.VMEM((TM_FINAL * n_slab, LANE), F32)],
        out_shape=[jax.ShapeDtypeStruct((T_P, D_MODEL), F32),
                   jax.ShapeDtypeStruct((T_S, D_MODEL), F32)],
        compiler_params=pltpu.CompilerParams(
            dimension_semantics=("arbitrary",), vmem_limit_bytes=VMEM_LIMIT),
        name="moe_unsort_final",
    )(pos.reshape(T // TM_FINAL, 1, TM_FINAL), ys, x1, mod, final_norm)


def _plan_kernel(gr_ref, cnt_ref, pos_ref, sched_ref):
    n_post = T // TM_POST
    lane = lax.broadcasted_iota(jnp.int32, (1, LANE), 1)
    grp_lane = lambda v, g: v[:, N_EXPERTS + g:N_EXPERTS + g + 1]
    counts = [cnt_ref[k, 0:1, :] for k in range(n_post)]
    gtot_v = counts[0]
    for k in range(1, n_post):
        gtot_v = gtot_v + counts[k]
    gtot = [grp_lane(gtot_v, g) for g in range(N_GROUPS)]
    goff = [jnp.zeros((1, 1), F32)]
    for g in range(1, N_GROUPS):
        goff.append(goff[-1] + gtot[g - 1])

    before = [jnp.zeros((1, 1), F32) for _ in range(N_GROUPS)]
    rows_per_tile = TM_POST // LANE
    for k in range(n_post):
        gid = gr_ref[0:1, TM_POST * k:TM_POST * (k + 1)]
        pos = gr_ref[1:2, TM_POST * k:TM_POST * (k + 1)]
        for g in range(N_GROUPS):
            pos = pos + jnp.where(gid == float(g), goff[g] + before[g], 0.0)
            before[g] = before[g] + grp_lane(counts[k], g)
        for r in range(rows_per_tile):
            pos_ref[rows_per_tile * k + r:rows_per_tile * k + r + 1, :] = (
                pos[:, LANE * r:LANE * (r + 1)].astype(jnp.int32))

    zero = jnp.zeros((1, LANE), F32)
    rows = {name: zero for name in ("vt", "vg", "vlo", "vhi")}
    slot = jnp.zeros((1, 1), F32)
    last_t = jnp.zeros((1, 1), F32)
    last_g = jnp.zeros((1, 1), F32)
    for i in range(T // TM_MOE):
        for g in range(N_GROUPS):
            lo = jnp.clip(goff[g] - float(TM_MOE * i), 0.0, float(TM_MOE))
            hi = jnp.clip(goff[g] + gtot[g] - float(TM_MOE * i), 0.0, float(TM_MOE))
            ok = hi > lo
            here = ok & (lane == slot.astype(jnp.int32))
            rows["vt"] = jnp.where(here, float(i), rows["vt"])
            rows["vg"] = jnp.where(here, float(g), rows["vg"])
            rows["vlo"] = jnp.where(here, lo, rows["vlo"])
            rows["vhi"] = jnp.where(here, hi, rows["vhi"])
            last_t = jnp.where(ok, float(i), last_t)
            last_g = jnp.where(ok, float(g), last_g)
            slot = slot + jnp.where(ok, 1.0, 0.0)
    n_ok = slot.astype(jnp.int32)
    valid = lane < n_ok
    vt = jnp.where(valid, rows["vt"], last_t)
    vg = jnp.where(valid, rows["vg"], last_g)
    prev_t = pltpu.roll(jnp.broadcast_to(vt, (8, LANE)), 1, 1)[0:1, :]
    next_t = pltpu.roll(jnp.broadcast_to(vt, (8, LANE)), LANE - 1, 1)[0:1, :]
    first = jnp.where((lane == 0) | (vt != prev_t), 1.0, 0.0)
    last = jnp.where((lane == n_ok - 1) | (vt != next_t), 1.0, 0.0)
    table = [vt, vg, rows["vlo"], rows["vhi"], first, last, jnp.where(valid, 1.0, 0.0), zero]
    for r, row in enumerate(table):
        sched_ref[r:r + 1, :] = row.astype(jnp.int32)


V_TILE, V_GROUP, V_LO, V_HI, V_FIRST, V_LAST, V_VALID = range(7)


def _plan(gr, cnt):
    pos, sched = pl.pallas_call(
        _plan_kernel,
        out_shape=[jax.ShapeDtypeStruct((T // LANE, LANE), jnp.int32),
                   jax.ShapeDtypeStruct((8, LANE), jnp.int32)],
        name="moe_plan",
    )(gr, cnt)
    return pos.reshape(T), sched


def _rope_tables():
    n = np.arange(DEC_SEQ)
    pos = np.stack([n // GRID_W, n % GRID_W], axis=1).astype(np.float32)
    half = ROPE_AXIS // 2
    inv = (1.0 / (ROPE_BASE ** (np.arange(0, ROPE_AXIS, 2, dtype=np.float32) / ROPE_AXIS))).astype(np.float32)
    ang = (pos[:, :, None] * inv[None, None, :]).astype(np.float32)
    cos = np.cos(ang).astype(np.float32)
    sin = np.sin(ang).astype(np.float32)
    tabs = np.zeros((2, 3, DEC_SEQ, LANE), np.float32)
    tabs[:, 0] = 1.0
    for a in range(2):
        lo = ROPE_LANE0 + a * ROPE_AXIS
        tabs[1, 0, :, lo:lo + half] = cos[:, a]
        tabs[1, 0, :, lo + half:lo + 2 * half] = cos[:, a]
        tabs[1, 1, :, lo + half:lo + 2 * half] = sin[:, a]
        tabs[1, 2, :, lo:lo + half] = -sin[:, a]
    return jnp.asarray(tabs)


def kernel(x_prompt, x_sample, cache_ckv, cache_krope, c, c_ctx, norm1, w_ada, b_ada, w_in, conv_w,
           w_conv_out, q_norm, w_uq, kv_norm, w_ukv, w_o, w_mix_out, norm2, w_grp, w_exp, w_up,
           w_gate, w_down, final_norm):
    l = 0
    xp = x_prompt.reshape(T_P, D_MODEL)
    xs = x_sample.reshape(T_S, D_MODEL)
    mod = _ada(c_ctx[None, :], c, w_ada[l], b_ada[l][None, :])

    w_in_t = w_in[l].T
    w_uq_slot = jnp.pad(w_uq[l].reshape(Q_LORA, N_HEADS, QK_NOPE + QK_ROPE),
                        ((0, 0), (0, 0), (0, LANE - QK_NOPE - QK_ROPE))
                        ).reshape(Q_LORA, N_HEADS * LANE).astype(BF16)
    wkv = w_ukv[l].reshape(KV_LORA, N_HEADS, QK_NOPE + V_HEAD)
    wk_slot = jnp.pad(wkv[:, :, :QK_NOPE], ((0, 0), (0, 0), (0, LANE - QK_NOPE)))
    wv = wkv[:, :, QK_NOPE:].reshape(KV_LORA, N_HEADS // 2, 2, V_HEAD)
    zero = jnp.zeros_like(wv[:, :, 0])
    wv_slot = jnp.stack([jnp.concatenate([wv[:, :, 0], zero], axis=-1),
                         jnp.concatenate([zero, wv[:, :, 1]], axis=-1)], axis=2)
    w_ukv_slot = jnp.concatenate([wk_slot.reshape(KV_LORA, N_HEADS * LANE),
                                  wv_slot.reshape(KV_LORA, N_HEADS * LANE)], axis=1).astype(BF16)
    w_route = jnp.pad(jnp.concatenate([w_exp[l], w_grp[l]], axis=1),
                      ((0, 0), (0, LANE - N_EXPERTS - N_GROUPS)))
    w_route_hi = w_route.astype(BF16)
    w_route_lo = (w_route - w_route_hi.astype(F32)).astype(BF16)
    w_route2 = jnp.concatenate([w_route_hi, w_route_lo], axis=1)
    cache_krs = jnp.pad(cache_krope[:, l], ((0, 0), (0, 0), (ROPE_LANE0, LANE - ROPE_LANE0 - QK_ROPE)))

    zc, q, ckv, krs, nckv, nkr = _inproj(xp, xs, mod, norm1[l][None, :], w_in_t, conv_w[l],
                              q_norm[l][None, :], kv_norm[l][None, :], w_uq_slot, _rope_tables())
    o_p, o_s, w_gates_t, w_co_b, w_o_b, w_mix_b = _attention(
        q, ckv, krs, cache_ckv[:, l], cache_krs, w_ukv_slot, w_in_t, w_conv_out[l], w_o[l], w_mix_out[l])
    x1, h3, meta, cnt, gr = _post(xp, xs, mod, norm1[l][None, :], w_gates_t, zc, o_p, o_s, w_co_b, w_o_b,
                                  w_mix_b, norm2[l][None, :], w_route2)
    pos, sched = _plan(gr, cnt)
    hs, ms = _dispatch(pos, h3, meta)
    ys = _moe(sched, hs, ms, w_up[l], w_gate[l], w_down[l])
    yp, ysm = _final(pos, ys, x1, mod, final_norm[None, :])

    y_prompt = yp.reshape(BATCH, SEQ, D_MODEL)
    y_sample = ysm.reshape(DEC_BATCH, DEC_SEQ, D_MODEL)
    new_ckv = nckv.reshape(BATCH, 1, SEQ, KV_LORA)
    new_krope = jnp.swapaxes(nkr, 1, 2).reshape(BATCH, 1, SEQ, QK_ROPE)
    return (y_prompt, y_sample, new_ckv, new_krope)
```

```python
import numpy as np
import jax
import jax.numpy as jnp
from jax import lax
from jax.experimental import pallas as pl
from jax.experimental.pallas import tpu as pltpu

F32 = jnp.float32
BF16 = jnp.bfloat16

D_MODEL = 1024
BATCH = 16
SEQ = 256
DEC_BATCH = 2
DEC_SEQ = 1024
PAST_LEN = 256
GRID_W = 64
N_HEADS = 8
QK_NOPE = 64
QK_ROPE = 32
V_HEAD = 64
Q_LORA = 256
KV_LORA = 128
ROPE_AXIS = QK_ROPE // 2
ROPE_BASE = 10000.0
ATTN_SCALE = (QK_NOPE + QK_ROPE) ** -0.5
D_CONV = D_MODEL
N_GROUPS = 4
EXP_PER_GROUP = 8
N_EXPERTS = N_GROUPS * EXP_PER_GROUP
D_EXPERT = 256
EPS = 1e-6

T_P = BATCH * SEQ
T_S = DEC_BATCH * DEC_SEQ
T = T_P + T_S
N_COND = 8
LANE = 128
ROPE_LANE0 = QK_NOPE
SMALL_COLS = Q_LORA + KV_LORA + LANE
VMEM_LIMIT = 56 * 1024 * 1024

TM_IN = 1024
TM_POST = 512
TM_MOE = T // N_GROUPS
TM_FINAL = 512
TM_DISP = 1024
N_SLAB = D_MODEL // LANE
MOE_SUB = 256
MOE_EPS = 2
N_VISITS = T // TM_MOE + N_GROUPS - 1
GID_LANE = 40
RANK_LANE = 41
Q_BLK_S = 512
PROMPT_SEQS = 1
CONV_CHUNK = 256


def _dot(a, b):
    return jnp.dot(a, b, preferred_element_type=F32)


def _rms(x):
    return lax.rsqrt(jnp.mean(x * x, axis=-1, keepdims=True) + EPS)


def _slab(t):
    return pl.ds(pl.multiple_of(t * N_SLAB, N_SLAB), N_SLAB)


def _mod_row(i, tm):
    n_prompt = T_P // tm
    return jnp.where(i >= n_prompt, 1 + ((i - n_prompt) * tm) // DEC_SEQ, 0)


def _ada_kernel(cctx_ref, c_ref, w_ref, b_ref, o_ref):
    c = jnp.concatenate([cctx_ref[...], c_ref[...],
                         jnp.zeros((N_COND - 1 - DEC_BATCH, D_MODEL), F32)], axis=0)
    a = (c * jax.nn.sigmoid(c)).astype(BF16)
    o_ref[...] = _dot(a, w_ref[...].astype(BF16)) + b_ref[...]


def _ada(c_ctx, c, w_ada, b_ada):
    n = 6 * D_MODEL
    bn = 1536
    return pl.pallas_call(
        _ada_kernel,
        grid=(n // bn,),
        in_specs=[
            pl.BlockSpec((1, D_MODEL), lambda j: (0, 0)),
            pl.BlockSpec((DEC_BATCH, D_MODEL), lambda j: (0, 0)),
            pl.BlockSpec((D_MODEL, bn), lambda j: (0, j)),
            pl.BlockSpec((1, bn), lambda j: (0, j)),
        ],
        out_specs=pl.BlockSpec((N_COND, bn), lambda j: (0, j)),
        out_shape=jax.ShapeDtypeStruct((N_COND, n), F32),
        compiler_params=pltpu.CompilerParams(
            dimension_semantics=("parallel",), vmem_limit_bytes=VMEM_LIMIT),
        name="ada_mod",
    )(c_ctx, c, w_ada, b_ada)


O_CQ = 3 * D_CONV
O_KR = O_CQ + Q_LORA + KV_LORA
O_GATE = O_KR + QK_ROPE
SMALL_BLK = 512
NT = (((1,), (1,)), ((), ()))


def _dot_nt(a, bt):
    return lax.dot_general(a, bt, NT, preferred_element_type=F32)


def _stream_maps(tm):
    n_prompt = T_P // tm
    return (lambda i, *_: (jnp.minimum(i, n_prompt - 1), 0),
            lambda i, *_: (jnp.maximum(i - n_prompt, 0), 0))


def _inproj_kernel(xp_ref, xs_ref, mod_ref, n1_ref, wc_ref, ws_ref, cw_ref, qn_ref, kvn_ref, wuq_ref,
                   rope_ref, zc_ref, q_ref, ckv_ref, krs_ref, nckv_ref, nkr_ref):
    i = pl.program_id(0)
    is_sample = i >= T_P // TM_IN
    seq = jnp.where(is_sample, DEC_SEQ, SEQ)
    mod = mod_ref[pl.ds(_mod_row(i, TM_IN), 1), :]
    shift1 = mod[:, 0:D_MODEL]
    scale1 = mod[:, D_MODEL:2 * D_MODEL]
    x = jnp.where(is_sample, xs_ref[...], xp_ref[...])
    h = ((x * _rms(x)) * n1_ref[...]) * (1.0 + scale1) + shift1
    hb = h.astype(BF16)

    n_lat = Q_LORA + KV_LORA
    w_small = jnp.concatenate([
        ws_ref[0:n_lat, :].astype(BF16),
        jnp.zeros((ROPE_LANE0, D_MODEL), BF16),
        ws_ref[n_lat:n_lat + QK_ROPE, :].astype(BF16),
        jnp.zeros((LANE - ROPE_LANE0 - QK_ROPE, D_MODEL), BF16)], axis=0)
    sm = _dot_nt(hb, w_small)
    cq = sm[:, 0:Q_LORA]
    ckv_raw = sm[:, Q_LORA:Q_LORA + KV_LORA]
    krs = sm[:, Q_LORA + KV_LORA:SMALL_COLS]
    cqn = (cq * _rms(cq)) * qn_ref[...]
    q = _dot(cqn.astype(BF16), wuq_ref[...])
    ckv = (ckv_raw * _rms(ckv_raw)) * kvn_ref[...]
    ckv_ref[...] = ckv

    @pl.when(jnp.logical_not(is_sample))
    def _():
        nckv_ref[...] = ckv
        for s in range(TM_IN // SEQ):
            kt = krs[s * SEQ:(s + 1) * SEQ, :].T
            nkr_ref[s] = kt[ROPE_LANE0:ROPE_LANE0 + QK_ROPE, :]

    cos = rope_ref[0]
    sin_lo = rope_ref[1]
    sin_hi = rope_ref[2]

    def rot(v):
        return v * cos + pltpu.roll(v, 8, 1) * sin_lo + pltpu.roll(v, LANE - 8, 1) * sin_hi

    krs_ref[...] = rot(krs)
    for hh in range(N_HEADS):
        q_ref[:, LANE * hh:LANE * (hh + 1)] = rot(q[:, LANE * hh:LANE * (hh + 1)]).astype(BF16)

    pos = lax.broadcasted_iota(jnp.int32, (TM_IN, 1), 0) & (seq - 1)
    first = pos == 0
    last = pos == seq - 1
    for j in range(D_CONV // CONV_CHUNK):
        c0 = j * CONV_CHUNK
        bg = _dot_nt(hb, wc_ref[c0:c0 + CONV_CHUNK, :].astype(BF16))
        cg = _dot_nt(hb, wc_ref[D_CONV + c0:D_CONV + c0 + CONV_CHUNK, :].astype(BF16))
        ui = _dot_nt(hb, wc_ref[2 * D_CONV + c0:2 * D_CONV + c0 + CONV_CHUNK, :].astype(BF16))
        u = cg * ui
        u_prev = jnp.where(first, 0.0, pltpu.roll(u, 1, 0))
        u_next = jnp.where(last, 0.0, pltpu.roll(u, TM_IN - 1, 0))
        cw = cw_ref[:, c0:c0 + CONV_CHUNK]
        conv = u_prev * cw[0:1] + u * cw[1:2] + u_next * cw[2:3]
        zc_ref[:, c0:c0 + CONV_CHUNK] = (bg * conv).astype(BF16)


def _inproj(xp, xs, mod, norm1, w_in_t, conv_w, q_norm, kv_norm, w_uq_slot, rope_tabs):
    n_prompt = T_P // TM_IN
    const = lambda i: (0, 0)
    pmap, smap = _stream_maps(TM_IN)
    once = pl.Buffered(1)
    return pl.pallas_call(
        _inproj_kernel,
        grid=(T // TM_IN,),
        in_specs=[
            pl.BlockSpec((TM_IN, D_MODEL), pmap),
            pl.BlockSpec((TM_IN, D_MODEL), smap),
            pl.BlockSpec((N_COND, 6 * D_MODEL), const),
            pl.BlockSpec((1, D_MODEL), const),
            pl.BlockSpec((O_CQ, D_MODEL), const, pipeline_mode=once),
            pl.BlockSpec((SMALL_BLK, D_MODEL), lambda i: (O_CQ // SMALL_BLK, 0), pipeline_mode=once),
            pl.BlockSpec((3, D_CONV), const),
            pl.BlockSpec((1, Q_LORA), const),
            pl.BlockSpec((1, KV_LORA), const),
            pl.BlockSpec((Q_LORA, N_HEADS * LANE), const),
            pl.BlockSpec((None, 3, TM_IN, LANE),
                         lambda i: (jnp.where(i >= n_prompt, 1, 0), 0, 0, 0)),
        ],
        out_specs=[
            pl.BlockSpec((TM_IN, D_CONV), lambda i: (i, 0)),
            pl.BlockSpec((TM_IN, N_HEADS * LANE), lambda i: (i, 0)),
            pl.BlockSpec((TM_IN, KV_LORA), lambda i: (i, 0)),
            pl.BlockSpec((TM_IN, LANE), lambda i: (i, 0)),
            pl.BlockSpec((TM_IN, KV_LORA), pmap),
            pl.BlockSpec((TM_IN // SEQ, QK_ROPE, SEQ), lambda i: (jnp.minimum(i, n_prompt - 1), 0, 0)),
        ],
        out_shape=[
            jax.ShapeDtypeStruct((T, D_CONV), BF16),
            jax.ShapeDtypeStruct((T, N_HEADS * LANE), BF16),
            jax.ShapeDtypeStruct((T, KV_LORA), F32),
            jax.ShapeDtypeStruct((T, LANE), F32),
            jax.ShapeDtypeStruct((T_P, KV_LORA), F32),
            jax.ShapeDtypeStruct((BATCH, QK_ROPE, SEQ), F32),
        ],
        compiler_params=pltpu.CompilerParams(
            dimension_semantics=("arbitrary",), vmem_limit_bytes=VMEM_LIMIT),
        name="in_proj",
    )(xp, xs, mod, norm1, w_in_t, w_in_t, conv_w, q_norm, kv_norm, w_uq_slot, rope_tabs)


def _fill_kv(ckv, krs, wukv_ref, kf_scr, v_scr, off):
    m = ckv.shape[0]
    kv = _dot(ckv.astype(BF16), wukv_ref[...])
    for hh in range(N_HEADS):
        kf_scr[hh, off:off + m, :] = (kv[:, LANE * hh:LANE * (hh + 1)] + krs).astype(BF16)
    v_scr[off:off + m, :] = kv[:, N_HEADS * LANE:].astype(BF16)


def _attend(q_ref, r0, rows, kf_scr, v_scr, o_ref):
    for pair in range(N_HEADS // 2):
        acc = None
        for hh in (2 * pair, 2 * pair + 1):
            qh = q_ref[r0:r0 + rows, LANE * hh:LANE * (hh + 1)]
            s = _dot_nt(qh, kf_scr[hh]) * ATTN_SCALE
            e = jnp.exp(s - jnp.max(s, axis=-1, keepdims=True))
            p = (e / jnp.sum(e, axis=-1, keepdims=True)).astype(BF16)
            part = _dot(p, v_scr[:, LANE * hh:LANE * (hh + 1)])
            acc = part if acc is None else acc + part
        o_ref[r0:r0 + rows, LANE * pair:LANE * (pair + 1)] = acc.astype(BF16)


def _attn_prompt_kernel(q_ref, ckv_ref, krs_ref, wukv_ref, o_ref):
    for s in range(PROMPT_SEQS):
        r0 = s * SEQ
        kv = _dot(ckv_ref[r0:r0 + SEQ, :].astype(BF16), wukv_ref[...])
        krs = krs_ref[r0:r0 + SEQ, :]
        for pair in range(N_HEADS // 2):
            acc = None
            for hh in (2 * pair, 2 * pair + 1):
                kf = (kv[:, LANE * hh:LANE * (hh + 1)] + krs).astype(BF16)
                vh = kv[:, LANE * (N_HEADS + hh):LANE * (N_HEADS + hh + 1)].astype(BF16)
                sc = _dot_nt(q_ref[r0:r0 + SEQ, LANE * hh:LANE * (hh + 1)], kf) * ATTN_SCALE
                e = jnp.exp(sc - jnp.max(sc, axis=-1, keepdims=True))
                p = (e / jnp.sum(e, axis=-1, keepdims=True)).astype(BF16)
                part = _dot(p, vh)
                acc = part if acc is None else acc + part
            o_ref[r0:r0 + SEQ, LANE * pair:LANE * (pair + 1)] = acc.astype(BF16)


def _attn_sample_kernel(q_ref, ckv_ref, cckv_ref, krs_ref, ckrs_ref, wukv_ref, wg_ref, wco_ref, wo_ref,
                        wmix_ref, o_ref, g_ref, co_ref, ob_ref, mix_ref, kf_scr, v_scr):
    @pl.when(pl.program_id(1) == 0)
    def _():
        _fill_kv(ckv_ref[...], krs_ref[...], wukv_ref, kf_scr, v_scr, 0)
        _fill_kv(cckv_ref[...], ckrs_ref[...], wukv_ref, kf_scr, v_scr, DEC_SEQ)

    _attend(q_ref, 0, Q_BLK_S, kf_scr, v_scr, o_ref)
    g_ref[...] = wg_ref[...].astype(BF16)
    co_ref[...] = wco_ref[...].astype(BF16)
    ob_ref[...] = wo_ref[...].astype(BF16)
    mix_ref[...] = wmix_ref[...].astype(BF16)


def _attention(q, ckv, krs, cache_ckv, cache_krs, w_ukv_slot, w_in_t, w_conv_out, w_o, w_mix_out):
    kv_cols = 2 * N_HEADS * LANE
    n_o = N_HEADS * V_HEAD
    steps = BATCH // PROMPT_SEQS
    rows = PROMPT_SEQS * SEQ
    o_prompt = pl.pallas_call(
        _attn_prompt_kernel,
        grid=(steps,),
        in_specs=[
            pl.BlockSpec((rows, N_HEADS * LANE), lambda b: (b, 0)),
            pl.BlockSpec((rows, KV_LORA), lambda b: (b, 0)),
            pl.BlockSpec((rows, LANE), lambda b: (b, 0)),
            pl.BlockSpec((KV_LORA, kv_cols), lambda b: (0, 0)),
        ],
        out_specs=pl.BlockSpec((rows, n_o), lambda b: (b, 0)),
        out_shape=jax.ShapeDtypeStruct((T_P, n_o), BF16),
        compiler_params=pltpu.CompilerParams(dimension_semantics=("parallel",),
                                             vmem_limit_bytes=VMEM_LIMIT),
        name="attn_prompt",
    )(q, ckv, krs, w_ukv_slot)

    m_all = DEC_SEQ + PAST_LEN
    nq = DEC_SEQ // Q_BLK_S
    q0 = T_P // Q_BLK_S
    s0 = T_P // DEC_SEQ
    n_step = DEC_BATCH * nq
    share = lambda n: pl.BlockSpec((n // n_step, D_MODEL), lambda b, j: (b * nq + j, 0))
    gate_rows = 2 * D_MODEL // n_step
    o_sample, w_gates_t, w_co_b, w_o_b, w_mix_b = pl.pallas_call(
        _attn_sample_kernel,
        grid=(DEC_BATCH, nq),
        in_specs=[
            pl.BlockSpec((Q_BLK_S, N_HEADS * LANE), lambda b, j: (q0 + b * nq + j, 0)),
            pl.BlockSpec((DEC_SEQ, KV_LORA), lambda b, j: (s0 + b, 0)),
            pl.BlockSpec((None, PAST_LEN, KV_LORA), lambda b, j: (b, 0, 0)),
            pl.BlockSpec((DEC_SEQ, LANE), lambda b, j: (s0 + b, 0)),
            pl.BlockSpec((None, PAST_LEN, LANE), lambda b, j: (b, 0, 0)),
            pl.BlockSpec((KV_LORA, kv_cols), lambda b, j: (0, 0)),
            pl.BlockSpec((pl.Element(gate_rows), pl.Element(D_MODEL)),
                         lambda b, j: (pl.multiple_of(O_GATE + (b * nq + j) * gate_rows, 32), 0)),
            share(D_CONV), share(n_o), share(D_MODEL),
        ],
        out_specs=[pl.BlockSpec((Q_BLK_S, n_o), lambda b, j: (b * nq + j, 0)),
                   share(2 * D_MODEL), share(D_CONV), share(n_o), share(D_MODEL)],
        out_shape=[jax.ShapeDtypeStruct((T_S, n_o), BF16),
                   jax.ShapeDtypeStruct((2 * D_MODEL, D_MODEL), BF16),
                   jax.ShapeDtypeStruct((D_CONV, D_MODEL), BF16),
                   jax.ShapeDtypeStruct((n_o, D_MODEL), BF16),
                   jax.ShapeDtypeStruct((D_MODEL, D_MODEL), BF16)],
        scratch_shapes=[pltpu.VMEM((N_HEADS, m_all, LANE), BF16),
                        pltpu.VMEM((m_all, N_HEADS * LANE), BF16)],
        compiler_params=pltpu.CompilerParams(dimension_semantics=("arbitrary", "arbitrary"),
                                             vmem_limit_bytes=VMEM_LIMIT),
        name="attn_sample",
    )(q, ckv, cache_ckv, krs, cache_krs, w_ukv_slot, w_in_t, w_conv_out, w_o, w_mix_out)
    return o_prompt, o_sample, w_gates_t, w_co_b, w_o_b, w_mix_b


def _route(logits):
    lane = lax.broadcasted_iota(jnp.int32, logits.shape, 1)
    neg = -jnp.inf
    big = jnp.int32(1 << 20)
    gmask = (lane >= N_EXPERTS) & (lane < N_EXPERTS + N_GROUPS)
    gl = jnp.where(gmask, logits, neg)
    gmax = jnp.max(gl, axis=-1, keepdims=True)
    gsum = jnp.sum(jnp.where(gmask, jnp.exp(gl - gmax), 0.0), axis=-1, keepdims=True)
    p_g = 1.0 / gsum
    g_idx = jnp.min(jnp.where(gl == gmax, lane, big), axis=-1, keepdims=True) - N_EXPERTS

    emask = (lane < N_EXPERTS) & ((lane >> 3) == g_idx)
    el = jnp.where(emask, logits, neg)
    m1 = jnp.max(el, axis=-1, keepdims=True)
    i1 = jnp.min(jnp.where(el == m1, lane, big), axis=-1, keepdims=True)
    el2 = jnp.where(lane == i1, neg, el)
    m2 = jnp.max(el2, axis=-1, keepdims=True)
    i2 = jnp.min(jnp.where(el2 == m2, lane, big), axis=-1, keepdims=True)
    z = jnp.sum(jnp.where(emask, jnp.exp(el - m1), 0.0), axis=-1, keepdims=True)
    p1 = 1.0 / z
    p2 = jnp.exp(m2 - m1) / z
    tot = p1 + p2
    w1 = p_g * p1 / tot
    w2 = p_g * p2 / tot
    return jnp.where(lane == i1, w1, 0.0) + jnp.where(lane == i2, w2, 0.0), g_idx


def _post_kernel(xp_ref, xs_ref, mod_ref, n1_ref, wg_ref, zc_ref, op_ref, os_ref, wco_ref, wo_ref,
                 wmix_ref, n2_ref, wr_ref, x1_ref, h3_ref, meta_ref, cnt_ref, gr_ref):
    i = pl.program_id(0)
    is_sample = i >= T_P // TM_POST
    x = jnp.where(is_sample, xs_ref[...], xp_ref[...])
    o = jnp.where(is_sample, os_ref[...], op_ref[...])
    mod = mod_ref[pl.ds(_mod_row(i, TM_POST), 1), :]
    shift1 = mod[:, 0:D_MODEL]
    scale1 = mod[:, D_MODEL:2 * D_MODEL]
    gate1 = mod[:, 2 * D_MODEL:3 * D_MODEL]
    shift2 = mod[:, 3 * D_MODEL:4 * D_MODEL]
    scale2 = mod[:, 4 * D_MODEL:5 * D_MODEL]
    y_conv = _dot(zc_ref[...], wco_ref[...])
    y_mla = _dot(o, wo_ref[...])
    h = ((x * _rms(x)) * n1_ref[...]) * (1.0 + scale1) + shift1
    g = _dot_nt(h.astype(BF16), wg_ref[...])
    merged = (jax.nn.sigmoid(g[:, 0:D_MODEL]) * y_conv
              + jax.nn.sigmoid(g[:, D_MODEL:2 * D_MODEL]) * y_mla)
    y = _dot(merged.astype(BF16), wmix_ref[...])
    x1 = x + gate1 * y
    x1_ref[...] = x1
    h2 = ((x1 * _rms(x1)) * n2_ref[...]) * (1.0 + scale2) + shift2
    h2_hi = h2.astype(BF16)
    h2_lo = (h2 - h2_hi.astype(F32)).astype(BF16)
    hh = _dot(h2_hi, wr_ref[...])
    logits = hh[:, 0:LANE] + hh[:, LANE:2 * LANE] + _dot(h2_lo, wr_ref[:, 0:LANE])
    comb, g_idx = _route(logits)

    lane = lax.broadcasted_iota(jnp.int32, comb.shape, 1)
    onehot = lane == g_idx + N_EXPERTS
    r_i = lax.broadcasted_iota(jnp.int32, (TM_POST, TM_POST), 0)
    c_i = lax.broadcasted_iota(jnp.int32, (TM_POST, TM_POST), 1)
    lower = jnp.where(c_i < r_i, 1.0, 0.0).astype(BF16)
    before = _dot(lower, jnp.where(onehot, 1.0, 0.0).astype(BF16))
    rank = jnp.sum(jnp.where(onehot, before, 0.0), axis=-1, keepdims=True)
    counts = jnp.sum(jnp.where(onehot, 1.0, 0.0), axis=0, keepdims=True)
    cnt_ref[...] = jnp.broadcast_to(counts, cnt_ref.shape)

    meta_ref[...] = comb
    idx = jnp.where(lane == GID_LANE, g_idx.astype(F32), 0.0) + jnp.where(lane == RANK_LANE, rank, 0.0)
    idx_hi = idx.astype(BF16)
    idx_lo = (idx - idx_hi.astype(F32)).astype(BF16)
    s_row = lax.broadcasted_iota(jnp.int32, (8, LANE), 0)
    s_lane = lax.broadcasted_iota(jnp.int32, (8, LANE), 1)
    sel = jnp.where(s_lane == GID_LANE + s_row, 1.0, 0.0).astype(BF16)
    gr_ref[...] = _dot_nt(sel, idx_hi) + _dot_nt(sel, idx_lo)
    for c in range(N_SLAB):
        h3_ref[pl.ds(c, TM_POST, stride=N_SLAB), :] = h2[:, LANE * c:LANE * (c + 1)]


def _post(xp, xs, mod, norm1, w_gates_t, zc, o_p, o_s, w_conv_out, w_o, w_mix_out, norm2, w_route):
    const = lambda i: (0, 0)
    row = lambda i: (i, 0)
    pmap, smap = _stream_maps(TM_POST)
    return pl.pallas_call(
        _post_kernel,
        grid=(T // TM_POST,),
        in_specs=[
            pl.BlockSpec((TM_POST, D_MODEL), pmap),
            pl.BlockSpec((TM_POST, D_MODEL), smap),
            pl.BlockSpec((N_COND, 6 * D_MODEL), const),
            pl.BlockSpec((1, D_MODEL), const),
            pl.BlockSpec((2 * D_MODEL, D_MODEL), const),
            pl.BlockSpec((TM_POST, D_CONV), row),
            pl.BlockSpec((TM_POST, N_HEADS * V_HEAD), pmap),
            pl.BlockSpec((TM_POST, N_HEADS * V_HEAD), smap),
            pl.BlockSpec((D_CONV, D_MODEL), const),
            pl.BlockSpec((N_HEADS * V_HEAD, D_MODEL), const),
            pl.BlockSpec((D_MODEL, D_MODEL), const),
            pl.BlockSpec((1, D_MODEL), const),
            pl.BlockSpec((D_MODEL, 2 * LANE), const),
        ],
        out_specs=[
            pl.BlockSpec((TM_POST, D_MODEL), row),
            pl.BlockSpec((TM_POST * N_SLAB, LANE), row),
            pl.BlockSpec((TM_POST, LANE), row),
            pl.BlockSpec((None, 8, LANE), lambda i: (i, 0, 0)),
            pl.BlockSpec((8, TM_POST), lambda i: (0, i)),
        ],
        out_shape=[
            jax.ShapeDtypeStruct((T, D_MODEL), F32),
            jax.ShapeDtypeStruct((T * N_SLAB, LANE), F32),
            jax.ShapeDtypeStruct((T, LANE), F32),
            jax.ShapeDtypeStruct((T // TM_POST, 8, LANE), F32),
            jax.ShapeDtypeStruct((8, T), F32),
        ],
        compiler_params=pltpu.CompilerParams(
            dimension_semantics=("parallel",), vmem_limit_bytes=VMEM_LIMIT),
        name="post_mixer",
    )(xp, xs, mod, norm1, w_gates_t, zc, o_p, o_s, w_conv_out, w_o, w_mix_out, norm2, w_route)


DISP_IN = T // TM_DISP
DISP_OUT = T // TM_MOE


def _dispatch_kernel(pos_ref, h3_ref, m_ref, hs_ref, ms_ref, xs_ref, mss_ref):
    i = pl.program_id(0)

    @pl.when(i < DISP_IN)
    def _():
        def body(r, carry):
            p = pos_ref[0, r]
            xs_ref[_slab(p), :] = h3_ref[_slab(r), :]
            mss_ref[pl.ds(p, 1), :] = m_ref[pl.ds(r, 1), :]
            return carry

        lax.fori_loop(0, TM_DISP, body, 0, unroll=8)

    @pl.when(i >= DISP_IN)
    def _():
        row0 = pl.multiple_of((i - DISP_IN) * TM_MOE, TM_MOE)
        for c in range(N_SLAB):
            hs_ref[:, LANE * c:LANE * (c + 1)] = (
                xs_ref[pl.ds(row0 * N_SLAB + c, TM_MOE, stride=N_SLAB), :].astype(BF16))
        ms_ref[...] = mss_ref[pl.ds(row0, TM_MOE), :]


def _dispatch(pos, h3, meta):
    n_slab = N_SLAB
    in_map = lambda i: (jnp.minimum(i, DISP_IN - 1), 0)
    out_map = lambda i: (jnp.maximum(i - DISP_IN, 0), 0)
    return pl.pallas_call(
        _dispatch_kernel,
        grid=(DISP_IN + DISP_OUT,),
        in_specs=[pl.BlockSpec((None, 1, TM_DISP), lambda i: (jnp.minimum(i, DISP_IN - 1), 0, 0),
                               memory_space=pltpu.SMEM),
                  pl.BlockSpec((TM_DISP * n_slab, LANE), in_map),
                  pl.BlockSpec((TM_DISP, LANE), in_map)],
        out_specs=[pl.BlockSpec((TM_MOE, D_MODEL), out_map),
                   pl.BlockSpec((TM_MOE, LANE), out_map)],
        scratch_shapes=[pltpu.VMEM((T * n_slab, LANE), F32),
                        pltpu.VMEM((T, LANE), F32)],
        out_shape=[jax.ShapeDtypeStruct((T, D_MODEL), BF16),
                   jax.ShapeDtypeStruct((T, LANE), F32)],
        compiler_params=pltpu.CompilerParams(
            dimension_semantics=("arbitrary",), vmem_limit_bytes=VMEM_LIMIT),
        name="moe_dispatch",
    )(pos.reshape(DISP_IN, 1, TM_DISP), h3, meta)


def _moe_kernel(sched_ref, hs_ref, ms_ref, wup_ref, wgate_ref, wdown_ref, y3_ref, acc_ref):
    v = pl.program_id(0)
    j = pl.program_id(1)
    valid = sched_ref[V_VALID, v] == 1
    lo = sched_ref[V_LO, v]
    hi = sched_ref[V_HI, v]
    e0 = sched_ref[V_GROUP, v] * EXP_PER_GROUP + j * MOE_EPS
    full = (hi - lo) * 4 >= TM_MOE * 3

    @pl.when(valid & (j == 0) & (sched_ref[V_FIRST, v] == 1))
    def _():
        acc_ref[...] = jnp.zeros_like(acc_ref)

    def expert_rows(r0, rows):
        w_in2 = jnp.concatenate(
            [w[k].astype(BF16) for k in range(MOE_EPS) for w in (wup_ref, wgate_ref)], axis=1)
        ag = _dot(hs_ref[pl.ds(r0, rows), :], w_in2)
        comb = ms_ref[pl.ds(r0, rows), :]
        lane = lax.broadcasted_iota(jnp.int32, comb.shape, 1)
        acts = []
        for k in range(MOE_EPS):
            a = ag[:, 2 * k * D_EXPERT:(2 * k + 1) * D_EXPERT]
            g = ag[:, (2 * k + 1) * D_EXPERT:(2 * k + 2) * D_EXPERT]
            cw = jnp.sum(jnp.where(lane == e0 + k, comb, 0.0), axis=-1, keepdims=True)
            acts.append(((g * jax.nn.sigmoid(g)) * a * cw).astype(BF16))
        w_out = jnp.concatenate([wdown_ref[k].astype(BF16) for k in range(MOE_EPS)], axis=0)
        acc_ref[pl.ds(r0, rows), :] += _dot(jnp.concatenate(acts, axis=1), w_out)

    @pl.when(valid & full)
    def _():
        expert_rows(0, TM_MOE)

    @pl.when(valid & jnp.logical_not(full))
    def _():
        def sub_block(s, carry):
            expert_rows(pl.multiple_of(s * MOE_SUB, MOE_SUB), MOE_SUB)
            return carry

        lax.fori_loop(lo // MOE_SUB, (hi + MOE_SUB - 1) // MOE_SUB, sub_block, 0)

    @pl.when(valid & (j == EXP_PER_GROUP // MOE_EPS - 1) & (sched_ref[V_LAST, v] == 1))
    def _():
        for c in range(N_SLAB):
            y3_ref[pl.ds(c, TM_MOE, stride=N_SLAB), :] = acc_ref[:, LANE * c:LANE * (c + 1)]


def _moe(sched, hs, ms, w_up, w_gate, w_down):
    steps = EXP_PER_GROUP // MOE_EPS
    wmap = lambda v, j, sched: (
        sched[V_GROUP, v] * steps + jnp.where(sched[V_VALID, v] == 1, j, steps - 1), 0, 0)
    tmap = lambda v, j, sched: (sched[V_TILE, v], 0)
    n_slab = D_MODEL // LANE
    return pl.pallas_call(
        _moe_kernel,
        grid_spec=pltpu.PrefetchScalarGridSpec(
            num_scalar_prefetch=1,
            grid=(N_VISITS, steps),
            in_specs=[
                pl.BlockSpec((TM_MOE, D_MODEL), tmap),
                pl.BlockSpec((TM_MOE, LANE), tmap),
                pl.BlockSpec((MOE_EPS, D_MODEL, D_EXPERT), wmap),
                pl.BlockSpec((MOE_EPS, D_MODEL, D_EXPERT), wmap),
                pl.BlockSpec((MOE_EPS, D_EXPERT, D_MODEL), wmap),
            ],
            out_specs=pl.BlockSpec((TM_MOE * n_slab, LANE), tmap),
            scratch_shapes=[pltpu.VMEM((TM_MOE, D_MODEL), F32)],
        ),
        out_shape=jax.ShapeDtypeStruct((T * n_slab, LANE), F32),
        compiler_params=pltpu.CompilerParams(
            dimension_semantics=("arbitrary", "arbitrary"), vmem_limit_bytes=VMEM_LIMIT),
        name="moe_grouped",
    )(sched, hs, ms, w_up, w_gate, w_down)


def _final_kernel(pos_ref, ys_ref, x1_ref, mod_ref, fn_ref, yp_ref, ysm_ref, g_ref):
    i = pl.program_id(0)

    def body(r, carry):
        g_ref[_slab(r), :] = ys_ref[_slab(pos_ref[0, r]), :]
        return carry

    lax.fori_loop(0, TM_FINAL, body, 0, unroll=8)
    mod = mod_ref[pl.ds(_mod_row(i, TM_FINAL), 1), :]
    gate2 = mod[:, 5 * D_MODEL:6 * D_MODEL]
    moe = jnp.concatenate([g_ref[pl.ds(c, TM_FINAL, stride=N_SLAB), :] for c in range(N_SLAB)], axis=1)
    x2 = x1_ref[...] + gate2 * moe
    y = (x2 * _rms(x2)) * fn_ref[...]
    is_sample = i >= T_P // TM_FINAL

    @pl.when(jnp.logical_not(is_sample))
    def _():
        yp_ref[...] = y

    @pl.when(is_sample)
    def _():
        ysm_ref[...] = y


def _final(pos, ys, x1, mod, final_norm):
    n_slab = D_MODEL // LANE
    pmap, smap = _stream_maps(TM_FINAL)
    return pl.pallas_call(
        _final_kernel,
        grid=(T // TM_FINAL,),
        in_specs=[
            pl.BlockSpec((None, 1, TM_FINAL), lambda i: (i, 0, 0), memory_space=pltpu.SMEM),
            pl.BlockSpec((T * n_slab, LANE), lambda i: (0, 0), pipeline_mode=pl.Buffered(1)),
            pl.BlockSpec((TM_FINAL, D_MODEL), lambda i: (i, 0)),
            pl.BlockSpec((N_COND, 6 * D_MODEL), lambda i: (0, 0)),
            pl.BlockSpec((1, D_MODEL), lambda i: (0, 0)),
        ],
        out_specs=[pl.BlockSpec((TM_FINAL, D_MODEL), pmap),
                   pl.BlockSpec((TM_FINAL, D_MODEL), smap)],
        scratch_shapes=[pltpu.VMEM((TM_FINAL * n_slab, LANE), F32)],
        out_shape=[jax.ShapeDtypeStruct((T_P, D_MODEL), F32),
                   jax.ShapeDtypeStruct((T_S, D_MODEL), F32)],
        compiler_params=pltpu.CompilerParams(
            dimension_semantics=("arbitrary",), vmem_limit_bytes=VMEM_LIMIT),
        name="moe_unsort_final",
    )(pos.reshape(T // TM_FINAL, 1, TM_FINAL), ys, x1, mod, final_norm)


def _plan_kernel(gr_ref, cnt_ref, pos_ref, sched_ref):
    n_post = T // TM_POST
    lane = lax.broadcasted_iota(jnp.int32, (1, LANE), 1)
    grp_lane = lambda v, g: v[:, N_EXPERTS + g:N_EXPERTS + g + 1]
    counts = [cnt_ref[k, 0:1, :] for k in range(n_post)]
    gtot_v = counts[0]
    for k in range(1, n_post):
        gtot_v = gtot_v + counts[k]
    gtot = [grp_lane(gtot_v, g) for g in range(N_GROUPS)]
    goff = [jnp.zeros((1, 1), F32)]
    for g in range(1, N_GROUPS):
        goff.append(goff[-1] + gtot[g - 1])

    before = [jnp.zeros((1, 1), F32) for _ in range(N_GROUPS)]
    rows_per_tile = TM_POST // LANE
    for k in range(n_post):
        gid = gr_ref[0:1, TM_POST * k:TM_POST * (k + 1)]
        pos = gr_ref[1:2, TM_POST * k:TM_POST * (k + 1)]
        for g in range(N_GROUPS):
            pos = pos + jnp.where(gid == float(g), goff[g] + before[g], 0.0)
            before[g] = before[g] + grp_lane(counts[k], g)
        for r in range(rows_per_tile):
            pos_ref[rows_per_tile * k + r:rows_per_tile * k + r + 1, :] = (
                pos[:, LANE * r:LANE * (r + 1)].astype(jnp.int32))

    zero = jnp.zeros((1, LANE), F32)
    rows = {name: zero for name in ("vt", "vg", "vlo", "vhi")}
    slot = jnp.zeros((1, 1), F32)
    last_t = jnp.zeros((1, 1), F32)
    last_g = jnp.zeros((1, 1), F32)
    for i in range(T // TM_MOE):
        for g in range(N_GROUPS):
            lo = jnp.clip(goff[g] - float(TM_MOE * i), 0.0, float(TM_MOE))
            hi = jnp.clip(goff[g] + gtot[g] - float(TM_MOE * i), 0.0, float(TM_MOE))
            ok = hi > lo
            here = ok & (lane == slot.astype(jnp.int32))
            rows["vt"] = jnp.where(here, float(i), rows["vt"])
            rows["vg"] = jnp.where(here, float(g), rows["vg"])
            rows["vlo"] = jnp.where(here, lo, rows["vlo"])
            rows["vhi"] = jnp.where(here, hi, rows["vhi"])
            last_t = jnp.where(ok, float(i), last_t)
            last_g = jnp.where(ok, float(g), last_g)
            slot = slot + jnp.where(ok, 1.0, 0.0)
    n_ok = slot.astype(jnp.int32)
    valid = lane < n_ok
    vt = jnp.where(valid, rows["vt"], last_t)
    vg = jnp.where(valid, rows["vg"], last_g)
    prev_t = pltpu.roll(jnp.broadcast_to(vt, (8, LANE)), 1, 1)[0:1, :]
    next_t = pltpu.roll(jnp.broadcast_to(vt, (8, LANE)), LANE - 1, 1)[0:1, :]
    first = jnp.where((lane == 0) | (vt != prev_t), 1.0, 0.0)
    last = jnp.where((lane == n_ok - 1) | (vt != next_t), 1.0, 0.0)
    table = [vt, vg, rows["vlo"], rows["vhi"], first, last, jnp.where(valid, 1.0, 0.0), zero]
    for r, row in enumerate(table):
        sched_ref[r:r + 1, :] = row.astype(jnp.int32)


V_TILE, V_GROUP, V_LO, V_HI, V_FIRST, V_LAST, V_VALID = range(7)


def _plan(gr, cnt):
    pos, sched = pl.pallas_call(
        _plan_kernel,
        out_shape=[jax.ShapeDtypeStruct((T // LANE, LANE), jnp.int32),
                   jax.ShapeDtypeStruct((8, LANE), jnp.int32)],
        name="moe_plan",
    )(gr, cnt)
    return pos.reshape(T), sched


def _rope_tables():
    n = np.arange(DEC_SEQ)
    pos = np.stack([n // GRID_W, n % GRID_W], axis=1).astype(np.float32)
    half = ROPE_AXIS // 2
    inv = (1.0 / (ROPE_BASE ** (np.arange(0, ROPE_AXIS, 2, dtype=np.float32) / ROPE_AXIS))).astype(np.float32)
    ang = (pos[:, :, None] * inv[None, None, :]).astype(np.float32)
    cos = np.cos(ang).astype(np.float32)
    sin = np.sin(ang).astype(np.float32)
    tabs = np.zeros((2, 3, DEC_SEQ, LANE), np.float32)
    tabs[:, 0] = 1.0
    for a in range(2):
        lo = ROPE_LANE0 + a * ROPE_AXIS
        tabs[1, 0, :, lo:lo + half] = cos[:, a]
        tabs[1, 0, :, lo + half:lo + 2 * half] = cos[:, a]
        tabs[1, 1, :, lo + half:lo + 2 * half] = sin[:, a]
        tabs[1, 2, :, lo:lo + half] = -sin[:, a]
    return jnp.asarray(tabs)


def kernel(x_prompt, x_sample, cache_ckv, cache_krope, c, c_ctx, norm1, w_ada, b_ada, w_in, conv_w,
           w_conv_out, q_norm, w_uq, kv_norm, w_ukv, w_o, w_mix_out, norm2, w_grp, w_exp, w_up,
           w_gate, w_down, final_norm):
    l = 0
    xp = x_prompt.reshape(T_P, D_MODEL)
    xs = x_sample.reshape(T_S, D_MODEL)
    mod = _ada(c_ctx[None, :], c, w_ada[l], b_ada[l][None, :])

    w_in_t = w_in[l].T
    w_uq_slot = jnp.pad(w_uq[l].reshape(Q_LORA, N_HEADS, QK_NOPE + QK_ROPE),
                        ((0, 0), (0, 0), (0, LANE - QK_NOPE - QK_ROPE))
                        ).reshape(Q_LORA, N_HEADS * LANE).astype(BF16)
    wkv = w_ukv[l].reshape(KV_LORA, N_HEADS, QK_NOPE + V_HEAD)
    wk_slot = jnp.pad(wkv[:, :, :QK_NOPE], ((0, 0), (0, 0), (0, LANE - QK_NOPE)))
    wv = wkv[:, :, QK_NOPE:].reshape(KV_LORA, N_HEADS // 2, 2, V_HEAD)
    zero = jnp.zeros_like(wv[:, :, 0])
    wv_slot = jnp.stack([jnp.concatenate([wv[:, :, 0], zero], axis=-1),
                         jnp.concatenate([zero, wv[:, :, 1]], axis=-1)], axis=2)
    w_ukv_slot = jnp.concatenate([wk_slot.reshape(KV_LORA, N_HEADS * LANE),
                                  wv_slot.reshape(KV_LORA, N_HEADS * LANE)], axis=1).astype(BF16)
    w_route = jnp.pad(jnp.concatenate([w_exp[l], w_grp[l]], axis=1),
                      ((0, 0), (0, LANE - N_EXPERTS - N_GROUPS)))
    w_route_hi = w_route.astype(BF16)
    w_route_lo = (w_route - w_route_hi.astype(F32)).astype(BF16)
    w_route2 = jnp.concatenate([w_route_hi, w_route_lo], axis=1)
    cache_krs = jnp.pad(cache_krope[:, l], ((0, 0), (0, 0), (ROPE_LANE0, LANE - ROPE_LANE0 - QK_ROPE)))

    zc, q, ckv, krs, nckv, nkr = _inproj(xp, xs, mod, norm1[l][None, :], w_in_t, conv_w[l],
                              q_norm[l][None, :], kv_norm[l][None, :], w_uq_slot, _rope_tables())
    o_p, o_s, w_gates_t, w_co_b, w_o_b, w_mix_b = _attention(
        q, ckv, krs, cache_ckv[:, l], cache_krs, w_ukv_slot, w_in_t, w_conv_out[l], w_o[l], w_mix_out[l])
    x1, h3, meta, cnt, gr = _post(xp, xs, mod, norm1[l][None, :], w_gates_t, zc, o_p, o_s, w_co_b, w_o_b,
                                  w_mix_b, norm2[l][None, :], w_route2)
    pos, sched = _plan(gr, cnt)
    hs, ms = _dispatch(pos, h3, meta)
    ys = _moe(sched, hs, ms, w_up[l], w_gate[l], w_down[l])
    yp, ysm = _final(pos, ys, x1, mod, final_norm[None, :])

    y_prompt = yp.reshape(BATCH, SEQ, D_MODEL)
    y_sample = ysm.reshape(DEC_BATCH, DEC_SEQ, D_MODEL)
    new_ckv = nckv.reshape(BATCH, 1, SEQ, KV_LORA)
    new_krope = jnp.swapaxes(nkr, 1, 2).reshape(BATCH, 1, SEQ, QK_ROPE)
    return (y_prompt, y_sample, new_ckv, new_krope)
```

```python
import numpy as np
import jax
import jax.numpy as jnp
from jax import lax
from jax.experimental import pallas as pl
from jax.experimental.pallas import tpu as pltpu

F32 = jnp.float32
BF16 = jnp.bfloat16

D_MODEL = 1024
BATCH = 16
SEQ = 256
DEC_BATCH = 2
DEC_SEQ = 1024
PAST_LEN = 256
GRID_W = 64
N_HEADS = 8
QK_NOPE = 64
QK_ROPE = 32
V_HEAD = 64
Q_LORA = 256
KV_LORA = 128
ROPE_AXIS = QK_ROPE // 2
ROPE_BASE = 10000.0
ATTN_SCALE = (QK_NOPE + QK_ROPE) ** -0.5
D_CONV = D_MODEL
N_GROUPS = 4
EXP_PER_GROUP = 8
N_EXPERTS = N_GROUPS * EXP_PER_GROUP
D_EXPERT = 256
EPS = 1e-6

T_P = BATCH * SEQ
T_S = DEC_BATCH * DEC_SEQ
T = T_P + T_S
N_COND = 8
LANE = 128
ROPE_LANE0 = QK_NOPE
SMALL_COLS = Q_LORA + KV_LORA + LANE
VMEM_LIMIT = 56 * 1024 * 1024

TM_IN = 1024
TM_POST = 512
TM_MOE = T // N_GROUPS
TM_FINAL = 512
TM_DISP = 1024
N_SLAB = D_MODEL // LANE
MOE_SUB = 256
MOE_EPS = 2
N_VISITS = T // TM_MOE + N_GROUPS - 1
GID_LANE = 40
RANK_LANE = 41
Q_BLK_S = 512
PROMPT_SEQS = 2
CONV_CHUNK = 256


def _dot(a, b):
    return jnp.dot(a, b, preferred_element_type=F32)


def _rms(x):
    return lax.rsqrt(jnp.mean(x * x, axis=-1, keepdims=True) + EPS)


def _slab(t):
    return pl.ds(pl.multiple_of(t * N_SLAB, N_SLAB), N_SLAB)


def _mod_row(i, tm):
    n_prompt = T_P // tm
    return jnp.where(i >= n_prompt, 1 + ((i - n_prompt) * tm) // DEC_SEQ, 0)


def _ada_kernel(cctx_ref, c_ref, w_ref, b_ref, o_ref):
    c = jnp.concatenate([cctx_ref[...], c_ref[...],
                         jnp.zeros((N_COND - 1 - DEC_BATCH, D_MODEL), F32)], axis=0)
    a = (c * jax.nn.sigmoid(c)).astype(BF16)
    o_ref[...] = _dot(a, w_ref[...].astype(BF16)) + b_ref[...]


def _ada(c_ctx, c, w_ada, b_ada):
    n = 6 * D_MODEL
    bn = 1536
    return pl.pallas_call(
        _ada_kernel,
        grid=(n // bn,),
        in_specs=[
            pl.BlockSpec((1, D_MODEL), lambda j: (0, 0)),
            pl.BlockSpec((DEC_BATCH, D_MODEL), lambda j: (0, 0)),
            pl.BlockSpec((D_MODEL, bn), lambda j: (0, j)),
            pl.BlockSpec((1, bn), lambda j: (0, j)),
        ],
        out_specs=pl.BlockSpec((N_COND, bn), lambda j: (0, j)),
        out_shape=jax.ShapeDtypeStruct((N_COND, n), F32),
        compiler_params=pltpu.CompilerParams(
            dimension_semantics=("parallel",), vmem_limit_bytes=VMEM_LIMIT),
        name="ada_mod",
    )(c_ctx, c, w_ada, b_ada)


O_CQ = 3 * D_CONV
O_KR = O_CQ + Q_LORA + KV_LORA
O_GATE = O_KR + QK_ROPE
SMALL_BLK = 512
NT = (((1,), (1,)), ((), ()))


def _dot_nt(a, bt):
    return lax.dot_general(a, bt, NT, preferred_element_type=F32)


def _stream_maps(tm):
    n_prompt = T_P // tm
    return (lambda i, *_: (jnp.minimum(i, n_prompt - 1), 0),
            lambda i, *_: (jnp.maximum(i - n_prompt, 0), 0))


def _inproj_kernel(xp_ref, xs_ref, mod_ref, n1_ref, wc_ref, ws_ref, cw_ref, qn_ref, kvn_ref, wuq_ref,
                   rope_ref, zc_ref, q_ref, ckv_ref, krs_ref, nckv_ref, nkr_ref):
    i = pl.program_id(0)
    is_sample = i >= T_P // TM_IN
    seq = jnp.where(is_sample, DEC_SEQ, SEQ)
    mod = mod_ref[pl.ds(_mod_row(i, TM_IN), 1), :]
    shift1 = mod[:, 0:D_MODEL]
    scale1 = mod[:, D_MODEL:2 * D_MODEL]
    x = jnp.where(is_sample, xs_ref[...], xp_ref[...])
    h = ((x * _rms(x)) * n1_ref[...]) * (1.0 + scale1) + shift1
    hb = h.astype(BF16)

    n_lat = Q_LORA + KV_LORA
    w_small = jnp.concatenate([
        ws_ref[0:n_lat, :].astype(BF16),
        jnp.zeros((ROPE_LANE0, D_MODEL), BF16),
        ws_ref[n_lat:n_lat + QK_ROPE, :].astype(BF16),
        jnp.zeros((LANE - ROPE_LANE0 - QK_ROPE, D_MODEL), BF16)], axis=0)
    sm = _dot_nt(hb, w_small)
    cq = sm[:, 0:Q_LORA]
    ckv_raw = sm[:, Q_LORA:Q_LORA + KV_LORA]
    krs = sm[:, Q_LORA + KV_LORA:SMALL_COLS]
    cqn = (cq * _rms(cq)) * qn_ref[...]
    q = _dot(cqn.astype(BF16), wuq_ref[...])
    ckv = (ckv_raw * _rms(ckv_raw)) * kvn_ref[...]
    ckv_ref[...] = ckv

    @pl.when(jnp.logical_not(is_sample))
    def _():
        nckv_ref[...] = ckv
        for s in range(TM_IN // SEQ):
            kt = krs[s * SEQ:(s + 1) * SEQ, :].T
            nkr_ref[s] = kt[ROPE_LANE0:ROPE_LANE0 + QK_ROPE, :]

    cos = rope_ref[0]
    sin_lo = rope_ref[1]
    sin_hi = rope_ref[2]

    def rot(v):
        return v * cos + pltpu.roll(v, 8, 1) * sin_lo + pltpu.roll(v, LANE - 8, 1) * sin_hi

    krs_ref[...] = rot(krs)
    for hh in range(N_HEADS):
        q_ref[:, LANE * hh:LANE * (hh + 1)] = rot(q[:, LANE * hh:LANE * (hh + 1)]).astype(BF16)

    pos = lax.broadcasted_iota(jnp.int32, (TM_IN, 1), 0) & (seq - 1)
    first = pos == 0
    last = pos == seq - 1
    for j in range(D_CONV // CONV_CHUNK):
        c0 = j * CONV_CHUNK
        bg = _dot_nt(hb, wc_ref[c0:c0 + CONV_CHUNK, :].astype(BF16))
        cg = _dot_nt(hb, wc_ref[D_CONV + c0:D_CONV + c0 + CONV_CHUNK, :].astype(BF16))
        ui = _dot_nt(hb, wc_ref[2 * D_CONV + c0:2 * D_CONV + c0 + CONV_CHUNK, :].astype(BF16))
        u = cg * ui
        u_prev = jnp.where(first, 0.0, pltpu.roll(u, 1, 0))
        u_next = jnp.where(last, 0.0, pltpu.roll(u, TM_IN - 1, 0))
        cw = cw_ref[:, c0:c0 + CONV_CHUNK]
        conv = u_prev * cw[0:1] + u * cw[1:2] + u_next * cw[2:3]
        zc_ref[:, c0:c0 + CONV_CHUNK] = (bg * conv).astype(BF16)


def _inproj(xp, xs, mod, norm1, w_in_t, conv_w, q_norm, kv_norm, w_uq_slot, rope_tabs):
    n_prompt = T_P // TM_IN
    const = lambda i: (0, 0)
    pmap, smap = _stream_maps(TM_IN)
    once = pl.Buffered(1)
    return pl.pallas_call(
        _inproj_kernel,
        grid=(T // TM_IN,),
        in_specs=[
            pl.BlockSpec((TM_IN, D_MODEL), pmap),
            pl.BlockSpec((TM_IN, D_MODEL), smap),
            pl.BlockSpec((N_COND, 6 * D_MODEL), const),
            pl.BlockSpec((1, D_MODEL), const),
            pl.BlockSpec((O_CQ, D_MODEL), const, pipeline_mode=once),
            pl.BlockSpec((SMALL_BLK, D_MODEL), lambda i: (O_CQ // SMALL_BLK, 0), pipeline_mode=once),
            pl.BlockSpec((3, D_CONV), const),
            pl.BlockSpec((1, Q_LORA), const),
            pl.BlockSpec((1, KV_LORA), const),
            pl.BlockSpec((Q_LORA, N_HEADS * LANE), const),
            pl.BlockSpec((None, 3, TM_IN, LANE),
                         lambda i: (jnp.where(i >= n_prompt, 1, 0), 0, 0, 0)),
        ],
        out_specs=[
            pl.BlockSpec((TM_IN, D_CONV), lambda i: (i, 0)),
            pl.BlockSpec((TM_IN, N_HEADS * LANE), lambda i: (i, 0)),
            pl.BlockSpec((TM_IN, KV_LORA), lambda i: (i, 0)),
            pl.BlockSpec((TM_IN, LANE), lambda i: (i, 0)),
            pl.BlockSpec((TM_IN, KV_LORA), pmap),
            pl.BlockSpec((TM_IN // SEQ, QK_ROPE, SEQ), lambda i: (jnp.minimum(i, n_prompt - 1), 0, 0)),
        ],
        out_shape=[
            jax.ShapeDtypeStruct((T, D_CONV), BF16),
            jax.ShapeDtypeStruct((T, N_HEADS * LANE), BF16),
            jax.ShapeDtypeStruct((T, KV_LORA), F32),
            jax.ShapeDtypeStruct((T, LANE), F32),
            jax.ShapeDtypeStruct((T_P, KV_LORA), F32),
            jax.ShapeDtypeStruct((BATCH, QK_ROPE, SEQ), F32),
        ],
        compiler_params=pltpu.CompilerParams(
            dimension_semantics=("arbitrary",), vmem_limit_bytes=VMEM_LIMIT),
        name="in_proj",
    )(xp, xs, mod, norm1, w_in_t, w_in_t, conv_w, q_norm, kv_norm, w_uq_slot, rope_tabs)


def _fill_kv(ckv, krs, wukv_ref, kf_scr, v_scr, off):
    m = ckv.shape[0]
    kv = _dot(ckv.astype(BF16), wukv_ref[...])
    for hh in range(N_HEADS):
        kf_scr[hh, off:off + m, :] = (kv[:, LANE * hh:LANE * (hh + 1)] + krs).astype(BF16)
    v_scr[off:off + m, :] = kv[:, N_HEADS * LANE:].astype(BF16)


def _attend(q_ref, r0, rows, kf_scr, v_scr, o_ref):
    for pair in range(N_HEADS // 2):
        acc = None
        for hh in (2 * pair, 2 * pair + 1):
            qh = q_ref[r0:r0 + rows, LANE * hh:LANE * (hh + 1)]
            s = _dot_nt(qh, kf_scr[hh]) * ATTN_SCALE
            e = jnp.exp(s - jnp.max(s, axis=-1, keepdims=True))
            p = (e / jnp.sum(e, axis=-1, keepdims=True)).astype(BF16)
            part = _dot(p, v_scr[:, LANE * hh:LANE * (hh + 1)])
            acc = part if acc is None else acc + part
        o_ref[r0:r0 + rows, LANE * pair:LANE * (pair + 1)] = acc.astype(BF16)


def _attn_prompt_kernel(q_ref, ckv_ref, krs_ref, wukv_ref, o_ref):
    for s in range(PROMPT_SEQS):
        r0 = s * SEQ
        kv = _dot(ckv_ref[r0:r0 + SEQ, :].astype(BF16), wukv_ref[...])
        krs = krs_ref[r0:r0 + SEQ, :]
        for pair in range(N_HEADS // 2):
            acc = None
            for hh in (2 * pair, 2 * pair + 1):
                kf = (kv[:, LANE * hh:LANE * (hh + 1)] + krs).astype(BF16)
                vh = kv[:, LANE * (N_HEADS + hh):LANE * (N_HEADS + hh + 1)].astype(BF16)
                sc = _dot_nt(q_ref[r0:r0 + SEQ, LANE * hh:LANE * (hh + 1)], kf) * ATTN_SCALE
                e = jnp.exp(sc - jnp.max(sc, axis=-1, keepdims=True))
                p = (e / jnp.sum(e, axis=-1, keepdims=True)).astype(BF16)
                part = _dot(p, vh)
                acc = part if acc is None else acc + part
            o_ref[r0:r0 + SEQ, LANE * pair:LANE * (pair + 1)] = acc.astype(BF16)


def _attn_sample_kernel(q_ref, ckv_ref, cckv_ref, krs_ref, ckrs_ref, wukv_ref, o_ref, kf_scr, v_scr):
    @pl.when(pl.program_id(1) == 0)
    def _():
        _fill_kv(ckv_ref[...], krs_ref[...], wukv_ref, kf_scr, v_scr, 0)
        _fill_kv(cckv_ref[...], ckrs_ref[...], wukv_ref, kf_scr, v_scr, DEC_SEQ)

    _attend(q_ref, 0, Q_BLK_S, kf_scr, v_scr, o_ref)


def _attention(q, ckv, krs, cache_ckv, cache_krs, w_ukv_slot):
    kv_cols = 2 * N_HEADS * LANE
    n_o = N_HEADS * V_HEAD
    steps = BATCH // PROMPT_SEQS
    rows = PROMPT_SEQS * SEQ
    o_prompt = pl.pallas_call(
        _attn_prompt_kernel,
        grid=(steps,),
        in_specs=[
            pl.BlockSpec((rows, N_HEADS * LANE), lambda b: (b, 0)),
            pl.BlockSpec((rows, KV_LORA), lambda b: (b, 0)),
            pl.BlockSpec((rows, LANE), lambda b: (b, 0)),
            pl.BlockSpec((KV_LORA, kv_cols), lambda b: (0, 0)),
        ],
        out_specs=pl.BlockSpec((rows, n_o), lambda b: (b, 0)),
        out_shape=jax.ShapeDtypeStruct((T_P, n_o), BF16),
        compiler_params=pltpu.CompilerParams(dimension_semantics=("parallel",),
                                             vmem_limit_bytes=VMEM_LIMIT),
        name="attn_prompt",
    )(q, ckv, krs, w_ukv_slot)

    m_all = DEC_SEQ + PAST_LEN
    nq = DEC_SEQ // Q_BLK_S
    q0 = T_P // Q_BLK_S
    s0 = T_P // DEC_SEQ
    o_sample = pl.pallas_call(
        _attn_sample_kernel,
        grid=(DEC_BATCH, nq),
        in_specs=[
            pl.BlockSpec((Q_BLK_S, N_HEADS * LANE), lambda b, j: (q0 + b * nq + j, 0)),
            pl.BlockSpec((DEC_SEQ, KV_LORA), lambda b, j: (s0 + b, 0)),
            pl.BlockSpec((None, PAST_LEN, KV_LORA), lambda b, j: (b, 0, 0)),
            pl.BlockSpec((DEC_SEQ, LANE), lambda b, j: (s0 + b, 0)),
            pl.BlockSpec((None, PAST_LEN, LANE), lambda b, j: (b, 0, 0)),
            pl.BlockSpec((KV_LORA, kv_cols), lambda b, j: (0, 0)),
        ],
        out_specs=pl.BlockSpec((Q_BLK_S, n_o), lambda b, j: (b * nq + j, 0)),
        out_shape=jax.ShapeDtypeStruct((T_S, n_o), BF16),
        scratch_shapes=[pltpu.VMEM((N_HEADS, m_all, LANE), BF16),
                        pltpu.VMEM((m_all, N_HEADS * LANE), BF16)],
        compiler_params=pltpu.CompilerParams(dimension_semantics=("parallel", "arbitrary"),
                                             vmem_limit_bytes=VMEM_LIMIT),
        name="attn_sample",
    )(q, ckv, cache_ckv, krs, cache_krs, w_ukv_slot)
    return o_prompt, o_sample


def _route(logits):
    lane = lax.broadcasted_iota(jnp.int32, logits.shape, 1)
    neg = -jnp.inf
    big = jnp.int32(1 << 20)
    gmask = (lane >= N_EXPERTS) & (lane < N_EXPERTS + N_GROUPS)
    gl = jnp.where(gmask, logits, neg)
    gmax = jnp.max(gl, axis=-1, keepdims=True)
    gsum = jnp.sum(jnp.where(gmask, jnp.exp(gl - gmax), 0.0), axis=-1, keepdims=True)
    p_g = 1.0 / gsum
    g_idx = jnp.min(jnp.where(gl == gmax, lane, big), axis=-1, keepdims=True) - N_EXPERTS

    emask = (lane < N_EXPERTS) & ((lane >> 3) == g_idx)
    el = jnp.where(emask, logits, neg)
    m1 = jnp.max(el, axis=-1, keepdims=True)
    i1 = jnp.min(jnp.where(el == m1, lane, big), axis=-1, keepdims=True)
    el2 = jnp.where(lane == i1, neg, el)
    m2 = jnp.max(el2, axis=-1, keepdims=True)
    i2 = jnp.min(jnp.where(el2 == m2, lane, big), axis=-1, keepdims=True)
    z = jnp.sum(jnp.where(emask, jnp.exp(el - m1), 0.0), axis=-1, keepdims=True)
    p1 = 1.0 / z
    p2 = jnp.exp(m2 - m1) / z
    tot = p1 + p2
    w1 = p_g * p1 / tot
    w2 = p_g * p2 / tot
    return jnp.where(lane == i1, w1, 0.0) + jnp.where(lane == i2, w2, 0.0), g_idx


def _post_kernel(xp_ref, xs_ref, mod_ref, n1_ref, wg_ref, zc_ref, op_ref, os_ref, wco_ref, wo_ref,
                 wmix_ref, n2_ref, wr_ref, x1_ref, h3_ref, meta_ref, cnt_ref, gr_ref):
    i = pl.program_id(0)
    is_sample = i >= T_P // TM_POST
    x = jnp.where(is_sample, xs_ref[...], xp_ref[...])
    o = jnp.where(is_sample, os_ref[...], op_ref[...])
    mod = mod_ref[pl.ds(_mod_row(i, TM_POST), 1), :]
    shift1 = mod[:, 0:D_MODEL]
    scale1 = mod[:, D_MODEL:2 * D_MODEL]
    gate1 = mod[:, 2 * D_MODEL:3 * D_MODEL]
    shift2 = mod[:, 3 * D_MODEL:4 * D_MODEL]
    scale2 = mod[:, 4 * D_MODEL:5 * D_MODEL]
    y_conv = _dot(zc_ref[...], wco_ref[...].astype(BF16))
    y_mla = _dot(o, wo_ref[...].astype(BF16))
    h = ((x * _rms(x)) * n1_ref[...]) * (1.0 + scale1) + shift1
    g = _dot_nt(h.astype(BF16), wg_ref[...].astype(BF16))
    merged = (jax.nn.sigmoid(g[:, 0:D_MODEL]) * y_conv
              + jax.nn.sigmoid(g[:, D_MODEL:2 * D_MODEL]) * y_mla)
    y = _dot(merged.astype(BF16), wmix_ref[...].astype(BF16))
    x1 = x + gate1 * y
    x1_ref[...] = x1
    h2 = ((x1 * _rms(x1)) * n2_ref[...]) * (1.0 + scale2) + shift2
    h2_hi = h2.astype(BF16)
    h2_lo = (h2 - h2_hi.astype(F32)).astype(BF16)
    hh = _dot(h2_hi, wr_ref[...])
    logits = hh[:, 0:LANE] + hh[:, LANE:2 * LANE] + _dot(h2_lo, wr_ref[:, 0:LANE])
    comb, g_idx = _route(logits)

    lane = lax.broadcasted_iota(jnp.int32, comb.shape, 1)
    onehot = lane == g_idx + N_EXPERTS
    r_i = lax.broadcasted_iota(jnp.int32, (TM_POST, TM_POST), 0)
    c_i = lax.broadcasted_iota(jnp.int32, (TM_POST, TM_POST), 1)
    lower = jnp.where(c_i < r_i, 1.0, 0.0).astype(BF16)
    before = _dot(lower, jnp.where(onehot, 1.0, 0.0).astype(BF16))
    rank = jnp.sum(jnp.where(onehot, before, 0.0), axis=-1, keepdims=True)
    counts = jnp.sum(jnp.where(onehot, 1.0, 0.0), axis=0, keepdims=True)
    cnt_ref[...] = jnp.broadcast_to(counts, cnt_ref.shape)

    meta_ref[...] = comb
    idx = jnp.where(lane == GID_LANE, g_idx.astype(F32), 0.0) + jnp.where(lane == RANK_LANE, rank, 0.0)
    idx_hi = idx.astype(BF16)
    idx_lo = (idx - idx_hi.astype(F32)).astype(BF16)
    s_row = lax.broadcasted_iota(jnp.int32, (8, LANE), 0)
    s_lane = lax.broadcasted_iota(jnp.int32, (8, LANE), 1)
    sel = jnp.where(s_lane == GID_LANE + s_row, 1.0, 0.0).astype(BF16)
    gr_ref[...] = _dot_nt(sel, idx_hi) + _dot_nt(sel, idx_lo)
    for c in range(N_SLAB):
        h3_ref[pl.ds(c, TM_POST, stride=N_SLAB), :] = h2[:, LANE * c:LANE * (c + 1)]


def _post(xp, xs, mod, norm1, w_in_t, zc, o_p, o_s, w_conv_out, w_o, w_mix_out, norm2, w_route):
    const = lambda i: (0, 0)
    row = lambda i: (i, 0)
    pmap, smap = _stream_maps(TM_POST)
    once = pl.Buffered(1)
    return pl.pallas_call(
        _post_kernel,
        grid=(T // TM_POST,),
        in_specs=[
            pl.BlockSpec((TM_POST, D_MODEL), pmap),
            pl.BlockSpec((TM_POST, D_MODEL), smap),
            pl.BlockSpec((N_COND, 6 * D_MODEL), const),
            pl.BlockSpec((1, D_MODEL), const),
            pl.BlockSpec((pl.Element(2 * D_MODEL), pl.Element(D_MODEL)), lambda i: (O_GATE, 0),
                         pipeline_mode=once),
            pl.BlockSpec((TM_POST, D_CONV), row),
            pl.BlockSpec((TM_POST, N_HEADS * V_HEAD), pmap),
            pl.BlockSpec((TM_POST, N_HEADS * V_HEAD), smap),
            pl.BlockSpec((D_CONV, D_MODEL), const, pipeline_mode=once),
            pl.BlockSpec((N_HEADS * V_HEAD, D_MODEL), const, pipeline_mode=once),
            pl.BlockSpec((D_MODEL, D_MODEL), const, pipeline_mode=once),
            pl.BlockSpec((1, D_MODEL), const),
            pl.BlockSpec((D_MODEL, 2 * LANE), const),
        ],
        out_specs=[
            pl.BlockSpec((TM_POST, D_MODEL), row),
            pl.BlockSpec((TM_POST * N_SLAB, LANE), row),
            pl.BlockSpec((TM_POST, LANE), row),
            pl.BlockSpec((None, 8, LANE), lambda i: (i, 0, 0)),
            pl.BlockSpec((8, TM_POST), lambda i: (0, i)),
        ],
        out_shape=[
            jax.ShapeDtypeStruct((T, D_MODEL), F32),
            jax.ShapeDtypeStruct((T * N_SLAB, LANE), F32),
            jax.ShapeDtypeStruct((T, LANE), F32),
            jax.ShapeDtypeStruct((T // TM_POST, 8, LANE), F32),
            jax.ShapeDtypeStruct((8, T), F32),
        ],
        compiler_params=pltpu.CompilerParams(
            dimension_semantics=("parallel",), vmem_limit_bytes=VMEM_LIMIT),
        name="post_mixer",
    )(xp, xs, mod, norm1, w_in_t, zc, o_p, o_s, w_conv_out, w_o, w_mix_out, norm2, w_route)


DISP_IN = T // TM_DISP
DISP_OUT = T // TM_MOE


def _dispatch_kernel(pos_ref, h3_ref, m_ref, hs_ref, ms_ref, xs_ref, mss_ref):
    i = pl.program_id(0)

    @pl.when(i < DISP_IN)
    def _():
        def body(r, carry):
            p = pos_ref[0, r]
            xs_ref[_slab(p), :] = h3_ref[_slab(r), :]
            mss_ref[pl.ds(p, 1), :] = m_ref[pl.ds(r, 1), :]
            return carry

        lax.fori_loop(0, TM_DISP, body, 0, unroll=8)

    @pl.when(i >= DISP_IN)
    def _():
        row0 = pl.multiple_of((i - DISP_IN) * TM_MOE, TM_MOE)
        for c in range(N_SLAB):
            hs_ref[:, LANE * c:LANE * (c + 1)] = (
                xs_ref[pl.ds(row0 * N_SLAB + c, TM_MOE, stride=N_SLAB), :].astype(BF16))
        ms_ref[...] = mss_ref[pl.ds(row0, TM_MOE), :]


def _dispatch(pos, h3, meta):
    n_slab = N_SLAB
    in_map = lambda i: (jnp.minimum(i, DISP_IN - 1), 0)
    out_map = lambda i: (jnp.maximum(i - DISP_IN, 0), 0)
    return pl.pallas_call(
        _dispatch_kernel,
        grid=(DISP_IN + DISP_OUT,),
        in_specs=[pl.BlockSpec((None, 1, TM_DISP), lambda i: (jnp.minimum(i, DISP_IN - 1), 0, 0),
                               memory_space=pltpu.SMEM),
                  pl.BlockSpec((TM_DISP * n_slab, LANE), in_map),
                  pl.BlockSpec((TM_DISP, LANE), in_map)],
        out_specs=[pl.BlockSpec((TM_MOE, D_MODEL), out_map),
                   pl.BlockSpec((TM_MOE, LANE), out_map)],
        scratch_shapes=[pltpu.VMEM((T * n_slab, LANE), F32),
                        pltpu.VMEM((T, LANE), F32)],
        out_shape=[jax.ShapeDtypeStruct((T, D_MODEL), BF16),
                   jax.ShapeDtypeStruct((T, LANE), F32)],
        compiler_params=pltpu.CompilerParams(
            dimension_semantics=("arbitrary",), vmem_limit_bytes=VMEM_LIMIT),
        name="moe_dispatch",
    )(pos.reshape(DISP_IN, 1, TM_DISP), h3, meta)


def _moe_kernel(sched_ref, hs_ref, ms_ref, wup_ref, wgate_ref, wdown_ref, y3_ref, acc_ref):
    v = pl.program_id(0)
    j = pl.program_id(1)
    valid = sched_ref[V_VALID, v] == 1
    lo = sched_ref[V_LO, v]
    hi = sched_ref[V_HI, v]
    e0 = sched_ref[V_GROUP, v] * EXP_PER_GROUP + j * MOE_EPS
    full = (hi - lo) * 4 >= TM_MOE * 3

    @pl.when(valid & (j == 0) & (sched_ref[V_FIRST, v] == 1))
    def _():
        acc_ref[...] = jnp.zeros_like(acc_ref)

    def expert_rows(r0, rows):
        w_in2 = jnp.concatenate(
            [w[k].astype(BF16) for k in range(MOE_EPS) for w in (wup_ref, wgate_ref)], axis=1)
        ag = _dot(hs_ref[pl.ds(r0, rows), :], w_in2)
        comb = ms_ref[pl.ds(r0, rows), :]
        lane = lax.broadcasted_iota(jnp.int32, comb.shape, 1)
        acts = []
        for k in range(MOE_EPS):
            a = ag[:, 2 * k * D_EXPERT:(2 * k + 1) * D_EXPERT]
            g = ag[:, (2 * k + 1) * D_EXPERT:(2 * k + 2) * D_EXPERT]
            cw = jnp.sum(jnp.where(lane == e0 + k, comb, 0.0), axis=-1, keepdims=True)
            acts.append(((g * jax.nn.sigmoid(g)) * a * cw).astype(BF16))
        w_out = jnp.concatenate([wdown_ref[k].astype(BF16) for k in range(MOE_EPS)], axis=0)
        acc_ref[pl.ds(r0, rows), :] += _dot(jnp.concatenate(acts, axis=1), w_out)

    @pl.when(valid & full)
    def _():
        expert_rows(0, TM_MOE)

    @pl.when(valid & jnp.logical_not(full))
    def _():
        def sub_block(s, carry):
            expert_rows(pl.multiple_of(s * MOE_SUB, MOE_SUB), MOE_SUB)
            return carry

        lax.fori_loop(lo // MOE_SUB, (hi + MOE_SUB - 1) // MOE_SUB, sub_block, 0)

    @pl.when(valid & (j == EXP_PER_GROUP // MOE_EPS - 1) & (sched_ref[V_LAST, v] == 1))
    def _():
        for c in range(N_SLAB):
            y3_ref[pl.ds(c, TM_MOE, stride=N_SLAB), :] = acc_ref[:, LANE * c:LANE * (c + 1)]


def _moe(sched, hs, ms, w_up, w_gate, w_down):
    steps = EXP_PER_GROUP // MOE_EPS
    wmap = lambda v, j, sched: (
        sched[V_GROUP, v] * steps + jnp.where(sched[V_VALID, v] == 1, j, steps - 1), 0, 0)
    tmap = lambda v, j, sched: (sched[V_TILE, v], 0)
    n_slab = D_MODEL // LANE
    return pl.pallas_call(
        _moe_kernel,
        grid_spec=pltpu.PrefetchScalarGridSpec(
            num_scalar_prefetch=1,
            grid=(N_VISITS, steps),
            in_specs=[
                pl.BlockSpec((TM_MOE, D_MODEL), tmap),
                pl.BlockSpec((TM_MOE, LANE), tmap),
                pl.BlockSpec((MOE_EPS, D_MODEL, D_EXPERT), wmap),
                pl.BlockSpec((MOE_EPS, D_MODEL, D_EXPERT), wmap),
                pl.BlockSpec((MOE_EPS, D_EXPERT, D_MODEL), wmap),
            ],
            out_specs=pl.BlockSpec((TM_MOE * n_slab, LANE), tmap),
            scratch_shapes=[pltpu.VMEM((TM_MOE, D_MODEL), F32)],
        ),
        out_shape=jax.ShapeDtypeStruct((T * n_slab, LANE), F32),
        compiler_params=pltpu.CompilerParams(
            dimension_semantics=("arbitrary", "arbitrary"), vmem_limit_bytes=VMEM_LIMIT),
        name="moe_grouped",
    )(sched, hs, ms, w_up, w_gate, w_down)


def _final_kernel(pos_ref, ys_ref, x1_ref, mod_ref, fn_ref, yp_ref, ysm_ref, g_ref):
    i = pl.program_id(0)

    def body(r, carry):
        g_ref[_slab(r), :] = ys_ref[_slab(pos_ref[0, r]), :]
        return carry

    lax.fori_loop(0, TM_FINAL, body, 0, unroll=8)
    mod = mod_ref[pl.ds(_mod_row(i, TM_FINAL), 1), :]
    gate2 = mod[:, 5 * D_MODEL:6 * D_MODEL]
    moe = jnp.concatenate([g_ref[pl.ds(c, TM_FINAL, stride=N_SLAB), :] for c in range(N_SLAB)], axis=1)
    x2 = x1_ref[...] + gate2 * moe
    y = (x2 * _rms(x2)) * fn_ref[...]
    is_sample = i >= T_P // TM_FINAL

    @pl.when(jnp.logical_not(is_sample))
    def _():
        yp_ref[...] = y

    @pl.when(is_sample)
    def _():
        ysm_ref[...] = y


def _final(pos, ys, x1, mod, final_norm):
    n_slab = D_MODEL // LANE
    pmap, smap = _stream_maps(TM_FINAL)
    return pl.pallas_call(
        _final_kernel,
        grid=(T // TM_FINAL,),
        in_specs=[
            pl.BlockSpec((None, 1, TM_FINAL), lambda i: (i, 0, 0), memory_space=pltpu.SMEM),
            pl.BlockSpec((T * n_slab, LANE), lambda i: (0, 0), pipeline_mode=pl.Buffered(1)),
            pl.BlockSpec((TM_FINAL, D_MODEL), lambda i: (i, 0)),
            pl.BlockSpec((N_COND, 6 * D_MODEL), lambda i: (0, 0)),
            pl.BlockSpec((1, D_MODEL), lambda i: (0, 0)),
        ],
        out_specs=[pl.BlockSpec((TM_FINAL, D_MODEL), pmap),
                   pl.BlockSpec((TM_FINAL, D_MODEL), smap)],
        scratch_shapes=[pltpu.VMEM((TM_FINAL * n_slab, LANE), F32)],
        out_shape=[jax.ShapeDtypeStruct((T_P, D_MODEL), F32),
                   jax.ShapeDtypeStruct((T_S, D_MODEL), F32)],
        compiler_params=pltpu.CompilerParams(
            dimension_semantics=("arbitrary",), vmem_limit_bytes=VMEM_LIMIT),
        name="moe_unsort_final",
    )(pos.reshape(T // TM_FINAL, 1, TM_FINAL), ys, x1, mod, final_norm)


def _plan_kernel(gr_ref, cnt_ref, pos_ref, sched_ref):
    n_post = T // TM_POST
    lane = lax.broadcasted_iota(jnp.int32, (1, LANE), 1)
    grp_lane = lambda v, g: v[:, N_EXPERTS + g:N_EXPERTS + g + 1]
    counts = [cnt_ref[k, 0:1, :] for k in range(n_post)]
    gtot_v = counts[0]
    for k in range(1, n_post):
        gtot_v = gtot_v + counts[k]
    gtot = [grp_lane(gtot_v, g) for g in range(N_GROUPS)]
    goff = [jnp.zeros((1, 1), F32)]
    for g in range(1, N_GROUPS):
        goff.append(goff[-1] + gtot[g - 1])

    before = [jnp.zeros((1, 1), F32) for _ in range(N_GROUPS)]
    rows_per_tile = TM_POST // LANE
    for k in range(n_post):
        gid = gr_ref[0:1, TM_POST * k:TM_POST * (k + 1)]
        pos = gr_ref[1:2, TM_POST * k:TM_POST * (k + 1)]
        for g in range(N_GROUPS):
            pos = pos + jnp.where(gid == float(g), goff[g] + before[g], 0.0)
            before[g] = before[g] + grp_lane(counts[k], g)
        for r in range(rows_per_tile):
            pos_ref[rows_per_tile * k + r:rows_per_tile * k + r + 1, :] = (
                pos[:, LANE * r:LANE * (r + 1)].astype(jnp.int32))

    zero = jnp.zeros((1, LANE), F32)
    rows = {name: zero for name in ("vt", "vg", "vlo", "vhi")}
    slot = jnp.zeros((1, 1), F32)
    last_t = jnp.zeros((1, 1), F32)
    last_g = jnp.zeros((1, 1), F32)
    for i in range(T // TM_MOE):
        for g in range(N_GROUPS):
            lo = jnp.clip(goff[g] - float(TM_MOE * i), 0.0, float(TM_MOE))
            hi = jnp.clip(goff[g] + gtot[g] - float(TM_MOE * i), 0.0, float(TM_MOE))
            ok = hi > lo
            here = ok & (lane == slot.astype(jnp.int32))
            rows["vt"] = jnp.where(here, float(i), rows["vt"])
            rows["vg"] = jnp.where(here, float(g), rows["vg"])
            rows["vlo"] = jnp.where(here, lo, rows["vlo"])
            rows["vhi"] = jnp.where(here, hi, rows["vhi"])
            last_t = jnp.where(ok, float(i), last_t)
            last_g = jnp.where(ok, float(g), last_g)
            slot = slot + jnp.where(ok, 1.0, 0.0)
    n_ok = slot.astype(jnp.int32)
    valid = lane < n_ok
    vt = jnp.where(valid, rows["vt"], last_t)
    vg = jnp.where(valid, rows["vg"], last_g)
    prev_t = pltpu.roll(jnp.broadcast_to(vt, (8, LANE)), 1, 1)[0:1, :]
    next_t = pltpu.roll(jnp.broadcast_to(vt, (8, LANE)), LANE - 1, 1)[0:1, :]
    first = jnp.where((lane == 0) | (vt != prev_t), 1.0, 0.0)
    last = jnp.where((lane == n_ok - 1) | (vt != next_t), 1.0, 0.0)
    table = [vt, vg, rows["vlo"], rows["vhi"], first, last, jnp.where(valid, 1.0, 0.0), zero]
    for r, row in enumerate(table):
        sched_ref[r:r + 1, :] = row.astype(jnp.int32)


V_TILE, V_GROUP, V_LO, V_HI, V_FIRST, V_LAST, V_VALID = range(7)


def _plan(gr, cnt):
    pos, sched = pl.pallas_call(
        _plan_kernel,
        out_shape=[jax.ShapeDtypeStruct((T // LANE, LANE), jnp.int32),
                   jax.ShapeDtypeStruct((8, LANE), jnp.int32)],
        name="moe_plan",
    )(gr, cnt)
    return pos.reshape(T), sched


def _rope_tables():
    n = np.arange(DEC_SEQ)
    pos = np.stack([n // GRID_W, n % GRID_W], axis=1).astype(np.float32)
    half = ROPE_AXIS // 2
    inv = (1.0 / (ROPE_BASE ** (np.arange(0, ROPE_AXIS, 2, dtype=np.float32) / ROPE_AXIS))).astype(np.float32)
    ang = (pos[:, :, None] * inv[None, None, :]).astype(np.float32)
    cos = np.cos(ang).astype(np.float32)
    sin = np.sin(ang).astype(np.float32)
    tabs = np.zeros((2, 3, DEC_SEQ, LANE), np.float32)
    tabs[:, 0] = 1.0
    for a in range(2):
        lo = ROPE_LANE0 + a * ROPE_AXIS
        tabs[1, 0, :, lo:lo + half] = cos[:, a]
        tabs[1, 0, :, lo + half:lo + 2 * half] = cos[:, a]
        tabs[1, 1, :, lo + half:lo + 2 * half] = sin[:, a]
        tabs[1, 2, :, lo:lo + half] = -sin[:, a]
    return jnp.asarray(tabs)


def kernel(x_prompt, x_sample, cache_ckv, cache_krope, c, c_ctx, norm1, w_ada, b_ada, w_in, conv_w,
           w_conv_out, q_norm, w_uq, kv_norm, w_ukv, w_o, w_mix_out, norm2, w_grp, w_exp, w_up,
           w_gate, w_down, final_norm):
    l = 0
    xp = x_prompt.reshape(T_P, D_MODEL)
    xs = x_sample.reshape(T_S, D_MODEL)
    mod = _ada(c_ctx[None, :], c, w_ada[l], b_ada[l][None, :])

    w_in_t = w_in[l].T
    w_uq_slot = jnp.pad(w_uq[l].reshape(Q_LORA, N_HEADS, QK_NOPE + QK_ROPE),
                        ((0, 0), (0, 0), (0, LANE - QK_NOPE - QK_ROPE))
                        ).reshape(Q_LORA, N_HEADS * LANE).astype(BF16)
    wkv = w_ukv[l].reshape(KV_LORA, N_HEADS, QK_NOPE + V_HEAD)
    wk_slot = jnp.pad(wkv[:, :, :QK_NOPE], ((0, 0), (0, 0), (0, LANE - QK_NOPE)))
    wv = wkv[:, :, QK_NOPE:].reshape(KV_LORA, N_HEADS // 2, 2, V_HEAD)
    zero = jnp.zeros_like(wv[:, :, 0])
    wv_slot = jnp.stack([jnp.concatenate([wv[:, :, 0], zero], axis=-1),
                         jnp.concatenate([zero, wv[:, :, 1]], axis=-1)], axis=2)
    w_ukv_slot = jnp.concatenate([wk_slot.reshape(KV_LORA, N_HEADS * LANE),
                                  wv_slot.reshape(KV_LORA, N_HEADS * LANE)], axis=1).astype(BF16)
    w_route = jnp.pad(jnp.concatenate([w_exp[l], w_grp[l]], axis=1),
                      ((0, 0), (0, LANE - N_EXPERTS - N_GROUPS)))
    w_route_hi = w_route.astype(BF16)
    w_route_lo = (w_route - w_route_hi.astype(F32)).astype(BF16)
    w_route2 = jnp.concatenate([w_route_hi, w_route_lo], axis=1)
    cache_krs = jnp.pad(cache_krope[:, l], ((0, 0), (0, 0), (ROPE_LANE0, LANE - ROPE_LANE0 - QK_ROPE)))

    zc, q, ckv, krs, nckv, nkr = _inproj(xp, xs, mod, norm1[l][None, :], w_in_t, conv_w[l],
                              q_norm[l][None, :], kv_norm[l][None, :], w_uq_slot, _rope_tables())
    o_p, o_s = _attention(q, ckv, krs, cache_ckv[:, l], cache_krs, w_ukv_slot)
    x1, h3, meta, cnt, gr = _post(xp, xs, mod, norm1[l][None, :], w_in_t, zc, o_p, o_s, w_conv_out[l],
                                  w_o[l], w_mix_out[l], norm2[l][None, :], w_route2)
    pos, sched = _plan(gr, cnt)
    hs, ms = _dispatch(pos, h3, meta)
    ys = _moe(sched, hs, ms, w_up[l], w_gate[l], w_down[l])
    yp, ysm = _final(pos, ys, x1, mod, final_norm[None, :])

    y_prompt = yp.reshape(BATCH, SEQ, D_MODEL)
    y_sample = ysm.reshape(DEC_BATCH, DEC_SEQ, D_MODEL)
    new_ckv = nckv.reshape(BATCH, 1, SEQ, KV_LORA)
    new_krope = jnp.swapaxes(nkr, 1, 2).reshape(BATCH, 1, SEQ, QK_ROPE)
    return (y_prompt, y_sample, new_ckv, new_krope)
```

```python
import numpy as np
import jax
import jax.numpy as jnp
from jax import lax
from jax.experimental import pallas as pl
from jax.experimental.pallas import tpu as pltpu

F32 = jnp.float32
BF16 = jnp.bfloat16

D_MODEL = 1024
BATCH = 16
SEQ = 256
DEC_BATCH = 2
DEC_SEQ = 1024
PAST_LEN = 256
GRID_W = 64
N_HEADS = 8
QK_NOPE = 64
QK_ROPE = 32
V_HEAD = 64
Q_LORA = 256
KV_LORA = 128
ROPE_AXIS = QK_ROPE // 2
ROPE_BASE = 10000.0
ATTN_SCALE = (QK_NOPE + QK_ROPE) ** -0.5
D_CONV = D_MODEL
N_GROUPS = 4
EXP_PER_GROUP = 8
N_EXPERTS = N_GROUPS * EXP_PER_GROUP
D_EXPERT = 256
EPS = 1e-6

T_P = BATCH * SEQ
T_S = DEC_BATCH * DEC_SEQ
T = T_P + T_S
N_COND = 8
LANE = 128
ROPE_LANE0 = QK_NOPE
SMALL_COLS = Q_LORA + KV_LORA + LANE
VMEM_LIMIT = 56 * 1024 * 1024

TM_IN = 1024
TM_POST = 512
TM_MOE = T // N_GROUPS
TM_FINAL = 512
TM_DISP = 1024
N_SLAB = D_MODEL // LANE
MOE_SUB = 256
MOE_EPS = 2
N_VISITS = T // TM_MOE + N_GROUPS - 1
GID_LANE = 40
RANK_LANE = 41
Q_BLK_S = 512
PROMPT_SEQS = 2
CONV_CHUNK = 256


def _dot(a, b):
    return jnp.dot(a, b, preferred_element_type=F32)


def _rms(x):
    return lax.rsqrt(jnp.mean(x * x, axis=-1, keepdims=True) + EPS)


def _slab(t):
    return pl.ds(pl.multiple_of(t * N_SLAB, N_SLAB), N_SLAB)


def _mod_row(i, tm):
    n_prompt = T_P // tm
    return jnp.where(i >= n_prompt, 1 + ((i - n_prompt) * tm) // DEC_SEQ, 0)


def _ada_kernel(cctx_ref, c_ref, w_ref, b_ref, o_ref):
    c = jnp.concatenate([cctx_ref[...], c_ref[...],
                         jnp.zeros((N_COND - 1 - DEC_BATCH, D_MODEL), F32)], axis=0)
    a = (c * jax.nn.sigmoid(c)).astype(BF16)
    o_ref[...] = _dot(a, w_ref[...].astype(BF16)) + b_ref[...]


def _ada(c_ctx, c, w_ada, b_ada):
    n = 6 * D_MODEL
    bn = 1536
    return pl.pallas_call(
        _ada_kernel,
        grid=(n // bn,),
        in_specs=[
            pl.BlockSpec((1, D_MODEL), lambda j: (0, 0)),
            pl.BlockSpec((DEC_BATCH, D_MODEL), lambda j: (0, 0)),
            pl.BlockSpec((D_MODEL, bn), lambda j: (0, j)),
            pl.BlockSpec((1, bn), lambda j: (0, j)),
        ],
        out_specs=pl.BlockSpec((N_COND, bn), lambda j: (0, j)),
        out_shape=jax.ShapeDtypeStruct((N_COND, n), F32),
        compiler_params=pltpu.CompilerParams(
            dimension_semantics=("parallel",), vmem_limit_bytes=VMEM_LIMIT),
        name="ada_mod",
    )(c_ctx, c, w_ada, b_ada)


O_CQ = 3 * D_CONV
O_KR = O_CQ + Q_LORA + KV_LORA
O_GATE = O_KR + QK_ROPE
SMALL_BLK = 512
NT = (((1,), (1,)), ((), ()))


def _dot_nt(a, bt):
    return lax.dot_general(a, bt, NT, preferred_element_type=F32)


def _stream_maps(tm):
    n_prompt = T_P // tm
    return (lambda i, *_: (jnp.minimum(i, n_prompt - 1), 0),
            lambda i, *_: (jnp.maximum(i - n_prompt, 0), 0))


def _inproj_kernel(xp_ref, xs_ref, mod_ref, n1_ref, wc_ref, ws_ref, cw_ref, qn_ref, kvn_ref, wuq_ref,
                   rope_ref, zc_ref, q_ref, ckv_ref, krs_ref, nckv_ref, nkr_ref):
    i = pl.program_id(0)
    is_sample = i >= T_P // TM_IN
    seq = jnp.where(is_sample, DEC_SEQ, SEQ)
    mod = mod_ref[pl.ds(_mod_row(i, TM_IN), 1), :]
    shift1 = mod[:, 0:D_MODEL]
    scale1 = mod[:, D_MODEL:2 * D_MODEL]
    x = jnp.where(is_sample, xs_ref[...], xp_ref[...])
    h = ((x * _rms(x)) * n1_ref[...]) * (1.0 + scale1) + shift1
    hb = h.astype(BF16)

    n_lat = Q_LORA + KV_LORA
    w_small = jnp.concatenate([
        ws_ref[0:n_lat, :].astype(BF16),
        jnp.zeros((ROPE_LANE0, D_MODEL), BF16),
        ws_ref[n_lat:n_lat + QK_ROPE, :].astype(BF16),
        jnp.zeros((LANE - ROPE_LANE0 - QK_ROPE, D_MODEL), BF16)], axis=0)
    sm = _dot_nt(hb, w_small)
    cq = sm[:, 0:Q_LORA]
    ckv_raw = sm[:, Q_LORA:Q_LORA + KV_LORA]
    krs = sm[:, Q_LORA + KV_LORA:SMALL_COLS]
    cqn = (cq * _rms(cq)) * qn_ref[...]
    q = _dot(cqn.astype(BF16), wuq_ref[...])
    ckv = (ckv_raw * _rms(ckv_raw)) * kvn_ref[...]
    ckv_ref[...] = ckv

    @pl.when(jnp.logical_not(is_sample))
    def _():
        nckv_ref[...] = ckv
        for s in range(TM_IN // SEQ):
            kt = krs[s * SEQ:(s + 1) * SEQ, :].T
            nkr_ref[s] = kt[ROPE_LANE0:ROPE_LANE0 + QK_ROPE, :]

    cos = rope_ref[0]
    sin_lo = rope_ref[1]
    sin_hi = rope_ref[2]

    def rot(v):
        return v * cos + pltpu.roll(v, 8, 1) * sin_lo + pltpu.roll(v, LANE - 8, 1) * sin_hi

    krs_ref[...] = rot(krs)
    for hh in range(N_HEADS):
        q_ref[:, LANE * hh:LANE * (hh + 1)] = rot(q[:, LANE * hh:LANE * (hh + 1)]).astype(BF16)

    pos = lax.broadcasted_iota(jnp.int32, (TM_IN, 1), 0) & (seq - 1)
    first = pos == 0
    last = pos == seq - 1
    for j in range(D_CONV // CONV_CHUNK):
        c0 = j * CONV_CHUNK
        bg = _dot_nt(hb, wc_ref[c0:c0 + CONV_CHUNK, :].astype(BF16))
        cg = _dot_nt(hb, wc_ref[D_CONV + c0:D_CONV + c0 + CONV_CHUNK, :].astype(BF16))
        ui = _dot_nt(hb, wc_ref[2 * D_CONV + c0:2 * D_CONV + c0 + CONV_CHUNK, :].astype(BF16))
        u = cg * ui
        u_prev = jnp.where(first, 0.0, pltpu.roll(u, 1, 0))
        u_next = jnp.where(last, 0.0, pltpu.roll(u, TM_IN - 1, 0))
        cw = cw_ref[:, c0:c0 + CONV_CHUNK]
        conv = u_prev * cw[0:1] + u * cw[1:2] + u_next * cw[2:3]
        zc_ref[:, c0:c0 + CONV_CHUNK] = (bg * conv).astype(BF16)


def _inproj(xp, xs, mod, norm1, w_in_t, conv_w, q_norm, kv_norm, w_uq_slot, rope_tabs):
    n_prompt = T_P // TM_IN
    const = lambda i: (0, 0)
    pmap, smap = _stream_maps(TM_IN)
    once = pl.Buffered(1)
    return pl.pallas_call(
        _inproj_kernel,
        grid=(T // TM_IN,),
        in_specs=[
            pl.BlockSpec((TM_IN, D_MODEL), pmap),
            pl.BlockSpec((TM_IN, D_MODEL), smap),
            pl.BlockSpec((N_COND, 6 * D_MODEL), const),
            pl.BlockSpec((1, D_MODEL), const),
            pl.BlockSpec((O_CQ, D_MODEL), const, pipeline_mode=once),
            pl.BlockSpec((SMALL_BLK, D_MODEL), lambda i: (O_CQ // SMALL_BLK, 0), pipeline_mode=once),
            pl.BlockSpec((3, D_CONV), const),
            pl.BlockSpec((1, Q_LORA), const),
            pl.BlockSpec((1, KV_LORA), const),
            pl.BlockSpec((Q_LORA, N_HEADS * LANE), const),
            pl.BlockSpec((None, 3, TM_IN, LANE),
                         lambda i: (jnp.where(i >= n_prompt, 1, 0), 0, 0, 0)),
        ],
        out_specs=[
            pl.BlockSpec((TM_IN, D_CONV), lambda i: (i, 0)),
            pl.BlockSpec((TM_IN, N_HEADS * LANE), lambda i: (i, 0)),
            pl.BlockSpec((TM_IN, KV_LORA), lambda i: (i, 0)),
            pl.BlockSpec((TM_IN, LANE), lambda i: (i, 0)),
            pl.BlockSpec((TM_IN, KV_LORA), pmap),
            pl.BlockSpec((TM_IN // SEQ, QK_ROPE, SEQ), lambda i: (jnp.minimum(i, n_prompt - 1), 0, 0)),
        ],
        out_shape=[
            jax.ShapeDtypeStruct((T, D_CONV), BF16),
            jax.ShapeDtypeStruct((T, N_HEADS * LANE), BF16),
            jax.ShapeDtypeStruct((T, KV_LORA), F32),
            jax.ShapeDtypeStruct((T, LANE), F32),
            jax.ShapeDtypeStruct((T_P, KV_LORA), F32),
            jax.ShapeDtypeStruct((BATCH, QK_ROPE, SEQ), F32),
        ],
        compiler_params=pltpu.CompilerParams(
            dimension_semantics=("arbitrary",), vmem_limit_bytes=VMEM_LIMIT),
        name="in_proj",
    )(xp, xs, mod, norm1, w_in_t, w_in_t, conv_w, q_norm, kv_norm, w_uq_slot, rope_tabs)


def _fill_kv(ckv, krs, wukv_ref, kf_scr, v_scr, off):
    m = ckv.shape[0]
    kv = _dot(ckv.astype(BF16), wukv_ref[...])
    for hh in range(N_HEADS):
        kf_scr[hh, off:off + m, :] = (kv[:, LANE * hh:LANE * (hh + 1)] + krs).astype(BF16)
    v_scr[off:off + m, :] = kv[:, N_HEADS * LANE:].astype(BF16)


def _attend(q_ref, r0, rows, kf_scr, v_scr, o_ref):
    for pair in range(N_HEADS // 2):
        acc = None
        for hh in (2 * pair, 2 * pair + 1):
            qh = q_ref[r0:r0 + rows, LANE * hh:LANE * (hh + 1)]
            s = _dot_nt(qh, kf_scr[hh]) * ATTN_SCALE
            e = jnp.exp(s - jnp.max(s, axis=-1, keepdims=True))
            p = (e / jnp.sum(e, axis=-1, keepdims=True)).astype(BF16)
            part = _dot(p, v_scr[:, LANE * hh:LANE * (hh + 1)])
            acc = part if acc is None else acc + part
        o_ref[r0:r0 + rows, LANE * pair:LANE * (pair + 1)] = acc.astype(BF16)


def _attn_prompt_kernel(q_ref, ckv_ref, krs_ref, wukv_ref, o_ref):
    for s in range(PROMPT_SEQS):
        r0 = s * SEQ
        kv = _dot(ckv_ref[r0:r0 + SEQ, :].astype(BF16), wukv_ref[...])
        krs = krs_ref[r0:r0 + SEQ, :]
        for pair in range(N_HEADS // 2):
            acc = None
            for hh in (2 * pair, 2 * pair + 1):
                kf = (kv[:, LANE * hh:LANE * (hh + 1)] + krs).astype(BF16)
                vh = kv[:, LANE * (N_HEADS + hh):LANE * (N_HEADS + hh + 1)].astype(BF16)
                sc = _dot_nt(q_ref[r0:r0 + SEQ, LANE * hh:LANE * (hh + 1)], kf) * ATTN_SCALE
                e = jnp.exp(sc - jnp.max(sc, axis=-1, keepdims=True))
                p = (e / jnp.sum(e, axis=-1, keepdims=True)).astype(BF16)
                part = _dot(p, vh)
                acc = part if acc is None else acc + part
            o_ref[r0:r0 + SEQ, LANE * pair:LANE * (pair + 1)] = acc.astype(BF16)


def _attn_sample_kernel(q_ref, ckv_ref, cckv_ref, krs_ref, ckrs_ref, wukv_ref, o_ref, kf_scr, v_scr):
    @pl.when(pl.program_id(1) == 0)
    def _():
        _fill_kv(ckv_ref[...], krs_ref[...], wukv_ref, kf_scr, v_scr, 0)
        _fill_kv(cckv_ref[...], ckrs_ref[...], wukv_ref, kf_scr, v_scr, DEC_SEQ)

    _attend(q_ref, 0, Q_BLK_S, kf_scr, v_scr, o_ref)


def _attention(q, ckv, krs, cache_ckv, cache_krs, w_ukv_slot):
    kv_cols = 2 * N_HEADS * LANE
    n_o = N_HEADS * V_HEAD
    steps = BATCH // PROMPT_SEQS
    rows = PROMPT_SEQS * SEQ
    o_prompt = pl.pallas_call(
        _attn_prompt_kernel,
        grid=(steps,),
        in_specs=[
            pl.BlockSpec((rows, N_HEADS * LANE), lambda b: (b, 0)),
            pl.BlockSpec((rows, KV_LORA), lambda b: (b, 0)),
            pl.BlockSpec((rows, LANE), lambda b: (b, 0)),
            pl.BlockSpec((KV_LORA, kv_cols), lambda b: (0, 0)),
        ],
        out_specs=pl.BlockSpec((rows, n_o), lambda b: (b, 0)),
        out_shape=jax.ShapeDtypeStruct((T_P, n_o), BF16),
        compiler_params=pltpu.CompilerParams(dimension_semantics=("parallel",),
                                             vmem_limit_bytes=VMEM_LIMIT),
        name="attn_prompt",
    )(q, ckv, krs, w_ukv_slot)

    m_all = DEC_SEQ + PAST_LEN
    nq = DEC_SEQ // Q_BLK_S
    q0 = T_P // Q_BLK_S
    s0 = T_P // DEC_SEQ
    o_sample = pl.pallas_call(
        _attn_sample_kernel,
        grid=(DEC_BATCH, nq),
        in_specs=[
            pl.BlockSpec((Q_BLK_S, N_HEADS * LANE), lambda b, j: (q0 + b * nq + j, 0)),
            pl.BlockSpec((DEC_SEQ, KV_LORA), lambda b, j: (s0 + b, 0)),
            pl.BlockSpec((None, PAST_LEN, KV_LORA), lambda b, j: (b, 0, 0)),
            pl.BlockSpec((DEC_SEQ, LANE), lambda b, j: (s0 + b, 0)),
            pl.BlockSpec((None, PAST_LEN, LANE), lambda b, j: (b, 0, 0)),
            pl.BlockSpec((KV_LORA, kv_cols), lambda b, j: (0, 0)),
        ],
        out_specs=pl.BlockSpec((Q_BLK_S, n_o), lambda b, j: (b * nq + j, 0)),
        out_shape=jax.ShapeDtypeStruct((T_S, n_o), BF16),
        scratch_shapes=[pltpu.VMEM((N_HEADS, m_all, LANE), BF16),
                        pltpu.VMEM((m_all, N_HEADS * LANE), BF16)],
        compiler_params=pltpu.CompilerParams(dimension_semantics=("parallel", "arbitrary"),
                                             vmem_limit_bytes=VMEM_LIMIT),
        name="attn_sample",
    )(q, ckv, cache_ckv, krs, cache_krs, w_ukv_slot)
    return o_prompt, o_sample


def _route(logits):
    lane = lax.broadcasted_iota(jnp.int32, logits.shape, 1)
    neg = -jnp.inf
    big = jnp.int32(1 << 20)
    gmask = (lane >= N_EXPERTS) & (lane < N_EXPERTS + N_GROUPS)
    gl = jnp.where(gmask, logits, neg)
    gmax = jnp.max(gl, axis=-1, keepdims=True)
    gsum = jnp.sum(jnp.where(gmask, jnp.exp(gl - gmax), 0.0), axis=-1, keepdims=True)
    p_g = 1.0 / gsum
    g_idx = jnp.min(jnp.where(gl == gmax, lane, big), axis=-1, keepdims=True) - N_EXPERTS

    emask = (lane < N_EXPERTS) & ((lane >> 3) == g_idx)
    el = jnp.where(emask, logits, neg)
    m1 = jnp.max(el, axis=-1, keepdims=True)
    i1 = jnp.min(jnp.where(el == m1, lane, big), axis=-1, keepdims=True)
    el2 = jnp.where(lane == i1, neg, el)
    m2 = jnp.max(el2, axis=-1, keepdims=True)
    i2 = jnp.min(jnp.where(el2 == m2, lane, big), axis=-1, keepdims=True)
    z = jnp.sum(jnp.where(emask, jnp.exp(el - m1), 0.0), axis=-1, keepdims=True)
    p1 = 1.0 / z
    p2 = jnp.exp(m2 - m1) / z
    tot = p1 + p2
    w1 = p_g * p1 / tot
    w2 = p_g * p2 / tot
    return jnp.where(lane == i1, w1, 0.0) + jnp.where(lane == i2, w2, 0.0), g_idx


def _post_kernel(xp_ref, xs_ref, mod_ref, n1_ref, wg_ref, zc_ref, op_ref, os_ref, wco_ref, wo_ref,
                 wmix_ref, n2_ref, wr_ref, x1_ref, h3_ref, meta_ref, cnt_ref, gr_ref):
    i = pl.program_id(0)
    is_sample = i >= T_P // TM_POST
    x = jnp.where(is_sample, xs_ref[...], xp_ref[...])
    o = jnp.where(is_sample, os_ref[...], op_ref[...])
    mod = mod_ref[pl.ds(_mod_row(i, TM_POST), 1), :]
    shift1 = mod[:, 0:D_MODEL]
    scale1 = mod[:, D_MODEL:2 * D_MODEL]
    gate1 = mod[:, 2 * D_MODEL:3 * D_MODEL]
    shift2 = mod[:, 3 * D_MODEL:4 * D_MODEL]
    scale2 = mod[:, 4 * D_MODEL:5 * D_MODEL]
    y_conv = _dot(zc_ref[...], wco_ref[...].astype(BF16))
    y_mla = _dot(o, wo_ref[...].astype(BF16))
    h = ((x * _rms(x)) * n1_ref[...]) * (1.0 + scale1) + shift1
    g = _dot_nt(h.astype(BF16), wg_ref[...].astype(BF16))
    merged = (jax.nn.sigmoid(g[:, 0:D_MODEL]) * y_conv
              + jax.nn.sigmoid(g[:, D_MODEL:2 * D_MODEL]) * y_mla)
    y = _dot(merged.astype(BF16), wmix_ref[...].astype(BF16))
    x1 = x + gate1 * y
    x1_ref[...] = x1
    h2 = ((x1 * _rms(x1)) * n2_ref[...]) * (1.0 + scale2) + shift2
    h2_hi = h2.astype(BF16)
    h2_lo = (h2 - h2_hi.astype(F32)).astype(BF16)
    wr = wr_ref[...]
    wr_hi = wr.astype(BF16)
    wr_lo = (wr - wr_hi.astype(F32)).astype(BF16)
    hh = _dot(h2_hi, jnp.concatenate([wr_hi, wr_lo], axis=1))
    logits = hh[:, 0:LANE] + hh[:, LANE:2 * LANE] + _dot(h2_lo, wr_hi)
    comb, g_idx = _route(logits)

    lane = lax.broadcasted_iota(jnp.int32, comb.shape, 1)
    onehot = lane == g_idx + N_EXPERTS
    r_i = lax.broadcasted_iota(jnp.int32, (TM_POST, TM_POST), 0)
    c_i = lax.broadcasted_iota(jnp.int32, (TM_POST, TM_POST), 1)
    lower = jnp.where(c_i < r_i, 1.0, 0.0).astype(BF16)
    before = _dot(lower, jnp.where(onehot, 1.0, 0.0).astype(BF16))
    rank = jnp.sum(jnp.where(onehot, before, 0.0), axis=-1, keepdims=True)
    counts = jnp.sum(jnp.where(onehot, 1.0, 0.0), axis=0, keepdims=True)
    cnt_ref[...] = jnp.broadcast_to(counts, cnt_ref.shape)

    meta_ref[...] = comb
    idx = jnp.where(lane == GID_LANE, g_idx.astype(F32), 0.0) + jnp.where(lane == RANK_LANE, rank, 0.0)
    idx_hi = idx.astype(BF16)
    idx_lo = (idx - idx_hi.astype(F32)).astype(BF16)
    s_row = lax.broadcasted_iota(jnp.int32, (8, LANE), 0)
    s_lane = lax.broadcasted_iota(jnp.int32, (8, LANE), 1)
    sel = jnp.where(s_lane == GID_LANE + s_row, 1.0, 0.0).astype(BF16)
    gr_ref[...] = _dot_nt(sel, idx_hi) + _dot_nt(sel, idx_lo)
    for c in range(N_SLAB):
        h3_ref[pl.ds(c, TM_POST, stride=N_SLAB), :] = h2[:, LANE * c:LANE * (c + 1)]


def _post(xp, xs, mod, norm1, w_in_t, zc, o_p, o_s, w_conv_out, w_o, w_mix_out, norm2, w_route):
    const = lambda i: (0, 0)
    row = lambda i: (i, 0)
    pmap, smap = _stream_maps(TM_POST)
    once = pl.Buffered(1)
    return pl.pallas_call(
        _post_kernel,
        grid=(T // TM_POST,),
        in_specs=[
            pl.BlockSpec((TM_POST, D_MODEL), pmap),
            pl.BlockSpec((TM_POST, D_MODEL), smap),
            pl.BlockSpec((N_COND, 6 * D_MODEL), const),
            pl.BlockSpec((1, D_MODEL), const),
            pl.BlockSpec((pl.Element(2 * D_MODEL), pl.Element(D_MODEL)), lambda i: (O_GATE, 0),
                         pipeline_mode=once),
            pl.BlockSpec((TM_POST, D_CONV), row),
            pl.BlockSpec((TM_POST, N_HEADS * V_HEAD), pmap),
            pl.BlockSpec((TM_POST, N_HEADS * V_HEAD), smap),
            pl.BlockSpec((D_CONV, D_MODEL), const, pipeline_mode=once),
            pl.BlockSpec((N_HEADS * V_HEAD, D_MODEL), const, pipeline_mode=once),
            pl.BlockSpec((D_MODEL, D_MODEL), const, pipeline_mode=once),
            pl.BlockSpec((1, D_MODEL), const),
            pl.BlockSpec((D_MODEL, LANE), const),
        ],
        out_specs=[
            pl.BlockSpec((TM_POST, D_MODEL), row),
            pl.BlockSpec((TM_POST * N_SLAB, LANE), row),
            pl.BlockSpec((TM_POST, LANE), row),
            pl.BlockSpec((None, 8, LANE), lambda i: (i, 0, 0)),
            pl.BlockSpec((8, TM_POST), lambda i: (0, i)),
        ],
        out_shape=[
            jax.ShapeDtypeStruct((T, D_MODEL), F32),
            jax.ShapeDtypeStruct((T * N_SLAB, LANE), F32),
            jax.ShapeDtypeStruct((T, LANE), F32),
            jax.ShapeDtypeStruct((T // TM_POST, 8, LANE), F32),
            jax.ShapeDtypeStruct((8, T), F32),
        ],
        compiler_params=pltpu.CompilerParams(
            dimension_semantics=("parallel",), vmem_limit_bytes=VMEM_LIMIT),
        name="post_mixer",
    )(xp, xs, mod, norm1, w_in_t, zc, o_p, o_s, w_conv_out, w_o, w_mix_out, norm2, w_route)


DISP_IN = T // TM_DISP
DISP_OUT = T // TM_MOE


def _dispatch_kernel(pos_ref, h3_ref, m_ref, hs_ref, ms_ref, xs_ref, mss_ref):
    i = pl.program_id(0)

    @pl.when(i < DISP_IN)
    def _():
        def body(r, carry):
            p = pos_ref[0, r]
            xs_ref[_slab(p), :] = h3_ref[_slab(r), :]
            mss_ref[pl.ds(p, 1), :] = m_ref[pl.ds(r, 1), :]
            return carry

        lax.fori_loop(0, TM_DISP, body, 0, unroll=8)

    @pl.when(i >= DISP_IN)
    def _():
        row0 = pl.multiple_of((i - DISP_IN) * TM_MOE, TM_MOE)
        for c in range(N_SLAB):
            hs_ref[:, LANE * c:LANE * (c + 1)] = (
                xs_ref[pl.ds(row0 * N_SLAB + c, TM_MOE, stride=N_SLAB), :].astype(BF16))
        ms_ref[...] = mss_ref[pl.ds(row0, TM_MOE), :]


def _dispatch(pos, h3, meta):
    n_slab = N_SLAB
    in_map = lambda i: (jnp.minimum(i, DISP_IN - 1), 0)
    out_map = lambda i: (jnp.maximum(i - DISP_IN, 0), 0)
    return pl.pallas_call(
        _dispatch_kernel,
        grid=(DISP_IN + DISP_OUT,),
        in_specs=[pl.BlockSpec((None, 1, TM_DISP), lambda i: (jnp.minimum(i, DISP_IN - 1), 0, 0),
                               memory_space=pltpu.SMEM),
                  pl.BlockSpec((TM_DISP * n_slab, LANE), in_map),
                  pl.BlockSpec((TM_DISP, LANE), in_map)],
        out_specs=[pl.BlockSpec((TM_MOE, D_MODEL), out_map),
                   pl.BlockSpec((TM_MOE, LANE), out_map)],
        scratch_shapes=[pltpu.VMEM((T * n_slab, LANE), F32),
                        pltpu.VMEM((T, LANE), F32)],
        out_shape=[jax.ShapeDtypeStruct((T, D_MODEL), BF16),
                   jax.ShapeDtypeStruct((T, LANE), F32)],
        compiler_params=pltpu.CompilerParams(
            dimension_semantics=("arbitrary",), vmem_limit_bytes=VMEM_LIMIT),
        name="moe_dispatch",
    )(pos.reshape(DISP_IN, 1, TM_DISP), h3, meta)


def _moe_kernel(sched_ref, hs_ref, ms_ref, wup_ref, wgate_ref, wdown_ref, y3_ref, acc_ref):
    v = pl.program_id(0)
    j = pl.program_id(1)
    valid = sched_ref[V_VALID, v] == 1
    lo = sched_ref[V_LO, v]
    hi = sched_ref[V_HI, v]
    e0 = sched_ref[V_GROUP, v] * EXP_PER_GROUP + j * MOE_EPS
    full = (hi - lo) * 4 >= TM_MOE * 3

    @pl.when(valid & (j == 0) & (sched_ref[V_FIRST, v] == 1))
    def _():
        acc_ref[...] = jnp.zeros_like(acc_ref)

    def expert_rows(r0, rows):
        w_in2 = jnp.concatenate(
            [w[k].astype(BF16) for k in range(MOE_EPS) for w in (wup_ref, wgate_ref)], axis=1)
        ag = _dot(hs_ref[pl.ds(r0, rows), :], w_in2)
        comb = ms_ref[pl.ds(r0, rows), :]
        lane = lax.broadcasted_iota(jnp.int32, comb.shape, 1)
        acts = []
        for k in range(MOE_EPS):
            a = ag[:, 2 * k * D_EXPERT:(2 * k + 1) * D_EXPERT]
            g = ag[:, (2 * k + 1) * D_EXPERT:(2 * k + 2) * D_EXPERT]
            cw = jnp.sum(jnp.where(lane == e0 + k, comb, 0.0), axis=-1, keepdims=True)
            acts.append(((g * jax.nn.sigmoid(g)) * a * cw).astype(BF16))
        w_out = jnp.concatenate([wdown_ref[k].astype(BF16) for k in range(MOE_EPS)], axis=0)
        acc_ref[pl.ds(r0, rows), :] += _dot(jnp.concatenate(acts, axis=1), w_out)

    @pl.when(valid & full)
    def _():
        expert_rows(0, TM_MOE)

    @pl.when(valid & jnp.logical_not(full))
    def _():
        def sub_block(s, carry):
            expert_rows(pl.multiple_of(s * MOE_SUB, MOE_SUB), MOE_SUB)
            return carry

        lax.fori_loop(lo // MOE_SUB, (hi + MOE_SUB - 1) // MOE_SUB, sub_block, 0)

    @pl.when(valid & (j == EXP_PER_GROUP // MOE_EPS - 1) & (sched_ref[V_LAST, v] == 1))
    def _():
        for c in range(N_SLAB):
            y3_ref[pl.ds(c, TM_MOE, stride=N_SLAB), :] = acc_ref[:, LANE * c:LANE * (c + 1)]


def _moe(sched, hs, ms, w_up, w_gate, w_down):
    steps = EXP_PER_GROUP // MOE_EPS
    wmap = lambda v, j, sched: (
        sched[V_GROUP, v] * steps + jnp.where(sched[V_VALID, v] == 1, j, steps - 1), 0, 0)
    tmap = lambda v, j, sched: (sched[V_TILE, v], 0)
    n_slab = D_MODEL // LANE
    return pl.pallas_call(
        _moe_kernel,
        grid_spec=pltpu.PrefetchScalarGridSpec(
            num_scalar_prefetch=1,
            grid=(N_VISITS, steps),
            in_specs=[
                pl.BlockSpec((TM_MOE, D_MODEL), tmap),
                pl.BlockSpec((TM_MOE, LANE), tmap),
                pl.BlockSpec((MOE_EPS, D_MODEL, D_EXPERT), wmap),
                pl.BlockSpec((MOE_EPS, D_MODEL, D_EXPERT), wmap),
                pl.BlockSpec((MOE_EPS, D_EXPERT, D_MODEL), wmap),
            ],
            out_specs=pl.BlockSpec((TM_MOE * n_slab, LANE), tmap),
            scratch_shapes=[pltpu.VMEM((TM_MOE, D_MODEL), F32)],
        ),
        out_shape=jax.ShapeDtypeStruct((T * n_slab, LANE), F32),
        compiler_params=pltpu.CompilerParams(
            dimension_semantics=("arbitrary", "arbitrary"), vmem_limit_bytes=VMEM_LIMIT),
        name="moe_grouped",
    )(sched, hs, ms, w_up, w_gate, w_down)


def _final_kernel(pos_ref, ys_ref, x1_ref, mod_ref, fn_ref, yp_ref, ysm_ref, g_ref):
    i = pl.program_id(0)

    def body(r, carry):
        g_ref[_slab(r), :] = ys_ref[_slab(pos_ref[0, r]), :]
        return carry

    lax.fori_loop(0, TM_FINAL, body, 0, unroll=8)
    mod = mod_ref[pl.ds(_mod_row(i, TM_FINAL), 1), :]
    gate2 = mod[:, 5 * D_MODEL:6 * D_MODEL]
    moe = jnp.concatenate([g_ref[pl.ds(c, TM_FINAL, stride=N_SLAB), :] for c in range(N_SLAB)], axis=1)
    x2 = x1_ref[...] + gate2 * moe
    y = (x2 * _rms(x2)) * fn_ref[...]
    is_sample = i >= T_P // TM_FINAL

    @pl.when(jnp.logical_not(is_sample))
    def _():
        yp_ref[...] = y

    @pl.when(is_sample)
    def _():
        ysm_ref[...] = y


def _final(pos, ys, x1, mod, final_norm):
    n_slab = D_MODEL // LANE
    pmap, smap = _stream_maps(TM_FINAL)
    return pl.pallas_call(
        _final_kernel,
        grid=(T // TM_FINAL,),
        in_specs=[
            pl.BlockSpec((None, 1, TM_FINAL), lambda i: (i, 0, 0), memory_space=pltpu.SMEM),
            pl.BlockSpec((T * n_slab, LANE), lambda i: (0, 0), pipeline_mode=pl.Buffered(1)),
            pl.BlockSpec((TM_FINAL, D_MODEL), lambda i: (i, 0)),
            pl.BlockSpec((N_COND, 6 * D_MODEL), lambda i: (0, 0)),
            pl.BlockSpec((1, D_MODEL), lambda i: (0, 0)),
        ],
        out_specs=[pl.BlockSpec((TM_FINAL, D_MODEL), pmap),
                   pl.BlockSpec((TM_FINAL, D_MODEL), smap)],
        scratch_shapes=[pltpu.VMEM((TM_FINAL * n_slab, LANE), F32)],
        out_shape=[jax.ShapeDtypeStruct((T_P, D_MODEL), F32),
                   jax.ShapeDtypeStruct((T_S, D_MODEL), F32)],
        compiler_params=pltpu.CompilerParams(
            dimension_semantics=("arbitrary",), vmem_limit_bytes=VMEM_LIMIT),
        name="moe_unsort_final",
    )(pos.reshape(T // TM_FINAL, 1, TM_FINAL), ys, x1, mod, final_norm)


def _plan_kernel(gr_ref, cnt_ref, pos_ref, sched_ref):
    n_post = T // TM_POST
    lane = lax.broadcasted_iota(jnp.int32, (1, LANE), 1)
    grp_lane = lambda v, g: v[:, N_EXPERTS + g:N_EXPERTS + g + 1]
    counts = [cnt_ref[k, 0:1, :] for k in range(n_post)]
    gtot_v = counts[0]
    for k in range(1, n_post):
        gtot_v = gtot_v + counts[k]
    gtot = [grp_lane(gtot_v, g) for g in range(N_GROUPS)]
    goff = [jnp.zeros((1, 1), F32)]
    for g in range(1, N_GROUPS):
        goff.append(goff[-1] + gtot[g - 1])

    before = [jnp.zeros((1, 1), F32) for _ in range(N_GROUPS)]
    rows_per_tile = TM_POST // LANE
    for k in range(n_post):
        gid = gr_ref[0:1, TM_POST * k:TM_POST * (k + 1)]
        pos = gr_ref[1:2, TM_POST * k:TM_POST * (k + 1)]
        for g in range(N_GROUPS):
            pos = pos + jnp.where(gid == float(g), goff[g] + before[g], 0.0)
            before[g] = before[g] + grp_lane(counts[k], g)
        for r in range(rows_per_tile):
            pos_ref[rows_per_tile * k + r:rows_per_tile * k + r + 1, :] = (
                pos[:, LANE * r:LANE * (r + 1)].astype(jnp.int32))

    zero = jnp.zeros((1, LANE), F32)
    rows = {name: zero for name in ("vt", "vg", "vlo", "vhi")}
    slot = jnp.zeros((1, 1), F32)
    last_t = jnp.zeros((1, 1), F32)
    last_g = jnp.zeros((1, 1), F32)
    for i in range(T // TM_MOE):
        for g in range(N_GROUPS):
            lo = jnp.clip(goff[g] - float(TM_MOE * i), 0.0, float(TM_MOE))
            hi = jnp.clip(goff[g] + gtot[g] - float(TM_MOE * i), 0.0, float(TM_MOE))
            ok = hi > lo
            here = ok & (lane == slot.astype(jnp.int32))
            rows["vt"] = jnp.where(here, float(i), rows["vt"])
            rows["vg"] = jnp.where(here, float(g), rows["vg"])
            rows["vlo"] = jnp.where(here, lo, rows["vlo"])
            rows["vhi"] = jnp.where(here, hi, rows["vhi"])
            last_t = jnp.where(ok, float(i), last_t)
            last_g = jnp.where(ok, float(g), last_g)
            slot = slot + jnp.where(ok, 1.0, 0.0)
    n_ok = slot.astype(jnp.int32)
    valid = lane < n_ok
    vt = jnp.where(valid, rows["vt"], last_t)
    vg = jnp.where(valid, rows["vg"], last_g)
    prev_t = pltpu.roll(jnp.broadcast_to(vt, (8, LANE)), 1, 1)[0:1, :]
    next_t = pltpu.roll(jnp.broadcast_to(vt, (8, LANE)), LANE - 1, 1)[0:1, :]
    first = jnp.where((lane == 0) | (vt != prev_t), 1.0, 0.0)
    last = jnp.where((lane == n_ok - 1) | (vt != next_t), 1.0, 0.0)
    table = [vt, vg, rows["vlo"], rows["vhi"], first, last, jnp.where(valid, 1.0, 0.0), zero]
    for r, row in enumerate(table):
        sched_ref[r:r + 1, :] = row.astype(jnp.int32)


V_TILE, V_GROUP, V_LO, V_HI, V_FIRST, V_LAST, V_VALID = range(7)


def _plan(gr, cnt):
    pos, sched = pl.pallas_call(
        _plan_kernel,
        out_shape=[jax.ShapeDtypeStruct((T // LANE, LANE), jnp.int32),
                   jax.ShapeDtypeStruct((8, LANE), jnp.int32)],
        name="moe_plan",
    )(gr, cnt)
    return pos.reshape(T), sched


def _rope_tables():
    n = np.arange(DEC_SEQ)
    pos = np.stack([n // GRID_W, n % GRID_W], axis=1).astype(np.float32)
    half = ROPE_AXIS // 2
    inv = (1.0 / (ROPE_BASE ** (np.arange(0, ROPE_AXIS, 2, dtype=np.float32) / ROPE_AXIS))).astype(np.float32)
    ang = (pos[:, :, None] * inv[None, None, :]).astype(np.float32)
    cos = np.cos(ang).astype(np.float32)
    sin = np.sin(ang).astype(np.float32)
    tabs = np.zeros((2, 3, DEC_SEQ, LANE), np.float32)
    tabs[:, 0] = 1.0
    for a in range(2):
        lo = ROPE_LANE0 + a * ROPE_AXIS
        tabs[1, 0, :, lo:lo + half] = cos[:, a]
        tabs[1, 0, :, lo + half:lo + 2 * half] = cos[:, a]
        tabs[1, 1, :, lo + half:lo + 2 * half] = sin[:, a]
        tabs[1, 2, :, lo:lo + half] = -sin[:, a]
    return jnp.asarray(tabs)


def kernel(x_prompt, x_sample, cache_ckv, cache_krope, c, c_ctx, norm1, w_ada, b_ada, w_in, conv_w,
           w_conv_out, q_norm, w_uq, kv_norm, w_ukv, w_o, w_mix_out, norm2, w_grp, w_exp, w_up,
           w_gate, w_down, final_norm):
    l = 0
    xp = x_prompt.reshape(T_P, D_MODEL)
    xs = x_sample.reshape(T_S, D_MODEL)
    mod = _ada(c_ctx[None, :], c, w_ada[l], b_ada[l][None, :])

    w_in_t = w_in[l].T
    w_uq_slot = jnp.pad(w_uq[l].reshape(Q_LORA, N_HEADS, QK_NOPE + QK_ROPE),
                        ((0, 0), (0, 0), (0, LANE - QK_NOPE - QK_ROPE))
                        ).reshape(Q_LORA, N_HEADS * LANE).astype(BF16)
    wkv = w_ukv[l].reshape(KV_LORA, N_HEADS, QK_NOPE + V_HEAD)
    wk_slot = jnp.pad(wkv[:, :, :QK_NOPE], ((0, 0), (0, 0), (0, LANE - QK_NOPE)))
    wv = wkv[:, :, QK_NOPE:].reshape(KV_LORA, N_HEADS // 2, 2, V_HEAD)
    zero = jnp.zeros_like(wv[:, :, 0])
    wv_slot = jnp.stack([jnp.concatenate([wv[:, :, 0], zero], axis=-1),
                         jnp.concatenate([zero, wv[:, :, 1]], axis=-1)], axis=2)
    w_ukv_slot = jnp.concatenate([wk_slot.reshape(KV_LORA, N_HEADS * LANE),
                                  wv_slot.reshape(KV_LORA, N_HEADS * LANE)], axis=1).astype(BF16)
    w_route = jnp.pad(jnp.concatenate([w_exp[l], w_grp[l]], axis=1),
                      ((0, 0), (0, LANE - N_EXPERTS - N_GROUPS)))
    cache_krs = jnp.pad(cache_krope[:, l], ((0, 0), (0, 0), (ROPE_LANE0, LANE - ROPE_LANE0 - QK_ROPE)))

    zc, q, ckv, krs, nckv, nkr = _inproj(xp, xs, mod, norm1[l][None, :], w_in_t, conv_w[l],
                              q_norm[l][None, :], kv_norm[l][None, :], w_uq_slot, _rope_tables())
    o_p, o_s = _attention(q, ckv, krs, cache_ckv[:, l], cache_krs, w_ukv_slot)
    x1, h3, meta, cnt, gr = _post(xp, xs, mod, norm1[l][None, :], w_in_t, zc, o_p, o_s, w_conv_out[l],
                                  w_o[l], w_mix_out[l], norm2[l][None, :], w_route)
    pos, sched = _plan(gr, cnt)
    hs, ms = _dispatch(pos, h3, meta)
    ys = _moe(sched, hs, ms, w_up[l], w_gate[l], w_down[l])
    yp, ysm = _final(pos, ys, x1, mod, final_norm[None, :])

    y_prompt = yp.reshape(BATCH, SEQ, D_MODEL)
    y_sample = ysm.reshape(DEC_BATCH, DEC_SEQ, D_MODEL)
    new_ckv = nckv.reshape(BATCH, 1, SEQ, KV_LORA)
    new_krope = jnp.swapaxes(nkr, 1, 2).reshape(BATCH, 1, SEQ, QK_ROPE)
    return (y_prompt, y_sample, new_ckv, new_krope)
```

```python
import numpy as np
import jax
import jax.numpy as jnp
from jax import lax
from jax.experimental import pallas as pl
from jax.experimental.pallas import tpu as pltpu

F32 = jnp.float32
BF16 = jnp.bfloat16

D_MODEL = 1024
BATCH = 16
SEQ = 256
DEC_BATCH = 2
DEC_SEQ = 1024
PAST_LEN = 256
GRID_W = 64
N_HEADS = 8
QK_NOPE = 64
QK_ROPE = 32
V_HEAD = 64
Q_LORA = 256
KV_LORA = 128
ROPE_AXIS = QK_ROPE // 2
ROPE_BASE = 10000.0
ATTN_SCALE = (QK_NOPE + QK_ROPE) ** -0.5
D_CONV = D_MODEL
N_GROUPS = 4
EXP_PER_GROUP = 8
N_EXPERTS = N_GROUPS * EXP_PER_GROUP
D_EXPERT = 256
EPS = 1e-6

T_P = BATCH * SEQ
T_S = DEC_BATCH * DEC_SEQ
T = T_P + T_S
N_COND = 8
LANE = 128
ROPE_LANE0 = QK_NOPE
SMALL_COLS = Q_LORA + KV_LORA + LANE
VMEM_LIMIT = 56 * 1024 * 1024

TM_IN = 1024
TM_POST = 512
TM_MOE = T // N_GROUPS
TM_FINAL = 512
TM_DISP = 1024
N_SLAB = D_MODEL // LANE
MOE_SUB = 256
MOE_EPS = 2
N_VISITS = T // TM_MOE + N_GROUPS - 1
GID_LANE = 40
RANK_LANE = 41
Q_BLK_S = 1024
PROMPT_SEQS = 2
CONV_CHUNK = 256


def _dot(a, b):
    return jnp.dot(a, b, preferred_element_type=F32)


def _rms(x):
    return lax.rsqrt(jnp.mean(x * x, axis=-1, keepdims=True) + EPS)


def _slab(t):
    return pl.ds(pl.multiple_of(t * N_SLAB, N_SLAB), N_SLAB)


def _mod_row(i, tm):
    n_prompt = T_P // tm
    return jnp.where(i >= n_prompt, 1 + ((i - n_prompt) * tm) // DEC_SEQ, 0)


def _ada_kernel(cctx_ref, c_ref, w_ref, b_ref, o_ref):
    c = jnp.concatenate([cctx_ref[...], c_ref[...],
                         jnp.zeros((N_COND - 1 - DEC_BATCH, D_MODEL), F32)], axis=0)
    a = (c * jax.nn.sigmoid(c)).astype(BF16)
    o_ref[...] = _dot(a, w_ref[...].astype(BF16)) + b_ref[...]


def _ada(c_ctx, c, w_ada, b_ada):
    n = 6 * D_MODEL
    bn = 1536
    return pl.pallas_call(
        _ada_kernel,
        grid=(n // bn,),
        in_specs=[
            pl.BlockSpec((1, D_MODEL), lambda j: (0, 0)),
            pl.BlockSpec((DEC_BATCH, D_MODEL), lambda j: (0, 0)),
            pl.BlockSpec((D_MODEL, bn), lambda j: (0, j)),
            pl.BlockSpec((1, bn), lambda j: (0, j)),
        ],
        out_specs=pl.BlockSpec((N_COND, bn), lambda j: (0, j)),
        out_shape=jax.ShapeDtypeStruct((N_COND, n), F32),
        compiler_params=pltpu.CompilerParams(
            dimension_semantics=("parallel",), vmem_limit_bytes=VMEM_LIMIT),
        name="ada_mod",
    )(c_ctx, c, w_ada, b_ada)


O_CQ = 3 * D_CONV
O_KR = O_CQ + Q_LORA + KV_LORA
O_GATE = O_KR + QK_ROPE
SMALL_BLK = 512
NT = (((1,), (1,)), ((), ()))


def _dot_nt(a, bt):
    return lax.dot_general(a, bt, NT, preferred_element_type=F32)


def _stream_maps(tm):
    n_prompt = T_P // tm
    return (lambda i, *_: (jnp.minimum(i, n_prompt - 1), 0),
            lambda i, *_: (jnp.maximum(i - n_prompt, 0), 0))


def _inproj_kernel(xp_ref, xs_ref, mod_ref, n1_ref, wc_ref, ws_ref, cw_ref, qn_ref, kvn_ref, wuq_ref,
                   rope_ref, zc_ref, q_ref, ckv_ref, krs_ref, nckv_ref, nkr_ref):
    i = pl.program_id(0)
    is_sample = i >= T_P // TM_IN
    seq = jnp.where(is_sample, DEC_SEQ, SEQ)
    mod = mod_ref[pl.ds(_mod_row(i, TM_IN), 1), :]
    shift1 = mod[:, 0:D_MODEL]
    scale1 = mod[:, D_MODEL:2 * D_MODEL]
    x = jnp.where(is_sample, xs_ref[...], xp_ref[...])
    h = ((x * _rms(x)) * n1_ref[...]) * (1.0 + scale1) + shift1
    hb = h.astype(BF16)

    n_lat = Q_LORA + KV_LORA
    w_small = jnp.concatenate([
        ws_ref[0:n_lat, :].astype(BF16),
        jnp.zeros((ROPE_LANE0, D_MODEL), BF16),
        ws_ref[n_lat:n_lat + QK_ROPE, :].astype(BF16),
        jnp.zeros((LANE - ROPE_LANE0 - QK_ROPE, D_MODEL), BF16)], axis=0)
    sm = _dot_nt(hb, w_small)
    cq = sm[:, 0:Q_LORA]
    ckv_raw = sm[:, Q_LORA:Q_LORA + KV_LORA]
    krs = sm[:, Q_LORA + KV_LORA:SMALL_COLS]
    cqn = (cq * _rms(cq)) * qn_ref[...]
    q = _dot(cqn.astype(BF16), wuq_ref[...])
    ckv = (ckv_raw * _rms(ckv_raw)) * kvn_ref[...]
    ckv_ref[...] = ckv

    @pl.when(jnp.logical_not(is_sample))
    def _():
        nckv_ref[...] = ckv
        for s in range(TM_IN // SEQ):
            kt = krs[s * SEQ:(s + 1) * SEQ, :].T
            nkr_ref[s] = kt[ROPE_LANE0:ROPE_LANE0 + QK_ROPE, :]

    cos = rope_ref[0]
    sin_lo = rope_ref[1]
    sin_hi = rope_ref[2]

    def rot(v):
        return v * cos + pltpu.roll(v, 8, 1) * sin_lo + pltpu.roll(v, LANE - 8, 1) * sin_hi

    krs_ref[...] = rot(krs)
    for hh in range(N_HEADS):
        q_ref[:, LANE * hh:LANE * (hh + 1)] = rot(q[:, LANE * hh:LANE * (hh + 1)]).astype(BF16)

    pos = lax.broadcasted_iota(jnp.int32, (TM_IN, 1), 0) & (seq - 1)
    first = pos == 0
    last = pos == seq - 1
    for j in range(D_CONV // CONV_CHUNK):
        c0 = j * CONV_CHUNK
        bg = _dot_nt(hb, wc_ref[c0:c0 + CONV_CHUNK, :].astype(BF16))
        cg = _dot_nt(hb, wc_ref[D_CONV + c0:D_CONV + c0 + CONV_CHUNK, :].astype(BF16))
        ui = _dot_nt(hb, wc_ref[2 * D_CONV + c0:2 * D_CONV + c0 + CONV_CHUNK, :].astype(BF16))
        u = cg * ui
        u_prev = jnp.where(first, 0.0, pltpu.roll(u, 1, 0))
        u_next = jnp.where(last, 0.0, pltpu.roll(u, TM_IN - 1, 0))
        cw = cw_ref[:, c0:c0 + CONV_CHUNK]
        conv = u_prev * cw[0:1] + u * cw[1:2] + u_next * cw[2:3]
        zc_ref[:, c0:c0 + CONV_CHUNK] = (bg * conv).astype(BF16)


def _inproj(xp, xs, mod, norm1, w_in_t, conv_w, q_norm, kv_norm, w_uq_slot, rope_tabs):
    n_prompt = T_P // TM_IN
    const = lambda i: (0, 0)
    pmap, smap = _stream_maps(TM_IN)
    once = pl.Buffered(1)
    return pl.pallas_call(
        _inproj_kernel,
        grid=(T // TM_IN,),
        in_specs=[
            pl.BlockSpec((TM_IN, D_MODEL), pmap),
            pl.BlockSpec((TM_IN, D_MODEL), smap),
            pl.BlockSpec((N_COND, 6 * D_MODEL), const),
            pl.BlockSpec((1, D_MODEL), const),
            pl.BlockSpec((O_CQ, D_MODEL), const, pipeline_mode=once),
            pl.BlockSpec((SMALL_BLK, D_MODEL), lambda i: (O_CQ // SMALL_BLK, 0), pipeline_mode=once),
            pl.BlockSpec((3, D_CONV), const),
            pl.BlockSpec((1, Q_LORA), const),
            pl.BlockSpec((1, KV_LORA), const),
            pl.BlockSpec((Q_LORA, N_HEADS * LANE), const),
            pl.BlockSpec((None, 3, TM_IN, LANE),
                         lambda i: (jnp.where(i >= n_prompt, 1, 0), 0, 0, 0)),
        ],
        out_specs=[
            pl.BlockSpec((TM_IN, D_CONV), lambda i: (i, 0)),
            pl.BlockSpec((TM_IN, N_HEADS * LANE), lambda i: (i, 0)),
            pl.BlockSpec((TM_IN, KV_LORA), lambda i: (i, 0)),
            pl.BlockSpec((TM_IN, LANE), lambda i: (i, 0)),
            pl.BlockSpec((TM_IN, KV_LORA), pmap),
            pl.BlockSpec((TM_IN // SEQ, QK_ROPE, SEQ), lambda i: (jnp.minimum(i, n_prompt - 1), 0, 0)),
        ],
        out_shape=[
            jax.ShapeDtypeStruct((T, D_CONV), BF16),
            jax.ShapeDtypeStruct((T, N_HEADS * LANE), BF16),
            jax.ShapeDtypeStruct((T, KV_LORA), F32),
            jax.ShapeDtypeStruct((T, LANE), F32),
            jax.ShapeDtypeStruct((T_P, KV_LORA), F32),
            jax.ShapeDtypeStruct((BATCH, QK_ROPE, SEQ), F32),
        ],
        compiler_params=pltpu.CompilerParams(
            dimension_semantics=("arbitrary",), vmem_limit_bytes=VMEM_LIMIT),
        name="in_proj",
    )(xp, xs, mod, norm1, w_in_t, w_in_t, conv_w, q_norm, kv_norm, w_uq_slot, rope_tabs)


def _fill_kv(ckv, krs, wukv_ref, kf_scr, v_scr, off):
    m = ckv.shape[0]
    kv = _dot(ckv.astype(BF16), wukv_ref[...])
    for hh in range(N_HEADS):
        kf_scr[hh, off:off + m, :] = (kv[:, LANE * hh:LANE * (hh + 1)] + krs).astype(BF16)
    v_scr[off:off + m, :] = kv[:, N_HEADS * LANE:].astype(BF16)


def _attend(q_ref, r0, rows, kf_scr, v_scr, o_ref):
    for pair in range(N_HEADS // 2):
        acc = None
        for hh in (2 * pair, 2 * pair + 1):
            qh = q_ref[r0:r0 + rows, LANE * hh:LANE * (hh + 1)]
            s = _dot_nt(qh, kf_scr[hh]) * ATTN_SCALE
            e = jnp.exp(s - jnp.max(s, axis=-1, keepdims=True))
            p = (e / jnp.sum(e, axis=-1, keepdims=True)).astype(BF16)
            part = _dot(p, v_scr[:, LANE * hh:LANE * (hh + 1)])
            acc = part if acc is None else acc + part
        o_ref[r0:r0 + rows, LANE * pair:LANE * (pair + 1)] = acc.astype(BF16)


def _attn_prompt_kernel(q_ref, ckv_ref, krs_ref, wukv_ref, o_ref):
    for s in range(PROMPT_SEQS):
        r0 = s * SEQ
        kv = _dot(ckv_ref[r0:r0 + SEQ, :].astype(BF16), wukv_ref[...])
        krs = krs_ref[r0:r0 + SEQ, :]
        for pair in range(N_HEADS // 2):
            acc = None
            for hh in (2 * pair, 2 * pair + 1):
                kf = (kv[:, LANE * hh:LANE * (hh + 1)] + krs).astype(BF16)
                vh = kv[:, LANE * (N_HEADS + hh):LANE * (N_HEADS + hh + 1)].astype(BF16)
                sc = _dot_nt(q_ref[r0:r0 + SEQ, LANE * hh:LANE * (hh + 1)], kf) * ATTN_SCALE
                e = jnp.exp(sc - jnp.max(sc, axis=-1, keepdims=True))
                p = (e / jnp.sum(e, axis=-1, keepdims=True)).astype(BF16)
                part = _dot(p, vh)
                acc = part if acc is None else acc + part
            o_ref[r0:r0 + SEQ, LANE * pair:LANE * (pair + 1)] = acc.astype(BF16)


def _attn_sample_kernel(q_ref, ckv_ref, cckv_ref, krs_ref, ckrs_ref, wukv_ref, o_ref, kf_scr, v_scr):
    @pl.when(pl.program_id(1) == 0)
    def _():
        _fill_kv(ckv_ref[...], krs_ref[...], wukv_ref, kf_scr, v_scr, 0)
        _fill_kv(cckv_ref[...], ckrs_ref[...], wukv_ref, kf_scr, v_scr, DEC_SEQ)

    _attend(q_ref, 0, Q_BLK_S, kf_scr, v_scr, o_ref)


def _attention(q, ckv, krs, cache_ckv, cache_krs, w_ukv_slot):
    kv_cols = 2 * N_HEADS * LANE
    n_o = N_HEADS * V_HEAD
    steps = BATCH // PROMPT_SEQS
    rows = PROMPT_SEQS * SEQ
    o_prompt = pl.pallas_call(
        _attn_prompt_kernel,
        grid=(steps,),
        in_specs=[
            pl.BlockSpec((rows, N_HEADS * LANE), lambda b: (b, 0)),
            pl.BlockSpec((rows, KV_LORA), lambda b: (b, 0)),
            pl.BlockSpec((rows, LANE), lambda b: (b, 0)),
            pl.BlockSpec((KV_LORA, kv_cols), lambda b: (0, 0)),
        ],
        out_specs=pl.BlockSpec((rows, n_o), lambda b: (b, 0)),
        out_shape=jax.ShapeDtypeStruct((T_P, n_o), BF16),
        compiler_params=pltpu.CompilerParams(dimension_semantics=("parallel",),
                                             vmem_limit_bytes=VMEM_LIMIT),
        name="attn_prompt",
    )(q, ckv, krs, w_ukv_slot)

    m_all = DEC_SEQ + PAST_LEN
    nq = DEC_SEQ // Q_BLK_S
    q0 = T_P // Q_BLK_S
    s0 = T_P // DEC_SEQ
    o_sample = pl.pallas_call(
        _attn_sample_kernel,
        grid=(DEC_BATCH, nq),
        in_specs=[
            pl.BlockSpec((Q_BLK_S, N_HEADS * LANE), lambda b, j: (q0 + b * nq + j, 0)),
            pl.BlockSpec((DEC_SEQ, KV_LORA), lambda b, j: (s0 + b, 0)),
            pl.BlockSpec((None, PAST_LEN, KV_LORA), lambda b, j: (b, 0, 0)),
            pl.BlockSpec((DEC_SEQ, LANE), lambda b, j: (s0 + b, 0)),
            pl.BlockSpec((None, PAST_LEN, LANE), lambda b, j: (b, 0, 0)),
            pl.BlockSpec((KV_LORA, kv_cols), lambda b, j: (0, 0)),
        ],
        out_specs=pl.BlockSpec((Q_BLK_S, n_o), lambda b, j: (b * nq + j, 0)),
        out_shape=jax.ShapeDtypeStruct((T_S, n_o), BF16),
        scratch_shapes=[pltpu.VMEM((N_HEADS, m_all, LANE), BF16),
                        pltpu.VMEM((m_all, N_HEADS * LANE), BF16)],
        compiler_params=pltpu.CompilerParams(dimension_semantics=("parallel", "arbitrary"),
                                             vmem_limit_bytes=VMEM_LIMIT),
        name="attn_sample",
    )(q, ckv, cache_ckv, krs, cache_krs, w_ukv_slot)
    return o_prompt, o_sample


def _route(logits):
    lane = lax.broadcasted_iota(jnp.int32, logits.shape, 1)
    neg = -jnp.inf
    big = jnp.int32(1 << 20)
    gmask = (lane >= N_EXPERTS) & (lane < N_EXPERTS + N_GROUPS)
    gl = jnp.where(gmask, logits, neg)
    gmax = jnp.max(gl, axis=-1, keepdims=True)
    gsum = jnp.sum(jnp.where(gmask, jnp.exp(gl - gmax), 0.0), axis=-1, keepdims=True)
    p_g = 1.0 / gsum
    g_idx = jnp.min(jnp.where(gl == gmax, lane, big), axis=-1, keepdims=True) - N_EXPERTS

    emask = (lane < N_EXPERTS) & ((lane >> 3) == g_idx)
    el = jnp.where(emask, logits, neg)
    m1 = jnp.max(el, axis=-1, keepdims=True)
    i1 = jnp.min(jnp.where(el == m1, lane, big), axis=-1, keepdims=True)
    el2 = jnp.where(lane == i1, neg, el)
    m2 = jnp.max(el2, axis=-1, keepdims=True)
    i2 = jnp.min(jnp.where(el2 == m2, lane, big), axis=-1, keepdims=True)
    z = jnp.sum(jnp.where(emask, jnp.exp(el - m1), 0.0), axis=-1, keepdims=True)
    p1 = 1.0 / z
    p2 = jnp.exp(m2 - m1) / z
    tot = p1 + p2
    w1 = p_g * p1 / tot
    w2 = p_g * p2 / tot
    return jnp.where(lane == i1, w1, 0.0) + jnp.where(lane == i2, w2, 0.0), g_idx


def _post_kernel(xp_ref, xs_ref, mod_ref, n1_ref, wg_ref, zc_ref, op_ref, os_ref, wco_ref, wo_ref,
                 wmix_ref, n2_ref, wr_ref, x1_ref, h3_ref, meta_ref, cnt_ref, gr_ref):
    i = pl.program_id(0)
    is_sample = i >= T_P // TM_POST
    x = jnp.where(is_sample, xs_ref[...], xp_ref[...])
    o = jnp.where(is_sample, os_ref[...], op_ref[...])
    mod = mod_ref[pl.ds(_mod_row(i, TM_POST), 1), :]
    shift1 = mod[:, 0:D_MODEL]
    scale1 = mod[:, D_MODEL:2 * D_MODEL]
    gate1 = mod[:, 2 * D_MODEL:3 * D_MODEL]
    shift2 = mod[:, 3 * D_MODEL:4 * D_MODEL]
    scale2 = mod[:, 4 * D_MODEL:5 * D_MODEL]
    y_conv = _dot(zc_ref[...], wco_ref[...].astype(BF16))
    y_mla = _dot(o, wo_ref[...].astype(BF16))
    h = ((x * _rms(x)) * n1_ref[...]) * (1.0 + scale1) + shift1
    g = _dot_nt(h.astype(BF16), wg_ref[...].astype(BF16))
    merged = (jax.nn.sigmoid(g[:, 0:D_MODEL]) * y_conv
              + jax.nn.sigmoid(g[:, D_MODEL:2 * D_MODEL]) * y_mla)
    y = _dot(merged.astype(BF16), wmix_ref[...].astype(BF16))
    x1 = x + gate1 * y
    x1_ref[...] = x1
    h2 = ((x1 * _rms(x1)) * n2_ref[...]) * (1.0 + scale2) + shift2
    h2_hi = h2.astype(BF16)
    h2_lo = (h2 - h2_hi.astype(F32)).astype(BF16)
    hh = _dot(h2_hi, wr_ref[...])
    logits = hh[:, 0:LANE] + hh[:, LANE:2 * LANE] + _dot(h2_lo, wr_ref[:, 0:LANE])
    comb, g_idx = _route(logits)

    lane = lax.broadcasted_iota(jnp.int32, comb.shape, 1)
    onehot = lane == g_idx + N_EXPERTS
    r_i = lax.broadcasted_iota(jnp.int32, (TM_POST, TM_POST), 0)
    c_i = lax.broadcasted_iota(jnp.int32, (TM_POST, TM_POST), 1)
    lower = jnp.where(c_i < r_i, 1.0, 0.0).astype(BF16)
    before = _dot(lower, jnp.where(onehot, 1.0, 0.0).astype(BF16))
    rank = jnp.sum(jnp.where(onehot, before, 0.0), axis=-1, keepdims=True)
    counts = jnp.sum(jnp.where(onehot, 1.0, 0.0), axis=0, keepdims=True)
    cnt_ref[...] = jnp.broadcast_to(counts, cnt_ref.shape)

    meta_ref[...] = comb
    idx = jnp.where(lane == GID_LANE, g_idx.astype(F32), 0.0) + jnp.where(lane == RANK_LANE, rank, 0.0)
    idx_hi = idx.astype(BF16)
    idx_lo = (idx - idx_hi.astype(F32)).astype(BF16)
    s_row = lax.broadcasted_iota(jnp.int32, (8, LANE), 0)
    s_lane = lax.broadcasted_iota(jnp.int32, (8, LANE), 1)
    sel = jnp.where(s_lane == GID_LANE + s_row, 1.0, 0.0).astype(BF16)
    gr_ref[...] = _dot_nt(sel, idx_hi) + _dot_nt(sel, idx_lo)
    for c in range(N_SLAB):
        h3_ref[pl.ds(c, TM_POST, stride=N_SLAB), :] = h2[:, LANE * c:LANE * (c + 1)]


def _post(xp, xs, mod, norm1, w_in_t, zc, o_p, o_s, w_conv_out, w_o, w_mix_out, norm2, w_route):
    const = lambda i: (0, 0)
    row = lambda i: (i, 0)
    pmap, smap = _stream_maps(TM_POST)
    once = pl.Buffered(1)
    return pl.pallas_call(
        _post_kernel,
        grid=(T // TM_POST,),
        in_specs=[
            pl.BlockSpec((TM_POST, D_MODEL), pmap),
            pl.BlockSpec((TM_POST, D_MODEL), smap),
            pl.BlockSpec((N_COND, 6 * D_MODEL), const),
            pl.BlockSpec((1, D_MODEL), const),
            pl.BlockSpec((pl.Element(2 * D_MODEL), pl.Element(D_MODEL)), lambda i: (O_GATE, 0),
                         pipeline_mode=once),
            pl.BlockSpec((TM_POST, D_CONV), row),
            pl.BlockSpec((TM_POST, N_HEADS * V_HEAD), pmap),
            pl.BlockSpec((TM_POST, N_HEADS * V_HEAD), smap),
            pl.BlockSpec((D_CONV, D_MODEL), const, pipeline_mode=once),
            pl.BlockSpec((N_HEADS * V_HEAD, D_MODEL), const, pipeline_mode=once),
            pl.BlockSpec((D_MODEL, D_MODEL), const, pipeline_mode=once),
            pl.BlockSpec((1, D_MODEL), const),
            pl.BlockSpec((D_MODEL, 2 * LANE), const),
        ],
        out_specs=[
            pl.BlockSpec((TM_POST, D_MODEL), row),
            pl.BlockSpec((TM_POST * N_SLAB, LANE), row),
            pl.BlockSpec((TM_POST, LANE), row),
            pl.BlockSpec((None, 8, LANE), lambda i: (i, 0, 0)),
            pl.BlockSpec((8, TM_POST), lambda i: (0, i)),
        ],
        out_shape=[
            jax.ShapeDtypeStruct((T, D_MODEL), F32),
            jax.ShapeDtypeStruct((T * N_SLAB, LANE), F32),
            jax.ShapeDtypeStruct((T, LANE), F32),
            jax.ShapeDtypeStruct((T // TM_POST, 8, LANE), F32),
            jax.ShapeDtypeStruct((8, T), F32),
        ],
        compiler_params=pltpu.CompilerParams(
            dimension_semantics=("parallel",), vmem_limit_bytes=VMEM_LIMIT),
        name="post_mixer",
    )(xp, xs, mod, norm1, w_in_t, zc, o_p, o_s, w_conv_out, w_o, w_mix_out, norm2, w_route)


DISP_IN = T // TM_DISP
DISP_OUT = T // TM_MOE


def _dispatch_kernel(pos_ref, h3_ref, m_ref, hs_ref, ms_ref, xs_ref, mss_ref):
    i = pl.program_id(0)

    @pl.when(i < DISP_IN)
    def _():
        def body(r, carry):
            p = pos_ref[0, r]
            xs_ref[_slab(p), :] = h3_ref[_slab(r), :]
            mss_ref[pl.ds(p, 1), :] = m_ref[pl.ds(r, 1), :]
            return carry

        lax.fori_loop(0, TM_DISP, body, 0, unroll=8)

    @pl.when(i >= DISP_IN)
    def _():
        row0 = pl.multiple_of((i - DISP_IN) * TM_MOE, TM_MOE)
        for c in range(N_SLAB):
            hs_ref[:, LANE * c:LANE * (c + 1)] = (
                xs_ref[pl.ds(row0 * N_SLAB + c, TM_MOE, stride=N_SLAB), :].astype(BF16))
        ms_ref[...] = mss_ref[pl.ds(row0, TM_MOE), :]


def _dispatch(pos, h3, meta):
    n_slab = N_SLAB
    in_map = lambda i: (jnp.minimum(i, DISP_IN - 1), 0)
    out_map = lambda i: (jnp.maximum(i - DISP_IN, 0), 0)
    return pl.pallas_call(
        _dispatch_kernel,
        grid=(DISP_IN + DISP_OUT,),
        in_specs=[pl.BlockSpec((None, 1, TM_DISP), lambda i: (jnp.minimum(i, DISP_IN - 1), 0, 0),
                               memory_space=pltpu.SMEM),
                  pl.BlockSpec((TM_DISP * n_slab, LANE), in_map),
                  pl.BlockSpec((TM_DISP, LANE), in_map)],
        out_specs=[pl.BlockSpec((TM_MOE, D_MODEL), out_map),
                   pl.BlockSpec((TM_MOE, LANE), out_map)],
        scratch_shapes=[pltpu.VMEM((T * n_slab, LANE), F32),
                        pltpu.VMEM((T, LANE), F32)],
        out_shape=[jax.ShapeDtypeStruct((T, D_MODEL), BF16),
                   jax.ShapeDtypeStruct((T, LANE), F32)],
        compiler_params=pltpu.CompilerParams(
            dimension_semantics=("arbitrary",), vmem_limit_bytes=VMEM_LIMIT),
        name="moe_dispatch",
    )(pos.reshape(DISP_IN, 1, TM_DISP), h3, meta)


def _moe_kernel(sched_ref, hs_ref, ms_ref, wup_ref, wgate_ref, wdown_ref, y3_ref, acc_ref):
    v = pl.program_id(0)
    j = pl.program_id(1)
    valid = sched_ref[V_VALID, v] == 1
    lo = sched_ref[V_LO, v]
    hi = sched_ref[V_HI, v]
    e0 = sched_ref[V_GROUP, v] * EXP_PER_GROUP + j * MOE_EPS
    full = (hi - lo) * 4 >= TM_MOE * 3

    @pl.when(valid & (j == 0) & (sched_ref[V_FIRST, v] == 1))
    def _():
        acc_ref[...] = jnp.zeros_like(acc_ref)

    def expert_rows(r0, rows):
        w_in2 = jnp.concatenate(
            [w[k].astype(BF16) for k in range(MOE_EPS) for w in (wup_ref, wgate_ref)], axis=1)
        ag = _dot(hs_ref[pl.ds(r0, rows), :], w_in2)
        comb = ms_ref[pl.ds(r0, rows), :]
        lane = lax.broadcasted_iota(jnp.int32, comb.shape, 1)
        acts = []
        for k in range(MOE_EPS):
            a = ag[:, 2 * k * D_EXPERT:(2 * k + 1) * D_EXPERT]
            g = ag[:, (2 * k + 1) * D_EXPERT:(2 * k + 2) * D_EXPERT]
            cw = jnp.sum(jnp.where(lane == e0 + k, comb, 0.0), axis=-1, keepdims=True)
            acts.append(((g * jax.nn.sigmoid(g)) * a * cw).astype(BF16))
        w_out = jnp.concatenate([wdown_ref[k].astype(BF16) for k in range(MOE_EPS)], axis=0)
        acc_ref[pl.ds(r0, rows), :] += _dot(jnp.concatenate(acts, axis=1), w_out)

    @pl.when(valid & full)
    def _():
        expert_rows(0, TM_MOE)

    @pl.when(valid & jnp.logical_not(full))
    def _():
        def sub_block(s, carry):
            expert_rows(pl.multiple_of(s * MOE_SUB, MOE_SUB), MOE_SUB)
            return carry

        lax.fori_loop(lo // MOE_SUB, (hi + MOE_SUB - 1) // MOE_SUB, sub_block, 0)

    @pl.when(valid & (j == EXP_PER_GROUP // MOE_EPS - 1) & (sched_ref[V_LAST, v] == 1))
    def _():
        for c in range(N_SLAB):
            y3_ref[pl.ds(c, TM_MOE, stride=N_SLAB), :] = acc_ref[:, LANE * c:LANE * (c + 1)]


def _moe(sched, hs, ms, w_up, w_gate, w_down):
    steps = EXP_PER_GROUP // MOE_EPS
    wmap = lambda v, j, sched: (
        sched[V_GROUP, v] * steps + jnp.where(sched[V_VALID, v] == 1, j, steps - 1), 0, 0)
    tmap = lambda v, j, sched: (sched[V_TILE, v], 0)
    n_slab = D_MODEL // LANE
    return pl.pallas_call(
        _moe_kernel,
        grid_spec=pltpu.PrefetchScalarGridSpec(
            num_scalar_prefetch=1,
            grid=(N_VISITS, steps),
            in_specs=[
                pl.BlockSpec((TM_MOE, D_MODEL), tmap),
                pl.BlockSpec((TM_MOE, LANE), tmap),
                pl.BlockSpec((MOE_EPS, D_MODEL, D_EXPERT), wmap),
                pl.BlockSpec((MOE_EPS, D_MODEL, D_EXPERT), wmap),
                pl.BlockSpec((MOE_EPS, D_EXPERT, D_MODEL), wmap),
            ],
            out_specs=pl.BlockSpec((TM_MOE * n_slab, LANE), tmap),
            scratch_shapes=[pltpu.VMEM((TM_MOE, D_MODEL), F32)],
        ),
        out_shape=jax.ShapeDtypeStruct((T * n_slab, LANE), F32),
        compiler_params=pltpu.CompilerParams(
            dimension_semantics=("arbitrary", "arbitrary"), vmem_limit_bytes=VMEM_LIMIT),
        name="moe_grouped",
    )(sched, hs, ms, w_up, w_gate, w_down)


def _final_kernel(pos_ref, ys_ref, x1_ref, mod_ref, fn_ref, yp_ref, ysm_ref, g_ref):
    i = pl.program_id(0)

    def body(r, carry):
        g_ref[_slab(r), :] = ys_ref[_slab(pos_ref[0, r]), :]
        return carry

    lax.fori_loop(0, TM_FINAL, body, 0, unroll=8)
    mod = mod_ref[pl.ds(_mod_row(i, TM_FINAL), 1), :]
    gate2 = mod[:, 5 * D_MODEL:6 * D_MODEL]
    moe = jnp.concatenate([g_ref[pl.ds(c, TM_FINAL, stride=N_SLAB), :] for c in range(N_SLAB)], axis=1)
    x2 = x1_ref[...] + gate2 * moe
    y = (x2 * _rms(x2)) * fn_ref[...]
    is_sample = i >= T_P // TM_FINAL

    @pl.when(jnp.logical_not(is_sample))
    def _():
        yp_ref[...] = y

    @pl.when(is_sample)
    def _():
        ysm_ref[...] = y


def _final(pos, ys, x1, mod, final_norm):
    n_slab = D_MODEL // LANE
    pmap, smap = _stream_maps(TM_FINAL)
    return pl.pallas_call(
        _final_kernel,
        grid=(T // TM_FINAL,),
        in_specs=[
            pl.BlockSpec((None, 1, TM_FINAL), lambda i: (i, 0, 0), memory_space=pltpu.SMEM),
            pl.BlockSpec((T * n_slab, LANE), lambda i: (0, 0), pipeline_mode=pl.Buffered(1)),
            pl.BlockSpec((TM_FINAL, D_MODEL), lambda i: (i, 0)),
            pl.BlockSpec((N_COND, 6 * D_MODEL), lambda i: (0, 0)),
            pl.BlockSpec((1, D_MODEL), lambda i: (0, 0)),
        ],
        out_specs=[pl.BlockSpec((TM_FINAL, D_MODEL), pmap),
                   pl.BlockSpec((TM_FINAL, D_MODEL), smap)],
        scratch_shapes=[pltpu.VMEM((TM_FINAL * n_slab, LANE), F32)],
        out_shape=[jax.ShapeDtypeStruct((T_P, D_MODEL), F32),
                   jax.ShapeDtypeStruct((T_S, D_MODEL), F32)],
        compiler_params=pltpu.CompilerParams(
            dimension_semantics=("arbitrary",), vmem_limit_bytes=VMEM_LIMIT),
        name="moe_unsort_final",
    )(pos.reshape(T // TM_FINAL, 1, TM_FINAL), ys, x1, mod, final_norm)


def _plan_kernel(gr_ref, cnt_ref, pos_ref, sched_ref):
    n_post = T // TM_POST
    lane = lax.broadcasted_iota(jnp.int32, (1, LANE), 1)
    grp_lane = lambda v, g: v[:, N_EXPERTS + g:N_EXPERTS + g + 1]
    counts = [cnt_ref[k, 0:1, :] for k in range(n_post)]
    gtot_v = counts[0]
    for k in range(1, n_post):
        gtot_v = gtot_v + counts[k]
    gtot = [grp_lane(gtot_v, g) for g in range(N_GROUPS)]
    goff = [jnp.zeros((1, 1), F32)]
    for g in range(1, N_GROUPS):
        goff.append(goff[-1] + gtot[g - 1])

    before = [jnp.zeros((1, 1), F32) for _ in range(N_GROUPS)]
    rows_per_tile = TM_POST // LANE
    for k in range(n_post):
        gid = gr_ref[0:1, TM_POST * k:TM_POST * (k + 1)]
        pos = gr_ref[1:2, TM_POST * k:TM_POST * (k + 1)]
        for g in range(N_GROUPS):
            pos = pos + jnp.where(gid == float(g), goff[g] + before[g], 0.0)
            before[g] = before[g] + grp_lane(counts[k], g)
        for r in range(rows_per_tile):
            pos_ref[rows_per_tile * k + r:rows_per_tile * k + r + 1, :] = (
                pos[:, LANE * r:LANE * (r + 1)].astype(jnp.int32))

    zero = jnp.zeros((1, LANE), F32)
    rows = {name: zero for name in ("vt", "vg", "vlo", "vhi")}
    slot = jnp.zeros((1, 1), F32)
    last_t = jnp.zeros((1, 1), F32)
    last_g = jnp.zeros((1, 1), F32)
    for i in range(T // TM_MOE):
        for g in range(N_GROUPS):
            lo = jnp.clip(goff[g] - float(TM_MOE * i), 0.0, float(TM_MOE))
            hi = jnp.clip(goff[g] + gtot[g] - float(TM_MOE * i), 0.0, float(TM_MOE))
            ok = hi > lo
            here = ok & (lane == slot.astype(jnp.int32))
            rows["vt"] = jnp.where(here, float(i), rows["vt"])
            rows["vg"] = jnp.where(here, float(g), rows["vg"])
            rows["vlo"] = jnp.where(here, lo, rows["vlo"])
            rows["vhi"] = jnp.where(here, hi, rows["vhi"])
            last_t = jnp.where(ok, float(i), last_t)
            last_g = jnp.where(ok, float(g), last_g)
            slot = slot + jnp.where(ok, 1.0, 0.0)
    n_ok = slot.astype(jnp.int32)
    valid = lane < n_ok
    vt = jnp.where(valid, rows["vt"], last_t)
    vg = jnp.where(valid, rows["vg"], last_g)
    prev_t = pltpu.roll(jnp.broadcast_to(vt, (8, LANE)), 1, 1)[0:1, :]
    next_t = pltpu.roll(jnp.broadcast_to(vt, (8, LANE)), LANE - 1, 1)[0:1, :]
    first = jnp.where((lane == 0) | (vt != prev_t), 1.0, 0.0)
    last = jnp.where((lane == n_ok - 1) | (vt != next_t), 1.0, 0.0)
    table = [vt, vg, rows["vlo"], rows["vhi"], first, last, jnp.where(valid, 1.0, 0.0), zero]
    for r, row in enumerate(table):
        sched_ref[r:r + 1, :] = row.astype(jnp.int32)


V_TILE, V_GROUP, V_LO, V_HI, V_FIRST, V_LAST, V_VALID = range(7)


def _plan(gr, cnt):
    pos, sched = pl.pallas_call(
        _plan_kernel,
        out_shape=[jax.ShapeDtypeStruct((T // LANE, LANE), jnp.int32),
                   jax.ShapeDtypeStruct((8, LANE), jnp.int32)],
        name="moe_plan",
    )(gr, cnt)
    return pos.reshape(T), sched


def _rope_tables():
    n = np.arange(DEC_SEQ)
    pos = np.stack([n // GRID_W, n % GRID_W], axis=1).astype(np.float32)
    half = ROPE_AXIS // 2
    inv = (1.0 / (ROPE_BASE ** (np.arange(0, ROPE_AXIS, 2, dtype=np.float32) / ROPE_AXIS))).astype(np.float32)
    ang = (pos[:, :, None] * inv[None, None, :]).astype(np.float32)
    cos = np.cos(ang).astype(np.float32)
    sin = np.sin(ang).astype(np.float32)
    tabs = np.zeros((2, 3, DEC_SEQ, LANE), np.float32)
    tabs[:, 0] = 1.0
    for a in range(2):
        lo = ROPE_LANE0 + a * ROPE_AXIS
        tabs[1, 0, :, lo:lo + half] = cos[:, a]
        tabs[1, 0, :, lo + half:lo + 2 * half] = cos[:, a]
        tabs[1, 1, :, lo + half:lo + 2 * half] = sin[:, a]
        tabs[1, 2, :, lo:lo + half] = -sin[:, a]
    return jnp.asarray(tabs)


def kernel(x_prompt, x_sample, cache_ckv, cache_krope, c, c_ctx, norm1, w_ada, b_ada, w_in, conv_w,
           w_conv_out, q_norm, w_uq, kv_norm, w_ukv, w_o, w_mix_out, norm2, w_grp, w_exp, w_up,
           w_gate, w_down, final_norm):
    l = 0
    xp = x_prompt.reshape(T_P, D_MODEL)
    xs = x_sample.reshape(T_S, D_MODEL)
    mod = _ada(c_ctx[None, :], c, w_ada[l], b_ada[l][None, :])

    w_in_t = w_in[l].T
    w_uq_slot = jnp.pad(w_uq[l].reshape(Q_LORA, N_HEADS, QK_NOPE + QK_ROPE),
                        ((0, 0), (0, 0), (0, LANE - QK_NOPE - QK_ROPE))
                        ).reshape(Q_LORA, N_HEADS * LANE).astype(BF16)
    wkv = w_ukv[l].reshape(KV_LORA, N_HEADS, QK_NOPE + V_HEAD)
    wk_slot = jnp.pad(wkv[:, :, :QK_NOPE], ((0, 0), (0, 0), (0, LANE - QK_NOPE)))
    wv = wkv[:, :, QK_NOPE:].reshape(KV_LORA, N_HEADS // 2, 2, V_HEAD)
    zero = jnp.zeros_like(wv[:, :, 0])
    wv_slot = jnp.stack([jnp.concatenate([wv[:, :, 0], zero], axis=-1),
                         jnp.concatenate([zero, wv[:, :, 1]], axis=-1)], axis=2)
    w_ukv_slot = jnp.concatenate([wk_slot.reshape(KV_LORA, N_HEADS * LANE),
                                  wv_slot.reshape(KV_LORA, N_HEADS * LANE)], axis=1).astype(BF16)
    w_route = jnp.pad(jnp.concatenate([w_exp[l], w_grp[l]], axis=1),
                      ((0, 0), (0, LANE - N_EXPERTS - N_GROUPS)))
    w_route_hi = w_route.astype(BF16)
    w_route_lo = (w_route - w_route_hi.astype(F32)).astype(BF16)
    w_route2 = jnp.concatenate([w_route_hi, w_route_lo], axis=1)
    cache_krs = jnp.pad(cache_krope[:, l], ((0, 0), (0, 0), (ROPE_LANE0, LANE - ROPE_LANE0 - QK_ROPE)))

    zc, q, ckv, krs, nckv, nkr = _inproj(xp, xs, mod, norm1[l][None, :], w_in_t, conv_w[l],
                              q_norm[l][None, :], kv_norm[l][None, :], w_uq_slot, _rope_tables())
    o_p, o_s = _attention(q, ckv, krs, cache_ckv[:, l], cache_krs, w_ukv_slot)
    x1, h3, meta, cnt, gr = _post(xp, xs, mod, norm1[l][None, :], w_in_t, zc, o_p, o_s, w_conv_out[l],
                                  w_o[l], w_mix_out[l], norm2[l][None, :], w_route2)
    pos, sched = _plan(gr, cnt)
    hs, ms = _dispatch(pos, h3, meta)
    ys = _moe(sched, hs, ms, w_up[l], w_gate[l], w_down[l])
    yp, ysm = _final(pos, ys, x1, mod, final_norm[None, :])

    y_prompt = yp.reshape(BATCH, SEQ, D_MODEL)
    y_sample = ysm.reshape(DEC_BATCH, DEC_SEQ, D_MODEL)
    new_ckv = nckv.reshape(BATCH, 1, SEQ, KV_LORA)
    new_krope = jnp.swapaxes(nkr, 1, 2).reshape(BATCH, 1, SEQ, QK_ROPE)
    return (y_prompt, y_sample, new_ckv, new_krope)
```

```python
import numpy as np
import jax
import jax.numpy as jnp
from jax import lax
from jax.experimental import pallas as pl
from jax.experimental.pallas import tpu as pltpu

F32 = jnp.float32
BF16 = jnp.bfloat16

D_MODEL = 1024
BATCH = 16
SEQ = 256
DEC_BATCH = 2
DEC_SEQ = 1024
PAST_LEN = 256
GRID_W = 64
N_HEADS = 8
QK_NOPE = 64
QK_ROPE = 32
V_HEAD = 64
Q_LORA = 256
KV_LORA = 128
ROPE_AXIS = QK_ROPE // 2
ROPE_BASE = 10000.0
ATTN_SCALE = (QK_NOPE + QK_ROPE) ** -0.5
D_CONV = D_MODEL
N_GROUPS = 4
EXP_PER_GROUP = 8
N_EXPERTS = N_GROUPS * EXP_PER_GROUP
D_EXPERT = 256
EPS = 1e-6

T_P = BATCH * SEQ
T_S = DEC_BATCH * DEC_SEQ
T = T_P + T_S
N_COND = 8
LANE = 128
ROPE_LANE0 = QK_NOPE
SMALL_COLS = Q_LORA + KV_LORA + LANE
VMEM_LIMIT = 56 * 1024 * 1024

TM_IN = 1024
TM_POST = 512
TM_MOE = T // N_GROUPS
TM_FINAL = 256
TM_DISP = 1024
N_SLAB = D_MODEL // LANE
MOE_SUB = 256
MOE_EPS = 2
N_VISITS = T // TM_MOE + N_GROUPS - 1
GID_LANE = 40
RANK_LANE = 41
Q_BLK_S = 512
PROMPT_SEQS = 2
CONV_CHUNK = 256


def _dot(a, b):
    return jnp.dot(a, b, preferred_element_type=F32)


def _rms(x):
    return lax.rsqrt(jnp.mean(x * x, axis=-1, keepdims=True) + EPS)


def _slab(t):
    return pl.ds(pl.multiple_of(t * N_SLAB, N_SLAB), N_SLAB)


def _mod_row(i, tm):
    n_prompt = T_P // tm
    return jnp.where(i >= n_prompt, 1 + ((i - n_prompt) * tm) // DEC_SEQ, 0)


def _ada_kernel(cctx_ref, c_ref, w_ref, b_ref, o_ref):
    c = jnp.concatenate([cctx_ref[...], c_ref[...],
                         jnp.zeros((N_COND - 1 - DEC_BATCH, D_MODEL), F32)], axis=0)
    a = (c * jax.nn.sigmoid(c)).astype(BF16)
    o_ref[...] = _dot(a, w_ref[...].astype(BF16)) + b_ref[...]


def _ada(c_ctx, c, w_ada, b_ada):
    n = 6 * D_MODEL
    bn = 1536
    return pl.pallas_call(
        _ada_kernel,
        grid=(n // bn,),
        in_specs=[
            pl.BlockSpec((1, D_MODEL), lambda j: (0, 0)),
            pl.BlockSpec((DEC_BATCH, D_MODEL), lambda j: (0, 0)),
            pl.BlockSpec((D_MODEL, bn), lambda j: (0, j)),
            pl.BlockSpec((1, bn), lambda j: (0, j)),
        ],
        out_specs=pl.BlockSpec((N_COND, bn), lambda j: (0, j)),
        out_shape=jax.ShapeDtypeStruct((N_COND, n), F32),
        compiler_params=pltpu.CompilerParams(
            dimension_semantics=("parallel",), vmem_limit_bytes=VMEM_LIMIT),
        name="ada_mod",
    )(c_ctx, c, w_ada, b_ada)


O_CQ = 3 * D_CONV
O_KR = O_CQ + Q_LORA + KV_LORA
O_GATE = O_KR + QK_ROPE
SMALL_BLK = 512
NT = (((1,), (1,)), ((), ()))


def _dot_nt(a, bt):
    return lax.dot_general(a, bt, NT, preferred_element_type=F32)


def _stream_maps(tm):
    n_prompt = T_P // tm
    return (lambda i, *_: (jnp.minimum(i, n_prompt - 1), 0),
            lambda i, *_: (jnp.maximum(i - n_prompt, 0), 0))


def _inproj_kernel(xp_ref, xs_ref, mod_ref, n1_ref, wc_ref, ws_ref, cw_ref, qn_ref, kvn_ref, wuq_ref,
                   rope_ref, zc_ref, q_ref, ckv_ref, krs_ref, nckv_ref, nkr_ref):
    i = pl.program_id(0)
    is_sample = i >= T_P // TM_IN
    seq = jnp.where(is_sample, DEC_SEQ, SEQ)
    mod = mod_ref[pl.ds(_mod_row(i, TM_IN), 1), :]
    shift1 = mod[:, 0:D_MODEL]
    scale1 = mod[:, D_MODEL:2 * D_MODEL]
    x = jnp.where(is_sample, xs_ref[...], xp_ref[...])
    h = ((x * _rms(x)) * n1_ref[...]) * (1.0 + scale1) + shift1
    hb = h.astype(BF16)

    n_lat = Q_LORA + KV_LORA
    w_small = jnp.concatenate([
        ws_ref[0:n_lat, :].astype(BF16),
        jnp.zeros((ROPE_LANE0, D_MODEL), BF16),
        ws_ref[n_lat:n_lat + QK_ROPE, :].astype(BF16),
        jnp.zeros((LANE - ROPE_LANE0 - QK_ROPE, D_MODEL), BF16)], axis=0)
    sm = _dot_nt(hb, w_small)
    cq = sm[:, 0:Q_LORA]
    ckv_raw = sm[:, Q_LORA:Q_LORA + KV_LORA]
    krs = sm[:, Q_LORA + KV_LORA:SMALL_COLS]
    cqn = (cq * _rms(cq)) * qn_ref[...]
    q = _dot(cqn.astype(BF16), wuq_ref[...])
    ckv = (ckv_raw * _rms(ckv_raw)) * kvn_ref[...]
    ckv_ref[...] = ckv

    @pl.when(jnp.logical_not(is_sample))
    def _():
        nckv_ref[...] = ckv
        for s in range(TM_IN // SEQ):
            kt = krs[s * SEQ:(s + 1) * SEQ, :].T
            nkr_ref[s] = kt[ROPE_LANE0:ROPE_LANE0 + QK_ROPE, :]

    cos = rope_ref[0]
    sin_lo = rope_ref[1]
    sin_hi = rope_ref[2]

    def rot(v):
        return v * cos + pltpu.roll(v, 8, 1) * sin_lo + pltpu.roll(v, LANE - 8, 1) * sin_hi

    krs_ref[...] = rot(krs)
    for hh in range(N_HEADS):
        q_ref[:, LANE * hh:LANE * (hh + 1)] = rot(q[:, LANE * hh:LANE * (hh + 1)]).astype(BF16)

    pos = lax.broadcasted_iota(jnp.int32, (TM_IN, 1), 0) & (seq - 1)
    first = pos == 0
    last = pos == seq - 1
    for j in range(D_CONV // CONV_CHUNK):
        c0 = j * CONV_CHUNK
        bg = _dot_nt(hb, wc_ref[c0:c0 + CONV_CHUNK, :].astype(BF16))
        cg = _dot_nt(hb, wc_ref[D_CONV + c0:D_CONV + c0 + CONV_CHUNK, :].astype(BF16))
        ui = _dot_nt(hb, wc_ref[2 * D_CONV + c0:2 * D_CONV + c0 + CONV_CHUNK, :].astype(BF16))
        u = cg * ui
        u_prev = jnp.where(first, 0.0, pltpu.roll(u, 1, 0))
        u_next = jnp.where(last, 0.0, pltpu.roll(u, TM_IN - 1, 0))
        cw = cw_ref[:, c0:c0 + CONV_CHUNK]
        conv = u_prev * cw[0:1] + u * cw[1:2] + u_next * cw[2:3]
        zc_ref[:, c0:c0 + CONV_CHUNK] = (bg * conv).astype(BF16)


def _inproj(xp, xs, mod, norm1, w_in_t, conv_w, q_norm, kv_norm, w_uq_slot, rope_tabs):
    n_prompt = T_P // TM_IN
    const = lambda i: (0, 0)
    pmap, smap = _stream_maps(TM_IN)
    once = pl.Buffered(1)
    return pl.pallas_call(
        _inproj_kernel,
        grid=(T // TM_IN,),
        in_specs=[
            pl.BlockSpec((TM_IN, D_MODEL), pmap),
            pl.BlockSpec((TM_IN, D_MODEL), smap),
            pl.BlockSpec((N_COND, 6 * D_MODEL), const),
            pl.BlockSpec((1, D_MODEL), const),
            pl.BlockSpec((O_CQ, D_MODEL), const, pipeline_mode=once),
            pl.BlockSpec((SMALL_BLK, D_MODEL), lambda i: (O_CQ // SMALL_BLK, 0), pipeline_mode=once),
            pl.BlockSpec((3, D_CONV), const),
            pl.BlockSpec((1, Q_LORA), const),
            pl.BlockSpec((1, KV_LORA), const),
            pl.BlockSpec((Q_LORA, N_HEADS * LANE), const),
            pl.BlockSpec((None, 3, TM_IN, LANE),
                         lambda i: (jnp.where(i >= n_prompt, 1, 0), 0, 0, 0)),
        ],
        out_specs=[
            pl.BlockSpec((TM_IN, D_CONV), lambda i: (i, 0)),
            pl.BlockSpec((TM_IN, N_HEADS * LANE), lambda i: (i, 0)),
            pl.BlockSpec((TM_IN, KV_LORA), lambda i: (i, 0)),
            pl.BlockSpec((TM_IN, LANE), lambda i: (i, 0)),
            pl.BlockSpec((TM_IN, KV_LORA), pmap),
            pl.BlockSpec((TM_IN // SEQ, QK_ROPE, SEQ), lambda i: (jnp.minimum(i, n_prompt - 1), 0, 0)),
        ],
        out_shape=[
            jax.ShapeDtypeStruct((T, D_CONV), BF16),
            jax.ShapeDtypeStruct((T, N_HEADS * LANE), BF16),
            jax.ShapeDtypeStruct((T, KV_LORA), F32),
            jax.ShapeDtypeStruct((T, LANE), F32),
            jax.ShapeDtypeStruct((T_P, KV_LORA), F32),
            jax.ShapeDtypeStruct((BATCH, QK_ROPE, SEQ), F32),
        ],
        compiler_params=pltpu.CompilerParams(
            dimension_semantics=("arbitrary",), vmem_limit_bytes=VMEM_LIMIT),
        name="in_proj",
    )(xp, xs, mod, norm1, w_in_t, w_in_t, conv_w, q_norm, kv_norm, w_uq_slot, rope_tabs)


def _fill_kv(ckv, krs, wukv_ref, kf_scr, v_scr, off):
    m = ckv.shape[0]
    kv = _dot(ckv.astype(BF16), wukv_ref[...])
    for hh in range(N_HEADS):
        kf_scr[hh, off:off + m, :] = (kv[:, LANE * hh:LANE * (hh + 1)] + krs).astype(BF16)
    v_scr[off:off + m, :] = kv[:, N_HEADS * LANE:].astype(BF16)


def _attend(q_ref, r0, rows, kf_scr, v_scr, o_ref):
    for pair in range(N_HEADS // 2):
        acc = None
        for hh in (2 * pair, 2 * pair + 1):
            qh = q_ref[r0:r0 + rows, LANE * hh:LANE * (hh + 1)]
            s = _dot_nt(qh, kf_scr[hh]) * ATTN_SCALE
            e = jnp.exp(s - jnp.max(s, axis=-1, keepdims=True))
            p = (e / jnp.sum(e, axis=-1, keepdims=True)).astype(BF16)
            part = _dot(p, v_scr[:, LANE * hh:LANE * (hh + 1)])
            acc = part if acc is None else acc + part
        o_ref[r0:r0 + rows, LANE * pair:LANE * (pair + 1)] = acc.astype(BF16)


def _attn_prompt_kernel(q_ref, ckv_ref, krs_ref, wukv_ref, o_ref):
    for s in range(PROMPT_SEQS):
        r0 = s * SEQ
        kv = _dot(ckv_ref[r0:r0 + SEQ, :].astype(BF16), wukv_ref[...])
        krs = krs_ref[r0:r0 + SEQ, :]
        for pair in range(N_HEADS // 2):
            acc = None
            for hh in (2 * pair, 2 * pair + 1):
                kf = (kv[:, LANE * hh:LANE * (hh + 1)] + krs).astype(BF16)
                vh = kv[:, LANE * (N_HEADS + hh):LANE * (N_HEADS + hh + 1)].astype(BF16)
                sc = _dot_nt(q_ref[r0:r0 + SEQ, LANE * hh:LANE * (hh + 1)], kf) * ATTN_SCALE
                e = jnp.exp(sc - jnp.max(sc, axis=-1, keepdims=True))
                p = (e / jnp.sum(e, axis=-1, keepdims=True)).astype(BF16)
                part = _dot(p, vh)
                acc = part if acc is None else acc + part
            o_ref[r0:r0 + SEQ, LANE * pair:LANE * (pair + 1)] = acc.astype(BF16)


def _attn_sample_kernel(q_ref, ckv_ref, cckv_ref, krs_ref, ckrs_ref, wukv_ref, o_ref, kf_scr, v_scr):
    @pl.when(pl.program_id(1) == 0)
    def _():
        _fill_kv(ckv_ref[...], krs_ref[...], wukv_ref, kf_scr, v_scr, 0)
        _fill_kv(cckv_ref[...], ckrs_ref[...], wukv_ref, kf_scr, v_scr, DEC_SEQ)

    _attend(q_ref, 0, Q_BLK_S, kf_scr, v_scr, o_ref)


def _attention(q, ckv, krs, cache_ckv, cache_krs, w_ukv_slot):
    kv_cols = 2 * N_HEADS * LANE
    n_o = N_HEADS * V_HEAD
    steps = BATCH // PROMPT_SEQS
    rows = PROMPT_SEQS * SEQ
    o_prompt = pl.pallas_call(
        _attn_prompt_kernel,
        grid=(steps,),
        in_specs=[
            pl.BlockSpec((rows, N_HEADS * LANE), lambda b: (b, 0)),
            pl.BlockSpec((rows, KV_LORA), lambda b: (b, 0)),
            pl.BlockSpec((rows, LANE), lambda b: (b, 0)),
            pl.BlockSpec((KV_LORA, kv_cols), lambda b: (0, 0)),
        ],
        out_specs=pl.BlockSpec((rows, n_o), lambda b: (b, 0)),
        out_shape=jax.ShapeDtypeStruct((T_P, n_o), BF16),
        compiler_params=pltpu.CompilerParams(dimension_semantics=("parallel",),
                                             vmem_limit_bytes=VMEM_LIMIT),
        name="attn_prompt",
    )(q, ckv, krs, w_ukv_slot)

    m_all = DEC_SEQ + PAST_LEN
    nq = DEC_SEQ // Q_BLK_S
    q0 = T_P // Q_BLK_S
    s0 = T_P // DEC_SEQ
    o_sample = pl.pallas_call(
        _attn_sample_kernel,
        grid=(DEC_BATCH, nq),
        in_specs=[
            pl.BlockSpec((Q_BLK_S, N_HEADS * LANE), lambda b, j: (q0 + b * nq + j, 0)),
            pl.BlockSpec((DEC_SEQ, KV_LORA), lambda b, j: (s0 + b, 0)),
            pl.BlockSpec((None, PAST_LEN, KV_LORA), lambda b, j: (b, 0, 0)),
            pl.BlockSpec((DEC_SEQ, LANE), lambda b, j: (s0 + b, 0)),
            pl.BlockSpec((None, PAST_LEN, LANE), lambda b, j: (b, 0, 0)),
            pl.BlockSpec((KV_LORA, kv_cols), lambda b, j: (0, 0)),
        ],
        out_specs=pl.BlockSpec((Q_BLK_S, n_o), lambda b, j: (b * nq + j, 0)),
        out_shape=jax.ShapeDtypeStruct((T_S, n_o), BF16),
        scratch_shapes=[pltpu.VMEM((N_HEADS, m_all, LANE), BF16),
                        pltpu.VMEM((m_all, N_HEADS * LANE), BF16)],
        compiler_params=pltpu.CompilerParams(dimension_semantics=("parallel", "arbitrary"),
                                             vmem_limit_bytes=VMEM_LIMIT),
        name="attn_sample",
    )(q, ckv, cache_ckv, krs, cache_krs, w_ukv_slot)
    return o_prompt, o_sample


def _route(logits):
    lane = lax.broadcasted_iota(jnp.int32, logits.shape, 1)
    neg = -jnp.inf
    big = jnp.int32(1 << 20)
    gmask = (lane >= N_EXPERTS) & (lane < N_EXPERTS + N_GROUPS)
    gl = jnp.where(gmask, logits, neg)
    gmax = jnp.max(gl, axis=-1, keepdims=True)
    gsum = jnp.sum(jnp.where(gmask, jnp.exp(gl - gmax), 0.0), axis=-1, keepdims=True)
    p_g = 1.0 / gsum
    g_idx = jnp.min(jnp.where(gl == gmax, lane, big), axis=-1, keepdims=True) - N_EXPERTS

    emask = (lane < N_EXPERTS) & ((lane >> 3) == g_idx)
    el = jnp.where(emask, logits, neg)
    m1 = jnp.max(el, axis=-1, keepdims=True)
    i1 = jnp.min(jnp.where(el == m1, lane, big), axis=-1, keepdims=True)
    el2 = jnp.where(lane == i1, neg, el)
    m2 = jnp.max(el2, axis=-1, keepdims=True)
    i2 = jnp.min(jnp.where(el2 == m2, lane, big), axis=-1, keepdims=True)
    z = jnp.sum(jnp.where(emask, jnp.exp(el - m1), 0.0), axis=-1, keepdims=True)
    p1 = 1.0 / z
    p2 = jnp.exp(m2 - m1) / z
    tot = p1 + p2
    w1 = p_g * p1 / tot
    w2 = p_g * p2 / tot
    return jnp.where(lane == i1, w1, 0.0) + jnp.where(lane == i2, w2, 0.0), g_idx


def _post_kernel(xp_ref, xs_ref, mod_ref, n1_ref, wg_ref, zc_ref, op_ref, os_ref, wco_ref, wo_ref,
                 wmix_ref, n2_ref, wr_ref, x1_ref, h3_ref, meta_ref, cnt_ref, gr_ref):
    i = pl.program_id(0)
    is_sample = i >= T_P // TM_POST
    x = jnp.where(is_sample, xs_ref[...], xp_ref[...])
    o = jnp.where(is_sample, os_ref[...], op_ref[...])
    mod = mod_ref[pl.ds(_mod_row(i, TM_POST), 1), :]
    shift1 = mod[:, 0:D_MODEL]
    scale1 = mod[:, D_MODEL:2 * D_MODEL]
    gate1 = mod[:, 2 * D_MODEL:3 * D_MODEL]
    shift2 = mod[:, 3 * D_MODEL:4 * D_MODEL]
    scale2 = mod[:, 4 * D_MODEL:5 * D_MODEL]
    y_conv = _dot(zc_ref[...], wco_ref[...].astype(BF16))
    y_mla = _dot(o, wo_ref[...].astype(BF16))
    h = ((x * _rms(x)) * n1_ref[...]) * (1.0 + scale1) + shift1
    g = _dot_nt(h.astype(BF16), wg_ref[...].astype(BF16))
    merged = (jax.nn.sigmoid(g[:, 0:D_MODEL]) * y_conv
              + jax.nn.sigmoid(g[:, D_MODEL:2 * D_MODEL]) * y_mla)
    y = _dot(merged.astype(BF16), wmix_ref[...].astype(BF16))
    x1 = x + gate1 * y
    x1_ref[...] = x1
    h2 = ((x1 * _rms(x1)) * n2_ref[...]) * (1.0 + scale2) + shift2
    h2_hi = h2.astype(BF16)
    h2_lo = (h2 - h2_hi.astype(F32)).astype(BF16)
    hh = _dot(h2_hi, wr_ref[...])
    logits = hh[:, 0:LANE] + hh[:, LANE:2 * LANE] + _dot(h2_lo, wr_ref[:, 0:LANE])
    comb, g_idx = _route(logits)

    lane = lax.broadcasted_iota(jnp.int32, comb.shape, 1)
    onehot = lane == g_idx + N_EXPERTS
    r_i = lax.broadcasted_iota(jnp.int32, (TM_POST, TM_POST), 0)
    c_i = lax.broadcasted_iota(jnp.int32, (TM_POST, TM_POST), 1)
    lower = jnp.where(c_i < r_i, 1.0, 0.0).astype(BF16)
    before = _dot(lower, jnp.where(onehot, 1.0, 0.0).astype(BF16))
    rank = jnp.sum(jnp.where(onehot, before, 0.0), axis=-1, keepdims=True)
    counts = jnp.sum(jnp.where(onehot, 1.0, 0.0), axis=0, keepdims=True)
    cnt_ref[...] = jnp.broadcast_to(counts, cnt_ref.shape)

    meta_ref[...] = comb
    idx = jnp.where(lane == GID_LANE, g_idx.astype(F32), 0.0) + jnp.where(lane == RANK_LANE, rank, 0.0)
    idx_hi = idx.astype(BF16)
    idx_lo = (idx - idx_hi.astype(F32)).astype(BF16)
    s_row = lax.broadcasted_iota(jnp.int32, (8, LANE), 0)
    s_lane = lax.broadcasted_iota(jnp.int32, (8, LANE), 1)
    sel = jnp.where(s_lane == GID_LANE + s_row, 1.0, 0.0).astype(BF16)
    gr_ref[...] = _dot_nt(sel, idx_hi) + _dot_nt(sel, idx_lo)
    for c in range(N_SLAB):
        h3_ref[pl.ds(c, TM_POST, stride=N_SLAB), :] = h2[:, LANE * c:LANE * (c + 1)]


def _post(xp, xs, mod, norm1, w_in_t, zc, o_p, o_s, w_conv_out, w_o, w_mix_out, norm2, w_route):
    const = lambda i: (0, 0)
    row = lambda i: (i, 0)
    pmap, smap = _stream_maps(TM_POST)
    once = pl.Buffered(1)
    return pl.pallas_call(
        _post_kernel,
        grid=(T // TM_POST,),
        in_specs=[
            pl.BlockSpec((TM_POST, D_MODEL), pmap),
            pl.BlockSpec((TM_POST, D_MODEL), smap),
            pl.BlockSpec((N_COND, 6 * D_MODEL), const),
            pl.BlockSpec((1, D_MODEL), const),
            pl.BlockSpec((pl.Element(2 * D_MODEL), pl.Element(D_MODEL)), lambda i: (O_GATE, 0),
                         pipeline_mode=once),
            pl.BlockSpec((TM_POST, D_CONV), row),
            pl.BlockSpec((TM_POST, N_HEADS * V_HEAD), pmap),
            pl.BlockSpec((TM_POST, N_HEADS * V_HEAD), smap),
            pl.BlockSpec((D_CONV, D_MODEL), const, pipeline_mode=once),
            pl.BlockSpec((N_HEADS * V_HEAD, D_MODEL), const, pipeline_mode=once),
            pl.BlockSpec((D_MODEL, D_MODEL), const, pipeline_mode=once),
            pl.BlockSpec((1, D_MODEL), const),
            pl.BlockSpec((D_MODEL, 2 * LANE), const),
        ],
        out_specs=[
            pl.BlockSpec((TM_POST, D_MODEL), row),
            pl.BlockSpec((TM_POST * N_SLAB, LANE), row),
            pl.BlockSpec((TM_POST, LANE), row),
            pl.BlockSpec((None, 8, LANE), lambda i: (i, 0, 0)),
            pl.BlockSpec((8, TM_POST), lambda i: (0, i)),
        ],
        out_shape=[
            jax.ShapeDtypeStruct((T, D_MODEL), F32),
            jax.ShapeDtypeStruct((T * N_SLAB, LANE), F32),
            jax.ShapeDtypeStruct((T, LANE), F32),
            jax.ShapeDtypeStruct((T // TM_POST, 8, LANE), F32),
            jax.ShapeDtypeStruct((8, T), F32),
        ],
        compiler_params=pltpu.CompilerParams(
            dimension_semantics=("parallel",), vmem_limit_bytes=VMEM_LIMIT),
        name="post_mixer",
    )(xp, xs, mod, norm1, w_in_t, zc, o_p, o_s, w_conv_out, w_o, w_mix_out, norm2, w_route)


DISP_IN = T // TM_DISP
DISP_OUT = T // TM_MOE


def _dispatch_kernel(pos_ref, h3_ref, m_ref, hs_ref, ms_ref, xs_ref, mss_ref):
    i = pl.program_id(0)

    @pl.when(i < DISP_IN)
    def _():
        def body(r, carry):
            p = pos_ref[0, r]
            xs_ref[_slab(p), :] = h3_ref[_slab(r), :]
            mss_ref[pl.ds(p, 1), :] = m_ref[pl.ds(r, 1), :]
            return carry

        lax.fori_loop(0, TM_DISP, body, 0, unroll=8)

    @pl.when(i >= DISP_IN)
    def _():
        row0 = pl.multiple_of((i - DISP_IN) * TM_MOE, TM_MOE)
        for c in range(N_SLAB):
            hs_ref[:, LANE * c:LANE * (c + 1)] = (
                xs_ref[pl.ds(row0 * N_SLAB + c, TM_MOE, stride=N_SLAB), :].astype(BF16))
        ms_ref[...] = mss_ref[pl.ds(row0, TM_MOE), :]


def _dispatch(pos, h3, meta):
    n_slab = N_SLAB
    in_map = lambda i: (jnp.minimum(i, DISP_IN - 1), 0)
    out_map = lambda i: (jnp.maximum(i - DISP_IN, 0), 0)
    return pl.pallas_call(
        _dispatch_kernel,
        grid=(DISP_IN + DISP_OUT,),
        in_specs=[pl.BlockSpec((None, 1, TM_DISP), lambda i: (jnp.minimum(i, DISP_IN - 1), 0, 0),
                               memory_space=pltpu.SMEM),
                  pl.BlockSpec((TM_DISP * n_slab, LANE), in_map),
                  pl.BlockSpec((TM_DISP, LANE), in_map)],
        out_specs=[pl.BlockSpec((TM_MOE, D_MODEL), out_map),
                   pl.BlockSpec((TM_MOE, LANE), out_map)],
        scratch_shapes=[pltpu.VMEM((T * n_slab, LANE), F32),
                        pltpu.VMEM((T, LANE), F32)],
        out_shape=[jax.ShapeDtypeStruct((T, D_MODEL), BF16),
                   jax.ShapeDtypeStruct((T, LANE), F32)],
        compiler_params=pltpu.CompilerParams(
            dimension_semantics=("arbitrary",), vmem_limit_bytes=VMEM_LIMIT),
        name="moe_dispatch",
    )(pos.reshape(DISP_IN, 1, TM_DISP), h3, meta)


def _moe_kernel(sched_ref, hs_ref, ms_ref, wup_ref, wgate_ref, wdown_ref, y3_ref, acc_ref):
    v = pl.program_id(0)
    j = pl.program_id(1)
    valid = sched_ref[V_VALID, v] == 1
    lo = sched_ref[V_LO, v]
    hi = sched_ref[V_HI, v]
    e0 = sched_ref[V_GROUP, v] * EXP_PER_GROUP + j * MOE_EPS
    full = (hi - lo) * 4 >= TM_MOE * 3

    @pl.when(valid & (j == 0) & (sched_ref[V_FIRST, v] == 1))
    def _():
        acc_ref[...] = jnp.zeros_like(acc_ref)

    def expert_rows(r0, rows):
        w_in2 = jnp.concatenate(
            [w[k].astype(BF16) for k in range(MOE_EPS) for w in (wup_ref, wgate_ref)], axis=1)
        ag = _dot(hs_ref[pl.ds(r0, rows), :], w_in2)
        comb = ms_ref[pl.ds(r0, rows), :]
        lane = lax.broadcasted_iota(jnp.int32, comb.shape, 1)
        acts = []
        for k in range(MOE_EPS):
            a = ag[:, 2 * k * D_EXPERT:(2 * k + 1) * D_EXPERT]
            g = ag[:, (2 * k + 1) * D_EXPERT:(2 * k + 2) * D_EXPERT]
            cw = jnp.sum(jnp.where(lane == e0 + k, comb, 0.0), axis=-1, keepdims=True)
            acts.append(((g * jax.nn.sigmoid(g)) * a * cw).astype(BF16))
        w_out = jnp.concatenate([wdown_ref[k].astype(BF16) for k in range(MOE_EPS)], axis=0)
        acc_ref[pl.ds(r0, rows), :] += _dot(jnp.concatenate(acts, axis=1), w_out)

    @pl.when(valid & full)
    def _():
        expert_rows(0, TM_MOE)

    @pl.when(valid & jnp.logical_not(full))
    def _():
        def sub_block(s, carry):
            expert_rows(pl.multiple_of(s * MOE_SUB, MOE_SUB), MOE_SUB)
            return carry

        lax.fori_loop(lo // MOE_SUB, (hi + MOE_SUB - 1) // MOE_SUB, sub_block, 0)

    @pl.when(valid & (j == EXP_PER_GROUP // MOE_EPS - 1) & (sched_ref[V_LAST, v] == 1))
    def _():
        for c in range(N_SLAB):
            y3_ref[pl.ds(c, TM_MOE, stride=N_SLAB), :] = acc_ref[:, LANE * c:LANE * (c + 1)]


def _moe(sched, hs, ms, w_up, w_gate, w_down):
    steps = EXP_PER_GROUP // MOE_EPS
    wmap = lambda v, j, sched: (
        sched[V_GROUP, v] * steps + jnp.where(sched[V_VALID, v] == 1, j, steps - 1), 0, 0)
    tmap = lambda v, j, sched: (sched[V_TILE, v], 0)
    n_slab = D_MODEL // LANE
    return pl.pallas_call(
        _moe_kernel,
        grid_spec=pltpu.PrefetchScalarGridSpec(
            num_scalar_prefetch=1,
            grid=(N_VISITS, steps),
            in_specs=[
                pl.BlockSpec((TM_MOE, D_MODEL), tmap),
                pl.BlockSpec((TM_MOE, LANE), tmap),
                pl.BlockSpec((MOE_EPS, D_MODEL, D_EXPERT), wmap),
                pl.BlockSpec((MOE_EPS, D_MODEL, D_EXPERT), wmap),
                pl.BlockSpec((MOE_EPS, D_EXPERT, D_MODEL), wmap),
            ],
            out_specs=pl.BlockSpec((TM_MOE * n_slab, LANE), tmap),
            scratch_shapes=[pltpu.VMEM((TM_MOE, D_MODEL), F32)],
        ),
        out_shape=jax.ShapeDtypeStruct((T * n_slab, LANE), F32),
        compiler_params=pltpu.CompilerParams(
            dimension_semantics=("arbitrary", "arbitrary"), vmem_limit_bytes=VMEM_LIMIT),
        name="moe_grouped",
    )(sched, hs, ms, w_up, w_gate, w_down)


def _final_kernel(pos_ref, ys_ref, x1_ref, mod_ref, fn_ref, yp_ref, ysm_ref, g_ref):
    i = pl.program_id(0)

    def body(r, carry):
        g_ref[_slab(r), :] = ys_ref[_slab(pos_ref[0, r]), :]
        return carry

    lax.fori_loop(0, TM_FINAL, body, 0, unroll=8)
    mod = mod_ref[pl.ds(_mod_row(i, TM_FINAL), 1), :]
    gate2 = mod[:, 5 * D_MODEL:6 * D_MODEL]
    moe = jnp.concatenate([g_ref[pl.ds(c, TM_FINAL, stride=N_SLAB), :] for c in range(N_SLAB)], axis=1)
    x2 = x1_ref[...] + gate2 * moe
    y = (x2 * _rms(x2)) * fn_ref[...]
    is_sample = i >= T_P // TM_FINAL

    @pl.when(jnp.logical_not(is_sample))
    def _():
        yp_ref[...] = y

    @pl.when(is_sample)
    def _():
        ysm_ref[...] = y


def _final(pos, ys, x1, mod, final_norm):
    n_slab = D_MODEL // LANE
    pmap, smap = _stream_maps(TM_FINAL)
    return pl.pallas_call(
        _final_kernel,
        grid=(T // TM_FINAL,),
        in_specs=[
            pl.BlockSpec((None, 1, TM_FINAL), lambda i: (i, 0, 0), memory_space=pltpu.SMEM),
            pl.BlockSpec((T * n_slab, LANE), lambda i: (0, 0), pipeline_mode=pl.Buffered(1)),
            pl.BlockSpec((TM_FINAL, D_MODEL), lambda i: (i, 0)),
            pl.BlockSpec((N_COND, 6 * D_MODEL), lambda i: (0, 0)),
            pl.BlockSpec((1, D_MODEL), lambda i: (0, 0)),
        ],
        out_specs=[pl.BlockSpec((TM_FINAL, D_MODEL), pmap),
                   pl.BlockSpec((TM_FINAL, D_MODEL), smap)],
        scratch_shapes=[pltpu.VMEM((TM_FINAL * n_slab, LANE), F32)],
        out_shape=[jax.ShapeDtypeStruct((T_P, D_MODEL), F32),
                   jax.ShapeDtypeStruct((T_S, D_MODEL), F32)],
        compiler_params=pltpu.CompilerParams(
            dimension_semantics=("arbitrary",), vmem_limit_bytes=VMEM_LIMIT),
        name="moe_unsort_final",
    )(pos.reshape(T // TM_FINAL, 1, TM_FINAL), ys, x1, mod, final_norm)


def _plan_kernel(gr_ref, cnt_ref, pos_ref, sched_ref):
    n_post = T // TM_POST
    lane = lax.broadcasted_iota(jnp.int32, (1, LANE), 1)
    grp_lane = lambda v, g: v[:, N_EXPERTS + g:N_EXPERTS + g + 1]
    counts = [cnt_ref[k, 0:1, :] for k in range(n_post)]
    gtot_v = counts[0]
    for k in range(1, n_post):
        gtot_v = gtot_v + counts[k]
    gtot = [grp_lane(gtot_v, g) for g in range(N_GROUPS)]
    goff = [jnp.zeros((1, 1), F32)]
    for g in range(1, N_GROUPS):
        goff.append(goff[-1] + gtot[g - 1])

    before = [jnp.zeros((1, 1), F32) for _ in range(N_GROUPS)]
    rows_per_tile = TM_POST // LANE
    for k in range(n_post):
        gid = gr_ref[0:1, TM_POST * k:TM_POST * (k + 1)]
        pos = gr_ref[1:2, TM_POST * k:TM_POST * (k + 1)]
        for g in range(N_GROUPS):
            pos = pos + jnp.where(gid == float(g), goff[g] + before[g], 0.0)
            before[g] = before[g] + grp_lane(counts[k], g)
        for r in range(rows_per_tile):
            pos_ref[rows_per_tile * k + r:rows_per_tile * k + r + 1, :] = (
                pos[:, LANE * r:LANE * (r + 1)].astype(jnp.int32))

    zero = jnp.zeros((1, LANE), F32)
    rows = {name: zero for name in ("vt", "vg", "vlo", "vhi")}
    slot = jnp.zeros((1, 1), F32)
    last_t = jnp.zeros((1, 1), F32)
    last_g = jnp.zeros((1, 1), F32)
    for i in range(T // TM_MOE):
        for g in range(N_GROUPS):
            lo = jnp.clip(goff[g] - float(TM_MOE * i), 0.0, float(TM_MOE))
            hi = jnp.clip(goff[g] + gtot[g] - float(TM_MOE * i), 0.0, float(TM_MOE))
            ok = hi > lo
            here = ok & (lane == slot.astype(jnp.int32))
            rows["vt"] = jnp.where(here, float(i), rows["vt"])
            rows["vg"] = jnp.where(here, float(g), rows["vg"])
            rows["vlo"] = jnp.where(here, lo, rows["vlo"])
            rows["vhi"] = jnp.where(here, hi, rows["vhi"])
            last_t = jnp.where(ok, float(i), last_t)
            last_g = jnp.where(ok, float(g), last_g)
            slot = slot + jnp.where(ok, 1.0, 0.0)
    n_ok = slot.astype(jnp.int32)
    valid = lane < n_ok
    vt = jnp.where(valid, rows["vt"], last_t)
    vg = jnp.where(valid, rows["vg"], last_g)
    prev_t = pltpu.roll(jnp.broadcast_to(vt, (8, LANE)), 1, 1)[0:1, :]
    next_t = pltpu.roll(jnp.broadcast_to(vt, (8, LANE)), LANE - 1, 1)[0:1, :]
    first = jnp.where((lane == 0) | (vt != prev_t), 1.0, 0.0)
    last = jnp.where((lane == n_ok - 1) | (vt != next_t), 1.0, 0.0)
    table = [vt, vg, rows["vlo"], rows["vhi"], first, last, jnp.where(valid, 1.0, 0.0), zero]
    for r, row in enumerate(table):
        sched_ref[r:r + 1, :] = row.astype(jnp.int32)


V_TILE, V_GROUP, V_LO, V_HI, V_FIRST, V_LAST, V_VALID = range(7)


def _plan(gr, cnt):
    pos, sched = pl.pallas_call(
        _plan_kernel,
        out_shape=[jax.ShapeDtypeStruct((T // LANE, LANE), jnp.int32),
                   jax.ShapeDtypeStruct((8, LANE), jnp.int32)],
        name="moe_plan",
    )(gr, cnt)
    return pos.reshape(T), sched


def _rope_tables():
    n = np.arange(DEC_SEQ)
    pos = np.stack([n // GRID_W, n % GRID_W], axis=1).astype(np.float32)
    half = ROPE_AXIS // 2
    inv = (1.0 / (ROPE_BASE ** (np.arange(0, ROPE_AXIS, 2, dtype=np.float32) / ROPE_AXIS))).astype(np.float32)
    ang = (pos[:, :, None] * inv[None, None, :]).astype(np.float32)
    cos = np.cos(ang).astype(np.float32)
    sin = np.sin(ang).astype(np.float32)
    tabs = np.zeros((2, 3, DEC_SEQ, LANE), np.float32)
    tabs[:, 0] = 1.0
    for a in range(2):
        lo = ROPE_LANE0 + a * ROPE_AXIS
        tabs[1, 0, :, lo:lo + half] = cos[:, a]
        tabs[1, 0, :, lo + half:lo + 2 * half] = cos[:, a]
        tabs[1, 1, :, lo + half:lo + 2 * half] = sin[:, a]
        tabs[1, 2, :, lo:lo + half] = -sin[:, a]
    return jnp.asarray(tabs)


def kernel(x_prompt, x_sample, cache_ckv, cache_krope, c, c_ctx, norm1, w_ada, b_ada, w_in, conv_w,
           w_conv_out, q_norm, w_uq, kv_norm, w_ukv, w_o, w_mix_out, norm2, w_grp, w_exp, w_up,
           w_gate, w_down, final_norm):
    l = 0
    xp = x_prompt.reshape(T_P, D_MODEL)
    xs = x_sample.reshape(T_S, D_MODEL)
    mod = _ada(c_ctx[None, :], c, w_ada[l], b_ada[l][None, :])

    w_in_t = w_in[l].T
    w_uq_slot = jnp.pad(w_uq[l].reshape(Q_LORA, N_HEADS, QK_NOPE + QK_ROPE),
                        ((0, 0), (0, 0), (0, LANE - QK_NOPE - QK_ROPE))
                        ).reshape(Q_LORA, N_HEADS * LANE).astype(BF16)
    wkv = w_ukv[l].reshape(KV_LORA, N_HEADS, QK_NOPE + V_HEAD)
    wk_slot = jnp.pad(wkv[:, :, :QK_NOPE], ((0, 0), (0, 0), (0, LANE - QK_NOPE)))
    wv = wkv[:, :, QK_NOPE:].reshape(KV_LORA, N_HEADS // 2, 2, V_HEAD)
    zero = jnp.zeros_like(wv[:, :, 0])
    wv_slot = jnp.stack([jnp.concatenate([wv[:, :, 0], zero], axis=-1),
                         jnp.concatenate([zero, wv[:, :, 1]], axis=-1)], axis=2)
    w_ukv_slot = jnp.concatenate([wk_slot.reshape(KV_LORA, N_HEADS * LANE),
                                  wv_slot.reshape(KV_LORA, N_HEADS * LANE)], axis=1).astype(BF16)
    w_route = jnp.pad(jnp.concatenate([w_exp[l], w_grp[l]], axis=1),
                      ((0, 0), (0, LANE - N_EXPERTS - N_GROUPS)))
    w_route_hi = w_route.astype(BF16)
    w_route_lo = (w_route - w_route_hi.astype(F32)).astype(BF16)
    w_route2 = jnp.concatenate([w_route_hi, w_route_lo], axis=1)
    cache_krs = jnp.pad(cache_krope[:, l], ((0, 0), (0, 0), (ROPE_LANE0, LANE - ROPE_LANE0 - QK_ROPE)))

    zc, q, ckv, krs, nckv, nkr = _inproj(xp, xs, mod, norm1[l][None, :], w_in_t, conv_w[l],
                              q_norm[l][None, :], kv_norm[l][None, :], w_uq_slot, _rope_tables())
    o_p, o_s = _attention(q, ckv, krs, cache_ckv[:, l], cache_krs, w_ukv_slot)
    x1, h3, meta, cnt, gr = _post(xp, xs, mod, norm1[l][None, :], w_in_t, zc, o_p, o_s, w_conv_out[l],
                                  w_o[l], w_mix_out[l], norm2[l][None, :], w_route2)
    pos, sched = _plan(gr, cnt)
    hs, ms = _dispatch(pos, h3, meta)
    ys = _moe(sched, hs, ms, w_up[l], w_gate[l], w_down[l])
    yp, ysm = _final(pos, ys, x1, mod, final_norm[None, :])

    y_prompt = yp.reshape(BATCH, SEQ, D_MODEL)
    y_sample = ysm.reshape(DEC_BATCH, DEC_SEQ, D_MODEL)
    new_ckv = nckv.reshape(BATCH, 1, SEQ, KV_LORA)
    new_krope = jnp.swapaxes(nkr, 1, 2).reshape(BATCH, 1, SEQ, QK_ROPE)
    return (y_prompt, y_sample, new_ckv, new_krope)
```

```python
import numpy as np
import jax
import jax.numpy as jnp
from jax import lax
from jax.experimental import pallas as pl
from jax.experimental.pallas import tpu as pltpu

F32 = jnp.float32
BF16 = jnp.bfloat16

D_MODEL = 1024
BATCH = 16
SEQ = 256
DEC_BATCH = 2
DEC_SEQ = 1024
PAST_LEN = 256
GRID_W = 64
N_HEADS = 8
QK_NOPE = 64
QK_ROPE = 32
V_HEAD = 64
Q_LORA = 256
KV_LORA = 128
ROPE_AXIS = QK_ROPE // 2
ROPE_BASE = 10000.0
ATTN_SCALE = (QK_NOPE + QK_ROPE) ** -0.5
D_CONV = D_MODEL
N_GROUPS = 4
EXP_PER_GROUP = 8
N_EXPERTS = N_GROUPS * EXP_PER_GROUP
D_EXPERT = 256
EPS = 1e-6

T_P = BATCH * SEQ
T_S = DEC_BATCH * DEC_SEQ
T = T_P + T_S
N_COND = 8
LANE = 128
ROPE_LANE0 = QK_NOPE
SMALL_COLS = Q_LORA + KV_LORA + LANE
VMEM_LIMIT = 56 * 1024 * 1024

TM_IN = 1024
TM_POST = 512
TM_MOE = T // N_GROUPS
TM_FINAL = 512
TM_DISP = 1024
N_SLAB = D_MODEL // LANE
MOE_SUB = 256
MOE_EPS = 2
N_VISITS = T // TM_MOE + N_GROUPS - 1
GID_LANE = 40
RANK_LANE = 41
Q_BLK_S = 512
PROMPT_SEQS = 2
CONV_CHUNK = 256


def _dot(a, b):
    return jnp.dot(a, b, preferred_element_type=F32)


def _rms(x):
    return lax.rsqrt(jnp.mean(x * x, axis=-1, keepdims=True) + EPS)


def _slab(t):
    return pl.ds(pl.multiple_of(t * N_SLAB, N_SLAB), N_SLAB)


def _mod_row(i, tm):
    n_prompt = T_P // tm
    return jnp.where(i >= n_prompt, 1 + ((i - n_prompt) * tm) // DEC_SEQ, 0)


def _ada_kernel(cctx_ref, c_ref, w_ref, b_ref, o_ref):
    c = jnp.concatenate([cctx_ref[...], c_ref[...],
                         jnp.zeros((N_COND - 1 - DEC_BATCH, D_MODEL), F32)], axis=0)
    a = (c * jax.nn.sigmoid(c)).astype(BF16)
    o_ref[...] = _dot(a, w_ref[...].astype(BF16)) + b_ref[...]


def _ada(c_ctx, c, w_ada, b_ada):
    n = 6 * D_MODEL
    bn = 1536
    return pl.pallas_call(
        _ada_kernel,
        grid=(n // bn,),
        in_specs=[
            pl.BlockSpec((1, D_MODEL), lambda j: (0, 0)),
            pl.BlockSpec((DEC_BATCH, D_MODEL), lambda j: (0, 0)),
            pl.BlockSpec((D_MODEL, bn), lambda j: (0, j)),
            pl.BlockSpec((1, bn), lambda j: (0, j)),
        ],
        out_specs=pl.BlockSpec((N_COND, bn), lambda j: (0, j)),
        out_shape=jax.ShapeDtypeStruct((N_COND, n), F32),
        compiler_params=pltpu.CompilerParams(
            dimension_semantics=("parallel",), vmem_limit_bytes=VMEM_LIMIT),
        name="ada_mod",
    )(c_ctx, c, w_ada, b_ada)


O_CQ = 3 * D_CONV
O_KR = O_CQ + Q_LORA + KV_LORA
O_GATE = O_KR + QK_ROPE
SMALL_BLK = 512
NT = (((1,), (1,)), ((), ()))


def _dot_nt(a, bt):
    return lax.dot_general(a, bt, NT, preferred_element_type=F32)


def _stream_maps(tm):
    n_prompt = T_P // tm
    return (lambda i, *_: (jnp.minimum(i, n_prompt - 1), 0),
            lambda i, *_: (jnp.maximum(i - n_prompt, 0), 0))


def _inproj_kernel(xp_ref, xs_ref, mod_ref, n1_ref, wc_ref, ws_ref, cw_ref, qn_ref, kvn_ref, wuq_ref,
                   rope_ref, zc_ref, q_ref, ckv_ref, krs_ref, nckv_ref, nkr_ref):
    i = pl.program_id(0)
    is_sample = i >= T_P // TM_IN
    seq = jnp.where(is_sample, DEC_SEQ, SEQ)
    mod = mod_ref[pl.ds(_mod_row(i, TM_IN), 1), :]
    shift1 = mod[:, 0:D_MODEL]
    scale1 = mod[:, D_MODEL:2 * D_MODEL]
    x = jnp.where(is_sample, xs_ref[...], xp_ref[...])
    h = ((x * _rms(x)) * n1_ref[...]) * (1.0 + scale1) + shift1
    hb = h.astype(BF16)

    n_lat = Q_LORA + KV_LORA
    w_small = jnp.concatenate([
        ws_ref[0:n_lat, :].astype(BF16),
        jnp.zeros((ROPE_LANE0, D_MODEL), BF16),
        ws_ref[n_lat:n_lat + QK_ROPE, :].astype(BF16),
        jnp.zeros((LANE - ROPE_LANE0 - QK_ROPE, D_MODEL), BF16)], axis=0)
    sm = _dot_nt(hb, w_small)
    cq = sm[:, 0:Q_LORA]
    ckv_raw = sm[:, Q_LORA:Q_LORA + KV_LORA]
    krs = sm[:, Q_LORA + KV_LORA:SMALL_COLS]
    cqn = (cq * _rms(cq)) * qn_ref[...]
    q = _dot(cqn.astype(BF16), wuq_ref[...])
    ckv = (ckv_raw * _rms(ckv_raw)) * kvn_ref[...]
    ckv_ref[...] = ckv

    @pl.when(jnp.logical_not(is_sample))
    def _():
        nckv_ref[...] = ckv
        for s in range(TM_IN // SEQ):
            kt = krs[s * SEQ:(s + 1) * SEQ, :].T
            nkr_ref[s] = kt[ROPE_LANE0:ROPE_LANE0 + QK_ROPE, :]

    cos = rope_ref[0]
    sin_lo = rope_ref[1]
    sin_hi = rope_ref[2]

    def rot(v):
        return v * cos + pltpu.roll(v, 8, 1) * sin_lo + pltpu.roll(v, LANE - 8, 1) * sin_hi

    krs_ref[...] = rot(krs)
    for hh in range(N_HEADS):
        q_ref[:, LANE * hh:LANE * (hh + 1)] = rot(q[:, LANE * hh:LANE * (hh + 1)]).astype(BF16)

    pos = lax.broadcasted_iota(jnp.int32, (TM_IN, 1), 0) & (seq - 1)
    first = pos == 0
    last = pos == seq - 1
    for j in range(D_CONV // CONV_CHUNK):
        c0 = j * CONV_CHUNK
        bg = _dot_nt(hb, wc_ref[c0:c0 + CONV_CHUNK, :].astype(BF16))
        cg = _dot_nt(hb, wc_ref[D_CONV + c0:D_CONV + c0 + CONV_CHUNK, :].astype(BF16))
        ui = _dot_nt(hb, wc_ref[2 * D_CONV + c0:2 * D_CONV + c0 + CONV_CHUNK, :].astype(BF16))
        u = cg * ui
        u_prev = jnp.where(first, 0.0, pltpu.roll(u, 1, 0))
        u_next = jnp.where(last, 0.0, pltpu.roll(u, TM_IN - 1, 0))
        cw = [cw_ref[:, k * D_CONV + c0:k * D_CONV + c0 + CONV_CHUNK] for k in range(3)]
        conv = u_prev * cw[0] + u * cw[1] + u_next * cw[2]
        zc_ref[:, c0:c0 + CONV_CHUNK] = (bg * conv).astype(BF16)


def _inproj(xp, xs, mod, norm1, w_in_t, conv_w, q_norm, kv_norm, w_uq_slot, rope_tabs):
    n_prompt = T_P // TM_IN
    const = lambda i: (0, 0)
    pmap, smap = _stream_maps(TM_IN)
    once = pl.Buffered(1)
    return pl.pallas_call(
        _inproj_kernel,
        grid=(T // TM_IN,),
        in_specs=[
            pl.BlockSpec((TM_IN, D_MODEL), pmap),
            pl.BlockSpec((TM_IN, D_MODEL), smap),
            pl.BlockSpec((N_COND, 6 * D_MODEL), const),
            pl.BlockSpec((1, D_MODEL), const),
            pl.BlockSpec((O_CQ, D_MODEL), const, pipeline_mode=once),
            pl.BlockSpec((SMALL_BLK, D_MODEL), lambda i: (O_CQ // SMALL_BLK, 0), pipeline_mode=once),
            pl.BlockSpec((1, 3 * D_CONV), const),
            pl.BlockSpec((1, Q_LORA), const),
            pl.BlockSpec((1, KV_LORA), const),
            pl.BlockSpec((Q_LORA, N_HEADS * LANE), const),
            pl.BlockSpec((None, 3, TM_IN, LANE),
                         lambda i: (jnp.where(i >= n_prompt, 1, 0), 0, 0, 0)),
        ],
        out_specs=[
            pl.BlockSpec((TM_IN, D_CONV), lambda i: (i, 0)),
            pl.BlockSpec((TM_IN, N_HEADS * LANE), lambda i: (i, 0)),
            pl.BlockSpec((TM_IN, KV_LORA), lambda i: (i, 0)),
            pl.BlockSpec((TM_IN, LANE), lambda i: (i, 0)),
            pl.BlockSpec((TM_IN, KV_LORA), pmap),
            pl.BlockSpec((TM_IN // SEQ, QK_ROPE, SEQ), lambda i: (jnp.minimum(i, n_prompt - 1), 0, 0)),
        ],
        out_shape=[
            jax.ShapeDtypeStruct((T, D_CONV), BF16),
            jax.ShapeDtypeStruct((T, N_HEADS * LANE), BF16),
            jax.ShapeDtypeStruct((T, KV_LORA), F32),
            jax.ShapeDtypeStruct((T, LANE), F32),
            jax.ShapeDtypeStruct((T_P, KV_LORA), F32),
            jax.ShapeDtypeStruct((BATCH, QK_ROPE, SEQ), F32),
        ],
        compiler_params=pltpu.CompilerParams(
            dimension_semantics=("arbitrary",), vmem_limit_bytes=VMEM_LIMIT),
        name="in_proj",
    )(xp, xs, mod, norm1, w_in_t, w_in_t, conv_w, q_norm, kv_norm, w_uq_slot, rope_tabs)


def _fill_kv(ckv, krs, wukv_ref, kf_scr, v_scr, off):
    m = ckv.shape[0]
    kv = _dot(ckv.astype(BF16), wukv_ref[...])
    for hh in range(N_HEADS):
        kf_scr[hh, off:off + m, :] = (kv[:, LANE * hh:LANE * (hh + 1)] + krs).astype(BF16)
    v_scr[off:off + m, :] = kv[:, N_HEADS * LANE:].astype(BF16)


def _attend(q_ref, r0, rows, kf_scr, v_scr, o_ref):
    for pair in range(N_HEADS // 2):
        acc = None
        for hh in (2 * pair, 2 * pair + 1):
            qh = q_ref[r0:r0 + rows, LANE * hh:LANE * (hh + 1)]
            s = _dot_nt(qh, kf_scr[hh]) * ATTN_SCALE
            e = jnp.exp(s - jnp.max(s, axis=-1, keepdims=True))
            p = (e / jnp.sum(e, axis=-1, keepdims=True)).astype(BF16)
            part = _dot(p, v_scr[:, LANE * hh:LANE * (hh + 1)])
            acc = part if acc is None else acc + part
        o_ref[r0:r0 + rows, LANE * pair:LANE * (pair + 1)] = acc.astype(BF16)


def _attn_prompt_kernel(q_ref, ckv_ref, krs_ref, wukv_ref, o_ref):
    for s in range(PROMPT_SEQS):
        r0 = s * SEQ
        kv = _dot(ckv_ref[r0:r0 + SEQ, :].astype(BF16), wukv_ref[...])
        krs = krs_ref[r0:r0 + SEQ, :]
        for pair in range(N_HEADS // 2):
            acc = None
            for hh in (2 * pair, 2 * pair + 1):
                kf = (kv[:, LANE * hh:LANE * (hh + 1)] + krs).astype(BF16)
                vh = kv[:, LANE * (N_HEADS + hh):LANE * (N_HEADS + hh + 1)].astype(BF16)
                sc = _dot_nt(q_ref[r0:r0 + SEQ, LANE * hh:LANE * (hh + 1)], kf) * ATTN_SCALE
                e = jnp.exp(sc - jnp.max(sc, axis=-1, keepdims=True))
                p = (e / jnp.sum(e, axis=-1, keepdims=True)).astype(BF16)
                part = _dot(p, vh)
                acc = part if acc is None else acc + part
            o_ref[r0:r0 + SEQ, LANE * pair:LANE * (pair + 1)] = acc.astype(BF16)


def _attn_sample_kernel(q_ref, ckv_ref, cckv_ref, krs_ref, ckrs_ref, wukv_ref, o_ref, kf_scr, v_scr):
    @pl.when(pl.program_id(1) == 0)
    def _():
        _fill_kv(ckv_ref[...], krs_ref[...], wukv_ref, kf_scr, v_scr, 0)
        _fill_kv(cckv_ref[...], ckrs_ref[...], wukv_ref, kf_scr, v_scr, DEC_SEQ)

    _attend(q_ref, 0, Q_BLK_S, kf_scr, v_scr, o_ref)


def _attention(q, ckv, krs, cache_ckv, cache_krs, w_ukv_slot):
    kv_cols = 2 * N_HEADS * LANE
    n_o = N_HEADS * V_HEAD
    steps = BATCH // PROMPT_SEQS
    rows = PROMPT_SEQS * SEQ
    o_prompt = pl.pallas_call(
        _attn_prompt_kernel,
        grid=(steps,),
        in_specs=[
            pl.BlockSpec((rows, N_HEADS * LANE), lambda b: (b, 0)),
            pl.BlockSpec((rows, KV_LORA), lambda b: (b, 0)),
            pl.BlockSpec((rows, LANE), lambda b: (b, 0)),
            pl.BlockSpec((KV_LORA, kv_cols), lambda b: (0, 0)),
        ],
        out_specs=pl.BlockSpec((rows, n_o), lambda b: (b, 0)),
        out_shape=jax.ShapeDtypeStruct((T_P, n_o), BF16),
        compiler_params=pltpu.CompilerParams(dimension_semantics=("parallel",),
                                             vmem_limit_bytes=VMEM_LIMIT),
        name="attn_prompt",
    )(q, ckv, krs, w_ukv_slot)

    m_all = DEC_SEQ + PAST_LEN
    nq = DEC_SEQ // Q_BLK_S
    q0 = T_P // Q_BLK_S
    s0 = T_P // DEC_SEQ
    o_sample = pl.pallas_call(
        _attn_sample_kernel,
        grid=(DEC_BATCH, nq),
        in_specs=[
            pl.BlockSpec((Q_BLK_S, N_HEADS * LANE), lambda b, j: (q0 + b * nq + j, 0)),
            pl.BlockSpec((DEC_SEQ, KV_LORA), lambda b, j: (s0 + b, 0)),
            pl.BlockSpec((None, PAST_LEN, KV_LORA), lambda b, j: (b, 0, 0)),
            pl.BlockSpec((DEC_SEQ, LANE), lambda b, j: (s0 + b, 0)),
            pl.BlockSpec((None, PAST_LEN, LANE), lambda b, j: (b, 0, 0)),
            pl.BlockSpec((KV_LORA, kv_cols), lambda b, j: (0, 0)),
        ],
        out_specs=pl.BlockSpec((Q_BLK_S, n_o), lambda b, j: (b * nq + j, 0)),
        out_shape=jax.ShapeDtypeStruct((T_S, n_o), BF16),
        scratch_shapes=[pltpu.VMEM((N_HEADS, m_all, LANE), BF16),
                        pltpu.VMEM((m_all, N_HEADS * LANE), BF16)],
        compiler_params=pltpu.CompilerParams(dimension_semantics=("parallel", "arbitrary"),
                                             vmem_limit_bytes=VMEM_LIMIT),
        name="attn_sample",
    )(q, ckv, cache_ckv, krs, cache_krs, w_ukv_slot)
    return o_prompt, o_sample


def _route(logits):
    lane = lax.broadcasted_iota(jnp.int32, logits.shape, 1)
    neg = -jnp.inf
    big = jnp.int32(1 << 20)
    gmask = (lane >= N_EXPERTS) & (lane < N_EXPERTS + N_GROUPS)
    gl = jnp.where(gmask, logits, neg)
    gmax = jnp.max(gl, axis=-1, keepdims=True)
    gsum = jnp.sum(jnp.where(gmask, jnp.exp(gl - gmax), 0.0), axis=-1, keepdims=True)
    p_g = 1.0 / gsum
    g_idx = jnp.min(jnp.where(gl == gmax, lane, big), axis=-1, keepdims=True) - N_EXPERTS

    emask = (lane < N_EXPERTS) & ((lane >> 3) == g_idx)
    el = jnp.where(emask, logits, neg)
    m1 = jnp.max(el, axis=-1, keepdims=True)
    i1 = jnp.min(jnp.where(el == m1, lane, big), axis=-1, keepdims=True)
    el2 = jnp.where(lane == i1, neg, el)
    m2 = jnp.max(el2, axis=-1, keepdims=True)
    i2 = jnp.min(jnp.where(el2 == m2, lane, big), axis=-1, keepdims=True)
    z = jnp.sum(jnp.where(emask, jnp.exp(el - m1), 0.0), axis=-1, keepdims=True)
    p1 = 1.0 / z
    p2 = jnp.exp(m2 - m1) / z
    tot = p1 + p2
    w1 = p_g * p1 / tot
    w2 = p_g * p2 / tot
    return jnp.where(lane == i1, w1, 0.0) + jnp.where(lane == i2, w2, 0.0), g_idx


def _post_kernel(xp_ref, xs_ref, mod_ref, n1_ref, wg_ref, zc_ref, op_ref, os_ref, wco_ref, wo_ref,
                 wmix_ref, n2_ref, wr_ref, x1_ref, h3_ref, meta_ref, cnt_ref, gr_ref):
    i = pl.program_id(0)
    is_sample = i >= T_P // TM_POST
    x = jnp.where(is_sample, xs_ref[...], xp_ref[...])
    o = jnp.where(is_sample, os_ref[...], op_ref[...])
    mod = mod_ref[pl.ds(_mod_row(i, TM_POST), 1), :]
    shift1 = mod[:, 0:D_MODEL]
    scale1 = mod[:, D_MODEL:2 * D_MODEL]
    gate1 = mod[:, 2 * D_MODEL:3 * D_MODEL]
    shift2 = mod[:, 3 * D_MODEL:4 * D_MODEL]
    scale2 = mod[:, 4 * D_MODEL:5 * D_MODEL]
    y_conv = _dot(zc_ref[...], wco_ref[...].astype(BF16))
    y_mla = _dot(o, wo_ref[...].astype(BF16))
    h = ((x * _rms(x)) * n1_ref[...]) * (1.0 + scale1) + shift1
    g = _dot_nt(h.astype(BF16), wg_ref[...].astype(BF16))
    merged = (jax.nn.sigmoid(g[:, 0:D_MODEL]) * y_conv
              + jax.nn.sigmoid(g[:, D_MODEL:2 * D_MODEL]) * y_mla)
    y = _dot(merged.astype(BF16), wmix_ref[...].astype(BF16))
    x1 = x + gate1 * y
    x1_ref[...] = x1
    h2 = ((x1 * _rms(x1)) * n2_ref[...]) * (1.0 + scale2) + shift2
    h2_hi = h2.astype(BF16)
    h2_lo = (h2 - h2_hi.astype(F32)).astype(BF16)
    hh = _dot(h2_hi, wr_ref[...])
    logits = hh[:, 0:LANE] + hh[:, LANE:2 * LANE] + _dot(h2_lo, wr_ref[:, 0:LANE])
    comb, g_idx = _route(logits)

    lane = lax.broadcasted_iota(jnp.int32, comb.shape, 1)
    onehot = lane == g_idx + N_EXPERTS
    r_i = lax.broadcasted_iota(jnp.int32, (TM_POST, TM_POST), 0)
    c_i = lax.broadcasted_iota(jnp.int32, (TM_POST, TM_POST), 1)
    lower = jnp.where(c_i < r_i, 1.0, 0.0).astype(BF16)
    before = _dot(lower, jnp.where(onehot, 1.0, 0.0).astype(BF16))
    rank = jnp.sum(jnp.where(onehot, before, 0.0), axis=-1, keepdims=True)
    counts = jnp.sum(jnp.where(onehot, 1.0, 0.0), axis=0, keepdims=True)
    cnt_ref[...] = jnp.broadcast_to(counts, cnt_ref.shape)

    meta_ref[...] = comb
    idx = jnp.where(lane == GID_LANE, g_idx.astype(F32), 0.0) + jnp.where(lane == RANK_LANE, rank, 0.0)
    idx_hi = idx.astype(BF16)
    idx_lo = (idx - idx_hi.astype(F32)).astype(BF16)
    s_row = lax.broadcasted_iota(jnp.int32, (8, LANE), 0)
    s_lane = lax.broadcasted_iota(jnp.int32, (8, LANE), 1)
    sel = jnp.where(s_lane == GID_LANE + s_row, 1.0, 0.0).astype(BF16)
    gr_ref[...] = _dot_nt(sel, idx_hi) + _dot_nt(sel, idx_lo)
    for c in range(N_SLAB):
        h3_ref[pl.ds(c, TM_POST, stride=N_SLAB), :] = h2[:, LANE * c:LANE * (c + 1)]


def _post(xp, xs, mod, norm1, w_in_t, zc, o_p, o_s, w_conv_out, w_o, w_mix_out, norm2, w_route):
    const = lambda i: (0, 0)
    row = lambda i: (i, 0)
    pmap, smap = _stream_maps(TM_POST)
    once = pl.Buffered(1)
    return pl.pallas_call(
        _post_kernel,
        grid=(T // TM_POST,),
        in_specs=[
            pl.BlockSpec((TM_POST, D_MODEL), pmap),
            pl.BlockSpec((TM_POST, D_MODEL), smap),
            pl.BlockSpec((N_COND, 6 * D_MODEL), const),
            pl.BlockSpec((1, D_MODEL), const),
            pl.BlockSpec((pl.Element(2 * D_MODEL), pl.Element(D_MODEL)), lambda i: (O_GATE, 0),
                         pipeline_mode=once),
            pl.BlockSpec((TM_POST, D_CONV), row),
            pl.BlockSpec((TM_POST, N_HEADS * V_HEAD), pmap),
            pl.BlockSpec((TM_POST, N_HEADS * V_HEAD), smap),
            pl.BlockSpec((D_CONV, D_MODEL), const, pipeline_mode=once),
            pl.BlockSpec((N_HEADS * V_HEAD, D_MODEL), const, pipeline_mode=once),
            pl.BlockSpec((D_MODEL, D_MODEL), const, pipeline_mode=once),
            pl.BlockSpec((1, D_MODEL), const),
            pl.BlockSpec((D_MODEL, 2 * LANE), const),
        ],
        out_specs=[
            pl.BlockSpec((TM_POST, D_MODEL), row),
            pl.BlockSpec((TM_POST * N_SLAB, LANE), row),
            pl.BlockSpec((TM_POST, LANE), row),
            pl.BlockSpec((None, 8, LANE), lambda i: (i, 0, 0)),
            pl.BlockSpec((8, TM_POST), lambda i: (0, i)),
        ],
        out_shape=[
            jax.ShapeDtypeStruct((T, D_MODEL), F32),
            jax.ShapeDtypeStruct((T * N_SLAB, LANE), F32),
            jax.ShapeDtypeStruct((T, LANE), F32),
            jax.ShapeDtypeStruct((T // TM_POST, 8, LANE), F32),
            jax.ShapeDtypeStruct((8, T), F32),
        ],
        compiler_params=pltpu.CompilerParams(
            dimension_semantics=("parallel",), vmem_limit_bytes=VMEM_LIMIT),
        name="post_mixer",
    )(xp, xs, mod, norm1, w_in_t, zc, o_p, o_s, w_conv_out, w_o, w_mix_out, norm2, w_route)


DISP_IN = T // TM_DISP
DISP_OUT = T // TM_MOE


def _dispatch_kernel(pos_ref, h3_ref, m_ref, hs_ref, ms_ref, xs_ref, mss_ref):
    i = pl.program_id(0)

    @pl.when(i < DISP_IN)
    def _():
        def body(r, carry):
            p = pos_ref[0, r]
            xs_ref[_slab(p), :] = h3_ref[_slab(r), :]
            mss_ref[pl.ds(p, 1), :] = m_ref[pl.ds(r, 1), :]
            return carry

        lax.fori_loop(0, TM_DISP, body, 0, unroll=8)

    @pl.when(i >= DISP_IN)
    def _():
        row0 = pl.multiple_of((i - DISP_IN) * TM_MOE, TM_MOE)
        for c in range(N_SLAB):
            hs_ref[:, LANE * c:LANE * (c + 1)] = (
                xs_ref[pl.ds(row0 * N_SLAB + c, TM_MOE, stride=N_SLAB), :].astype(BF16))
        ms_ref[...] = mss_ref[pl.ds(row0, TM_MOE), :]


def _dispatch(pos, h3, meta):
    n_slab = N_SLAB
    in_map = lambda i: (jnp.minimum(i, DISP_IN - 1), 0)
    out_map = lambda i: (jnp.maximum(i - DISP_IN, 0), 0)
    return pl.pallas_call(
        _dispatch_kernel,
        grid=(DISP_IN + DISP_OUT,),
        in_specs=[pl.BlockSpec((None, 1, TM_DISP), lambda i: (jnp.minimum(i, DISP_IN - 1), 0, 0),
                               memory_space=pltpu.SMEM),
                  pl.BlockSpec((TM_DISP * n_slab, LANE), in_map),
                  pl.BlockSpec((TM_DISP, LANE), in_map)],
        out_specs=[pl.BlockSpec((TM_MOE, D_MODEL), out_map),
                   pl.BlockSpec((TM_MOE, LANE), out_map)],
        scratch_shapes=[pltpu.VMEM((T * n_slab, LANE), F32),
                        pltpu.VMEM((T, LANE), F32)],
        out_shape=[jax.ShapeDtypeStruct((T, D_MODEL), BF16),
                   jax.ShapeDtypeStruct((T, LANE), F32)],
        compiler_params=pltpu.CompilerParams(
            dimension_semantics=("arbitrary",), vmem_limit_bytes=VMEM_LIMIT),
        name="moe_dispatch",
    )(pos.reshape(DISP_IN, 1, TM_DISP), h3, meta)


def _moe_kernel(sched_ref, hs_ref, ms_ref, wup_ref, wgate_ref, wdown_ref, y3_ref, acc_ref):
    v = pl.program_id(0)
    j = pl.program_id(1)
    valid = sched_ref[V_VALID, v] == 1
    lo = sched_ref[V_LO, v]
    hi = sched_ref[V_HI, v]
    e0 = sched_ref[V_GROUP, v] * EXP_PER_GROUP + j * MOE_EPS
    full = (hi - lo) * 4 >= TM_MOE * 3

    @pl.when(valid & (j == 0) & (sched_ref[V_FIRST, v] == 1))
    def _():
        acc_ref[...] = jnp.zeros_like(acc_ref)

    def expert_rows(r0, rows):
        w_in2 = jnp.concatenate(
            [w[k].astype(BF16) for k in range(MOE_EPS) for w in (wup_ref, wgate_ref)], axis=1)
        ag = _dot(hs_ref[pl.ds(r0, rows), :], w_in2)
        comb = ms_ref[pl.ds(r0, rows), :]
        lane = lax.broadcasted_iota(jnp.int32, comb.shape, 1)
        acts = []
        for k in range(MOE_EPS):
            a = ag[:, 2 * k * D_EXPERT:(2 * k + 1) * D_EXPERT]
            g = ag[:, (2 * k + 1) * D_EXPERT:(2 * k + 2) * D_EXPERT]
            cw = jnp.sum(jnp.where(lane == e0 + k, comb, 0.0), axis=-1, keepdims=True)
            acts.append(((g * jax.nn.sigmoid(g)) * a * cw).astype(BF16))
        w_out = jnp.concatenate([wdown_ref[k].astype(BF16) for k in range(MOE_EPS)], axis=0)
        acc_ref[pl.ds(r0, rows), :] += _dot(jnp.concatenate(acts, axis=1), w_out)

    @pl.when(valid & full)
    def _():
        expert_rows(0, TM_MOE)

    @pl.when(valid & jnp.logical_not(full))
    def _():
        def sub_block(s, carry):
            expert_rows(pl.multiple_of(s * MOE_SUB, MOE_SUB), MOE_SUB)
            return carry

        lax.fori_loop(lo // MOE_SUB, (hi + MOE_SUB - 1) // MOE_SUB, sub_block, 0)

    @pl.when(valid & (j == EXP_PER_GROUP // MOE_EPS - 1) & (sched_ref[V_LAST, v] == 1))
    def _():
        for c in range(N_SLAB):
            y3_ref[pl.ds(c, TM_MOE, stride=N_SLAB), :] = acc_ref[:, LANE * c:LANE * (c + 1)]


def _moe(sched, hs, ms, w_up, w_gate, w_down):
    steps = EXP_PER_GROUP // MOE_EPS
    wmap = lambda v, j, sched: (
        sched[V_GROUP, v] * steps + jnp.where(sched[V_VALID, v] == 1, j, steps - 1), 0, 0)
    tmap = lambda v, j, sched: (sched[V_TILE, v], 0)
    n_slab = D_MODEL // LANE
    return pl.pallas_call(
        _moe_kernel,
        grid_spec=pltpu.PrefetchScalarGridSpec(
            num_scalar_prefetch=1,
            grid=(N_VISITS, steps),
            in_specs=[
                pl.BlockSpec((TM_MOE, D_MODEL), tmap),
                pl.BlockSpec((TM_MOE, LANE), tmap),
                pl.BlockSpec((MOE_EPS, D_MODEL, D_EXPERT), wmap),
                pl.BlockSpec((MOE_EPS, D_MODEL, D_EXPERT), wmap),
                pl.BlockSpec((MOE_EPS, D_EXPERT, D_MODEL), wmap),
            ],
            out_specs=pl.BlockSpec((TM_MOE * n_slab, LANE), tmap),
            scratch_shapes=[pltpu.VMEM((TM_MOE, D_MODEL), F32)],
        ),
        out_shape=jax.ShapeDtypeStruct((T * n_slab, LANE), F32),
        compiler_params=pltpu.CompilerParams(
            dimension_semantics=("arbitrary", "arbitrary"), vmem_limit_bytes=VMEM_LIMIT),
        name="moe_grouped",
    )(sched, hs, ms, w_up, w_gate, w_down)


def _final_kernel(pos_ref, ys_ref, x1_ref, mod_ref, fn_ref, yp_ref, ysm_ref, g_ref):
    i = pl.program_id(0)

    def body(r, carry):
        g_ref[_slab(r), :] = ys_ref[_slab(pos_ref[0, r]), :]
        return carry

    lax.fori_loop(0, TM_FINAL, body, 0, unroll=8)
    mod = mod_ref[pl.ds(_mod_row(i, TM_FINAL), 1), :]
    gate2 = mod[:, 5 * D_MODEL:6 * D_MODEL]
    moe = jnp.concatenate([g_ref[pl.ds(c, TM_FINAL, stride=N_SLAB), :] for c in range(N_SLAB)], axis=1)
    x2 = x1_ref[...] + gate2 * moe
    y = (x2 * _rms(x2)) * fn_ref[...]
    is_sample = i >= T_P // TM_FINAL

    @pl.when(jnp.logical_not(is_sample))
    def _():
        yp_ref[...] = y

    @pl.when(is_sample)
    def _():
        ysm_ref[...] = y


def _final(pos, ys, x1, mod, final_norm):
    n_slab = D_MODEL // LANE
    pmap, smap = _stream_maps(TM_FINAL)
    return pl.pallas_call(
        _final_kernel,
        grid=(T // TM_FINAL,),
        in_specs=[
            pl.BlockSpec((None, 1, TM_FINAL), lambda i: (i, 0, 0), memory_space=pltpu.SMEM),
            pl.BlockSpec((T * n_slab, LANE), lambda i: (0, 0), pipeline_mode=pl.Buffered(1)),
            pl.BlockSpec((TM_FINAL, D_MODEL), lambda i: (i, 0)),
            pl.BlockSpec((N_COND, 6 * D_MODEL), lambda i: (0, 0)),
            pl.BlockSpec((1, D_MODEL), lambda i: (0, 0)),
        ],
        out_specs=[pl.BlockSpec((TM_FINAL, D_MODEL), pmap),
                   pl.BlockSpec((TM_FINAL, D_MODEL), smap)],
        scratch_shapes=[pltpu.VMEM((TM_FINAL * n_slab, LANE), F32)],
        out_shape=[jax.ShapeDtypeStruct((T_P, D_MODEL), F32),
                   jax.ShapeDtypeStruct((T_S, D_MODEL), F32)],
        compiler_params=pltpu.CompilerParams(
            dimension_semantics=("arbitrary",), vmem_limit_bytes=VMEM_LIMIT),
        name="moe_unsort_final",
    )(pos.reshape(T // TM_FINAL, 1, TM_FINAL), ys, x1, mod, final_norm)


def _plan_kernel(gr_ref, cnt_ref, pos_ref, sched_ref):
    n_post = T // TM_POST
    lane = lax.broadcasted_iota(jnp.int32, (1, LANE), 1)
    grp_lane = lambda v, g: v[:, N_EXPERTS + g:N_EXPERTS + g + 1]
    counts = [cnt_ref[k, 0:1, :] for k in range(n_post)]
    gtot_v = counts[0]
    for k in range(1, n_post):
        gtot_v = gtot_v + counts[k]
    gtot = [grp_lane(gtot_v, g) for g in range(N_GROUPS)]
    goff = [jnp.zeros((1, 1), F32)]
    for g in range(1, N_GROUPS):
        goff.append(goff[-1] + gtot[g - 1])

    before = [jnp.zeros((1, 1), F32) for _ in range(N_GROUPS)]
    rows_per_tile = TM_POST // LANE
    for k in range(n_post):
        gid = gr_ref[0:1, TM_POST * k:TM_POST * (k + 1)]
        pos = gr_ref[1:2, TM_POST * k:TM_POST * (k + 1)]
        for g in range(N_GROUPS):
            pos = pos + jnp.where(gid == float(g), goff[g] + before[g], 0.0)
            before[g] = before[g] + grp_lane(counts[k], g)
        for r in range(rows_per_tile):
            pos_ref[rows_per_tile * k + r:rows_per_tile * k + r + 1, :] = (
                pos[:, LANE * r:LANE * (r + 1)].astype(jnp.int32))

    zero = jnp.zeros((1, LANE), F32)
    rows = {name: zero for name in ("vt", "vg", "vlo", "vhi")}
    slot = jnp.zeros((1, 1), F32)
    last_t = jnp.zeros((1, 1), F32)
    last_g = jnp.zeros((1, 1), F32)
    for i in range(T // TM_MOE):
        for g in range(N_GROUPS):
            lo = jnp.clip(goff[g] - float(TM_MOE * i), 0.0, float(TM_MOE))
            hi = jnp.clip(goff[g] + gtot[g] - float(TM_MOE * i), 0.0, float(TM_MOE))
            ok = hi > lo
            here = ok & (lane == slot.astype(jnp.int32))
            rows["vt"] = jnp.where(here, float(i), rows["vt"])
            rows["vg"] = jnp.where(here, float(g), rows["vg"])
            rows["vlo"] = jnp.where(here, lo, rows["vlo"])
            rows["vhi"] = jnp.where(here, hi, rows["vhi"])
            last_t = jnp.where(ok, float(i), last_t)
            last_g = jnp.where(ok, float(g), last_g)
            slot = slot + jnp.where(ok, 1.0, 0.0)
    n_ok = slot.astype(jnp.int32)
    valid = lane < n_ok
    vt = jnp.where(valid, rows["vt"], last_t)
    vg = jnp.where(valid, rows["vg"], last_g)
    prev_t = pltpu.roll(jnp.broadcast_to(vt, (8, LANE)), 1, 1)[0:1, :]
    next_t = pltpu.roll(jnp.broadcast_to(vt, (8, LANE)), LANE - 1, 1)[0:1, :]
    first = jnp.where((lane == 0) | (vt != prev_t), 1.0, 0.0)
    last = jnp.where((lane == n_ok - 1) | (vt != next_t), 1.0, 0.0)
    table = [vt, vg, rows["vlo"], rows["vhi"], first, last, jnp.where(valid, 1.0, 0.0), zero]
    for r, row in enumerate(table):
        sched_ref[r:r + 1, :] = row.astype(jnp.int32)


V_TILE, V_GROUP, V_LO, V_HI, V_FIRST, V_LAST, V_VALID = range(7)


def _plan(gr, cnt):
    pos, sched = pl.pallas_call(
        _plan_kernel,
        out_shape=[jax.ShapeDtypeStruct((T // LANE, LANE), jnp.int32),
                   jax.ShapeDtypeStruct((8, LANE), jnp.int32)],
        name="moe_plan",
    )(gr, cnt)
    return pos.reshape(T), sched


def _rope_tables():
    n = np.arange(DEC_SEQ)
    pos = np.stack([n // GRID_W, n % GRID_W], axis=1).astype(np.float32)
    half = ROPE_AXIS // 2
    inv = (1.0 / (ROPE_BASE ** (np.arange(0, ROPE_AXIS, 2, dtype=np.float32) / ROPE_AXIS))).astype(np.float32)
    ang = (pos[:, :, None] * inv[None, None, :]).astype(np.float32)
    cos = np.cos(ang).astype(np.float32)
    sin = np.sin(ang).astype(np.float32)
    tabs = np.zeros((2, 3, DEC_SEQ, LANE), np.float32)
    tabs[:, 0] = 1.0
    for a in range(2):
        lo = ROPE_LANE0 + a * ROPE_AXIS
        tabs[1, 0, :, lo:lo + half] = cos[:, a]
        tabs[1, 0, :, lo + half:lo + 2 * half] = cos[:, a]
        tabs[1, 1, :, lo + half:lo + 2 * half] = sin[:, a]
        tabs[1, 2, :, lo:lo + half] = -sin[:, a]
    return jnp.asarray(tabs)


def kernel(x_prompt, x_sample, cache_ckv, cache_krope, c, c_ctx, norm1, w_ada, b_ada, w_in, conv_w,
           w_conv_out, q_norm, w_uq, kv_norm, w_ukv, w_o, w_mix_out, norm2, w_grp, w_exp, w_up,
           w_gate, w_down, final_norm):
    l = 0
    xp = x_prompt.reshape(T_P, D_MODEL)
    xs = x_sample.reshape(T_S, D_MODEL)
    mod = _ada(c_ctx[None, :], c, w_ada[l], b_ada[l][None, :])

    w_in_t = w_in[l].T
    w_uq_slot = jnp.pad(w_uq[l].reshape(Q_LORA, N_HEADS, QK_NOPE + QK_ROPE),
                        ((0, 0), (0, 0), (0, LANE - QK_NOPE - QK_ROPE))
                        ).reshape(Q_LORA, N_HEADS * LANE).astype(BF16)
    wkv = w_ukv[l].reshape(KV_LORA, N_HEADS, QK_NOPE + V_HEAD)
    wk_slot = jnp.pad(wkv[:, :, :QK_NOPE], ((0, 0), (0, 0), (0, LANE - QK_NOPE)))
    wv = wkv[:, :, QK_NOPE:].reshape(KV_LORA, N_HEADS // 2, 2, V_HEAD)
    zero = jnp.zeros_like(wv[:, :, 0])
    wv_slot = jnp.stack([jnp.concatenate([wv[:, :, 0], zero], axis=-1),
                         jnp.concatenate([zero, wv[:, :, 1]], axis=-1)], axis=2)
    w_ukv_slot = jnp.concatenate([wk_slot.reshape(KV_LORA, N_HEADS * LANE),
                                  wv_slot.reshape(KV_LORA, N_HEADS * LANE)], axis=1).astype(BF16)
    w_route = jnp.pad(jnp.concatenate([w_exp[l], w_grp[l]], axis=1),
                      ((0, 0), (0, LANE - N_EXPERTS - N_GROUPS)))
    w_route_hi = w_route.astype(BF16)
    w_route_lo = (w_route - w_route_hi.astype(F32)).astype(BF16)
    w_route2 = jnp.concatenate([w_route_hi, w_route_lo], axis=1)
    cache_krs = jnp.pad(cache_krope[:, l], ((0, 0), (0, 0), (ROPE_LANE0, LANE - ROPE_LANE0 - QK_ROPE)))

    zc, q, ckv, krs, nckv, nkr = _inproj(xp, xs, mod, norm1[l][None, :], w_in_t,
                                         conv_w[l].reshape(1, 3 * D_CONV),
                              q_norm[l][None, :], kv_norm[l][None, :], w_uq_slot, _rope_tables())
    o_p, o_s = _attention(q, ckv, krs, cache_ckv[:, l], cache_krs, w_ukv_slot)
    x1, h3, meta, cnt, gr = _post(xp, xs, mod, norm1[l][None, :], w_in_t, zc, o_p, o_s, w_conv_out[l],
                                  w_o[l], w_mix_out[l], norm2[l][None, :], w_route2)
    pos, sched = _plan(gr, cnt)
    hs, ms = _dispatch(pos, h3, meta)
    ys = _moe(sched, hs, ms, w_up[l], w_gate[l], w_down[l])
    yp, ysm = _final(pos, ys, x1, mod, final_norm[None, :])

    y_prompt = yp.reshape(BATCH, SEQ, D_MODEL)
    y_sample = ysm.reshape(DEC_BATCH, DEC_SEQ, D_MODEL)
    new_ckv = nckv.reshape(BATCH, 1, SEQ, KV_LORA)
    new_krope = jnp.swapaxes(nkr, 1, 2).reshape(BATCH, 1, SEQ, QK_ROPE)
    return (y_prompt, y_sample, new_ckv, new_krope)
```

```python
import numpy as np
import jax
import jax.numpy as jnp
from jax import lax
from jax.experimental import pallas as pl
from jax.experimental.pallas import tpu as pltpu

F32 = jnp.float32
BF16 = jnp.bfloat16

D_MODEL = 1024
BATCH = 16
SEQ = 256
DEC_BATCH = 2
DEC_SEQ = 1024
PAST_LEN = 256
GRID_W = 64
N_HEADS = 8
QK_NOPE = 64
QK_ROPE = 32
V_HEAD = 64
Q_LORA = 256
KV_LORA = 128
ROPE_AXIS = QK_ROPE // 2
ROPE_BASE = 10000.0
ATTN_SCALE = (QK_NOPE + QK_ROPE) ** -0.5
D_CONV = D_MODEL
N_GROUPS = 4
EXP_PER_GROUP = 8
N_EXPERTS = N_GROUPS * EXP_PER_GROUP
D_EXPERT = 256
EPS = 1e-6

T_P = BATCH * SEQ
T_S = DEC_BATCH * DEC_SEQ
T = T_P + T_S
N_COND = 8
LANE = 128
ROPE_LANE0 = QK_NOPE
SMALL_COLS = Q_LORA + KV_LORA + LANE
VMEM_LIMIT = 56 * 1024 * 1024

TM_IN = 1024
TM_POST = 512
TM_MOE = T // N_GROUPS
TM_FINAL = 512
TM_DISP = 1024
N_SLAB = D_MODEL // LANE
MOE_SUB = 256
MOE_EPS = 2
N_VISITS = T // TM_MOE + N_GROUPS - 1
GID_LANE = 40
RANK_LANE = 41
Q_BLK_S = 512
PROMPT_SEQS = 2
CONV_CHUNK = 256


def _dot(a, b):
    return jnp.dot(a, b, preferred_element_type=F32)


def _rms(x):
    return lax.rsqrt(jnp.mean(x * x, axis=-1, keepdims=True) + EPS)


def _slab(t):
    return pl.ds(pl.multiple_of(t * N_SLAB, N_SLAB), N_SLAB)


def _mod_row(i, tm):
    n_prompt = T_P // tm
    return jnp.where(i >= n_prompt, 1 + ((i - n_prompt) * tm) // DEC_SEQ, 0)


def _ada_kernel(cctx_ref, c_ref, w_ref, b_ref, o_ref):
    c = jnp.concatenate([cctx_ref[...], c_ref[...],
                         jnp.zeros((N_COND - 1 - DEC_BATCH, D_MODEL), F32)], axis=0)
    a = (c * jax.nn.sigmoid(c)).astype(BF16)
    o_ref[...] = _dot(a, w_ref[...].astype(BF16)) + b_ref[...]


def _ada(c_ctx, c, w_ada, b_ada):
    n = 6 * D_MODEL
    bn = 1536
    return pl.pallas_call(
        _ada_kernel,
        grid=(n // bn,),
        in_specs=[
            pl.BlockSpec((1, D_MODEL), lambda j: (0, 0)),
            pl.BlockSpec((DEC_BATCH, D_MODEL), lambda j: (0, 0)),
            pl.BlockSpec((D_MODEL, bn), lambda j: (0, j)),
            pl.BlockSpec((1, bn), lambda j: (0, j)),
        ],
        out_specs=pl.BlockSpec((N_COND, bn), lambda j: (0, j)),
        out_shape=jax.ShapeDtypeStruct((N_COND, n), F32),
        compiler_params=pltpu.CompilerParams(
            dimension_semantics=("parallel",), vmem_limit_bytes=VMEM_LIMIT),
        name="ada_mod",
    )(c_ctx, c, w_ada, b_ada)


O_CQ = 3 * D_CONV
O_KR = O_CQ + Q_LORA + KV_LORA
O_GATE = O_KR + QK_ROPE
SMALL_BLK = 512
NT = (((1,), (1,)), ((), ()))


def _dot_nt(a, bt):
    return lax.dot_general(a, bt, NT, preferred_element_type=F32)


def _stream_maps(tm):
    n_prompt = T_P // tm
    return (lambda i, *_: (jnp.minimum(i, n_prompt - 1), 0),
            lambda i, *_: (jnp.maximum(i - n_prompt, 0), 0))


def _inproj_kernel(xp_ref, xs_ref, mod_ref, n1_ref, wc_ref, ws_ref, cw_ref, qn_ref, kvn_ref, wuq_ref,
                   rope_ref, zc_ref, q_ref, ckv_ref, krs_ref, nckv_ref, nkr_ref):
    i = pl.program_id(0)
    is_sample = i >= T_P // TM_IN
    seq = jnp.where(is_sample, DEC_SEQ, SEQ)
    mod = mod_ref[pl.ds(_mod_row(i, TM_IN), 1), :]
    shift1 = mod[:, 0:D_MODEL]
    scale1 = mod[:, D_MODEL:2 * D_MODEL]
    x = jnp.where(is_sample, xs_ref[...], xp_ref[...])
    h = ((x * _rms(x)) * n1_ref[...]) * (1.0 + scale1) + shift1
    hb = h.astype(BF16)

    n_lat = Q_LORA + KV_LORA
    w_small = jnp.concatenate([
        ws_ref[0:n_lat, :].astype(BF16),
        jnp.zeros((ROPE_LANE0, D_MODEL), BF16),
        ws_ref[n_lat:n_lat + QK_ROPE, :].astype(BF16),
        jnp.zeros((LANE - ROPE_LANE0 - QK_ROPE, D_MODEL), BF16)], axis=0)
    sm = _dot_nt(hb, w_small)
    cq = sm[:, 0:Q_LORA]
    ckv_raw = sm[:, Q_LORA:Q_LORA + KV_LORA]
    krs = sm[:, Q_LORA + KV_LORA:SMALL_COLS]
    cqn = (cq * _rms(cq)) * qn_ref[...]
    q = _dot_nt(cqn.astype(BF16), wuq_ref[...])
    ckv = (ckv_raw * _rms(ckv_raw)) * kvn_ref[...]
    ckv_ref[...] = ckv

    @pl.when(jnp.logical_not(is_sample))
    def _():
        nckv_ref[...] = ckv
        for s in range(TM_IN // SEQ):
            kt = krs[s * SEQ:(s + 1) * SEQ, :].T
            nkr_ref[s] = kt[ROPE_LANE0:ROPE_LANE0 + QK_ROPE, :]

    cos = rope_ref[0]
    sin_lo = rope_ref[1]
    sin_hi = rope_ref[2]

    def rot(v):
        return v * cos + pltpu.roll(v, 8, 1) * sin_lo + pltpu.roll(v, LANE - 8, 1) * sin_hi

    krs_ref[...] = rot(krs)
    for hh in range(N_HEADS):
        q_ref[:, LANE * hh:LANE * (hh + 1)] = rot(q[:, LANE * hh:LANE * (hh + 1)]).astype(BF16)

    pos = lax.broadcasted_iota(jnp.int32, (TM_IN, 1), 0) & (seq - 1)
    first = pos == 0
    last = pos == seq - 1
    for j in range(D_CONV // CONV_CHUNK):
        c0 = j * CONV_CHUNK
        bg = _dot_nt(hb, wc_ref[c0:c0 + CONV_CHUNK, :].astype(BF16))
        cg = _dot_nt(hb, wc_ref[D_CONV + c0:D_CONV + c0 + CONV_CHUNK, :].astype(BF16))
        ui = _dot_nt(hb, wc_ref[2 * D_CONV + c0:2 * D_CONV + c0 + CONV_CHUNK, :].astype(BF16))
        u = cg * ui
        u_prev = jnp.where(first, 0.0, pltpu.roll(u, 1, 0))
        u_next = jnp.where(last, 0.0, pltpu.roll(u, TM_IN - 1, 0))
        cw = [cw_ref[:, k * D_CONV + c0:k * D_CONV + c0 + CONV_CHUNK] for k in range(3)]
        conv = u_prev * cw[0] + u * cw[1] + u_next * cw[2]
        zc_ref[:, c0:c0 + CONV_CHUNK] = (bg * conv).astype(BF16)


def _inproj(xp, xs, mod, norm1, w_in_t, conv_w, q_norm, kv_norm, w_uq_slot, rope_tabs):
    n_prompt = T_P // TM_IN
    const = lambda i: (0, 0)
    pmap, smap = _stream_maps(TM_IN)
    once = pl.Buffered(1)
    return pl.pallas_call(
        _inproj_kernel,
        grid=(T // TM_IN,),
        in_specs=[
            pl.BlockSpec((TM_IN, D_MODEL), pmap),
            pl.BlockSpec((TM_IN, D_MODEL), smap),
            pl.BlockSpec((N_COND, 6 * D_MODEL), const),
            pl.BlockSpec((1, D_MODEL), const),
            pl.BlockSpec((O_CQ, D_MODEL), const, pipeline_mode=once),
            pl.BlockSpec((SMALL_BLK, D_MODEL), lambda i: (O_CQ // SMALL_BLK, 0), pipeline_mode=once),
            pl.BlockSpec((1, 3 * D_CONV), const),
            pl.BlockSpec((1, Q_LORA), const),
            pl.BlockSpec((1, KV_LORA), const),
            pl.BlockSpec((N_HEADS * LANE, Q_LORA), const),
            pl.BlockSpec((None, 3, TM_IN, LANE),
                         lambda i: (jnp.where(i >= n_prompt, 1, 0), 0, 0, 0)),
        ],
        out_specs=[
            pl.BlockSpec((TM_IN, D_CONV), lambda i: (i, 0)),
            pl.BlockSpec((TM_IN, N_HEADS * LANE), lambda i: (i, 0)),
            pl.BlockSpec((TM_IN, KV_LORA), lambda i: (i, 0)),
            pl.BlockSpec((TM_IN, LANE), lambda i: (i, 0)),
            pl.BlockSpec((TM_IN, KV_LORA), pmap),
            pl.BlockSpec((TM_IN // SEQ, QK_ROPE, SEQ), lambda i: (jnp.minimum(i, n_prompt - 1), 0, 0)),
        ],
        out_shape=[
            jax.ShapeDtypeStruct((T, D_CONV), BF16),
            jax.ShapeDtypeStruct((T, N_HEADS * LANE), BF16),
            jax.ShapeDtypeStruct((T, KV_LORA), F32),
            jax.ShapeDtypeStruct((T, LANE), F32),
            jax.ShapeDtypeStruct((T_P, KV_LORA), F32),
            jax.ShapeDtypeStruct((BATCH, QK_ROPE, SEQ), F32),
        ],
        compiler_params=pltpu.CompilerParams(
            dimension_semantics=("arbitrary",), vmem_limit_bytes=VMEM_LIMIT),
        name="in_proj",
    )(xp, xs, mod, norm1, w_in_t, w_in_t, conv_w, q_norm, kv_norm, w_uq_slot, rope_tabs)


def _fill_kv(ckv, krs, wukv_ref, kf_scr, v_scr, off):
    m = ckv.shape[0]
    kv = _dot(ckv.astype(BF16), wukv_ref[...])
    for hh in range(N_HEADS):
        kf_scr[hh, off:off + m, :] = (kv[:, LANE * hh:LANE * (hh + 1)] + krs).astype(BF16)
    v_scr[off:off + m, :] = kv[:, N_HEADS * LANE:].astype(BF16)


def _attend(q_ref, r0, rows, kf_scr, v_scr, o_ref):
    for pair in range(N_HEADS // 2):
        acc = None
        for hh in (2 * pair, 2 * pair + 1):
            qh = q_ref[r0:r0 + rows, LANE * hh:LANE * (hh + 1)]
            s = _dot_nt(qh, kf_scr[hh]) * ATTN_SCALE
            e = jnp.exp(s - jnp.max(s, axis=-1, keepdims=True))
            p = (e / jnp.sum(e, axis=-1, keepdims=True)).astype(BF16)
            part = _dot(p, v_scr[:, LANE * hh:LANE * (hh + 1)])
            acc = part if acc is None else acc + part
        o_ref[r0:r0 + rows, LANE * pair:LANE * (pair + 1)] = acc.astype(BF16)


def _attn_prompt_kernel(q_ref, ckv_ref, krs_ref, wukv_ref, o_ref):
    for s in range(PROMPT_SEQS):
        r0 = s * SEQ
        kv = _dot(ckv_ref[r0:r0 + SEQ, :].astype(BF16), wukv_ref[...])
        krs = krs_ref[r0:r0 + SEQ, :]
        for pair in range(N_HEADS // 2):
            acc = None
            for hh in (2 * pair, 2 * pair + 1):
                kf = (kv[:, LANE * hh:LANE * (hh + 1)] + krs).astype(BF16)
                vh = kv[:, LANE * (N_HEADS + hh):LANE * (N_HEADS + hh + 1)].astype(BF16)
                sc = _dot_nt(q_ref[r0:r0 + SEQ, LANE * hh:LANE * (hh + 1)], kf) * ATTN_SCALE
                e = jnp.exp(sc - jnp.max(sc, axis=-1, keepdims=True))
                p = (e / jnp.sum(e, axis=-1, keepdims=True)).astype(BF16)
                part = _dot(p, vh)
                acc = part if acc is None else acc + part
            o_ref[r0:r0 + SEQ, LANE * pair:LANE * (pair + 1)] = acc.astype(BF16)


def _attn_sample_kernel(q_ref, ckv_ref, cckv_ref, krs_ref, ckrs_ref, wukv_ref, o_ref, kf_scr, v_scr):
    @pl.when(pl.program_id(1) == 0)
    def _():
        _fill_kv(ckv_ref[...], krs_ref[...], wukv_ref, kf_scr, v_scr, 0)
        _fill_kv(cckv_ref[...], ckrs_ref[...], wukv_ref, kf_scr, v_scr, DEC_SEQ)

    _attend(q_ref, 0, Q_BLK_S, kf_scr, v_scr, o_ref)


def _attention(q, ckv, krs, cache_ckv, cache_krs, w_ukv_slot):
    kv_cols = 2 * N_HEADS * LANE
    n_o = N_HEADS * V_HEAD
    steps = BATCH // PROMPT_SEQS
    rows = PROMPT_SEQS * SEQ
    o_prompt = pl.pallas_call(
        _attn_prompt_kernel,
        grid=(steps,),
        in_specs=[
            pl.BlockSpec((rows, N_HEADS * LANE), lambda b: (b, 0)),
            pl.BlockSpec((rows, KV_LORA), lambda b: (b, 0)),
            pl.BlockSpec((rows, LANE), lambda b: (b, 0)),
            pl.BlockSpec((KV_LORA, kv_cols), lambda b: (0, 0)),
        ],
        out_specs=pl.BlockSpec((rows, n_o), lambda b: (b, 0)),
        out_shape=jax.ShapeDtypeStruct((T_P, n_o), BF16),
        compiler_params=pltpu.CompilerParams(dimension_semantics=("parallel",),
                                             vmem_limit_bytes=VMEM_LIMIT),
        name="attn_prompt",
    )(q, ckv, krs, w_ukv_slot)

    m_all = DEC_SEQ + PAST_LEN
    nq = DEC_SEQ // Q_BLK_S
    q0 = T_P // Q_BLK_S
    s0 = T_P // DEC_SEQ
    o_sample = pl.pallas_call(
        _attn_sample_kernel,
        grid=(DEC_BATCH, nq),
        in_specs=[
            pl.BlockSpec((Q_BLK_S, N_HEADS * LANE), lambda b, j: (q0 + b * nq + j, 0)),
            pl.BlockSpec((DEC_SEQ, KV_LORA), lambda b, j: (s0 + b, 0)),
            pl.BlockSpec((None, PAST_LEN, KV_LORA), lambda b, j: (b, 0, 0)),
            pl.BlockSpec((DEC_SEQ, LANE), lambda b, j: (s0 + b, 0)),
            pl.BlockSpec((None, PAST_LEN, LANE), lambda b, j: (b, 0, 0)),
            pl.BlockSpec((KV_LORA, kv_cols), lambda b, j: (0, 0)),
        ],
        out_specs=pl.BlockSpec((Q_BLK_S, n_o), lambda b, j: (b * nq + j, 0)),
        out_shape=jax.ShapeDtypeStruct((T_S, n_o), BF16),
        scratch_shapes=[pltpu.VMEM((N_HEADS, m_all, LANE), BF16),
                        pltpu.VMEM((m_all, N_HEADS * LANE), BF16)],
        compiler_params=pltpu.CompilerParams(dimension_semantics=("parallel", "arbitrary"),
                                             vmem_limit_bytes=VMEM_LIMIT),
        name="attn_sample",
    )(q, ckv, cache_ckv, krs, cache_krs, w_ukv_slot)
    return o_prompt, o_sample


def _route(logits):
    lane = lax.broadcasted_iota(jnp.int32, logits.shape, 1)
    neg = -jnp.inf
    big = jnp.int32(1 << 20)
    gmask = (lane >= N_EXPERTS) & (lane < N_EXPERTS + N_GROUPS)
    gl = jnp.where(gmask, logits, neg)
    gmax = jnp.max(gl, axis=-1, keepdims=True)
    gsum = jnp.sum(jnp.where(gmask, jnp.exp(gl - gmax), 0.0), axis=-1, keepdims=True)
    p_g = 1.0 / gsum
    g_idx = jnp.min(jnp.where(gl == gmax, lane, big), axis=-1, keepdims=True) - N_EXPERTS

    emask = (lane < N_EXPERTS) & ((lane >> 3) == g_idx)
    el = jnp.where(emask, logits, neg)
    m1 = jnp.max(el, axis=-1, keepdims=True)
    i1 = jnp.min(jnp.where(el == m1, lane, big), axis=-1, keepdims=True)
    el2 = jnp.where(lane == i1, neg, el)
    m2 = jnp.max(el2, axis=-1, keepdims=True)
    i2 = jnp.min(jnp.where(el2 == m2, lane, big), axis=-1, keepdims=True)
    z = jnp.sum(jnp.where(emask, jnp.exp(el - m1), 0.0), axis=-1, keepdims=True)
    p1 = 1.0 / z
    p2 = jnp.exp(m2 - m1) / z
    tot = p1 + p2
    w1 = p_g * p1 / tot
    w2 = p_g * p2 / tot
    return jnp.where(lane == i1, w1, 0.0) + jnp.where(lane == i2, w2, 0.0), g_idx


def _post_kernel(xp_ref, xs_ref, mod_ref, n1_ref, wg_ref, zc_ref, op_ref, os_ref, wco_ref, wo_ref,
                 wmix_ref, n2_ref, wr_ref, x1_ref, h3_ref, meta_ref, cnt_ref, gr_ref):
    i = pl.program_id(0)
    is_sample = i >= T_P // TM_POST
    x = jnp.where(is_sample, xs_ref[...], xp_ref[...])
    o = jnp.where(is_sample, os_ref[...], op_ref[...])
    mod = mod_ref[pl.ds(_mod_row(i, TM_POST), 1), :]
    shift1 = mod[:, 0:D_MODEL]
    scale1 = mod[:, D_MODEL:2 * D_MODEL]
    gate1 = mod[:, 2 * D_MODEL:3 * D_MODEL]
    shift2 = mod[:, 3 * D_MODEL:4 * D_MODEL]
    scale2 = mod[:, 4 * D_MODEL:5 * D_MODEL]
    y_conv = _dot(zc_ref[...], wco_ref[...].astype(BF16))
    y_mla = _dot(o, wo_ref[...].astype(BF16))
    h = ((x * _rms(x)) * n1_ref[...]) * (1.0 + scale1) + shift1
    g = _dot_nt(h.astype(BF16), wg_ref[...].astype(BF16))
    merged = (jax.nn.sigmoid(g[:, 0:D_MODEL]) * y_conv
              + jax.nn.sigmoid(g[:, D_MODEL:2 * D_MODEL]) * y_mla)
    y = _dot(merged.astype(BF16), wmix_ref[...].astype(BF16))
    x1 = x + gate1 * y
    x1_ref[...] = x1
    h2 = ((x1 * _rms(x1)) * n2_ref[...]) * (1.0 + scale2) + shift2
    h2_hi = h2.astype(BF16)
    h2_lo = (h2 - h2_hi.astype(F32)).astype(BF16)
    hh = _dot(h2_hi, wr_ref[...])
    logits = hh[:, 0:LANE] + hh[:, LANE:2 * LANE] + _dot(h2_lo, wr_ref[:, 0:LANE])
    comb, g_idx = _route(logits)

    lane = lax.broadcasted_iota(jnp.int32, comb.shape, 1)
    onehot = lane == g_idx + N_EXPERTS
    r_i = lax.broadcasted_iota(jnp.int32, (TM_POST, TM_POST), 0)
    c_i = lax.broadcasted_iota(jnp.int32, (TM_POST, TM_POST), 1)
    lower = jnp.where(c_i < r_i, 1.0, 0.0).astype(BF16)
    before = _dot(lower, jnp.where(onehot, 1.0, 0.0).astype(BF16))
    rank = jnp.sum(jnp.where(onehot, before, 0.0), axis=-1, keepdims=True)
    counts = jnp.sum(jnp.where(onehot, 1.0, 0.0), axis=0, keepdims=True)
    cnt_ref[...] = jnp.broadcast_to(counts, cnt_ref.shape)

    meta_ref[...] = comb
    idx = jnp.where(lane == GID_LANE, g_idx.astype(F32), 0.0) + jnp.where(lane == RANK_LANE, rank, 0.0)
    idx_hi = idx.astype(BF16)
    idx_lo = (idx - idx_hi.astype(F32)).astype(BF16)
    s_row = lax.broadcasted_iota(jnp.int32, (8, LANE), 0)
    s_lane = lax.broadcasted_iota(jnp.int32, (8, LANE), 1)
    sel = jnp.where(s_lane == GID_LANE + s_row, 1.0, 0.0).astype(BF16)
    gr_ref[...] = _dot_nt(sel, idx_hi) + _dot_nt(sel, idx_lo)
    for c in range(N_SLAB):
        h3_ref[pl.ds(c, TM_POST, stride=N_SLAB), :] = h2[:, LANE * c:LANE * (c + 1)]


def _post(xp, xs, mod, norm1, w_in_t, zc, o_p, o_s, w_conv_out, w_o, w_mix_out, norm2, w_route):
    const = lambda i: (0, 0)
    row = lambda i: (i, 0)
    pmap, smap = _stream_maps(TM_POST)
    once = pl.Buffered(1)
    return pl.pallas_call(
        _post_kernel,
        grid=(T // TM_POST,),
        in_specs=[
            pl.BlockSpec((TM_POST, D_MODEL), pmap),
            pl.BlockSpec((TM_POST, D_MODEL), smap),
            pl.BlockSpec((N_COND, 6 * D_MODEL), const),
            pl.BlockSpec((1, D_MODEL), const),
            pl.BlockSpec((pl.Element(2 * D_MODEL), pl.Element(D_MODEL)), lambda i: (O_GATE, 0),
                         pipeline_mode=once),
            pl.BlockSpec((TM_POST, D_CONV), row),
            pl.BlockSpec((TM_POST, N_HEADS * V_HEAD), pmap),
            pl.BlockSpec((TM_POST, N_HEADS * V_HEAD), smap),
            pl.BlockSpec((D_CONV, D_MODEL), const, pipeline_mode=once),
            pl.BlockSpec((N_HEADS * V_HEAD, D_MODEL), const, pipeline_mode=once),
            pl.BlockSpec((D_MODEL, D_MODEL), const, pipeline_mode=once),
            pl.BlockSpec((1, D_MODEL), const),
            pl.BlockSpec((D_MODEL, 2 * LANE), const),
        ],
        out_specs=[
            pl.BlockSpec((TM_POST, D_MODEL), row),
            pl.BlockSpec((TM_POST * N_SLAB, LANE), row),
            pl.BlockSpec((TM_POST, LANE), row),
            pl.BlockSpec((None, 8, LANE), lambda i: (i, 0, 0)),
            pl.BlockSpec((8, TM_POST), lambda i: (0, i)),
        ],
        out_shape=[
            jax.ShapeDtypeStruct((T, D_MODEL), F32),
            jax.ShapeDtypeStruct((T * N_SLAB, LANE), F32),
            jax.ShapeDtypeStruct((T, LANE), F32),
            jax.ShapeDtypeStruct((T // TM_POST, 8, LANE), F32),
            jax.ShapeDtypeStruct((8, T), F32),
        ],
        compiler_params=pltpu.CompilerParams(
            dimension_semantics=("parallel",), vmem_limit_bytes=VMEM_LIMIT),
        name="post_mixer",
    )(xp, xs, mod, norm1, w_in_t, zc, o_p, o_s, w_conv_out, w_o, w_mix_out, norm2, w_route)


DISP_IN = T // TM_DISP
DISP_OUT = T // TM_MOE


def _dispatch_kernel(pos_ref, h3_ref, m_ref, hs_ref, ms_ref, xs_ref, mss_ref):
    i = pl.program_id(0)

    @pl.when(i < DISP_IN)
    def _():
        def body(r, carry):
            p = pos_ref[0, r]
            xs_ref[_slab(p), :] = h3_ref[_slab(r), :]
            mss_ref[pl.ds(p, 1), :] = m_ref[pl.ds(r, 1), :]
            return carry

        lax.fori_loop(0, TM_DISP, body, 0, unroll=8)

    @pl.when(i >= DISP_IN)
    def _():
        row0 = pl.multiple_of((i - DISP_IN) * TM_MOE, TM_MOE)
        for c in range(N_SLAB):
            hs_ref[:, LANE * c:LANE * (c + 1)] = (
                xs_ref[pl.ds(row0 * N_SLAB + c, TM_MOE, stride=N_SLAB), :].astype(BF16))
        ms_ref[...] = mss_ref[pl.ds(row0, TM_MOE), :]


def _dispatch(pos, h3, meta):
    n_slab = N_SLAB
    in_map = lambda i: (jnp.minimum(i, DISP_IN - 1), 0)
    out_map = lambda i: (jnp.maximum(i - DISP_IN, 0), 0)
    return pl.pallas_call(
        _dispatch_kernel,
        grid=(DISP_IN + DISP_OUT,),
        in_specs=[pl.BlockSpec((None, 1, TM_DISP), lambda i: (jnp.minimum(i, DISP_IN - 1), 0, 0),
                               memory_space=pltpu.SMEM),
                  pl.BlockSpec((TM_DISP * n_slab, LANE), in_map),
                  pl.BlockSpec((TM_DISP, LANE), in_map)],
        out_specs=[pl.BlockSpec((TM_MOE, D_MODEL), out_map),
                   pl.BlockSpec((TM_MOE, LANE), out_map)],
        scratch_shapes=[pltpu.VMEM((T * n_slab, LANE), F32),
                        pltpu.VMEM((T, LANE), F32)],
        out_shape=[jax.ShapeDtypeStruct((T, D_MODEL), BF16),
                   jax.ShapeDtypeStruct((T, LANE), F32)],
        compiler_params=pltpu.CompilerParams(
            dimension_semantics=("arbitrary",), vmem_limit_bytes=VMEM_LIMIT),
        name="moe_dispatch",
    )(pos.reshape(DISP_IN, 1, TM_DISP), h3, meta)


def _moe_kernel(sched_ref, hs_ref, ms_ref, wup_ref, wgate_ref, wdown_ref, y3_ref, acc_ref):
    v = pl.program_id(0)
    j = pl.program_id(1)
    valid = sched_ref[V_VALID, v] == 1
    lo = sched_ref[V_LO, v]
    hi = sched_ref[V_HI, v]
    e0 = sched_ref[V_GROUP, v] * EXP_PER_GROUP + j * MOE_EPS
    full = (hi - lo) * 4 >= TM_MOE * 3

    @pl.when(valid & (j == 0) & (sched_ref[V_FIRST, v] == 1))
    def _():
        acc_ref[...] = jnp.zeros_like(acc_ref)

    def expert_rows(r0, rows):
        w_in2 = jnp.concatenate(
            [w[k].astype(BF16) for k in range(MOE_EPS) for w in (wup_ref, wgate_ref)], axis=1)
        ag = _dot(hs_ref[pl.ds(r0, rows), :], w_in2)
        comb = ms_ref[pl.ds(r0, rows), :]
        lane = lax.broadcasted_iota(jnp.int32, comb.shape, 1)
        acts = []
        for k in range(MOE_EPS):
            a = ag[:, 2 * k * D_EXPERT:(2 * k + 1) * D_EXPERT]
            g = ag[:, (2 * k + 1) * D_EXPERT:(2 * k + 2) * D_EXPERT]
            cw = jnp.sum(jnp.where(lane == e0 + k, comb, 0.0), axis=-1, keepdims=True)
            acts.append(((g * jax.nn.sigmoid(g)) * a * cw).astype(BF16))
        w_out = jnp.concatenate([wdown_ref[k].astype(BF16) for k in range(MOE_EPS)], axis=0)
        acc_ref[pl.ds(r0, rows), :] += _dot(jnp.concatenate(acts, axis=1), w_out)

    @pl.when(valid & full)
    def _():
        expert_rows(0, TM_MOE)

    @pl.when(valid & jnp.logical_not(full))
    def _():
        def sub_block(s, carry):
            expert_rows(pl.multiple_of(s * MOE_SUB, MOE_SUB), MOE_SUB)
            return carry

        lax.fori_loop(lo // MOE_SUB, (hi + MOE_SUB - 1) // MOE_SUB, sub_block, 0)

    @pl.when(valid & (j == EXP_PER_GROUP // MOE_EPS - 1) & (sched_ref[V_LAST, v] == 1))
    def _():
        for c in range(N_SLAB):
            y3_ref[pl.ds(c, TM_MOE, stride=N_SLAB), :] = acc_ref[:, LANE * c:LANE * (c + 1)]


def _moe(sched, hs, ms, w_up, w_gate, w_down):
    steps = EXP_PER_GROUP // MOE_EPS
    wmap = lambda v, j, sched: (
        sched[V_GROUP, v] * steps + jnp.where(sched[V_VALID, v] == 1, j, steps - 1), 0, 0)
    tmap = lambda v, j, sched: (sched[V_TILE, v], 0)
    n_slab = D_MODEL // LANE
    return pl.pallas_call(
        _moe_kernel,
        grid_spec=pltpu.PrefetchScalarGridSpec(
            num_scalar_prefetch=1,
            grid=(N_VISITS, steps),
            in_specs=[
                pl.BlockSpec((TM_MOE, D_MODEL), tmap),
                pl.BlockSpec((TM_MOE, LANE), tmap),
                pl.BlockSpec((MOE_EPS, D_MODEL, D_EXPERT), wmap),
                pl.BlockSpec((MOE_EPS, D_MODEL, D_EXPERT), wmap),
                pl.BlockSpec((MOE_EPS, D_EXPERT, D_MODEL), wmap),
            ],
            out_specs=pl.BlockSpec((TM_MOE * n_slab, LANE), tmap),
            scratch_shapes=[pltpu.VMEM((TM_MOE, D_MODEL), F32)],
        ),
        out_shape=jax.ShapeDtypeStruct((T * n_slab, LANE), F32),
        compiler_params=pltpu.CompilerParams(
            dimension_semantics=("arbitrary", "arbitrary"), vmem_limit_bytes=VMEM_LIMIT),
        name="moe_grouped",
    )(sched, hs, ms, w_up, w_gate, w_down)


def _final_kernel(pos_ref, ys_ref, x1_ref, mod_ref, fn_ref, yp_ref, ysm_ref, g_ref):
    i = pl.program_id(0)

    def body(r, carry):
        g_ref[_slab(r), :] = ys_ref[_slab(pos_ref[0, r]), :]
        return carry

    lax.fori_loop(0, TM_FINAL, body, 0, unroll=8)
    mod = mod_ref[pl.ds(_mod_row(i, TM_FINAL), 1), :]
    gate2 = mod[:, 5 * D_MODEL:6 * D_MODEL]
    moe = jnp.concatenate([g_ref[pl.ds(c, TM_FINAL, stride=N_SLAB), :] for c in range(N_SLAB)], axis=1)
    x2 = x1_ref[...] + gate2 * moe
    y = (x2 * _rms(x2)) * fn_ref[...]
    is_sample = i >= T_P // TM_FINAL

    @pl.when(jnp.logical_not(is_sample))
    def _():
        yp_ref[...] = y

    @pl.when(is_sample)
    def _():
        ysm_ref[...] = y


def _final(pos, ys, x1, mod, final_norm):
    n_slab = D_MODEL // LANE
    pmap, smap = _stream_maps(TM_FINAL)
    return pl.pallas_call(
        _final_kernel,
        grid=(T // TM_FINAL,),
        in_specs=[
            pl.BlockSpec((None, 1, TM_FINAL), lambda i: (i, 0, 0), memory_space=pltpu.SMEM),
            pl.BlockSpec((T * n_slab, LANE), lambda i: (0, 0), pipeline_mode=pl.Buffered(1)),
            pl.BlockSpec((TM_FINAL, D_MODEL), lambda i: (i, 0)),
            pl.BlockSpec((N_COND, 6 * D_MODEL), lambda i: (0, 0)),
            pl.BlockSpec((1, D_MODEL), lambda i: (0, 0)),
        ],
        out_specs=[pl.BlockSpec((TM_FINAL, D_MODEL), pmap),
                   pl.BlockSpec((TM_FINAL, D_MODEL), smap)],
        scratch_shapes=[pltpu.VMEM((TM_FINAL * n_slab, LANE), F32)],
        out_shape=[jax.ShapeDtypeStruct((T_P, D_MODEL), F32),
                   jax.ShapeDtypeStruct((T_S, D_MODEL), F32)],
        compiler_params=pltpu.CompilerParams(
            dimension_semantics=("arbitrary",), vmem_limit_bytes=VMEM_LIMIT),
        name="moe_unsort_final",
    )(pos.reshape(T // TM_FINAL, 1, TM_FINAL), ys, x1, mod, final_norm)


def _plan_kernel(gr_ref, cnt_ref, pos_ref, sched_ref):
    n_post = T // TM_POST
    lane = lax.broadcasted_iota(jnp.int32, (1, LANE), 1)
    grp_lane = lambda v, g: v[:, N_EXPERTS + g:N_EXPERTS + g + 1]
    counts = [cnt_ref[k, 0:1, :] for k in range(n_post)]
    gtot_v = counts[0]
    for k in range(1, n_post):
        gtot_v = gtot_v + counts[k]
    gtot = [grp_lane(gtot_v, g) for g in range(N_GROUPS)]
    goff = [jnp.zeros((1, 1), F32)]
    for g in range(1, N_GROUPS):
        goff.append(goff[-1] + gtot[g - 1])

    before = [jnp.zeros((1, 1), F32) for _ in range(N_GROUPS)]
    rows_per_tile = TM_POST // LANE
    for k in range(n_post):
        gid = gr_ref[0:1, TM_POST * k:TM_POST * (k + 1)]
        pos = gr_ref[1:2, TM_POST * k:TM_POST * (k + 1)]
        for g in range(N_GROUPS):
            pos = pos + jnp.where(gid == float(g), goff[g] + before[g], 0.0)
            before[g] = before[g] + grp_lane(counts[k], g)
        for r in range(rows_per_tile):
            pos_ref[rows_per_tile * k + r:rows_per_tile * k + r + 1, :] = (
                pos[:, LANE * r:LANE * (r + 1)].astype(jnp.int32))

    zero = jnp.zeros((1, LANE), F32)
    rows = {name: zero for name in ("vt", "vg", "vlo", "vhi")}
    slot = jnp.zeros((1, 1), F32)
    last_t = jnp.zeros((1, 1), F32)
    last_g = jnp.zeros((1, 1), F32)
    for i in range(T // TM_MOE):
        for g in range(N_GROUPS):
            lo = jnp.clip(goff[g] - float(TM_MOE * i), 0.0, float(TM_MOE))
            hi = jnp.clip(goff[g] + gtot[g] - float(TM_MOE * i), 0.0, float(TM_MOE))
            ok = hi > lo
            here = ok & (lane == slot.astype(jnp.int32))
            rows["vt"] = jnp.where(here, float(i), rows["vt"])
            rows["vg"] = jnp.where(here, float(g), rows["vg"])
            rows["vlo"] = jnp.where(here, lo, rows["vlo"])
            rows["vhi"] = jnp.where(here, hi, rows["vhi"])
            last_t = jnp.where(ok, float(i), last_t)
            last_g = jnp.where(ok, float(g), last_g)
            slot = slot + jnp.where(ok, 1.0, 0.0)
    n_ok = slot.astype(jnp.int32)
    valid = lane < n_ok
    vt = jnp.where(valid, rows["vt"], last_t)
    vg = jnp.where(valid, rows["vg"], last_g)
    prev_t = pltpu.roll(jnp.broadcast_to(vt, (8, LANE)), 1, 1)[0:1, :]
    next_t = pltpu.roll(jnp.broadcast_to(vt, (8, LANE)), LANE - 1, 1)[0:1, :]
    first = jnp.where((lane == 0) | (vt != prev_t), 1.0, 0.0)
    last = jnp.where((lane == n_ok - 1) | (vt != next_t), 1.0, 0.0)
    table = [vt, vg, rows["vlo"], rows["vhi"], first, last, jnp.where(valid, 1.0, 0.0), zero]
    for r, row in enumerate(table):
        sched_ref[r:r + 1, :] = row.astype(jnp.int32)


V_TILE, V_GROUP, V_LO, V_HI, V_FIRST, V_LAST, V_VALID = range(7)


def _plan(gr, cnt):
    pos, sched = pl.pallas_call(
        _plan_kernel,
        out_shape=[jax.ShapeDtypeStruct((T // LANE, LANE), jnp.int32),
                   jax.ShapeDtypeStruct((8, LANE), jnp.int32)],
        name="moe_plan",
    )(gr, cnt)
    return pos.reshape(T), sched


def _rope_tables():
    n = np.arange(DEC_SEQ)
    pos = np.stack([n // GRID_W, n % GRID_W], axis=1).astype(np.float32)
    half = ROPE_AXIS // 2
    inv = (1.0 / (ROPE_BASE ** (np.arange(0, ROPE_AXIS, 2, dtype=np.float32) / ROPE_AXIS))).astype(np.float32)
    ang = (pos[:, :, None] * inv[None, None, :]).astype(np.float32)
    cos = np.cos(ang).astype(np.float32)
    sin = np.sin(ang).astype(np.float32)
    tabs = np.zeros((2, 3, DEC_SEQ, LANE), np.float32)
    tabs[:, 0] = 1.0
    for a in range(2):
        lo = ROPE_LANE0 + a * ROPE_AXIS
        tabs[1, 0, :, lo:lo + half] = cos[:, a]
        tabs[1, 0, :, lo + half:lo + 2 * half] = cos[:, a]
        tabs[1, 1, :, lo + half:lo + 2 * half] = sin[:, a]
        tabs[1, 2, :, lo:lo + half] = -sin[:, a]
    return jnp.asarray(tabs)


def kernel(x_prompt, x_sample, cache_ckv, cache_krope, c, c_ctx, norm1, w_ada, b_ada, w_in, conv_w,
           w_conv_out, q_norm, w_uq, kv_norm, w_ukv, w_o, w_mix_out, norm2, w_grp, w_exp, w_up,
           w_gate, w_down, final_norm):
    l = 0
    xp = x_prompt.reshape(T_P, D_MODEL)
    xs = x_sample.reshape(T_S, D_MODEL)
    mod = _ada(c_ctx[None, :], c, w_ada[l], b_ada[l][None, :])

    w_in_t = w_in[l].T
    w_uq_slot = jnp.pad(w_uq[l].T.reshape(N_HEADS, QK_NOPE + QK_ROPE, Q_LORA),
                        ((0, 0), (0, LANE - QK_NOPE - QK_ROPE), (0, 0))
                        ).reshape(N_HEADS * LANE, Q_LORA).astype(BF16)
    wkv = w_ukv[l].reshape(KV_LORA, N_HEADS, QK_NOPE + V_HEAD)
    wk_slot = jnp.pad(wkv[:, :, :QK_NOPE], ((0, 0), (0, 0), (0, LANE - QK_NOPE)))
    wv = wkv[:, :, QK_NOPE:].reshape(KV_LORA, N_HEADS // 2, 2, V_HEAD)
    zero = jnp.zeros_like(wv[:, :, 0])
    wv_slot = jnp.stack([jnp.concatenate([wv[:, :, 0], zero], axis=-1),
                         jnp.concatenate([zero, wv[:, :, 1]], axis=-1)], axis=2)
    w_ukv_slot = jnp.concatenate([wk_slot.reshape(KV_LORA, N_HEADS * LANE),
                                  wv_slot.reshape(KV_LORA, N_HEADS * LANE)], axis=1).astype(BF16)
    w_route = jnp.pad(jnp.concatenate([w_exp[l], w_grp[l]], axis=1),
                      ((0, 0), (0, LANE - N_EXPERTS - N_GROUPS)))
    w_route_hi = w_route.astype(BF16)
    w_route_lo = (w_route - w_route_hi.astype(F32)).astype(BF16)
    w_route2 = jnp.concatenate([w_route_hi, w_route_lo], axis=1)
    cache_krs = jnp.pad(cache_krope[:, l], ((0, 0), (0, 0), (ROPE_LANE0, LANE - ROPE_LANE0 - QK_ROPE)))

    zc, q, ckv, krs, nckv, nkr = _inproj(xp, xs, mod, norm1[l][None, :], w_in_t,
                                         conv_w[l].reshape(1, 3 * D_CONV),
                              q_norm[l][None, :], kv_norm[l][None, :], w_uq_slot, _rope_tables())
    o_p, o_s = _attention(q, ckv, krs, cache_ckv[:, l], cache_krs, w_ukv_slot)
    x1, h3, meta, cnt, gr = _post(xp, xs, mod, norm1[l][None, :], w_in_t, zc, o_p, o_s, w_conv_out[l],
                                  w_o[l], w_mix_out[l], norm2[l][None, :], w_route2)
    pos, sched = _plan(gr, cnt)
    hs, ms = _dispatch(pos, h3, meta)
    ys = _moe(sched, hs, ms, w_up[l], w_gate[l], w_down[l])
    yp, ysm = _final(pos, ys, x1, mod, final_norm[None, :])

    y_prompt = yp.reshape(BATCH, SEQ, D_MODEL)
    y_sample = ysm.reshape(DEC_BATCH, DEC_SEQ, D_MODEL)
    new_ckv = nckv.reshape(BATCH, 1, SEQ, KV_LORA)
    new_krope = jnp.swapaxes(nkr, 1, 2).reshape(BATCH, 1, SEQ, QK_ROPE)
    return (y_prompt, y_sample, new_ckv, new_krope)
```

```python
import numpy as np
import jax
import jax.numpy as jnp
from jax import lax
from jax.experimental import pallas as pl
from jax.experimental.pallas import tpu as pltpu

F32 = jnp.float32
BF16 = jnp.bfloat16

D_MODEL = 1024
BATCH = 16
SEQ = 256
DEC_BATCH = 2
DEC_SEQ = 1024
PAST_LEN = 256
GRID_W = 64
N_HEADS = 8
QK_NOPE = 64
QK_ROPE = 32
V_HEAD = 64
Q_LORA = 256
KV_LORA = 128
ROPE_AXIS = QK_ROPE // 2
ROPE_BASE = 10000.0
ATTN_SCALE = (QK_NOPE + QK_ROPE) ** -0.5
D_CONV = D_MODEL
N_GROUPS = 4
EXP_PER_GROUP = 8
N_EXPERTS = N_GROUPS * EXP_PER_GROUP
D_EXPERT = 256
EPS = 1e-6

T_P = BATCH * SEQ
T_S = DEC_BATCH * DEC_SEQ
T = T_P + T_S
N_COND = 8
LANE = 128
ROPE_LANE0 = QK_NOPE
SMALL_COLS = Q_LORA + KV_LORA + LANE
VMEM_LIMIT = 56 * 1024 * 1024

TM_IN = 1024
TM_POST = 512
TM_MOE = T // N_GROUPS
TM_FINAL = 512
TM_DISP = 1024
N_SLAB = D_MODEL // LANE
MOE_SUB = 256
MOE_EPS = 2
N_VISITS = T // TM_MOE + N_GROUPS - 1
GID_LANE = 40
RANK_LANE = 41
Q_BLK_S = 512
PROMPT_SEQS = 2
CONV_CHUNK = 256


def _dot(a, b):
    return jnp.dot(a, b, preferred_element_type=F32)


def _rms(x):
    return lax.rsqrt(jnp.mean(x * x, axis=-1, keepdims=True) + EPS)


def _slab(t):
    return pl.ds(pl.multiple_of(t * N_SLAB, N_SLAB), N_SLAB)


def _mod_row(i, tm):
    n_prompt = T_P // tm
    return jnp.where(i >= n_prompt, 1 + ((i - n_prompt) * tm) // DEC_SEQ, 0)


def _ada_kernel(cctx_ref, c_ref, w_ref, b_ref, o_ref):
    c = jnp.concatenate([cctx_ref[...], c_ref[...],
                         jnp.zeros((N_COND - 1 - DEC_BATCH, D_MODEL), F32)], axis=0)
    a = (c * jax.nn.sigmoid(c)).astype(BF16)
    o_ref[...] = _dot(a, w_ref[...].astype(BF16)) + b_ref[...]


def _ada(c_ctx, c, w_ada, b_ada):
    n = 6 * D_MODEL
    bn = 1536
    return pl.pallas_call(
        _ada_kernel,
        grid=(n // bn,),
        in_specs=[
            pl.BlockSpec((1, D_MODEL), lambda j: (0, 0)),
            pl.BlockSpec((DEC_BATCH, D_MODEL), lambda j: (0, 0)),
            pl.BlockSpec((D_MODEL, bn), lambda j: (0, j)),
            pl.BlockSpec((1, bn), lambda j: (0, j)),
        ],
        out_specs=pl.BlockSpec((N_COND, bn), lambda j: (0, j)),
        out_shape=jax.ShapeDtypeStruct((N_COND, n), F32),
        compiler_params=pltpu.CompilerParams(
            dimension_semantics=("parallel",), vmem_limit_bytes=VMEM_LIMIT),
        name="ada_mod",
    )(c_ctx, c, w_ada, b_ada)


O_CQ = 3 * D_CONV
O_KR = O_CQ + Q_LORA + KV_LORA
O_GATE = O_KR + QK_ROPE
SMALL_BLK = 512
NT = (((1,), (1,)), ((), ()))


def _dot_nt(a, bt):
    return lax.dot_general(a, bt, NT, preferred_element_type=F32)


def _stream_maps(tm):
    n_prompt = T_P // tm
    return (lambda i, *_: (jnp.minimum(i, n_prompt - 1), 0),
            lambda i, *_: (jnp.maximum(i - n_prompt, 0), 0))


def _inproj_kernel(xp_ref, xs_ref, mod_ref, n1_ref, wc_ref, ws_ref, cw_ref, qn_ref, kvn_ref, wuq_ref,
                   rope_ref, zc_ref, q_ref, ckv_ref, krs_ref, nckv_ref, nkr_ref):
    i = pl.program_id(0)
    is_sample = i >= T_P // TM_IN
    seq = jnp.where(is_sample, DEC_SEQ, SEQ)
    mod = mod_ref[pl.ds(_mod_row(i, TM_IN), 1), :]
    shift1 = mod[:, 0:D_MODEL]
    scale1 = mod[:, D_MODEL:2 * D_MODEL]
    x = jnp.where(is_sample, xs_ref[...], xp_ref[...])
    h = ((x * _rms(x)) * n1_ref[...]) * (1.0 + scale1) + shift1
    hb = h.astype(BF16)

    n_lat = Q_LORA + KV_LORA
    w_small = jnp.concatenate([
        ws_ref[0:n_lat, :].astype(BF16),
        jnp.zeros((ROPE_LANE0, D_MODEL), BF16),
        ws_ref[n_lat:n_lat + QK_ROPE, :].astype(BF16),
        jnp.zeros((LANE - ROPE_LANE0 - QK_ROPE, D_MODEL), BF16)], axis=0)
    sm = _dot_nt(hb, w_small)
    cq = sm[:, 0:Q_LORA]
    ckv_raw = sm[:, Q_LORA:Q_LORA + KV_LORA]
    krs = sm[:, Q_LORA + KV_LORA:SMALL_COLS]
    cqn = (cq * _rms(cq)) * qn_ref[...]
    q = _dot_nt(cqn.astype(BF16), wuq_ref[...])
    ckv = (ckv_raw * _rms(ckv_raw)) * kvn_ref[...]
    ckv_ref[...] = ckv

    @pl.when(jnp.logical_not(is_sample))
    def _():
        nckv_ref[...] = ckv
        for s in range(TM_IN // SEQ):
            kt = krs[s * SEQ:(s + 1) * SEQ, :].T
            nkr_ref[s] = kt[ROPE_LANE0:ROPE_LANE0 + QK_ROPE, :]

    cos = rope_ref[0]
    sin_lo = rope_ref[1]
    sin_hi = rope_ref[2]

    def rot(v):
        return v * cos + pltpu.roll(v, 8, 1) * sin_lo + pltpu.roll(v, LANE - 8, 1) * sin_hi

    krs_ref[...] = rot(krs)
    for hh in range(N_HEADS):
        q_ref[:, LANE * hh:LANE * (hh + 1)] = rot(q[:, LANE * hh:LANE * (hh + 1)]).astype(BF16)

    pos = lax.broadcasted_iota(jnp.int32, (TM_IN, 1), 0) & (seq - 1)
    first = pos == 0
    last = pos == seq - 1
    for j in range(D_CONV // CONV_CHUNK):
        c0 = j * CONV_CHUNK
        bg = _dot_nt(hb, wc_ref[c0:c0 + CONV_CHUNK, :].astype(BF16))
        cg = _dot_nt(hb, wc_ref[D_CONV + c0:D_CONV + c0 + CONV_CHUNK, :].astype(BF16))
        ui = _dot_nt(hb, wc_ref[2 * D_CONV + c0:2 * D_CONV + c0 + CONV_CHUNK, :].astype(BF16))
        u = cg * ui
        u_prev = jnp.where(first, 0.0, pltpu.roll(u, 1, 0))
        u_next = jnp.where(last, 0.0, pltpu.roll(u, TM_IN - 1, 0))
        cw = [cw_ref[:, k * D_CONV + c0:k * D_CONV + c0 + CONV_CHUNK] for k in range(3)]
        conv = u_prev * cw[0] + u * cw[1] + u_next * cw[2]
        zc_ref[:, c0:c0 + CONV_CHUNK] = (bg * conv).astype(BF16)


def _inproj(xp, xs, mod, norm1, w_in_t, conv_w, q_norm, kv_norm, w_uq_slot, rope_tabs):
    n_prompt = T_P // TM_IN
    const = lambda i: (0, 0)
    pmap, smap = _stream_maps(TM_IN)
    once = pl.Buffered(1)
    return pl.pallas_call(
        _inproj_kernel,
        grid=(T // TM_IN,),
        in_specs=[
            pl.BlockSpec((TM_IN, D_MODEL), pmap),
            pl.BlockSpec((TM_IN, D_MODEL), smap),
            pl.BlockSpec((N_COND, 6 * D_MODEL), const),
            pl.BlockSpec((1, D_MODEL), const),
            pl.BlockSpec((O_CQ, D_MODEL), const, pipeline_mode=once),
            pl.BlockSpec((SMALL_BLK, D_MODEL), lambda i: (O_CQ // SMALL_BLK, 0), pipeline_mode=once),
            pl.BlockSpec((1, 3 * D_CONV), const),
            pl.BlockSpec((1, Q_LORA), const),
            pl.BlockSpec((1, KV_LORA), const),
            pl.BlockSpec((N_HEADS * LANE, Q_LORA), const),
            pl.BlockSpec((None, 3, TM_IN, LANE),
                         lambda i: (jnp.where(i >= n_prompt, 1, 0), 0, 0, 0)),
        ],
        out_specs=[
            pl.BlockSpec((TM_IN, D_CONV), lambda i: (i, 0)),
            pl.BlockSpec((TM_IN, N_HEADS * LANE), lambda i: (i, 0)),
            pl.BlockSpec((TM_IN, KV_LORA), lambda i: (i, 0)),
            pl.BlockSpec((TM_IN, LANE), lambda i: (i, 0)),
            pl.BlockSpec((TM_IN, KV_LORA), pmap),
            pl.BlockSpec((TM_IN // SEQ, QK_ROPE, SEQ), lambda i: (jnp.minimum(i, n_prompt - 1), 0, 0)),
        ],
        out_shape=[
            jax.ShapeDtypeStruct((T, D_CONV), BF16),
            jax.ShapeDtypeStruct((T, N_HEADS * LANE), BF16),
            jax.ShapeDtypeStruct((T, KV_LORA), F32),
            jax.ShapeDtypeStruct((T, LANE), F32),
            jax.ShapeDtypeStruct((T_P, KV_LORA), F32),
            jax.ShapeDtypeStruct((BATCH, QK_ROPE, SEQ), F32),
        ],
        compiler_params=pltpu.CompilerParams(
            dimension_semantics=("arbitrary",), vmem_limit_bytes=VMEM_LIMIT),
        name="in_proj",
    )(xp, xs, mod, norm1, w_in_t, w_in_t, conv_w, q_norm, kv_norm, w_uq_slot, rope_tabs)


def _fill_kv(ckv, krs, wukv_ref, kf_scr, v_scr, off):
    m = ckv.shape[0]
    kv = _dot_nt(ckv.astype(BF16), wukv_ref[...])
    for hh in range(N_HEADS):
        kf_scr[hh, off:off + m, :] = (kv[:, LANE * hh:LANE * (hh + 1)] + krs).astype(BF16)
    v_scr[off:off + m, :] = kv[:, N_HEADS * LANE:].astype(BF16)


def _attend(q_ref, r0, rows, kf_scr, v_scr, o_ref):
    for pair in range(N_HEADS // 2):
        acc = None
        for hh in (2 * pair, 2 * pair + 1):
            qh = q_ref[r0:r0 + rows, LANE * hh:LANE * (hh + 1)]
            s = _dot_nt(qh, kf_scr[hh]) * ATTN_SCALE
            e = jnp.exp(s - jnp.max(s, axis=-1, keepdims=True))
            p = (e / jnp.sum(e, axis=-1, keepdims=True)).astype(BF16)
            part = _dot(p, v_scr[:, LANE * hh:LANE * (hh + 1)])
            acc = part if acc is None else acc + part
        o_ref[r0:r0 + rows, LANE * pair:LANE * (pair + 1)] = acc.astype(BF16)


def _attn_prompt_kernel(q_ref, ckv_ref, krs_ref, wukv_ref, o_ref):
    for s in range(PROMPT_SEQS):
        r0 = s * SEQ
        kv = _dot_nt(ckv_ref[r0:r0 + SEQ, :].astype(BF16), wukv_ref[...])
        krs = krs_ref[r0:r0 + SEQ, :]
        for pair in range(N_HEADS // 2):
            acc = None
            for hh in (2 * pair, 2 * pair + 1):
                kf = (kv[:, LANE * hh:LANE * (hh + 1)] + krs).astype(BF16)
                vh = kv[:, LANE * (N_HEADS + hh):LANE * (N_HEADS + hh + 1)].astype(BF16)
                sc = _dot_nt(q_ref[r0:r0 + SEQ, LANE * hh:LANE * (hh + 1)], kf) * ATTN_SCALE
                e = jnp.exp(sc - jnp.max(sc, axis=-1, keepdims=True))
                p = (e / jnp.sum(e, axis=-1, keepdims=True)).astype(BF16)
                part = _dot(p, vh)
                acc = part if acc is None else acc + part
            o_ref[r0:r0 + SEQ, LANE * pair:LANE * (pair + 1)] = acc.astype(BF16)


def _attn_sample_kernel(q_ref, ckv_ref, cckv_ref, krs_ref, ckrs_ref, wukv_ref, o_ref, kf_scr, v_scr):
    @pl.when(pl.program_id(1) == 0)
    def _():
        _fill_kv(ckv_ref[...], krs_ref[...], wukv_ref, kf_scr, v_scr, 0)
        _fill_kv(cckv_ref[...], ckrs_ref[...], wukv_ref, kf_scr, v_scr, DEC_SEQ)

    _attend(q_ref, 0, Q_BLK_S, kf_scr, v_scr, o_ref)


def _attention(q, ckv, krs, cache_ckv, cache_krs, w_ukv_slot):
    kv_cols = 2 * N_HEADS * LANE
    n_o = N_HEADS * V_HEAD
    steps = BATCH // PROMPT_SEQS
    rows = PROMPT_SEQS * SEQ
    o_prompt = pl.pallas_call(
        _attn_prompt_kernel,
        grid=(steps,),
        in_specs=[
            pl.BlockSpec((rows, N_HEADS * LANE), lambda b: (b, 0)),
            pl.BlockSpec((rows, KV_LORA), lambda b: (b, 0)),
            pl.BlockSpec((rows, LANE), lambda b: (b, 0)),
            pl.BlockSpec((kv_cols, KV_LORA), lambda b: (0, 0)),
        ],
        out_specs=pl.BlockSpec((rows, n_o), lambda b: (b, 0)),
        out_shape=jax.ShapeDtypeStruct((T_P, n_o), BF16),
        compiler_params=pltpu.CompilerParams(dimension_semantics=("parallel",),
                                             vmem_limit_bytes=VMEM_LIMIT),
        name="attn_prompt",
    )(q, ckv, krs, w_ukv_slot)

    m_all = DEC_SEQ + PAST_LEN
    nq = DEC_SEQ // Q_BLK_S
    q0 = T_P // Q_BLK_S
    s0 = T_P // DEC_SEQ
    o_sample = pl.pallas_call(
        _attn_sample_kernel,
        grid=(DEC_BATCH, nq),
        in_specs=[
            pl.BlockSpec((Q_BLK_S, N_HEADS * LANE), lambda b, j: (q0 + b * nq + j, 0)),
            pl.BlockSpec((DEC_SEQ, KV_LORA), lambda b, j: (s0 + b, 0)),
            pl.BlockSpec((None, PAST_LEN, KV_LORA), lambda b, j: (b, 0, 0)),
            pl.BlockSpec((DEC_SEQ, LANE), lambda b, j: (s0 + b, 0)),
            pl.BlockSpec((None, PAST_LEN, LANE), lambda b, j: (b, 0, 0)),
            pl.BlockSpec((kv_cols, KV_LORA), lambda b, j: (0, 0)),
        ],
        out_specs=pl.BlockSpec((Q_BLK_S, n_o), lambda b, j: (b * nq + j, 0)),
        out_shape=jax.ShapeDtypeStruct((T_S, n_o), BF16),
        scratch_shapes=[pltpu.VMEM((N_HEADS, m_all, LANE), BF16),
                        pltpu.VMEM((m_all, N_HEADS * LANE), BF16)],
        compiler_params=pltpu.CompilerParams(dimension_semantics=("parallel", "arbitrary"),
                                             vmem_limit_bytes=VMEM_LIMIT),
        name="attn_sample",
    )(q, ckv, cache_ckv, krs, cache_krs, w_ukv_slot)
    return o_prompt, o_sample


def _route(logits):
    lane = lax.broadcasted_iota(jnp.int32, logits.shape, 1)
    neg = -jnp.inf
    big = jnp.int32(1 << 20)
    gmask = (lane >= N_EXPERTS) & (lane < N_EXPERTS + N_GROUPS)
    gl = jnp.where(gmask, logits, neg)
    gmax = jnp.max(gl, axis=-1, keepdims=True)
    gsum = jnp.sum(jnp.where(gmask, jnp.exp(gl - gmax), 0.0), axis=-1, keepdims=True)
    p_g = 1.0 / gsum
    g_idx = jnp.min(jnp.where(gl == gmax, lane, big), axis=-1, keepdims=True) - N_EXPERTS

    emask = (lane < N_EXPERTS) & ((lane >> 3) == g_idx)
    el = jnp.where(emask, logits, neg)
    m1 = jnp.max(el, axis=-1, keepdims=True)
    i1 = jnp.min(jnp.where(el == m1, lane, big), axis=-1, keepdims=True)
    el2 = jnp.where(lane == i1, neg, el)
    m2 = jnp.max(el2, axis=-1, keepdims=True)
    i2 = jnp.min(jnp.where(el2 == m2, lane, big), axis=-1, keepdims=True)
    z = jnp.sum(jnp.where(emask, jnp.exp(el - m1), 0.0), axis=-1, keepdims=True)
    p1 = 1.0 / z
    p2 = jnp.exp(m2 - m1) / z
    tot = p1 + p2
    w1 = p_g * p1 / tot
    w2 = p_g * p2 / tot
    return jnp.where(lane == i1, w1, 0.0) + jnp.where(lane == i2, w2, 0.0), g_idx


def _post_kernel(xp_ref, xs_ref, mod_ref, n1_ref, wg_ref, zc_ref, op_ref, os_ref, wco_ref, wo_ref,
                 wmix_ref, n2_ref, wr_ref, x1_ref, h3_ref, meta_ref, cnt_ref, gr_ref):
    i = pl.program_id(0)
    is_sample = i >= T_P // TM_POST
    x = jnp.where(is_sample, xs_ref[...], xp_ref[...])
    o = jnp.where(is_sample, os_ref[...], op_ref[...])
    mod = mod_ref[pl.ds(_mod_row(i, TM_POST), 1), :]
    shift1 = mod[:, 0:D_MODEL]
    scale1 = mod[:, D_MODEL:2 * D_MODEL]
    gate1 = mod[:, 2 * D_MODEL:3 * D_MODEL]
    shift2 = mod[:, 3 * D_MODEL:4 * D_MODEL]
    scale2 = mod[:, 4 * D_MODEL:5 * D_MODEL]
    y_conv = _dot(zc_ref[...], wco_ref[...].astype(BF16))
    y_mla = _dot(o, wo_ref[...].astype(BF16))
    h = ((x * _rms(x)) * n1_ref[...]) * (1.0 + scale1) + shift1
    g = _dot_nt(h.astype(BF16), wg_ref[...].astype(BF16))
    merged = (jax.nn.sigmoid(g[:, 0:D_MODEL]) * y_conv
              + jax.nn.sigmoid(g[:, D_MODEL:2 * D_MODEL]) * y_mla)
    y = _dot(merged.astype(BF16), wmix_ref[...].astype(BF16))
    x1 = x + gate1 * y
    x1_ref[...] = x1
    h2 = ((x1 * _rms(x1)) * n2_ref[...]) * (1.0 + scale2) + shift2
    h2_hi = h2.astype(BF16)
    h2_lo = (h2 - h2_hi.astype(F32)).astype(BF16)
    hh = _dot(h2_hi, wr_ref[...])
    logits = hh[:, 0:LANE] + hh[:, LANE:2 * LANE] + _dot(h2_lo, wr_ref[:, 0:LANE])
    comb, g_idx = _route(logits)

    lane = lax.broadcasted_iota(jnp.int32, comb.shape, 1)
    onehot = lane == g_idx + N_EXPERTS
    r_i = lax.broadcasted_iota(jnp.int32, (TM_POST, TM_POST), 0)
    c_i = lax.broadcasted_iota(jnp.int32, (TM_POST, TM_POST), 1)
    lower = jnp.where(c_i < r_i, 1.0, 0.0).astype(BF16)
    before = _dot(lower, jnp.where(onehot, 1.0, 0.0).astype(BF16))
    rank = jnp.sum(jnp.where(onehot, before, 0.0), axis=-1, keepdims=True)
    counts = jnp.sum(jnp.where(onehot, 1.0, 0.0), axis=0, keepdims=True)
    cnt_ref[...] = jnp.broadcast_to(counts, cnt_ref.shape)

    meta_ref[...] = comb
    idx = jnp.where(lane == GID_LANE, g_idx.astype(F32), 0.0) + jnp.where(lane == RANK_LANE, rank, 0.0)
    idx_hi = idx.astype(BF16)
    idx_lo = (idx - idx_hi.astype(F32)).astype(BF16)
    s_row = lax.broadcasted_iota(jnp.int32, (8, LANE), 0)
    s_lane = lax.broadcasted_iota(jnp.int32, (8, LANE), 1)
    sel = jnp.where(s_lane == GID_LANE + s_row, 1.0, 0.0).astype(BF16)
    gr_ref[...] = _dot_nt(sel, idx_hi) + _dot_nt(sel, idx_lo)
    for c in range(N_SLAB):
        h3_ref[pl.ds(c, TM_POST, stride=N_SLAB), :] = h2[:, LANE * c:LANE * (c + 1)]


def _post(xp, xs, mod, norm1, w_in_t, zc, o_p, o_s, w_conv_out, w_o, w_mix_out, norm2, w_route):
    const = lambda i: (0, 0)
    row = lambda i: (i, 0)
    pmap, smap = _stream_maps(TM_POST)
    once = pl.Buffered(1)
    return pl.pallas_call(
        _post_kernel,
        grid=(T // TM_POST,),
        in_specs=[
            pl.BlockSpec((TM_POST, D_MODEL), pmap),
            pl.BlockSpec((TM_POST, D_MODEL), smap),
            pl.BlockSpec((N_COND, 6 * D_MODEL), const),
            pl.BlockSpec((1, D_MODEL), const),
            pl.BlockSpec((pl.Element(2 * D_MODEL), pl.Element(D_MODEL)), lambda i: (O_GATE, 0),
                         pipeline_mode=once),
            pl.BlockSpec((TM_POST, D_CONV), row),
            pl.BlockSpec((TM_POST, N_HEADS * V_HEAD), pmap),
            pl.BlockSpec((TM_POST, N_HEADS * V_HEAD), smap),
            pl.BlockSpec((D_CONV, D_MODEL), const, pipeline_mode=once),
            pl.BlockSpec((N_HEADS * V_HEAD, D_MODEL), const, pipeline_mode=once),
            pl.BlockSpec((D_MODEL, D_MODEL), const, pipeline_mode=once),
            pl.BlockSpec((1, D_MODEL), const),
            pl.BlockSpec((D_MODEL, 2 * LANE), const),
        ],
        out_specs=[
            pl.BlockSpec((TM_POST, D_MODEL), row),
            pl.BlockSpec((TM_POST * N_SLAB, LANE), row),
            pl.BlockSpec((TM_POST, LANE), row),
            pl.BlockSpec((None, 8, LANE), lambda i: (i, 0, 0)),
            pl.BlockSpec((8, TM_POST), lambda i: (0, i)),
        ],
        out_shape=[
            jax.ShapeDtypeStruct((T, D_MODEL), F32),
            jax.ShapeDtypeStruct((T * N_SLAB, LANE), F32),
            jax.ShapeDtypeStruct((T, LANE), F32),
            jax.ShapeDtypeStruct((T // TM_POST, 8, LANE), F32),
            jax.ShapeDtypeStruct((8, T), F32),
        ],
        compiler_params=pltpu.CompilerParams(
            dimension_semantics=("parallel",), vmem_limit_bytes=VMEM_LIMIT),
        name="post_mixer",
    )(xp, xs, mod, norm1, w_in_t, zc, o_p, o_s, w_conv_out, w_o, w_mix_out, norm2, w_route)


DISP_IN = T // TM_DISP
DISP_OUT = T // TM_MOE


def _dispatch_kernel(pos_ref, h3_ref, m_ref, hs_ref, ms_ref, xs_ref, mss_ref):
    i = pl.program_id(0)

    @pl.when(i < DISP_IN)
    def _():
        def body(r, carry):
            p = pos_ref[0, r]
            xs_ref[_slab(p), :] = h3_ref[_slab(r), :]
            mss_ref[pl.ds(p, 1), :] = m_ref[pl.ds(r, 1), :]
            return carry

        lax.fori_loop(0, TM_DISP, body, 0, unroll=8)

    @pl.when(i >= DISP_IN)
    def _():
        row0 = pl.multiple_of((i - DISP_IN) * TM_MOE, TM_MOE)
        for c in range(N_SLAB):
            hs_ref[:, LANE * c:LANE * (c + 1)] = (
                xs_ref[pl.ds(row0 * N_SLAB + c, TM_MOE, stride=N_SLAB), :].astype(BF16))
        ms_ref[...] = mss_ref[pl.ds(row0, TM_MOE), :]


def _dispatch(pos, h3, meta):
    n_slab = N_SLAB
    in_map = lambda i: (jnp.minimum(i, DISP_IN - 1), 0)
    out_map = lambda i: (jnp.maximum(i - DISP_IN, 0), 0)
    return pl.pallas_call(
        _dispatch_kernel,
        grid=(DISP_IN + DISP_OUT,),
        in_specs=[pl.BlockSpec((None, 1, TM_DISP), lambda i: (jnp.minimum(i, DISP_IN - 1), 0, 0),
                               memory_space=pltpu.SMEM),
                  pl.BlockSpec((TM_DISP * n_slab, LANE), in_map),
                  pl.BlockSpec((TM_DISP, LANE), in_map)],
        out_specs=[pl.BlockSpec((TM_MOE, D_MODEL), out_map),
                   pl.BlockSpec((TM_MOE, LANE), out_map)],
        scratch_shapes=[pltpu.VMEM((T * n_slab, LANE), F32),
                        pltpu.VMEM((T, LANE), F32)],
        out_shape=[jax.ShapeDtypeStruct((T, D_MODEL), BF16),
                   jax.ShapeDtypeStruct((T, LANE), F32)],
        compiler_params=pltpu.CompilerParams(
            dimension_semantics=("arbitrary",), vmem_limit_bytes=VMEM_LIMIT),
        name="moe_dispatch",
    )(pos.reshape(DISP_IN, 1, TM_DISP), h3, meta)


def _moe_kernel(sched_ref, hs_ref, ms_ref, wup_ref, wgate_ref, wdown_ref, y3_ref, acc_ref):
    v = pl.program_id(0)
    j = pl.program_id(1)
    valid = sched_ref[V_VALID, v] == 1
    lo = sched_ref[V_LO, v]
    hi = sched_ref[V_HI, v]
    e0 = sched_ref[V_GROUP, v] * EXP_PER_GROUP + j * MOE_EPS
    full = (hi - lo) * 4 >= TM_MOE * 3

    @pl.when(valid & (j == 0) & (sched_ref[V_FIRST, v] == 1))
    def _():
        acc_ref[...] = jnp.zeros_like(acc_ref)

    def expert_rows(r0, rows):
        w_in2 = jnp.concatenate(
            [w[k].astype(BF16) for k in range(MOE_EPS) for w in (wup_ref, wgate_ref)], axis=1)
        ag = _dot(hs_ref[pl.ds(r0, rows), :], w_in2)
        comb = ms_ref[pl.ds(r0, rows), :]
        lane = lax.broadcasted_iota(jnp.int32, comb.shape, 1)
        acts = []
        for k in range(MOE_EPS):
            a = ag[:, 2 * k * D_EXPERT:(2 * k + 1) * D_EXPERT]
            g = ag[:, (2 * k + 1) * D_EXPERT:(2 * k + 2) * D_EXPERT]
            cw = jnp.sum(jnp.where(lane == e0 + k, comb, 0.0), axis=-1, keepdims=True)
            acts.append(((g * jax.nn.sigmoid(g)) * a * cw).astype(BF16))
        w_out = jnp.concatenate([wdown_ref[k].astype(BF16) for k in range(MOE_EPS)], axis=0)
        acc_ref[pl.ds(r0, rows), :] += _dot(jnp.concatenate(acts, axis=1), w_out)

    @pl.when(valid & full)
    def _():
        expert_rows(0, TM_MOE)

    @pl.when(valid & jnp.logical_not(full))
    def _():
        def sub_block(s, carry):
            expert_rows(pl.multiple_of(s * MOE_SUB, MOE_SUB), MOE_SUB)
            return carry

        lax.fori_loop(lo // MOE_SUB, (hi + MOE_SUB - 1) // MOE_SUB, sub_block, 0)

    @pl.when(valid & (j == EXP_PER_GROUP // MOE_EPS - 1) & (sched_ref[V_LAST, v] == 1))
    def _():
        for c in range(N_SLAB):
            y3_ref[pl.ds(c, TM_MOE, stride=N_SLAB), :] = acc_ref[:, LANE * c:LANE * (c + 1)]


def _moe(sched, hs, ms, w_up, w_gate, w_down):
    steps = EXP_PER_GROUP // MOE_EPS
    wmap = lambda v, j, sched: (
        sched[V_GROUP, v] * steps + jnp.where(sched[V_VALID, v] == 1, j, steps - 1), 0, 0)
    tmap = lambda v, j, sched: (sched[V_TILE, v], 0)
    n_slab = D_MODEL // LANE
    return pl.pallas_call(
        _moe_kernel,
        grid_spec=pltpu.PrefetchScalarGridSpec(
            num_scalar_prefetch=1,
            grid=(N_VISITS, steps),
            in_specs=[
                pl.BlockSpec((TM_MOE, D_MODEL), tmap),
                pl.BlockSpec((TM_MOE, LANE), tmap),
                pl.BlockSpec((MOE_EPS, D_MODEL, D_EXPERT), wmap),
                pl.BlockSpec((MOE_EPS, D_MODEL, D_EXPERT), wmap),
                pl.BlockSpec((MOE_EPS, D_EXPERT, D_MODEL), wmap),
            ],
            out_specs=pl.BlockSpec((TM_MOE * n_slab, LANE), tmap),
            scratch_shapes=[pltpu.VMEM((TM_MOE, D_MODEL), F32)],
        ),
        out_shape=jax.ShapeDtypeStruct((T * n_slab, LANE), F32),
        compiler_params=pltpu.CompilerParams(
            dimension_semantics=("arbitrary", "arbitrary"), vmem_limit_bytes=VMEM_LIMIT),
        name="moe_grouped",
    )(sched, hs, ms, w_up, w_gate, w_down)


def _final_kernel(pos_ref, ys_ref, x1_ref, mod_ref, fn_ref, yp_ref, ysm_ref, g_ref):
    i = pl.program_id(0)

    def body(r, carry):
        g_ref[_slab(r), :] = ys_ref[_slab(pos_ref[0, r]), :]
        return carry

    lax.fori_loop(0, TM_FINAL, body, 0, unroll=8)
    mod = mod_ref[pl.ds(_mod_row(i, TM_FINAL), 1), :]
    gate2 = mod[:, 5 * D_MODEL:6 * D_MODEL]
    moe = jnp.concatenate([g_ref[pl.ds(c, TM_FINAL, stride=N_SLAB), :] for c in range(N_SLAB)], axis=1)
    x2 = x1_ref[...] + gate2 * moe
    y = (x2 * _rms(x2)) * fn_ref[...]
    is_sample = i >= T_P // TM_FINAL

    @pl.when(jnp.logical_not(is_sample))
    def _():
        yp_ref[...] = y

    @pl.when(is_sample)
    def _():
        ysm_ref[...] = y


def _final(pos, ys, x1, mod, final_norm):
    n_slab = D_MODEL // LANE
    pmap, smap = _stream_maps(TM_FINAL)
    return pl.pallas_call(
        _final_kernel,
        grid=(T // TM_FINAL,),
        in_specs=[
            pl.BlockSpec((None, 1, TM_FINAL), lambda i: (i, 0, 0), memory_space=pltpu.SMEM),
            pl.BlockSpec((T * n_slab, LANE), lambda i: (0, 0), pipeline_mode=pl.Buffered(1)),
            pl.BlockSpec((TM_FINAL, D_MODEL), lambda i: (i, 0)),
            pl.BlockSpec((N_COND, 6 * D_MODEL), lambda i: (0, 0)),
            pl.BlockSpec((1, D_MODEL), lambda i: (0, 0)),
        ],
        out_specs=[pl.BlockSpec((TM_FINAL, D_MODEL), pmap),
                   pl.BlockSpec((TM_FINAL, D_MODEL), smap)],
        scratch_shapes=[pltpu.VMEM((TM_FINAL * n_slab, LANE), F32)],
        out_shape=[jax.ShapeDtypeStruct((T_P, D_MODEL), F32),
                   jax.ShapeDtypeStruct((T_S, D_MODEL), F32)],
        compiler_params=pltpu.CompilerParams(
            dimension_semantics=("arbitrary",), vmem_limit_bytes=VMEM_LIMIT),
        name="moe_unsort_final",
    )(pos.reshape(T // TM_FINAL, 1, TM_FINAL), ys, x1, mod, final_norm)


def _plan_kernel(gr_ref, cnt_ref, pos_ref, sched_ref):
    n_post = T // TM_POST
    lane = lax.broadcasted_iota(jnp.int32, (1, LANE), 1)
    grp_lane = lambda v, g: v[:, N_EXPERTS + g:N_EXPERTS + g + 1]
    counts = [cnt_ref[k, 0:1, :] for k in range(n_post)]
    gtot_v = counts[0]
    for k in range(1, n_post):
        gtot_v = gtot_v + counts[k]
    gtot = [grp_lane(gtot_v, g) for g in range(N_GROUPS)]
    goff = [jnp.zeros((1, 1), F32)]
    for g in range(1, N_GROUPS):
        goff.append(goff[-1] + gtot[g - 1])

    before = [jnp.zeros((1, 1), F32) for _ in range(N_GROUPS)]
    rows_per_tile = TM_POST // LANE
    for k in range(n_post):
        gid = gr_ref[0:1, TM_POST * k:TM_POST * (k + 1)]
        pos = gr_ref[1:2, TM_POST * k:TM_POST * (k + 1)]
        for g in range(N_GROUPS):
            pos = pos + jnp.where(gid == float(g), goff[g] + before[g], 0.0)
            before[g] = before[g] + grp_lane(counts[k], g)
        for r in range(rows_per_tile):
            pos_ref[rows_per_tile * k + r:rows_per_tile * k + r + 1, :] = (
                pos[:, LANE * r:LANE * (r + 1)].astype(jnp.int32))

    zero = jnp.zeros((1, LANE), F32)
    rows = {name: zero for name in ("vt", "vg", "vlo", "vhi")}
    slot = jnp.zeros((1, 1), F32)
    last_t = jnp.zeros((1, 1), F32)
    last_g = jnp.zeros((1, 1), F32)
    for i in range(T // TM_MOE):
        for g in range(N_GROUPS):
            lo = jnp.clip(goff[g] - float(TM_MOE * i), 0.0, float(TM_MOE))
            hi = jnp.clip(goff[g] + gtot[g] - float(TM_MOE * i), 0.0, float(TM_MOE))
            ok = hi > lo
            here = ok & (lane == slot.astype(jnp.int32))
            rows["vt"] = jnp.where(here, float(i), rows["vt"])
            rows["vg"] = jnp.where(here, float(g), rows["vg"])
            rows["vlo"] = jnp.where(here, lo, rows["vlo"])
            rows["vhi"] = jnp.where(here, hi, rows["vhi"])
            last_t = jnp.where(ok, float(i), last_t)
            last_g = jnp.where(ok, float(g), last_g)
            slot = slot + jnp.where(ok, 1.0, 0.0)
    n_ok = slot.astype(jnp.int32)
    valid = lane < n_ok
    vt = jnp.where(valid, rows["vt"], last_t)
    vg = jnp.where(valid, rows["vg"], last_g)
    prev_t = pltpu.roll(jnp.broadcast_to(vt, (8, LANE)), 1, 1)[0:1, :]
    next_t = pltpu.roll(jnp.broadcast_to(vt, (8, LANE)), LANE - 1, 1)[0:1, :]
    first = jnp.where((lane == 0) | (vt != prev_t), 1.0, 0.0)
    last = jnp.where((lane == n_ok - 1) | (vt != next_t), 1.0, 0.0)
    table = [vt, vg, rows["vlo"], rows["vhi"], first, last, jnp.where(valid, 1.0, 0.0), zero]
    for r, row in enumerate(table):
        sched_ref[r:r + 1, :] = row.astype(jnp.int32)


V_TILE, V_GROUP, V_LO, V_HI, V_FIRST, V_LAST, V_VALID = range(7)


def _plan(gr, cnt):
    pos, sched = pl.pallas_call(
        _plan_kernel,
        out_shape=[jax.ShapeDtypeStruct((T // LANE, LANE), jnp.int32),
                   jax.ShapeDtypeStruct((8, LANE), jnp.int32)],
        name="moe_plan",
    )(gr, cnt)
    return pos.reshape(T), sched


def _rope_tables():
    n = np.arange(DEC_SEQ)
    pos = np.stack([n // GRID_W, n % GRID_W], axis=1).astype(np.float32)
    half = ROPE_AXIS // 2
    inv = (1.0 / (ROPE_BASE ** (np.arange(0, ROPE_AXIS, 2, dtype=np.float32) / ROPE_AXIS))).astype(np.float32)
    ang = (pos[:, :, None] * inv[None, None, :]).astype(np.float32)
    cos = np.cos(ang).astype(np.float32)
    sin = np.sin(ang).astype(np.float32)
    tabs = np.zeros((2, 3, DEC_SEQ, LANE), np.float32)
    tabs[:, 0] = 1.0
    for a in range(2):
        lo = ROPE_LANE0 + a * ROPE_AXIS
        tabs[1, 0, :, lo:lo + half] = cos[:, a]
        tabs[1, 0, :, lo + half:lo + 2 * half] = cos[:, a]
        tabs[1, 1, :, lo + half:lo + 2 * half] = sin[:, a]
        tabs[1, 2, :, lo:lo + half] = -sin[:, a]
    return jnp.asarray(tabs)


def kernel(x_prompt, x_sample, cache_ckv, cache_krope, c, c_ctx, norm1, w_ada, b_ada, w_in, conv_w,
           w_conv_out, q_norm, w_uq, kv_norm, w_ukv, w_o, w_mix_out, norm2, w_grp, w_exp, w_up,
           w_gate, w_down, final_norm):
    l = 0
    xp = x_prompt.reshape(T_P, D_MODEL)
    xs = x_sample.reshape(T_S, D_MODEL)
    mod = _ada(c_ctx[None, :], c, w_ada[l], b_ada[l][None, :])

    w_in_t = w_in[l].T
    w_uq_slot = jnp.pad(w_uq[l].T.reshape(N_HEADS, QK_NOPE + QK_ROPE, Q_LORA),
                        ((0, 0), (0, LANE - QK_NOPE - QK_ROPE), (0, 0))
                        ).reshape(N_HEADS * LANE, Q_LORA).astype(BF16)
    wkv_t = w_ukv[l].T.reshape(N_HEADS, QK_NOPE + V_HEAD, KV_LORA)
    wk_slot = jnp.pad(wkv_t[:, :QK_NOPE], ((0, 0), (0, LANE - QK_NOPE), (0, 0)))
    wv_t = wkv_t[:, QK_NOPE:].reshape(N_HEADS // 2, 2, V_HEAD, KV_LORA)
    zero = jnp.zeros_like(wv_t[:, 0])
    wv_slot = jnp.stack([jnp.concatenate([wv_t[:, 0], zero], axis=1),
                         jnp.concatenate([zero, wv_t[:, 1]], axis=1)], axis=1)
    w_ukv_slot = jnp.concatenate([wk_slot.reshape(N_HEADS * LANE, KV_LORA),
                                  wv_slot.reshape(N_HEADS * LANE, KV_LORA)], axis=0).astype(BF16)
    w_route = jnp.pad(jnp.concatenate([w_exp[l], w_grp[l]], axis=1),
                      ((0, 0), (0, LANE - N_EXPERTS - N_GROUPS)))
    w_route_hi = w_route.astype(BF16)
    w_route_lo = (w_route - w_route_hi.astype(F32)).astype(BF16)
    w_route2 = jnp.concatenate([w_route_hi, w_route_lo], axis=1)
    cache_krs = jnp.pad(cache_krope[:, l], ((0, 0), (0, 0), (ROPE_LANE0, LANE - ROPE_LANE0 - QK_ROPE)))

    zc, q, ckv, krs, nckv, nkr = _inproj(xp, xs, mod, norm1[l][None, :], w_in_t,
                                         conv_w[l].reshape(1, 3 * D_CONV),
                              q_norm[l][None, :], kv_norm[l][None, :], w_uq_slot, _rope_tables())
    o_p, o_s = _attention(q, ckv, krs, cache_ckv[:, l], cache_krs, w_ukv_slot)
    x1, h3, meta, cnt, gr = _post(xp, xs, mod, norm1[l][None, :], w_in_t, zc, o_p, o_s, w_conv_out[l],
                                  w_o[l], w_mix_out[l], norm2[l][None, :], w_route2)
    pos, sched = _plan(gr, cnt)
    hs, ms = _dispatch(pos, h3, meta)
    ys = _moe(sched, hs, ms, w_up[l], w_gate[l], w_down[l])
    yp, ysm = _final(pos, ys, x1, mod, final_norm[None, :])

    y_prompt = yp.reshape(BATCH, SEQ, D_MODEL)
    y_sample = ysm.reshape(DEC_BATCH, DEC_SEQ, D_MODEL)
    new_ckv = nckv.reshape(BATCH, 1, SEQ, KV_LORA)
    new_krope = jnp.swapaxes(nkr, 1, 2).reshape(BATCH, 1, SEQ, QK_ROPE)
    return (y_prompt, y_sample, new_ckv, new_krope)
```
